```python
import jax, jax.numpy as jnp
from jax import lax
import numpy as np

D_MODEL = 1024
BATCH = 16
SEQ = 256
DEPTH = 1
DEC_BATCH = 2
DEC_SEQ = 1024
PAST_LEN = 512

GRID_W = 64
HEAD_DIM = 64
GLA_HEADS = 8
GLA_DK = 64
GLA_DV = 64
GLA_WIDTH = GLA_HEADS * GLA_DV
GATE_RANK = 16
GATE_TEMP = 16.0
GLA_CHUNK = 64
SWA_HEADS = 8
SWA_KV_HEADS = 2
SWA_GROUP = SWA_HEADS // SWA_KV_HEADS
SWA_WIDTH = SWA_HEADS * HEAD_DIM
WINDOW = 128
BLOCK = 128
MIX_WIDTH = GLA_WIDTH + SWA_WIDTH
D_FF = 4 * D_MODEL
ROPE_THETA = 10000.0
ROPE_AXIS_DIM = HEAD_DIM // 2
EPS = 1e-6
NEG_INF = -1e30
SPLIT_SIZES = (GLA_HEADS * GLA_DK, GLA_HEADS * GLA_DK, GLA_WIDTH, GLA_WIDTH, 2 * GATE_RANK,
               SWA_WIDTH, SWA_KV_HEADS * HEAD_DIM, SWA_KV_HEADS * HEAD_DIM)
D_IN = 2 * GLA_HEADS * GLA_DK + 2 * GLA_WIDTH + 2 * GATE_RANK + SWA_WIDTH + 2 * SWA_KV_HEADS * HEAD_DIM

kernel_name = "hybrid_gla_swa_prefix_diffusion_step"


def rmsnorm(x, gain):
    xf = x.astype(jnp.float32)
    xf = xf * lax.rsqrt(jnp.mean(xf * xf, axis=-1, keepdims=True) + EPS)
    return xf.astype(x.dtype) * gain


def modulation(cond, w_ada, b_ada):
    m = jax.nn.silu(cond) @ w_ada + b_ada
    return jnp.split(m[:, None, :], 6, axis=-1)


def adaln(x, gain, shift, scale):
    return rmsnorm(x, gain) * (1.0 + scale) + shift


def axial_rope(T):
    rows = T // GRID_W
    t = jnp.arange(rows * GRID_W)
    row = (t // GRID_W).astype(jnp.float32)
    col = (t % GRID_W).astype(jnp.float32)
    half = ROPE_AXIS_DIM // 2
    freqs = ROPE_THETA ** (-jnp.arange(half, dtype=jnp.float32) / half)
    ang_r = row[:, None] * freqs[None, :]
    ang_c = col[:, None] * freqs[None, :]
    return (jnp.cos(ang_r), jnp.sin(ang_r), jnp.cos(ang_c), jnp.sin(ang_c))


def _rotate(x, cos, sin):
    x1, x2 = jnp.split(x, 2, axis=-1)
    return jnp.concatenate([x1 * cos - x2 * sin, x2 * cos + x1 * sin], axis=-1)


def apply_rope(x, rope):
    cos_r, sin_r, cos_c, sin_c = rope
    xr, xc = x[..., :ROPE_AXIS_DIM], x[..., ROPE_AXIS_DIM:]
    out = jnp.concatenate([_rotate(xr, cos_r, sin_r), _rotate(xc, cos_c, sin_c)], axis=-1)
    return out.astype(x.dtype)


def project(h, w_in, w_gk_f, b_gk_f, w_gk_b, b_gk_b, q_norm, k_norm):
    B, T, _ = h.shape
    idx, acc = [], 0
    for s in SPLIT_SIZES[:-1]:
        acc += s
        idx.append(acc)
    gq, gk, gv, gg, glr, sq, sk, sv = jnp.split(h @ w_in, idx, axis=-1)

    def heads(t, n):
        return t.reshape(B, T, n, -1).transpose(0, 2, 1, 3)

    gq = heads(gq, GLA_HEADS) * (GLA_DK ** -0.5)
    gk = heads(gk, GLA_HEADS)
    gv = heads(gv, GLA_HEADS)
    lr_f, lr_b = jnp.split(glr, 2, axis=-1)
    g_f = heads(jax.nn.log_sigmoid(lr_f @ w_gk_f + b_gk_f) / GATE_TEMP, GLA_HEADS)
    g_b = heads(jax.nn.log_sigmoid(lr_b @ w_gk_b + b_gk_b) / GATE_TEMP, GLA_HEADS)
    sq = rmsnorm(heads(sq, SWA_HEADS), q_norm).reshape(B, SWA_KV_HEADS, SWA_GROUP, T, HEAD_DIM)
    sk = rmsnorm(heads(sk, SWA_KV_HEADS), k_norm)
    sv = heads(sv, SWA_KV_HEADS)
    return (gq, gk, gv, g_f, g_b), (sq, sk, sv), gg


def gla_scan(q, k, v, g, s0):
    B, H, T, DK = q.shape
    DV = v.shape[-1]
    n = T // GLA_CHUNK

    def chunks(t):
        return jnp.moveaxis(t.reshape(B, H, n, GLA_CHUNK, t.shape[-1]), 2, 0)

    causal = jnp.tril(jnp.ones((GLA_CHUNK, GLA_CHUNK), dtype=bool))[:, :, None]

    def step(S, inp):
        qc, kc, vc, gc = inp
        b = jnp.cumsum(gc.astype(jnp.float32), axis=2)
        rel = jnp.where(causal, b[:, :, :, None, :] - b[:, :, None, :, :], -jnp.inf)
        decay = jnp.exp(rel)
        A = jnp.einsum('bhid,bhjd,bhijd->bhij', qc, kc, decay)
        o = (jnp.einsum('bhid,bhde->bhie', qc * jnp.exp(b), S)
             + jnp.einsum('bhij,bhje->bhie', A, vc))
        b_last = b[:, :, -1:, :]
        S = (jnp.exp(b_last[:, :, 0, :])[..., None] * S
             + jnp.einsum('bhjd,bhje->bhde', kc * jnp.exp(b_last - b), vc))
        return S, o

    S, o = lax.scan(step, s0.astype(jnp.float32), (chunks(q), chunks(k), chunks(v), chunks(g)))
    o = jnp.moveaxis(o, 0, 2).reshape(B, H, T, DV)
    return o.astype(v.dtype), S


def gla_bidirectional(q, k, v, g_f, g_b, s0_f, s0_b):
    o_f, s_f = gla_scan(q, k, v, g_f, s0_f)
    flip = lambda t: jnp.flip(t, axis=2)
    o_b, s_b = gla_scan(flip(q), flip(k), flip(v), flip(g_b), s0_b)
    return o_f + flip(o_b), s_f, s_b


def softmax_with_sink(s, sink):
    sk = jnp.broadcast_to(sink.astype(jnp.float32).reshape(1, SWA_KV_HEADS, SWA_GROUP, 1, 1),
                          s.shape[:-1] + (1,))
    p = jax.nn.softmax(jnp.concatenate([s, sk], axis=-1), axis=-1)
    return p[..., :-1]


def context_attention(q, k, v, sink):
    B, KVH, G, Lc, D = q.shape
    scale = HEAD_DIM ** -0.5

    def blk(i):
        qb = lax.dynamic_slice_in_dim(q, i * BLOCK, BLOCK, axis=3)
        s = jnp.einsum('bkgqd,bksd->bkgqs', qb, k).astype(jnp.float32) * scale
        p = softmax_with_sink(s, sink).astype(v.dtype)
        return jnp.einsum('bkgqs,bksd->bkgqd', p, v)

    o = lax.map(blk, jnp.arange(Lc // BLOCK))
    return jnp.moveaxis(o, 0, 3).reshape(B, KVH, G, Lc, D)


def latent_attention(q, k, v, kc, vc, sink):
    B, KVH, G, T, D = q.shape
    scale = HEAD_DIM ** -0.5
    span = BLOCK + 2 * WINDOW
    pad = ((0, 0), (0, 0), (WINDOW, WINDOW), (0, 0))
    kp, vp = jnp.pad(k, pad), jnp.pad(v, pad)

    def blk(i):
        qb = lax.dynamic_slice_in_dim(q, i * BLOCK, BLOCK, axis=3)
        kb = lax.dynamic_slice_in_dim(kp, i * BLOCK, span, axis=2)
        vb = lax.dynamic_slice_in_dim(vp, i * BLOCK, span, axis=2)
        qpos = i * BLOCK + jnp.arange(BLOCK)
        kpos = i * BLOCK - WINDOW + jnp.arange(span)
        valid = ((jnp.abs(qpos[:, None] - kpos[None, :]) <= WINDOW)
                 & (kpos[None, :] >= 0) & (kpos[None, :] < T))
        s_loc = jnp.einsum('bkgqd,bksd->bkgqs', qb, kb).astype(jnp.float32) * scale
        s_loc = jnp.where(valid, s_loc, NEG_INF)
        s_ctx = jnp.einsum('bkgqd,bksd->bkgqs', qb, kc).astype(jnp.float32) * scale
        p = softmax_with_sink(jnp.concatenate([s_loc, s_ctx], axis=-1), sink).astype(v.dtype)
        return (jnp.einsum('bkgqs,bksd->bkgqd', p[..., :span], vb)
                + jnp.einsum('bkgqs,bksd->bkgqd', p[..., span:], vc))

    o = lax.map(blk, jnp.arange(T // BLOCK))
    return jnp.moveaxis(o, 0, 3).reshape(B, KVH, G, T, D)


def merge(o_gla, gg, gla_norm, o_swa, w_o):
    B, H, T, DV = o_gla.shape
    og = rmsnorm(o_gla.transpose(0, 2, 1, 3), gla_norm).reshape(B, T, GLA_WIDTH) * jax.nn.silu(gg)
    osw = o_swa.transpose(0, 3, 1, 2, 4).reshape(B, T, SWA_WIDTH)
    return jnp.concatenate([og, osw], axis=-1) @ w_o


def sq_relu_mlp(h, w1, w2):
    return jnp.square(jax.nn.relu(h @ w1)) @ w2


def setup_inputs(seed: int = 0) -> dict:
    key = jax.random.key(seed)
    ks = jax.random.split(key, 26)
    nrm = lambda k, shape, s: jax.random.normal(k, shape, jnp.float32) * s
    gain = lambda k, shape: 1.0 + nrm(k, shape, 0.02)
    L = DEPTH
    return {
        "x_prompt": nrm(ks[0], (BATCH, SEQ, D_MODEL), 1.0),
        "x_sample": nrm(ks[1], (DEC_BATCH, DEC_SEQ, D_MODEL), 1.0),
        "c": nrm(ks[2], (DEC_BATCH, D_MODEL), 1.0),
        "cache_k": nrm(ks[3], (DEC_BATCH, L, SWA_KV_HEADS, PAST_LEN, HEAD_DIM), 1.0),
        "cache_v": nrm(ks[4], (DEC_BATCH, L, SWA_KV_HEADS, PAST_LEN, HEAD_DIM), 1.0),
        "state_gla_fwd": nrm(ks[5], (DEC_BATCH, L, GLA_HEADS, GLA_DK, GLA_DV), 0.3),
        "state_gla_bwd": nrm(ks[6], (DEC_BATCH, L, GLA_HEADS, GLA_DK, GLA_DV), 0.3),
        "c_ctx": nrm(ks[7], (D_MODEL,), 1.0),
        "w_ada": nrm(ks[8], (L, D_MODEL, 6 * D_MODEL), 0.5 * D_MODEL ** -0.5),
        "b_ada": nrm(ks[9], (L, 6 * D_MODEL), 0.02),
        "norm_mix": gain(ks[10], (L, D_MODEL)),
        "norm_ff": gain(ks[11], (L, D_MODEL)),
        "w_in": nrm(ks[12], (L, D_MODEL, D_IN), D_MODEL ** -0.5),
        "w_gk_fwd": nrm(ks[13], (L, GATE_RANK, GLA_HEADS * GLA_DK), GATE_RANK ** -0.5),
        "b_gk_fwd": nrm(ks[14], (L, GLA_HEADS * GLA_DK), 0.1),
        "w_gk_bwd": nrm(ks[15], (L, GATE_RANK, GLA_HEADS * GLA_DK), GATE_RANK ** -0.5),
        "b_gk_bwd": nrm(ks[16], (L, GLA_HEADS * GLA_DK), 0.1),
        "gla_norm": gain(ks[17], (L, GLA_DV)),
        "q_norm": gain(ks[18], (L, HEAD_DIM)),
        "k_norm": gain(ks[19], (L, HEAD_DIM)),
        "sink": nrm(ks[20], (L, SWA_HEADS), 0.5),
        "w_o": nrm(ks[21], (L, MIX_WIDTH, D_MODEL), MIX_WIDTH ** -0.5),
        "w_ff1": nrm(ks[22], (L, D_MODEL, D_FF), D_MODEL ** -0.5),
        "w_ff2": nrm(ks[23], (L, D_FF, D_MODEL), D_FF ** -0.5),
    }


def reference(x_prompt, x_sample, c, cache_k, cache_v, state_gla_fwd, state_gla_bwd, c_ctx,
              w_ada, b_ada, norm_mix, norm_ff, w_in, w_gk_fwd, b_gk_fwd, w_gk_bwd, b_gk_bwd,
              gla_norm, q_norm, k_norm, sink, w_o, w_ff1, w_ff2):
    rope = axial_rope(x_sample.shape[1])
    xp, xs = x_prompt, x_sample
    new_k, new_v, new_sf, new_sb = [], [], [], []
    for l in range(DEPTH):
        proj_w = (w_in[l], w_gk_fwd[l], b_gk_fwd[l], w_gk_bwd[l], b_gk_bwd[l], q_norm[l], k_norm[l])

        sh1, sc1, g1, sh2, sc2, g2 = modulation(c_ctx[None, :], w_ada[l], b_ada[l])
        h = adaln(xp, norm_mix[l], sh1, sc1)
        (gq, gk, gv, gf, gb), (sq, sk, sv), gg = project(h, *proj_w)
        zeros = jnp.zeros((xp.shape[0], GLA_HEADS, GLA_DK, GLA_DV), jnp.float32)
        o_gla, s_f, s_b = gla_bidirectional(gq, gk, gv, gf, gb, zeros, zeros)
        o_swa = context_attention(sq, sk, sv, sink[l])
        xp = xp + g1 * merge(o_gla, gg, gla_norm[l], o_swa, w_o[l])
        h = adaln(xp, norm_ff[l], sh2, sc2)
        xp = xp + g2 * sq_relu_mlp(h, w_ff1[l], w_ff2[l])
        new_k.append(sk)
        new_v.append(sv)
        new_sf.append(s_f.astype(xp.dtype))
        new_sb.append(s_b.astype(xp.dtype))

        sh1, sc1, g1, sh2, sc2, g2 = modulation(c, w_ada[l], b_ada[l])
        h = adaln(xs, norm_mix[l], sh1, sc1)
        (gq, gk, gv, gf, gb), (sq, sk, sv), gg = project(h, *proj_w)
        sq = apply_rope(sq, rope)
        sk = apply_rope(sk, rope)
        o_gla, _, _ = gla_bidirectional(gq, gk, gv, gf, gb, state_gla_fwd[:, l], state_gla_bwd[:, l])
        o_swa = latent_attention(sq, sk, sv, cache_k[:, l], cache_v[:, l], sink[l])
        xs = xs + g1 * merge(o_gla, gg, gla_norm[l], o_swa, w_o[l])
        h = adaln(xs, norm_ff[l], sh2, sc2)
        xs = xs + g2 * sq_relu_mlp(h, w_ff1[l], w_ff2[l])

    return (xp, xs, jnp.stack(new_k, axis=1), jnp.stack(new_v, axis=1),
            jnp.stack(new_sf, axis=1), jnp.stack(new_sb, axis=1))
```

```python
import functools

import jax
import jax.numpy as jnp
from jax import lax
from jax.experimental import pallas as pl
from jax.experimental.pallas import tpu as pltpu

F32 = jnp.float32
BF16 = jnp.bfloat16

D_MODEL = 1024
GRID_W = 64
HEAD_DIM = 64
GLA_HEADS = 8
GLA_WIDTH = GLA_HEADS * HEAD_DIM
GATE_RANK = 16
GATE_TEMP = 16.0
CHUNK = 64
SWA_HEADS = 8
SWA_KV_HEADS = 2
SWA_GROUP = SWA_HEADS // SWA_KV_HEADS
SWA_WIDTH = SWA_HEADS * HEAD_DIM
KV_WIDTH = SWA_KV_HEADS * HEAD_DIM
WINDOW = 128
QBLOCK = 128
D_FF = 4 * D_MODEL
ROPE_THETA = 10000.0
EPS = 1e-6
NEG_INF = -1e30
LANES = 128

C_GQ, C_GK, C_GV, C_GG = 0, 512, 1024, 1536
C_SQ, C_SK, C_SV, C_LR = 2048, 2560, 2688, 2816
W_IN_COLS = 2944

VMEM_LIMIT = 56 * 1024 * 1024


def _cparams(n_axes):
    return pltpu.CompilerParams(dimension_semantics=("arbitrary",) * n_axes,
                                vmem_limit_bytes=VMEM_LIMIT)


def _mod_kernel(condt_ref, w_ref, b_ref, o_ref, *, n_rows):
    w = w_ref[...]
    rows = []
    for r in range(n_rows):
        cnd = condt_ref[:, r:r + 1]
        s = cnd * jax.nn.sigmoid(cnd)
        rows.append(jnp.sum(w * s, axis=0, keepdims=True) + b_ref[...])
    rows.append(jnp.zeros((8 - n_rows, w.shape[1]), F32))
    o_ref[...] = jnp.concatenate(rows, axis=0)


def _modulation(cond, w_ada, b_ada):
    n_rows = cond.shape[0]
    condt = jnp.zeros((D_MODEL, 8), F32).at[:, :n_rows].set(cond.T)
    tn = 512
    n_out = w_ada.shape[1]
    return pl.pallas_call(
        functools.partial(_mod_kernel, n_rows=n_rows),
        grid=(n_out // tn,),
        in_specs=[pl.BlockSpec((D_MODEL, 8), lambda j: (0, 0)),
                  pl.BlockSpec((D_MODEL, tn), lambda j: (0, j)),
                  pl.BlockSpec((1, tn), lambda j: (0, j))],
        out_specs=pl.BlockSpec((8, tn), lambda j: (0, j)),
        out_shape=jax.ShapeDtypeStruct((8, n_out), F32),
        compiler_params=_cparams(1),
        name="modulation",
    )(condt, w_ada, b_ada.reshape(1, n_out))


def _adaln(x, gain, shift, scale):
    ms = jnp.mean(x * x, axis=-1, keepdims=True)
    xn = x * lax.rsqrt(ms + EPS)
    return (xn * gain) * (1.0 + scale) + shift


def _head_mean_sq(y):
    cols = y.shape[1]
    lane = lax.broadcasted_iota(jnp.int32, (1, LANES), 1)
    first = lane < HEAD_DIM
    outs = []
    for p in range(cols // LANES):
        s = y[:, p * LANES:(p + 1) * LANES]
        s = s * s
        tot = jnp.sum(s, axis=-1, keepdims=True)
        lo = jnp.sum(jnp.where(first, s, 0.0), axis=-1, keepdims=True)
        outs.append(jnp.where(first, lo, tot - lo) * (1.0 / HEAD_DIM))
    return outs[0] if len(outs) == 1 else jnp.concatenate(outs, axis=-1)


def _log_sigmoid(x):
    return jnp.minimum(x, 0.0) - jnp.log1p(jnp.exp(-jnp.abs(x)))


def _dot(a, b):
    return jnp.dot(a, b, preferred_element_type=F32)


def _dot_nt(a, b):
    return lax.dot_general(a, b, (((1,), (1,)), ((), ())), preferred_element_type=F32)


def _dot_tn(a, b):
    return lax.dot_general(a, b, (((0,), (0,)), ((), ())), preferred_element_type=F32)


def _proj_kernel(*refs, rope, emit_cache, seqs_per_tile, seq_len):
    (x_ref, mod_ref, gain_ref, w_ref, wg_ref, bg_ref, qkg_ref) = refs[:7]
    pos = 7
    if rope:
        cos_ref, sin_a_ref, sin_b_ref = refs[pos:pos + 3]
        pos += 3
    (gq_ref, gk_ref, gv_ref, sgg_ref, gf_ref, gb_ref, sq_ref, sk_ref, sv_ref) = refs[pos:pos + 9]
    pos += 9
    if emit_cache:
        nk_ref, nv_ref = refs[pos:pos + 2]

    h = _adaln(x_ref[...], gain_ref[...], mod_ref[0, 0:1, :], mod_ref[0, 1:2, :]).astype(BF16)

    gq_ref[...] = (_dot(h, w_ref[:, C_GQ:C_GK]) * (HEAD_DIM ** -0.5)).astype(BF16)
    gk_ref[...] = _dot(h, w_ref[:, C_GK:C_GV]).astype(BF16)
    gv_ref[...] = _dot(h, w_ref[:, C_GV:C_GG]).astype(BF16)
    gg = _dot(h, w_ref[:, C_GG:C_SQ])
    sgg_ref[...] = (gg * jax.nn.sigmoid(gg)).astype(BF16)

    lr = _dot(h, w_ref[:, C_LR:W_IN_COLS]).astype(BF16)
    gate = _log_sigmoid(_dot(lr, wg_ref[...]) + bg_ref[...]) * (1.0 / GATE_TEMP)
    gf_ref[...] = gate[:, :GLA_WIDTH]
    gb_ref[...] = gate[:, GLA_WIDTH:]

    qk = _dot(h, w_ref[:, C_SQ:C_SV])
    qk = qk * lax.rsqrt(_head_mean_sq(qk) + EPS) * qkg_ref[...]
    if rope:
        reps = (SWA_WIDTH + KV_WIDTH) // LANES
        cos = jnp.concatenate([cos_ref[...]] * reps, axis=-1)
        sin_a = jnp.concatenate([sin_a_ref[...]] * reps, axis=-1)
        sin_b = jnp.concatenate([sin_b_ref[...]] * reps, axis=-1)
        width = qk.shape[1]
        quarter = HEAD_DIM // 4
        qk = (qk * cos + pltpu.roll(qk, quarter, 1) * sin_a
              + pltpu.roll(qk, width - quarter, 1) * sin_b)
    sq_ref[...] = qk[:, :SWA_WIDTH].astype(BF16)
    sk = qk[:, SWA_WIDTH:]
    sk_ref[...] = sk.astype(BF16)
    sv = _dot(h, w_ref[:, C_SV:C_LR])
    sv_ref[...] = sv.astype(BF16)
    if emit_cache:
        for s in range(seqs_per_tile):
            rows = slice(s * seq_len, (s + 1) * seq_len)
            for g in range(SWA_KV_HEADS):
                lanes = slice(g * HEAD_DIM, (g + 1) * HEAD_DIM)
                nk_ref[s, 0, g, :, :] = sk[rows, lanes]
                nv_ref[s, 0, g, :, :] = sv[rows, lanes]


def _project(x, mod3, mod_row_of_tile, gain, w_perm, wg, bg, qkg, rope_tabs, emit_cache, seq_len, tm):
    n = x.shape[0]
    n_tiles = n // tm
    rope = rope_tabs is not None
    seqs_per_tile = max(tm // seq_len, 1)
    const = lambda shape: pl.BlockSpec(shape, lambda i: (0,) * len(shape))
    in_specs = [pl.BlockSpec((tm, D_MODEL), lambda i: (i, 0)),
                pl.BlockSpec((1, 6, D_MODEL), lambda i: (mod_row_of_tile(i), 0, 0)),
                const((1, D_MODEL)), const((D_MODEL, W_IN_COLS)), const((LANES, 2 * GLA_WIDTH)),
                const((1, 2 * GLA_WIDTH)), const((1, SWA_WIDTH + KV_WIDTH))]
    args = [x, mod3, gain, w_perm, wg, bg, qkg]
    if rope:
        tiles_per_seq = seq_len // tm
        in_specs += [pl.BlockSpec((tm, LANES), lambda i: (i % tiles_per_seq, 0))] * 3
        args += list(rope_tabs)
    row = lambda w: pl.BlockSpec((tm, w), lambda i: (i, 0))
    out_specs = [row(GLA_WIDTH)] * 6 + [row(SWA_WIDTH), row(KV_WIDTH), row(KV_WIDTH)]
    out_shape = ([jax.ShapeDtypeStruct((n, GLA_WIDTH), BF16)] * 4
                 + [jax.ShapeDtypeStruct((n, GLA_WIDTH), F32)] * 2
                 + [jax.ShapeDtypeStruct((n, SWA_WIDTH), BF16),
                    jax.ShapeDtypeStruct((n, KV_WIDTH), BF16), jax.ShapeDtypeStruct((n, KV_WIDTH), BF16)])
    if emit_cache:
        n_seq = n // seq_len
        cache_shape = (n_seq, 1, SWA_KV_HEADS, seq_len, HEAD_DIM)
        cache_spec = pl.BlockSpec((seqs_per_tile, 1, SWA_KV_HEADS, seq_len, HEAD_DIM), lambda i: (i, 0, 0, 0, 0))
        out_specs += [cache_spec] * 2
        out_shape += [jax.ShapeDtypeStruct(cache_shape, F32)] * 2
    return pl.pallas_call(
        functools.partial(_proj_kernel, rope=rope, emit_cache=emit_cache,
                          seqs_per_tile=seqs_per_tile, seq_len=seq_len),
        grid=(n_tiles,), in_specs=in_specs, out_specs=out_specs, out_shape=out_shape,
        compiler_params=_cparams(1), name="adaln_in_proj",
    )(*args)


def _gla_kernel(*refs, seq_len, has_state, emit_state):
    (q_ref, k_ref, v_ref, gf_ref, gb_ref, sgg_ref, norm_ref) = refs[:7]
    pos = 7
    if has_state:
        s0f_ref, s0b_ref = refs[pos:pos + 2]
        pos += 2
    og_ref = refs[pos]
    pos += 1
    if emit_state:
        sf_ref, sb_ref = refs[pos:pos + 2]
        pos += 2
    acc_ref = refs[pos]

    n_chunks = seq_len // CHUNK
    ri = lax.broadcasted_iota(jnp.int32, (CHUNK, CHUNK), 0)
    ci = lax.broadcasted_iota(jnp.int32, (CHUNK, CHUNK), 1)
    heads = [slice(0, HEAD_DIM), slice(HEAD_DIM, 2 * HEAD_DIM)]

    def run(g_ref, s0_ref, backward):
        keep = (ci >= ri) if backward else (ci <= ri)
        tri = jnp.where(keep, 1.0, 0.0).astype(BF16)
        edge = 0 if backward else CHUNK - 1
        mid = CHUNK // 2

        def body(t, states):
            c = (n_chunks - 1 - t) if backward else t
            rows = pl.ds(pl.multiple_of(c * CHUNK, CHUNK), CHUNK)
            g = g_ref[rows, :]
            g_hi = g.astype(BF16)
            g_lo = (g - g_hi.astype(F32)).astype(BF16)
            b = _dot(tri, g_hi) + _dot(tri, g_lo)
            b_edge = b[edge:edge + 1, :]
            b_mid = b[mid:mid + 1, :]
            q = q_ref[rows, :].astype(F32)
            k = k_ref[rows, :].astype(F32)
            v = v_ref[rows, :]
            q_in = (q * jnp.exp(b - b_mid)).astype(BF16)
            k_in = (k * jnp.exp(b_mid - b)).astype(BF16)
            q_st = (q * jnp.exp(b)).astype(BF16)
            k_st = (k * jnp.exp(b_edge - b)).astype(BF16)
            decay = jnp.exp(b_edge)
            outs, new_states = [], []
            for hd, st in zip(heads, states):
                a = _dot_nt(q_in[:, hd], k_in[:, hd])
                a = jnp.where(keep, a, 0.0).astype(BF16)
                o = _dot_nt(q_st[:, hd], st.astype(BF16)) + _dot(a, v[:, hd])
                outs.append(o)
                new_states.append(st * decay[:, hd] + _dot_tn(v[:, hd], k_st[:, hd]))
            o2 = jnp.concatenate(outs, axis=-1)
            if backward:
                acc_ref[rows, :] = acc_ref[rows, :] + o2
            else:
                acc_ref[rows, :] = o2
            return tuple(new_states)

        if s0_ref is None:
            init = (jnp.zeros((HEAD_DIM, HEAD_DIM), F32),) * 2
        else:
            init = tuple(s0_ref[0, 0, i, :, :].T for i in range(2))
        return lax.fori_loop(0, n_chunks, body, init)

    fin_f = run(gf_ref, s0f_ref if has_state else None, False)
    fin_b = run(gb_ref, s0b_ref if has_state else None, True)
    if emit_state:
        for i in range(2):
            sf_ref[0, 0, i, :, :] = fin_f[i].T
            sb_ref[0, 0, i, :, :] = fin_b[i].T

    o = acc_ref[...]
    o = o * lax.rsqrt(_head_mean_sq(o) + EPS) * norm_ref[...]
    og_ref[...] = (o * sgg_ref[...].astype(F32)).astype(BF16)


def _gla(gq, gk, gv, gf, gb, sgg, norm2, s0f, s0b, n_seq, seq_len, emit_state):
    n = gq.shape[0]
    has_state = s0f is not None
    pairs = GLA_WIDTH // LANES
    blk = pl.BlockSpec((seq_len, LANES), lambda b, p: (b, p))
    st_spec = pl.BlockSpec((1, 1, 2, HEAD_DIM, HEAD_DIM), lambda b, p: (b, 0, p, 0, 0))
    in_specs = [blk] * 6 + [pl.BlockSpec((1, LANES), lambda b, p: (0, 0))]
    args = [gq, gk, gv, gf, gb, sgg, norm2]
    if has_state:
        in_specs += [st_spec] * 2
        args += [s0f, s0b]
    out_specs = [blk]
    out_shape = [jax.ShapeDtypeStruct((n, GLA_WIDTH), BF16)]
    if emit_state:
        out_specs += [st_spec] * 2
        out_shape += [jax.ShapeDtypeStruct((n_seq, 1, GLA_HEADS, HEAD_DIM, HEAD_DIM), F32)] * 2
    return pl.pallas_call(
        functools.partial(_gla_kernel, seq_len=seq_len, has_state=has_state, emit_state=emit_state),
        grid=(n_seq, pairs), in_specs=in_specs, out_specs=out_specs, out_shape=out_shape,
        scratch_shapes=[pltpu.VMEM((seq_len, LANES), F32)],
        compiler_params=_cparams(2), name="gla_bidirectional",
    )(*args)


def _softmax_parts(scores, sink):
    m = sink
    for s in scores:
        m = jnp.maximum(m, jnp.max(s, axis=-1, keepdims=True))
    es = [jnp.exp(s - m) for s in scores]
    denom = jnp.exp(sink - m)
    for e in es:
        denom = denom + jnp.sum(e, axis=-1, keepdims=True)
    return es, denom


def _ctx_attn_kernel(sink_ref, q_ref, k_ref, v_ref, o_ref):
    scale = HEAD_DIM ** -0.5
    q = q_ref[...]
    k = k_ref[...]
    v = v_ref[...]
    outs = []
    for h in range(SWA_HEADS):
        g = h // SWA_GROUP
        kv = slice(g * HEAD_DIM, (g + 1) * HEAD_DIM)
        s = _dot_nt(q[:, h * HEAD_DIM:(h + 1) * HEAD_DIM], k[:, kv]) * scale
        (e,), denom = _softmax_parts([s], sink_ref[h])
        outs.append(_dot(e.astype(BF16), v[:, kv]) / denom)
    o_ref[...] = jnp.concatenate(outs, axis=-1).astype(BF16)


def _context_attention(sink, sq, sk, sv, n_seq, seq_len):
    n = sq.shape[0]
    return pl.pallas_call(
        _ctx_attn_kernel,
        grid=(n_seq,),
        in_specs=[pl.BlockSpec(memory_space=pltpu.SMEM),
                  pl.BlockSpec((seq_len, SWA_WIDTH), lambda b: (b, 0)),
                  pl.BlockSpec((seq_len, KV_WIDTH), lambda b: (b, 0)),
                  pl.BlockSpec((seq_len, KV_WIDTH), lambda b: (b, 0))],
        out_specs=pl.BlockSpec((seq_len, SWA_WIDTH), lambda b: (b, 0)),
        out_shape=jax.ShapeDtypeStruct((n, SWA_WIDTH), BF16),
        compiler_params=_cparams(1), name="context_attention",
    )(sink, sq, sk, sv)


def _lat_attn_kernel(sink_ref, q_ref, k_ref, v_ref, kc_ref, vc_ref, o_ref, *, seq_len):
    scale = HEAD_DIM ** -0.5
    span = QBLOCK + 2 * WINDOW
    i = pl.program_id(1)
    start = jnp.clip(i * QBLOCK - WINDOW, 0, seq_len - span)
    start = pl.multiple_of(start, QBLOCK)
    q = q_ref[...]
    k = k_ref[pl.ds(start, span), :]
    v = v_ref[pl.ds(start, span), :]
    qpos = i * QBLOCK + lax.broadcasted_iota(jnp.int32, (QBLOCK, span), 0)
    kpos = start + lax.broadcasted_iota(jnp.int32, (QBLOCK, span), 1)
    valid = jnp.abs(qpos - kpos) <= WINDOW
    outs = []
    for h in range(SWA_HEADS):
        g = h // SWA_GROUP
        kv = slice(g * HEAD_DIM, (g + 1) * HEAD_DIM)
        qh = q[:, h * HEAD_DIM:(h + 1) * HEAD_DIM]
        s_loc = jnp.where(valid, _dot_nt(qh, k[:, kv]) * scale, NEG_INF)
        kc = kc_ref[0, 0, g, :, :].astype(BF16)
        vc = vc_ref[0, 0, g, :, :].astype(BF16)
        s_ctx = _dot_nt(qh, kc) * scale
        (e_loc, e_ctx), denom = _softmax_parts([s_loc, s_ctx], sink_ref[h])
        o = _dot(e_loc.astype(BF16), v[:, kv]) + _dot(e_ctx.astype(BF16), vc)
        outs.append(o / denom)
    o_ref[...] = jnp.concatenate(outs, axis=-1).astype(BF16)


def _latent_attention(sink, sq, sk, sv, cache_k, cache_v, n_seq, seq_len):
    n = sq.shape[0]
    nq = seq_len // QBLOCK
    past = cache_k.shape[3]
    cache_spec = pl.BlockSpec((1, 1, SWA_KV_HEADS, past, HEAD_DIM), lambda b, i: (b, 0, 0, 0, 0))
    return pl.pallas_call(
        functools.partial(_lat_attn_kernel, seq_len=seq_len),
        grid=(n_seq, nq),
        in_specs=[pl.BlockSpec(memory_space=pltpu.SMEM),
                  pl.BlockSpec((QBLOCK, SWA_WIDTH), lambda b, i: (b * nq + i, 0)),
                  pl.BlockSpec((seq_len, KV_WIDTH), lambda b, i: (b, 0)),
                  pl.BlockSpec((seq_len, KV_WIDTH), lambda b, i: (b, 0)),
                  cache_spec, cache_spec],
        out_specs=pl.BlockSpec((QBLOCK, SWA_WIDTH), lambda b, i: (b * nq + i, 0)),
        out_shape=jax.ShapeDtypeStruct((n, SWA_WIDTH), BF16),
        compiler_params=_cparams(2), name="latent_attention",
    )(sink, sq, sk, sv, cache_k, cache_v)


def _out_kernel(x_ref, og_ref, osw_ref, mod_ref, gain_ref, wo_ref, w1_ref, w2_ref, y_ref):
    mix = _dot(og_ref[...], wo_ref[:GLA_WIDTH, :]) + _dot(osw_ref[...], wo_ref[GLA_WIDTH:, :])
    x1 = x_ref[...] + mod_ref[0, 2:3, :] * mix
    h = _adaln(x1, gain_ref[...], mod_ref[0, 3:4, :], mod_ref[0, 4:5, :]).astype(BF16)
    ff_tile = 1024
    acc = None
    for j in range(D_FF // ff_tile):
        cols = slice(j * ff_tile, (j + 1) * ff_tile)
        a = jnp.maximum(_dot(h, w1_ref[:, cols]), 0.0)
        part = _dot((a * a).astype(BF16), w2_ref[cols, :])
        acc = part if acc is None else acc + part
    y_ref[...] = x1 + mod_ref[0, 5:6, :] * acc


def _mix_out(x, og, osw, mod3, mod_row_of_tile, gain, wo, w1, w2, tm):
    n = x.shape[0]
    const = lambda shape: pl.BlockSpec(shape, lambda i: (0,) * len(shape), pipeline_mode=pl.Buffered(1))
    row = lambda w: pl.BlockSpec((tm, w), lambda i: (i, 0))
    return pl.pallas_call(
        _out_kernel,
        grid=(n // tm,),
        in_specs=[row(D_MODEL), row(GLA_WIDTH), row(SWA_WIDTH),
                  pl.BlockSpec((1, 6, D_MODEL), lambda i: (mod_row_of_tile(i), 0, 0)),
                  const((1, D_MODEL)), const((GLA_WIDTH + SWA_WIDTH, D_MODEL)),
                  const((D_MODEL, D_FF)), const((D_FF, D_MODEL))],
        out_specs=row(D_MODEL),
        out_shape=jax.ShapeDtypeStruct((n, D_MODEL), F32),
        compiler_params=_cparams(1), name="out_proj_mlp",
    )(x, og, osw, mod3, gain, wo, w1, w2)


def _rope_tables(seq_len):
    axis_dim = HEAD_DIM // 2
    half = axis_dim // 2
    t = jnp.arange(seq_len)
    rowp = (t // GRID_W).astype(F32)
    colp = (t % GRID_W).astype(F32)
    freqs = ROPE_THETA ** (-jnp.arange(half, dtype=F32) / half)
    lane = jnp.arange(LANES)
    d = lane % HEAD_DIM
    posv = jnp.where((d // axis_dim)[None, :] == 0, rowp[:, None], colp[:, None])
    ang = posv * freqs[d % half][None, :]
    second = ((d % axis_dim) >= half)[None, :]
    cos, sin = jnp.cos(ang), jnp.sin(ang)
    return cos, jnp.where(second, sin, 0.0), jnp.where(second, 0.0, -sin)


def kernel(x_prompt, x_sample, c, cache_k, cache_v, state_gla_fwd, state_gla_bwd, c_ctx, w_ada, b_ada, norm_mix, norm_ff, w_in, w_gk_fwd, b_gk_fwd, w_gk_bwd, b_gk_bwd, gla_norm, q_norm, k_norm, sink, w_o, w_ff1, w_ff2):
    depth = w_in.shape[0]
    assert depth == 1, "single trunk layer"
    bp, tp, _ = x_prompt.shape
    bs, ts, _ = x_sample.shape
    l = 0

    cond = jnp.concatenate([c_ctx[None, :], c], axis=0)
    mod3 = _modulation(cond, w_ada[l], b_ada[l]).reshape(8, 6, D_MODEL)

    wi = w_in[l]
    w_perm = jnp.concatenate(
        [wi[:, 0:2048], wi[:, 2080:2848], wi[:, 2048:2080],
         jnp.zeros((D_MODEL, W_IN_COLS - 2848), F32)], axis=1).astype(BF16)
    wg = jnp.zeros((LANES, 2 * GLA_WIDTH), F32)
    wg = wg.at[:GATE_RANK, :GLA_WIDTH].set(w_gk_fwd[l]).at[GATE_RANK:2 * GATE_RANK, GLA_WIDTH:].set(w_gk_bwd[l])
    wg = wg.astype(BF16)
    bg = jnp.concatenate([b_gk_fwd[l], b_gk_bwd[l]])[None, :]
    qkg = jnp.concatenate([jnp.tile(q_norm[l], SWA_HEADS), jnp.tile(k_norm[l], SWA_KV_HEADS)])[None, :]
    norm2 = jnp.tile(gla_norm[l], 2)[None, :]
    wo = w_o[l].astype(BF16)
    w1 = w_ff1[l].astype(BF16)
    w2 = w_ff2[l].astype(BF16)
    gain_mix = norm_mix[l][None, :]
    gain_ff = norm_ff[l][None, :]
    sink_l = sink[l]

    tm = 512
    xp = x_prompt.reshape(bp * tp, D_MODEL)
    row_p = lambda i: 0
    (gq, gk, gv, sgg, gf, gb, sq, sk, sv, new_k, new_v) = _project(
        xp, mod3, row_p, gain_mix, w_perm, wg, bg, qkg, None, True, tp, tm)
    og, s_f, s_b = _gla(gq, gk, gv, gf, gb, sgg, norm2, None, None, bp, tp, True)
    osw = _context_attention(sink_l, sq, sk, sv, bp, tp)
    yp = _mix_out(xp, og, osw, mod3, row_p, gain_ff, wo, w1, w2, tm)

    xs = x_sample.reshape(bs * ts, D_MODEL)
    tiles_per_seq = ts // tm
    row_s = lambda i: 1 + i // tiles_per_seq
    (gq, gk, gv, sgg, gf, gb, sq, sk, sv) = _project(
        xs, mod3, row_s, gain_mix, w_perm, wg, bg, qkg, _rope_tables(ts), False, ts, tm)
    (og,) = _gla(gq, gk, gv, gf, gb, sgg, norm2, state_gla_fwd, state_gla_bwd, bs, ts, False)
    osw = _latent_attention(sink_l, sq, sk, sv, cache_k, cache_v, bs, ts)
    ys = _mix_out(xs, og, osw, mod3, row_s, gain_ff, wo, w1, w2, tm)

    return (yp.reshape(bp, tp, D_MODEL), ys.reshape(bs, ts, D_MODEL), new_k, new_v, s_f, s_b)
```

```python
import functools

import jax
import jax.numpy as jnp
from jax import lax
from jax.experimental import pallas as pl
from jax.experimental.pallas import tpu as pltpu

F32 = jnp.float32
BF16 = jnp.bfloat16

D_MODEL = 1024
GRID_W = 64
HEAD_DIM = 64
GLA_HEADS = 8
GLA_WIDTH = GLA_HEADS * HEAD_DIM
GATE_RANK = 16
GATE_TEMP = 16.0
CHUNK = 64
GROUP = 4 * CHUNK
SWA_HEADS = 8
SWA_KV_HEADS = 2
SWA_GROUP = SWA_HEADS // SWA_KV_HEADS
SWA_WIDTH = SWA_HEADS * HEAD_DIM
KV_WIDTH = SWA_KV_HEADS * HEAD_DIM
WINDOW = 128
QBLOCK = 128
D_FF = 4 * D_MODEL
ROPE_THETA = 10000.0
EPS = 1e-6
NEG_INF = -1e30
LANES = 128

C_GQ, C_GK, C_GV, C_GG = 0, 512, 1024, 1536
C_SQ, C_SK, C_SV, C_LR = 2048, 2560, 2688, 2816
W_IN_COLS = 2944

VMEM_LIMIT = 56 * 1024 * 1024


def _cparams(n_axes):
    return pltpu.CompilerParams(dimension_semantics=("arbitrary",) * n_axes,
                                vmem_limit_bytes=VMEM_LIMIT)


def _mod_kernel(condt_ref, w_ref, b_ref, o_ref, *, n_rows):
    w = w_ref[...]
    rows = []
    for r in range(n_rows):
        cnd = condt_ref[:, r:r + 1]
        s = cnd * jax.nn.sigmoid(cnd)
        rows.append(jnp.sum(w * s, axis=0, keepdims=True) + b_ref[...])
    rows.append(jnp.zeros((8 - n_rows, w.shape[1]), F32))
    o_ref[...] = jnp.concatenate(rows, axis=0)


def _modulation(cond, w_ada, b_ada):
    n_rows = cond.shape[0]
    condt = jnp.zeros((D_MODEL, 8), F32).at[:, :n_rows].set(cond.T)
    tn = 512
    n_out = w_ada.shape[1]
    return pl.pallas_call(
        functools.partial(_mod_kernel, n_rows=n_rows),
        grid=(n_out // tn,),
        in_specs=[pl.BlockSpec((D_MODEL, 8), lambda j: (0, 0)),
                  pl.BlockSpec((D_MODEL, tn), lambda j: (0, j)),
                  pl.BlockSpec((1, tn), lambda j: (0, j))],
        out_specs=pl.BlockSpec((8, tn), lambda j: (0, j)),
        out_shape=jax.ShapeDtypeStruct((8, n_out), F32),
        compiler_params=_cparams(1),
        name="modulation",
    )(condt, w_ada, b_ada.reshape(1, n_out))


def _adaln(x, gain, shift, scale):
    ms = jnp.mean(x * x, axis=-1, keepdims=True)
    xn = x * lax.rsqrt(ms + EPS)
    return (xn * gain) * (1.0 + scale) + shift


def _head_mean_sq(y):
    cols = y.shape[1]
    lane = lax.broadcasted_iota(jnp.int32, (1, LANES), 1)
    first = lane < HEAD_DIM
    outs = []
    for p in range(cols // LANES):
        s = y[:, p * LANES:(p + 1) * LANES]
        s = s * s
        tot = jnp.sum(s, axis=-1, keepdims=True)
        lo = jnp.sum(jnp.where(first, s, 0.0), axis=-1, keepdims=True)
        outs.append(jnp.where(first, lo, tot - lo) * (1.0 / HEAD_DIM))
    return outs[0] if len(outs) == 1 else jnp.concatenate(outs, axis=-1)


def _log_sigmoid(x):
    return jnp.minimum(x, 0.0) - jnp.log1p(jnp.exp(-jnp.abs(x)))


def _dot(a, b):
    return jnp.dot(a, b, preferred_element_type=F32)


def _dot_nt(a, b):
    return lax.dot_general(a, b, (((1,), (1,)), ((), ())), preferred_element_type=F32)


def _dot_tn(a, b):
    return lax.dot_general(a, b, (((0,), (0,)), ((), ())), preferred_element_type=F32)


def _proj_kernel(*refs, rope, emit_cache, seqs_per_tile, seq_len):
    (x_ref, mod_ref, gain_ref, w_ref, wg_ref, bg_ref, qkg_ref) = refs[:7]
    pos = 7
    if rope:
        cos_ref, sin_a_ref, sin_b_ref = refs[pos:pos + 3]
        pos += 3
    (gq_ref, gk_ref, gv_ref, sgg_ref, gf_ref, gb_ref, sq_ref, sk_ref, sv_ref) = refs[pos:pos + 9]
    pos += 9
    if emit_cache:
        nk_ref, nv_ref = refs[pos:pos + 2]

    h = _adaln(x_ref[...], gain_ref[...], mod_ref[0, 0:1, :], mod_ref[0, 1:2, :]).astype(BF16)

    gq_ref[...] = (_dot(h, w_ref[:, C_GQ:C_GK]) * (HEAD_DIM ** -0.5)).astype(BF16)
    gk_ref[...] = _dot(h, w_ref[:, C_GK:C_GV]).astype(BF16)
    gv_ref[...] = _dot(h, w_ref[:, C_GV:C_GG]).astype(BF16)
    gg = _dot(h, w_ref[:, C_GG:C_SQ])
    sgg_ref[...] = (gg * jax.nn.sigmoid(gg)).astype(BF16)

    lr = _dot(h, w_ref[:, C_LR:W_IN_COLS]).astype(BF16)
    gate = _log_sigmoid(_dot(lr, wg_ref[...]) + bg_ref[...]) * (1.0 / GATE_TEMP)
    gf_ref[...] = gate[:, :GLA_WIDTH]
    gb_ref[...] = gate[:, GLA_WIDTH:]

    qk = _dot(h, w_ref[:, C_SQ:C_SV])
    qk = qk * lax.rsqrt(_head_mean_sq(qk) + EPS) * qkg_ref[...]
    if rope:
        reps = (SWA_WIDTH + KV_WIDTH) // LANES
        cos = jnp.concatenate([cos_ref[...]] * reps, axis=-1)
        sin_a = jnp.concatenate([sin_a_ref[...]] * reps, axis=-1)
        sin_b = jnp.concatenate([sin_b_ref[...]] * reps, axis=-1)
        width = qk.shape[1]
        quarter = HEAD_DIM // 4
        qk = (qk * cos + pltpu.roll(qk, quarter, 1) * sin_a
              + pltpu.roll(qk, width - quarter, 1) * sin_b)
    sq_ref[...] = qk[:, :SWA_WIDTH].astype(BF16)
    sk = qk[:, SWA_WIDTH:]
    sk_ref[...] = sk.astype(BF16)
    sv = _dot(h, w_ref[:, C_SV:C_LR])
    sv_ref[...] = sv.astype(BF16)
    if emit_cache:
        for s in range(seqs_per_tile):
            rows = slice(s * seq_len, (s + 1) * seq_len)
            for g in range(SWA_KV_HEADS):
                lanes = slice(g * HEAD_DIM, (g + 1) * HEAD_DIM)
                nk_ref[s, 0, g, :, :] = sk[rows, lanes]
                nv_ref[s, 0, g, :, :] = sv[rows, lanes]


def _project(x, mod3, mod_row_of_tile, gain, w_perm, wg, bg, qkg, rope_tabs, emit_cache, seq_len, tm):
    n = x.shape[0]
    n_tiles = n // tm
    rope = rope_tabs is not None
    seqs_per_tile = max(tm // seq_len, 1)
    const = lambda shape: pl.BlockSpec(shape, lambda i: (0,) * len(shape))
    in_specs = [pl.BlockSpec((tm, D_MODEL), lambda i: (i, 0)),
                pl.BlockSpec((1, 6, D_MODEL), lambda i: (mod_row_of_tile(i), 0, 0)),
                const((1, D_MODEL)), const((D_MODEL, W_IN_COLS)), const((LANES, 2 * GLA_WIDTH)),
                const((1, 2 * GLA_WIDTH)), const((1, SWA_WIDTH + KV_WIDTH))]
    args = [x, mod3, gain, w_perm, wg, bg, qkg]
    if rope:
        tiles_per_seq = seq_len // tm
        in_specs += [pl.BlockSpec((tm, LANES), lambda i: (i % tiles_per_seq, 0))] * 3
        args += list(rope_tabs)
    row = lambda w: pl.BlockSpec((tm, w), lambda i: (i, 0))
    out_specs = [row(GLA_WIDTH)] * 6 + [row(SWA_WIDTH), row(KV_WIDTH), row(KV_WIDTH)]
    out_shape = ([jax.ShapeDtypeStruct((n, GLA_WIDTH), BF16)] * 4
                 + [jax.ShapeDtypeStruct((n, GLA_WIDTH), F32)] * 2
                 + [jax.ShapeDtypeStruct((n, SWA_WIDTH), BF16),
                    jax.ShapeDtypeStruct((n, KV_WIDTH), BF16), jax.ShapeDtypeStruct((n, KV_WIDTH), BF16)])
    if emit_cache:
        n_seq = n // seq_len
        cache_shape = (n_seq, 1, SWA_KV_HEADS, seq_len, HEAD_DIM)
        cache_spec = pl.BlockSpec((seqs_per_tile, 1, SWA_KV_HEADS, seq_len, HEAD_DIM), lambda i: (i, 0, 0, 0, 0))
        out_specs += [cache_spec] * 2
        out_shape += [jax.ShapeDtypeStruct(cache_shape, F32)] * 2
    return pl.pallas_call(
        functools.partial(_proj_kernel, rope=rope, emit_cache=emit_cache,
                          seqs_per_tile=seqs_per_tile, seq_len=seq_len),
        grid=(n_tiles,), in_specs=in_specs, out_specs=out_specs, out_shape=out_shape,
        compiler_params=_cparams(1), name="adaln_in_proj",
    )(*args)


def _gla_kernel(*refs, seq_len, has_state, emit_state):
    (q_ref, k_ref, v_ref, gf_ref, gb_ref, sgg_ref, norm_ref,
     trif_ref, trib_ref, keepf_ref, keepb_ref) = refs[:11]
    pos = 11
    if has_state:
        s0f_ref, s0b_ref = refs[pos:pos + 2]
        pos += 2
    og_ref = refs[pos]
    pos += 1
    if emit_state:
        sf_ref, sb_ref = refs[pos:pos + 2]
        pos += 2
    accf_ref, accb_ref = refs[pos:pos + 2]

    n_groups = seq_len // GROUP
    per_group = GROUP // CHUNK
    lane = lax.broadcasted_iota(jnp.int32, (1, LANES), 1)
    head0 = lane < HEAD_DIM
    m0 = jnp.where(head0, 1.0, 0.0).astype(BF16)
    m1 = jnp.where(head0, 0.0, 1.0).astype(BF16)
    row_top = lax.broadcasted_iota(jnp.int32, (LANES, 1), 0) < HEAD_DIM
    same_head = row_top == head0

    def stack_heads(x):
        return jnp.concatenate([x * m0, x * m1], axis=0)

    def direction(row0, g_ref, tri_ref, keep_ref, state, acc_ref, backward):
        rows = pl.ds(row0, GROUP)
        g = g_ref[rows, :]
        g_hi = g.astype(BF16)
        g_lo = (g - g_hi.astype(F32)).astype(BF16)
        r = _dot(tri_ref[...], jnp.concatenate([g_hi, g_lo], axis=-1))
        b = (r[:, :LANES] + r[:, LANES:]).reshape(per_group, CHUNK, LANES)
        edge = 0 if backward else CHUNK - 1
        mid = CHUNK // 2
        b_edge = b[:, edge:edge + 1, :]
        b_mid = b[:, mid:mid + 1, :]
        q = q_ref[rows, :].astype(F32).reshape(per_group, CHUNK, LANES)
        k = k_ref[rows, :].astype(F32).reshape(per_group, CHUNK, LANES)
        v = v_ref[rows, :]
        q_in = (q * jnp.exp(b - b_mid)).astype(BF16)
        k_in = (k * jnp.exp(b_mid - b)).astype(BF16).reshape(GROUP, LANES)
        q_st = (q * jnp.exp(b)).astype(BF16)
        k_st = (k * jnp.exp(b_edge - b)).astype(BF16).reshape(GROUP, LANES)
        decay = jnp.exp(b_edge)
        keep = keep_ref[...]
        pair_rows = 2 * CHUNK
        pair_order = range(per_group // 2 - 1, -1, -1) if backward else range(per_group // 2)
        for p in pair_order:
            c0 = 2 * p
            prow = slice(p * pair_rows, (p + 1) * pair_rows)
            q_pair = jnp.concatenate([stack_heads(q_in[c0]), stack_heads(q_in[c0 + 1])], axis=0)
            a = _dot_nt(q_pair, k_in[prow, :]).astype(BF16) * keep
            v_pair = v[prow, :]
            intra = _dot(a, v_pair)
            v_t = v_pair.astype(F32).T.astype(BF16)
            k_pair = k_st[prow, :]
            zero = jnp.zeros_like(k_pair)
            k_blk = jnp.concatenate([jnp.where(row_top, k_pair, zero), jnp.where(row_top, zero, k_pair)], axis=-1)
            kv_t = _dot(v_t, k_blk)
            for j in ((1, 0) if backward else (0, 1)):
                c = c0 + j
                inter = _dot_nt(stack_heads(q_st[c]), state.astype(BF16))
                tot = intra[j * pair_rows:(j + 1) * pair_rows, :] + inter
                acc_ref[pl.ds(row0 + c * CHUNK, CHUNK), :] = jnp.where(head0, tot[:CHUNK, :], tot[CHUNK:, :])
                state = state * decay[c] + jnp.where(same_head, kv_t[:, j * LANES:(j + 1) * LANES], 0.0)
        return state

    def load_state(s_ref):
        z = jnp.zeros((HEAD_DIM, HEAD_DIM), F32)
        top = jnp.concatenate([s_ref[0, 0, 0, :, :].T, z], axis=-1)
        bot = jnp.concatenate([z, s_ref[0, 0, 1, :, :].T], axis=-1)
        return jnp.concatenate([top, bot], axis=0)

    if has_state:
        init = (load_state(s0f_ref), load_state(s0b_ref))
    else:
        init = (jnp.zeros((LANES, LANES), F32),) * 2

    def body(t, carry):
        st_f, st_b = carry
        row_f = t * GROUP
        row_b = (n_groups - 1 - t) * GROUP
        if n_groups > 1:
            row_f = pl.multiple_of(row_f, GROUP)
            row_b = pl.multiple_of(row_b, GROUP)
        st_f = direction(row_f, gf_ref, trif_ref, keepf_ref, st_f, accf_ref, False)
        st_b = direction(row_b, gb_ref, trib_ref, keepb_ref, st_b, accb_ref, True)
        return st_f, st_b

    if n_groups == 1:
        fin_f, fin_b = body(0, init)
    else:
        fin_f, fin_b = lax.fori_loop(0, n_groups, body, init)
    if emit_state:
        for s_ref, fin in ((sf_ref, fin_f), (sb_ref, fin_b)):
            s_ref[0, 0, 0, :, :] = fin[:HEAD_DIM, :HEAD_DIM].T
            s_ref[0, 0, 1, :, :] = fin[HEAD_DIM:, HEAD_DIM:].T

    o = accf_ref[...] + accb_ref[...]
    o = o * lax.rsqrt(_head_mean_sq(o) + EPS) * norm_ref[...]
    og_ref[...] = (o * sgg_ref[...].astype(F32)).astype(BF16)


def _gla_constants():
    i = jnp.arange(GROUP)[:, None]
    j = jnp.arange(GROUP)[None, :]
    same = (i // CHUNK) == (j // CHUNK)
    tri_f = (same & (j <= i)).astype(BF16)
    tri_b = (same & (j >= i)).astype(BF16)
    r = jnp.arange(4 * CHUNK)[:, None]
    l = jnp.arange(LANES)[None, :]
    same = (r // (2 * CHUNK)) == (l // CHUNK)
    keep_f = (same & ((l % CHUNK) <= (r % CHUNK))).astype(BF16)
    keep_b = (same & ((l % CHUNK) >= (r % CHUNK))).astype(BF16)
    return tri_f, tri_b, keep_f, keep_b


def _gla(gq, gk, gv, gf, gb, sgg, norm2, s0f, s0b, n_seq, seq_len, emit_state):
    n = gq.shape[0]
    has_state = s0f is not None
    pairs = GLA_WIDTH // LANES
    blk = pl.BlockSpec((seq_len, LANES), lambda b, p: (b, p))
    st_spec = pl.BlockSpec((1, 1, 2, HEAD_DIM, HEAD_DIM), lambda b, p: (b, 0, p, 0, 0))
    const = lambda shape: pl.BlockSpec(shape, lambda b, p: (0, 0))
    in_specs = ([blk] * 6 + [const((1, LANES))] + [const((GROUP, GROUP))] * 2
                + [const((4 * CHUNK, LANES))] * 2)
    args = [gq, gk, gv, gf, gb, sgg, norm2, *_gla_constants()]
    if has_state:
        in_specs += [st_spec] * 2
        args += [s0f, s0b]
    out_specs = [blk]
    out_shape = [jax.ShapeDtypeStruct((n, GLA_WIDTH), BF16)]
    if emit_state:
        out_specs += [st_spec] * 2
        out_shape += [jax.ShapeDtypeStruct((n_seq, 1, GLA_HEADS, HEAD_DIM, HEAD_DIM), F32)] * 2
    return pl.pallas_call(
        functools.partial(_gla_kernel, seq_len=seq_len, has_state=has_state, emit_state=emit_state),
        grid=(n_seq, pairs), in_specs=in_specs, out_specs=out_specs, out_shape=out_shape,
        scratch_shapes=[pltpu.VMEM((seq_len, LANES), F32)] * 2,
        compiler_params=_cparams(2), name="gla_bidirectional",
    )(*args)


def _softmax_parts(scores, sink):
    m = sink
    for s in scores:
        m = jnp.maximum(m, jnp.max(s, axis=-1, keepdims=True))
    es = [jnp.exp(s - m) for s in scores]
    denom = jnp.exp(sink - m)
    for e in es:
        denom = denom + jnp.sum(e, axis=-1, keepdims=True)
    return es, denom


def _ctx_attn_kernel(sink_ref, q_ref, k_ref, v_ref, o_ref):
    scale = HEAD_DIM ** -0.5
    q = q_ref[...]
    k = k_ref[...]
    v = v_ref[...]
    outs = []
    for h in range(SWA_HEADS):
        g = h // SWA_GROUP
        kv = slice(g * HEAD_DIM, (g + 1) * HEAD_DIM)
        s = _dot_nt(q[:, h * HEAD_DIM:(h + 1) * HEAD_DIM], k[:, kv]) * scale
        (e,), denom = _softmax_parts([s], sink_ref[h])
        outs.append(_dot(e.astype(BF16), v[:, kv]) / denom)
    o_ref[...] = jnp.concatenate(outs, axis=-1).astype(BF16)


def _context_attention(sink, sq, sk, sv, n_seq, seq_len):
    n = sq.shape[0]
    return pl.pallas_call(
        _ctx_attn_kernel,
        grid=(n_seq,),
        in_specs=[pl.BlockSpec(memory_space=pltpu.SMEM),
                  pl.BlockSpec((seq_len, SWA_WIDTH), lambda b: (b, 0)),
                  pl.BlockSpec((seq_len, KV_WIDTH), lambda b: (b, 0)),
                  pl.BlockSpec((seq_len, KV_WIDTH), lambda b: (b, 0))],
        out_specs=pl.BlockSpec((seq_len, SWA_WIDTH), lambda b: (b, 0)),
        out_shape=jax.ShapeDtypeStruct((n, SWA_WIDTH), BF16),
        compiler_params=_cparams(1), name="context_attention",
    )(sink, sq, sk, sv)


def _lat_attn_kernel(sink_ref, q_ref, k_ref, v_ref, kc_ref, vc_ref, o_ref, *, seq_len):
    scale = HEAD_DIM ** -0.5
    span = QBLOCK + 2 * WINDOW
    i = pl.program_id(1)
    start = jnp.clip(i * QBLOCK - WINDOW, 0, seq_len - span)
    start = pl.multiple_of(start, QBLOCK)
    q = q_ref[...]
    k = k_ref[pl.ds(start, span), :]
    v = v_ref[pl.ds(start, span), :]
    qpos = i * QBLOCK + lax.broadcasted_iota(jnp.int32, (QBLOCK, span), 0)
    kpos = start + lax.broadcasted_iota(jnp.int32, (QBLOCK, span), 1)
    valid = jnp.abs(qpos - kpos) <= WINDOW
    outs = []
    for h in range(SWA_HEADS):
        g = h // SWA_GROUP
        kv = slice(g * HEAD_DIM, (g + 1) * HEAD_DIM)
        qh = q[:, h * HEAD_DIM:(h + 1) * HEAD_DIM]
        s_loc = jnp.where(valid, _dot_nt(qh, k[:, kv]) * scale, NEG_INF)
        kc = kc_ref[0, 0, g, :, :].astype(BF16)
        vc = vc_ref[0, 0, g, :, :].astype(BF16)
        s_ctx = _dot_nt(qh, kc) * scale
        (e_loc, e_ctx), denom = _softmax_parts([s_loc, s_ctx], sink_ref[h])
        o = _dot(e_loc.astype(BF16), v[:, kv]) + _dot(e_ctx.astype(BF16), vc)
        outs.append(o / denom)
    o_ref[...] = jnp.concatenate(outs, axis=-1).astype(BF16)


def _latent_attention(sink, sq, sk, sv, cache_k, cache_v, n_seq, seq_len):
    n = sq.shape[0]
    nq = seq_len // QBLOCK
    past = cache_k.shape[3]
    cache_spec = pl.BlockSpec((1, 1, SWA_KV_HEADS, past, HEAD_DIM), lambda b, i: (b, 0, 0, 0, 0))
    return pl.pallas_call(
        functools.partial(_lat_attn_kernel, seq_len=seq_len),
        grid=(n_seq, nq),
        in_specs=[pl.BlockSpec(memory_space=pltpu.SMEM),
                  pl.BlockSpec((QBLOCK, SWA_WIDTH), lambda b, i: (b * nq + i, 0)),
                  pl.BlockSpec((seq_len, KV_WIDTH), lambda b, i: (b, 0)),
                  pl.BlockSpec((seq_len, KV_WIDTH), lambda b, i: (b, 0)),
                  cache_spec, cache_spec],
        out_specs=pl.BlockSpec((QBLOCK, SWA_WIDTH), lambda b, i: (b * nq + i, 0)),
        out_shape=jax.ShapeDtypeStruct((n, SWA_WIDTH), BF16),
        compiler_params=_cparams(2), name="latent_attention",
    )(sink, sq, sk, sv, cache_k, cache_v)


def _out_kernel(x_ref, og_ref, osw_ref, mod_ref, gain_ref, wo_ref, w1_ref, w2_ref, y_ref):
    mix = _dot(og_ref[...], wo_ref[:GLA_WIDTH, :]) + _dot(osw_ref[...], wo_ref[GLA_WIDTH:, :])
    x1 = x_ref[...] + mod_ref[0, 2:3, :] * mix
    h = _adaln(x1, gain_ref[...], mod_ref[0, 3:4, :], mod_ref[0, 4:5, :]).astype(BF16)
    ff_tile = 1024
    acc = None
    for j in range(D_FF // ff_tile):
        cols = slice(j * ff_tile, (j + 1) * ff_tile)
        a = jnp.maximum(_dot(h, w1_ref[:, cols]), 0.0)
        part = _dot((a * a).astype(BF16), w2_ref[cols, :])
        acc = part if acc is None else acc + part
    y_ref[...] = x1 + mod_ref[0, 5:6, :] * acc


def _mix_out(x, og, osw, mod3, mod_row_of_tile, gain, wo, w1, w2, tm):
    n = x.shape[0]
    const = lambda shape: pl.BlockSpec(shape, lambda i: (0,) * len(shape), pipeline_mode=pl.Buffered(1))
    row = lambda w: pl.BlockSpec((tm, w), lambda i: (i, 0))
    return pl.pallas_call(
        _out_kernel,
        grid=(n // tm,),
        in_specs=[row(D_MODEL), row(GLA_WIDTH), row(SWA_WIDTH),
                  pl.BlockSpec((1, 6, D_MODEL), lambda i: (mod_row_of_tile(i), 0, 0)),
                  const((1, D_MODEL)), const((GLA_WIDTH + SWA_WIDTH, D_MODEL)),
                  const((D_MODEL, D_FF)), const((D_FF, D_MODEL))],
        out_specs=row(D_MODEL),
        out_shape=jax.ShapeDtypeStruct((n, D_MODEL), F32),
        compiler_params=_cparams(1), name="out_proj_mlp",
    )(x, og, osw, mod3, gain, wo, w1, w2)


def _rope_tables(seq_len):
    axis_dim = HEAD_DIM // 2
    half = axis_dim // 2
    t = jnp.arange(seq_len)
    rowp = (t // GRID_W).astype(F32)
    colp = (t % GRID_W).astype(F32)
    freqs = ROPE_THETA ** (-jnp.arange(half, dtype=F32) / half)
    lane = jnp.arange(LANES)
    d = lane % HEAD_DIM
    posv = jnp.where((d // axis_dim)[None, :] == 0, rowp[:, None], colp[:, None])
    ang = posv * freqs[d % half][None, :]
    second = ((d % axis_dim) >= half)[None, :]
    cos, sin = jnp.cos(ang), jnp.sin(ang)
    return cos, jnp.where(second, sin, 0.0), jnp.where(second, 0.0, -sin)


def kernel(x_prompt, x_sample, c, cache_k, cache_v, state_gla_fwd, state_gla_bwd, c_ctx, w_ada, b_ada, norm_mix, norm_ff, w_in, w_gk_fwd, b_gk_fwd, w_gk_bwd, b_gk_bwd, gla_norm, q_norm, k_norm, sink, w_o, w_ff1, w_ff2):
    depth = w_in.shape[0]
    assert depth == 1, "single trunk layer"
    bp, tp, _ = x_prompt.shape
    bs, ts, _ = x_sample.shape
    l = 0

    cond = jnp.concatenate([c_ctx[None, :], c], axis=0)
    mod3 = _modulation(cond, w_ada[l], b_ada[l]).reshape(8, 6, D_MODEL)

    wi = w_in[l]
    w_perm = jnp.concatenate(
        [wi[:, 0:2048], wi[:, 2080:2848], wi[:, 2048:2080],
         jnp.zeros((D_MODEL, W_IN_COLS - 2848), F32)], axis=1).astype(BF16)
    wg = jnp.zeros((LANES, 2 * GLA_WIDTH), F32)
    wg = wg.at[:GATE_RANK, :GLA_WIDTH].set(w_gk_fwd[l]).at[GATE_RANK:2 * GATE_RANK, GLA_WIDTH:].set(w_gk_bwd[l])
    wg = wg.astype(BF16)
    bg = jnp.concatenate([b_gk_fwd[l], b_gk_bwd[l]])[None, :]
    qkg = jnp.concatenate([jnp.tile(q_norm[l], SWA_HEADS), jnp.tile(k_norm[l], SWA_KV_HEADS)])[None, :]
    norm2 = jnp.tile(gla_norm[l], 2)[None, :]
    wo = w_o[l].astype(BF16)
    w1 = w_ff1[l].astype(BF16)
    w2 = w_ff2[l].astype(BF16)
    gain_mix = norm_mix[l][None, :]
    gain_ff = norm_ff[l][None, :]
    sink_l = sink[l]

    tm = 512
    xp = x_prompt.reshape(bp * tp, D_MODEL)
    row_p = lambda i: 0
    (gq, gk, gv, sgg, gf, gb, sq, sk, sv, new_k, new_v) = _project(
        xp, mod3, row_p, gain_mix, w_perm, wg, bg, qkg, None, True, tp, tm)
    og, s_f, s_b = _gla(gq, gk, gv, gf, gb, sgg, norm2, None, None, bp, tp, True)
    osw = _context_attention(sink_l, sq, sk, sv, bp, tp)
    yp = _mix_out(xp, og, osw, mod3, row_p, gain_ff, wo, w1, w2, tm)

    xs = x_sample.reshape(bs * ts, D_MODEL)
    tiles_per_seq = ts // tm
    row_s = lambda i: 1 + i // tiles_per_seq
    (gq, gk, gv, sgg, gf, gb, sq, sk, sv) = _project(
        xs, mod3, row_s, gain_mix, w_perm, wg, bg, qkg, _rope_tables(ts), False, ts, tm)
    (og,) = _gla(gq, gk, gv, gf, gb, sgg, norm2, state_gla_fwd, state_gla_bwd, bs, ts, False)
    osw = _latent_attention(sink_l, sq, sk, sv, cache_k, cache_v, bs, ts)
    ys = _mix_out(xs, og, osw, mod3, row_s, gain_ff, wo, w1, w2, tm)

    return (yp.reshape(bp, tp, D_MODEL), ys.reshape(bs, ts, D_MODEL), new_k, new_v, s_f, s_b)
```

```python
import functools

import numpy as np
import jax
import jax.numpy as jnp
from jax import lax
from jax.experimental import pallas as pl
from jax.experimental.pallas import tpu as pltpu

F32 = jnp.float32
BF16 = jnp.bfloat16

D_MODEL = 1024
GRID_W = 64
HEAD_DIM = 64
GLA_HEADS = 8
GLA_WIDTH = GLA_HEADS * HEAD_DIM
GATE_RANK = 16
GATE_TEMP = 16.0
CHUNK = 64
GROUP = 4 * CHUNK
SWA_HEADS = 8
SWA_KV_HEADS = 2
SWA_GROUP = SWA_HEADS // SWA_KV_HEADS
SWA_WIDTH = SWA_HEADS * HEAD_DIM
KV_WIDTH = SWA_KV_HEADS * HEAD_DIM
WINDOW = 128
QBLOCK = 128
D_FF = 4 * D_MODEL
ROPE_THETA = 10000.0
EPS = 1e-6
NEG_INF = -1e30
LANES = 128
TM = 512

D_IN = 2848
SRC_LR = 2048
SRC_SQ = SRC_LR + 2 * GATE_RANK
C_GQ, C_GK, C_GV, C_GG = 0, 512, 1024, 1536
C_SQ, C_SK, C_SV, C_LR = 2048, 2560, 2688, 2816
W_IN_COLS = 2944

VMEM_LIMIT = 56 * 1024 * 1024


def _cparams(n_axes):
    return pltpu.CompilerParams(dimension_semantics=("arbitrary",) * n_axes,
                                vmem_limit_bytes=VMEM_LIMIT)


def _mod_kernel(condt_ref, w_ref, b_ref, o_ref, *, n_rows):
    w = w_ref[...]
    rows = []
    for r in range(n_rows):
        cnd = condt_ref[:, r:r + 1]
        s = cnd * jax.nn.sigmoid(cnd)
        rows.append(jnp.sum(w * s, axis=0, keepdims=True) + b_ref[...])
    rows.append(jnp.zeros((8 - n_rows, w.shape[1]), F32))
    o_ref[...] = jnp.concatenate(rows, axis=0)


def _modulation(cond, w_ada, b_ada):
    n_rows = cond.shape[0]
    condt = jnp.zeros((D_MODEL, 8), F32).at[:, :n_rows].set(cond.T)
    tn = 512
    n_out = w_ada.shape[1]
    return pl.pallas_call(
        functools.partial(_mod_kernel, n_rows=n_rows),
        grid=(n_out // tn,),
        in_specs=[pl.BlockSpec((D_MODEL, 8), lambda j: (0, 0)),
                  pl.BlockSpec((D_MODEL, tn), lambda j: (0, j)),
                  pl.BlockSpec((1, tn), lambda j: (0, j))],
        out_specs=pl.BlockSpec((8, tn), lambda j: (0, j)),
        out_shape=jax.ShapeDtypeStruct((8, n_out), F32),
        compiler_params=_cparams(1),
        name="modulation",
    )(condt, w_ada, b_ada.reshape(1, n_out))


def _adaln(x, gain, shift, scale):
    ms = jnp.mean(x * x, axis=-1, keepdims=True)
    xn = x * lax.rsqrt(ms + EPS)
    return (xn * gain) * (1.0 + scale) + shift


def _head_mean_sq(y):
    cols = y.shape[1]
    lane = lax.broadcasted_iota(jnp.int32, (1, LANES), 1)
    first = lane < HEAD_DIM
    outs = []
    for p in range(cols // LANES):
        s = y[:, p * LANES:(p + 1) * LANES]
        s = s * s
        tot = jnp.sum(s, axis=-1, keepdims=True)
        lo = jnp.sum(jnp.where(first, s, 0.0), axis=-1, keepdims=True)
        outs.append(jnp.where(first, lo, tot - lo) * (1.0 / HEAD_DIM))
    return outs[0] if len(outs) == 1 else jnp.concatenate(outs, axis=-1)


def _log_sigmoid(x):
    return jnp.minimum(x, 0.0) - jnp.log1p(jnp.exp(-jnp.abs(x)))


def _dot(a, b):
    return jnp.dot(a, b, preferred_element_type=F32)


def _dot_nt(a, b):
    return lax.dot_general(a, b, (((1,), (1,)), ((), ())), preferred_element_type=F32)


def _proj_kernel(xp_ref, xs_ref, mod_ref, gain_ref, w_ref, wg_ref, bg_ref, qkg_ref,
                 cos_ref, sin_a_ref, sin_b_ref,
                 gq_ref, gk_ref, gv_ref, sgg_ref, gf_ref, gb_ref, sq_ref, sk_ref, sv_ref, nk_ref, nv_ref,
                 wb_ref, *, n_prompt_tiles, seqs_per_tile, seq_len):
    i = pl.program_id(0)

    @pl.when(i == 0)
    def _():
        blk = 128

        def conv(r, carry):
            rows = pl.ds(pl.multiple_of(r * blk, blk), blk)
            wb_ref[rows, 0:C_SQ] = w_ref[rows, 0:SRC_LR].astype(BF16)
            wb_ref[rows, C_SQ:C_LR] = w_ref[rows, SRC_SQ:D_IN].astype(BF16)
            wb_ref[rows, C_LR:W_IN_COLS] = w_ref[rows, SRC_LR:SRC_LR + LANES].astype(BF16)
            return carry

        lax.fori_loop(0, D_MODEL // blk, conv, 0)

    def body(x_ref, rope, emit_cache):
        h = _adaln(x_ref[...], gain_ref[...], mod_ref[0, 0:1, :], mod_ref[0, 1:2, :]).astype(BF16)

        gq_ref[...] = (_dot(h, wb_ref[:, C_GQ:C_GK]) * (HEAD_DIM ** -0.5)).astype(BF16)
        gk_ref[...] = _dot(h, wb_ref[:, C_GK:C_GV]).astype(BF16)
        gv_ref[...] = _dot(h, wb_ref[:, C_GV:C_GG]).astype(BF16)
        gg = _dot(h, wb_ref[:, C_GG:C_SQ])
        sgg_ref[...] = (gg * jax.nn.sigmoid(gg)).astype(BF16)

        tail = _dot(h, wb_ref[:, C_SQ:W_IN_COLS])
        lr = tail[:, C_LR - C_SQ:].astype(BF16)
        gate = _log_sigmoid(_dot(lr, wg_ref[...]) + bg_ref[...]) * (1.0 / GATE_TEMP)
        gf_ref[...] = gate[:, :GLA_WIDTH]
        gb_ref[...] = gate[:, GLA_WIDTH:]

        qk = tail[:, :C_SV - C_SQ]
        qk = qk * lax.rsqrt(_head_mean_sq(qk) + EPS) * qkg_ref[...]
        if rope:
            reps = (SWA_WIDTH + KV_WIDTH) // LANES
            cos = jnp.concatenate([cos_ref[...]] * reps, axis=-1)
            sin_a = jnp.concatenate([sin_a_ref[...]] * reps, axis=-1)
            sin_b = jnp.concatenate([sin_b_ref[...]] * reps, axis=-1)
            width = qk.shape[1]
            quarter = HEAD_DIM // 4
            qk = (qk * cos + pltpu.roll(qk, quarter, 1) * sin_a
                  + pltpu.roll(qk, width - quarter, 1) * sin_b)
        sq_ref[...] = (qk[:, :SWA_WIDTH] * (HEAD_DIM ** -0.5)).astype(BF16)
        sk = qk[:, SWA_WIDTH:]
        sk_ref[...] = sk.astype(BF16)
        sv = tail[:, C_SV - C_SQ:C_LR - C_SQ]
        sv_ref[...] = sv.astype(BF16)
        if emit_cache:
            for s in range(seqs_per_tile):
                rows = slice(s * seq_len, (s + 1) * seq_len)
                for g in range(SWA_KV_HEADS):
                    lanes = slice(g * HEAD_DIM, (g + 1) * HEAD_DIM)
                    nk_ref[s, 0, g, :, :] = sk[rows, lanes]
                    nv_ref[s, 0, g, :, :] = sv[rows, lanes]

    @pl.when(i < n_prompt_tiles)
    def _():
        body(xp_ref, rope=False, emit_cache=True)

    @pl.when(i >= n_prompt_tiles)
    def _():
        body(xs_ref, rope=True, emit_cache=False)


def _project(xp, xs, mod3, gain, w_in, wg, bg, qkg, rope_tabs, seq_p, seq_s):
    n_p, n_s = xp.shape[0], xs.shape[0]
    tp, ts = n_p // TM, n_s // TM
    n = n_p + n_s
    seqs_per_tile = TM // seq_p
    tiles_per_seq = seq_s // TM
    n_seq_p = n_p // seq_p
    const = lambda shape: pl.BlockSpec(shape, lambda i: (0,) * len(shape))
    mod_row = lambda i: jnp.where(i < tp, 0, 1 + (i - tp) // tiles_per_seq)
    rope_spec = pl.BlockSpec((TM, LANES), lambda i: (jnp.maximum(i - tp, 0) % tiles_per_seq, 0))
    in_specs = [pl.BlockSpec((TM, D_MODEL), lambda i: (jnp.minimum(i, tp - 1), 0)),
                pl.BlockSpec((TM, D_MODEL), lambda i: (jnp.maximum(i - tp, 0), 0)),
                pl.BlockSpec((1, 6, D_MODEL), lambda i: (mod_row(i), 0, 0)),
                const((1, D_MODEL)),
                pl.BlockSpec((D_MODEL, D_IN), lambda i: (0, 0), pipeline_mode=pl.Buffered(1)),
                const((LANES, 2 * GLA_WIDTH)), const((1, 2 * GLA_WIDTH)), const((1, SWA_WIDTH + KV_WIDTH)),
                rope_spec, rope_spec, rope_spec]
    row = lambda w: pl.BlockSpec((TM, w), lambda i: (i, 0))
    cache_spec = pl.BlockSpec((seqs_per_tile, 1, SWA_KV_HEADS, seq_p, HEAD_DIM),
                              lambda i: (jnp.minimum(i, tp - 1), 0, 0, 0, 0))
    out_specs = [row(GLA_WIDTH)] * 6 + [row(SWA_WIDTH), row(KV_WIDTH), row(KV_WIDTH)] + [cache_spec] * 2
    cache_shape = (n_seq_p, 1, SWA_KV_HEADS, seq_p, HEAD_DIM)
    out_shape = ([jax.ShapeDtypeStruct((n, GLA_WIDTH), BF16)] * 4
                 + [jax.ShapeDtypeStruct((n, GLA_WIDTH), F32)] * 2
                 + [jax.ShapeDtypeStruct((n, SWA_WIDTH), BF16),
                    jax.ShapeDtypeStruct((n, KV_WIDTH), BF16), jax.ShapeDtypeStruct((n, KV_WIDTH), BF16)]
                 + [jax.ShapeDtypeStruct(cache_shape, F32)] * 2)
    return pl.pallas_call(
        functools.partial(_proj_kernel, n_prompt_tiles=tp, seqs_per_tile=seqs_per_tile, seq_len=seq_p),
        grid=(tp + ts,), in_specs=in_specs, out_specs=out_specs, out_shape=out_shape,
        scratch_shapes=[pltpu.VMEM((D_MODEL, W_IN_COLS), BF16)],
        compiler_params=_cparams(1), name="adaln_in_proj",
    )(xp, xs, mod3, gain, w_in, wg, bg, qkg, *rope_tabs)


def _gla_kernel(*refs, seq_len, has_state, emit_state, n_cast):
    (q_ref, k_ref, v_ref, gf_ref, gb_ref, sgg_ref, norm_ref,
     trif_ref, trib_ref, keepf_ref, keepb_ref) = refs[:11]
    pos = 11
    if has_state:
        s0f_ref, s0b_ref = refs[pos:pos + 2]
        pos += 2
    cast_in = refs[pos:pos + n_cast]
    pos += n_cast
    og_ref = refs[pos]
    pos += 1
    if emit_state:
        sf_ref, sb_ref = refs[pos:pos + 2]
        pos += 2
    cast_out = refs[pos:pos + n_cast]
    pos += n_cast
    accf_ref, accb_ref = refs[pos:pos + 2]

    for src, dst in zip(cast_in, cast_out):
        dst[...] = src[...].astype(BF16)

    n_groups = seq_len // GROUP
    per_group = GROUP // CHUNK
    lane = lax.broadcasted_iota(jnp.int32, (1, LANES), 1)
    head0 = lane < HEAD_DIM
    m0 = jnp.where(head0, 1.0, 0.0).astype(BF16)
    m1 = jnp.where(head0, 0.0, 1.0).astype(BF16)
    row_top = lax.broadcasted_iota(jnp.int32, (LANES, 1), 0) < HEAD_DIM
    same_head = row_top == head0

    def stack_heads(x):
        return jnp.concatenate([x * m0, x * m1], axis=0)

    def direction(row0, g_ref, tri_ref, keep_ref, state, acc_ref, backward):
        rows = pl.ds(row0, GROUP)
        g = g_ref[rows, :]
        g_hi = g.astype(BF16)
        g_lo = (g - g_hi.astype(F32)).astype(BF16)
        r = _dot(tri_ref[...], jnp.concatenate([g_hi, g_lo], axis=-1))
        b = (r[:, :LANES] + r[:, LANES:]).reshape(per_group, CHUNK, LANES)
        edge = 0 if backward else CHUNK - 1
        mid = CHUNK // 2
        b_edge = b[:, edge:edge + 1, :]
        b_mid = b[:, mid:mid + 1, :]
        q = q_ref[rows, :].astype(F32).reshape(per_group, CHUNK, LANES)
        k = k_ref[rows, :].astype(F32).reshape(per_group, CHUNK, LANES)
        v = v_ref[rows, :]
        q_in = (q * jnp.exp(b - b_mid)).astype(BF16)
        k_in = (k * jnp.exp(b_mid - b)).astype(BF16).reshape(GROUP, LANES)
        q_st = (q * jnp.exp(b)).astype(BF16)
        k_st = (k * jnp.exp(b_edge - b)).astype(BF16).reshape(GROUP, LANES)
        decay = jnp.exp(b_edge)
        keep = keep_ref[...]
        pair_rows = 2 * CHUNK
        pair_order = range(per_group // 2 - 1, -1, -1) if backward else range(per_group // 2)
        for p in pair_order:
            c0 = 2 * p
            prow = slice(p * pair_rows, (p + 1) * pair_rows)
            q_pair = jnp.concatenate([stack_heads(q_in[c0]), stack_heads(q_in[c0 + 1])], axis=0)
            a = _dot_nt(q_pair, k_in[prow, :]).astype(BF16) * keep
            v_pair = v[prow, :]
            intra = _dot(a, v_pair)
            v_t = v_pair.astype(F32).T.astype(BF16)
            k_pair = k_st[prow, :]
            zero = jnp.zeros_like(k_pair)
            k_blk = jnp.concatenate([jnp.where(row_top, k_pair, zero), jnp.where(row_top, zero, k_pair)], axis=-1)
            kv_t = _dot(v_t, k_blk)
            for j in ((1, 0) if backward else (0, 1)):
                c = c0 + j
                inter = _dot_nt(stack_heads(q_st[c]), state.astype(BF16))
                tot = intra[j * pair_rows:(j + 1) * pair_rows, :] + inter
                acc_ref[pl.ds(row0 + c * CHUNK, CHUNK), :] = jnp.where(head0, tot[:CHUNK, :], tot[CHUNK:, :])
                state = state * decay[c] + jnp.where(same_head, kv_t[:, j * LANES:(j + 1) * LANES], 0.0)
        return state

    def load_state(s_ref):
        z = jnp.zeros((HEAD_DIM, HEAD_DIM), F32)
        top = jnp.concatenate([s_ref[0, 0, 0, :, :].T, z], axis=-1)
        bot = jnp.concatenate([z, s_ref[0, 0, 1, :, :].T], axis=-1)
        return jnp.concatenate([top, bot], axis=0)

    if has_state:
        init = (load_state(s0f_ref), load_state(s0b_ref))
    else:
        init = (jnp.zeros((LANES, LANES), F32),) * 2

    def body(t, carry):
        st_f, st_b = carry
        row_f = t * GROUP
        row_b = (n_groups - 1 - t) * GROUP
        if n_groups > 1:
            row_f = pl.multiple_of(row_f, GROUP)
            row_b = pl.multiple_of(row_b, GROUP)
        st_f = direction(row_f, gf_ref, trif_ref, keepf_ref, st_f, accf_ref, False)
        st_b = direction(row_b, gb_ref, trib_ref, keepb_ref, st_b, accb_ref, True)
        return st_f, st_b

    if n_groups == 1:
        fin_f, fin_b = body(0, init)
    else:
        fin_f, fin_b = lax.fori_loop(0, n_groups, body, init)
    if emit_state:
        for s_ref, fin in ((sf_ref, fin_f), (sb_ref, fin_b)):
            s_ref[0, 0, 0, :, :] = fin[:HEAD_DIM, :HEAD_DIM].T
            s_ref[0, 0, 1, :, :] = fin[HEAD_DIM:, HEAD_DIM:].T

    o = accf_ref[...] + accb_ref[...]
    o = o * lax.rsqrt(_head_mean_sq(o) + EPS) * norm_ref[...]
    og_ref[...] = (o * sgg_ref[...].astype(F32)).astype(BF16)


def _gla_constants():
    i = np.arange(GROUP)[:, None]
    j = np.arange(GROUP)[None, :]
    same = (i // CHUNK) == (j // CHUNK)
    tri_f = same & (j <= i)
    tri_b = same & (j >= i)
    r = np.arange(4 * CHUNK)[:, None]
    l = np.arange(LANES)[None, :]
    same = (r // (2 * CHUNK)) == (l // CHUNK)
    keep_f = same & ((l % CHUNK) <= (r % CHUNK))
    keep_b = same & ((l % CHUNK) >= (r % CHUNK))
    return tuple(jnp.asarray(m.astype(np.float32), dtype=BF16) for m in (tri_f, tri_b, keep_f, keep_b))


def _gla(gq, gk, gv, gf, gb, sgg, norm2, s0f, s0b, n_seq, seq_len, row_block0, emit_state, cast_weights=()):
    has_state = s0f is not None
    pairs = GLA_WIDTH // LANES
    steps = n_seq * pairs
    blk = pl.BlockSpec((seq_len, LANES), lambda b, p: (row_block0 + b, p))
    out_blk = pl.BlockSpec((seq_len, LANES), lambda b, p: (b, p))
    st_spec = pl.BlockSpec((1, 1, 2, HEAD_DIM, HEAD_DIM), lambda b, p: (b, 0, p, 0, 0))
    const = lambda shape: pl.BlockSpec(shape, lambda b, p: (0, 0))
    in_specs = ([blk] * 6 + [const((1, LANES))] + [const((GROUP, GROUP))] * 2
                + [const((4 * CHUNK, LANES))] * 2)
    args = [gq, gk, gv, gf, gb, sgg, norm2, *_gla_constants()]
    if has_state:
        in_specs += [st_spec] * 2
        args += [s0f, s0b]
    out_specs = [out_blk]
    out_shape = [jax.ShapeDtypeStruct((n_seq * seq_len, GLA_WIDTH), BF16)]
    if emit_state:
        out_specs += [st_spec] * 2
        out_shape += [jax.ShapeDtypeStruct((n_seq, 1, GLA_HEADS, HEAD_DIM, HEAD_DIM), F32)] * 2
    for w in cast_weights:
        rows = w.shape[0] // steps
        spec = pl.BlockSpec((rows, w.shape[1]), lambda b, p: (b * pairs + p, 0))
        in_specs.append(spec)
        args.append(w)
        out_specs.append(spec)
        out_shape.append(jax.ShapeDtypeStruct(w.shape, BF16))
    return pl.pallas_call(
        functools.partial(_gla_kernel, seq_len=seq_len, has_state=has_state, emit_state=emit_state,
                          n_cast=len(cast_weights)),
        grid=(n_seq, pairs), in_specs=in_specs, out_specs=out_specs, out_shape=out_shape,
        scratch_shapes=[pltpu.VMEM((seq_len, LANES), F32)] * 2,
        compiler_params=_cparams(2), name="gla_bidirectional",
    )(*args)


def _softmax_parts(scores, sink):
    m = sink
    for s in scores:
        m = jnp.maximum(m, jnp.max(s, axis=-1, keepdims=True))
    es = [jnp.exp(s - m) for s in scores]
    denom = jnp.exp(sink - m)
    for e in es:
        denom = denom + jnp.sum(e, axis=-1, keepdims=True)
    return es, denom


def _ctx_attn_kernel(sink_ref, q_ref, k_ref, v_ref, o_ref):
    q = q_ref[...]
    k = k_ref[...]
    v = v_ref[...]
    outs = []
    for h in range(SWA_HEADS):
        g = h // SWA_GROUP
        kv = slice(g * HEAD_DIM, (g + 1) * HEAD_DIM)
        s = _dot_nt(q[:, h * HEAD_DIM:(h + 1) * HEAD_DIM], k[:, kv])
        (e,), denom = _softmax_parts([s], sink_ref[h])
        outs.append(_dot(e.astype(BF16), v[:, kv]) / denom)
    o_ref[...] = jnp.concatenate(outs, axis=-1).astype(BF16)


def _context_attention(sink, sq, sk, sv, n_seq, seq_len):
    return pl.pallas_call(
        _ctx_attn_kernel,
        grid=(n_seq,),
        in_specs=[pl.BlockSpec(memory_space=pltpu.SMEM),
                  pl.BlockSpec((seq_len, SWA_WIDTH), lambda b: (b, 0)),
                  pl.BlockSpec((seq_len, KV_WIDTH), lambda b: (b, 0)),
                  pl.BlockSpec((seq_len, KV_WIDTH), lambda b: (b, 0))],
        out_specs=pl.BlockSpec((seq_len, SWA_WIDTH), lambda b: (b, 0)),
        out_shape=jax.ShapeDtypeStruct((n_seq * seq_len, SWA_WIDTH), BF16),
        compiler_params=_cparams(1), name="context_attention",
    )(sink, sq, sk, sv)


def _lat_attn_kernel(sink_ref, q_ref, k_ref, v_ref, kc_ref, vc_ref, o_ref, *, seq_len):
    span = QBLOCK + 2 * WINDOW
    i = pl.program_id(1)
    start = jnp.clip(i * QBLOCK - WINDOW, 0, seq_len - span)
    start = pl.multiple_of(start, QBLOCK)
    q = q_ref[...]
    k = k_ref[pl.ds(start, span), :]
    v = v_ref[pl.ds(start, span), :]
    qpos = i * QBLOCK + lax.broadcasted_iota(jnp.int32, (QBLOCK, span), 0)
    kpos = start + lax.broadcasted_iota(jnp.int32, (QBLOCK, span), 1)
    valid = jnp.abs(qpos - kpos) <= WINDOW
    outs = []
    for h in range(SWA_HEADS):
        g = h // SWA_GROUP
        kv = slice(g * HEAD_DIM, (g + 1) * HEAD_DIM)
        qh = q[:, h * HEAD_DIM:(h + 1) * HEAD_DIM]
        s_loc = jnp.where(valid, _dot_nt(qh, k[:, kv]), NEG_INF)
        kc = kc_ref[0, 0, g, :, :].astype(BF16)
        vc = vc_ref[0, 0, g, :, :].astype(BF16)
        s_ctx = _dot_nt(qh, kc)
        (e_loc, e_ctx), denom = _softmax_parts([s_loc, s_ctx], sink_ref[h])
        o = _dot(e_loc.astype(BF16), v[:, kv]) + _dot(e_ctx.astype(BF16), vc)
        outs.append(o / denom)
    o_ref[...] = jnp.concatenate(outs, axis=-1).astype(BF16)


def _latent_attention(sink, sq, sk, sv, cache_k, cache_v, n_seq, seq_len, row0):
    nq = seq_len // QBLOCK
    q0 = row0 // QBLOCK
    s0 = row0 // seq_len
    past = cache_k.shape[3]
    cache_spec = pl.BlockSpec((1, 1, SWA_KV_HEADS, past, HEAD_DIM), lambda b, i: (b, 0, 0, 0, 0))
    return pl.pallas_call(
        functools.partial(_lat_attn_kernel, seq_len=seq_len),
        grid=(n_seq, nq),
        in_specs=[pl.BlockSpec(memory_space=pltpu.SMEM),
                  pl.BlockSpec((QBLOCK, SWA_WIDTH), lambda b, i: (q0 + b * nq + i, 0)),
                  pl.BlockSpec((seq_len, KV_WIDTH), lambda b, i: (s0 + b, 0)),
                  pl.BlockSpec((seq_len, KV_WIDTH), lambda b, i: (s0 + b, 0)),
                  cache_spec, cache_spec],
        out_specs=pl.BlockSpec((QBLOCK, SWA_WIDTH), lambda b, i: (b * nq + i, 0)),
        out_shape=jax.ShapeDtypeStruct((n_seq * seq_len, SWA_WIDTH), BF16),
        compiler_params=_cparams(2), name="latent_attention",
    )(sink, sq, sk, sv, cache_k, cache_v)


def _out_kernel(xp_ref, xs_ref, ogp_ref, ogs_ref, oswp_ref, osws_ref, mod_ref, gain_ref,
                wo_ref, w1_ref, w2_ref, yp_ref, ys_ref, *, n_prompt_tiles):
    def body(x_ref, og_ref, osw_ref, y_ref):
        mix = _dot(og_ref[...], wo_ref[:GLA_WIDTH, :]) + _dot(osw_ref[...], wo_ref[GLA_WIDTH:, :])
        x1 = x_ref[...] + mod_ref[0, 2:3, :] * mix
        h = _adaln(x1, gain_ref[...], mod_ref[0, 3:4, :], mod_ref[0, 4:5, :]).astype(BF16)
        ff_tile = 1024
        acc = None
        for j in range(D_FF // ff_tile):
            cols = slice(j * ff_tile, (j + 1) * ff_tile)
            a = jnp.maximum(_dot(h, w1_ref[:, cols]), 0.0)
            part = _dot((a * a).astype(BF16), w2_ref[cols, :])
            acc = part if acc is None else acc + part
        y_ref[...] = x1 + mod_ref[0, 5:6, :] * acc

    i = pl.program_id(0)

    @pl.when(i < n_prompt_tiles)
    def _():
        body(xp_ref, ogp_ref, oswp_ref, yp_ref)

    @pl.when(i >= n_prompt_tiles)
    def _():
        body(xs_ref, ogs_ref, osws_ref, ys_ref)


def _mix_out(xp, xs, og_p, og_s, osw_p, osw_s, mod3, gain, wo, w1, w2, seq_s):
    n_p, n_s = xp.shape[0], xs.shape[0]
    tp, ts = n_p // TM, n_s // TM
    tiles_per_seq = seq_s // TM
    const = lambda shape: pl.BlockSpec(shape, lambda i: (0,) * len(shape), pipeline_mode=pl.Buffered(1))
    row_p = lambda w: pl.BlockSpec((TM, w), lambda i: (jnp.minimum(i, tp - 1), 0))
    row_s = lambda w: pl.BlockSpec((TM, w), lambda i: (jnp.maximum(i - tp, 0), 0))
    mod_row = lambda i: jnp.where(i < tp, 0, 1 + (i - tp) // tiles_per_seq)
    return pl.pallas_call(
        functools.partial(_out_kernel, n_prompt_tiles=tp),
        grid=(tp + ts,),
        in_specs=[row_p(D_MODEL), row_s(D_MODEL), row_p(GLA_WIDTH), row_s(GLA_WIDTH),
                  row_p(SWA_WIDTH), row_s(SWA_WIDTH),
                  pl.BlockSpec((1, 6, D_MODEL), lambda i: (mod_row(i), 0, 0)),
                  const((1, D_MODEL)), const((GLA_WIDTH + SWA_WIDTH, D_MODEL)),
                  const((D_MODEL, D_FF)), const((D_FF, D_MODEL))],
        out_specs=[row_p(D_MODEL), row_s(D_MODEL)],
        out_shape=[jax.ShapeDtypeStruct((n_p, D_MODEL), F32), jax.ShapeDtypeStruct((n_s, D_MODEL), F32)],
        compiler_params=_cparams(1), name="out_proj_mlp",
    )(xp, xs, og_p, og_s, osw_p, osw_s, mod3, gain, wo, w1, w2)


def _rope_tables(seq_len):
    axis_dim = HEAD_DIM // 2
    half = axis_dim // 2
    t = np.arange(seq_len)
    rowp = (t // GRID_W).astype(np.float64)
    colp = (t % GRID_W).astype(np.float64)
    freqs = ROPE_THETA ** (-np.arange(half, dtype=np.float64) / half)
    lane = np.arange(LANES)
    d = lane % HEAD_DIM
    posv = np.where((d // axis_dim)[None, :] == 0, rowp[:, None], colp[:, None])
    ang = posv * freqs[d % half][None, :]
    second = ((d % axis_dim) >= half)[None, :]
    cos, sin = np.cos(ang), np.sin(ang)
    zero = np.zeros_like(sin)
    tabs = (cos, np.where(second, sin, zero), np.where(second, zero, -sin))
    return tuple(jnp.asarray(tab.astype(np.float32)) for tab in tabs)


def kernel(x_prompt, x_sample, c, cache_k, cache_v, state_gla_fwd, state_gla_bwd, c_ctx, w_ada, b_ada, norm_mix, norm_ff, w_in, w_gk_fwd, b_gk_fwd, w_gk_bwd, b_gk_bwd, gla_norm, q_norm, k_norm, sink, w_o, w_ff1, w_ff2):
    depth = w_in.shape[0]
    assert depth == 1, "single trunk layer"
    bp, tp, _ = x_prompt.shape
    bs, ts, _ = x_sample.shape
    n_p = bp * tp
    l = 0

    cond = jnp.concatenate([c_ctx[None, :], c], axis=0)
    mod3 = _modulation(cond, w_ada[l], b_ada[l]).reshape(8, 6, D_MODEL)

    wg = jnp.zeros((LANES, 2 * GLA_WIDTH), F32)
    wg = wg.at[:GATE_RANK, :GLA_WIDTH].set(w_gk_fwd[l]).at[GATE_RANK:2 * GATE_RANK, GLA_WIDTH:].set(w_gk_bwd[l])
    wg = wg.astype(BF16)
    bg = jnp.concatenate([b_gk_fwd[l], b_gk_bwd[l]])[None, :]
    qkg = jnp.concatenate([jnp.tile(q_norm[l], SWA_HEADS), jnp.tile(k_norm[l], SWA_KV_HEADS)])[None, :]
    norm2 = jnp.tile(gla_norm[l], 2)[None, :]
    gain_mix = norm_mix[l][None, :]
    gain_ff = norm_ff[l][None, :]
    sink_l = sink[l]

    xp = x_prompt.reshape(n_p, D_MODEL)
    xs = x_sample.reshape(bs * ts, D_MODEL)
    (gq, gk, gv, sgg, gf, gb, sq, sk, sv, new_k, new_v) = _project(
        xp, xs, mod3, gain_mix, w_in[l], wg, bg, qkg, _rope_tables(ts), tp, ts)

    og_p, s_f, s_b, wo, w1, w2 = _gla(gq, gk, gv, gf, gb, sgg, norm2, None, None, bp, tp, 0, True,
                                      cast_weights=(w_o[l], w_ff1[l], w_ff2[l]))
    osw_p = _context_attention(sink_l, sq, sk, sv, bp, tp)
    (og_s,) = _gla(gq, gk, gv, gf, gb, sgg, norm2, state_gla_fwd, state_gla_bwd, bs, ts, n_p // ts, False)
    osw_s = _latent_attention(sink_l, sq, sk, sv, cache_k, cache_v, bs, ts, n_p)

    yp, ys = _mix_out(xp, xs, og_p, og_s, osw_p, osw_s, mod3, gain_ff, wo, w1, w2, ts)
    return (yp.reshape(bp, tp, D_MODEL), ys.reshape(bs, ts, D_MODEL), new_k, new_v, s_f, s_b)
```

```python
import functools

import numpy as np
import jax
import jax.numpy as jnp
from jax import lax
from jax.experimental import pallas as pl
from jax.experimental.pallas import tpu as pltpu

F32 = jnp.float32
BF16 = jnp.bfloat16

D_MODEL = 1024
GRID_W = 64
HEAD_DIM = 64
GLA_HEADS = 8
GLA_WIDTH = GLA_HEADS * HEAD_DIM
GATE_RANK = 16
GATE_TEMP = 16.0
CHUNK = 64
GROUP = 4 * CHUNK
SWA_HEADS = 8
SWA_KV_HEADS = 2
SWA_GROUP = SWA_HEADS // SWA_KV_HEADS
SWA_WIDTH = SWA_HEADS * HEAD_DIM
KV_WIDTH = SWA_KV_HEADS * HEAD_DIM
WINDOW = 128
QBLOCK = 128
D_FF = 4 * D_MODEL
ROPE_THETA = 10000.0
EPS = 1e-6
NEG_INF = -1e30
LANES = 128
TM = 512

D_IN = 2848
R_GQ, R_GK, R_GV, R_GG, R_LR = 0, 512, 1024, 1536, 2048
R_SQ = R_LR + 2 * GATE_RANK
R_SV = R_SQ + SWA_WIDTH + KV_WIDTH

VMEM_LIMIT = 56 * 1024 * 1024


def _cparams(n_axes):
    return pltpu.CompilerParams(dimension_semantics=("arbitrary",) * n_axes,
                                vmem_limit_bytes=VMEM_LIMIT)


def _mod_kernel(condt_ref, w_ref, b_ref, o_ref, *, n_rows):
    w = w_ref[...]
    rows = []
    for r in range(n_rows):
        cnd = condt_ref[:, r:r + 1]
        s = cnd * jax.nn.sigmoid(cnd)
        rows.append(jnp.sum(w * s, axis=0, keepdims=True) + b_ref[...])
    rows.append(jnp.zeros((8 - n_rows, w.shape[1]), F32))
    o_ref[...] = jnp.concatenate(rows, axis=0)


def _modulation(cond, w_ada, b_ada):
    n_rows = cond.shape[0]
    condt = jnp.zeros((D_MODEL, 8), F32).at[:, :n_rows].set(cond.T)
    tn = 512
    n_out = w_ada.shape[1]
    return pl.pallas_call(
        functools.partial(_mod_kernel, n_rows=n_rows),
        grid=(n_out // tn,),
        in_specs=[pl.BlockSpec((D_MODEL, 8), lambda j: (0, 0)),
                  pl.BlockSpec((D_MODEL, tn), lambda j: (0, j)),
                  pl.BlockSpec((1, tn), lambda j: (0, j))],
        out_specs=pl.BlockSpec((8, tn), lambda j: (0, j)),
        out_shape=jax.ShapeDtypeStruct((8, n_out), F32),
        compiler_params=_cparams(1),
        name="modulation",
    )(condt, w_ada, b_ada.reshape(1, n_out))


def _adaln(x, gain, shift, scale):
    ms = jnp.mean(x * x, axis=-1, keepdims=True)
    xn = x * lax.rsqrt(ms + EPS)
    return (xn * gain) * (1.0 + scale) + shift


def _head_mean_sq(y):
    cols = y.shape[1]
    lane = lax.broadcasted_iota(jnp.int32, (1, LANES), 1)
    first = lane < HEAD_DIM
    outs = []
    for p in range(cols // LANES):
        s = y[:, p * LANES:(p + 1) * LANES]
        s = s * s
        tot = jnp.sum(s, axis=-1, keepdims=True)
        lo = jnp.sum(jnp.where(first, s, 0.0), axis=-1, keepdims=True)
        outs.append(jnp.where(first, lo, tot - lo) * (1.0 / HEAD_DIM))
    return outs[0] if len(outs) == 1 else jnp.concatenate(outs, axis=-1)


def _log_sigmoid(x):
    return jnp.minimum(x, 0.0) - jnp.log1p(jnp.exp(-jnp.abs(x)))


def _dot(a, b):
    return jnp.dot(a, b, preferred_element_type=F32)


def _dot_nt(a, b):
    return lax.dot_general(a, b, (((1,), (1,)), ((), ())), preferred_element_type=F32)


def _proj_kernel(xp_ref, xs_ref, mod_ref, gain_ref, w_ref, wg_ref, bg_ref, qkg_ref,
                 cos_ref, sin_a_ref, sin_b_ref,
                 gq_ref, gk_ref, gv_ref, sgg_ref, gf_ref, gb_ref, sq_ref, sk_ref, sv_ref, nk_ref, nv_ref,
                 wb_ref, *, n_prompt_tiles, seqs_per_tile, seq_len):
    i = pl.program_id(0)

    @pl.when(i == 0)
    def _():
        blk = 32

        def conv(r, carry):
            rows = pl.ds(pl.multiple_of(r * blk, blk), blk)
            wb_ref[rows, :] = w_ref[rows, :].astype(BF16)
            return carry

        lax.fori_loop(0, D_IN // blk, conv, 0)

    def body(x_ref, rope, emit_cache):
        h = _adaln(x_ref[...], gain_ref[...], mod_ref[0, 0:1, :], mod_ref[0, 1:2, :]).astype(BF16)

        gq_ref[...] = (_dot_nt(h, wb_ref[R_GQ:R_GK, :]) * (HEAD_DIM ** -0.5)).astype(BF16)
        gk_ref[...] = _dot_nt(h, wb_ref[R_GK:R_GV, :]).astype(BF16)
        gv_ref[...] = _dot_nt(h, wb_ref[R_GV:R_GG, :]).astype(BF16)
        gg = _dot_nt(h, wb_ref[R_GG:R_LR, :])
        sgg_ref[...] = (gg * jax.nn.sigmoid(gg)).astype(BF16)

        lr = _dot_nt(h, wb_ref[R_LR:R_LR + LANES, :]).astype(BF16)
        gate = _log_sigmoid(_dot(lr, wg_ref[...]) + bg_ref[...]) * (1.0 / GATE_TEMP)
        gf_ref[...] = gate[:, :GLA_WIDTH]
        gb_ref[...] = gate[:, GLA_WIDTH:]

        tail = _dot_nt(h, wb_ref[R_SQ:D_IN, :])
        qk = tail[:, :R_SV - R_SQ]
        qk = qk * lax.rsqrt(_head_mean_sq(qk) + EPS) * qkg_ref[...]
        if rope:
            reps = (SWA_WIDTH + KV_WIDTH) // LANES
            cos = jnp.concatenate([cos_ref[...]] * reps, axis=-1)
            sin_a = jnp.concatenate([sin_a_ref[...]] * reps, axis=-1)
            sin_b = jnp.concatenate([sin_b_ref[...]] * reps, axis=-1)
            width = qk.shape[1]
            quarter = HEAD_DIM // 4
            qk = (qk * cos + pltpu.roll(qk, quarter, 1) * sin_a
                  + pltpu.roll(qk, width - quarter, 1) * sin_b)
        sq_ref[...] = (qk[:, :SWA_WIDTH] * (HEAD_DIM ** -0.5)).astype(BF16)
        sk = qk[:, SWA_WIDTH:]
        sk_ref[...] = sk.astype(BF16)
        sv = tail[:, R_SV - R_SQ:]
        sv_ref[...] = sv.astype(BF16)
        if emit_cache:
            sk_t, sv_t = sk.T, sv.T
            for s in range(seqs_per_tile):
                cols = slice(s * seq_len, (s + 1) * seq_len)
                for g in range(SWA_KV_HEADS):
                    feat = slice(g * HEAD_DIM, (g + 1) * HEAD_DIM)
                    nk_ref[s, 0, g, :, :] = sk_t[feat, cols]
                    nv_ref[s, 0, g, :, :] = sv_t[feat, cols]

    @pl.when(i < n_prompt_tiles)
    def _():
        body(xp_ref, rope=False, emit_cache=True)

    @pl.when(i >= n_prompt_tiles)
    def _():
        body(xs_ref, rope=True, emit_cache=False)


def _project(xp, xs, mod3, gain, w_in, wg, bg, qkg, rope_tabs, seq_p, seq_s):
    n_p, n_s = xp.shape[0], xs.shape[0]
    tp, ts = n_p // TM, n_s // TM
    n = n_p + n_s
    seqs_per_tile = TM // seq_p
    tiles_per_seq = seq_s // TM
    n_seq_p = n_p // seq_p
    const = lambda shape: pl.BlockSpec(shape, lambda i: (0,) * len(shape))
    mod_row = lambda i: jnp.where(i < tp, 0, 1 + (i - tp) // tiles_per_seq)
    rope_spec = pl.BlockSpec((TM, LANES), lambda i: (jnp.maximum(i - tp, 0) % tiles_per_seq, 0))
    in_specs = [pl.BlockSpec((TM, D_MODEL), lambda i: (jnp.minimum(i, tp - 1), 0)),
                pl.BlockSpec((TM, D_MODEL), lambda i: (jnp.maximum(i - tp, 0), 0)),
                pl.BlockSpec((1, 6, D_MODEL), lambda i: (mod_row(i), 0, 0)),
                const((1, D_MODEL)),
                pl.BlockSpec((D_IN, D_MODEL), lambda i: (0, 0), pipeline_mode=pl.Buffered(1)),
                const((LANES, 2 * GLA_WIDTH)), const((1, 2 * GLA_WIDTH)), const((1, SWA_WIDTH + KV_WIDTH)),
                rope_spec, rope_spec, rope_spec]
    row = lambda w: pl.BlockSpec((TM, w), lambda i: (i, 0))
    cache_spec = pl.BlockSpec((seqs_per_tile, 1, SWA_KV_HEADS, HEAD_DIM, seq_p),
                              lambda i: (jnp.minimum(i, tp - 1), 0, 0, 0, 0))
    out_specs = [row(GLA_WIDTH)] * 6 + [row(SWA_WIDTH), row(KV_WIDTH), row(KV_WIDTH)] + [cache_spec] * 2
    cache_shape = (n_seq_p, 1, SWA_KV_HEADS, HEAD_DIM, seq_p)
    out_shape = ([jax.ShapeDtypeStruct((n, GLA_WIDTH), BF16)] * 4
                 + [jax.ShapeDtypeStruct((n, GLA_WIDTH), F32)] * 2
                 + [jax.ShapeDtypeStruct((n, SWA_WIDTH), BF16),
                    jax.ShapeDtypeStruct((n, KV_WIDTH), BF16), jax.ShapeDtypeStruct((n, KV_WIDTH), BF16)]
                 + [jax.ShapeDtypeStruct(cache_shape, F32)] * 2)
    return pl.pallas_call(
        functools.partial(_proj_kernel, n_prompt_tiles=tp, seqs_per_tile=seqs_per_tile, seq_len=seq_p),
        grid=(tp + ts,), in_specs=in_specs, out_specs=out_specs, out_shape=out_shape,
        scratch_shapes=[pltpu.VMEM((D_IN, D_MODEL), BF16)],
        compiler_params=_cparams(1), name="adaln_in_proj",
    )(xp, xs, mod3, gain, w_in, wg, bg, qkg, *rope_tabs)


def _gla_kernel(*refs, seq_len, has_state, emit_state, n_cast):
    (q_ref, k_ref, v_ref, gf_ref, gb_ref, sgg_ref, norm_ref,
     trif_ref, trib_ref, keepf_ref, keepb_ref) = refs[:11]
    pos = 11
    if has_state:
        s0f_ref, s0b_ref = refs[pos:pos + 2]
        pos += 2
    cast_in = refs[pos:pos + n_cast]
    pos += n_cast
    og_ref = refs[pos]
    pos += 1
    if emit_state:
        sf_ref, sb_ref = refs[pos:pos + 2]
        pos += 2
    cast_out = refs[pos:pos + n_cast]
    pos += n_cast
    accf_ref, accb_ref = refs[pos:pos + 2]

    for src, dst in zip(cast_in, cast_out):
        dst[...] = src[...].astype(BF16)

    n_groups = seq_len // GROUP
    per_group = GROUP // CHUNK
    n_pairs = q_ref.shape[1] // LANES
    lane = lax.broadcasted_iota(jnp.int32, (1, LANES), 1)
    head0 = lane < HEAD_DIM
    m0 = jnp.where(head0, 1.0, 0.0).astype(BF16)
    m1 = jnp.where(head0, 0.0, 1.0).astype(BF16)
    row_top = lax.broadcasted_iota(jnp.int32, (LANES, 1), 0) < HEAD_DIM
    same_head = row_top == head0

    def stack_heads(x):
        return jnp.concatenate([x * m0, x * m1], axis=0)

    def direction(row0, lanes, g_ref, tri_ref, keep_ref, state, acc_ref, backward):
        rows = pl.ds(row0, GROUP)
        g = g_ref[rows, lanes]
        g_hi = g.astype(BF16)
        g_lo = (g - g_hi.astype(F32)).astype(BF16)
        r = _dot(tri_ref[...], jnp.concatenate([g_hi, g_lo], axis=-1))
        b = (r[:, :LANES] + r[:, LANES:]).reshape(per_group, CHUNK, LANES)
        edge = 0 if backward else CHUNK - 1
        mid = CHUNK // 2
        b_edge = b[:, edge:edge + 1, :]
        b_mid = b[:, mid:mid + 1, :]
        q = q_ref[rows, lanes].astype(F32).reshape(per_group, CHUNK, LANES)
        k = k_ref[rows, lanes].astype(F32).reshape(per_group, CHUNK, LANES)
        v = v_ref[rows, lanes]
        q_in = (q * jnp.exp(b - b_mid)).astype(BF16)
        k_in = (k * jnp.exp(b_mid - b)).astype(BF16).reshape(GROUP, LANES)
        q_st = (q * jnp.exp(b)).astype(BF16)
        k_st = (k * jnp.exp(b_edge - b)).astype(BF16).reshape(GROUP, LANES)
        decay = jnp.exp(b_edge)
        keep = keep_ref[...]
        pair_rows = 2 * CHUNK
        pair_order = range(per_group // 2 - 1, -1, -1) if backward else range(per_group // 2)
        for p in pair_order:
            c0 = 2 * p
            prow = slice(p * pair_rows, (p + 1) * pair_rows)
            q_pair = jnp.concatenate([stack_heads(q_in[c0]), stack_heads(q_in[c0 + 1])], axis=0)
            a = _dot_nt(q_pair, k_in[prow, :]).astype(BF16) * keep
            v_pair = v[prow, :]
            intra = _dot(a, v_pair)
            v_t = v_pair.astype(F32).T.astype(BF16)
            k_pair = k_st[prow, :]
            zero = jnp.zeros_like(k_pair)
            k_blk = jnp.concatenate([jnp.where(row_top, k_pair, zero), jnp.where(row_top, zero, k_pair)], axis=-1)
            kv_t = _dot(v_t, k_blk)
            for j in ((1, 0) if backward else (0, 1)):
                c = c0 + j
                inter = _dot_nt(stack_heads(q_st[c]), state.astype(BF16))
                tot = intra[j * pair_rows:(j + 1) * pair_rows, :] + inter
                acc_ref[pl.ds(row0 + c * CHUNK, CHUNK), lanes] = jnp.where(head0, tot[:CHUNK, :], tot[CHUNK:, :])
                state = state * decay[c] + jnp.where(same_head, kv_t[:, j * LANES:(j + 1) * LANES], 0.0)
        return state

    def load_state(s_ref, p):
        z = jnp.zeros((HEAD_DIM, HEAD_DIM), F32)
        top = jnp.concatenate([s_ref[0, 0, 2 * p, :, :].T, z], axis=-1)
        bot = jnp.concatenate([z, s_ref[0, 0, 2 * p + 1, :, :].T], axis=-1)
        return jnp.concatenate([top, bot], axis=0)

    if has_state:
        init = tuple(load_state(s_ref, p) for s_ref in (s0f_ref, s0b_ref) for p in range(n_pairs))
    else:
        init = (jnp.zeros((LANES, LANES), F32),) * (2 * n_pairs)

    def body(t, carry):
        row_f = t * GROUP
        row_b = (n_groups - 1 - t) * GROUP
        if n_groups > 1:
            row_f = pl.multiple_of(row_f, GROUP)
            row_b = pl.multiple_of(row_b, GROUP)
        new_f, new_b = [], []
        for p in range(n_pairs):
            lanes = slice(p * LANES, (p + 1) * LANES)
            new_f.append(direction(row_f, lanes, gf_ref, trif_ref, keepf_ref, carry[p], accf_ref, False))
            new_b.append(direction(row_b, lanes, gb_ref, trib_ref, keepb_ref, carry[n_pairs + p], accb_ref, True))
        return tuple(new_f + new_b)

    if n_groups == 1:
        final = body(0, init)
    else:
        final = lax.fori_loop(0, n_groups, body, init)
    if emit_state:
        for s_ref, fins in ((sf_ref, final[:n_pairs]), (sb_ref, final[n_pairs:])):
            for p, fin in enumerate(fins):
                s_ref[0, 0, 2 * p, :, :] = fin[:HEAD_DIM, :HEAD_DIM].T
                s_ref[0, 0, 2 * p + 1, :, :] = fin[HEAD_DIM:, HEAD_DIM:].T

    o = accf_ref[...] + accb_ref[...]
    o = o * lax.rsqrt(_head_mean_sq(o) + EPS) * norm_ref[...]
    og_ref[...] = (o * sgg_ref[...].astype(F32)).astype(BF16)


def _gla_constants():
    i = np.arange(GROUP)[:, None]
    j = np.arange(GROUP)[None, :]
    same = (i // CHUNK) == (j // CHUNK)
    tri_f = same & (j <= i)
    tri_b = same & (j >= i)
    r = np.arange(4 * CHUNK)[:, None]
    l = np.arange(LANES)[None, :]
    same = (r // (2 * CHUNK)) == (l // CHUNK)
    keep_f = same & ((l % CHUNK) <= (r % CHUNK))
    keep_b = same & ((l % CHUNK) >= (r % CHUNK))
    return tuple(jnp.asarray(m.astype(np.float32), dtype=BF16) for m in (tri_f, tri_b, keep_f, keep_b))


def _gla(gq, gk, gv, gf, gb, sgg, norm_row, s0f, s0b, n_seq, seq_len, row_block0, emit_state, cast_weights=()):
    has_state = s0f is not None
    steps = n_seq
    blk = pl.BlockSpec((seq_len, GLA_WIDTH), lambda b: (row_block0 + b, 0))
    out_blk = pl.BlockSpec((seq_len, GLA_WIDTH), lambda b: (b, 0))
    st_spec = pl.BlockSpec((1, 1, GLA_HEADS, HEAD_DIM, HEAD_DIM), lambda b: (b, 0, 0, 0, 0))
    const = lambda shape: pl.BlockSpec(shape, lambda b: (0, 0))
    in_specs = ([blk] * 6 + [const((1, GLA_WIDTH))] + [const((GROUP, GROUP))] * 2
                + [const((4 * CHUNK, LANES))] * 2)
    args = [gq, gk, gv, gf, gb, sgg, norm_row, *_gla_constants()]
    if has_state:
        in_specs += [st_spec] * 2
        args += [s0f, s0b]
    out_specs = [out_blk]
    out_shape = [jax.ShapeDtypeStruct((n_seq * seq_len, GLA_WIDTH), BF16)]
    if emit_state:
        out_specs += [st_spec] * 2
        out_shape += [jax.ShapeDtypeStruct((n_seq, 1, GLA_HEADS, HEAD_DIM, HEAD_DIM), F32)] * 2
    for w in cast_weights:
        rows = w.shape[0] // steps
        spec = pl.BlockSpec((rows, w.shape[1]), lambda b: (b, 0))
        in_specs.append(spec)
        args.append(w)
        out_specs.append(spec)
        out_shape.append(jax.ShapeDtypeStruct(w.shape, BF16))
    return pl.pallas_call(
        functools.partial(_gla_kernel, seq_len=seq_len, has_state=has_state, emit_state=emit_state,
                          n_cast=len(cast_weights)),
        grid=(n_seq,), in_specs=in_specs, out_specs=out_specs, out_shape=out_shape,
        scratch_shapes=[pltpu.VMEM((seq_len, GLA_WIDTH), F32)] * 2,
        compiler_params=_cparams(1), name="gla_bidirectional",
    )(*args)


def _softmax_parts(scores, sink):
    m = sink
    for s in scores:
        m = jnp.maximum(m, jnp.max(s, axis=-1, keepdims=True))
    es = [jnp.exp(s - m) for s in scores]
    denom = jnp.exp(sink - m)
    for e in es:
        denom = denom + jnp.sum(e, axis=-1, keepdims=True)
    return es, denom


def _ctx_attn_kernel(sink_ref, q_ref, k_ref, v_ref, o_ref):
    q = q_ref[...]
    k = k_ref[...]
    v = v_ref[...]
    outs = []
    for h in range(SWA_HEADS):
        g = h // SWA_GROUP
        kv = slice(g * HEAD_DIM, (g + 1) * HEAD_DIM)
        s = _dot_nt(q[:, h * HEAD_DIM:(h + 1) * HEAD_DIM], k[:, kv])
        (e,), denom = _softmax_parts([s], sink_ref[h])
        outs.append(_dot(e.astype(BF16), v[:, kv]) / denom)
    o_ref[...] = jnp.concatenate(outs, axis=-1).astype(BF16)


def _context_attention(sink, sq, sk, sv, n_seq, seq_len):
    return pl.pallas_call(
        _ctx_attn_kernel,
        grid=(n_seq,),
        in_specs=[pl.BlockSpec(memory_space=pltpu.SMEM),
                  pl.BlockSpec((seq_len, SWA_WIDTH), lambda b: (b, 0)),
                  pl.BlockSpec((seq_len, KV_WIDTH), lambda b: (b, 0)),
                  pl.BlockSpec((seq_len, KV_WIDTH), lambda b: (b, 0))],
        out_specs=pl.BlockSpec((seq_len, SWA_WIDTH), lambda b: (b, 0)),
        out_shape=jax.ShapeDtypeStruct((n_seq * seq_len, SWA_WIDTH), BF16),
        compiler_params=_cparams(1), name="context_attention",
    )(sink, sq, sk, sv)


def _lat_attn_kernel(sink_ref, q_ref, k_ref, v_ref, kc_ref, vc_ref, o_ref, *, seq_len):
    span = QBLOCK + 2 * WINDOW
    i = pl.program_id(1)
    start = jnp.clip(i * QBLOCK - WINDOW, 0, seq_len - span)
    start = pl.multiple_of(start, QBLOCK)
    q = q_ref[...]
    k = k_ref[pl.ds(start, span), :]
    v = v_ref[pl.ds(start, span), :]
    qpos = i * QBLOCK + lax.broadcasted_iota(jnp.int32, (QBLOCK, span), 0)
    kpos = start + lax.broadcasted_iota(jnp.int32, (QBLOCK, span), 1)
    valid = jnp.abs(qpos - kpos) <= WINDOW
    outs = []
    for h in range(SWA_HEADS):
        g = h // SWA_GROUP
        kv = slice(g * HEAD_DIM, (g + 1) * HEAD_DIM)
        qh = q[:, h * HEAD_DIM:(h + 1) * HEAD_DIM]
        s_loc = jnp.where(valid, _dot_nt(qh, k[:, kv]), NEG_INF)
        kc_t = kc_ref[0, 0, g, :, :].astype(BF16)
        vc_t = vc_ref[0, 0, g, :, :].astype(BF16)
        s_ctx = _dot(qh, kc_t)
        (e_loc, e_ctx), denom = _softmax_parts([s_loc, s_ctx], sink_ref[h])
        o = _dot(e_loc.astype(BF16), v[:, kv]) + _dot_nt(e_ctx.astype(BF16), vc_t)
        outs.append(o / denom)
    o_ref[...] = jnp.concatenate(outs, axis=-1).astype(BF16)


def _latent_attention(sink, sq, sk, sv, cache_k, cache_v, n_seq, seq_len, row0):
    nq = seq_len // QBLOCK
    q0 = row0 // QBLOCK
    s0 = row0 // seq_len
    past = cache_k.shape[4]
    cache_spec = pl.BlockSpec((1, 1, SWA_KV_HEADS, HEAD_DIM, past), lambda b, i: (b, 0, 0, 0, 0))
    return pl.pallas_call(
        functools.partial(_lat_attn_kernel, seq_len=seq_len),
        grid=(n_seq, nq),
        in_specs=[pl.BlockSpec(memory_space=pltpu.SMEM),
                  pl.BlockSpec((QBLOCK, SWA_WIDTH), lambda b, i: (q0 + b * nq + i, 0)),
                  pl.BlockSpec((seq_len, KV_WIDTH), lambda b, i: (s0 + b, 0)),
                  pl.BlockSpec((seq_len, KV_WIDTH), lambda b, i: (s0 + b, 0)),
                  cache_spec, cache_spec],
        out_specs=pl.BlockSpec((QBLOCK, SWA_WIDTH), lambda b, i: (b * nq + i, 0)),
        out_shape=jax.ShapeDtypeStruct((n_seq * seq_len, SWA_WIDTH), BF16),
        compiler_params=_cparams(2), name="latent_attention",
    )(sink, sq, sk, sv, cache_k, cache_v)


def _out_kernel(xp_ref, xs_ref, ogp_ref, ogs_ref, oswp_ref, osws_ref, mod_ref, gain_ref,
                wo_ref, w1_ref, w2_ref, yp_ref, ys_ref, *, n_prompt_tiles):
    def body(x_ref, og_ref, osw_ref, y_ref):
        mix = _dot(og_ref[...], wo_ref[:GLA_WIDTH, :]) + _dot(osw_ref[...], wo_ref[GLA_WIDTH:, :])
        x1 = x_ref[...] + mod_ref[0, 2:3, :] * mix
        h = _adaln(x1, gain_ref[...], mod_ref[0, 3:4, :], mod_ref[0, 4:5, :]).astype(BF16)
        ff_tile = 1024
        acc = None
        for j in range(D_FF // ff_tile):
            cols = slice(j * ff_tile, (j + 1) * ff_tile)
            a = jnp.maximum(_dot(h, w1_ref[:, cols]), 0.0)
            part = _dot((a * a).astype(BF16), w2_ref[cols, :])
            acc = part if acc is None else acc + part
        y_ref[...] = x1 + mod_ref[0, 5:6, :] * acc

    i = pl.program_id(0)

    @pl.when(i < n_prompt_tiles)
    def _():
        body(xp_ref, ogp_ref, oswp_ref, yp_ref)

    @pl.when(i >= n_prompt_tiles)
    def _():
        body(xs_ref, ogs_ref, osws_ref, ys_ref)


def _mix_out(xp, xs, og_p, og_s, osw_p, osw_s, mod3, gain, wo, w1, w2, seq_s):
    n_p, n_s = xp.shape[0], xs.shape[0]
    tp, ts = n_p // TM, n_s // TM
    tiles_per_seq = seq_s // TM
    const = lambda shape: pl.BlockSpec(shape, lambda i: (0,) * len(shape), pipeline_mode=pl.Buffered(1))
    row_p = lambda w: pl.BlockSpec((TM, w), lambda i: (jnp.minimum(i, tp - 1), 0))
    row_s = lambda w: pl.BlockSpec((TM, w), lambda i: (jnp.maximum(i - tp, 0), 0))
    mod_row = lambda i: jnp.where(i < tp, 0, 1 + (i - tp) // tiles_per_seq)
    return pl.pallas_call(
        functools.partial(_out_kernel, n_prompt_tiles=tp),
        grid=(tp + ts,),
        in_specs=[row_p(D_MODEL), row_s(D_MODEL), row_p(GLA_WIDTH), row_s(GLA_WIDTH),
                  row_p(SWA_WIDTH), row_s(SWA_WIDTH),
                  pl.BlockSpec((1, 6, D_MODEL), lambda i: (mod_row(i), 0, 0)),
                  const((1, D_MODEL)), const((GLA_WIDTH + SWA_WIDTH, D_MODEL)),
                  const((D_MODEL, D_FF)), const((D_FF, D_MODEL))],
        out_specs=[row_p(D_MODEL), row_s(D_MODEL)],
        out_shape=[jax.ShapeDtypeStruct((n_p, D_MODEL), F32), jax.ShapeDtypeStruct((n_s, D_MODEL), F32)],
        compiler_params=_cparams(1), name="out_proj_mlp",
    )(xp, xs, og_p, og_s, osw_p, osw_s, mod3, gain, wo, w1, w2)


def _rope_tables(seq_len):
    axis_dim = HEAD_DIM // 2
    half = axis_dim // 2
    t = np.arange(seq_len)
    rowp = (t // GRID_W).astype(np.float64)
    colp = (t % GRID_W).astype(np.float64)
    freqs = ROPE_THETA ** (-np.arange(half, dtype=np.float64) / half)
    lane = np.arange(LANES)
    d = lane % HEAD_DIM
    posv = np.where((d // axis_dim)[None, :] == 0, rowp[:, None], colp[:, None])
    ang = posv * freqs[d % half][None, :]
    second = ((d % axis_dim) >= half)[None, :]
    cos, sin = np.cos(ang), np.sin(ang)
    zero = np.zeros_like(sin)
    tabs = (cos, np.where(second, sin, zero), np.where(second, zero, -sin))
    return tuple(jnp.asarray(tab.astype(np.float32)) for tab in tabs)


def kernel(x_prompt, x_sample, c, cache_k, cache_v, state_gla_fwd, state_gla_bwd, c_ctx, w_ada, b_ada, norm_mix, norm_ff, w_in, w_gk_fwd, b_gk_fwd, w_gk_bwd, b_gk_bwd, gla_norm, q_norm, k_norm, sink, w_o, w_ff1, w_ff2):
    depth = w_in.shape[0]
    assert depth == 1, "single trunk layer"
    bp, tp, _ = x_prompt.shape
    bs, ts, _ = x_sample.shape
    n_p = bp * tp
    l = 0

    cond = jnp.concatenate([c_ctx[None, :], c], axis=0)
    mod3 = _modulation(cond, w_ada[l], b_ada[l]).reshape(8, 6, D_MODEL)

    wg = jnp.zeros((LANES, 2 * GLA_WIDTH), F32)
    wg = wg.at[:GATE_RANK, :GLA_WIDTH].set(w_gk_fwd[l]).at[GATE_RANK:2 * GATE_RANK, GLA_WIDTH:].set(w_gk_bwd[l])
    wg = wg.astype(BF16)
    bg = jnp.concatenate([b_gk_fwd[l], b_gk_bwd[l]])[None, :]
    qkg = jnp.concatenate([jnp.tile(q_norm[l], SWA_HEADS), jnp.tile(k_norm[l], SWA_KV_HEADS)])[None, :]
    norm2 = jnp.tile(gla_norm[l], GLA_HEADS)[None, :]
    gain_mix = norm_mix[l][None, :]
    gain_ff = norm_ff[l][None, :]
    sink_l = sink[l]

    w_in_t = jnp.transpose(w_in[l])
    cache_k_t = jnp.swapaxes(cache_k, 3, 4)
    cache_v_t = jnp.swapaxes(cache_v, 3, 4)

    xp = x_prompt.reshape(n_p, D_MODEL)
    xs = x_sample.reshape(bs * ts, D_MODEL)
    (gq, gk, gv, sgg, gf, gb, sq, sk, sv, new_k_t, new_v_t) = _project(
        xp, xs, mod3, gain_mix, w_in_t, wg, bg, qkg, _rope_tables(ts), tp, ts)
    new_k = jnp.swapaxes(new_k_t, 3, 4)
    new_v = jnp.swapaxes(new_v_t, 3, 4)

    og_p, s_f, s_b, wo, w1, w2 = _gla(gq, gk, gv, gf, gb, sgg, norm2, None, None, bp, tp, 0, True,
                                      cast_weights=(w_o[l], w_ff1[l], w_ff2[l]))
    osw_p = _context_attention(sink_l, sq, sk, sv, bp, tp)
    (og_s,) = _gla(gq, gk, gv, gf, gb, sgg, norm2, state_gla_fwd, state_gla_bwd, bs, ts, n_p // ts, False)
    osw_s = _latent_attention(sink_l, sq, sk, sv, cache_k_t, cache_v_t, bs, ts, n_p)

    yp, ys = _mix_out(xp, xs, og_p, og_s, osw_p, osw_s, mod3, gain_ff, wo, w1, w2, ts)
    return (yp.reshape(bp, tp, D_MODEL), ys.reshape(bs, ts, D_MODEL), new_k, new_v, s_f, s_b)
```

```python
import functools

import numpy as np
import jax
import jax.numpy as jnp
from jax import lax
from jax.experimental import pallas as pl
from jax.experimental.pallas import tpu as pltpu

F32 = jnp.float32
BF16 = jnp.bfloat16

D_MODEL = 1024
GRID_W = 64
HEAD_DIM = 64
GLA_HEADS = 8
GLA_WIDTH = GLA_HEADS * HEAD_DIM
GATE_RANK = 16
GATE_TEMP = 16.0
CHUNK = 64
GROUP = 4 * CHUNK
SWA_HEADS = 8
SWA_KV_HEADS = 2
SWA_GROUP = SWA_HEADS // SWA_KV_HEADS
SWA_WIDTH = SWA_HEADS * HEAD_DIM
KV_WIDTH = SWA_KV_HEADS * HEAD_DIM
WINDOW = 128
QBLOCK = 128
D_FF = 4 * D_MODEL
ROPE_THETA = 10000.0
EPS = 1e-6
NEG_INF = -1e30
LANES = 128
TM = 512
CTX_HEAD_STACK = 1
LAT_HEAD_STACK = 4

D_IN = 2848
R_GQ, R_GK, R_GV, R_GG, R_LR = 0, 512, 1024, 1536, 2048
R_SQ = R_LR + 2 * GATE_RANK
R_SV = R_SQ + SWA_WIDTH + KV_WIDTH

VMEM_LIMIT = 56 * 1024 * 1024


def _cparams(n_axes):
    return pltpu.CompilerParams(dimension_semantics=("arbitrary",) * n_axes,
                                vmem_limit_bytes=VMEM_LIMIT)


def _mod_kernel(condt_ref, w_ref, b_ref, o_ref, *, n_rows):
    k = pl.program_id(0)
    w = w_ref[...]
    rows = []
    for r in range(n_rows):
        cnd = condt_ref[:, r:r + 1]
        s = cnd * jax.nn.sigmoid(cnd)
        rows.append(jnp.sum(w * s, axis=0, keepdims=True))
    rows.append(jnp.zeros((8 - n_rows, w.shape[1]), F32))
    part = jnp.concatenate(rows, axis=0)

    @pl.when(k == 0)
    def _():
        o_ref[...] = part + b_ref[...]

    @pl.when(k > 0)
    def _():
        o_ref[...] += part


def _modulation(cond, w_ada, b_ada):
    n_rows = cond.shape[0]
    condt = jnp.zeros((D_MODEL, 8), F32).at[:, :n_rows].set(cond.T)
    tk = 128
    n_out = w_ada.shape[1]
    return pl.pallas_call(
        functools.partial(_mod_kernel, n_rows=n_rows),
        grid=(D_MODEL // tk,),
        in_specs=[pl.BlockSpec((tk, 8), lambda k: (k, 0)),
                  pl.BlockSpec((tk, n_out), lambda k: (k, 0)),
                  pl.BlockSpec((1, n_out), lambda k: (0, 0))],
        out_specs=pl.BlockSpec((8, n_out), lambda k: (0, 0)),
        out_shape=jax.ShapeDtypeStruct((8, n_out), F32),
        compiler_params=_cparams(1),
        name="modulation",
    )(condt, w_ada, b_ada.reshape(1, n_out))


def _adaln(x, gain, shift, scale):
    ms = jnp.mean(x * x, axis=-1, keepdims=True)
    xn = x * lax.rsqrt(ms + EPS)
    return (xn * gain) * (1.0 + scale) + shift


def _head_mean_sq(y):
    cols = y.shape[1]
    lane = lax.broadcasted_iota(jnp.int32, (1, LANES), 1)
    first = lane < HEAD_DIM
    outs = []
    for p in range(cols // LANES):
        s = y[:, p * LANES:(p + 1) * LANES]
        s = s * s
        tot = jnp.sum(s, axis=-1, keepdims=True)
        lo = jnp.sum(jnp.where(first, s, 0.0), axis=-1, keepdims=True)
        outs.append(jnp.where(first, lo, tot - lo) * (1.0 / HEAD_DIM))
    return outs[0] if len(outs) == 1 else jnp.concatenate(outs, axis=-1)


def _log_sigmoid(x):
    return jnp.minimum(x, 0.0) - jnp.log1p(jnp.exp(-jnp.abs(x)))


def _dot(a, b):
    return jnp.dot(a, b, preferred_element_type=F32)


def _dot_nt(a, b):
    return lax.dot_general(a, b, (((1,), (1,)), ((), ())), preferred_element_type=F32)


def _proj_kernel(xp_ref, xs_ref, mod_ref, gain_ref, w_ref, wg_ref, bg_ref, qkg_ref,
                 cos_ref, sin_a_ref, sin_b_ref,
                 gq_ref, gk_ref, gv_ref, sgg_ref, gf_ref, gb_ref, sq_ref, sk_ref, sv_ref, nk_ref, nv_ref,
                 wb_ref, *, n_prompt_tiles, seqs_per_tile, seq_len):
    i = pl.program_id(0)

    @pl.when(i == 0)
    def _():
        blk = 32

        def conv(r, carry):
            rows = pl.ds(pl.multiple_of(r * blk, blk), blk)
            wb_ref[rows, :] = w_ref[rows, :].astype(BF16)
            return carry

        lax.fori_loop(0, D_IN // blk, conv, 0)

    def body(x_ref, rope, emit_cache):
        h = _adaln(x_ref[...], gain_ref[...], mod_ref[0, 0:1, :], mod_ref[0, 1:2, :]).astype(BF16)

        gq_ref[...] = (_dot_nt(h, wb_ref[R_GQ:R_GK, :]) * (HEAD_DIM ** -0.5)).astype(BF16)
        gk_ref[...] = _dot_nt(h, wb_ref[R_GK:R_GV, :]).astype(BF16)
        gv_ref[...] = _dot_nt(h, wb_ref[R_GV:R_GG, :]).astype(BF16)
        gg = _dot_nt(h, wb_ref[R_GG:R_LR, :])
        sgg_ref[...] = (gg * jax.nn.sigmoid(gg)).astype(BF16)

        lr = _dot_nt(h, wb_ref[R_LR:R_LR + LANES, :]).astype(BF16)
        gate = _log_sigmoid(_dot(lr, wg_ref[...]) + bg_ref[...]) * (1.0 / GATE_TEMP)
        gf_ref[...] = gate[:, :GLA_WIDTH]
        gb_ref[...] = gate[:, GLA_WIDTH:]

        tail = _dot_nt(h, wb_ref[R_SQ:D_IN, :])
        qk = tail[:, :R_SV - R_SQ]
        qk = qk * lax.rsqrt(_head_mean_sq(qk) + EPS) * qkg_ref[...]
        if rope:
            reps = (SWA_WIDTH + KV_WIDTH) // LANES
            cos = jnp.concatenate([cos_ref[...]] * reps, axis=-1)
            sin_a = jnp.concatenate([sin_a_ref[...]] * reps, axis=-1)
            sin_b = jnp.concatenate([sin_b_ref[...]] * reps, axis=-1)
            width = qk.shape[1]
            quarter = HEAD_DIM // 4
            qk = (qk * cos + pltpu.roll(qk, quarter, 1) * sin_a
                  + pltpu.roll(qk, width - quarter, 1) * sin_b)
        sq_ref[...] = (qk[:, :SWA_WIDTH] * (HEAD_DIM ** -0.5)).astype(BF16)
        sk = qk[:, SWA_WIDTH:]
        sk_ref[...] = sk.astype(BF16)
        sv = tail[:, R_SV - R_SQ:]
        sv_ref[...] = sv.astype(BF16)
        if emit_cache:
            sk_t, sv_t = sk.T, sv.T
            for s in range(seqs_per_tile):
                cols = slice(s * seq_len, (s + 1) * seq_len)
                for g in range(SWA_KV_HEADS):
                    feat = slice(g * HEAD_DIM, (g + 1) * HEAD_DIM)
                    nk_ref[s, 0, g, :, :] = sk_t[feat, cols]
                    nv_ref[s, 0, g, :, :] = sv_t[feat, cols]

    @pl.when(i < n_prompt_tiles)
    def _():
        body(xp_ref, rope=False, emit_cache=True)

    @pl.when(i >= n_prompt_tiles)
    def _():
        body(xs_ref, rope=True, emit_cache=False)


def _project(xp, xs, mod3, gain, w_in, wg, bg, qkg, rope_tabs, seq_p, seq_s):
    n_p, n_s = xp.shape[0], xs.shape[0]
    tp, ts = n_p // TM, n_s // TM
    n = n_p + n_s
    seqs_per_tile = TM // seq_p
    tiles_per_seq = seq_s // TM
    n_seq_p = n_p // seq_p
    const = lambda shape: pl.BlockSpec(shape, lambda i: (0,) * len(shape))
    mod_row = lambda i: jnp.where(i < tp, 0, 1 + (i - tp) // tiles_per_seq)
    rope_spec = pl.BlockSpec((TM, LANES), lambda i: (jnp.maximum(i - tp, 0) % tiles_per_seq, 0))
    in_specs = [pl.BlockSpec((TM, D_MODEL), lambda i: (jnp.minimum(i, tp - 1), 0)),
                pl.BlockSpec((TM, D_MODEL), lambda i: (jnp.maximum(i - tp, 0), 0)),
                pl.BlockSpec((1, 6, D_MODEL), lambda i: (mod_row(i), 0, 0)),
                const((1, D_MODEL)),
                pl.BlockSpec((D_IN, D_MODEL), lambda i: (0, 0), pipeline_mode=pl.Buffered(1)),
                const((LANES, 2 * GLA_WIDTH)), const((1, 2 * GLA_WIDTH)), const((1, SWA_WIDTH + KV_WIDTH)),
                rope_spec, rope_spec, rope_spec]
    row = lambda w: pl.BlockSpec((TM, w), lambda i: (i, 0))
    cache_spec = pl.BlockSpec((seqs_per_tile, 1, SWA_KV_HEADS, HEAD_DIM, seq_p),
                              lambda i: (jnp.minimum(i, tp - 1), 0, 0, 0, 0))
    out_specs = [row(GLA_WIDTH)] * 6 + [row(SWA_WIDTH), row(KV_WIDTH), row(KV_WIDTH)] + [cache_spec] * 2
    cache_shape = (n_seq_p, 1, SWA_KV_HEADS, HEAD_DIM, seq_p)
    out_shape = ([jax.ShapeDtypeStruct((n, GLA_WIDTH), BF16)] * 4
                 + [jax.ShapeDtypeStruct((n, GLA_WIDTH), F32)] * 2
                 + [jax.ShapeDtypeStruct((n, SWA_WIDTH), BF16),
                    jax.ShapeDtypeStruct((n, KV_WIDTH), BF16), jax.ShapeDtypeStruct((n, KV_WIDTH), BF16)]
                 + [jax.ShapeDtypeStruct(cache_shape, F32)] * 2)
    return pl.pallas_call(
        functools.partial(_proj_kernel, n_prompt_tiles=tp, seqs_per_tile=seqs_per_tile, seq_len=seq_p),
        grid=(tp + ts,), in_specs=in_specs, out_specs=out_specs, out_shape=out_shape,
        scratch_shapes=[pltpu.VMEM((D_IN, D_MODEL), BF16)],
        compiler_params=_cparams(1), name="adaln_in_proj",
    )(xp, xs, mod3, gain, w_in, wg, bg, qkg, *rope_tabs)


def _gla_kernel(*refs, seq_len, has_state, emit_state, n_cast):
    (q_ref, k_ref, v_ref, gf_ref, gb_ref, sgg_ref, norm_ref,
     trif_ref, trib_ref, keepf_ref, keepb_ref) = refs[:11]
    pos = 11
    if has_state:
        s0f_ref, s0b_ref = refs[pos:pos + 2]
        pos += 2
    cast_in = refs[pos:pos + n_cast]
    pos += n_cast
    og_ref = refs[pos]
    pos += 1
    if emit_state:
        sf_ref, sb_ref = refs[pos:pos + 2]
        pos += 2
    cast_out = refs[pos:pos + n_cast]
    pos += n_cast
    accf_ref, accb_ref = refs[pos:pos + 2]

    for src, dst in zip(cast_in, cast_out):
        dst[...] = src[...].astype(BF16)

    n_groups = seq_len // GROUP
    per_group = GROUP // CHUNK
    n_pairs = q_ref.shape[1] // LANES
    lane = lax.broadcasted_iota(jnp.int32, (1, LANES), 1)
    head0 = lane < HEAD_DIM
    m0 = jnp.where(head0, 1.0, 0.0).astype(BF16)
    m1 = jnp.where(head0, 0.0, 1.0).astype(BF16)
    row_top = lax.broadcasted_iota(jnp.int32, (LANES, 1), 0) < HEAD_DIM
    same_head = row_top == head0

    def stack_heads(x):
        return jnp.concatenate([x * m0, x * m1], axis=0)

    def direction(row0, lanes, g_ref, tri_ref, keep_ref, state, acc_ref, backward):
        rows = pl.ds(row0, GROUP)
        g = g_ref[rows, lanes]
        g_hi = g.astype(BF16)
        g_lo = (g - g_hi.astype(F32)).astype(BF16)
        r = _dot(tri_ref[...], jnp.concatenate([g_hi, g_lo], axis=-1))
        b = (r[:, :LANES] + r[:, LANES:]).reshape(per_group, CHUNK, LANES)
        edge = 0 if backward else CHUNK - 1
        mid = CHUNK // 2
        b_edge = b[:, edge:edge + 1, :]
        b_mid = b[:, mid:mid + 1, :]
        q = q_ref[rows, lanes].astype(F32).reshape(per_group, CHUNK, LANES)
        k = k_ref[rows, lanes].astype(F32).reshape(per_group, CHUNK, LANES)
        v = v_ref[rows, lanes]
        q_in = (q * jnp.exp(b - b_mid)).astype(BF16)
        k_in = (k * jnp.exp(b_mid - b)).astype(BF16).reshape(GROUP, LANES)
        q_st = (q * jnp.exp(b)).astype(BF16)
        k_st = (k * jnp.exp(b_edge - b)).astype(BF16).reshape(GROUP, LANES)
        decay = jnp.exp(b_edge)
        keep = keep_ref[...]
        pair_rows = 2 * CHUNK
        pair_order = range(per_group // 2 - 1, -1, -1) if backward else range(per_group // 2)
        for p in pair_order:
            c0 = 2 * p
            prow = slice(p * pair_rows, (p + 1) * pair_rows)
            q_pair = jnp.concatenate([stack_heads(q_in[c0]), stack_heads(q_in[c0 + 1])], axis=0)
            a = _dot_nt(q_pair, k_in[prow, :]).astype(BF16) * keep
            v_pair = v[prow, :]
            intra = _dot(a, v_pair)
            v_t = v_pair.astype(F32).T.astype(BF16)
            k_pair = k_st[prow, :]
            zero = jnp.zeros_like(k_pair)
            k_blk = jnp.concatenate([jnp.where(row_top, k_pair, zero), jnp.where(row_top, zero, k_pair)], axis=-1)
            kv_t = _dot(v_t, k_blk)
            for j in ((1, 0) if backward else (0, 1)):
                c = c0 + j
                inter = _dot_nt(stack_heads(q_st[c]), state.astype(BF16))
                tot = intra[j * pair_rows:(j + 1) * pair_rows, :] + inter
                acc_ref[pl.ds(row0 + c * CHUNK, CHUNK), lanes] = jnp.where(head0, tot[:CHUNK, :], tot[CHUNK:, :])
                state = state * decay[c] + jnp.where(same_head, kv_t[:, j * LANES:(j + 1) * LANES], 0.0)
        return state

    def load_state(s_ref, p):
        z = jnp.zeros((HEAD_DIM, HEAD_DIM), F32)
        top = jnp.concatenate([s_ref[0, 0, 2 * p, :, :].T, z], axis=-1)
        bot = jnp.concatenate([z, s_ref[0, 0, 2 * p + 1, :, :].T], axis=-1)
        return jnp.concatenate([top, bot], axis=0)

    if has_state:
        init = tuple(load_state(s_ref, p) for s_ref in (s0f_ref, s0b_ref) for p in range(n_pairs))
    else:
        init = (jnp.zeros((LANES, LANES), F32),) * (2 * n_pairs)

    def body(t, carry):
        row_f = t * GROUP
        row_b = (n_groups - 1 - t) * GROUP
        if n_groups > 1:
            row_f = pl.multiple_of(row_f, GROUP)
            row_b = pl.multiple_of(row_b, GROUP)
        new_f, new_b = [], []
        for p in range(n_pairs):
            lanes = slice(p * LANES, (p + 1) * LANES)
            new_f.append(direction(row_f, lanes, gf_ref, trif_ref, keepf_ref, carry[p], accf_ref, False))
            new_b.append(direction(row_b, lanes, gb_ref, trib_ref, keepb_ref, carry[n_pairs + p], accb_ref, True))
        return tuple(new_f + new_b)

    if n_groups == 1:
        final = body(0, init)
    else:
        final = lax.fori_loop(0, n_groups, body, init)
    if emit_state:
        for s_ref, fins in ((sf_ref, final[:n_pairs]), (sb_ref, final[n_pairs:])):
            for p, fin in enumerate(fins):
                s_ref[0, 0, 2 * p, :, :] = fin[:HEAD_DIM, :HEAD_DIM].T
                s_ref[0, 0, 2 * p + 1, :, :] = fin[HEAD_DIM:, HEAD_DIM:].T

    o = accf_ref[...] + accb_ref[...]
    o = o * lax.rsqrt(_head_mean_sq(o) + EPS) * norm_ref[...]
    og_ref[...] = (o * sgg_ref[...].astype(F32)).astype(BF16)


def _gla_constants():
    i = np.arange(GROUP)[:, None]
    j = np.arange(GROUP)[None, :]
    same = (i // CHUNK) == (j // CHUNK)
    tri_f = same & (j <= i)
    tri_b = same & (j >= i)
    r = np.arange(4 * CHUNK)[:, None]
    l = np.arange(LANES)[None, :]
    same = (r // (2 * CHUNK)) == (l // CHUNK)
    keep_f = same & ((l % CHUNK) <= (r % CHUNK))
    keep_b = same & ((l % CHUNK) >= (r % CHUNK))
    return tuple(jnp.asarray(m.astype(np.float32), dtype=BF16) for m in (tri_f, tri_b, keep_f, keep_b))


def _gla(gq, gk, gv, gf, gb, sgg, norm_row, s0f, s0b, n_seq, seq_len, row_block0, emit_state, cast_weights=()):
    has_state = s0f is not None
    steps = n_seq
    blk = pl.BlockSpec((seq_len, GLA_WIDTH), lambda b: (row_block0 + b, 0))
    out_blk = pl.BlockSpec((seq_len, GLA_WIDTH), lambda b: (b, 0))
    st_spec = pl.BlockSpec((1, 1, GLA_HEADS, HEAD_DIM, HEAD_DIM), lambda b: (b, 0, 0, 0, 0))
    const = lambda shape: pl.BlockSpec(shape, lambda b: (0, 0))
    in_specs = ([blk] * 6 + [const((1, GLA_WIDTH))] + [const((GROUP, GROUP))] * 2
                + [const((4 * CHUNK, LANES))] * 2)
    args = [gq, gk, gv, gf, gb, sgg, norm_row, *_gla_constants()]
    if has_state:
        in_specs += [st_spec] * 2
        args += [s0f, s0b]
    out_specs = [out_blk]
    out_shape = [jax.ShapeDtypeStruct((n_seq * seq_len, GLA_WIDTH), BF16)]
    if emit_state:
        out_specs += [st_spec] * 2
        out_shape += [jax.ShapeDtypeStruct((n_seq, 1, GLA_HEADS, HEAD_DIM, HEAD_DIM), F32)] * 2
    for w in cast_weights:
        rows = w.shape[0] // steps
        spec = pl.BlockSpec((rows, w.shape[1]), lambda b: (b, 0))
        in_specs.append(spec)
        args.append(w)
        out_specs.append(spec)
        out_shape.append(jax.ShapeDtypeStruct(w.shape, BF16))
    return pl.pallas_call(
        functools.partial(_gla_kernel, seq_len=seq_len, has_state=has_state, emit_state=emit_state,
                          n_cast=len(cast_weights)),
        grid=(n_seq,), in_specs=in_specs, out_specs=out_specs, out_shape=out_shape,
        scratch_shapes=[pltpu.VMEM((seq_len, GLA_WIDTH), F32)] * 2,
        compiler_params=_cparams(1), name="gla_bidirectional",
    )(*args)


def _group_attention(q, heads, sink_ref, score_fn, value_fn):
    rows = q.shape[0]
    q_stack = jnp.concatenate([q[:, h * HEAD_DIM:(h + 1) * HEAD_DIM] for h in heads], axis=0)
    scores = score_fn(q_stack)
    e_parts = [[] for _ in scores]
    denoms = []
    for j, h in enumerate(heads):
        slab = slice(j * rows, (j + 1) * rows)
        m = sink_ref[h]
        for s in scores:
            m = jnp.maximum(m, jnp.max(s[slab, :], axis=-1, keepdims=True))
        denom = jnp.exp(sink_ref[h] - m)
        for part, s in zip(e_parts, scores):
            e = jnp.exp(s[slab, :] - m)
            denom = denom + jnp.sum(e, axis=-1, keepdims=True)
            part.append(e.astype(BF16))
        denoms.append(denom)
    acc = value_fn([jnp.concatenate(part, axis=0) for part in e_parts])
    return [acc[j * rows:(j + 1) * rows, :] / denoms[j] for j in range(len(heads))]


def _ctx_attn_kernel(sink_ref, q_ref, k_ref, v_ref, o_ref):
    q = q_ref[...]
    k = k_ref[...]
    v = v_ref[...]
    outs = []
    for g in range(SWA_KV_HEADS):
        k_g = k[:, g * HEAD_DIM:(g + 1) * HEAD_DIM]
        v_g = v[:, g * HEAD_DIM:(g + 1) * HEAD_DIM]
        for h0 in range(SWA_GROUP * g, SWA_GROUP * (g + 1), CTX_HEAD_STACK):
            outs += _group_attention(
                q, list(range(h0, h0 + CTX_HEAD_STACK)), sink_ref,
                lambda qs: [_dot_nt(qs, k_g)],
                lambda es: _dot(es[0], v_g))
    o_ref[...] = jnp.concatenate(outs, axis=-1).astype(BF16)


def _context_attention(sink, sq, sk, sv, n_seq, seq_len):
    return pl.pallas_call(
        _ctx_attn_kernel,
        grid=(n_seq,),
        in_specs=[pl.BlockSpec(memory_space=pltpu.SMEM),
                  pl.BlockSpec((seq_len, SWA_WIDTH), lambda b: (b, 0)),
                  pl.BlockSpec((seq_len, KV_WIDTH), lambda b: (b, 0)),
                  pl.BlockSpec((seq_len, KV_WIDTH), lambda b: (b, 0))],
        out_specs=pl.BlockSpec((seq_len, SWA_WIDTH), lambda b: (b, 0)),
        out_shape=jax.ShapeDtypeStruct((n_seq * seq_len, SWA_WIDTH), BF16),
        compiler_params=_cparams(1), name="context_attention",
    )(sink, sq, sk, sv)


def _lat_attn_kernel(sink_ref, q_ref, k_ref, v_ref, kc_ref, vc_ref, o_ref, *, seq_len):
    span = QBLOCK + 2 * WINDOW
    i = pl.program_id(1)
    start = jnp.clip(i * QBLOCK - WINDOW, 0, seq_len - span)
    start = pl.multiple_of(start, QBLOCK)
    q = q_ref[...]
    k = k_ref[pl.ds(start, span), :]
    v = v_ref[pl.ds(start, span), :]
    qpos = i * QBLOCK + lax.broadcasted_iota(jnp.int32, (QBLOCK, span), 0)
    kpos = start + lax.broadcasted_iota(jnp.int32, (QBLOCK, span), 1)
    valid = jnp.abs(qpos - kpos) <= WINDOW
    valid = jnp.concatenate([valid] * LAT_HEAD_STACK, axis=0)
    outs = []
    for g in range(SWA_KV_HEADS):
        k_g = k[:, g * HEAD_DIM:(g + 1) * HEAD_DIM]
        v_g = v[:, g * HEAD_DIM:(g + 1) * HEAD_DIM]
        kc_t = kc_ref[0, 0, g, :, :].astype(BF16)
        vc_t = vc_ref[0, 0, g, :, :].astype(BF16)
        for h0 in range(SWA_GROUP * g, SWA_GROUP * (g + 1), LAT_HEAD_STACK):
            outs += _group_attention(
                q, list(range(h0, h0 + LAT_HEAD_STACK)), sink_ref,
                lambda qs: [jnp.where(valid, _dot_nt(qs, k_g), NEG_INF), _dot(qs, kc_t)],
                lambda es: _dot(es[0], v_g) + _dot_nt(es[1], vc_t))
    o_ref[...] = jnp.concatenate(outs, axis=-1).astype(BF16)


def _latent_attention(sink, sq, sk, sv, cache_k, cache_v, n_seq, seq_len, row0):
    nq = seq_len // QBLOCK
    q0 = row0 // QBLOCK
    s0 = row0 // seq_len
    past = cache_k.shape[4]
    cache_spec = pl.BlockSpec((1, 1, SWA_KV_HEADS, HEAD_DIM, past), lambda b, i: (b, 0, 0, 0, 0))
    return pl.pallas_call(
        functools.partial(_lat_attn_kernel, seq_len=seq_len),
        grid=(n_seq, nq),
        in_specs=[pl.BlockSpec(memory_space=pltpu.SMEM),
                  pl.BlockSpec((QBLOCK, SWA_WIDTH), lambda b, i: (q0 + b * nq + i, 0)),
                  pl.BlockSpec((seq_len, KV_WIDTH), lambda b, i: (s0 + b, 0)),
                  pl.BlockSpec((seq_len, KV_WIDTH), lambda b, i: (s0 + b, 0)),
                  cache_spec, cache_spec],
        out_specs=pl.BlockSpec((QBLOCK, SWA_WIDTH), lambda b, i: (b * nq + i, 0)),
        out_shape=jax.ShapeDtypeStruct((n_seq * seq_len, SWA_WIDTH), BF16),
        compiler_params=_cparams(2), name="latent_attention",
    )(sink, sq, sk, sv, cache_k, cache_v)


def _out_kernel(xp_ref, xs_ref, ogp_ref, ogs_ref, oswp_ref, osws_ref, mod_ref, gain_ref,
                wo_ref, w1_ref, w2_ref, yp_ref, ys_ref, *, n_prompt_tiles):
    def body(x_ref, og_ref, osw_ref, y_ref):
        mix = _dot(og_ref[...], wo_ref[:GLA_WIDTH, :]) + _dot(osw_ref[...], wo_ref[GLA_WIDTH:, :])
        x1 = x_ref[...] + mod_ref[0, 2:3, :] * mix
        h = _adaln(x1, gain_ref[...], mod_ref[0, 3:4, :], mod_ref[0, 4:5, :]).astype(BF16)
        ff_tile = 1024
        acc = None
        for j in range(D_FF // ff_tile):
            cols = slice(j * ff_tile, (j + 1) * ff_tile)
            a = jnp.maximum(_dot(h, w1_ref[:, cols]), 0.0)
            part = _dot((a * a).astype(BF16), w2_ref[cols, :])
            acc = part if acc is None else acc + part
        y_ref[...] = x1 + mod_ref[0, 5:6, :] * acc

    i = pl.program_id(0)

    @pl.when(i < n_prompt_tiles)
    def _():
        body(xp_ref, ogp_ref, oswp_ref, yp_ref)

    @pl.when(i >= n_prompt_tiles)
    def _():
        body(xs_ref, ogs_ref, osws_ref, ys_ref)


def _mix_out(xp, xs, og_p, og_s, osw_p, osw_s, mod3, gain, wo, w1, w2, seq_s):
    n_p, n_s = xp.shape[0], xs.shape[0]
    tp, ts = n_p // TM, n_s // TM
    tiles_per_seq = seq_s // TM
    const = lambda shape: pl.BlockSpec(shape, lambda i: (0,) * len(shape), pipeline_mode=pl.Buffered(1))
    row_p = lambda w: pl.BlockSpec((TM, w), lambda i: (jnp.minimum(i, tp - 1), 0))
    row_s = lambda w: pl.BlockSpec((TM, w), lambda i: (jnp.maximum(i - tp, 0), 0))
    mod_row = lambda i: jnp.where(i < tp, 0, 1 + (i - tp) // tiles_per_seq)
    return pl.pallas_call(
        functools.partial(_out_kernel, n_prompt_tiles=tp),
        grid=(tp + ts,),
        in_specs=[row_p(D_MODEL), row_s(D_MODEL), row_p(GLA_WIDTH), row_s(GLA_WIDTH),
                  row_p(SWA_WIDTH), row_s(SWA_WIDTH),
                  pl.BlockSpec((1, 6, D_MODEL), lambda i: (mod_row(i), 0, 0)),
                  const((1, D_MODEL)), const((GLA_WIDTH + SWA_WIDTH, D_MODEL)),
                  const((D_MODEL, D_FF)), const((D_FF, D_MODEL))],
        out_specs=[row_p(D_MODEL), row_s(D_MODEL)],
        out_shape=[jax.ShapeDtypeStruct((n_p, D_MODEL), F32), jax.ShapeDtypeStruct((n_s, D_MODEL), F32)],
        compiler_params=_cparams(1), name="out_proj_mlp",
    )(xp, xs, og_p, og_s, osw_p, osw_s, mod3, gain, wo, w1, w2)


def _rope_tables(seq_len):
    axis_dim = HEAD_DIM // 2
    half = axis_dim // 2
    t = np.arange(seq_len)
    rowp = (t // GRID_W).astype(np.float64)
    colp = (t % GRID_W).astype(np.float64)
    freqs = ROPE_THETA ** (-np.arange(half, dtype=np.float64) / half)
    lane = np.arange(LANES)
    d = lane % HEAD_DIM
    posv = np.where((d // axis_dim)[None, :] == 0, rowp[:, None], colp[:, None])
    ang = posv * freqs[d % half][None, :]
    second = ((d % axis_dim) >= half)[None, :]
    cos, sin = np.cos(ang), np.sin(ang)
    zero = np.zeros_like(sin)
    tabs = (cos, np.where(second, sin, zero), np.where(second, zero, -sin))
    return tuple(jnp.asarray(tab.astype(np.float32)) for tab in tabs)


def kernel(x_prompt, x_sample, c, cache_k, cache_v, state_gla_fwd, state_gla_bwd, c_ctx, w_ada, b_ada, norm_mix, norm_ff, w_in, w_gk_fwd, b_gk_fwd, w_gk_bwd, b_gk_bwd, gla_norm, q_norm, k_norm, sink, w_o, w_ff1, w_ff2):
    depth = w_in.shape[0]
    assert depth == 1, "single trunk layer"
    bp, tp, _ = x_prompt.shape
    bs, ts, _ = x_sample.shape
    n_p = bp * tp
    l = 0

    cond = jnp.concatenate([c_ctx[None, :], c], axis=0)
    mod3 = _modulation(cond, w_ada[l], b_ada[l]).reshape(8, 6, D_MODEL)

    wg = jnp.zeros((LANES, 2 * GLA_WIDTH), F32)
    wg = wg.at[:GATE_RANK, :GLA_WIDTH].set(w_gk_fwd[l]).at[GATE_RANK:2 * GATE_RANK, GLA_WIDTH:].set(w_gk_bwd[l])
    wg = wg.astype(BF16)
    bg = jnp.concatenate([b_gk_fwd[l], b_gk_bwd[l]])[None, :]
    qkg = jnp.concatenate([jnp.tile(q_norm[l], SWA_HEADS), jnp.tile(k_norm[l], SWA_KV_HEADS)])[None, :]
    norm2 = jnp.tile(gla_norm[l], GLA_HEADS)[None, :]
    gain_mix = norm_mix[l][None, :]
    gain_ff = norm_ff[l][None, :]
    sink_l = sink[l]

    w_in_t = jnp.transpose(w_in[l])
    cache_k_t = jnp.swapaxes(cache_k, 3, 4)
    cache_v_t = jnp.swapaxes(cache_v, 3, 4)

    xp = x_prompt.reshape(n_p, D_MODEL)
    xs = x_sample.reshape(bs * ts, D_MODEL)
    (gq, gk, gv, sgg, gf, gb, sq, sk, sv, new_k_t, new_v_t) = _project(
        xp, xs, mod3, gain_mix, w_in_t, wg, bg, qkg, _rope_tables(ts), tp, ts)
    new_k = jnp.swapaxes(new_k_t, 3, 4)
    new_v = jnp.swapaxes(new_v_t, 3, 4)

    og_p, s_f, s_b, wo, w1, w2 = _gla(gq, gk, gv, gf, gb, sgg, norm2, None, None, bp, tp, 0, True,
                                      cast_weights=(w_o[l], w_ff1[l], w_ff2[l]))
    osw_p = _context_attention(sink_l, sq, sk, sv, bp, tp)
    (og_s,) = _gla(gq, gk, gv, gf, gb, sgg, norm2, state_gla_fwd, state_gla_bwd, bs, ts, n_p // ts, False)
    osw_s = _latent_attention(sink_l, sq, sk, sv, cache_k_t, cache_v_t, bs, ts, n_p)

    yp, ys = _mix_out(xp, xs, og_p, og_s, osw_p, osw_s, mod3, gain_ff, wo, w1, w2, ts)
    return (yp.reshape(bp, tp, D_MODEL), ys.reshape(bs, ts, D_MODEL), new_k, new_v, s_f, s_b)
```

```python
import functools

import numpy as np
import jax
import jax.numpy as jnp
from jax import lax
from jax.experimental import pallas as pl
from jax.experimental.pallas import tpu as pltpu

F32 = jnp.float32
BF16 = jnp.bfloat16

D_MODEL = 1024
GRID_W = 64
HEAD_DIM = 64
GLA_HEADS = 8
GLA_WIDTH = GLA_HEADS * HEAD_DIM
GATE_RANK = 16
GATE_TEMP = 16.0
CHUNK = 64
GROUP = 4 * CHUNK
SWA_HEADS = 8
SWA_KV_HEADS = 2
SWA_GROUP = SWA_HEADS // SWA_KV_HEADS
SWA_WIDTH = SWA_HEADS * HEAD_DIM
KV_WIDTH = SWA_KV_HEADS * HEAD_DIM
WINDOW = 128
QBLOCK = 128
D_FF = 4 * D_MODEL
ROPE_THETA = 10000.0
EPS = 1e-6
NEG_INF = -1e30
LANES = 128
TM = 512
CTX_HEAD_STACK = 1
LAT_HEAD_STACK = 4

D_IN = 2848
R_GQ, R_GK, R_GV, R_GG, R_LR = 0, 512, 1024, 1536, 2048
R_SQ = R_LR + 2 * GATE_RANK
R_SV = R_SQ + SWA_WIDTH + KV_WIDTH

VMEM_LIMIT = 56 * 1024 * 1024


def _cparams(n_axes, flags=None):
    return pltpu.CompilerParams(dimension_semantics=("arbitrary",) * n_axes,
                                vmem_limit_bytes=VMEM_LIMIT, flags=flags)


def _mod_kernel(condt_ref, w_ref, b_ref, o_ref, *, n_rows):
    k = pl.program_id(0)
    w = w_ref[...]
    rows = []
    for r in range(n_rows):
        cnd = condt_ref[:, r:r + 1]
        s = cnd * jax.nn.sigmoid(cnd)
        rows.append(jnp.sum(w * s, axis=0, keepdims=True))
    rows.append(jnp.zeros((8 - n_rows, w.shape[1]), F32))
    part = jnp.concatenate(rows, axis=0)

    @pl.when(k == 0)
    def _():
        o_ref[...] = part + b_ref[...]

    @pl.when(k > 0)
    def _():
        o_ref[...] += part


def _modulation(cond, w_ada, b_ada):
    n_rows = cond.shape[0]
    condt = jnp.zeros((D_MODEL, 8), F32).at[:, :n_rows].set(cond.T)
    tk = 128
    n_out = w_ada.shape[1]
    return pl.pallas_call(
        functools.partial(_mod_kernel, n_rows=n_rows),
        grid=(D_MODEL // tk,),
        in_specs=[pl.BlockSpec((tk, 8), lambda k: (k, 0)),
                  pl.BlockSpec((tk, n_out), lambda k: (k, 0)),
                  pl.BlockSpec((1, n_out), lambda k: (0, 0))],
        out_specs=pl.BlockSpec((8, n_out), lambda k: (0, 0)),
        out_shape=jax.ShapeDtypeStruct((8, n_out), F32),
        compiler_params=_cparams(1),
        name="modulation",
    )(condt, w_ada, b_ada.reshape(1, n_out))


def _adaln(x, gain, shift, scale):
    ms = jnp.mean(x * x, axis=-1, keepdims=True)
    xn = x * lax.rsqrt(ms + EPS)
    return (xn * gain) * (1.0 + scale) + shift


def _head_mean_sq(y):
    cols = y.shape[1]
    lane = lax.broadcasted_iota(jnp.int32, (1, LANES), 1)
    first = lane < HEAD_DIM
    outs = []
    for p in range(cols // LANES):
        s = y[:, p * LANES:(p + 1) * LANES]
        s = s * s
        tot = jnp.sum(s, axis=-1, keepdims=True)
        lo = jnp.sum(jnp.where(first, s, 0.0), axis=-1, keepdims=True)
        outs.append(jnp.where(first, lo, tot - lo) * (1.0 / HEAD_DIM))
    return outs[0] if len(outs) == 1 else jnp.concatenate(outs, axis=-1)


def _log_sigmoid(x):
    return jnp.minimum(x, 0.0) - jnp.log(1.0 + jnp.exp(-jnp.abs(x)))


def _dot(a, b):
    return jnp.dot(a, b, preferred_element_type=F32)


def _dot_nt(a, b):
    return lax.dot_general(a, b, (((1,), (1,)), ((), ())), preferred_element_type=F32)


def _proj_kernel(xp_ref, xs_ref, mod_ref, gain_ref, w_ref, wg_ref, bg_ref, qkg_ref,
                 cos_ref, sin_a_ref, sin_b_ref,
                 gq_ref, gk_ref, gv_ref, sgg_ref, gf_ref, gb_ref, sq_ref, sk_ref, sv_ref, nk_ref, nv_ref,
                 wb_ref, h0_ref, h1_ref, *, n_prompt_tiles, seqs_per_tile, seq_len):
    h_refs = (h0_ref, h1_ref)
    s = pl.program_id(0)

    @pl.when(s == 0)
    def _():
        blk = 32

        def conv(r, carry):
            rows = pl.ds(pl.multiple_of(r * blk, blk), blk)
            wb_ref[rows, :] = w_ref[rows, :].astype(BF16)
            return carry

        lax.fori_loop(0, D_IN // blk, conv, 0)

    def normalise_next(parity):
        x = jnp.where(s < n_prompt_tiles, xp_ref[...], xs_ref[...])
        h_refs[parity][...] = _adaln(x, gain_ref[...], mod_ref[0, 0:1, :], mod_ref[0, 1:2, :]).astype(BF16)

    def body(rope, emit_cache, parity):
        normalise_next(parity)
        h_ref = h_refs[1 - parity]

        def proj(r0, r1):
            return _dot_nt(h_ref[...], wb_ref[r0:r1, :])

        tail = proj(R_SQ, D_IN)
        lr = proj(R_LR, R_LR + LANES).astype(BF16)
        gate = _log_sigmoid(_dot(lr, wg_ref[...]) + bg_ref[...]) * (1.0 / GATE_TEMP)
        gf_ref[...] = gate[:, :GLA_WIDTH]
        gb_ref[...] = gate[:, GLA_WIDTH:]

        qk = tail[:, :R_SV - R_SQ]
        qk = qk * lax.rsqrt(_head_mean_sq(qk) + EPS) * qkg_ref[...]
        if rope:
            reps = (SWA_WIDTH + KV_WIDTH) // LANES
            cos = jnp.concatenate([cos_ref[...]] * reps, axis=-1)
            sin_a = jnp.concatenate([sin_a_ref[...]] * reps, axis=-1)
            sin_b = jnp.concatenate([sin_b_ref[...]] * reps, axis=-1)
            width = qk.shape[1]
            quarter = HEAD_DIM // 4
            qk = (qk * cos + pltpu.roll(qk, quarter, 1) * sin_a
                  + pltpu.roll(qk, width - quarter, 1) * sin_b)
        sq_ref[...] = (qk[:, :SWA_WIDTH] * (HEAD_DIM ** -0.5)).astype(BF16)
        sk = qk[:, SWA_WIDTH:]
        sk_ref[...] = sk.astype(BF16)
        sv = tail[:, R_SV - R_SQ:]
        sv_ref[...] = sv.astype(BF16)
        if emit_cache:
            sk_t, sv_t = sk.T, sv.T
            for q in range(seqs_per_tile):
                cols = slice(q * seq_len, (q + 1) * seq_len)
                for g in range(SWA_KV_HEADS):
                    feat = slice(g * HEAD_DIM, (g + 1) * HEAD_DIM)
                    nk_ref[q, 0, g, :, :] = sk_t[feat, cols]
                    nv_ref[q, 0, g, :, :] = sv_t[feat, cols]

        gg = proj(R_GG, R_LR)
        sgg_ref[...] = (gg * jax.nn.sigmoid(gg)).astype(BF16)
        gq_ref[...] = (proj(R_GQ, R_GK) * (HEAD_DIM ** -0.5)).astype(BF16)
        gk_ref[...] = proj(R_GK, R_GV).astype(BF16)
        gv_ref[...] = proj(R_GV, R_GG).astype(BF16)

    @pl.when(s == 0)
    def _():
        normalise_next(0)

    for parity in range(2):
        on_parity = (s % 2) == parity

        @pl.when((s >= 1) & (s <= n_prompt_tiles) & on_parity)
        def _():
            body(rope=False, emit_cache=True, parity=parity)

        @pl.when((s > n_prompt_tiles) & on_parity)
        def _():
            body(rope=True, emit_cache=False, parity=parity)


def _project(xp, xs, mod3, gain, w_in, wg, bg, qkg, rope_tabs, seq_p, seq_s):
    n_p, n_s = xp.shape[0], xs.shape[0]
    tp, ts = n_p // TM, n_s // TM
    n = n_p + n_s
    seqs_per_tile = TM // seq_p
    tiles_per_seq = seq_s // TM
    n_seq_p = n_p // seq_p
    const = lambda shape: pl.BlockSpec(shape, lambda s: (0,) * len(shape))
    n_tiles = tp + ts
    in_tile = lambda s: jnp.minimum(s, n_tiles - 1)
    out_tile = lambda s: jnp.maximum(s - 1, 0)
    mod_row = lambda t: jnp.where(t < tp, 0, 1 + (t - tp) // tiles_per_seq)
    rope_spec = pl.BlockSpec((TM, LANES), lambda s: (jnp.maximum(out_tile(s) - tp, 0) % tiles_per_seq, 0))
    in_specs = [pl.BlockSpec((TM, D_MODEL), lambda s: (jnp.minimum(s, tp - 1), 0)),
                pl.BlockSpec((TM, D_MODEL), lambda s: (jnp.clip(s - tp, 0, ts - 1), 0)),
                pl.BlockSpec((1, 6, D_MODEL), lambda s: (mod_row(in_tile(s)), 0, 0)),
                const((1, D_MODEL)),
                pl.BlockSpec((D_IN, D_MODEL), lambda s: (0, 0), pipeline_mode=pl.Buffered(1)),
                const((LANES, 2 * GLA_WIDTH)), const((1, 2 * GLA_WIDTH)), const((1, SWA_WIDTH + KV_WIDTH)),
                rope_spec, rope_spec, rope_spec]
    row = lambda w: pl.BlockSpec((TM, w), lambda s: (out_tile(s), 0))
    cache_spec = pl.BlockSpec((seqs_per_tile, 1, SWA_KV_HEADS, HEAD_DIM, seq_p),
                              lambda s: (jnp.minimum(out_tile(s), tp - 1), 0, 0, 0, 0))
    out_specs = [row(GLA_WIDTH)] * 6 + [row(SWA_WIDTH), row(KV_WIDTH), row(KV_WIDTH)] + [cache_spec] * 2
    cache_shape = (n_seq_p, 1, SWA_KV_HEADS, HEAD_DIM, seq_p)
    out_shape = ([jax.ShapeDtypeStruct((n, GLA_WIDTH), BF16)] * 4
                 + [jax.ShapeDtypeStruct((n, GLA_WIDTH), F32)] * 2
                 + [jax.ShapeDtypeStruct((n, SWA_WIDTH), BF16),
                    jax.ShapeDtypeStruct((n, KV_WIDTH), BF16), jax.ShapeDtypeStruct((n, KV_WIDTH), BF16)]
                 + [jax.ShapeDtypeStruct(cache_shape, F32)] * 2)
    return pl.pallas_call(
        functools.partial(_proj_kernel, n_prompt_tiles=tp, seqs_per_tile=seqs_per_tile, seq_len=seq_p),
        grid=(n_tiles + 1,), in_specs=in_specs, out_specs=out_specs, out_shape=out_shape,
        scratch_shapes=[pltpu.VMEM((D_IN, D_MODEL), BF16)] + [pltpu.VMEM((TM, D_MODEL), BF16)] * 2,
        compiler_params=_cparams(1), name="adaln_in_proj",
    )(xp, xs, mod3, gain, w_in, wg, bg, qkg, *rope_tabs)


def _gla_kernel(*refs, seq_len, has_state, emit_state, n_cast):
    (q_ref, k_ref, v_ref, gf_ref, gb_ref, sgg_ref, norm_ref,
     trif_ref, trib_ref, keepf_ref, keepb_ref) = refs[:11]
    pos = 11
    if has_state:
        s0f_ref, s0b_ref = refs[pos:pos + 2]
        pos += 2
    cast_in = refs[pos:pos + n_cast]
    pos += n_cast
    og_ref = refs[pos]
    pos += 1
    if emit_state:
        sf_ref, sb_ref = refs[pos:pos + 2]
        pos += 2
    cast_out = refs[pos:pos + n_cast]
    pos += n_cast
    accf_ref, accb_ref = refs[pos:pos + 2]

    for src, dst in zip(cast_in, cast_out):
        dst[...] = src[...].astype(BF16)

    n_groups = seq_len // GROUP
    per_group = GROUP // CHUNK
    n_pairs = q_ref.shape[1] // LANES
    lane = lax.broadcasted_iota(jnp.int32, (1, LANES), 1)
    head0 = lane < HEAD_DIM
    m0 = jnp.where(head0, 1.0, 0.0).astype(BF16)
    m1 = jnp.where(head0, 0.0, 1.0).astype(BF16)
    row_top = lax.broadcasted_iota(jnp.int32, (LANES, 1), 0) < HEAD_DIM
    same_head = row_top == head0

    def stack_heads(x):
        return jnp.concatenate([x * m0, x * m1], axis=0)

    def direction(row0, lanes, g_ref, tri_ref, keep_ref, state, acc_ref, backward):
        rows = pl.ds(row0, GROUP)
        g = g_ref[rows, lanes]
        g_hi = g.astype(BF16)
        g_lo = (g - g_hi.astype(F32)).astype(BF16)
        r = _dot(tri_ref[...], jnp.concatenate([g_hi, g_lo], axis=-1))
        b = (r[:, :LANES] + r[:, LANES:]).reshape(per_group, CHUNK, LANES)
        edge = 0 if backward else CHUNK - 1
        mid = CHUNK // 2
        b_edge = b[:, edge:edge + 1, :]
        b_mid = b[:, mid:mid + 1, :]
        q = q_ref[rows, lanes].astype(F32).reshape(per_group, CHUNK, LANES)
        k = k_ref[rows, lanes].astype(F32).reshape(per_group, CHUNK, LANES)
        v = v_ref[rows, lanes]
        q_in = (q * jnp.exp(b - b_mid)).astype(BF16)
        k_in = (k * jnp.exp(b_mid - b)).astype(BF16).reshape(GROUP, LANES)
        q_st = (q * jnp.exp(b)).astype(BF16)
        k_st = (k * jnp.exp(b_edge - b)).astype(BF16).reshape(GROUP, LANES)
        decay = jnp.exp(b_edge)
        keep = keep_ref[...]
        pair_rows = 2 * CHUNK
        pair_order = range(per_group // 2 - 1, -1, -1) if backward else range(per_group // 2)
        for p in pair_order:
            c0 = 2 * p
            prow = slice(p * pair_rows, (p + 1) * pair_rows)
            q_pair = jnp.concatenate([stack_heads(q_in[c0]), stack_heads(q_in[c0 + 1])], axis=0)
            a = _dot_nt(q_pair, k_in[prow, :]).astype(BF16) * keep
            v_pair = v[prow, :]
            intra = _dot(a, v_pair)
            v_t = v_pair.astype(F32).T.astype(BF16)
            k_pair = k_st[prow, :]
            zero = jnp.zeros_like(k_pair)
            k_blk = jnp.concatenate([jnp.where(row_top, k_pair, zero), jnp.where(row_top, zero, k_pair)], axis=-1)
            kv_t = _dot(v_t, k_blk)
            for j in ((1, 0) if backward else (0, 1)):
                c = c0 + j
                inter = _dot_nt(stack_heads(q_st[c]), state.astype(BF16))
                tot = intra[j * pair_rows:(j + 1) * pair_rows, :] + inter
                acc_ref[pl.ds(row0 + c * CHUNK, CHUNK), lanes] = jnp.where(head0, tot[:CHUNK, :], tot[CHUNK:, :])
                state = state * decay[c] + jnp.where(same_head, kv_t[:, j * LANES:(j + 1) * LANES], 0.0)
        return state

    def load_state(s_ref, p):
        z = jnp.zeros((HEAD_DIM, HEAD_DIM), F32)
        top = jnp.concatenate([s_ref[0, 0, 2 * p, :, :].T, z], axis=-1)
        bot = jnp.concatenate([z, s_ref[0, 0, 2 * p + 1, :, :].T], axis=-1)
        return jnp.concatenate([top, bot], axis=0)

    if has_state:
        init = tuple(load_state(s_ref, p) for s_ref in (s0f_ref, s0b_ref) for p in range(n_pairs))
    else:
        init = (jnp.zeros((LANES, LANES), F32),) * (2 * n_pairs)

    def body(t, carry):
        row_f = t * GROUP
        row_b = (n_groups - 1 - t) * GROUP
        if n_groups > 1:
            row_f = pl.multiple_of(row_f, GROUP)
            row_b = pl.multiple_of(row_b, GROUP)
        new_f, new_b = [], []
        for p in range(n_pairs):
            lanes = slice(p * LANES, (p + 1) * LANES)
            new_f.append(direction(row_f, lanes, gf_ref, trif_ref, keepf_ref, carry[p], accf_ref, False))
            new_b.append(direction(row_b, lanes, gb_ref, trib_ref, keepb_ref, carry[n_pairs + p], accb_ref, True))
        return tuple(new_f + new_b)

    if n_groups == 1:
        final = body(0, init)
    else:
        final = lax.fori_loop(0, n_groups, body, init)
    if emit_state:
        for s_ref, fins in ((sf_ref, final[:n_pairs]), (sb_ref, final[n_pairs:])):
            for p, fin in enumerate(fins):
                s_ref[0, 0, 2 * p, :, :] = fin[:HEAD_DIM, :HEAD_DIM].T
                s_ref[0, 0, 2 * p + 1, :, :] = fin[HEAD_DIM:, HEAD_DIM:].T

    o = accf_ref[...] + accb_ref[...]
    o = o * lax.rsqrt(_head_mean_sq(o) + EPS) * norm_ref[...]
    og_ref[...] = (o * sgg_ref[...].astype(F32)).astype(BF16)


def _gla_constants():
    i = np.arange(GROUP)[:, None]
    j = np.arange(GROUP)[None, :]
    same = (i // CHUNK) == (j // CHUNK)
    tri_f = same & (j <= i)
    tri_b = same & (j >= i)
    r = np.arange(4 * CHUNK)[:, None]
    l = np.arange(LANES)[None, :]
    same = (r // (2 * CHUNK)) == (l // CHUNK)
    keep_f = same & ((l % CHUNK) <= (r % CHUNK))
    keep_b = same & ((l % CHUNK) >= (r % CHUNK))
    return tuple(jnp.asarray(m.astype(np.float32), dtype=BF16) for m in (tri_f, tri_b, keep_f, keep_b))


def _gla(gq, gk, gv, gf, gb, sgg, norm_row, s0f, s0b, n_seq, seq_len, row_block0, emit_state, cast_weights=()):
    has_state = s0f is not None
    steps = n_seq
    blk = pl.BlockSpec((seq_len, GLA_WIDTH), lambda b: (row_block0 + b, 0))
    out_blk = pl.BlockSpec((seq_len, GLA_WIDTH), lambda b: (b, 0))
    st_spec = pl.BlockSpec((1, 1, GLA_HEADS, HEAD_DIM, HEAD_DIM), lambda b: (b, 0, 0, 0, 0))
    const = lambda shape: pl.BlockSpec(shape, lambda b: (0, 0))
    in_specs = ([blk] * 6 + [const((1, GLA_WIDTH))] + [const((GROUP, GROUP))] * 2
                + [const((4 * CHUNK, LANES))] * 2)
    args = [gq, gk, gv, gf, gb, sgg, norm_row, *_gla_constants()]
    if has_state:
        in_specs += [st_spec] * 2
        args += [s0f, s0b]
    out_specs = [out_blk]
    out_shape = [jax.ShapeDtypeStruct((n_seq * seq_len, GLA_WIDTH), BF16)]
    if emit_state:
        out_specs += [st_spec] * 2
        out_shape += [jax.ShapeDtypeStruct((n_seq, 1, GLA_HEADS, HEAD_DIM, HEAD_DIM), F32)] * 2
    for w in cast_weights:
        rows = w.shape[0] // steps
        spec = pl.BlockSpec((rows, w.shape[1]), lambda b: (b, 0))
        in_specs.append(spec)
        args.append(w)
        out_specs.append(spec)
        out_shape.append(jax.ShapeDtypeStruct(w.shape, BF16))
    return pl.pallas_call(
        functools.partial(_gla_kernel, seq_len=seq_len, has_state=has_state, emit_state=emit_state,
                          n_cast=len(cast_weights)),
        grid=(n_seq,), in_specs=in_specs, out_specs=out_specs, out_shape=out_shape,
        scratch_shapes=[pltpu.VMEM((seq_len, GLA_WIDTH), F32)] * 2,
        compiler_params=_cparams(1), name="gla_bidirectional",
    )(*args)


def _group_attention(q, heads, sink_ref, score_fn, value_fn):
    rows = q.shape[0]
    q_stack = jnp.concatenate([q[:, h * HEAD_DIM:(h + 1) * HEAD_DIM] for h in heads], axis=0)
    scores = score_fn(q_stack)
    e_parts = [[] for _ in scores]
    denoms = []
    for j, h in enumerate(heads):
        slab = slice(j * rows, (j + 1) * rows)
        m = sink_ref[h]
        for s in scores:
            m = jnp.maximum(m, jnp.max(s[slab, :], axis=-1, keepdims=True))
        denom = jnp.exp(sink_ref[h] - m)
        for part, s in zip(e_parts, scores):
            e = jnp.exp(s[slab, :] - m)
            denom = denom + jnp.sum(e, axis=-1, keepdims=True)
            part.append(e.astype(BF16))
        denoms.append(denom)
    acc = value_fn([jnp.concatenate(part, axis=0) for part in e_parts])
    return [acc[j * rows:(j + 1) * rows, :] / denoms[j] for j in range(len(heads))]


def _ctx_attn_kernel(sink_ref, q_ref, k_ref, v_ref, o_ref):
    q = q_ref[...]
    k = k_ref[...]
    v = v_ref[...]
    outs = []
    for g in range(SWA_KV_HEADS):
        k_g = k[:, g * HEAD_DIM:(g + 1) * HEAD_DIM]
        v_g = v[:, g * HEAD_DIM:(g + 1) * HEAD_DIM]
        for h0 in range(SWA_GROUP * g, SWA_GROUP * (g + 1), CTX_HEAD_STACK):
            outs += _group_attention(
                q, list(range(h0, h0 + CTX_HEAD_STACK)), sink_ref,
                lambda qs: [_dot_nt(qs, k_g)],
                lambda es: _dot(es[0], v_g))
    o_ref[...] = jnp.concatenate(outs, axis=-1).astype(BF16)


def _context_attention(sink, sq, sk, sv, n_seq, seq_len):
    return pl.pallas_call(
        _ctx_attn_kernel,
        grid=(n_seq,),
        in_specs=[pl.BlockSpec(memory_space=pltpu.SMEM),
                  pl.BlockSpec((seq_len, SWA_WIDTH), lambda b: (b, 0)),
                  pl.BlockSpec((seq_len, KV_WIDTH), lambda b: (b, 0)),
                  pl.BlockSpec((seq_len, KV_WIDTH), lambda b: (b, 0))],
        out_specs=pl.BlockSpec((seq_len, SWA_WIDTH), lambda b: (b, 0)),
        out_shape=jax.ShapeDtypeStruct((n_seq * seq_len, SWA_WIDTH), BF16),
        compiler_params=_cparams(1), name="context_attention",
    )(sink, sq, sk, sv)


def _lat_attn_kernel(sink_ref, q_ref, k_ref, v_ref, kc_ref, vc_ref, o_ref, *, seq_len):
    span = QBLOCK + 2 * WINDOW
    i = pl.program_id(1)
    start = jnp.clip(i * QBLOCK - WINDOW, 0, seq_len - span)
    start = pl.multiple_of(start, QBLOCK)
    q = q_ref[...]
    k = k_ref[pl.ds(start, span), :]
    v = v_ref[pl.ds(start, span), :]
    qpos = i * QBLOCK + lax.broadcasted_iota(jnp.int32, (QBLOCK, span), 0)
    kpos = start + lax.broadcasted_iota(jnp.int32, (QBLOCK, span), 1)
    valid = jnp.abs(qpos - kpos) <= WINDOW
    valid = jnp.concatenate([valid] * LAT_HEAD_STACK, axis=0)
    outs = []
    for g in range(SWA_KV_HEADS):
        k_g = k[:, g * HEAD_DIM:(g + 1) * HEAD_DIM]
        v_g = v[:, g * HEAD_DIM:(g + 1) * HEAD_DIM]
        kc_t = kc_ref[0, 0, g, :, :].astype(BF16)
        vc_t = vc_ref[0, 0, g, :, :].astype(BF16)
        for h0 in range(SWA_GROUP * g, SWA_GROUP * (g + 1), LAT_HEAD_STACK):
            outs += _group_attention(
                q, list(range(h0, h0 + LAT_HEAD_STACK)), sink_ref,
                lambda qs: [jnp.where(valid, _dot_nt(qs, k_g), NEG_INF), _dot(qs, kc_t)],
                lambda es: _dot(es[0], v_g) + _dot_nt(es[1], vc_t))
    o_ref[...] = jnp.concatenate(outs, axis=-1).astype(BF16)


def _latent_attention(sink, sq, sk, sv, cache_k, cache_v, n_seq, seq_len, row0):
    nq = seq_len // QBLOCK
    q0 = row0 // QBLOCK
    s0 = row0 // seq_len
    past = cache_k.shape[4]
    cache_spec = pl.BlockSpec((1, 1, SWA_KV_HEADS, HEAD_DIM, past), lambda b, i: (b, 0, 0, 0, 0))
    return pl.pallas_call(
        functools.partial(_lat_attn_kernel, seq_len=seq_len),
        grid=(n_seq, nq),
        in_specs=[pl.BlockSpec(memory_space=pltpu.SMEM),
                  pl.BlockSpec((QBLOCK, SWA_WIDTH), lambda b, i: (q0 + b * nq + i, 0)),
                  pl.BlockSpec((seq_len, KV_WIDTH), lambda b, i: (s0 + b, 0)),
                  pl.BlockSpec((seq_len, KV_WIDTH), lambda b, i: (s0 + b, 0)),
                  cache_spec, cache_spec],
        out_specs=pl.BlockSpec((QBLOCK, SWA_WIDTH), lambda b, i: (b * nq + i, 0)),
        out_shape=jax.ShapeDtypeStruct((n_seq * seq_len, SWA_WIDTH), BF16),
        compiler_params=_cparams(2), name="latent_attention",
    )(sink, sq, sk, sv, cache_k, cache_v)


def _out_kernel(xp_ref, xs_ref, ogp_ref, ogs_ref, oswp_ref, osws_ref, mod_ref, gain_ref,
                wo_ref, w1_ref, w2_ref, yp_ref, ys_ref, *, n_prompt_tiles):
    def body(x_ref, og_ref, osw_ref, y_ref):
        mix = _dot(og_ref[...], wo_ref[:GLA_WIDTH, :]) + _dot(osw_ref[...], wo_ref[GLA_WIDTH:, :])
        x1 = x_ref[...] + mod_ref[0, 2:3, :] * mix
        h = _adaln(x1, gain_ref[...], mod_ref[0, 3:4, :], mod_ref[0, 4:5, :]).astype(BF16)
        ff_tile = 1024
        acc = None
        for j in range(D_FF // ff_tile):
            cols = slice(j * ff_tile, (j + 1) * ff_tile)
            a = jnp.maximum(_dot(h, w1_ref[:, cols]), 0.0)
            part = _dot((a * a).astype(BF16), w2_ref[cols, :])
            acc = part if acc is None else acc + part
        y_ref[...] = x1 + mod_ref[0, 5:6, :] * acc

    i = pl.program_id(0)

    @pl.when(i < n_prompt_tiles)
    def _():
        body(xp_ref, ogp_ref, oswp_ref, yp_ref)

    @pl.when(i >= n_prompt_tiles)
    def _():
        body(xs_ref, ogs_ref, osws_ref, ys_ref)


def _mix_out(xp, xs, og_p, og_s, osw_p, osw_s, mod3, gain, wo, w1, w2, seq_s):
    n_p, n_s = xp.shape[0], xs.shape[0]
    tp, ts = n_p // TM, n_s // TM
    tiles_per_seq = seq_s // TM
    const = lambda shape: pl.BlockSpec(shape, lambda i: (0,) * len(shape), pipeline_mode=pl.Buffered(1))
    row_p = lambda w: pl.BlockSpec((TM, w), lambda i: (jnp.minimum(i, tp - 1), 0))
    row_s = lambda w: pl.BlockSpec((TM, w), lambda i: (jnp.maximum(i - tp, 0), 0))
    mod_row = lambda i: jnp.where(i < tp, 0, 1 + (i - tp) // tiles_per_seq)
    return pl.pallas_call(
        functools.partial(_out_kernel, n_prompt_tiles=tp),
        grid=(tp + ts,),
        in_specs=[row_p(D_MODEL), row_s(D_MODEL), row_p(GLA_WIDTH), row_s(GLA_WIDTH),
                  row_p(SWA_WIDTH), row_s(SWA_WIDTH),
                  pl.BlockSpec((1, 6, D_MODEL), lambda i: (mod_row(i), 0, 0)),
                  const((1, D_MODEL)), const((GLA_WIDTH + SWA_WIDTH, D_MODEL)),
                  const((D_MODEL, D_FF)), const((D_FF, D_MODEL))],
        out_specs=[row_p(D_MODEL), row_s(D_MODEL)],
        out_shape=[jax.ShapeDtypeStruct((n_p, D_MODEL), F32), jax.ShapeDtypeStruct((n_s, D_MODEL), F32)],
        compiler_params=_cparams(1), name="out_proj_mlp",
    )(xp, xs, og_p, og_s, osw_p, osw_s, mod3, gain, wo, w1, w2)


def _rope_tables(seq_len):
    axis_dim = HEAD_DIM // 2
    half = axis_dim // 2
    t = np.arange(seq_len)
    rowp = (t // GRID_W).astype(np.float64)
    colp = (t % GRID_W).astype(np.float64)
    freqs = ROPE_THETA ** (-np.arange(half, dtype=np.float64) / half)
    lane = np.arange(LANES)
    d = lane % HEAD_DIM
    posv = np.where((d // axis_dim)[None, :] == 0, rowp[:, None], colp[:, None])
    ang = posv * freqs[d % half][None, :]
    second = ((d % axis_dim) >= half)[None, :]
    cos, sin = np.cos(ang), np.sin(ang)
    zero = np.zeros_like(sin)
    tabs = (cos, np.where(second, sin, zero), np.where(second, zero, -sin))
    return tuple(jnp.asarray(tab.astype(np.float32)) for tab in tabs)


def kernel(x_prompt, x_sample, c, cache_k, cache_v, state_gla_fwd, state_gla_bwd, c_ctx, w_ada, b_ada, norm_mix, norm_ff, w_in, w_gk_fwd, b_gk_fwd, w_gk_bwd, b_gk_bwd, gla_norm, q_norm, k_norm, sink, w_o, w_ff1, w_ff2):
    depth = w_in.shape[0]
    assert depth == 1, "single trunk layer"
    bp, tp, _ = x_prompt.shape
    bs, ts, _ = x_sample.shape
    n_p = bp * tp
    l = 0

    cond = jnp.concatenate([c_ctx[None, :], c], axis=0)
    mod3 = _modulation(cond, w_ada[l], b_ada[l]).reshape(8, 6, D_MODEL)

    wg = jnp.zeros((LANES, 2 * GLA_WIDTH), F32)
    wg = wg.at[:GATE_RANK, :GLA_WIDTH].set(w_gk_fwd[l]).at[GATE_RANK:2 * GATE_RANK, GLA_WIDTH:].set(w_gk_bwd[l])
    wg = wg.astype(BF16)
    bg = jnp.concatenate([b_gk_fwd[l], b_gk_bwd[l]])[None, :]
    qkg = jnp.concatenate([jnp.tile(q_norm[l], SWA_HEADS), jnp.tile(k_norm[l], SWA_KV_HEADS)])[None, :]
    norm2 = jnp.tile(gla_norm[l], GLA_HEADS)[None, :]
    gain_mix = norm_mix[l][None, :]
    gain_ff = norm_ff[l][None, :]
    sink_l = sink[l]

    w_in_t = jnp.transpose(w_in[l])
    cache_k_t = jnp.swapaxes(cache_k, 3, 4)
    cache_v_t = jnp.swapaxes(cache_v, 3, 4)

    xp = x_prompt.reshape(n_p, D_MODEL)
    xs = x_sample.reshape(bs * ts, D_MODEL)
    (gq, gk, gv, sgg, gf, gb, sq, sk, sv, new_k_t, new_v_t) = _project(
        xp, xs, mod3, gain_mix, w_in_t, wg, bg, qkg, _rope_tables(ts), tp, ts)
    new_k = jnp.swapaxes(new_k_t, 3, 4)
    new_v = jnp.swapaxes(new_v_t, 3, 4)

    og_p, s_f, s_b, wo, w1, w2 = _gla(gq, gk, gv, gf, gb, sgg, norm2, None, None, bp, tp, 0, True,
                                      cast_weights=(w_o[l], w_ff1[l], w_ff2[l]))
    osw_p = _context_attention(sink_l, sq, sk, sv, bp, tp)
    (og_s,) = _gla(gq, gk, gv, gf, gb, sgg, norm2, state_gla_fwd, state_gla_bwd, bs, ts, n_p // ts, False)
    osw_s = _latent_attention(sink_l, sq, sk, sv, cache_k_t, cache_v_t, bs, ts, n_p)

    yp, ys = _mix_out(xp, xs, og_p, og_s, osw_p, osw_s, mod3, gain_ff, wo, w1, w2, ts)
    return (yp.reshape(bp, tp, D_MODEL), ys.reshape(bs, ts, D_MODEL), new_k, new_v, s_f, s_b)
```

```python
import functools

import numpy as np
import jax
import jax.numpy as jnp
from jax import lax
from jax.experimental import pallas as pl
from jax.experimental.pallas import tpu as pltpu

F32 = jnp.float32
BF16 = jnp.bfloat16

D_MODEL = 1024
GRID_W = 64
HEAD_DIM = 64
GLA_HEADS = 8
GLA_WIDTH = GLA_HEADS * HEAD_DIM
GATE_RANK = 16
GATE_TEMP = 16.0
CHUNK = 64
GROUP = 4 * CHUNK
SWA_HEADS = 8
SWA_KV_HEADS = 2
SWA_GROUP = SWA_HEADS // SWA_KV_HEADS
SWA_WIDTH = SWA_HEADS * HEAD_DIM
KV_WIDTH = SWA_KV_HEADS * HEAD_DIM
WINDOW = 128
QBLOCK = 128
D_FF = 4 * D_MODEL
ROPE_THETA = 10000.0
EPS = 1e-6
NEG_INF = -1e30
LANES = 128
TM = 512
CTX_HEAD_STACK = 1
LAT_HEAD_STACK = 4

D_IN = 2848
R_GQ, R_GK, R_GV, R_GG, R_LR = 0, 512, 1024, 1536, 2048
R_SQ = R_LR + 2 * GATE_RANK
R_SV = R_SQ + SWA_WIDTH + KV_WIDTH

VMEM_LIMIT = 56 * 1024 * 1024


def _cparams(n_axes, flags=None):
    return pltpu.CompilerParams(dimension_semantics=("arbitrary",) * n_axes,
                                vmem_limit_bytes=VMEM_LIMIT, flags=flags)


def _mod_kernel(condt_ref, w_ref, b_ref, o_ref, *, n_rows):
    k = pl.program_id(0)
    w = w_ref[...]
    rows = []
    for r in range(n_rows):
        cnd = condt_ref[:, r:r + 1]
        s = cnd * jax.nn.sigmoid(cnd)
        rows.append(jnp.sum(w * s, axis=0, keepdims=True))
    rows.append(jnp.zeros((8 - n_rows, w.shape[1]), F32))
    part = jnp.concatenate(rows, axis=0)

    @pl.when(k == 0)
    def _():
        o_ref[...] = part + b_ref[...]

    @pl.when(k > 0)
    def _():
        o_ref[...] += part


def _modulation(cond, w_ada, b_ada):
    n_rows = cond.shape[0]
    condt = jnp.zeros((D_MODEL, 8), F32).at[:, :n_rows].set(cond.T)
    tk = 128
    n_out = w_ada.shape[1]
    return pl.pallas_call(
        functools.partial(_mod_kernel, n_rows=n_rows),
        grid=(D_MODEL // tk,),
        in_specs=[pl.BlockSpec((tk, 8), lambda k: (k, 0)),
                  pl.BlockSpec((tk, n_out), lambda k: (k, 0)),
                  pl.BlockSpec((1, n_out), lambda k: (0, 0))],
        out_specs=pl.BlockSpec((8, n_out), lambda k: (0, 0)),
        out_shape=jax.ShapeDtypeStruct((8, n_out), F32),
        compiler_params=_cparams(1),
        name="modulation",
    )(condt, w_ada, b_ada.reshape(1, n_out))


def _adaln(x, gain, shift, scale):
    ms = jnp.mean(x * x, axis=-1, keepdims=True)
    xn = x * lax.rsqrt(ms + EPS)
    return (xn * gain) * (1.0 + scale) + shift


def _head_mean_sq(y):
    cols = y.shape[1]
    lane = lax.broadcasted_iota(jnp.int32, (1, LANES), 1)
    first = lane < HEAD_DIM
    outs = []
    for p in range(cols // LANES):
        s = y[:, p * LANES:(p + 1) * LANES]
        s = s * s
        tot = jnp.sum(s, axis=-1, keepdims=True)
        lo = jnp.sum(jnp.where(first, s, 0.0), axis=-1, keepdims=True)
        outs.append(jnp.where(first, lo, tot - lo) * (1.0 / HEAD_DIM))
    return outs[0] if len(outs) == 1 else jnp.concatenate(outs, axis=-1)


def _log_sigmoid(x):
    return jnp.minimum(x, 0.0) - jnp.log(1.0 + jnp.exp(-jnp.abs(x)))


def _dot(a, b):
    return jnp.dot(a, b, preferred_element_type=F32)


def _dot_nt(a, b):
    return lax.dot_general(a, b, (((1,), (1,)), ((), ())), preferred_element_type=F32)


def _proj_kernel(xp_ref, xs_ref, mod_ref, gain_ref, w_ref, wg_ref, bg_ref, qkg_ref,
                 cos_ref, sin_a_ref, sin_b_ref,
                 gq_ref, gk_ref, gv_ref, sgg_ref, gf_ref, gb_ref, sq_ref, sk_ref, sv_ref, nk_ref, nv_ref,
                 wb_ref, h0_ref, h1_ref, *, n_prompt_tiles, seqs_per_tile, seq_len):
    h_refs = (h0_ref, h1_ref)
    s = pl.program_id(0)

    @pl.when(s == 0)
    def _():
        blk = 32

        def conv(r, carry):
            rows = pl.ds(pl.multiple_of(r * blk, blk), blk)
            wb_ref[rows, :] = w_ref[rows, :].astype(BF16)
            return carry

        lax.fori_loop(0, D_IN // blk, conv, 0)

    def normalise_next(parity):
        x = jnp.where(s < n_prompt_tiles, xp_ref[...], xs_ref[...])
        h_refs[parity][...] = _adaln(x, gain_ref[...], mod_ref[0, 0:1, :], mod_ref[0, 1:2, :]).astype(BF16)

    def body(rope, emit_cache, parity):
        normalise_next(parity)
        h_ref = h_refs[1 - parity]

        def proj(r0, r1):
            return _dot_nt(h_ref[...], wb_ref[r0:r1, :])

        tail = proj(R_SQ, D_IN)
        lr = proj(R_LR, R_LR + LANES).astype(BF16)
        gate = _log_sigmoid(_dot(lr, wg_ref[...]) + bg_ref[...]) * (1.0 / GATE_TEMP)
        gf_ref[...] = gate[:, :GLA_WIDTH]
        gb_ref[...] = gate[:, GLA_WIDTH:]

        qk = tail[:, :R_SV - R_SQ]
        qk = qk * lax.rsqrt(_head_mean_sq(qk) + EPS) * qkg_ref[...]
        if rope:
            reps = (SWA_WIDTH + KV_WIDTH) // LANES
            cos = jnp.concatenate([cos_ref[...]] * reps, axis=-1)
            sin_a = jnp.concatenate([sin_a_ref[...]] * reps, axis=-1)
            sin_b = jnp.concatenate([sin_b_ref[...]] * reps, axis=-1)
            width = qk.shape[1]
            quarter = HEAD_DIM // 4
            qk = (qk * cos + pltpu.roll(qk, quarter, 1) * sin_a
                  + pltpu.roll(qk, width - quarter, 1) * sin_b)
        sq_ref[...] = (qk[:, :SWA_WIDTH] * (HEAD_DIM ** -0.5)).astype(BF16)
        sk = qk[:, SWA_WIDTH:]
        sk_ref[...] = sk.astype(BF16)
        sv = tail[:, R_SV - R_SQ:]
        sv_ref[...] = sv.astype(BF16)
        if emit_cache:
            sk_t, sv_t = sk.T, sv.T
            for q in range(seqs_per_tile):
                cols = slice(q * seq_len, (q + 1) * seq_len)
                for g in range(SWA_KV_HEADS):
                    feat = slice(g * HEAD_DIM, (g + 1) * HEAD_DIM)
                    nk_ref[q, 0, g, :, :] = sk_t[feat, cols]
                    nv_ref[q, 0, g, :, :] = sv_t[feat, cols]

        gg = proj(R_GG, R_LR)
        sgg_ref[...] = (gg * jax.nn.sigmoid(gg)).astype(BF16)
        gq_ref[...] = (proj(R_GQ, R_GK) * (HEAD_DIM ** -0.5)).astype(BF16)
        gk_ref[...] = proj(R_GK, R_GV).astype(BF16)
        gv_ref[...] = proj(R_GV, R_GG).astype(BF16)

    @pl.when(s == 0)
    def _():
        normalise_next(0)

    for parity in range(2):
        on_parity = (s % 2) == parity

        @pl.when((s >= 1) & (s <= n_prompt_tiles) & on_parity)
        def _():
            body(rope=False, emit_cache=True, parity=parity)

        @pl.when((s > n_prompt_tiles) & on_parity)
        def _():
            body(rope=True, emit_cache=False, parity=parity)


def _project(xp, xs, mod3, gain, w_in, wg, bg, qkg, rope_tabs, seq_p, seq_s):
    n_p, n_s = xp.shape[0], xs.shape[0]
    tp, ts = n_p // TM, n_s // TM
    n = n_p + n_s
    seqs_per_tile = TM // seq_p
    tiles_per_seq = seq_s // TM
    n_seq_p = n_p // seq_p
    const = lambda shape: pl.BlockSpec(shape, lambda s: (0,) * len(shape))
    n_tiles = tp + ts
    in_tile = lambda s: jnp.minimum(s, n_tiles - 1)
    out_tile = lambda s: jnp.maximum(s - 1, 0)
    mod_row = lambda t: jnp.where(t < tp, 0, 1 + (t - tp) // tiles_per_seq)
    rope_spec = pl.BlockSpec((TM, LANES), lambda s: (jnp.maximum(out_tile(s) - tp, 0) % tiles_per_seq, 0))
    in_specs = [pl.BlockSpec((TM, D_MODEL), lambda s: (jnp.minimum(s, tp - 1), 0)),
                pl.BlockSpec((TM, D_MODEL), lambda s: (jnp.clip(s - tp, 0, ts - 1), 0)),
                pl.BlockSpec((1, 6, D_MODEL), lambda s: (mod_row(in_tile(s)), 0, 0)),
                const((1, D_MODEL)),
                pl.BlockSpec((D_IN, D_MODEL), lambda s: (0, 0), pipeline_mode=pl.Buffered(1)),
                const((LANES, 2 * GLA_WIDTH)), const((1, 2 * GLA_WIDTH)), const((1, SWA_WIDTH + KV_WIDTH)),
                rope_spec, rope_spec, rope_spec]
    row = lambda w: pl.BlockSpec((TM, w), lambda s: (out_tile(s), 0))
    cache_spec = pl.BlockSpec((seqs_per_tile, 1, SWA_KV_HEADS, HEAD_DIM, seq_p),
                              lambda s: (jnp.minimum(out_tile(s), tp - 1), 0, 0, 0, 0))
    out_specs = [row(GLA_WIDTH)] * 6 + [row(SWA_WIDTH), row(KV_WIDTH), row(KV_WIDTH)] + [cache_spec] * 2
    cache_shape = (n_seq_p, 1, SWA_KV_HEADS, HEAD_DIM, seq_p)
    out_shape = ([jax.ShapeDtypeStruct((n, GLA_WIDTH), BF16)] * 4
                 + [jax.ShapeDtypeStruct((n, GLA_WIDTH), F32)] * 2
                 + [jax.ShapeDtypeStruct((n, SWA_WIDTH), BF16),
                    jax.ShapeDtypeStruct((n, KV_WIDTH), BF16), jax.ShapeDtypeStruct((n, KV_WIDTH), BF16)]
                 + [jax.ShapeDtypeStruct(cache_shape, F32)] * 2)
    return pl.pallas_call(
        functools.partial(_proj_kernel, n_prompt_tiles=tp, seqs_per_tile=seqs_per_tile, seq_len=seq_p),
        grid=(n_tiles + 1,), in_specs=in_specs, out_specs=out_specs, out_shape=out_shape,
        scratch_shapes=[pltpu.VMEM((D_IN, D_MODEL), BF16)] + [pltpu.VMEM((TM, D_MODEL), BF16)] * 2,
        compiler_params=_cparams(1), name="adaln_in_proj",
    )(xp, xs, mod3, gain, w_in, wg, bg, qkg, *rope_tabs)


def _gla_kernel(*refs, seq_len, has_state, emit_state, n_cast):
    (q_ref, k_ref, v_ref, gf_ref, gb_ref, sgg_ref, norm_ref,
     trif_ref, trib_ref, keepf_ref, keepb_ref) = refs[:11]
    pos = 11
    if has_state:
        s0f_ref, s0b_ref = refs[pos:pos + 2]
        pos += 2
    cast_in = refs[pos:pos + n_cast]
    pos += n_cast
    og_ref = refs[pos]
    pos += 1
    if emit_state:
        sf_ref, sb_ref = refs[pos:pos + 2]
        pos += 2
    cast_out = refs[pos:pos + n_cast]
    pos += n_cast
    accf_ref, accb_ref = refs[pos:pos + 2]

    for src, dst in zip(cast_in, cast_out):
        dst[...] = src[...].astype(BF16)

    n_groups = seq_len // GROUP
    per_group = GROUP // CHUNK
    n_pairs = q_ref.shape[1] // LANES
    lane = lax.broadcasted_iota(jnp.int32, (1, LANES), 1)
    head0 = lane < HEAD_DIM
    m0 = jnp.where(head0, 1.0, 0.0).astype(BF16)
    m1 = jnp.where(head0, 0.0, 1.0).astype(BF16)
    row_top = lax.broadcasted_iota(jnp.int32, (LANES, 1), 0) < HEAD_DIM

    def stack_heads(x):
        return jnp.concatenate([x * m0, x * m1], axis=0)

    def direction(row0, lanes, g_ref, tri_ref, keep_ref, state, acc_ref, backward):
        rows = pl.ds(row0, GROUP)
        g = g_ref[rows, lanes]
        g_hi = g.astype(BF16)
        g_lo = (g - g_hi.astype(F32)).astype(BF16)
        r = _dot(tri_ref[...], jnp.concatenate([g_hi, g_lo], axis=-1))
        yield
        b = (r[:, :LANES] + r[:, LANES:]).reshape(per_group, CHUNK, LANES)
        edge = 0 if backward else CHUNK - 1
        mid = CHUNK // 2
        b_edge = b[:, edge:edge + 1, :]
        b_mid = b[:, mid:mid + 1, :]
        q = q_ref[rows, lanes].astype(F32).reshape(per_group, CHUNK, LANES)
        k = k_ref[rows, lanes].astype(F32).reshape(per_group, CHUNK, LANES)
        v = v_ref[rows, lanes]
        q_mid = q * jnp.exp(b - b_mid)
        k_mid = k * jnp.exp(b_mid - b)
        q_in = q_mid.astype(BF16)
        k_in = k_mid.astype(BF16).reshape(GROUP, LANES)
        q_st = (q_mid * jnp.exp(b_mid)).astype(BF16)
        k_st = (k_mid * jnp.exp(b_edge - b_mid)).astype(BF16).reshape(GROUP, LANES)
        decay = jnp.exp(b_edge)
        keep = keep_ref[...]
        pair_rows = 2 * CHUNK
        n_cp = per_group // 2
        yield
        scores, kvs = [], []
        for p in range(n_cp):
            c0 = 2 * p
            prow = slice(p * pair_rows, (p + 1) * pair_rows)
            q_pair = jnp.concatenate([stack_heads(q_in[c0]), stack_heads(q_in[c0 + 1])], axis=0)
            scores.append(_dot_nt(q_pair, k_in[prow, :]))
            v_t = v[prow, :].astype(F32).T.astype(BF16)
            k_pair = k_st[prow, :]
            zero = jnp.zeros_like(k_pair)
            k_blk = jnp.concatenate([jnp.where(row_top, k_pair, zero), jnp.where(row_top, zero, k_pair)], axis=-1)
            kvs.append(_dot(v_t, k_blk))
        yield
        intras = [_dot(scores[p].astype(BF16) * keep, v[p * pair_rows:(p + 1) * pair_rows, :]) for p in range(n_cp)]
        yield
        for p in (range(n_cp - 1, -1, -1) if backward else range(n_cp)):
            for j in ((1, 0) if backward else (0, 1)):
                c = 2 * p + j
                inter = _dot_nt(stack_heads(q_st[c]), state.astype(BF16))
                tot = intras[p][j * pair_rows:(j + 1) * pair_rows, :] + inter
                acc_ref[pl.ds(row0 + c * CHUNK, CHUNK), lanes] = jnp.where(head0, tot[:CHUNK, :], tot[CHUNK:, :])
                state = state * decay[c] + kvs[p][:, j * LANES:(j + 1) * LANES]
            yield
        return state

    def load_state(s_ref, p):
        z = jnp.zeros((HEAD_DIM, HEAD_DIM), F32)
        top = jnp.concatenate([s_ref[0, 0, 2 * p, :, :].T, z], axis=-1)
        bot = jnp.concatenate([z, s_ref[0, 0, 2 * p + 1, :, :].T], axis=-1)
        return jnp.concatenate([top, bot], axis=0)

    if has_state:
        init = tuple(load_state(s_ref, p) for s_ref in (s0f_ref, s0b_ref) for p in range(n_pairs))
    else:
        init = (jnp.zeros((LANES, LANES), F32),) * (2 * n_pairs)

    def body(t, carry):
        row_f = t * GROUP
        row_b = (n_groups - 1 - t) * GROUP
        if n_groups > 1:
            row_f = pl.multiple_of(row_f, GROUP)
            row_b = pl.multiple_of(row_b, GROUP)
        units = []
        for p in range(n_pairs):
            lanes = slice(p * LANES, (p + 1) * LANES)
            units.append(direction(row_f, lanes, gf_ref, trif_ref, keepf_ref, carry[p], accf_ref, False))
        for p in range(n_pairs):
            lanes = slice(p * LANES, (p + 1) * LANES)
            units.append(direction(row_b, lanes, gb_ref, trib_ref, keepb_ref, carry[n_pairs + p], accb_ref, True))
        return tuple(_run_staged(units))

    if n_groups == 1:
        final = body(0, init)
    else:
        final = lax.fori_loop(0, n_groups, body, init)
    if emit_state:
        for s_ref, fins in ((sf_ref, final[:n_pairs]), (sb_ref, final[n_pairs:])):
            for p, fin in enumerate(fins):
                s_ref[0, 0, 2 * p, :, :] = fin[:HEAD_DIM, :HEAD_DIM].T
                s_ref[0, 0, 2 * p + 1, :, :] = fin[HEAD_DIM:, HEAD_DIM:].T

    o = accf_ref[...] + accb_ref[...]
    o = o * lax.rsqrt(_head_mean_sq(o) + EPS) * norm_ref[...]
    og_ref[...] = (o * sgg_ref[...].astype(F32)).astype(BF16)


def _gla_constants():
    i = np.arange(GROUP)[:, None]
    j = np.arange(GROUP)[None, :]
    same = (i // CHUNK) == (j // CHUNK)
    tri_f = same & (j <= i)
    tri_b = same & (j >= i)
    r = np.arange(4 * CHUNK)[:, None]
    l = np.arange(LANES)[None, :]
    same = (r // (2 * CHUNK)) == (l // CHUNK)
    keep_f = same & ((l % CHUNK) <= (r % CHUNK))
    keep_b = same & ((l % CHUNK) >= (r % CHUNK))
    return tuple(jnp.asarray(m.astype(np.float32), dtype=BF16) for m in (tri_f, tri_b, keep_f, keep_b))


def _gla(gq, gk, gv, gf, gb, sgg, norm_row, s0f, s0b, n_seq, seq_len, row_block0, emit_state, cast_weights=()):
    has_state = s0f is not None
    steps = n_seq
    blk = pl.BlockSpec((seq_len, GLA_WIDTH), lambda b: (row_block0 + b, 0))
    out_blk = pl.BlockSpec((seq_len, GLA_WIDTH), lambda b: (b, 0))
    st_spec = pl.BlockSpec((1, 1, GLA_HEADS, HEAD_DIM, HEAD_DIM), lambda b: (b, 0, 0, 0, 0))
    const = lambda shape: pl.BlockSpec(shape, lambda b: (0, 0))
    in_specs = ([blk] * 6 + [const((1, GLA_WIDTH))] + [const((GROUP, GROUP))] * 2
                + [const((4 * CHUNK, LANES))] * 2)
    args = [gq, gk, gv, gf, gb, sgg, norm_row, *_gla_constants()]
    if has_state:
        in_specs += [st_spec] * 2
        args += [s0f, s0b]
    out_specs = [out_blk]
    out_shape = [jax.ShapeDtypeStruct((n_seq * seq_len, GLA_WIDTH), BF16)]
    if emit_state:
        out_specs += [st_spec] * 2
        out_shape += [jax.ShapeDtypeStruct((n_seq, 1, GLA_HEADS, HEAD_DIM, HEAD_DIM), F32)] * 2
    for w in cast_weights:
        rows = w.shape[0] // steps
        spec = pl.BlockSpec((rows, w.shape[1]), lambda b: (b, 0))
        in_specs.append(spec)
        args.append(w)
        out_specs.append(spec)
        out_shape.append(jax.ShapeDtypeStruct(w.shape, BF16))
    return pl.pallas_call(
        functools.partial(_gla_kernel, seq_len=seq_len, has_state=has_state, emit_state=emit_state,
                          n_cast=len(cast_weights)),
        grid=(n_seq,), in_specs=in_specs, out_specs=out_specs, out_shape=out_shape,
        scratch_shapes=[pltpu.VMEM((seq_len, GLA_WIDTH), F32)] * 2,
        compiler_params=_cparams(1), name="gla_bidirectional",
    )(*args)


def _group_attention(q, heads, sink_ref, score_fn, value_fn):
    rows = q.shape[0]
    q_stack = jnp.concatenate([q[:, h * HEAD_DIM:(h + 1) * HEAD_DIM] for h in heads], axis=0)
    scores = score_fn(q_stack)
    yield
    slabs = [slice(j * rows, (j + 1) * rows) for j in range(len(heads))]
    ms = []
    for slab, h in zip(slabs, heads):
        m = sink_ref[h]
        for s in scores:
            m = jnp.maximum(m, jnp.max(s[slab, :], axis=-1, keepdims=True))
        ms.append(m)
    yield
    e_parts = [[] for _ in scores]
    denoms = []
    for slab, h, m in zip(slabs, heads, ms):
        denom = jnp.exp(sink_ref[h] - m)
        for part, s in zip(e_parts, scores):
            e = jnp.exp(s[slab, :] - m)
            denom = denom + jnp.sum(e, axis=-1, keepdims=True)
            part.append(e.astype(BF16))
        denoms.append(denom)
    yield
    acc = value_fn([jnp.concatenate(part, axis=0) for part in e_parts])
    yield
    return [acc[slab, :] / denom for slab, denom in zip(slabs, denoms)]


def _run_staged(units):
    results = [None] * len(units)
    live = list(range(len(units)))
    while live:
        still = []
        for u in live:
            try:
                next(units[u])
                still.append(u)
            except StopIteration as stop:
                results[u] = stop.value
        live = still
    return results


def _ctx_attn_kernel(sink_ref, q_ref, k_ref, v_ref, o_ref):
    q = q_ref[...]
    k = k_ref[...]
    v = v_ref[...]
    units = []
    for g in range(SWA_KV_HEADS):
        k_g = k[:, g * HEAD_DIM:(g + 1) * HEAD_DIM]
        v_g = v[:, g * HEAD_DIM:(g + 1) * HEAD_DIM]
        for h0 in range(SWA_GROUP * g, SWA_GROUP * (g + 1), CTX_HEAD_STACK):
            units.append(_group_attention(
                q, list(range(h0, h0 + CTX_HEAD_STACK)), sink_ref,
                lambda qs, k_g=k_g: [_dot_nt(qs, k_g)],
                lambda es, v_g=v_g: _dot(es[0], v_g)))
    outs = [o for unit_outs in _run_staged(units) for o in unit_outs]
    o_ref[...] = jnp.concatenate(outs, axis=-1).astype(BF16)


def _context_attention(sink, sq, sk, sv, n_seq, seq_len):
    return pl.pallas_call(
        _ctx_attn_kernel,
        grid=(n_seq,),
        in_specs=[pl.BlockSpec(memory_space=pltpu.SMEM),
                  pl.BlockSpec((seq_len, SWA_WIDTH), lambda b: (b, 0)),
                  pl.BlockSpec((seq_len, KV_WIDTH), lambda b: (b, 0)),
                  pl.BlockSpec((seq_len, KV_WIDTH), lambda b: (b, 0))],
        out_specs=pl.BlockSpec((seq_len, SWA_WIDTH), lambda b: (b, 0)),
        out_shape=jax.ShapeDtypeStruct((n_seq * seq_len, SWA_WIDTH), BF16),
        compiler_params=_cparams(1), name="context_attention",
    )(sink, sq, sk, sv)


def _lat_attn_kernel(sink_ref, q_ref, k_ref, v_ref, kc_ref, vc_ref, o_ref, *, seq_len):
    span = QBLOCK + 2 * WINDOW
    i = pl.program_id(1)
    start = jnp.clip(i * QBLOCK - WINDOW, 0, seq_len - span)
    start = pl.multiple_of(start, QBLOCK)
    q = q_ref[...]
    k = k_ref[pl.ds(start, span), :]
    v = v_ref[pl.ds(start, span), :]
    qpos = i * QBLOCK + lax.broadcasted_iota(jnp.int32, (QBLOCK, span), 0)
    kpos = start + lax.broadcasted_iota(jnp.int32, (QBLOCK, span), 1)
    valid = jnp.abs(qpos - kpos) <= WINDOW
    valid = jnp.concatenate([valid] * LAT_HEAD_STACK, axis=0)
    units = []
    for g in range(SWA_KV_HEADS):
        k_g = k[:, g * HEAD_DIM:(g + 1) * HEAD_DIM]
        v_g = v[:, g * HEAD_DIM:(g + 1) * HEAD_DIM]
        kc_t = kc_ref[0, 0, g, :, :].astype(BF16)
        vc_t = vc_ref[0, 0, g, :, :].astype(BF16)
        for h0 in range(SWA_GROUP * g, SWA_GROUP * (g + 1), LAT_HEAD_STACK):
            units.append(_group_attention(
                q, list(range(h0, h0 + LAT_HEAD_STACK)), sink_ref,
                lambda qs, k_g=k_g, kc_t=kc_t: [jnp.where(valid, _dot_nt(qs, k_g), NEG_INF), _dot(qs, kc_t)],
                lambda es, v_g=v_g, vc_t=vc_t: _dot(es[0], v_g) + _dot_nt(es[1], vc_t)))
    outs = [o for unit_outs in _run_staged(units) for o in unit_outs]
    o_ref[...] = jnp.concatenate(outs, axis=-1).astype(BF16)


def _latent_attention(sink, sq, sk, sv, cache_k, cache_v, n_seq, seq_len, row0):
    nq = seq_len // QBLOCK
    q0 = row0 // QBLOCK
    s0 = row0 // seq_len
    past = cache_k.shape[4]
    cache_spec = pl.BlockSpec((1, 1, SWA_KV_HEADS, HEAD_DIM, past), lambda b, i: (b, 0, 0, 0, 0))
    return pl.pallas_call(
        functools.partial(_lat_attn_kernel, seq_len=seq_len),
        grid=(n_seq, nq),
        in_specs=[pl.BlockSpec(memory_space=pltpu.SMEM),
                  pl.BlockSpec((QBLOCK, SWA_WIDTH), lambda b, i: (q0 + b * nq + i, 0)),
                  pl.BlockSpec((seq_len, KV_WIDTH), lambda b, i: (s0 + b, 0)),
                  pl.BlockSpec((seq_len, KV_WIDTH), lambda b, i: (s0 + b, 0)),
                  cache_spec, cache_spec],
        out_specs=pl.BlockSpec((QBLOCK, SWA_WIDTH), lambda b, i: (b * nq + i, 0)),
        out_shape=jax.ShapeDtypeStruct((n_seq * seq_len, SWA_WIDTH), BF16),
        compiler_params=_cparams(2), name="latent_attention",
    )(sink, sq, sk, sv, cache_k, cache_v)


def _out_kernel(xp_ref, xs_ref, ogp_ref, ogs_ref, oswp_ref, osws_ref, mod_ref, gain_ref,
                wo_ref, w1_ref, w2_ref, yp_ref, ys_ref, *, n_prompt_tiles):
    def body(x_ref, og_ref, osw_ref, y_ref):
        mix = _dot(og_ref[...], wo_ref[:GLA_WIDTH, :]) + _dot(osw_ref[...], wo_ref[GLA_WIDTH:, :])
        x1 = x_ref[...] + mod_ref[0, 2:3, :] * mix
        h = _adaln(x1, gain_ref[...], mod_ref[0, 3:4, :], mod_ref[0, 4:5, :]).astype(BF16)
        ff_tile = 1024
        acc = None
        for j in range(D_FF // ff_tile):
            cols = slice(j * ff_tile, (j + 1) * ff_tile)
            a = jnp.maximum(_dot(h, w1_ref[:, cols]), 0.0)
            part = _dot((a * a).astype(BF16), w2_ref[cols, :])
            acc = part if acc is None else acc + part
        y_ref[...] = x1 + mod_ref[0, 5:6, :] * acc

    i = pl.program_id(0)

    @pl.when(i < n_prompt_tiles)
    def _():
        body(xp_ref, ogp_ref, oswp_ref, yp_ref)

    @pl.when(i >= n_prompt_tiles)
    def _():
        body(xs_ref, ogs_ref, osws_ref, ys_ref)


def _mix_out(xp, xs, og_p, og_s, osw_p, osw_s, mod3, gain, wo, w1, w2, seq_s):
    n_p, n_s = xp.shape[0], xs.shape[0]
    tp, ts = n_p // TM, n_s // TM
    tiles_per_seq = seq_s // TM
    const = lambda shape: pl.BlockSpec(shape, lambda i: (0,) * len(shape), pipeline_mode=pl.Buffered(1))
    row_p = lambda w: pl.BlockSpec((TM, w), lambda i: (jnp.minimum(i, tp - 1), 0))
    row_s = lambda w: pl.BlockSpec((TM, w), lambda i: (jnp.maximum(i - tp, 0), 0))
    mod_row = lambda i: jnp.where(i < tp, 0, 1 + (i - tp) // tiles_per_seq)
    return pl.pallas_call(
        functools.partial(_out_kernel, n_prompt_tiles=tp),
        grid=(tp + ts,),
        in_specs=[row_p(D_MODEL), row_s(D_MODEL), row_p(GLA_WIDTH), row_s(GLA_WIDTH),
                  row_p(SWA_WIDTH), row_s(SWA_WIDTH),
                  pl.BlockSpec((1, 6, D_MODEL), lambda i: (mod_row(i), 0, 0)),
                  const((1, D_MODEL)), const((GLA_WIDTH + SWA_WIDTH, D_MODEL)),
                  const((D_MODEL, D_FF)), const((D_FF, D_MODEL))],
        out_specs=[row_p(D_MODEL), row_s(D_MODEL)],
        out_shape=[jax.ShapeDtypeStruct((n_p, D_MODEL), F32), jax.ShapeDtypeStruct((n_s, D_MODEL), F32)],
        compiler_params=_cparams(1), name="out_proj_mlp",
    )(xp, xs, og_p, og_s, osw_p, osw_s, mod3, gain, wo, w1, w2)


def _rope_tables(seq_len):
    axis_dim = HEAD_DIM // 2
    half = axis_dim // 2
    t = np.arange(seq_len)
    rowp = (t // GRID_W).astype(np.float64)
    colp = (t % GRID_W).astype(np.float64)
    freqs = ROPE_THETA ** (-np.arange(half, dtype=np.float64) / half)
    lane = np.arange(LANES)
    d = lane % HEAD_DIM
    posv = np.where((d // axis_dim)[None, :] == 0, rowp[:, None], colp[:, None])
    ang = posv * freqs[d % half][None, :]
    second = ((d % axis_dim) >= half)[None, :]
    cos, sin = np.cos(ang), np.sin(ang)
    zero = np.zeros_like(sin)
    tabs = (cos, np.where(second, sin, zero), np.where(second, zero, -sin))
    return tuple(jnp.asarray(tab.astype(np.float32)) for tab in tabs)


def kernel(x_prompt, x_sample, c, cache_k, cache_v, state_gla_fwd, state_gla_bwd, c_ctx, w_ada, b_ada, norm_mix, norm_ff, w_in, w_gk_fwd, b_gk_fwd, w_gk_bwd, b_gk_bwd, gla_norm, q_norm, k_norm, sink, w_o, w_ff1, w_ff2):
    depth = w_in.shape[0]
    assert depth == 1, "single trunk layer"
    bp, tp, _ = x_prompt.shape
    bs, ts, _ = x_sample.shape
    n_p = bp * tp
    l = 0

    cond = jnp.concatenate([c_ctx[None, :], c], axis=0)
    mod3 = _modulation(cond, w_ada[l], b_ada[l]).reshape(8, 6, D_MODEL)

    wg = jnp.zeros((LANES, 2 * GLA_WIDTH), F32)
    wg = wg.at[:GATE_RANK, :GLA_WIDTH].set(w_gk_fwd[l]).at[GATE_RANK:2 * GATE_RANK, GLA_WIDTH:].set(w_gk_bwd[l])
    wg = wg.astype(BF16)
    bg = jnp.concatenate([b_gk_fwd[l], b_gk_bwd[l]])[None, :]
    qkg = jnp.concatenate([jnp.tile(q_norm[l], SWA_HEADS), jnp.tile(k_norm[l], SWA_KV_HEADS)])[None, :]
    norm2 = jnp.tile(gla_norm[l], GLA_HEADS)[None, :]
    gain_mix = norm_mix[l][None, :]
    gain_ff = norm_ff[l][None, :]
    sink_l = sink[l]

    w_in_t = jnp.transpose(w_in[l])
    cache_k_t = jnp.swapaxes(cache_k, 3, 4)
    cache_v_t = jnp.swapaxes(cache_v, 3, 4)

    xp = x_prompt.reshape(n_p, D_MODEL)
    xs = x_sample.reshape(bs * ts, D_MODEL)
    (gq, gk, gv, sgg, gf, gb, sq, sk, sv, new_k_t, new_v_t) = _project(
        xp, xs, mod3, gain_mix, w_in_t, wg, bg, qkg, _rope_tables(ts), tp, ts)
    new_k = jnp.swapaxes(new_k_t, 3, 4)
    new_v = jnp.swapaxes(new_v_t, 3, 4)

    og_p, s_f, s_b, wo, w1, w2 = _gla(gq, gk, gv, gf, gb, sgg, norm2, None, None, bp, tp, 0, True,
                                      cast_weights=(w_o[l], w_ff1[l], w_ff2[l]))
    osw_p = _context_attention(sink_l, sq, sk, sv, bp, tp)
    (og_s,) = _gla(gq, gk, gv, gf, gb, sgg, norm2, state_gla_fwd, state_gla_bwd, bs, ts, n_p // ts, False)
    osw_s = _latent_attention(sink_l, sq, sk, sv, cache_k_t, cache_v_t, bs, ts, n_p)

    yp, ys = _mix_out(xp, xs, og_p, og_s, osw_p, osw_s, mod3, gain_ff, wo, w1, w2, ts)
    return (yp.reshape(bp, tp, D_MODEL), ys.reshape(bs, ts, D_MODEL), new_k, new_v, s_f, s_b)
```

```python
import functools

import numpy as np
import jax
import jax.numpy as jnp
from jax import lax
from jax.experimental import pallas as pl
from jax.experimental.pallas import tpu as pltpu

F32 = jnp.float32
BF16 = jnp.bfloat16

D_MODEL = 1024
GRID_W = 64
HEAD_DIM = 64
GLA_HEADS = 8
GLA_WIDTH = GLA_HEADS * HEAD_DIM
GATE_RANK = 16
GATE_TEMP = 16.0
CHUNK = 64
GROUP = 4 * CHUNK
SWA_HEADS = 8
SWA_KV_HEADS = 2
SWA_GROUP = SWA_HEADS // SWA_KV_HEADS
SWA_WIDTH = SWA_HEADS * HEAD_DIM
KV_WIDTH = SWA_KV_HEADS * HEAD_DIM
WINDOW = 128
QBLOCK = 128
D_FF = 4 * D_MODEL
ROPE_THETA = 10000.0
EPS = 1e-6
NEG_INF = -1e30
LANES = 128
TM = 512

D_IN = 2848
R_GQ, R_GK, R_GV, R_GG, R_LR = 0, 512, 1024, 1536, 2048
R_SQ = R_LR + 2 * GATE_RANK
R_SV = R_SQ + SWA_WIDTH + KV_WIDTH

VMEM_LIMIT = 56 * 1024 * 1024


def _cparams(n_axes, flags=None):
    return pltpu.CompilerParams(dimension_semantics=("arbitrary",) * n_axes,
                                vmem_limit_bytes=VMEM_LIMIT, flags=flags)


def _mod_kernel(condt_ref, w_ref, b_ref, o_ref, *, n_rows):
    k = pl.program_id(0)
    w = w_ref[...]
    rows = []
    for r in range(n_rows):
        cnd = condt_ref[:, r:r + 1]
        s = cnd * jax.nn.sigmoid(cnd)
        rows.append(jnp.sum(w * s, axis=0, keepdims=True))
    rows.append(jnp.zeros((8 - n_rows, w.shape[1]), F32))
    part = jnp.concatenate(rows, axis=0)

    @pl.when(k == 0)
    def _():
        o_ref[...] = part + b_ref[...]

    @pl.when(k > 0)
    def _():
        o_ref[...] += part


def _modulation(cond, w_ada, b_ada):
    n_rows = cond.shape[0]
    condt = jnp.zeros((D_MODEL, 8), F32).at[:, :n_rows].set(cond.T)
    tk = 128
    n_out = w_ada.shape[1]
    return pl.pallas_call(
        functools.partial(_mod_kernel, n_rows=n_rows),
        grid=(D_MODEL // tk,),
        in_specs=[pl.BlockSpec((tk, 8), lambda k: (k, 0)),
                  pl.BlockSpec((tk, n_out), lambda k: (k, 0)),
                  pl.BlockSpec((1, n_out), lambda k: (0, 0))],
        out_specs=pl.BlockSpec((8, n_out), lambda k: (0, 0)),
        out_shape=jax.ShapeDtypeStruct((8, n_out), F32),
        compiler_params=_cparams(1),
        name="modulation",
    )(condt, w_ada, b_ada.reshape(1, n_out))


def _adaln(x, gain, shift, scale):
    ms = jnp.mean(x * x, axis=-1, keepdims=True)
    xn = x * lax.rsqrt(ms + EPS)
    return (xn * gain) * (1.0 + scale) + shift


def _head_mean_sq(y):
    cols = y.shape[1]
    lane = lax.broadcasted_iota(jnp.int32, (1, LANES), 1)
    first = lane < HEAD_DIM
    outs = []
    for p in range(cols // LANES):
        s = y[:, p * LANES:(p + 1) * LANES]
        s = s * s
        tot = jnp.sum(s, axis=-1, keepdims=True)
        lo = jnp.sum(jnp.where(first, s, 0.0), axis=-1, keepdims=True)
        outs.append(jnp.where(first, lo, tot - lo) * (1.0 / HEAD_DIM))
    return outs[0] if len(outs) == 1 else jnp.concatenate(outs, axis=-1)


def _log_sigmoid(x):
    return jnp.minimum(x, 0.0) - jnp.log(1.0 + jnp.exp(-jnp.abs(x)))


def _dot(a, b):
    return jnp.dot(a, b, preferred_element_type=F32)


def _dot_nt(a, b):
    return lax.dot_general(a, b, (((1,), (1,)), ((), ())), preferred_element_type=F32)


def _proj_kernel(xp_ref, xs_ref, mod_ref, gain_ref, w_ref, wg_ref, bg_ref, qkg_ref,
                 cos_ref, sin_a_ref, sin_b_ref,
                 gq_ref, gk_ref, gv_ref, sgg_ref, gf_ref, gb_ref, sq_ref, sk_ref, sv_ref, nk_ref, nv_ref,
                 wb_ref, h0_ref, h1_ref, *, n_prompt_tiles, seqs_per_tile, seq_len):
    h_refs = (h0_ref, h1_ref)
    s = pl.program_id(0)

    @pl.when(s == 0)
    def _():
        blk = 32

        def conv(r, carry):
            rows = pl.ds(pl.multiple_of(r * blk, blk), blk)
            wb_ref[rows, :] = w_ref[rows, :].astype(BF16)
            return carry

        lax.fori_loop(0, D_IN // blk, conv, 0)

    def normalise_next(parity):
        x = jnp.where(s < n_prompt_tiles, xp_ref[...], xs_ref[...])
        h_refs[parity][...] = _adaln(x, gain_ref[...], mod_ref[0, 0:1, :], mod_ref[0, 1:2, :]).astype(BF16)

    def body(rope, emit_cache, parity):
        normalise_next(parity)
        h_ref = h_refs[1 - parity]

        def proj(r0, r1):
            return _dot_nt(h_ref[...], wb_ref[r0:r1, :])

        tail = proj(R_SQ, D_IN)
        lr = proj(R_LR, R_LR + LANES).astype(BF16)
        gate = _log_sigmoid(_dot(lr, wg_ref[...]) + bg_ref[...]) * (1.0 / GATE_TEMP)
        gf_ref[...] = gate[:, :GLA_WIDTH]
        gb_ref[...] = gate[:, GLA_WIDTH:]

        qk = tail[:, :R_SV - R_SQ]
        qk = qk * lax.rsqrt(_head_mean_sq(qk) + EPS) * qkg_ref[...]
        if rope:
            reps = (SWA_WIDTH + KV_WIDTH) // LANES
            cos = jnp.concatenate([cos_ref[...]] * reps, axis=-1)
            sin_a = jnp.concatenate([sin_a_ref[...]] * reps, axis=-1)
            sin_b = jnp.concatenate([sin_b_ref[...]] * reps, axis=-1)
            width = qk.shape[1]
            quarter = HEAD_DIM // 4
            qk = (qk * cos + pltpu.roll(qk, quarter, 1) * sin_a
                  + pltpu.roll(qk, width - quarter, 1) * sin_b)
        sq_ref[...] = (qk[:, :SWA_WIDTH] * (HEAD_DIM ** -0.5)).astype(BF16)
        sk = qk[:, SWA_WIDTH:]
        sk_ref[...] = sk.astype(BF16)
        sv = tail[:, R_SV - R_SQ:]
        sv_ref[...] = sv.astype(BF16)
        if emit_cache:
            sk_t, sv_t = sk.T, sv.T
            for q in range(seqs_per_tile):
                cols = slice(q * seq_len, (q + 1) * seq_len)
                for g in range(SWA_KV_HEADS):
                    feat = slice(g * HEAD_DIM, (g + 1) * HEAD_DIM)
                    nk_ref[q, 0, g, :, :] = sk_t[feat, cols]
                    nv_ref[q, 0, g, :, :] = sv_t[feat, cols]

        gg = proj(R_GG, R_LR)
        sgg_ref[...] = (gg * jax.nn.sigmoid(gg)).astype(BF16)
        gq_ref[...] = (proj(R_GQ, R_GK) * (HEAD_DIM ** -0.5)).astype(BF16)
        gk_ref[...] = proj(R_GK, R_GV).astype(BF16)
        gv_ref[...] = proj(R_GV, R_GG).astype(BF16)

    @pl.when(s == 0)
    def _():
        normalise_next(0)

    for parity in range(2):
        on_parity = (s % 2) == parity

        @pl.when((s >= 1) & (s <= n_prompt_tiles) & on_parity)
        def _():
            body(rope=False, emit_cache=True, parity=parity)

        @pl.when((s > n_prompt_tiles) & on_parity)
        def _():
            body(rope=True, emit_cache=False, parity=parity)


def _project(xp, xs, mod3, gain, w_in, wg, bg, qkg, rope_tabs, seq_p, seq_s):
    n_p, n_s = xp.shape[0], xs.shape[0]
    tp, ts = n_p // TM, n_s // TM
    n = n_p + n_s
    seqs_per_tile = TM // seq_p
    tiles_per_seq = seq_s // TM
    n_seq_p = n_p // seq_p
    const = lambda shape: pl.BlockSpec(shape, lambda s: (0,) * len(shape))
    n_tiles = tp + ts
    in_tile = lambda s: jnp.minimum(s, n_tiles - 1)
    out_tile = lambda s: jnp.maximum(s - 1, 0)
    mod_row = lambda t: jnp.where(t < tp, 0, 1 + (t - tp) // tiles_per_seq)
    rope_spec = pl.BlockSpec((TM, LANES), lambda s: (jnp.maximum(out_tile(s) - tp, 0) % tiles_per_seq, 0))
    in_specs = [pl.BlockSpec((TM, D_MODEL), lambda s: (jnp.minimum(s, tp - 1), 0)),
                pl.BlockSpec((TM, D_MODEL), lambda s: (jnp.clip(s - tp, 0, ts - 1), 0)),
                pl.BlockSpec((1, 6, D_MODEL), lambda s: (mod_row(in_tile(s)), 0, 0)),
                const((1, D_MODEL)),
                pl.BlockSpec((D_IN, D_MODEL), lambda s: (0, 0), pipeline_mode=pl.Buffered(1)),
                const((LANES, 2 * GLA_WIDTH)), const((1, 2 * GLA_WIDTH)), const((1, SWA_WIDTH + KV_WIDTH)),
                rope_spec, rope_spec, rope_spec]
    row = lambda w: pl.BlockSpec((TM, w), lambda s: (out_tile(s), 0))
    cache_spec = pl.BlockSpec((seqs_per_tile, 1, SWA_KV_HEADS, HEAD_DIM, seq_p),
                              lambda s: (jnp.minimum(out_tile(s), tp - 1), 0, 0, 0, 0))
    out_specs = [row(GLA_WIDTH)] * 6 + [row(SWA_WIDTH), row(KV_WIDTH), row(KV_WIDTH)] + [cache_spec] * 2
    cache_shape = (n_seq_p, 1, SWA_KV_HEADS, HEAD_DIM, seq_p)
    out_shape = ([jax.ShapeDtypeStruct((n, GLA_WIDTH), BF16)] * 4
                 + [jax.ShapeDtypeStruct((n, GLA_WIDTH), F32)] * 2
                 + [jax.ShapeDtypeStruct((n, SWA_WIDTH), BF16),
                    jax.ShapeDtypeStruct((n, KV_WIDTH), BF16), jax.ShapeDtypeStruct((n, KV_WIDTH), BF16)]
                 + [jax.ShapeDtypeStruct(cache_shape, F32)] * 2)
    return pl.pallas_call(
        functools.partial(_proj_kernel, n_prompt_tiles=tp, seqs_per_tile=seqs_per_tile, seq_len=seq_p),
        grid=(n_tiles + 1,), in_specs=in_specs, out_specs=out_specs, out_shape=out_shape,
        scratch_shapes=[pltpu.VMEM((D_IN, D_MODEL), BF16)] + [pltpu.VMEM((TM, D_MODEL), BF16)] * 2,
        compiler_params=_cparams(1), name="adaln_in_proj",
    )(xp, xs, mod3, gain, w_in, wg, bg, qkg, *rope_tabs)


def _gla_kernel(*refs, seq_len, seqs, has_state, emit_state, n_cast):
    (q_ref, k_ref, v_ref, gf_ref, gb_ref, sgg_ref, norm_ref,
     trif_ref, trib_ref, keepf_ref, keepb_ref) = refs[:11]
    pos = 11
    if has_state:
        s0f_ref, s0b_ref = refs[pos:pos + 2]
        pos += 2
    cast_in = refs[pos:pos + n_cast]
    pos += n_cast
    og_ref = refs[pos]
    pos += 1
    if emit_state:
        sf_ref, sb_ref = refs[pos:pos + 2]
        pos += 2
    cast_out = refs[pos:pos + n_cast]
    pos += n_cast
    accf_ref, accb_ref = refs[pos:pos + 2]

    for src, dst in zip(cast_in, cast_out):
        dst[...] = src[...].astype(BF16)

    n_groups = seq_len // GROUP
    per_group = GROUP // CHUNK
    n_pairs = q_ref.shape[1] // LANES
    lane = lax.broadcasted_iota(jnp.int32, (1, LANES), 1)
    head0 = lane < HEAD_DIM
    m0 = jnp.where(head0, 1.0, 0.0).astype(BF16)
    m1 = jnp.where(head0, 0.0, 1.0).astype(BF16)
    row_top = lax.broadcasted_iota(jnp.int32, (LANES, 1), 0) < HEAD_DIM

    def stack_heads(x):
        return jnp.concatenate([x * m0, x * m1], axis=0)

    def direction(row0, lanes, g_ref, tri_ref, keep_ref, state, acc_ref, backward):
        rows = pl.ds(row0, GROUP)
        g = g_ref[rows, lanes]
        g_hi = g.astype(BF16)
        g_lo = (g - g_hi.astype(F32)).astype(BF16)
        r = _dot(tri_ref[...], jnp.concatenate([g_hi, g_lo], axis=-1))
        yield
        b = (r[:, :LANES] + r[:, LANES:]).reshape(per_group, CHUNK, LANES)
        edge = 0 if backward else CHUNK - 1
        mid = CHUNK // 2
        b_edge = b[:, edge:edge + 1, :]
        b_mid = b[:, mid:mid + 1, :]
        q = q_ref[rows, lanes].astype(F32).reshape(per_group, CHUNK, LANES)
        k = k_ref[rows, lanes].astype(F32).reshape(per_group, CHUNK, LANES)
        v = v_ref[rows, lanes]
        q_mid = q * jnp.exp(b - b_mid)
        k_mid = k * jnp.exp(b_mid - b)
        q_in = q_mid.astype(BF16)
        k_in = k_mid.astype(BF16).reshape(GROUP, LANES)
        q_st = (q_mid * jnp.exp(b_mid)).astype(BF16)
        k_st = (k_mid * jnp.exp(b_edge - b_mid)).astype(BF16).reshape(GROUP, LANES)
        decay = jnp.exp(b_edge)
        keep = keep_ref[...]
        pair_rows = 2 * CHUNK
        n_cp = per_group // 2
        yield
        scores, kvs = [], []
        for p in range(n_cp):
            c0 = 2 * p
            prow = slice(p * pair_rows, (p + 1) * pair_rows)
            q_pair = jnp.concatenate([stack_heads(q_in[c0]), stack_heads(q_in[c0 + 1])], axis=0)
            scores.append(_dot_nt(q_pair, k_in[prow, :]))
            v_t = v[prow, :].astype(F32).T.astype(BF16)
            k_pair = k_st[prow, :]
            zero = jnp.zeros_like(k_pair)
            k_blk = jnp.concatenate([jnp.where(row_top, k_pair, zero), jnp.where(row_top, zero, k_pair)], axis=-1)
            kvs.append(_dot(v_t, k_blk))
        yield
        intras = [_dot(scores[p].astype(BF16) * keep, v[p * pair_rows:(p + 1) * pair_rows, :]) for p in range(n_cp)]
        yield
        for p in (range(n_cp - 1, -1, -1) if backward else range(n_cp)):
            for j in ((1, 0) if backward else (0, 1)):
                c = 2 * p + j
                inter = _dot_nt(stack_heads(q_st[c]), state.astype(BF16))
                tot = intras[p][j * pair_rows:(j + 1) * pair_rows, :] + inter
                acc_ref[pl.ds(row0 + c * CHUNK, CHUNK), lanes] = jnp.where(head0, tot[:CHUNK, :], tot[CHUNK:, :])
                state = state * decay[c] + kvs[p][:, j * LANES:(j + 1) * LANES]
            yield
        return state

    def load_state(s_ref, q, p):
        z = jnp.zeros((HEAD_DIM, HEAD_DIM), F32)
        top = jnp.concatenate([s_ref[q, 0, 2 * p, :, :].T, z], axis=-1)
        bot = jnp.concatenate([z, s_ref[q, 0, 2 * p + 1, :, :].T], axis=-1)
        return jnp.concatenate([top, bot], axis=0)

    per_dir = seqs * n_pairs
    if has_state:
        init = tuple(load_state(s_ref, q, p)
                     for s_ref in (s0f_ref, s0b_ref) for q in range(seqs) for p in range(n_pairs))
    else:
        init = (jnp.zeros((LANES, LANES), F32),) * (2 * per_dir)

    def body(t, carry):
        row_f = t * GROUP
        row_b = (n_groups - 1 - t) * GROUP
        if n_groups > 1:
            row_f = pl.multiple_of(row_f, GROUP)
            row_b = pl.multiple_of(row_b, GROUP)
        units = []
        for backward, row, g_ref, tri_ref, keep_ref, acc_ref in (
                (False, row_f, gf_ref, trif_ref, keepf_ref, accf_ref),
                (True, row_b, gb_ref, trib_ref, keepb_ref, accb_ref)):
            for q in range(seqs):
                for p in range(n_pairs):
                    lanes = slice(p * LANES, (p + 1) * LANES)
                    state = carry[backward * per_dir + q * n_pairs + p]
                    units.append(direction(q * seq_len + row, lanes, g_ref, tri_ref, keep_ref, state, acc_ref,
                                           backward))
        return tuple(_run_staged(units))

    if n_groups == 1:
        final = body(0, init)
    else:
        final = lax.fori_loop(0, n_groups, body, init)
    if emit_state:
        for d, s_ref in enumerate((sf_ref, sb_ref)):
            for q in range(seqs):
                for p in range(n_pairs):
                    fin = final[d * per_dir + q * n_pairs + p]
                    s_ref[q, 0, 2 * p, :, :] = fin[:HEAD_DIM, :HEAD_DIM].T
                    s_ref[q, 0, 2 * p + 1, :, :] = fin[HEAD_DIM:, HEAD_DIM:].T

    o = accf_ref[...] + accb_ref[...]
    o = o * lax.rsqrt(_head_mean_sq(o) + EPS) * norm_ref[...]
    og_ref[...] = (o * sgg_ref[...].astype(F32)).astype(BF16)


def _gla_constants():
    i = np.arange(GROUP)[:, None]
    j = np.arange(GROUP)[None, :]
    same = (i // CHUNK) == (j // CHUNK)
    tri_f = same & (j <= i)
    tri_b = same & (j >= i)
    r = np.arange(4 * CHUNK)[:, None]
    l = np.arange(LANES)[None, :]
    same = (r // (2 * CHUNK)) == (l // CHUNK)
    keep_f = same & ((l % CHUNK) <= (r % CHUNK))
    keep_b = same & ((l % CHUNK) >= (r % CHUNK))
    return tuple(jnp.asarray(m.astype(np.float32), dtype=BF16) for m in (tri_f, tri_b, keep_f, keep_b))


def _gla(gq, gk, gv, gf, gb, sgg, norm_row, s0f, s0b, n_seq, seq_len, row_block0, emit_state, seqs=1,
         cast_weights=()):
    has_state = s0f is not None
    steps = n_seq // seqs
    rows_per_step = seqs * seq_len
    blk = pl.BlockSpec((rows_per_step, GLA_WIDTH), lambda b: (row_block0 // seqs + b, 0))
    out_blk = pl.BlockSpec((rows_per_step, GLA_WIDTH), lambda b: (b, 0))
    st_spec = pl.BlockSpec((seqs, 1, GLA_HEADS, HEAD_DIM, HEAD_DIM), lambda b: (b, 0, 0, 0, 0))
    const = lambda shape: pl.BlockSpec(shape, lambda b: (0, 0))
    in_specs = ([blk] * 6 + [const((1, GLA_WIDTH))] + [const((GROUP, GROUP))] * 2
                + [const((4 * CHUNK, LANES))] * 2)
    args = [gq, gk, gv, gf, gb, sgg, norm_row, *_gla_constants()]
    if has_state:
        in_specs += [st_spec] * 2
        args += [s0f, s0b]
    out_specs = [out_blk]
    out_shape = [jax.ShapeDtypeStruct((n_seq * seq_len, GLA_WIDTH), BF16)]
    if emit_state:
        out_specs += [st_spec] * 2
        out_shape += [jax.ShapeDtypeStruct((n_seq, 1, GLA_HEADS, HEAD_DIM, HEAD_DIM), F32)] * 2
    cast_specs, cast_shapes = _cast_specs(cast_weights, steps)
    in_specs += cast_specs
    args += list(cast_weights)
    out_specs += cast_specs
    out_shape += cast_shapes
    return pl.pallas_call(
        functools.partial(_gla_kernel, seq_len=seq_len, seqs=seqs, has_state=has_state, emit_state=emit_state,
                          n_cast=len(cast_weights)),
        grid=(steps,), in_specs=in_specs, out_specs=out_specs, out_shape=out_shape,
        scratch_shapes=[pltpu.VMEM((rows_per_step, GLA_WIDTH), F32)] * 2,
        compiler_params=_cparams(1), name="gla_bidirectional",
    )(*args)


def _group_attention(q, heads, sink_ref, score_fn, value_fn):
    rows = q.shape[0]
    q_stack = jnp.concatenate([q[:, h * HEAD_DIM:(h + 1) * HEAD_DIM] for h in heads], axis=0)
    scores = score_fn(q_stack)
    yield
    slabs = [slice(j * rows, (j + 1) * rows) for j in range(len(heads))]
    ms = []
    for slab, h in zip(slabs, heads):
        m = sink_ref[h]
        for s in scores:
            m = jnp.maximum(m, jnp.max(s[slab, :], axis=-1, keepdims=True))
        ms.append(m)
    yield
    e_parts = [[] for _ in scores]
    denoms = []
    for slab, h, m in zip(slabs, heads, ms):
        denom = jnp.exp(sink_ref[h] - m)
        for part, s in zip(e_parts, scores):
            e = jnp.exp(s[slab, :] - m)
            denom = denom + jnp.sum(e, axis=-1, keepdims=True)
            part.append(e.astype(BF16))
        denoms.append(denom)
    yield
    acc = value_fn([jnp.concatenate(part, axis=0) for part in e_parts])
    yield
    return [acc[slab, :] / denom for slab, denom in zip(slabs, denoms)]


def _run_staged(units):
    results = [None] * len(units)
    live = list(range(len(units)))
    while live:
        still = []
        for u in live:
            try:
                next(units[u])
                still.append(u)
            except StopIteration as stop:
                results[u] = stop.value
        live = still
    return results


def _ctx_attn_kernel(sink_ref, q_ref, k_ref, v_ref, *rest):
    n_cast = (len(rest) - 1) // 2
    cast_in, o_ref, cast_out = rest[:n_cast], rest[n_cast], rest[n_cast + 1:]
    for src, dst in zip(cast_in, cast_out):
        dst[...] = src[...].astype(BF16)
    q = q_ref[...]
    k = k_ref[...]
    v = v_ref[...]
    units = []
    for g in range(SWA_KV_HEADS):
        k_g = k[:, g * HEAD_DIM:(g + 1) * HEAD_DIM]
        v_g = v[:, g * HEAD_DIM:(g + 1) * HEAD_DIM]
        units.append(_group_attention(
            q, list(range(SWA_GROUP * g, SWA_GROUP * (g + 1))), sink_ref,
            lambda qs, k_g=k_g: [_dot_nt(qs, k_g)],
            lambda es, v_g=v_g: _dot(es[0], v_g)))
    outs = [o for unit_outs in _run_staged(units) for o in unit_outs]
    o_ref[...] = jnp.concatenate(outs, axis=-1).astype(BF16)


def _cast_specs(cast_weights, steps):
    specs = [pl.BlockSpec((w.shape[0] // steps, w.shape[1]), lambda b: (b, 0)) for w in cast_weights]
    shapes = [jax.ShapeDtypeStruct(w.shape, BF16) for w in cast_weights]
    return specs, shapes


def _context_attention(sink, sq, sk, sv, n_seq, seq_len, cast_weights=()):
    cast_specs, cast_shapes = _cast_specs(cast_weights, n_seq)
    return pl.pallas_call(
        _ctx_attn_kernel,
        grid=(n_seq,),
        in_specs=[pl.BlockSpec(memory_space=pltpu.SMEM),
                  pl.BlockSpec((seq_len, SWA_WIDTH), lambda b: (b, 0)),
                  pl.BlockSpec((seq_len, KV_WIDTH), lambda b: (b, 0)),
                  pl.BlockSpec((seq_len, KV_WIDTH), lambda b: (b, 0))] + cast_specs,
        out_specs=[pl.BlockSpec((seq_len, SWA_WIDTH), lambda b: (b, 0))] + cast_specs,
        out_shape=[jax.ShapeDtypeStruct((n_seq * seq_len, SWA_WIDTH), BF16)] + cast_shapes,
        compiler_params=_cparams(1), name="context_attention",
    )(sink, sq, sk, sv, *cast_weights)


def _lat_attn_kernel(sink_ref, q_ref, k_ref, v_ref, kc_ref, vc_ref, o_ref, *, seq_len):
    span = QBLOCK + 2 * WINDOW
    i = pl.program_id(1)
    start = jnp.clip(i * QBLOCK - WINDOW, 0, seq_len - span)
    start = pl.multiple_of(start, QBLOCK)
    q = q_ref[...]
    k = k_ref[pl.ds(start, span), :]
    v = v_ref[pl.ds(start, span), :]
    qpos = i * QBLOCK + lax.broadcasted_iota(jnp.int32, (QBLOCK, span), 0)
    kpos = start + lax.broadcasted_iota(jnp.int32, (QBLOCK, span), 1)
    valid = jnp.abs(qpos - kpos) <= WINDOW
    valid = jnp.concatenate([valid] * SWA_GROUP, axis=0)
    units = []
    for g in range(SWA_KV_HEADS):
        k_g = k[:, g * HEAD_DIM:(g + 1) * HEAD_DIM]
        v_g = v[:, g * HEAD_DIM:(g + 1) * HEAD_DIM]
        kc_t = kc_ref[0, 0, g, :, :].astype(BF16)
        vc_t = vc_ref[0, 0, g, :, :].astype(BF16)
        units.append(_group_attention(
            q, list(range(SWA_GROUP * g, SWA_GROUP * (g + 1))), sink_ref,
            lambda qs, k_g=k_g, kc_t=kc_t: [jnp.where(valid, _dot_nt(qs, k_g), NEG_INF), _dot(qs, kc_t)],
            lambda es, v_g=v_g, vc_t=vc_t: _dot(es[0], v_g) + _dot_nt(es[1], vc_t)))
    outs = [o for unit_outs in _run_staged(units) for o in unit_outs]
    o_ref[...] = jnp.concatenate(outs, axis=-1).astype(BF16)


def _latent_attention(sink, sq, sk, sv, cache_k, cache_v, n_seq, seq_len, row0):
    nq = seq_len // QBLOCK
    q0 = row0 // QBLOCK
    s0 = row0 // seq_len
    past = cache_k.shape[4]
    cache_spec = pl.BlockSpec((1, 1, SWA_KV_HEADS, HEAD_DIM, past), lambda b, i: (b, 0, 0, 0, 0))
    return pl.pallas_call(
        functools.partial(_lat_attn_kernel, seq_len=seq_len),
        grid=(n_seq, nq),
        in_specs=[pl.BlockSpec(memory_space=pltpu.SMEM),
                  pl.BlockSpec((QBLOCK, SWA_WIDTH), lambda b, i: (q0 + b * nq + i, 0)),
                  pl.BlockSpec((seq_len, KV_WIDTH), lambda b, i: (s0 + b, 0)),
                  pl.BlockSpec((seq_len, KV_WIDTH), lambda b, i: (s0 + b, 0)),
                  cache_spec, cache_spec],
        out_specs=pl.BlockSpec((QBLOCK, SWA_WIDTH), lambda b, i: (b * nq + i, 0)),
        out_shape=jax.ShapeDtypeStruct((n_seq * seq_len, SWA_WIDTH), BF16),
        compiler_params=_cparams(2), name="latent_attention",
    )(sink, sq, sk, sv, cache_k, cache_v)


def _out_kernel(xp_ref, xs_ref, ogp_ref, ogs_ref, oswp_ref, osws_ref, mod_ref, gain_ref,
                wo_ref, w1_ref, w2_ref, yp_ref, ys_ref, *, n_prompt_tiles):
    def body(x_ref, og_ref, osw_ref, y_ref):
        mix = _dot(og_ref[...], wo_ref[:GLA_WIDTH, :]) + _dot(osw_ref[...], wo_ref[GLA_WIDTH:, :])
        x1 = x_ref[...] + mod_ref[0, 2:3, :] * mix
        h = _adaln(x1, gain_ref[...], mod_ref[0, 3:4, :], mod_ref[0, 4:5, :]).astype(BF16)
        ff_tile = 1024
        acc = None
        for j in range(D_FF // ff_tile):
            cols = slice(j * ff_tile, (j + 1) * ff_tile)
            a = jnp.maximum(_dot(h, w1_ref[:, cols]), 0.0)
            part = _dot((a * a).astype(BF16), w2_ref[cols, :])
            acc = part if acc is None else acc + part
        y_ref[...] = x1 + mod_ref[0, 5:6, :] * acc

    i = pl.program_id(0)

    @pl.when(i < n_prompt_tiles)
    def _():
        body(xp_ref, ogp_ref, oswp_ref, yp_ref)

    @pl.when(i >= n_prompt_tiles)
    def _():
        body(xs_ref, ogs_ref, osws_ref, ys_ref)


def _mix_out(xp, xs, og_p, og_s, osw_p, osw_s, mod3, gain, wo, w1, w2, seq_s):
    n_p, n_s = xp.shape[0], xs.shape[0]
    tp, ts = n_p // TM, n_s // TM
    tiles_per_seq = seq_s // TM
    const = lambda shape: pl.BlockSpec(shape, lambda i: (0,) * len(shape), pipeline_mode=pl.Buffered(1))
    row_p = lambda w: pl.BlockSpec((TM, w), lambda i: (jnp.minimum(i, tp - 1), 0))
    row_s = lambda w: pl.BlockSpec((TM, w), lambda i: (jnp.maximum(i - tp, 0), 0))
    mod_row = lambda i: jnp.where(i < tp, 0, 1 + (i - tp) // tiles_per_seq)
    return pl.pallas_call(
        functools.partial(_out_kernel, n_prompt_tiles=tp),
        grid=(tp + ts,),
        in_specs=[row_p(D_MODEL), row_s(D_MODEL), row_p(GLA_WIDTH), row_s(GLA_WIDTH),
                  row_p(SWA_WIDTH), row_s(SWA_WIDTH),
                  pl.BlockSpec((1, 6, D_MODEL), lambda i: (mod_row(i), 0, 0)),
                  const((1, D_MODEL)), const((GLA_WIDTH + SWA_WIDTH, D_MODEL)),
                  const((D_MODEL, D_FF)), const((D_FF, D_MODEL))],
        out_specs=[row_p(D_MODEL), row_s(D_MODEL)],
        out_shape=[jax.ShapeDtypeStruct((n_p, D_MODEL), F32), jax.ShapeDtypeStruct((n_s, D_MODEL), F32)],
        compiler_params=_cparams(1), name="out_proj_mlp",
    )(xp, xs, og_p, og_s, osw_p, osw_s, mod3, gain, wo, w1, w2)


def _rope_tables(seq_len):
    axis_dim = HEAD_DIM // 2
    half = axis_dim // 2
    t = np.arange(seq_len)
    rowp = (t // GRID_W).astype(np.float64)
    colp = (t % GRID_W).astype(np.float64)
    freqs = ROPE_THETA ** (-np.arange(half, dtype=np.float64) / half)
    lane = np.arange(LANES)
    d = lane % HEAD_DIM
    posv = np.where((d // axis_dim)[None, :] == 0, rowp[:, None], colp[:, None])
    ang = posv * freqs[d % half][None, :]
    second = ((d % axis_dim) >= half)[None, :]
    cos, sin = np.cos(ang), np.sin(ang)
    zero = np.zeros_like(sin)
    tabs = (cos, np.where(second, sin, zero), np.where(second, zero, -sin))
    return tuple(jnp.asarray(tab.astype(np.float32)) for tab in tabs)


def kernel(x_prompt, x_sample, c, cache_k, cache_v, state_gla_fwd, state_gla_bwd, c_ctx, w_ada, b_ada, norm_mix, norm_ff, w_in, w_gk_fwd, b_gk_fwd, w_gk_bwd, b_gk_bwd, gla_norm, q_norm, k_norm, sink, w_o, w_ff1, w_ff2):
    depth = w_in.shape[0]
    assert depth == 1, "single trunk layer"
    bp, tp, _ = x_prompt.shape
    bs, ts, _ = x_sample.shape
    n_p = bp * tp
    l = 0

    cond = jnp.concatenate([c_ctx[None, :], c], axis=0)
    mod3 = _modulation(cond, w_ada[l], b_ada[l]).reshape(8, 6, D_MODEL)

    wg = jnp.zeros((LANES, 2 * GLA_WIDTH), F32)
    wg = wg.at[:GATE_RANK, :GLA_WIDTH].set(w_gk_fwd[l]).at[GATE_RANK:2 * GATE_RANK, GLA_WIDTH:].set(w_gk_bwd[l])
    wg = wg.astype(BF16)
    bg = jnp.concatenate([b_gk_fwd[l], b_gk_bwd[l]])[None, :]
    qkg = jnp.concatenate([jnp.tile(q_norm[l], SWA_HEADS), jnp.tile(k_norm[l], SWA_KV_HEADS)])[None, :]
    norm2 = jnp.tile(gla_norm[l], GLA_HEADS)[None, :]
    gain_mix = norm_mix[l][None, :]
    gain_ff = norm_ff[l][None, :]
    sink_l = sink[l]

    w_in_t = jnp.transpose(w_in[l])
    cache_k_t = jnp.swapaxes(cache_k, 3, 4)
    cache_v_t = jnp.swapaxes(cache_v, 3, 4)

    xp = x_prompt.reshape(n_p, D_MODEL)
    xs = x_sample.reshape(bs * ts, D_MODEL)
    (gq, gk, gv, sgg, gf, gb, sq, sk, sv, new_k_t, new_v_t) = _project(
        xp, xs, mod3, gain_mix, w_in_t, wg, bg, qkg, _rope_tables(ts), tp, ts)
    new_k = jnp.swapaxes(new_k_t, 3, 4)
    new_v = jnp.swapaxes(new_v_t, 3, 4)

    og_p, s_f, s_b, w1 = _gla(gq, gk, gv, gf, gb, sgg, norm2, None, None, bp, tp, 0, True, seqs=2,
                              cast_weights=(w_ff1[l],))
    osw_p, wo, w2 = _context_attention(sink_l, sq, sk, sv, bp, tp, cast_weights=(w_o[l], w_ff2[l]))
    (og_s,) = _gla(gq, gk, gv, gf, gb, sgg, norm2, state_gla_fwd, state_gla_bwd, bs, ts, n_p // ts, False)
    osw_s = _latent_attention(sink_l, sq, sk, sv, cache_k_t, cache_v_t, bs, ts, n_p)

    yp, ys = _mix_out(xp, xs, og_p, og_s, osw_p, osw_s, mod3, gain_ff, wo, w1, w2, ts)
    return (yp.reshape(bp, tp, D_MODEL), ys.reshape(bs, ts, D_MODEL), new_k, new_v, s_f, s_b)
```

```python
import functools

import numpy as np
import jax
import jax.numpy as jnp
from jax import lax
from jax.experimental import pallas as pl
from jax.experimental.pallas import tpu as pltpu

F32 = jnp.float32
BF16 = jnp.bfloat16

D_MODEL = 1024
GRID_W = 64
HEAD_DIM = 64
GLA_HEADS = 8
GLA_WIDTH = GLA_HEADS * HEAD_DIM
GATE_RANK = 16
GATE_TEMP = 16.0
CHUNK = 64
GROUP = 4 * CHUNK
SWA_HEADS = 8
SWA_KV_HEADS = 2
SWA_GROUP = SWA_HEADS // SWA_KV_HEADS
SWA_WIDTH = SWA_HEADS * HEAD_DIM
KV_WIDTH = SWA_KV_HEADS * HEAD_DIM
WINDOW = 128
QBLOCK = 128
D_FF = 4 * D_MODEL
ROPE_THETA = 10000.0
EPS = 1e-6
NEG_INF = -1e30
LANES = 128
TM = 512

D_IN = 2848
R_GQ, R_GK, R_GV, R_GG, R_LR = 0, 512, 1024, 1536, 2048
R_SQ = R_LR + 2 * GATE_RANK
R_SV = R_SQ + SWA_WIDTH + KV_WIDTH

VMEM_LIMIT = 56 * 1024 * 1024


def _cparams(n_axes, flags=None):
    return pltpu.CompilerParams(dimension_semantics=("arbitrary",) * n_axes,
                                vmem_limit_bytes=VMEM_LIMIT, flags=flags)


def _mod_kernel(cond_ref, w_ref, b_ref, o_ref):
    k = pl.program_id(0)
    cnd = cond_ref[...]
    s = (cnd * jax.nn.sigmoid(cnd)).astype(BF16)
    part = _dot(s, w_ref[...].astype(BF16))

    @pl.when(k == 0)
    def _():
        o_ref[...] = part + b_ref[...]

    @pl.when(k > 0)
    def _():
        o_ref[...] += part


def _modulation(cond, w_ada, b_ada):
    tk = 128
    n_out = w_ada.shape[1]
    return pl.pallas_call(
        _mod_kernel,
        grid=(D_MODEL // tk,),
        in_specs=[pl.BlockSpec((8, tk), lambda k: (0, k)),
                  pl.BlockSpec((tk, n_out), lambda k: (k, 0)),
                  pl.BlockSpec((1, n_out), lambda k: (0, 0))],
        out_specs=pl.BlockSpec((8, n_out), lambda k: (0, 0)),
        out_shape=jax.ShapeDtypeStruct((8, n_out), F32),
        compiler_params=_cparams(1),
        name="modulation",
    )(cond, w_ada, b_ada.reshape(1, n_out))


def _mod_row(tile, n_prompt_tiles, tiles_per_seq):
    return jnp.where(tile < n_prompt_tiles, 0, 1 + (tile - n_prompt_tiles) // tiles_per_seq)


def _mod_chunk(mod_ref, row, j):
    return mod_ref[pl.ds(row, 1), j * D_MODEL:(j + 1) * D_MODEL]


def _adaln(x, gain, shift, scale):
    ms = jnp.mean(x * x, axis=-1, keepdims=True)
    xn = x * lax.rsqrt(ms + EPS)
    return (xn * gain) * (1.0 + scale) + shift


def _head_mean_sq(y):
    cols = y.shape[1]
    lane = lax.broadcasted_iota(jnp.int32, (1, LANES), 1)
    first = lane < HEAD_DIM
    outs = []
    for p in range(cols // LANES):
        s = y[:, p * LANES:(p + 1) * LANES]
        s = s * s
        tot = jnp.sum(s, axis=-1, keepdims=True)
        lo = jnp.sum(jnp.where(first, s, 0.0), axis=-1, keepdims=True)
        outs.append(jnp.where(first, lo, tot - lo) * (1.0 / HEAD_DIM))
    return outs[0] if len(outs) == 1 else jnp.concatenate(outs, axis=-1)


def _log_sigmoid(x):
    return jnp.minimum(x, 0.0) - jnp.log(1.0 + jnp.exp(-jnp.abs(x)))


def _dot(a, b):
    return jnp.dot(a, b, preferred_element_type=F32)


def _dot_nt(a, b):
    return lax.dot_general(a, b, (((1,), (1,)), ((), ())), preferred_element_type=F32)


def _proj_kernel(xp_ref, xs_ref, mod_ref, gain_ref, w_ref, wg_ref, bg_ref, qkg_ref,
                 cos_ref, sin_a_ref, sin_b_ref,
                 gq_ref, gk_ref, gv_ref, sgg_ref, gf_ref, gb_ref, sq_ref, sk_ref, sv_ref, nk_ref, nv_ref,
                 wb_ref, h0_ref, h1_ref, *, n_prompt_tiles, n_tiles, tiles_per_seq, seqs_per_tile, seq_len):
    h_refs = (h0_ref, h1_ref)
    s = pl.program_id(0)

    @pl.when(s == 0)
    def _():
        blk = 32

        def conv(r, carry):
            rows = pl.ds(pl.multiple_of(r * blk, blk), blk)
            wb_ref[rows, :] = w_ref[rows, :].astype(BF16)
            return carry

        lax.fori_loop(0, D_IN // blk, conv, 0)

    def normalise_next(parity):
        x = jnp.where(s < n_prompt_tiles, xp_ref[...], xs_ref[...])
        row = _mod_row(jnp.minimum(s, n_tiles - 1), n_prompt_tiles, tiles_per_seq)
        shift, scale = _mod_chunk(mod_ref, row, 0), _mod_chunk(mod_ref, row, 1)
        h_refs[parity][...] = _adaln(x, gain_ref[...], shift, scale).astype(BF16)

    def body(rope, emit_cache, parity):
        h_ref = h_refs[1 - parity]

        def proj(r0, r1):
            return _dot_nt(h_ref[...], wb_ref[r0:r1, :])

        tail = proj(R_SQ, D_IN)
        lr = proj(R_LR, R_LR + LANES).astype(BF16)
        gate_pre = _dot(lr, wg_ref[...])
        gg = proj(R_GG, R_LR)
        sgg_ref[...] = (gg * jax.nn.sigmoid(gg)).astype(BF16)
        attention_inputs(tail, rope, emit_cache)
        gq_ref[...] = (proj(R_GQ, R_GK) * (HEAD_DIM ** -0.5)).astype(BF16)
        gf_ref[...] = _log_sigmoid(gate_pre[:, :GLA_WIDTH] + bg_ref[:, :GLA_WIDTH]) * (1.0 / GATE_TEMP)
        gk_ref[...] = proj(R_GK, R_GV).astype(BF16)
        gb_ref[...] = _log_sigmoid(gate_pre[:, GLA_WIDTH:] + bg_ref[:, GLA_WIDTH:]) * (1.0 / GATE_TEMP)
        gv_ref[...] = proj(R_GV, R_GG).astype(BF16)
        normalise_next(parity)

    def attention_inputs(tail, rope, emit_cache):
        qk = tail[:, :R_SV - R_SQ]
        qk = qk * lax.rsqrt(_head_mean_sq(qk) + EPS) * qkg_ref[...]
        if rope:
            reps = (SWA_WIDTH + KV_WIDTH) // LANES
            cos = jnp.concatenate([cos_ref[...]] * reps, axis=-1)
            sin_a = jnp.concatenate([sin_a_ref[...]] * reps, axis=-1)
            sin_b = jnp.concatenate([sin_b_ref[...]] * reps, axis=-1)
            width = qk.shape[1]
            quarter = HEAD_DIM // 4
            qk = (qk * cos + pltpu.roll(qk, quarter, 1) * sin_a
                  + pltpu.roll(qk, width - quarter, 1) * sin_b)
        sq_ref[...] = (qk[:, :SWA_WIDTH] * (HEAD_DIM ** -0.5)).astype(BF16)
        sk = qk[:, SWA_WIDTH:]
        sk_ref[...] = sk.astype(BF16)
        sv = tail[:, R_SV - R_SQ:]
        sv_ref[...] = sv.astype(BF16)
        if emit_cache:
            sk_t, sv_t = sk.T, sv.T
            for q in range(seqs_per_tile):
                cols = slice(q * seq_len, (q + 1) * seq_len)
                for g in range(SWA_KV_HEADS):
                    feat = slice(g * HEAD_DIM, (g + 1) * HEAD_DIM)
                    nk_ref[q, 0, g, :, :] = sk_t[feat, cols]
                    nv_ref[q, 0, g, :, :] = sv_t[feat, cols]

    @pl.when(s == 0)
    def _():
        normalise_next(0)

    for parity in range(2):
        on_parity = (s % 2) == parity

        @pl.when((s >= 1) & (s <= n_prompt_tiles) & on_parity)
        def _():
            body(rope=False, emit_cache=True, parity=parity)

        @pl.when((s > n_prompt_tiles) & on_parity)
        def _():
            body(rope=True, emit_cache=False, parity=parity)


def _project(xp, xs, mod, gain, w_in, wg, bg, qkg, rope_tabs, seq_p, seq_s):
    n_p, n_s = xp.shape[0], xs.shape[0]
    tp, ts = n_p // TM, n_s // TM
    n = n_p + n_s
    seqs_per_tile = TM // seq_p
    tiles_per_seq = seq_s // TM
    n_seq_p = n_p // seq_p
    const = lambda shape: pl.BlockSpec(shape, lambda s: (0,) * len(shape))
    n_tiles = tp + ts
    out_tile = lambda s: jnp.maximum(s - 1, 0)
    rope_spec = pl.BlockSpec((TM, LANES), lambda s: (jnp.maximum(out_tile(s) - tp, 0) % tiles_per_seq, 0))
    in_specs = [pl.BlockSpec((TM, D_MODEL), lambda s: (jnp.minimum(s, tp - 1), 0)),
                pl.BlockSpec((TM, D_MODEL), lambda s: (jnp.clip(s - tp, 0, ts - 1), 0)),
                const(mod.shape),
                const((1, D_MODEL)),
                pl.BlockSpec((D_IN, D_MODEL), lambda s: (0, 0), pipeline_mode=pl.Buffered(1)),
                const((LANES, 2 * GLA_WIDTH)), const((1, 2 * GLA_WIDTH)), const((1, SWA_WIDTH + KV_WIDTH)),
                rope_spec, rope_spec, rope_spec]
    row = lambda w: pl.BlockSpec((TM, w), lambda s: (out_tile(s), 0))
    cache_spec = pl.BlockSpec((seqs_per_tile, 1, SWA_KV_HEADS, HEAD_DIM, seq_p),
                              lambda s: (jnp.minimum(out_tile(s), tp - 1), 0, 0, 0, 0))
    out_specs = [row(GLA_WIDTH)] * 6 + [row(SWA_WIDTH), row(KV_WIDTH), row(KV_WIDTH)] + [cache_spec] * 2
    cache_shape = (n_seq_p, 1, SWA_KV_HEADS, HEAD_DIM, seq_p)
    out_shape = ([jax.ShapeDtypeStruct((n, GLA_WIDTH), BF16)] * 4
                 + [jax.ShapeDtypeStruct((n, GLA_WIDTH), F32)] * 2
                 + [jax.ShapeDtypeStruct((n, SWA_WIDTH), BF16),
                    jax.ShapeDtypeStruct((n, KV_WIDTH), BF16), jax.ShapeDtypeStruct((n, KV_WIDTH), BF16)]
                 + [jax.ShapeDtypeStruct(cache_shape, F32)] * 2)
    return pl.pallas_call(
        functools.partial(_proj_kernel, n_prompt_tiles=tp, n_tiles=n_tiles, tiles_per_seq=tiles_per_seq,
                          seqs_per_tile=seqs_per_tile, seq_len=seq_p),
        grid=(n_tiles + 1,), in_specs=in_specs, out_specs=out_specs, out_shape=out_shape,
        scratch_shapes=[pltpu.VMEM((D_IN, D_MODEL), BF16)] + [pltpu.VMEM((TM, D_MODEL), BF16)] * 2,
        compiler_params=_cparams(1), name="adaln_in_proj",
    )(xp, xs, mod, gain, w_in, wg, bg, qkg, *rope_tabs)


def _gla_kernel(*refs, seq_len, seqs, has_state, emit_state, n_cast):
    (q_ref, k_ref, v_ref, gf_ref, gb_ref, sgg_ref, norm_ref,
     trif_ref, trib_ref, keepf_ref, keepb_ref) = refs[:11]
    pos = 11
    if has_state:
        s0f_ref, s0b_ref = refs[pos:pos + 2]
        pos += 2
    cast_in = refs[pos:pos + n_cast]
    pos += n_cast
    og_ref = refs[pos]
    pos += 1
    if emit_state:
        sf_ref, sb_ref = refs[pos:pos + 2]
        pos += 2
    cast_out = refs[pos:pos + n_cast]
    pos += n_cast
    accf_ref, accb_ref = refs[pos:pos + 2]

    for src, dst in zip(cast_in, cast_out):
        dst[...] = src[...].astype(BF16)

    n_groups = seq_len // GROUP
    per_group = GROUP // CHUNK
    n_pairs = q_ref.shape[1] // LANES
    lane = lax.broadcasted_iota(jnp.int32, (1, LANES), 1)
    head0 = lane < HEAD_DIM
    m0 = jnp.where(head0, 1.0, 0.0).astype(BF16)
    m1 = jnp.where(head0, 0.0, 1.0).astype(BF16)
    row_top = lax.broadcasted_iota(jnp.int32, (LANES, 1), 0) < HEAD_DIM

    def stack_heads(x):
        return jnp.concatenate([x * m0, x * m1], axis=0)

    def direction(row0, lanes, g_ref, tri_ref, keep_ref, state, acc_ref, backward):
        rows = pl.ds(row0, GROUP)
        g = g_ref[rows, lanes]
        g_hi = g.astype(BF16)
        g_lo = (g - g_hi.astype(F32)).astype(BF16)
        r = _dot(tri_ref[...], jnp.concatenate([g_hi, g_lo], axis=-1))
        yield
        b = (r[:, :LANES] + r[:, LANES:]).reshape(per_group, CHUNK, LANES)
        edge = 0 if backward else CHUNK - 1
        mid = CHUNK // 2
        b_edge = b[:, edge:edge + 1, :]
        b_mid = b[:, mid:mid + 1, :]
        q = q_ref[rows, lanes].astype(F32).reshape(per_group, CHUNK, LANES)
        k = k_ref[rows, lanes].astype(F32).reshape(per_group, CHUNK, LANES)
        v = v_ref[rows, lanes]
        q_mid = q * jnp.exp(b - b_mid)
        k_mid = k * jnp.exp(b_mid - b)
        q_in = q_mid.astype(BF16)
        k_in = k_mid.astype(BF16).reshape(GROUP, LANES)
        q_st = (q_mid * jnp.exp(b_mid)).astype(BF16)
        k_st = (k_mid * jnp.exp(b_edge - b_mid)).astype(BF16).reshape(GROUP, LANES)
        decay = jnp.exp(b_edge)
        keep = keep_ref[...]
        pair_rows = 2 * CHUNK
        n_cp = per_group // 2
        yield
        scores, kvs = [], []
        for p in range(n_cp):
            c0 = 2 * p
            prow = slice(p * pair_rows, (p + 1) * pair_rows)
            q_pair = jnp.concatenate([stack_heads(q_in[c0]), stack_heads(q_in[c0 + 1])], axis=0)
            scores.append(_dot_nt(q_pair, k_in[prow, :]))
            v_t = v[prow, :].astype(F32).T.astype(BF16)
            k_pair = k_st[prow, :]
            zero = jnp.zeros_like(k_pair)
            k_blk = jnp.concatenate([jnp.where(row_top, k_pair, zero), jnp.where(row_top, zero, k_pair)], axis=-1)
            kvs.append(_dot(v_t, k_blk))
        yield
        intras = [_dot(scores[p].astype(BF16) * keep, v[p * pair_rows:(p + 1) * pair_rows, :]) for p in range(n_cp)]
        yield
        for p in (range(n_cp - 1, -1, -1) if backward else range(n_cp)):
            for j in ((1, 0) if backward else (0, 1)):
                c = 2 * p + j
                inter = _dot_nt(stack_heads(q_st[c]), state.astype(BF16))
                tot = intras[p][j * pair_rows:(j + 1) * pair_rows, :] + inter
                acc_ref[pl.ds(row0 + c * CHUNK, CHUNK), lanes] = jnp.where(head0, tot[:CHUNK, :], tot[CHUNK:, :])
                state = state * decay[c] + kvs[p][:, j * LANES:(j + 1) * LANES]
            yield
        return state

    def load_state(s_ref, q, p):
        z = jnp.zeros((HEAD_DIM, HEAD_DIM), F32)
        top = jnp.concatenate([s_ref[q, 0, 2 * p, :, :].T, z], axis=-1)
        bot = jnp.concatenate([z, s_ref[q, 0, 2 * p + 1, :, :].T], axis=-1)
        return jnp.concatenate([top, bot], axis=0)

    per_dir = seqs * n_pairs
    if has_state:
        init = tuple(load_state(s_ref, q, p)
                     for s_ref in (s0f_ref, s0b_ref) for q in range(seqs) for p in range(n_pairs))
    else:
        init = (jnp.zeros((LANES, LANES), F32),) * (2 * per_dir)

    def body(t, carry):
        row_f = t * GROUP
        row_b = (n_groups - 1 - t) * GROUP
        if n_groups > 1:
            row_f = pl.multiple_of(row_f, GROUP)
            row_b = pl.multiple_of(row_b, GROUP)
        units = []
        for backward, row, g_ref, tri_ref, keep_ref, acc_ref in (
                (False, row_f, gf_ref, trif_ref, keepf_ref, accf_ref),
                (True, row_b, gb_ref, trib_ref, keepb_ref, accb_ref)):
            for q in range(seqs):
                for p in range(n_pairs):
                    lanes = slice(p * LANES, (p + 1) * LANES)
                    state = carry[backward * per_dir + q * n_pairs + p]
                    units.append(direction(q * seq_len + row, lanes, g_ref, tri_ref, keep_ref, state, acc_ref,
                                           backward))
        return tuple(_run_staged(units))

    if n_groups == 1:
        final = body(0, init)
    else:
        final = lax.fori_loop(0, n_groups, body, init)
    if emit_state:
        for d, s_ref in enumerate((sf_ref, sb_ref)):
            for q in range(seqs):
                for p in range(n_pairs):
                    fin = final[d * per_dir + q * n_pairs + p]
                    s_ref[q, 0, 2 * p, :, :] = fin[:HEAD_DIM, :HEAD_DIM].T
                    s_ref[q, 0, 2 * p + 1, :, :] = fin[HEAD_DIM:, HEAD_DIM:].T

    o = accf_ref[...] + accb_ref[...]
    o = o * lax.rsqrt(_head_mean_sq(o) + EPS) * norm_ref[...]
    og_ref[...] = (o * sgg_ref[...].astype(F32)).astype(BF16)


def _gla_constants():
    i = np.arange(GROUP)[:, None]
    j = np.arange(GROUP)[None, :]
    same = (i // CHUNK) == (j // CHUNK)
    tri_f = same & (j <= i)
    tri_b = same & (j >= i)
    r = np.arange(4 * CHUNK)[:, None]
    l = np.arange(LANES)[None, :]
    same = (r // (2 * CHUNK)) == (l // CHUNK)
    keep_f = same & ((l % CHUNK) <= (r % CHUNK))
    keep_b = same & ((l % CHUNK) >= (r % CHUNK))
    return tuple(jnp.asarray(m.astype(np.float32), dtype=BF16) for m in (tri_f, tri_b, keep_f, keep_b))


def _gla(gq, gk, gv, gf, gb, sgg, norm_row, s0f, s0b, n_seq, seq_len, row_block0, emit_state, seqs=1,
         cast_weights=()):
    has_state = s0f is not None
    steps = n_seq // seqs
    rows_per_step = seqs * seq_len
    blk = pl.BlockSpec((rows_per_step, GLA_WIDTH), lambda b: (row_block0 // seqs + b, 0))
    out_blk = pl.BlockSpec((rows_per_step, GLA_WIDTH), lambda b: (b, 0))
    st_spec = pl.BlockSpec((seqs, 1, GLA_HEADS, HEAD_DIM, HEAD_DIM), lambda b: (b, 0, 0, 0, 0))
    const = lambda shape: pl.BlockSpec(shape, lambda b: (0, 0))
    in_specs = ([blk] * 6 + [const((1, GLA_WIDTH))] + [const((GROUP, GROUP))] * 2
                + [const((4 * CHUNK, LANES))] * 2)
    args = [gq, gk, gv, gf, gb, sgg, norm_row, *_gla_constants()]
    if has_state:
        in_specs += [st_spec] * 2
        args += [s0f, s0b]
    out_specs = [out_blk]
    out_shape = [jax.ShapeDtypeStruct((n_seq * seq_len, GLA_WIDTH), BF16)]
    if emit_state:
        out_specs += [st_spec] * 2
        out_shape += [jax.ShapeDtypeStruct((n_seq, 1, GLA_HEADS, HEAD_DIM, HEAD_DIM), F32)] * 2
    cast_specs, cast_shapes = _cast_specs(cast_weights, steps)
    in_specs += cast_specs
    args += list(cast_weights)
    out_specs += cast_specs
    out_shape += cast_shapes
    return pl.pallas_call(
        functools.partial(_gla_kernel, seq_len=seq_len, seqs=seqs, has_state=has_state, emit_state=emit_state,
                          n_cast=len(cast_weights)),
        grid=(steps,), in_specs=in_specs, out_specs=out_specs, out_shape=out_shape,
        scratch_shapes=[pltpu.VMEM((rows_per_step, GLA_WIDTH), F32)] * 2,
        compiler_params=_cparams(1), name="gla_bidirectional",
    )(*args)


def _group_attention(q, heads, sink_ref, score_fn, value_fn):
    rows = q.shape[0]
    q_stack = jnp.concatenate([q[:, h * HEAD_DIM:(h + 1) * HEAD_DIM] for h in heads], axis=0)
    scores = score_fn(q_stack)
    yield
    slabs = [slice(j * rows, (j + 1) * rows) for j in range(len(heads))]
    ms = []
    for slab, h in zip(slabs, heads):
        m = sink_ref[h]
        for s in scores:
            m = jnp.maximum(m, jnp.max(s[slab, :], axis=-1, keepdims=True))
        ms.append(m)
    yield
    e_parts = [[] for _ in scores]
    denoms = []
    for slab, h, m in zip(slabs, heads, ms):
        denom = jnp.exp(sink_ref[h] - m)
        for part, s in zip(e_parts, scores):
            e = jnp.exp(s[slab, :] - m)
            denom = denom + jnp.sum(e, axis=-1, keepdims=True)
            part.append(e.astype(BF16))
        denoms.append(denom)
    yield
    acc = value_fn([jnp.concatenate(part, axis=0) for part in e_parts])
    yield
    return [acc[slab, :] / denom for slab, denom in zip(slabs, denoms)]


def _run_staged(units):
    results = [None] * len(units)
    live = list(range(len(units)))
    while live:
        still = []
        for u in live:
            try:
                next(units[u])
                still.append(u)
            except StopIteration as stop:
                results[u] = stop.value
        live = still
    return results


def _ctx_attn_kernel(sink_ref, q_ref, k_ref, v_ref, *rest):
    n_cast = (len(rest) - 1) // 2
    cast_in, o_ref, cast_out = rest[:n_cast], rest[n_cast], rest[n_cast + 1:]
    for src, dst in zip(cast_in, cast_out):
        dst[...] = src[...].astype(BF16)
    q = q_ref[...]
    k = k_ref[...]
    v = v_ref[...]
    units = []
    for g in range(SWA_KV_HEADS):
        k_g = k[:, g * HEAD_DIM:(g + 1) * HEAD_DIM]
        v_g = v[:, g * HEAD_DIM:(g + 1) * HEAD_DIM]
        units.append(_group_attention(
            q, list(range(SWA_GROUP * g, SWA_GROUP * (g + 1))), sink_ref,
            lambda qs, k_g=k_g: [_dot_nt(qs, k_g)],
            lambda es, v_g=v_g: _dot(es[0], v_g)))
    outs = [o for unit_outs in _run_staged(units) for o in unit_outs]
    o_ref[...] = jnp.concatenate(outs, axis=-1).astype(BF16)


def _cast_specs(cast_weights, steps):
    specs = [pl.BlockSpec((w.shape[0] // steps, w.shape[1]), lambda b: (b, 0)) for w in cast_weights]
    shapes = [jax.ShapeDtypeStruct(w.shape, BF16) for w in cast_weights]
    return specs, shapes


def _context_attention(sink, sq, sk, sv, n_seq, seq_len, cast_weights=()):
    cast_specs, cast_shapes = _cast_specs(cast_weights, n_seq)
    return pl.pallas_call(
        _ctx_attn_kernel,
        grid=(n_seq,),
        in_specs=[pl.BlockSpec(memory_space=pltpu.SMEM),
                  pl.BlockSpec((seq_len, SWA_WIDTH), lambda b: (b, 0)),
                  pl.BlockSpec((seq_len, KV_WIDTH), lambda b: (b, 0)),
                  pl.BlockSpec((seq_len, KV_WIDTH), lambda b: (b, 0))] + cast_specs,
        out_specs=[pl.BlockSpec((seq_len, SWA_WIDTH), lambda b: (b, 0))] + cast_specs,
        out_shape=[jax.ShapeDtypeStruct((n_seq * seq_len, SWA_WIDTH), BF16)] + cast_shapes,
        compiler_params=_cparams(1), name="context_attention",
    )(sink, sq, sk, sv, *cast_weights)


def _lat_attn_kernel(sink_ref, q_ref, k_ref, v_ref, kc_ref, vc_ref, o_ref, *, seq_len):
    span = QBLOCK + 2 * WINDOW
    i = pl.program_id(1)
    start = jnp.clip(i * QBLOCK - WINDOW, 0, seq_len - span)
    start = pl.multiple_of(start, QBLOCK)
    q = q_ref[...]
    k = k_ref[pl.ds(start, span), :]
    v = v_ref[pl.ds(start, span), :]
    qpos = i * QBLOCK + lax.broadcasted_iota(jnp.int32, (QBLOCK, span), 0)
    kpos = start + lax.broadcasted_iota(jnp.int32, (QBLOCK, span), 1)
    valid = jnp.abs(qpos - kpos) <= WINDOW
    valid = jnp.concatenate([valid] * SWA_GROUP, axis=0)
    units = []
    for g in range(SWA_KV_HEADS):
        k_g = k[:, g * HEAD_DIM:(g + 1) * HEAD_DIM]
        v_g = v[:, g * HEAD_DIM:(g + 1) * HEAD_DIM]
        kc_t = kc_ref[0, 0, g, :, :].astype(BF16)
        vc_t = vc_ref[0, 0, g, :, :].astype(BF16)
        units.append(_group_attention(
            q, list(range(SWA_GROUP * g, SWA_GROUP * (g + 1))), sink_ref,
            lambda qs, k_g=k_g, kc_t=kc_t: [jnp.where(valid, _dot_nt(qs, k_g), NEG_INF), _dot(qs, kc_t)],
            lambda es, v_g=v_g, vc_t=vc_t: _dot(es[0], v_g) + _dot_nt(es[1], vc_t)))
    outs = [o for unit_outs in _run_staged(units) for o in unit_outs]
    o_ref[...] = jnp.concatenate(outs, axis=-1).astype(BF16)


def _latent_attention(sink, sq, sk, sv, cache_k, cache_v, n_seq, seq_len, row0):
    nq = seq_len // QBLOCK
    q0 = row0 // QBLOCK
    s0 = row0 // seq_len
    past = cache_k.shape[4]
    cache_spec = pl.BlockSpec((1, 1, SWA_KV_HEADS, HEAD_DIM, past), lambda b, i: (b, 0, 0, 0, 0))
    return pl.pallas_call(
        functools.partial(_lat_attn_kernel, seq_len=seq_len),
        grid=(n_seq, nq),
        in_specs=[pl.BlockSpec(memory_space=pltpu.SMEM),
                  pl.BlockSpec((QBLOCK, SWA_WIDTH), lambda b, i: (q0 + b * nq + i, 0)),
                  pl.BlockSpec((seq_len, KV_WIDTH), lambda b, i: (s0 + b, 0)),
                  pl.BlockSpec((seq_len, KV_WIDTH), lambda b, i: (s0 + b, 0)),
                  cache_spec, cache_spec],
        out_specs=pl.BlockSpec((QBLOCK, SWA_WIDTH), lambda b, i: (b * nq + i, 0)),
        out_shape=jax.ShapeDtypeStruct((n_seq * seq_len, SWA_WIDTH), BF16),
        compiler_params=_cparams(2), name="latent_attention",
    )(sink, sq, sk, sv, cache_k, cache_v)


def _out_kernel(xp_ref, xs_ref, ogp_ref, ogs_ref, oswp_ref, osws_ref, mod_ref, gain_ref,
                wo_ref, w1_ref, w2_ref, yp_ref, ys_ref, *, n_prompt_tiles, tiles_per_seq):
    i = pl.program_id(0)
    row = _mod_row(i, n_prompt_tiles, tiles_per_seq)

    def body(x_ref, og_ref, osw_ref, y_ref):
        mix = _dot(og_ref[...], wo_ref[:GLA_WIDTH, :]) + _dot(osw_ref[...], wo_ref[GLA_WIDTH:, :])
        x1 = x_ref[...] + _mod_chunk(mod_ref, row, 2) * mix
        h = _adaln(x1, gain_ref[...], _mod_chunk(mod_ref, row, 3), _mod_chunk(mod_ref, row, 4)).astype(BF16)
        ff_tile = 1024
        acc = None
        for j in range(D_FF // ff_tile):
            cols = slice(j * ff_tile, (j + 1) * ff_tile)
            a = jnp.maximum(_dot(h, w1_ref[:, cols]), 0.0)
            part = _dot((a * a).astype(BF16), w2_ref[cols, :])
            acc = part if acc is None else acc + part
        y_ref[...] = x1 + _mod_chunk(mod_ref, row, 5) * acc

    @pl.when(i < n_prompt_tiles)
    def _():
        body(xp_ref, ogp_ref, oswp_ref, yp_ref)

    @pl.when(i >= n_prompt_tiles)
    def _():
        body(xs_ref, ogs_ref, osws_ref, ys_ref)


def _mix_out(xp, xs, og_p, og_s, osw_p, osw_s, mod, gain, wo, w1, w2, seq_s):
    n_p, n_s = xp.shape[0], xs.shape[0]
    tp, ts = n_p // TM, n_s // TM
    tiles_per_seq = seq_s // TM
    const = lambda shape: pl.BlockSpec(shape, lambda i: (0,) * len(shape), pipeline_mode=pl.Buffered(1))
    row_p = lambda w: pl.BlockSpec((TM, w), lambda i: (jnp.minimum(i, tp - 1), 0))
    row_s = lambda w: pl.BlockSpec((TM, w), lambda i: (jnp.maximum(i - tp, 0), 0))
    return pl.pallas_call(
        functools.partial(_out_kernel, n_prompt_tiles=tp, tiles_per_seq=tiles_per_seq),
        grid=(tp + ts,),
        in_specs=[row_p(D_MODEL), row_s(D_MODEL), row_p(GLA_WIDTH), row_s(GLA_WIDTH),
                  row_p(SWA_WIDTH), row_s(SWA_WIDTH),
                  const(mod.shape),
                  const((1, D_MODEL)), const((GLA_WIDTH + SWA_WIDTH, D_MODEL)),
                  const((D_MODEL, D_FF)), const((D_FF, D_MODEL))],
        out_specs=[row_p(D_MODEL), row_s(D_MODEL)],
        out_shape=[jax.ShapeDtypeStruct((n_p, D_MODEL), F32), jax.ShapeDtypeStruct((n_s, D_MODEL), F32)],
        compiler_params=_cparams(1), name="out_proj_mlp",
    )(xp, xs, og_p, og_s, osw_p, osw_s, mod, gain, wo, w1, w2)


def _rope_tables(seq_len):
    axis_dim = HEAD_DIM // 2
    half = axis_dim // 2
    t = np.arange(seq_len)
    rowp = (t // GRID_W).astype(np.float64)
    colp = (t % GRID_W).astype(np.float64)
    freqs = ROPE_THETA ** (-np.arange(half, dtype=np.float64) / half)
    lane = np.arange(LANES)
    d = lane % HEAD_DIM
    posv = np.where((d // axis_dim)[None, :] == 0, rowp[:, None], colp[:, None])
    ang = posv * freqs[d % half][None, :]
    second = ((d % axis_dim) >= half)[None, :]
    cos, sin = np.cos(ang), np.sin(ang)
    zero = np.zeros_like(sin)
    tabs = (cos, np.where(second, sin, zero), np.where(second, zero, -sin))
    return tuple(jnp.asarray(tab.astype(np.float32)) for tab in tabs)


def kernel(x_prompt, x_sample, c, cache_k, cache_v, state_gla_fwd, state_gla_bwd, c_ctx, w_ada, b_ada, norm_mix, norm_ff, w_in, w_gk_fwd, b_gk_fwd, w_gk_bwd, b_gk_bwd, gla_norm, q_norm, k_norm, sink, w_o, w_ff1, w_ff2):
    depth = w_in.shape[0]
    assert depth == 1, "single trunk layer"
    bp, tp, _ = x_prompt.shape
    bs, ts, _ = x_sample.shape
    n_p = bp * tp
    l = 0

    cond = jnp.concatenate([c_ctx[None, :], c, jnp.zeros((8 - 1 - bs, D_MODEL), F32)], axis=0)
    mod = _modulation(cond, w_ada[l], b_ada[l])

    wg = jnp.zeros((LANES, 2 * GLA_WIDTH), F32)
    wg = wg.at[:GATE_RANK, :GLA_WIDTH].set(w_gk_fwd[l]).at[GATE_RANK:2 * GATE_RANK, GLA_WIDTH:].set(w_gk_bwd[l])
    wg = wg.astype(BF16)
    bg = jnp.concatenate([b_gk_fwd[l], b_gk_bwd[l]])[None, :]
    qkg = jnp.concatenate([jnp.tile(q_norm[l], SWA_HEADS), jnp.tile(k_norm[l], SWA_KV_HEADS)])[None, :]
    norm2 = jnp.tile(gla_norm[l], GLA_HEADS)[None, :]
    gain_mix = norm_mix[l][None, :]
    gain_ff = norm_ff[l][None, :]
    sink_l = sink[l]

    w_in_t = jnp.transpose(w_in[l])
    cache_k_t = jnp.swapaxes(cache_k, 3, 4)
    cache_v_t = jnp.swapaxes(cache_v, 3, 4)

    xp = x_prompt.reshape(n_p, D_MODEL)
    xs = x_sample.reshape(bs * ts, D_MODEL)
    (gq, gk, gv, sgg, gf, gb, sq, sk, sv, new_k_t, new_v_t) = _project(
        xp, xs, mod, gain_mix, w_in_t, wg, bg, qkg, _rope_tables(ts), tp, ts)
    new_k = jnp.swapaxes(new_k_t, 3, 4)
    new_v = jnp.swapaxes(new_v_t, 3, 4)

    og_p, s_f, s_b, w1 = _gla(gq, gk, gv, gf, gb, sgg, norm2, None, None, bp, tp, 0, True, seqs=2,
                              cast_weights=(w_ff1[l],))
    osw_p, wo, w2 = _context_attention(sink_l, sq, sk, sv, bp, tp, cast_weights=(w_o[l], w_ff2[l]))
    (og_s,) = _gla(gq, gk, gv, gf, gb, sgg, norm2, state_gla_fwd, state_gla_bwd, bs, ts, n_p // ts, False)
    osw_s = _latent_attention(sink_l, sq, sk, sv, cache_k_t, cache_v_t, bs, ts, n_p)

    yp, ys = _mix_out(xp, xs, og_p, og_s, osw_p, osw_s, mod, gain_ff, wo, w1, w2, ts)
    return (yp.reshape(bp, tp, D_MODEL), ys.reshape(bs, ts, D_MODEL), new_k, new_v, s_f, s_b)
```

```python
import functools

import numpy as np
import jax
import jax.numpy as jnp
from jax import lax
from jax.experimental import pallas as pl
from jax.experimental.pallas import tpu as pltpu

F32 = jnp.float32
BF16 = jnp.bfloat16

D_MODEL = 1024
GRID_W = 64
HEAD_DIM = 64
GLA_HEADS = 8
GLA_WIDTH = GLA_HEADS * HEAD_DIM
GATE_RANK = 16
GATE_TEMP = 16.0
CHUNK = 64
GROUP = 4 * CHUNK
SWA_HEADS = 8
SWA_KV_HEADS = 2
SWA_GROUP = SWA_HEADS // SWA_KV_HEADS
SWA_WIDTH = SWA_HEADS * HEAD_DIM
KV_WIDTH = SWA_KV_HEADS * HEAD_DIM
WINDOW = 128
QBLOCK = 128
D_FF = 4 * D_MODEL
ROPE_THETA = 10000.0
EPS = 1e-6
NEG_INF = -1e30
LANES = 128
TM = 512

D_IN = 2848
R_GQ, R_GK, R_GV, R_GG, R_LR = 0, 512, 1024, 1536, 2048
R_SQ = R_LR + 2 * GATE_RANK
R_SV = R_SQ + SWA_WIDTH + KV_WIDTH

VMEM_LIMIT = 56 * 1024 * 1024


def _cparams(n_axes, flags=None):
    return pltpu.CompilerParams(dimension_semantics=("arbitrary",) * n_axes,
                                vmem_limit_bytes=VMEM_LIMIT, flags=flags)


def _mod_kernel(cond_ref, w_ref, b_ref, o_ref):
    k = pl.program_id(0)
    cnd = cond_ref[...]
    s = (cnd * jax.nn.sigmoid(cnd)).astype(BF16)
    part = _dot(s, w_ref[...].astype(BF16))

    @pl.when(k == 0)
    def _():
        o_ref[...] = part + b_ref[...]

    @pl.when(k > 0)
    def _():
        o_ref[...] += part


def _modulation(cond, w_ada, b_ada):
    tk = 128
    n_out = w_ada.shape[1]
    return pl.pallas_call(
        _mod_kernel,
        grid=(D_MODEL // tk,),
        in_specs=[pl.BlockSpec((8, tk), lambda k: (0, k)),
                  pl.BlockSpec((tk, n_out), lambda k: (k, 0)),
                  pl.BlockSpec((1, n_out), lambda k: (0, 0))],
        out_specs=pl.BlockSpec((8, n_out), lambda k: (0, 0)),
        out_shape=jax.ShapeDtypeStruct((8, n_out), F32),
        compiler_params=_cparams(1),
        name="modulation",
    )(cond, w_ada, b_ada.reshape(1, n_out))


def _mod_row(tile, n_prompt_tiles, tiles_per_seq):
    return jnp.where(tile < n_prompt_tiles, 0, 1 + (tile - n_prompt_tiles) // tiles_per_seq)


def _mod_chunk(mod_ref, row, j):
    return mod_ref[pl.ds(row, 1), j * D_MODEL:(j + 1) * D_MODEL]


def _adaln(x, gain, shift, scale):
    ms = jnp.mean(x * x, axis=-1, keepdims=True)
    xn = x * lax.rsqrt(ms + EPS)
    return (xn * gain) * (1.0 + scale) + shift


def _head_mean_sq(y):
    cols = y.shape[1]
    lane = lax.broadcasted_iota(jnp.int32, (1, LANES), 1)
    first = lane < HEAD_DIM
    outs = []
    for p in range(cols // LANES):
        s = y[:, p * LANES:(p + 1) * LANES]
        s = s * s
        tot = jnp.sum(s, axis=-1, keepdims=True)
        lo = jnp.sum(jnp.where(first, s, 0.0), axis=-1, keepdims=True)
        outs.append(jnp.where(first, lo, tot - lo) * (1.0 / HEAD_DIM))
    return outs[0] if len(outs) == 1 else jnp.concatenate(outs, axis=-1)


def _log_sigmoid(x):
    return jnp.minimum(x, 0.0) - jnp.log(1.0 + jnp.exp(-jnp.abs(x)))


def _dot(a, b):
    return jnp.dot(a, b, preferred_element_type=F32)


def _dot_nt(a, b):
    return lax.dot_general(a, b, (((1,), (1,)), ((), ())), preferred_element_type=F32)


def _proj_kernel(xp_ref, xs_ref, mod_ref, gain_ref, w_ref, wg_ref, bg_ref, qkg_ref,
                 cos_ref, sin_a_ref, sin_b_ref,
                 gq_ref, gk_ref, gv_ref, sgg_ref, gf_ref, gb_ref, sq_ref, sk_ref, sv_ref, nk_ref, nv_ref,
                 wb_ref, h0_ref, h1_ref, *, n_prompt_tiles, n_tiles, tiles_per_seq, seqs_per_tile, seq_len):
    h_refs = (h0_ref, h1_ref)
    s = pl.program_id(0)

    @pl.when(s == 0)
    def _():
        blk = 32

        def conv(r, carry):
            rows = pl.ds(pl.multiple_of(r * blk, blk), blk)
            wb_ref[rows, :] = w_ref[rows, :].astype(BF16)
            return carry

        lax.fori_loop(0, D_IN // blk, conv, 0)

    def normalise_next(parity):
        x = jnp.where(s < n_prompt_tiles, xp_ref[...], xs_ref[...])
        row = _mod_row(jnp.minimum(s, n_tiles - 1), n_prompt_tiles, tiles_per_seq)
        shift, scale = _mod_chunk(mod_ref, row, 0), _mod_chunk(mod_ref, row, 1)
        h_refs[parity][...] = _adaln(x, gain_ref[...], shift, scale).astype(BF16)

    def body(rope, emit_cache, parity):
        h_ref = h_refs[1 - parity]

        def proj(r0, r1):
            return _dot_nt(h_ref[...], wb_ref[r0:r1, :])

        tail = proj(R_SQ, D_IN)
        lr = proj(R_LR, R_LR + LANES).astype(BF16)
        gate_pre = _dot(lr, wg_ref[...])
        gg = proj(R_GG, R_LR)
        sgg_ref[...] = (gg * jax.nn.sigmoid(gg)).astype(BF16)
        attention_inputs(tail, rope, emit_cache)
        gq_ref[...] = (proj(R_GQ, R_GK) * (HEAD_DIM ** -0.5)).astype(BF16)
        gf_ref[...] = _log_sigmoid(gate_pre[:, :GLA_WIDTH] + bg_ref[:, :GLA_WIDTH]) * (1.0 / GATE_TEMP)
        gk_ref[...] = proj(R_GK, R_GV).astype(BF16)
        gb_ref[...] = _log_sigmoid(gate_pre[:, GLA_WIDTH:] + bg_ref[:, GLA_WIDTH:]) * (1.0 / GATE_TEMP)
        gv_ref[...] = proj(R_GV, R_GG).astype(BF16)
        normalise_next(parity)

    def attention_inputs(tail, rope, emit_cache):
        qk = tail[:, :R_SV - R_SQ]
        qk = qk * lax.rsqrt(_head_mean_sq(qk) + EPS) * qkg_ref[...]
        if rope:
            reps = (SWA_WIDTH + KV_WIDTH) // LANES
            cos = jnp.concatenate([cos_ref[...]] * reps, axis=-1)
            sin_a = jnp.concatenate([sin_a_ref[...]] * reps, axis=-1)
            sin_b = jnp.concatenate([sin_b_ref[...]] * reps, axis=-1)
            width = qk.shape[1]
            quarter = HEAD_DIM // 4
            qk = (qk * cos + pltpu.roll(qk, quarter, 1) * sin_a
                  + pltpu.roll(qk, width - quarter, 1) * sin_b)
        sq_ref[...] = (qk[:, :SWA_WIDTH] * (HEAD_DIM ** -0.5)).astype(BF16)
        sk = qk[:, SWA_WIDTH:]
        sk_ref[...] = sk.astype(BF16)
        sv = tail[:, R_SV - R_SQ:]
        sv_ref[...] = sv.astype(BF16)
        if emit_cache:
            sk_t, sv_t = sk.T, sv.T
            for q in range(seqs_per_tile):
                cols = slice(q * seq_len, (q + 1) * seq_len)
                for g in range(SWA_KV_HEADS):
                    feat = slice(g * HEAD_DIM, (g + 1) * HEAD_DIM)
                    nk_ref[q, 0, g, :, :] = sk_t[feat, cols]
                    nv_ref[q, 0, g, :, :] = sv_t[feat, cols]

    @pl.when(s == 0)
    def _():
        normalise_next(0)

    for parity in range(2):
        on_parity = (s % 2) == parity

        @pl.when((s >= 1) & (s <= n_prompt_tiles) & on_parity)
        def _():
            body(rope=False, emit_cache=True, parity=parity)

        @pl.when((s > n_prompt_tiles) & on_parity)
        def _():
            body(rope=True, emit_cache=False, parity=parity)


def _project(xp, xs, mod, gain, w_in, wg, bg, qkg, rope_tabs, seq_p, seq_s):
    n_p, n_s = xp.shape[0], xs.shape[0]
    tp, ts = n_p // TM, n_s // TM
    n = n_p + n_s
    seqs_per_tile = TM // seq_p
    tiles_per_seq = seq_s // TM
    n_seq_p = n_p // seq_p
    const = lambda shape: pl.BlockSpec(shape, lambda s: (0,) * len(shape))
    n_tiles = tp + ts
    out_tile = lambda s: jnp.maximum(s - 1, 0)
    rope_spec = pl.BlockSpec((TM, LANES), lambda s: (jnp.maximum(out_tile(s) - tp, 0) % tiles_per_seq, 0))
    in_specs = [pl.BlockSpec((TM, D_MODEL), lambda s: (jnp.minimum(s, tp - 1), 0)),
                pl.BlockSpec((TM, D_MODEL), lambda s: (jnp.clip(s - tp, 0, ts - 1), 0)),
                const(mod.shape),
                const((1, D_MODEL)),
                pl.BlockSpec((D_IN, D_MODEL), lambda s: (0, 0), pipeline_mode=pl.Buffered(1)),
                const((LANES, 2 * GLA_WIDTH)), const((1, 2 * GLA_WIDTH)), const((1, SWA_WIDTH + KV_WIDTH)),
                rope_spec, rope_spec, rope_spec]
    row = lambda w: pl.BlockSpec((TM, w), lambda s: (out_tile(s), 0))
    cache_spec = pl.BlockSpec((seqs_per_tile, 1, SWA_KV_HEADS, HEAD_DIM, seq_p),
                              lambda s: (jnp.minimum(out_tile(s), tp - 1), 0, 0, 0, 0))
    out_specs = [row(GLA_WIDTH)] * 6 + [row(SWA_WIDTH), row(KV_WIDTH), row(KV_WIDTH)] + [cache_spec] * 2
    cache_shape = (n_seq_p, 1, SWA_KV_HEADS, HEAD_DIM, seq_p)
    out_shape = ([jax.ShapeDtypeStruct((n, GLA_WIDTH), BF16)] * 4
                 + [jax.ShapeDtypeStruct((n, GLA_WIDTH), F32)] * 2
                 + [jax.ShapeDtypeStruct((n, SWA_WIDTH), BF16),
                    jax.ShapeDtypeStruct((n, KV_WIDTH), BF16), jax.ShapeDtypeStruct((n, KV_WIDTH), BF16)]
                 + [jax.ShapeDtypeStruct(cache_shape, F32)] * 2)
    return pl.pallas_call(
        functools.partial(_proj_kernel, n_prompt_tiles=tp, n_tiles=n_tiles, tiles_per_seq=tiles_per_seq,
                          seqs_per_tile=seqs_per_tile, seq_len=seq_p),
        grid=(n_tiles + 1,), in_specs=in_specs, out_specs=out_specs, out_shape=out_shape,
        scratch_shapes=[pltpu.VMEM((D_IN, D_MODEL), BF16)] + [pltpu.VMEM((TM, D_MODEL), BF16)] * 2,
        compiler_params=_cparams(1), name="adaln_in_proj",
    )(xp, xs, mod, gain, w_in, wg, bg, qkg, *rope_tabs)


def _gla_kernel(*refs, seq_len, seqs, has_state, emit_state, with_attention, n_cast):
    (q_ref, k_ref, v_ref, gf_ref, gb_ref, sgg_ref, norm_ref,
     trif_ref, trib_ref, keepf_ref, keepb_ref) = refs[:11]
    pos = 11
    if has_state:
        s0f_ref, s0b_ref = refs[pos:pos + 2]
        pos += 2
    if with_attention:
        sink_ref, aq_ref, ak_ref, av_ref = refs[pos:pos + 4]
        pos += 4
    cast_in = refs[pos:pos + n_cast]
    pos += n_cast
    og_ref = refs[pos]
    pos += 1
    if emit_state:
        sf_ref, sb_ref = refs[pos:pos + 2]
        pos += 2
    if with_attention:
        osw_ref = refs[pos]
        pos += 1
    cast_out = refs[pos:pos + n_cast]
    pos += n_cast
    accf_ref, accb_ref = refs[pos:pos + 2]

    for src, dst in zip(cast_in, cast_out):
        dst[...] = src[...].astype(BF16)

    n_groups = seq_len // GROUP
    per_group = GROUP // CHUNK
    n_pairs = q_ref.shape[1] // LANES
    lane = lax.broadcasted_iota(jnp.int32, (1, LANES), 1)
    head0 = lane < HEAD_DIM
    m0 = jnp.where(head0, 1.0, 0.0).astype(BF16)
    m1 = jnp.where(head0, 0.0, 1.0).astype(BF16)
    row_top = lax.broadcasted_iota(jnp.int32, (LANES, 1), 0) < HEAD_DIM

    def stack_heads(x):
        return jnp.concatenate([x * m0, x * m1], axis=0)

    def direction(row0, lanes, g_ref, tri_ref, keep_ref, state, acc_ref, backward):
        rows = pl.ds(row0, GROUP)
        g = g_ref[rows, lanes]
        g_hi = g.astype(BF16)
        g_lo = (g - g_hi.astype(F32)).astype(BF16)
        r = _dot(tri_ref[...], jnp.concatenate([g_hi, g_lo], axis=-1))
        yield
        b = (r[:, :LANES] + r[:, LANES:]).reshape(per_group, CHUNK, LANES)
        edge = 0 if backward else CHUNK - 1
        mid = CHUNK // 2
        b_edge = b[:, edge:edge + 1, :]
        b_mid = b[:, mid:mid + 1, :]
        q = q_ref[rows, lanes].astype(F32).reshape(per_group, CHUNK, LANES)
        k = k_ref[rows, lanes].astype(F32).reshape(per_group, CHUNK, LANES)
        v = v_ref[rows, lanes]
        q_mid = q * jnp.exp(b - b_mid)
        k_mid = k * jnp.exp(b_mid - b)
        q_in = q_mid.astype(BF16)
        k_in = k_mid.astype(BF16).reshape(GROUP, LANES)
        q_st = (q_mid * jnp.exp(b_mid)).astype(BF16)
        k_st = (k_mid * jnp.exp(b_edge - b_mid)).astype(BF16).reshape(GROUP, LANES)
        decay = jnp.exp(b_edge)
        keep = keep_ref[...]
        pair_rows = 2 * CHUNK
        n_cp = per_group // 2
        yield
        scores, kvs = [], []
        for p in range(n_cp):
            c0 = 2 * p
            prow = slice(p * pair_rows, (p + 1) * pair_rows)
            q_pair = jnp.concatenate([stack_heads(q_in[c0]), stack_heads(q_in[c0 + 1])], axis=0)
            scores.append(_dot_nt(q_pair, k_in[prow, :]))
            v_t = v[prow, :].astype(F32).T.astype(BF16)
            k_pair = k_st[prow, :]
            zero = jnp.zeros_like(k_pair)
            k_blk = jnp.concatenate([jnp.where(row_top, k_pair, zero), jnp.where(row_top, zero, k_pair)], axis=-1)
            kvs.append(_dot(v_t, k_blk))
        yield
        intras = [_dot(scores[p].astype(BF16) * keep, v[p * pair_rows:(p + 1) * pair_rows, :]) for p in range(n_cp)]
        yield
        for p in (range(n_cp - 1, -1, -1) if backward else range(n_cp)):
            for j in ((1, 0) if backward else (0, 1)):
                c = 2 * p + j
                inter = _dot_nt(stack_heads(q_st[c]), state.astype(BF16))
                tot = intras[p][j * pair_rows:(j + 1) * pair_rows, :] + inter
                acc_ref[pl.ds(row0 + c * CHUNK, CHUNK), lanes] = jnp.where(head0, tot[:CHUNK, :], tot[CHUNK:, :])
                state = state * decay[c] + kvs[p][:, j * LANES:(j + 1) * LANES]
            yield
        return state

    def load_state(s_ref, q, p):
        z = jnp.zeros((HEAD_DIM, HEAD_DIM), F32)
        top = jnp.concatenate([s_ref[q, 0, 2 * p, :, :].T, z], axis=-1)
        bot = jnp.concatenate([z, s_ref[q, 0, 2 * p + 1, :, :].T], axis=-1)
        return jnp.concatenate([top, bot], axis=0)

    per_dir = seqs * n_pairs
    if has_state:
        init = tuple(load_state(s_ref, q, p)
                     for s_ref in (s0f_ref, s0b_ref) for q in range(seqs) for p in range(n_pairs))
    else:
        init = (jnp.zeros((LANES, LANES), F32),) * (2 * per_dir)

    def body(t, carry):
        row_f = t * GROUP
        row_b = (n_groups - 1 - t) * GROUP
        if n_groups > 1:
            row_f = pl.multiple_of(row_f, GROUP)
            row_b = pl.multiple_of(row_b, GROUP)
        units = []
        for backward, row, g_ref, tri_ref, keep_ref, acc_ref in (
                (False, row_f, gf_ref, trif_ref, keepf_ref, accf_ref),
                (True, row_b, gb_ref, trib_ref, keepb_ref, accb_ref)):
            for q in range(seqs):
                for p in range(n_pairs):
                    lanes = slice(p * LANES, (p + 1) * LANES)
                    state = carry[backward * per_dir + q * n_pairs + p]
                    units.append(direction(q * seq_len + row, lanes, g_ref, tri_ref, keep_ref, state, acc_ref,
                                           backward))
        return units

    if n_groups == 1:
        units = body(0, init)
        if with_attention:
            for q in range(seqs):
                units += _context_attention_units(sink_ref, aq_ref, ak_ref, av_ref,
                                                  slice(q * seq_len, (q + 1) * seq_len))
        results = _run_staged(units)
        final = results[:2 * per_dir]
        if with_attention:
            for q in range(seqs):
                heads = [o for outs in results[2 * per_dir + q * SWA_KV_HEADS:
                                               2 * per_dir + (q + 1) * SWA_KV_HEADS] for o in outs]
                osw_ref[q * seq_len:(q + 1) * seq_len, :] = jnp.concatenate(heads, axis=-1).astype(BF16)
    else:
        assert not with_attention
        final = lax.fori_loop(0, n_groups, lambda t, carry: tuple(_run_staged(body(t, carry))), init)
    if emit_state:
        for d, s_ref in enumerate((sf_ref, sb_ref)):
            for q in range(seqs):
                for p in range(n_pairs):
                    fin = final[d * per_dir + q * n_pairs + p]
                    s_ref[q, 0, 2 * p, :, :] = fin[:HEAD_DIM, :HEAD_DIM].T
                    s_ref[q, 0, 2 * p + 1, :, :] = fin[HEAD_DIM:, HEAD_DIM:].T

    o = accf_ref[...] + accb_ref[...]
    o = o * lax.rsqrt(_head_mean_sq(o) + EPS) * norm_ref[...]
    og_ref[...] = (o * sgg_ref[...].astype(F32)).astype(BF16)


def _gla_constants():
    i = np.arange(GROUP)[:, None]
    j = np.arange(GROUP)[None, :]
    same = (i // CHUNK) == (j // CHUNK)
    tri_f = same & (j <= i)
    tri_b = same & (j >= i)
    r = np.arange(4 * CHUNK)[:, None]
    l = np.arange(LANES)[None, :]
    same = (r // (2 * CHUNK)) == (l // CHUNK)
    keep_f = same & ((l % CHUNK) <= (r % CHUNK))
    keep_b = same & ((l % CHUNK) >= (r % CHUNK))
    return tuple(jnp.asarray(m.astype(np.float32), dtype=BF16) for m in (tri_f, tri_b, keep_f, keep_b))


def _gla(gq, gk, gv, gf, gb, sgg, norm_row, s0f, s0b, n_seq, seq_len, row_block0, emit_state, seqs=1,
         attention=None, cast_weights=()):
    has_state = s0f is not None
    steps = n_seq // seqs
    rows_per_step = seqs * seq_len
    blk = pl.BlockSpec((rows_per_step, GLA_WIDTH), lambda b: (row_block0 // seqs + b, 0))
    out_blk = pl.BlockSpec((rows_per_step, GLA_WIDTH), lambda b: (b, 0))
    st_spec = pl.BlockSpec((seqs, 1, GLA_HEADS, HEAD_DIM, HEAD_DIM), lambda b: (b, 0, 0, 0, 0))
    const = lambda shape: pl.BlockSpec(shape, lambda b: (0, 0))
    in_specs = ([blk] * 6 + [const((1, GLA_WIDTH))] + [const((GROUP, GROUP))] * 2
                + [const((4 * CHUNK, LANES))] * 2)
    args = [gq, gk, gv, gf, gb, sgg, norm_row, *_gla_constants()]
    if has_state:
        in_specs += [st_spec] * 2
        args += [s0f, s0b]
    out_specs = [out_blk]
    out_shape = [jax.ShapeDtypeStruct((n_seq * seq_len, GLA_WIDTH), BF16)]
    if emit_state:
        out_specs += [st_spec] * 2
        out_shape += [jax.ShapeDtypeStruct((n_seq, 1, GLA_HEADS, HEAD_DIM, HEAD_DIM), F32)] * 2
    if attention is not None:
        sink, sq, sk, sv = attention
        in_specs += [pl.BlockSpec(memory_space=pltpu.SMEM),
                     pl.BlockSpec((rows_per_step, SWA_WIDTH), lambda b: (row_block0 // seqs + b, 0)),
                     pl.BlockSpec((rows_per_step, KV_WIDTH), lambda b: (row_block0 // seqs + b, 0)),
                     pl.BlockSpec((rows_per_step, KV_WIDTH), lambda b: (row_block0 // seqs + b, 0))]
        args += [sink, sq, sk, sv]
        out_specs.append(pl.BlockSpec((rows_per_step, SWA_WIDTH), lambda b: (b, 0)))
        out_shape.append(jax.ShapeDtypeStruct((n_seq * seq_len, SWA_WIDTH), BF16))
    cast_specs, cast_shapes = _cast_specs(cast_weights, steps)
    in_specs += cast_specs
    args += list(cast_weights)
    out_specs += cast_specs
    out_shape += cast_shapes
    return pl.pallas_call(
        functools.partial(_gla_kernel, seq_len=seq_len, seqs=seqs, has_state=has_state, emit_state=emit_state,
                          with_attention=attention is not None, n_cast=len(cast_weights)),
        grid=(steps,), in_specs=in_specs, out_specs=out_specs, out_shape=out_shape,
        scratch_shapes=[pltpu.VMEM((rows_per_step, GLA_WIDTH), F32)] * 2,
        compiler_params=_cparams(1), name="gla_bidirectional",
    )(*args)


def _group_attention(q, heads, sink_ref, score_fn, value_fn):
    rows = q.shape[0]
    q_stack = jnp.concatenate([q[:, h * HEAD_DIM:(h + 1) * HEAD_DIM] for h in heads], axis=0)
    scores = score_fn(q_stack)
    yield
    slabs = [slice(j * rows, (j + 1) * rows) for j in range(len(heads))]
    ms = []
    for slab, h in zip(slabs, heads):
        m = sink_ref[h]
        for s in scores:
            m = jnp.maximum(m, jnp.max(s[slab, :], axis=-1, keepdims=True))
        ms.append(m)
    yield
    e_parts = [[] for _ in scores]
    denoms = []
    for slab, h, m in zip(slabs, heads, ms):
        denom = jnp.exp(sink_ref[h] - m)
        for part, s in zip(e_parts, scores):
            e = jnp.exp(s[slab, :] - m)
            denom = denom + jnp.sum(e, axis=-1, keepdims=True)
            part.append(e.astype(BF16))
        denoms.append(denom)
    yield
    acc = value_fn([jnp.concatenate(part, axis=0) for part in e_parts])
    yield
    return [acc[slab, :] / denom for slab, denom in zip(slabs, denoms)]


def _run_staged(units):
    results = [None] * len(units)
    live = list(range(len(units)))
    while live:
        still = []
        for u in live:
            try:
                next(units[u])
                still.append(u)
            except StopIteration as stop:
                results[u] = stop.value
        live = still
    return results


def _context_attention_units(sink_ref, q_ref, k_ref, v_ref, rows):
    q = q_ref[rows, :]
    k = k_ref[rows, :]
    v = v_ref[rows, :]
    units = []
    for g in range(SWA_KV_HEADS):
        k_g = k[:, g * HEAD_DIM:(g + 1) * HEAD_DIM]
        v_g = v[:, g * HEAD_DIM:(g + 1) * HEAD_DIM]
        units.append(_group_attention(
            q, list(range(SWA_GROUP * g, SWA_GROUP * (g + 1))), sink_ref,
            lambda qs, k_g=k_g: [_dot_nt(qs, k_g)],
            lambda es, v_g=v_g: _dot(es[0], v_g)))
    return units


def _cast_specs(cast_weights, steps, step_of=lambda *idx: idx[0]):
    specs = [pl.BlockSpec((w.shape[0] // steps, w.shape[1]), lambda *idx: (step_of(*idx), 0))
             for w in cast_weights]
    shapes = [jax.ShapeDtypeStruct(w.shape, BF16) for w in cast_weights]
    return specs, shapes


def _lat_attn_kernel(sink_ref, q_ref, k_ref, v_ref, kc_ref, vc_ref, *rest, seq_len):
    n_cast = (len(rest) - 1) // 2
    cast_in, o_ref, cast_out = rest[:n_cast], rest[n_cast], rest[n_cast + 1:]
    for src, dst in zip(cast_in, cast_out):
        dst[...] = src[...].astype(BF16)
    span = QBLOCK + 2 * WINDOW
    i = pl.program_id(1)
    start = jnp.clip(i * QBLOCK - WINDOW, 0, seq_len - span)
    start = pl.multiple_of(start, QBLOCK)
    q = q_ref[...]
    k = k_ref[pl.ds(start, span), :]
    v = v_ref[pl.ds(start, span), :]
    qpos = i * QBLOCK + lax.broadcasted_iota(jnp.int32, (QBLOCK, span), 0)
    kpos = start + lax.broadcasted_iota(jnp.int32, (QBLOCK, span), 1)
    valid = jnp.abs(qpos - kpos) <= WINDOW
    valid = jnp.concatenate([valid] * SWA_GROUP, axis=0)
    units = []
    for g in range(SWA_KV_HEADS):
        k_g = k[:, g * HEAD_DIM:(g + 1) * HEAD_DIM]
        v_g = v[:, g * HEAD_DIM:(g + 1) * HEAD_DIM]
        kc_t = kc_ref[0, 0, g, :, :].astype(BF16)
        vc_t = vc_ref[0, 0, g, :, :].astype(BF16)
        units.append(_group_attention(
            q, list(range(SWA_GROUP * g, SWA_GROUP * (g + 1))), sink_ref,
            lambda qs, k_g=k_g, kc_t=kc_t: [jnp.where(valid, _dot_nt(qs, k_g), NEG_INF), _dot(qs, kc_t)],
            lambda es, v_g=v_g, vc_t=vc_t: _dot(es[0], v_g) + _dot_nt(es[1], vc_t)))
    outs = [o for unit_outs in _run_staged(units) for o in unit_outs]
    o_ref[...] = jnp.concatenate(outs, axis=-1).astype(BF16)


def _latent_attention(sink, sq, sk, sv, cache_k, cache_v, n_seq, seq_len, row0, cast_weights=()):
    nq = seq_len // QBLOCK
    q0 = row0 // QBLOCK
    s0 = row0 // seq_len
    past = cache_k.shape[4]
    cache_spec = pl.BlockSpec((1, 1, SWA_KV_HEADS, HEAD_DIM, past), lambda b, i: (b, 0, 0, 0, 0))
    cast_specs, cast_shapes = _cast_specs(cast_weights, n_seq * nq, lambda b, i: b * nq + i)
    return pl.pallas_call(
        functools.partial(_lat_attn_kernel, seq_len=seq_len),
        grid=(n_seq, nq),
        in_specs=[pl.BlockSpec(memory_space=pltpu.SMEM),
                  pl.BlockSpec((QBLOCK, SWA_WIDTH), lambda b, i: (q0 + b * nq + i, 0)),
                  pl.BlockSpec((seq_len, KV_WIDTH), lambda b, i: (s0 + b, 0)),
                  pl.BlockSpec((seq_len, KV_WIDTH), lambda b, i: (s0 + b, 0)),
                  cache_spec, cache_spec] + cast_specs,
        out_specs=[pl.BlockSpec((QBLOCK, SWA_WIDTH), lambda b, i: (b * nq + i, 0))] + cast_specs,
        out_shape=[jax.ShapeDtypeStruct((n_seq * seq_len, SWA_WIDTH), BF16)] + cast_shapes,
        compiler_params=_cparams(2), name="latent_attention",
    )(sink, sq, sk, sv, cache_k, cache_v, *cast_weights)


def _out_kernel(xp_ref, xs_ref, ogp_ref, ogs_ref, oswp_ref, osws_ref, mod_ref, gain_ref,
                wo_ref, w1_ref, w2_ref, yp_ref, ys_ref, *, n_prompt_tiles, tiles_per_seq):
    i = pl.program_id(0)
    row = _mod_row(i, n_prompt_tiles, tiles_per_seq)

    def body(x_ref, og_ref, osw_ref, y_ref):
        mix = _dot(og_ref[...], wo_ref[:GLA_WIDTH, :]) + _dot(osw_ref[...], wo_ref[GLA_WIDTH:, :])
        x1 = x_ref[...] + _mod_chunk(mod_ref, row, 2) * mix
        h = _adaln(x1, gain_ref[...], _mod_chunk(mod_ref, row, 3), _mod_chunk(mod_ref, row, 4)).astype(BF16)
        ff_tile = 1024
        acc = None
        for j in range(D_FF // ff_tile):
            cols = slice(j * ff_tile, (j + 1) * ff_tile)
            a = jnp.maximum(_dot(h, w1_ref[:, cols]), 0.0)
            part = _dot((a * a).astype(BF16), w2_ref[cols, :])
            acc = part if acc is None else acc + part
        y_ref[...] = x1 + _mod_chunk(mod_ref, row, 5) * acc

    @pl.when(i < n_prompt_tiles)
    def _():
        body(xp_ref, ogp_ref, oswp_ref, yp_ref)

    @pl.when(i >= n_prompt_tiles)
    def _():
        body(xs_ref, ogs_ref, osws_ref, ys_ref)


def _mix_out(xp, xs, og_p, og_s, osw_p, osw_s, mod, gain, wo, w1, w2, seq_s):
    n_p, n_s = xp.shape[0], xs.shape[0]
    tp, ts = n_p // TM, n_s // TM
    tiles_per_seq = seq_s // TM
    const = lambda shape: pl.BlockSpec(shape, lambda i: (0,) * len(shape), pipeline_mode=pl.Buffered(1))
    row_p = lambda w: pl.BlockSpec((TM, w), lambda i: (jnp.minimum(i, tp - 1), 0))
    row_s = lambda w: pl.BlockSpec((TM, w), lambda i: (jnp.maximum(i - tp, 0), 0))
    return pl.pallas_call(
        functools.partial(_out_kernel, n_prompt_tiles=tp, tiles_per_seq=tiles_per_seq),
        grid=(tp + ts,),
        in_specs=[row_p(D_MODEL), row_s(D_MODEL), row_p(GLA_WIDTH), row_s(GLA_WIDTH),
                  row_p(SWA_WIDTH), row_s(SWA_WIDTH),
                  const(mod.shape),
                  const((1, D_MODEL)), const((GLA_WIDTH + SWA_WIDTH, D_MODEL)),
                  const((D_MODEL, D_FF)), const((D_FF, D_MODEL))],
        out_specs=[row_p(D_MODEL), row_s(D_MODEL)],
        out_shape=[jax.ShapeDtypeStruct((n_p, D_MODEL), F32), jax.ShapeDtypeStruct((n_s, D_MODEL), F32)],
        compiler_params=_cparams(1), name="out_proj_mlp",
    )(xp, xs, og_p, og_s, osw_p, osw_s, mod, gain, wo, w1, w2)


def _rope_tables(seq_len):
    axis_dim = HEAD_DIM // 2
    half = axis_dim // 2
    t = np.arange(seq_len)
    rowp = (t // GRID_W).astype(np.float64)
    colp = (t % GRID_W).astype(np.float64)
    freqs = ROPE_THETA ** (-np.arange(half, dtype=np.float64) / half)
    lane = np.arange(LANES)
    d = lane % HEAD_DIM
    posv = np.where((d // axis_dim)[None, :] == 0, rowp[:, None], colp[:, None])
    ang = posv * freqs[d % half][None, :]
    second = ((d % axis_dim) >= half)[None, :]
    cos, sin = np.cos(ang), np.sin(ang)
    zero = np.zeros_like(sin)
    tabs = (cos, np.where(second, sin, zero), np.where(second, zero, -sin))
    return tuple(jnp.asarray(tab.astype(np.float32)) for tab in tabs)


def kernel(x_prompt, x_sample, c, cache_k, cache_v, state_gla_fwd, state_gla_bwd, c_ctx, w_ada, b_ada, norm_mix, norm_ff, w_in, w_gk_fwd, b_gk_fwd, w_gk_bwd, b_gk_bwd, gla_norm, q_norm, k_norm, sink, w_o, w_ff1, w_ff2):
    depth = w_in.shape[0]
    assert depth == 1, "single trunk layer"
    bp, tp, _ = x_prompt.shape
    bs, ts, _ = x_sample.shape
    n_p = bp * tp
    l = 0

    cond = jnp.concatenate([c_ctx[None, :], c, jnp.zeros((8 - 1 - bs, D_MODEL), F32)], axis=0)
    mod = _modulation(cond, w_ada[l], b_ada[l])

    wg = jnp.zeros((LANES, 2 * GLA_WIDTH), F32)
    wg = wg.at[:GATE_RANK, :GLA_WIDTH].set(w_gk_fwd[l]).at[GATE_RANK:2 * GATE_RANK, GLA_WIDTH:].set(w_gk_bwd[l])
    wg = wg.astype(BF16)
    bg = jnp.concatenate([b_gk_fwd[l], b_gk_bwd[l]])[None, :]
    qkg = jnp.concatenate([jnp.tile(q_norm[l], SWA_HEADS), jnp.tile(k_norm[l], SWA_KV_HEADS)])[None, :]
    norm2 = jnp.tile(gla_norm[l], GLA_HEADS)[None, :]
    gain_mix = norm_mix[l][None, :]
    gain_ff = norm_ff[l][None, :]
    sink_l = sink[l]

    w_in_t = jnp.transpose(w_in[l])
    cache_k_t = jnp.swapaxes(cache_k, 3, 4)
    cache_v_t = jnp.swapaxes(cache_v, 3, 4)

    xp = x_prompt.reshape(n_p, D_MODEL)
    xs = x_sample.reshape(bs * ts, D_MODEL)
    (gq, gk, gv, sgg, gf, gb, sq, sk, sv, new_k_t, new_v_t) = _project(
        xp, xs, mod, gain_mix, w_in_t, wg, bg, qkg, _rope_tables(ts), tp, ts)
    new_k = jnp.swapaxes(new_k_t, 3, 4)
    new_v = jnp.swapaxes(new_v_t, 3, 4)

    og_p, s_f, s_b, osw_p, w1 = _gla(gq, gk, gv, gf, gb, sgg, norm2, None, None, bp, tp, 0, True, seqs=2,
                                     attention=(sink_l, sq, sk, sv), cast_weights=(w_ff1[l],))
    (og_s,) = _gla(gq, gk, gv, gf, gb, sgg, norm2, state_gla_fwd, state_gla_bwd, bs, ts, n_p // ts, False)
    osw_s, wo, w2 = _latent_attention(sink_l, sq, sk, sv, cache_k_t, cache_v_t, bs, ts, n_p,
                                      cast_weights=(w_o[l], w_ff2[l]))

    yp, ys = _mix_out(xp, xs, og_p, og_s, osw_p, osw_s, mod, gain_ff, wo, w1, w2, ts)
    return (yp.reshape(bp, tp, D_MODEL), ys.reshape(bs, ts, D_MODEL), new_k, new_v, s_f, s_b)
```

```python
import functools

import numpy as np
import jax
import jax.numpy as jnp
from jax import lax
from jax.experimental import pallas as pl
from jax.experimental.pallas import tpu as pltpu

F32 = jnp.float32
BF16 = jnp.bfloat16

D_MODEL = 1024
GRID_W = 64
HEAD_DIM = 64
GLA_HEADS = 8
GLA_WIDTH = GLA_HEADS * HEAD_DIM
GATE_RANK = 16
GATE_TEMP = 16.0
CHUNK = 64
GROUP = 4 * CHUNK
SWA_HEADS = 8
SWA_KV_HEADS = 2
SWA_GROUP = SWA_HEADS // SWA_KV_HEADS
SWA_WIDTH = SWA_HEADS * HEAD_DIM
KV_WIDTH = SWA_KV_HEADS * HEAD_DIM
WINDOW = 128
QBLOCK = 128
D_FF = 4 * D_MODEL
ROPE_THETA = 10000.0
EPS = 1e-6
NEG_INF = -1e30
LANES = 128
TM = 512

D_IN = 2848
R_GQ, R_GK, R_GV, R_GG, R_LR = 0, 512, 1024, 1536, 2048
R_SQ = R_LR + 2 * GATE_RANK
R_SV = R_SQ + SWA_WIDTH + KV_WIDTH

VMEM_LIMIT = 56 * 1024 * 1024


def _cparams(n_axes, flags=None):
    return pltpu.CompilerParams(dimension_semantics=("arbitrary",) * n_axes,
                                vmem_limit_bytes=VMEM_LIMIT, flags=flags)


def _mod_kernel(cond_ref, w_ref, b_ref, o_ref):
    k = pl.program_id(0)
    cnd = cond_ref[...]
    s = (cnd * jax.nn.sigmoid(cnd)).astype(BF16)
    part = _dot(s, w_ref[...].astype(BF16))

    @pl.when(k == 0)
    def _():
        o_ref[...] = part + b_ref[...]

    @pl.when(k > 0)
    def _():
        o_ref[...] += part


def _modulation(cond, w_ada, b_ada):
    tk = 128
    n_out = w_ada.shape[1]
    return pl.pallas_call(
        _mod_kernel,
        grid=(D_MODEL // tk,),
        in_specs=[pl.BlockSpec((8, tk), lambda k: (0, k)),
                  pl.BlockSpec((tk, n_out), lambda k: (k, 0)),
                  pl.BlockSpec((1, n_out), lambda k: (0, 0))],
        out_specs=pl.BlockSpec((8, n_out), lambda k: (0, 0)),
        out_shape=jax.ShapeDtypeStruct((8, n_out), F32),
        compiler_params=_cparams(1),
        name="modulation",
    )(cond, w_ada, b_ada.reshape(1, n_out))


def _mod_row(tile, n_prompt_tiles, tiles_per_seq):
    return jnp.where(tile < n_prompt_tiles, 0, 1 + (tile - n_prompt_tiles) // tiles_per_seq)


def _mod_chunk(mod_ref, row, j):
    return mod_ref[pl.ds(row, 1), j * D_MODEL:(j + 1) * D_MODEL]


def _adaln(x, gain, shift, scale):
    ms = jnp.mean(x * x, axis=-1, keepdims=True)
    xn = x * lax.rsqrt(ms + EPS)
    return (xn * gain) * (1.0 + scale) + shift


def _head_mean_sq(y):
    cols = y.shape[1]
    lane = lax.broadcasted_iota(jnp.int32, (1, LANES), 1)
    first = lane < HEAD_DIM
    outs = []
    for p in range(cols // LANES):
        s = y[:, p * LANES:(p + 1) * LANES]
        s = s * s
        tot = jnp.sum(s, axis=-1, keepdims=True)
        lo = jnp.sum(jnp.where(first, s, 0.0), axis=-1, keepdims=True)
        outs.append(jnp.where(first, lo, tot - lo) * (1.0 / HEAD_DIM))
    return outs[0] if len(outs) == 1 else jnp.concatenate(outs, axis=-1)


def _log_sigmoid(x):
    return jnp.minimum(x, 0.0) - jnp.log(1.0 + jnp.exp(-jnp.abs(x)))


def _dot(a, b):
    return jnp.dot(a, b, preferred_element_type=F32)


def _dot_nt(a, b):
    return lax.dot_general(a, b, (((1,), (1,)), ((), ())), preferred_element_type=F32)


def _proj_kernel(xp_ref, xs_ref, mod_ref, gain_ref, w_ref, wg_ref, bg_ref, qkg_ref,
                 cos_ref, sin_a_ref, sin_b_ref,
                 gq_ref, gk_ref, gv_ref, sgg_ref, gf_ref, gb_ref, sq_ref, sk_ref, sv_ref, nk_ref, nv_ref,
                 wb_ref, h0_ref, h1_ref, *, n_prompt_tiles, n_tiles, tiles_per_seq, seqs_per_tile, seq_len):
    h_refs = (h0_ref, h1_ref)
    s = pl.program_id(0)

    @pl.when(s == 0)
    def _():
        blk = 32

        def conv(r, carry):
            rows = pl.ds(pl.multiple_of(r * blk, blk), blk)
            wb_ref[rows, :] = w_ref[rows, :].astype(BF16)
            return carry

        lax.fori_loop(0, D_IN // blk, conv, 0)

    def normalise_next(parity):
        x = jnp.where(s < n_prompt_tiles, xp_ref[...], xs_ref[...])
        row = _mod_row(jnp.minimum(s, n_tiles - 1), n_prompt_tiles, tiles_per_seq)
        shift, scale = _mod_chunk(mod_ref, row, 0), _mod_chunk(mod_ref, row, 1)
        h_refs[parity][...] = _adaln(x, gain_ref[...], shift, scale).astype(BF16)

    def body(rope, emit_cache, parity):
        h_ref = h_refs[1 - parity]

        def proj(r0, r1):
            return _dot_nt(h_ref[...], wb_ref[r0:r1, :])

        tail = proj(R_SQ, D_IN)
        lr = proj(R_LR, R_LR + LANES).astype(BF16)
        gate_pre = _dot(lr, wg_ref[...])
        gg = proj(R_GG, R_LR)
        sgg_ref[...] = (gg * jax.nn.sigmoid(gg)).astype(BF16)
        attention_inputs(tail, rope, emit_cache)
        gq_ref[...] = (proj(R_GQ, R_GK) * (HEAD_DIM ** -0.5)).astype(BF16)
        gf_ref[...] = _log_sigmoid(gate_pre[:, :GLA_WIDTH] + bg_ref[:, :GLA_WIDTH]) * (1.0 / GATE_TEMP)
        gk_ref[...] = proj(R_GK, R_GV).astype(BF16)
        gb_ref[...] = _log_sigmoid(gate_pre[:, GLA_WIDTH:] + bg_ref[:, GLA_WIDTH:]) * (1.0 / GATE_TEMP)
        gv_ref[...] = proj(R_GV, R_GG).astype(BF16)
        normalise_next(parity)

    def attention_inputs(tail, rope, emit_cache):
        qk = tail[:, :R_SV - R_SQ]
        qk = qk * lax.rsqrt(_head_mean_sq(qk) + EPS) * qkg_ref[...]
        if rope:
            reps = (SWA_WIDTH + KV_WIDTH) // LANES
            cos = jnp.concatenate([cos_ref[...]] * reps, axis=-1)
            sin_a = jnp.concatenate([sin_a_ref[...]] * reps, axis=-1)
            sin_b = jnp.concatenate([sin_b_ref[...]] * reps, axis=-1)
            width = qk.shape[1]
            quarter = HEAD_DIM // 4
            qk = (qk * cos + pltpu.roll(qk, quarter, 1) * sin_a
                  + pltpu.roll(qk, width - quarter, 1) * sin_b)
        sq_ref[...] = (qk[:, :SWA_WIDTH] * (HEAD_DIM ** -0.5)).astype(BF16)
        sk = qk[:, SWA_WIDTH:]
        sk_ref[...] = sk.astype(BF16)
        sv = tail[:, R_SV - R_SQ:]
        sv_ref[...] = sv.astype(BF16)
        if emit_cache:
            sk_t, sv_t = sk.T, sv.T
            for q in range(seqs_per_tile):
                cols = slice(q * seq_len, (q + 1) * seq_len)
                for g in range(SWA_KV_HEADS):
                    feat = slice(g * HEAD_DIM, (g + 1) * HEAD_DIM)
                    nk_ref[q, 0, g, :, :] = sk_t[feat, cols]
                    nv_ref[q, 0, g, :, :] = sv_t[feat, cols]

    @pl.when(s == 0)
    def _():
        normalise_next(0)

    for parity in range(2):
        on_parity = (s % 2) == parity

        @pl.when((s >= 1) & (s <= n_prompt_tiles) & on_parity)
        def _():
            body(rope=False, emit_cache=True, parity=parity)

        @pl.when((s > n_prompt_tiles) & on_parity)
        def _():
            body(rope=True, emit_cache=False, parity=parity)


def _project(xp, xs, mod, gain, w_in, wg, bg, qkg, rope_tabs, seq_p, seq_s):
    n_p, n_s = xp.shape[0], xs.shape[0]
    tp, ts = n_p // TM, n_s // TM
    n = n_p + n_s
    seqs_per_tile = TM // seq_p
    tiles_per_seq = seq_s // TM
    n_seq_p = n_p // seq_p
    const = lambda shape: pl.BlockSpec(shape, lambda s: (0,) * len(shape))
    n_tiles = tp + ts
    out_tile = lambda s: jnp.maximum(s - 1, 0)
    rope_spec = pl.BlockSpec((TM, LANES), lambda s: (jnp.maximum(out_tile(s) - tp, 0) % tiles_per_seq, 0))
    in_specs = [pl.BlockSpec((TM, D_MODEL), lambda s: (jnp.minimum(s, tp - 1), 0)),
                pl.BlockSpec((TM, D_MODEL), lambda s: (jnp.clip(s - tp, 0, ts - 1), 0)),
                const(mod.shape),
                const((1, D_MODEL)),
                pl.BlockSpec((D_IN, D_MODEL), lambda s: (0, 0), pipeline_mode=pl.Buffered(1)),
                const((LANES, 2 * GLA_WIDTH)), const((1, 2 * GLA_WIDTH)), const((1, SWA_WIDTH + KV_WIDTH)),
                rope_spec, rope_spec, rope_spec]
    row = lambda w: pl.BlockSpec((TM, w), lambda s: (out_tile(s), 0))
    cache_spec = pl.BlockSpec((seqs_per_tile, 1, SWA_KV_HEADS, HEAD_DIM, seq_p),
                              lambda s: (jnp.minimum(out_tile(s), tp - 1), 0, 0, 0, 0))
    out_specs = [row(GLA_WIDTH)] * 6 + [row(SWA_WIDTH), row(KV_WIDTH), row(KV_WIDTH)] + [cache_spec] * 2
    cache_shape = (n_seq_p, 1, SWA_KV_HEADS, HEAD_DIM, seq_p)
    out_shape = ([jax.ShapeDtypeStruct((n, GLA_WIDTH), BF16)] * 4
                 + [jax.ShapeDtypeStruct((n, GLA_WIDTH), F32)] * 2
                 + [jax.ShapeDtypeStruct((n, SWA_WIDTH), BF16),
                    jax.ShapeDtypeStruct((n, KV_WIDTH), BF16), jax.ShapeDtypeStruct((n, KV_WIDTH), BF16)]
                 + [jax.ShapeDtypeStruct(cache_shape, F32)] * 2)
    return pl.pallas_call(
        functools.partial(_proj_kernel, n_prompt_tiles=tp, n_tiles=n_tiles, tiles_per_seq=tiles_per_seq,
                          seqs_per_tile=seqs_per_tile, seq_len=seq_p),
        grid=(n_tiles + 1,), in_specs=in_specs, out_specs=out_specs, out_shape=out_shape,
        scratch_shapes=[pltpu.VMEM((D_IN, D_MODEL), BF16)] + [pltpu.VMEM((TM, D_MODEL), BF16)] * 2,
        compiler_params=_cparams(1), name="adaln_in_proj",
    )(xp, xs, mod, gain, w_in, wg, bg, qkg, *rope_tabs)


def _gla_kernel(*refs, seq_len, seqs, has_state, emit_state, with_attention, n_cast):
    (q_ref, k_ref, v_ref, gf_ref, gb_ref, sgg_ref, norm_ref,
     trif_ref, trib_ref, keepf_ref, keepb_ref) = refs[:11]
    pos = 11
    if has_state:
        s0f_ref, s0b_ref = refs[pos:pos + 2]
        pos += 2
    if with_attention:
        sink_ref, aq_ref, ak_ref, av_ref = refs[pos:pos + 4]
        pos += 4
    if with_attention == "latent":
        kc_ref, vc_ref = refs[pos:pos + 2]
        pos += 2
    cast_in = refs[pos:pos + n_cast]
    pos += n_cast
    og_ref = refs[pos]
    pos += 1
    if emit_state:
        sf_ref, sb_ref = refs[pos:pos + 2]
        pos += 2
    if with_attention:
        osw_ref = refs[pos]
        pos += 1
    cast_out = refs[pos:pos + n_cast]
    pos += n_cast
    accf_ref, accb_ref = refs[pos:pos + 2]

    for src, dst in zip(cast_in, cast_out):
        dst[...] = src[...].astype(BF16)

    n_groups = seq_len // GROUP
    per_group = GROUP // CHUNK
    n_pairs = q_ref.shape[1] // LANES
    lane = lax.broadcasted_iota(jnp.int32, (1, LANES), 1)
    head0 = lane < HEAD_DIM
    m0 = jnp.where(head0, 1.0, 0.0).astype(BF16)
    m1 = jnp.where(head0, 0.0, 1.0).astype(BF16)
    row_top = lax.broadcasted_iota(jnp.int32, (LANES, 1), 0) < HEAD_DIM

    def stack_heads(x):
        return jnp.concatenate([x * m0, x * m1], axis=0)

    def direction(row0, lanes, g_ref, tri_ref, keep_ref, state, acc_ref, backward):
        rows = pl.ds(row0, GROUP)
        g = g_ref[rows, lanes]
        g_hi = g.astype(BF16)
        g_lo = (g - g_hi.astype(F32)).astype(BF16)
        r = _dot(tri_ref[...], jnp.concatenate([g_hi, g_lo], axis=-1))
        yield
        b = (r[:, :LANES] + r[:, LANES:]).reshape(per_group, CHUNK, LANES)
        edge = 0 if backward else CHUNK - 1
        mid = CHUNK // 2
        b_edge = b[:, edge:edge + 1, :]
        b_mid = b[:, mid:mid + 1, :]
        q = q_ref[rows, lanes].astype(F32).reshape(per_group, CHUNK, LANES)
        k = k_ref[rows, lanes].astype(F32).reshape(per_group, CHUNK, LANES)
        v = v_ref[rows, lanes]
        q_mid = q * jnp.exp(b - b_mid)
        k_mid = k * jnp.exp(b_mid - b)
        q_in = q_mid.astype(BF16)
        k_in = k_mid.astype(BF16).reshape(GROUP, LANES)
        q_st = (q_mid * jnp.exp(b_mid)).astype(BF16)
        k_st = (k_mid * jnp.exp(b_edge - b_mid)).astype(BF16).reshape(GROUP, LANES)
        decay = jnp.exp(b_edge)
        keep = keep_ref[...]
        pair_rows = 2 * CHUNK
        n_cp = per_group // 2
        yield
        scores, kvs = [], []
        for p in range(n_cp):
            c0 = 2 * p
            prow = slice(p * pair_rows, (p + 1) * pair_rows)
            q_pair = jnp.concatenate([stack_heads(q_in[c0]), stack_heads(q_in[c0 + 1])], axis=0)
            scores.append(_dot_nt(q_pair, k_in[prow, :]))
            v_t = v[prow, :].astype(F32).T.astype(BF16)
            k_pair = k_st[prow, :]
            zero = jnp.zeros_like(k_pair)
            k_blk = jnp.concatenate([jnp.where(row_top, k_pair, zero), jnp.where(row_top, zero, k_pair)], axis=-1)
            kvs.append(_dot(v_t, k_blk))
        yield
        intras = [_dot(scores[p].astype(BF16) * keep, v[p * pair_rows:(p + 1) * pair_rows, :]) for p in range(n_cp)]
        yield
        for p in (range(n_cp - 1, -1, -1) if backward else range(n_cp)):
            for j in ((1, 0) if backward else (0, 1)):
                c = 2 * p + j
                inter = _dot_nt(stack_heads(q_st[c]), state.astype(BF16))
                tot = intras[p][j * pair_rows:(j + 1) * pair_rows, :] + inter
                acc_ref[pl.ds(row0 + c * CHUNK, CHUNK), lanes] = jnp.where(head0, tot[:CHUNK, :], tot[CHUNK:, :])
                state = state * decay[c] + kvs[p][:, j * LANES:(j + 1) * LANES]
            yield
        return state

    def load_state(s_ref, q, p):
        z = jnp.zeros((HEAD_DIM, HEAD_DIM), F32)
        top = jnp.concatenate([s_ref[q, 0, 2 * p, :, :].T, z], axis=-1)
        bot = jnp.concatenate([z, s_ref[q, 0, 2 * p + 1, :, :].T], axis=-1)
        return jnp.concatenate([top, bot], axis=0)

    per_dir = seqs * n_pairs
    if has_state:
        init = tuple(load_state(s_ref, q, p)
                     for s_ref in (s0f_ref, s0b_ref) for q in range(seqs) for p in range(n_pairs))
    else:
        init = (jnp.zeros((LANES, LANES), F32),) * (2 * per_dir)

    def body(t, carry):
        row_f = t * GROUP
        row_b = (n_groups - 1 - t) * GROUP
        if n_groups > 1:
            row_f = pl.multiple_of(row_f, GROUP)
            row_b = pl.multiple_of(row_b, GROUP)
        units = []
        for backward, row, g_ref, tri_ref, keep_ref, acc_ref in (
                (False, row_f, gf_ref, trif_ref, keepf_ref, accf_ref),
                (True, row_b, gb_ref, trib_ref, keepb_ref, accb_ref)):
            for q in range(seqs):
                for p in range(n_pairs):
                    lanes = slice(p * LANES, (p + 1) * LANES)
                    state = carry[backward * per_dir + q * n_pairs + p]
                    units.append(direction(q * seq_len + row, lanes, g_ref, tri_ref, keep_ref, state, acc_ref,
                                           backward))
        return units

    if n_groups == 1:
        units = body(0, init)
        if with_attention:
            for q in range(seqs):
                units += _context_attention_units(sink_ref, aq_ref, ak_ref, av_ref,
                                                  slice(q * seq_len, (q + 1) * seq_len))
        results = _run_staged(units)
        final = results[:2 * per_dir]
        if with_attention:
            for q in range(seqs):
                heads = [o for outs in results[2 * per_dir + q * SWA_KV_HEADS:
                                               2 * per_dir + (q + 1) * SWA_KV_HEADS] for o in outs]
                osw_ref[q * seq_len:(q + 1) * seq_len, :] = jnp.concatenate(heads, axis=-1).astype(BF16)
    else:
        assert with_attention != "context" and (seqs == 1 or not with_attention)
        if with_attention:
            ctx_kv = [(kc_ref[0, 0, g, :, :].astype(BF16), vc_ref[0, 0, g, :, :].astype(BF16))
                      for g in range(SWA_KV_HEADS)]
            q_blocks = GROUP // QBLOCK

        def loop_body(t, carry):
            units = body(t, carry)
            n_gla = len(units)
            if with_attention:
                for j in range(q_blocks):
                    units += _latent_attention_units(sink_ref, aq_ref, ak_ref, av_ref, ctx_kv,
                                                     t * q_blocks + j, seq_len)
            results = _run_staged(units)
            if with_attention:
                for j in range(q_blocks):
                    heads = [o for outs in results[n_gla + j * SWA_KV_HEADS:n_gla + (j + 1) * SWA_KV_HEADS]
                             for o in outs]
                    rows = pl.ds(pl.multiple_of((t * q_blocks + j) * QBLOCK, QBLOCK), QBLOCK)
                    osw_ref[rows, :] = jnp.concatenate(heads, axis=-1).astype(BF16)
            return tuple(results[:n_gla])

        final = lax.fori_loop(0, n_groups, loop_body, init)
    if emit_state:
        for d, s_ref in enumerate((sf_ref, sb_ref)):
            for q in range(seqs):
                for p in range(n_pairs):
                    fin = final[d * per_dir + q * n_pairs + p]
                    s_ref[q, 0, 2 * p, :, :] = fin[:HEAD_DIM, :HEAD_DIM].T
                    s_ref[q, 0, 2 * p + 1, :, :] = fin[HEAD_DIM:, HEAD_DIM:].T

    o = accf_ref[...] + accb_ref[...]
    o = o * lax.rsqrt(_head_mean_sq(o) + EPS) * norm_ref[...]
    og_ref[...] = (o * sgg_ref[...].astype(F32)).astype(BF16)


def _gla_constants():
    i = np.arange(GROUP)[:, None]
    j = np.arange(GROUP)[None, :]
    same = (i // CHUNK) == (j // CHUNK)
    tri_f = same & (j <= i)
    tri_b = same & (j >= i)
    r = np.arange(4 * CHUNK)[:, None]
    l = np.arange(LANES)[None, :]
    same = (r // (2 * CHUNK)) == (l // CHUNK)
    keep_f = same & ((l % CHUNK) <= (r % CHUNK))
    keep_b = same & ((l % CHUNK) >= (r % CHUNK))
    return tuple(jnp.asarray(m.astype(np.float32), dtype=BF16) for m in (tri_f, tri_b, keep_f, keep_b))


def _gla(gq, gk, gv, gf, gb, sgg, norm_row, s0f, s0b, n_seq, seq_len, row_block0, emit_state, seqs=1,
         attention=None, cast_weights=()):
    has_state = s0f is not None
    steps = n_seq // seqs
    rows_per_step = seqs * seq_len
    blk = pl.BlockSpec((rows_per_step, GLA_WIDTH), lambda b: (row_block0 // seqs + b, 0))
    out_blk = pl.BlockSpec((rows_per_step, GLA_WIDTH), lambda b: (b, 0))
    st_spec = pl.BlockSpec((seqs, 1, GLA_HEADS, HEAD_DIM, HEAD_DIM), lambda b: (b, 0, 0, 0, 0))
    const = lambda shape: pl.BlockSpec(shape, lambda b: (0, 0))
    in_specs = ([blk] * 6 + [const((1, GLA_WIDTH))] + [const((GROUP, GROUP))] * 2
                + [const((4 * CHUNK, LANES))] * 2)
    args = [gq, gk, gv, gf, gb, sgg, norm_row, *_gla_constants()]
    if has_state:
        in_specs += [st_spec] * 2
        args += [s0f, s0b]
    out_specs = [out_blk]
    out_shape = [jax.ShapeDtypeStruct((n_seq * seq_len, GLA_WIDTH), BF16)]
    if emit_state:
        out_specs += [st_spec] * 2
        out_shape += [jax.ShapeDtypeStruct((n_seq, 1, GLA_HEADS, HEAD_DIM, HEAD_DIM), F32)] * 2
    with_attention = None
    if attention is not None:
        sink, sq, sk, sv = attention[:4]
        with_attention = "context"
        in_specs += [pl.BlockSpec(memory_space=pltpu.SMEM),
                     pl.BlockSpec((rows_per_step, SWA_WIDTH), lambda b: (row_block0 // seqs + b, 0)),
                     pl.BlockSpec((rows_per_step, KV_WIDTH), lambda b: (row_block0 // seqs + b, 0)),
                     pl.BlockSpec((rows_per_step, KV_WIDTH), lambda b: (row_block0 // seqs + b, 0))]
        args += [sink, sq, sk, sv]
        if len(attention) > 4:
            with_attention = "latent"
            cache_k_t, cache_v_t = attention[4:]
            past = cache_k_t.shape[4]
            cache_spec = pl.BlockSpec((seqs, 1, SWA_KV_HEADS, HEAD_DIM, past), lambda b: (b, 0, 0, 0, 0))
            in_specs += [cache_spec] * 2
            args += [cache_k_t, cache_v_t]
        out_specs.append(pl.BlockSpec((rows_per_step, SWA_WIDTH), lambda b: (b, 0)))
        out_shape.append(jax.ShapeDtypeStruct((n_seq * seq_len, SWA_WIDTH), BF16))
    cast_specs, cast_shapes = _cast_specs(cast_weights, steps)
    in_specs += cast_specs
    args += list(cast_weights)
    out_specs += cast_specs
    out_shape += cast_shapes
    return pl.pallas_call(
        functools.partial(_gla_kernel, seq_len=seq_len, seqs=seqs, has_state=has_state, emit_state=emit_state,
                          with_attention=with_attention, n_cast=len(cast_weights)),
        grid=(steps,), in_specs=in_specs, out_specs=out_specs, out_shape=out_shape,
        scratch_shapes=[pltpu.VMEM((rows_per_step, GLA_WIDTH), F32)] * 2,
        compiler_params=_cparams(1), name="gla_bidirectional",
    )(*args)


def _group_attention(q, heads, sink_ref, score_fn, value_fn):
    rows = q.shape[0]
    q_stack = jnp.concatenate([q[:, h * HEAD_DIM:(h + 1) * HEAD_DIM] for h in heads], axis=0)
    scores = score_fn(q_stack)
    yield
    slabs = [slice(j * rows, (j + 1) * rows) for j in range(len(heads))]
    ms = []
    for slab, h in zip(slabs, heads):
        m = sink_ref[h]
        for s in scores:
            m = jnp.maximum(m, jnp.max(s[slab, :], axis=-1, keepdims=True))
        ms.append(m)
    yield
    e_parts = [[] for _ in scores]
    denoms = []
    for slab, h, m in zip(slabs, heads, ms):
        denom = jnp.exp(sink_ref[h] - m)
        for part, s in zip(e_parts, scores):
            e = jnp.exp(s[slab, :] - m)
            denom = denom + jnp.sum(e, axis=-1, keepdims=True)
            part.append(e.astype(BF16))
        denoms.append(denom)
    yield
    acc = value_fn([jnp.concatenate(part, axis=0) for part in e_parts])
    yield
    return [acc[slab, :] / denom for slab, denom in zip(slabs, denoms)]


def _run_staged(units):
    results = [None] * len(units)
    live = list(range(len(units)))
    while live:
        still = []
        for u in live:
            try:
                next(units[u])
                still.append(u)
            except StopIteration as stop:
                results[u] = stop.value
        live = still
    return results


def _context_attention_units(sink_ref, q_ref, k_ref, v_ref, rows):
    q = q_ref[rows, :]
    k = k_ref[rows, :]
    v = v_ref[rows, :]
    units = []
    for g in range(SWA_KV_HEADS):
        k_g = k[:, g * HEAD_DIM:(g + 1) * HEAD_DIM]
        v_g = v[:, g * HEAD_DIM:(g + 1) * HEAD_DIM]
        units.append(_group_attention(
            q, list(range(SWA_GROUP * g, SWA_GROUP * (g + 1))), sink_ref,
            lambda qs, k_g=k_g: [_dot_nt(qs, k_g)],
            lambda es, v_g=v_g: _dot(es[0], v_g)))
    return units


def _cast_specs(cast_weights, steps, step_of=lambda *idx: idx[0]):
    specs = [pl.BlockSpec((w.shape[0] // steps, w.shape[1]), lambda *idx: (step_of(*idx), 0))
             for w in cast_weights]
    shapes = [jax.ShapeDtypeStruct(w.shape, BF16) for w in cast_weights]
    return specs, shapes


def _latent_attention_units(sink_ref, q_ref, k_ref, v_ref, ctx_kv, i, seq_len):
    span = QBLOCK + 2 * WINDOW
    start = pl.multiple_of(jnp.clip(i * QBLOCK - WINDOW, 0, seq_len - span), QBLOCK)
    q = q_ref[pl.ds(pl.multiple_of(i * QBLOCK, QBLOCK), QBLOCK), :]
    k = k_ref[pl.ds(start, span), :]
    v = v_ref[pl.ds(start, span), :]
    qpos = i * QBLOCK + lax.broadcasted_iota(jnp.int32, (QBLOCK, span), 0)
    kpos = start + lax.broadcasted_iota(jnp.int32, (QBLOCK, span), 1)
    valid = jnp.abs(qpos - kpos) <= WINDOW
    valid = jnp.concatenate([valid] * SWA_GROUP, axis=0)
    units = []
    for g in range(SWA_KV_HEADS):
        k_g = k[:, g * HEAD_DIM:(g + 1) * HEAD_DIM]
        v_g = v[:, g * HEAD_DIM:(g + 1) * HEAD_DIM]
        kc_t, vc_t = ctx_kv[g]
        units.append(_group_attention(
            q, list(range(SWA_GROUP * g, SWA_GROUP * (g + 1))), sink_ref,
            lambda qs, k_g=k_g, kc_t=kc_t: [jnp.where(valid, _dot_nt(qs, k_g), NEG_INF), _dot(qs, kc_t)],
            lambda es, v_g=v_g, vc_t=vc_t: _dot(es[0], v_g) + _dot_nt(es[1], vc_t)))
    return units


def _out_kernel(xp_ref, xs_ref, ogp_ref, ogs_ref, oswp_ref, osws_ref, mod_ref, gain_ref,
                wo_ref, w1_ref, w2_ref, yp_ref, ys_ref, *, n_prompt_tiles, tiles_per_seq):
    i = pl.program_id(0)
    row = _mod_row(i, n_prompt_tiles, tiles_per_seq)

    def body(x_ref, og_ref, osw_ref, y_ref):
        mix = _dot(og_ref[...], wo_ref[:GLA_WIDTH, :]) + _dot(osw_ref[...], wo_ref[GLA_WIDTH:, :])
        x1 = x_ref[...] + _mod_chunk(mod_ref, row, 2) * mix
        h = _adaln(x1, gain_ref[...], _mod_chunk(mod_ref, row, 3), _mod_chunk(mod_ref, row, 4)).astype(BF16)
        ff_tile = 1024
        acc = None
        for j in range(D_FF // ff_tile):
            cols = slice(j * ff_tile, (j + 1) * ff_tile)
            a = jnp.maximum(_dot(h, w1_ref[:, cols]), 0.0)
            part = _dot((a * a).astype(BF16), w2_ref[cols, :])
            acc = part if acc is None else acc + part
        y_ref[...] = x1 + _mod_chunk(mod_ref, row, 5) * acc

    @pl.when(i < n_prompt_tiles)
    def _():
        body(xp_ref, ogp_ref, oswp_ref, yp_ref)

    @pl.when(i >= n_prompt_tiles)
    def _():
        body(xs_ref, ogs_ref, osws_ref, ys_ref)


def _mix_out(xp, xs, og_p, og_s, osw_p, osw_s, mod, gain, wo, w1, w2, seq_s):
    n_p, n_s = xp.shape[0], xs.shape[0]
    tp, ts = n_p // TM, n_s // TM
    tiles_per_seq = seq_s // TM
    const = lambda shape: pl.BlockSpec(shape, lambda i: (0,) * len(shape), pipeline_mode=pl.Buffered(1))
    row_p = lambda w: pl.BlockSpec((TM, w), lambda i: (jnp.minimum(i, tp - 1), 0))
    row_s = lambda w: pl.BlockSpec((TM, w), lambda i: (jnp.maximum(i - tp, 0), 0))
    return pl.pallas_call(
        functools.partial(_out_kernel, n_prompt_tiles=tp, tiles_per_seq=tiles_per_seq),
        grid=(tp + ts,),
        in_specs=[row_p(D_MODEL), row_s(D_MODEL), row_p(GLA_WIDTH), row_s(GLA_WIDTH),
                  row_p(SWA_WIDTH), row_s(SWA_WIDTH),
                  const(mod.shape),
                  const((1, D_MODEL)), const((GLA_WIDTH + SWA_WIDTH, D_MODEL)),
                  const((D_MODEL, D_FF)), const((D_FF, D_MODEL))],
        out_specs=[row_p(D_MODEL), row_s(D_MODEL)],
        out_shape=[jax.ShapeDtypeStruct((n_p, D_MODEL), F32), jax.ShapeDtypeStruct((n_s, D_MODEL), F32)],
        compiler_params=_cparams(1), name="out_proj_mlp",
    )(xp, xs, og_p, og_s, osw_p, osw_s, mod, gain, wo, w1, w2)


def _rope_tables(seq_len):
    axis_dim = HEAD_DIM // 2
    half = axis_dim // 2
    t = np.arange(seq_len)
    rowp = (t // GRID_W).astype(np.float64)
    colp = (t % GRID_W).astype(np.float64)
    freqs = ROPE_THETA ** (-np.arange(half, dtype=np.float64) / half)
    lane = np.arange(LANES)
    d = lane % HEAD_DIM
    posv = np.where((d // axis_dim)[None, :] == 0, rowp[:, None], colp[:, None])
    ang = posv * freqs[d % half][None, :]
    second = ((d % axis_dim) >= half)[None, :]
    cos, sin = np.cos(ang), np.sin(ang)
    zero = np.zeros_like(sin)
    tabs = (cos, np.where(second, sin, zero), np.where(second, zero, -sin))
    return tuple(jnp.asarray(tab.astype(np.float32)) for tab in tabs)


def kernel(x_prompt, x_sample, c, cache_k, cache_v, state_gla_fwd, state_gla_bwd, c_ctx, w_ada, b_ada, norm_mix, norm_ff, w_in, w_gk_fwd, b_gk_fwd, w_gk_bwd, b_gk_bwd, gla_norm, q_norm, k_norm, sink, w_o, w_ff1, w_ff2):
    depth = w_in.shape[0]
    assert depth == 1, "single trunk layer"
    bp, tp, _ = x_prompt.shape
    bs, ts, _ = x_sample.shape
    n_p = bp * tp
    l = 0

    cond = jnp.concatenate([c_ctx[None, :], c, jnp.zeros((8 - 1 - bs, D_MODEL), F32)], axis=0)
    mod = _modulation(cond, w_ada[l], b_ada[l])

    wg = jnp.zeros((LANES, 2 * GLA_WIDTH), F32)
    wg = wg.at[:GATE_RANK, :GLA_WIDTH].set(w_gk_fwd[l]).at[GATE_RANK:2 * GATE_RANK, GLA_WIDTH:].set(w_gk_bwd[l])
    wg = wg.astype(BF16)
    bg = jnp.concatenate([b_gk_fwd[l], b_gk_bwd[l]])[None, :]
    qkg = jnp.concatenate([jnp.tile(q_norm[l], SWA_HEADS), jnp.tile(k_norm[l], SWA_KV_HEADS)])[None, :]
    norm2 = jnp.tile(gla_norm[l], GLA_HEADS)[None, :]
    gain_mix = norm_mix[l][None, :]
    gain_ff = norm_ff[l][None, :]
    sink_l = sink[l]

    w_in_t = jnp.transpose(w_in[l])
    cache_k_t = jnp.swapaxes(cache_k, 3, 4)
    cache_v_t = jnp.swapaxes(cache_v, 3, 4)

    xp = x_prompt.reshape(n_p, D_MODEL)
    xs = x_sample.reshape(bs * ts, D_MODEL)
    (gq, gk, gv, sgg, gf, gb, sq, sk, sv, new_k_t, new_v_t) = _project(
        xp, xs, mod, gain_mix, w_in_t, wg, bg, qkg, _rope_tables(ts), tp, ts)
    new_k = jnp.swapaxes(new_k_t, 3, 4)
    new_v = jnp.swapaxes(new_v_t, 3, 4)

    og_p, s_f, s_b, osw_p, wo, w1, w2 = _gla(
        gq, gk, gv, gf, gb, sgg, norm2, None, None, bp, tp, 0, True, seqs=2,
        attention=(sink_l, sq, sk, sv), cast_weights=(w_o[l], w_ff1[l], w_ff2[l]))
    og_s, osw_s = _gla(gq, gk, gv, gf, gb, sgg, norm2, state_gla_fwd, state_gla_bwd, bs, ts, n_p // ts, False,
                       attention=(sink_l, sq, sk, sv, cache_k_t, cache_v_t))

    yp, ys = _mix_out(xp, xs, og_p, og_s, osw_p, osw_s, mod, gain_ff, wo, w1, w2, ts)
    return (yp.reshape(bp, tp, D_MODEL), ys.reshape(bs, ts, D_MODEL), new_k, new_v, s_f, s_b)
```

```python
import functools

import numpy as np
import jax
import jax.numpy as jnp
from jax import lax
from jax.experimental import pallas as pl
from jax.experimental.pallas import tpu as pltpu

F32 = jnp.float32
BF16 = jnp.bfloat16

D_MODEL = 1024
GRID_W = 64
HEAD_DIM = 64
GLA_HEADS = 8
GLA_WIDTH = GLA_HEADS * HEAD_DIM
GATE_RANK = 16
GATE_TEMP = 16.0
CHUNK = 64
GROUP = 4 * CHUNK
SWA_HEADS = 8
SWA_KV_HEADS = 2
SWA_GROUP = SWA_HEADS // SWA_KV_HEADS
SWA_WIDTH = SWA_HEADS * HEAD_DIM
KV_WIDTH = SWA_KV_HEADS * HEAD_DIM
WINDOW = 128
QBLOCK = 128
D_FF = 4 * D_MODEL
ROPE_THETA = 10000.0
EPS = 1e-6
NEG_INF = -1e30
LANES = 128
TM = 512
FF_TILE = 1024

D_IN = 2848
R_GQ, R_GK, R_GV, R_GG, R_LR = 0, 512, 1024, 1536, 2048
R_SQ = R_LR + 2 * GATE_RANK
R_SV = R_SQ + SWA_WIDTH + KV_WIDTH

VMEM_LIMIT = 56 * 1024 * 1024


def _cparams(n_axes, flags=None):
    return pltpu.CompilerParams(dimension_semantics=("arbitrary",) * n_axes,
                                vmem_limit_bytes=VMEM_LIMIT, flags=flags)


def _mod_kernel(cond_ref, w_ref, b_ref, o_ref):
    k = pl.program_id(0)
    cnd = cond_ref[...]
    s = (cnd * jax.nn.sigmoid(cnd)).astype(BF16)
    part = _dot(s, w_ref[...].astype(BF16))

    @pl.when(k == 0)
    def _():
        o_ref[...] = part + b_ref[...]

    @pl.when(k > 0)
    def _():
        o_ref[...] += part


def _modulation(cond, w_ada, b_ada):
    tk = 256
    n_out = w_ada.shape[1]
    return pl.pallas_call(
        _mod_kernel,
        grid=(D_MODEL // tk,),
        in_specs=[pl.BlockSpec((8, tk), lambda k: (0, k)),
                  pl.BlockSpec((tk, n_out), lambda k: (k, 0)),
                  pl.BlockSpec((1, n_out), lambda k: (0, 0))],
        out_specs=pl.BlockSpec((8, n_out), lambda k: (0, 0)),
        out_shape=jax.ShapeDtypeStruct((8, n_out), F32),
        compiler_params=_cparams(1),
        name="modulation",
    )(cond, w_ada, b_ada.reshape(1, n_out))


def _mod_row(tile, n_prompt_tiles, tiles_per_seq):
    return jnp.where(tile < n_prompt_tiles, 0, 1 + (tile - n_prompt_tiles) // tiles_per_seq)


def _mod_chunk(mod_ref, row, j):
    return mod_ref[pl.ds(row, 1), j * D_MODEL:(j + 1) * D_MODEL]


def _adaln(x, gain, shift, scale):
    ms = jnp.mean(x * x, axis=-1, keepdims=True)
    xn = x * lax.rsqrt(ms + EPS)
    return xn * (gain * (1.0 + scale)) + shift


def _head_rms_norm(y, gain):
    cols = y.shape[1]
    lane = lax.broadcasted_iota(jnp.int32, (1, LANES), 1)
    first = lane < HEAD_DIM
    outs = []
    for p in range(cols // LANES):
        s = y[:, p * LANES:(p + 1) * LANES]
        s = s * s
        tot = jnp.sum(s, axis=-1, keepdims=True)
        lo = jnp.sum(jnp.where(first, s, 0.0), axis=-1, keepdims=True)
        outs.append(jnp.where(first, lo, tot - lo))
    sums = outs[0] if len(outs) == 1 else jnp.concatenate(outs, axis=-1)
    return y * lax.rsqrt(sums + HEAD_DIM * EPS) * (gain * (HEAD_DIM ** 0.5))


def _log_sigmoid(x):
    return jnp.minimum(x, 0.0) - jnp.log(1.0 + jnp.exp(-jnp.abs(x)))


def _dot(a, b):
    return jnp.dot(a, b, preferred_element_type=F32)


def _dot_nt(a, b):
    return lax.dot_general(a, b, (((1,), (1,)), ((), ())), preferred_element_type=F32)


def _proj_kernel(xp_ref, xs_ref, mod_ref, gain_ref, w_ref, wg_ref, bg_ref, qkg_ref,
                 cos_ref, sin_a_ref, sin_b_ref,
                 gq_ref, gk_ref, gv_ref, sgg_ref, gf_ref, gb_ref, sq_ref, sk_ref, sv_ref, nk_ref, nv_ref,
                 wb_ref, h0_ref, h1_ref, *, n_prompt_tiles, n_tiles, tiles_per_seq, seqs_per_tile, seq_len):
    h_refs = (h0_ref, h1_ref)
    s = pl.program_id(0)

    @pl.when(s == 0)
    def _():
        blk = 32

        def conv(r, carry):
            rows = pl.ds(pl.multiple_of(r * blk, blk), blk)
            wb_ref[rows, :] = w_ref[rows, :].astype(BF16)
            return carry

        lax.fori_loop(0, D_IN // blk, conv, 0)

    def normalise_next(parity):
        x = jnp.where(s < n_prompt_tiles, xp_ref[...], xs_ref[...])
        row = _mod_row(jnp.minimum(s, n_tiles - 1), n_prompt_tiles, tiles_per_seq)
        shift, scale = _mod_chunk(mod_ref, row, 0), _mod_chunk(mod_ref, row, 1)
        h_refs[parity][...] = _adaln(x, gain_ref[...], shift, scale).astype(BF16)

    def body(rope, emit_cache, parity):
        h_ref = h_refs[1 - parity]

        def proj(r0, r1):
            return _dot_nt(h_ref[...], wb_ref[r0:r1, :])

        tail = proj(R_SQ, D_IN)
        lr = proj(R_LR, R_LR + LANES).astype(BF16)
        gate_pre = _dot(lr, wg_ref[...])
        gg = proj(R_GG, R_LR)
        sgg_ref[...] = (gg * jax.nn.sigmoid(gg)).astype(BF16)
        attention_inputs(tail, rope, emit_cache)
        gq_ref[...] = (proj(R_GQ, R_GK) * (HEAD_DIM ** -0.5)).astype(BF16)
        gf_ref[...] = _log_sigmoid(gate_pre[:, :GLA_WIDTH] + bg_ref[:, :GLA_WIDTH]) * (1.0 / GATE_TEMP)
        gk_ref[...] = proj(R_GK, R_GV).astype(BF16)
        gb_ref[...] = _log_sigmoid(gate_pre[:, GLA_WIDTH:] + bg_ref[:, GLA_WIDTH:]) * (1.0 / GATE_TEMP)
        gv_ref[...] = proj(R_GV, R_GG).astype(BF16)
        normalise_next(parity)

    def attention_inputs(tail, rope, emit_cache):
        qk = tail[:, :R_SV - R_SQ]
        qk = _head_rms_norm(qk, qkg_ref[...])
        if rope:
            reps = (SWA_WIDTH + KV_WIDTH) // LANES
            cos = jnp.concatenate([cos_ref[...]] * reps, axis=-1)
            sin_a = jnp.concatenate([sin_a_ref[...]] * reps, axis=-1)
            sin_b = jnp.concatenate([sin_b_ref[...]] * reps, axis=-1)
            width = qk.shape[1]
            quarter = HEAD_DIM // 4
            qk = (qk * cos + pltpu.roll(qk, quarter, 1) * sin_a
                  + pltpu.roll(qk, width - quarter, 1) * sin_b)
        sq_ref[...] = (qk[:, :SWA_WIDTH] * (HEAD_DIM ** -0.5)).astype(BF16)
        sk = qk[:, SWA_WIDTH:]
        sk_ref[...] = sk.astype(BF16)
        sv = tail[:, R_SV - R_SQ:]
        sv_ref[...] = sv.astype(BF16)
        if emit_cache:
            sk_t, sv_t = sk.T, sv.T
            for q in range(seqs_per_tile):
                cols = slice(q * seq_len, (q + 1) * seq_len)
                for g in range(SWA_KV_HEADS):
                    feat = slice(g * HEAD_DIM, (g + 1) * HEAD_DIM)
                    nk_ref[q, 0, g, :, :] = sk_t[feat, cols]
                    nv_ref[q, 0, g, :, :] = sv_t[feat, cols]

    @pl.when(s == 0)
    def _():
        normalise_next(0)

    for parity in range(2):
        on_parity = (s % 2) == parity

        @pl.when((s >= 1) & (s <= n_prompt_tiles) & on_parity)
        def _():
            body(rope=False, emit_cache=True, parity=parity)

        @pl.when((s > n_prompt_tiles) & on_parity)
        def _():
            body(rope=True, emit_cache=False, parity=parity)


def _project(xp, xs, mod, gain, w_in, wg, bg, qkg, rope_tabs, seq_p, seq_s):
    n_p, n_s = xp.shape[0], xs.shape[0]
    tp, ts = n_p // TM, n_s // TM
    n = n_p + n_s
    seqs_per_tile = TM // seq_p
    tiles_per_seq = seq_s // TM
    n_seq_p = n_p // seq_p
    const = lambda shape: pl.BlockSpec(shape, lambda s: (0,) * len(shape))
    n_tiles = tp + ts
    out_tile = lambda s: jnp.maximum(s - 1, 0)
    rope_spec = pl.BlockSpec((TM, LANES), lambda s: (jnp.maximum(out_tile(s) - tp, 0) % tiles_per_seq, 0))
    in_specs = [pl.BlockSpec((TM, D_MODEL), lambda s: (jnp.minimum(s, tp - 1), 0)),
                pl.BlockSpec((TM, D_MODEL), lambda s: (jnp.clip(s - tp, 0, ts - 1), 0)),
                const(mod.shape),
                const((1, D_MODEL)),
                pl.BlockSpec((D_IN, D_MODEL), lambda s: (0, 0), pipeline_mode=pl.Buffered(1)),
                const((LANES, 2 * GLA_WIDTH)), const((1, 2 * GLA_WIDTH)), const((1, SWA_WIDTH + KV_WIDTH)),
                rope_spec, rope_spec, rope_spec]
    row = lambda w: pl.BlockSpec((TM, w), lambda s: (out_tile(s), 0))
    cache_spec = pl.BlockSpec((seqs_per_tile, 1, SWA_KV_HEADS, HEAD_DIM, seq_p),
                              lambda s: (jnp.minimum(out_tile(s), tp - 1), 0, 0, 0, 0))
    out_specs = [row(GLA_WIDTH)] * 6 + [row(SWA_WIDTH), row(KV_WIDTH), row(KV_WIDTH)] + [cache_spec] * 2
    cache_shape = (n_seq_p, 1, SWA_KV_HEADS, HEAD_DIM, seq_p)
    out_shape = ([jax.ShapeDtypeStruct((n, GLA_WIDTH), BF16)] * 4
                 + [jax.ShapeDtypeStruct((n, GLA_WIDTH), F32)] * 2
                 + [jax.ShapeDtypeStruct((n, SWA_WIDTH), BF16),
                    jax.ShapeDtypeStruct((n, KV_WIDTH), BF16), jax.ShapeDtypeStruct((n, KV_WIDTH), BF16)]
                 + [jax.ShapeDtypeStruct(cache_shape, F32)] * 2)
    return pl.pallas_call(
        functools.partial(_proj_kernel, n_prompt_tiles=tp, n_tiles=n_tiles, tiles_per_seq=tiles_per_seq,
                          seqs_per_tile=seqs_per_tile, seq_len=seq_p),
        grid=(n_tiles + 1,), in_specs=in_specs, out_specs=out_specs, out_shape=out_shape,
        scratch_shapes=[pltpu.VMEM((D_IN, D_MODEL), BF16)] + [pltpu.VMEM((TM, D_MODEL), BF16)] * 2,
        compiler_params=_cparams(1), name="adaln_in_proj",
    )(xp, xs, mod, gain, w_in, wg, bg, qkg, *rope_tabs)


def _gla_kernel(*refs, seq_len, seqs, has_state, emit_state, with_attention, n_cast):
    (q_ref, k_ref, v_ref, gf_ref, gb_ref, sgg_ref, norm_ref,
     trif_ref, trib_ref, keepf_ref, keepb_ref) = refs[:11]
    pos = 11
    if has_state:
        s0f_ref, s0b_ref = refs[pos:pos + 2]
        pos += 2
    if with_attention:
        sink_ref, aq_ref, ak_ref, av_ref = refs[pos:pos + 4]
        pos += 4
    if with_attention == "latent":
        kc_ref, vc_ref = refs[pos:pos + 2]
        pos += 2
    cast_in = refs[pos:pos + n_cast]
    pos += n_cast
    og_ref = refs[pos]
    pos += 1
    if emit_state:
        sf_ref, sb_ref = refs[pos:pos + 2]
        pos += 2
    if with_attention:
        osw_ref = refs[pos]
        pos += 1
    cast_out = refs[pos:pos + n_cast]
    pos += n_cast
    accf_ref, accb_ref = refs[pos:pos + 2]

    for src, dst in zip(cast_in, cast_out):
        dst[...] = src[...].astype(BF16)

    n_groups = seq_len // GROUP
    per_group = GROUP // CHUNK
    n_pairs = q_ref.shape[1] // LANES
    lane = lax.broadcasted_iota(jnp.int32, (1, LANES), 1)
    head0 = lane < HEAD_DIM
    m0 = jnp.where(head0, 1.0, 0.0).astype(BF16)
    m1 = jnp.where(head0, 0.0, 1.0).astype(BF16)
    row_top = lax.broadcasted_iota(jnp.int32, (LANES, 1), 0) < HEAD_DIM

    def stack_heads(x):
        return jnp.concatenate([x * m0, x * m1], axis=0)

    def direction(row0, lanes, g_ref, tri_ref, keep_ref, state, acc_ref, backward):
        rows = pl.ds(row0, GROUP)
        g = g_ref[rows, lanes]
        g_hi = g.astype(BF16)
        g_lo = (g - g_hi.astype(F32)).astype(BF16)
        r = _dot(tri_ref[...], jnp.concatenate([g_hi, g_lo], axis=-1))
        yield
        b = (r[:, :LANES] + r[:, LANES:]).reshape(per_group, CHUNK, LANES)
        edge = 0 if backward else CHUNK - 1
        mid = CHUNK // 2
        b_edge = b[:, edge:edge + 1, :]
        b_mid = b[:, mid:mid + 1, :]
        q = q_ref[rows, lanes].astype(F32).reshape(per_group, CHUNK, LANES)
        k = k_ref[rows, lanes].astype(F32).reshape(per_group, CHUNK, LANES)
        v = v_ref[rows, lanes]
        q_mid = q * jnp.exp(b - b_mid)
        k_mid = k * jnp.exp(b_mid - b)
        q_in = q_mid.astype(BF16)
        k_in = k_mid.astype(BF16).reshape(GROUP, LANES)
        q_st = (q_mid * jnp.exp(b_mid)).astype(BF16)
        k_st = (k_mid * jnp.exp(b_edge - b_mid)).astype(BF16).reshape(GROUP, LANES)
        decay = jnp.exp(b_edge)
        keep = keep_ref[...]
        pair_rows = 2 * CHUNK
        n_cp = per_group // 2
        yield
        scores, kvs = [], []
        for p in range(n_cp):
            c0 = 2 * p
            prow = slice(p * pair_rows, (p + 1) * pair_rows)
            q_pair = jnp.concatenate([stack_heads(q_in[c0]), stack_heads(q_in[c0 + 1])], axis=0)
            scores.append(_dot_nt(q_pair, k_in[prow, :]))
            v_t = v[prow, :].astype(F32).T.astype(BF16)
            k_pair = k_st[prow, :]
            zero = jnp.zeros_like(k_pair)
            k_blk = jnp.concatenate([jnp.where(row_top, k_pair, zero), jnp.where(row_top, zero, k_pair)], axis=-1)
            kvs.append(_dot(v_t, k_blk))
        yield
        intras = [_dot(scores[p].astype(BF16) * keep, v[p * pair_rows:(p + 1) * pair_rows, :]) for p in range(n_cp)]
        yield
        for p in (range(n_cp - 1, -1, -1) if backward else range(n_cp)):
            for j in ((1, 0) if backward else (0, 1)):
                c = 2 * p + j
                inter = _dot_nt(stack_heads(q_st[c]), state.astype(BF16))
                tot = intras[p][j * pair_rows:(j + 1) * pair_rows, :] + inter
                acc_ref[pl.ds(row0 + c * CHUNK, CHUNK), lanes] = jnp.where(head0, tot[:CHUNK, :], tot[CHUNK:, :])
                state = state * decay[c] + kvs[p][:, j * LANES:(j + 1) * LANES]
            yield
        return state

    def load_state(s_ref, q, p):
        z = jnp.zeros((HEAD_DIM, HEAD_DIM), F32)
        top = jnp.concatenate([s_ref[q, 0, 2 * p, :, :].T, z], axis=-1)
        bot = jnp.concatenate([z, s_ref[q, 0, 2 * p + 1, :, :].T], axis=-1)
        return jnp.concatenate([top, bot], axis=0)

    per_dir = seqs * n_pairs
    if has_state:
        init = tuple(load_state(s_ref, q, p)
                     for s_ref in (s0f_ref, s0b_ref) for q in range(seqs) for p in range(n_pairs))
    else:
        init = (jnp.zeros((LANES, LANES), F32),) * (2 * per_dir)

    def body(t, carry):
        row_f = t * GROUP
        row_b = (n_groups - 1 - t) * GROUP
        if n_groups > 1:
            row_f = pl.multiple_of(row_f, GROUP)
            row_b = pl.multiple_of(row_b, GROUP)
        units = []
        for backward, row, g_ref, tri_ref, keep_ref, acc_ref in (
                (False, row_f, gf_ref, trif_ref, keepf_ref, accf_ref),
                (True, row_b, gb_ref, trib_ref, keepb_ref, accb_ref)):
            for q in range(seqs):
                for p in range(n_pairs):
                    lanes = slice(p * LANES, (p + 1) * LANES)
                    state = carry[backward * per_dir + q * n_pairs + p]
                    units.append(direction(q * seq_len + row, lanes, g_ref, tri_ref, keep_ref, state, acc_ref,
                                           backward))
        return units

    if n_groups == 1:
        units = body(0, init)
        if with_attention:
            for q in range(seqs):
                units += _context_attention_units(sink_ref, aq_ref, ak_ref, av_ref,
                                                  slice(q * seq_len, (q + 1) * seq_len))
        results = _run_staged(units)
        final = results[:2 * per_dir]
        if with_attention:
            for q in range(seqs):
                heads = [o for outs in results[2 * per_dir + q * SWA_KV_HEADS:
                                               2 * per_dir + (q + 1) * SWA_KV_HEADS] for o in outs]
                osw_ref[q * seq_len:(q + 1) * seq_len, :] = jnp.concatenate(heads, axis=-1).astype(BF16)
    else:
        assert with_attention != "context" and (seqs == 1 or not with_attention)
        if with_attention:
            ctx_kv = [(kc_ref[0, 0, g, :, :].astype(BF16), vc_ref[0, 0, g, :, :].astype(BF16))
                      for g in range(SWA_KV_HEADS)]
            q_blocks = GROUP // QBLOCK

        def loop_body(t, carry):
            units = body(t, carry)
            n_gla = len(units)
            if with_attention:
                for j in range(q_blocks):
                    units += _latent_attention_units(sink_ref, aq_ref, ak_ref, av_ref, ctx_kv,
                                                     t * q_blocks + j, seq_len)
            results = _run_staged(units)
            if with_attention:
                for j in range(q_blocks):
                    heads = [o for outs in results[n_gla + j * SWA_KV_HEADS:n_gla + (j + 1) * SWA_KV_HEADS]
                             for o in outs]
                    rows = pl.ds(pl.multiple_of((t * q_blocks + j) * QBLOCK, QBLOCK), QBLOCK)
                    osw_ref[rows, :] = jnp.concatenate(heads, axis=-1).astype(BF16)
            return tuple(results[:n_gla])

        final = lax.fori_loop(0, n_groups, loop_body, init)
    if emit_state:
        for d, s_ref in enumerate((sf_ref, sb_ref)):
            for q in range(seqs):
                for p in range(n_pairs):
                    fin = final[d * per_dir + q * n_pairs + p]
                    s_ref[q, 0, 2 * p, :, :] = fin[:HEAD_DIM, :HEAD_DIM].T
                    s_ref[q, 0, 2 * p + 1, :, :] = fin[HEAD_DIM:, HEAD_DIM:].T

    o = accf_ref[...] + accb_ref[...]
    o = _head_rms_norm(o, norm_ref[...])
    og_ref[...] = (o * sgg_ref[...].astype(F32)).astype(BF16)


def _gla_constants():
    i = np.arange(GROUP)[:, None]
    j = np.arange(GROUP)[None, :]
    same = (i // CHUNK) == (j // CHUNK)
    tri_f = same & (j <= i)
    tri_b = same & (j >= i)
    r = np.arange(4 * CHUNK)[:, None]
    l = np.arange(LANES)[None, :]
    same = (r // (2 * CHUNK)) == (l // CHUNK)
    keep_f = same & ((l % CHUNK) <= (r % CHUNK))
    keep_b = same & ((l % CHUNK) >= (r % CHUNK))
    return tuple(jnp.asarray(m.astype(np.float32), dtype=BF16) for m in (tri_f, tri_b, keep_f, keep_b))


def _gla(gq, gk, gv, gf, gb, sgg, norm_row, s0f, s0b, n_seq, seq_len, row_block0, emit_state, seqs=1,
         attention=None, cast_weights=()):
    has_state = s0f is not None
    steps = n_seq // seqs
    rows_per_step = seqs * seq_len
    blk = pl.BlockSpec((rows_per_step, GLA_WIDTH), lambda b: (row_block0 // seqs + b, 0))
    out_blk = pl.BlockSpec((rows_per_step, GLA_WIDTH), lambda b: (b, 0))
    st_spec = pl.BlockSpec((seqs, 1, GLA_HEADS, HEAD_DIM, HEAD_DIM), lambda b: (b, 0, 0, 0, 0))
    const = lambda shape: pl.BlockSpec(shape, lambda b: (0, 0))
    in_specs = ([blk] * 6 + [const((1, GLA_WIDTH))] + [const((GROUP, GROUP))] * 2
                + [const((4 * CHUNK, LANES))] * 2)
    args = [gq, gk, gv, gf, gb, sgg, norm_row, *_gla_constants()]
    if has_state:
        in_specs += [st_spec] * 2
        args += [s0f, s0b]
    out_specs = [out_blk]
    out_shape = [jax.ShapeDtypeStruct((n_seq * seq_len, GLA_WIDTH), BF16)]
    if emit_state:
        out_specs += [st_spec] * 2
        out_shape += [jax.ShapeDtypeStruct((n_seq, 1, GLA_HEADS, HEAD_DIM, HEAD_DIM), F32)] * 2
    with_attention = None
    if attention is not None:
        sink, sq, sk, sv = attention[:4]
        with_attention = "context"
        in_specs += [pl.BlockSpec(memory_space=pltpu.SMEM),
                     pl.BlockSpec((rows_per_step, SWA_WIDTH), lambda b: (row_block0 // seqs + b, 0)),
                     pl.BlockSpec((rows_per_step, KV_WIDTH), lambda b: (row_block0 // seqs + b, 0)),
                     pl.BlockSpec((rows_per_step, KV_WIDTH), lambda b: (row_block0 // seqs + b, 0))]
        args += [sink, sq, sk, sv]
        if len(attention) > 4:
            with_attention = "latent"
            cache_k_t, cache_v_t = attention[4:]
            past = cache_k_t.shape[4]
            cache_spec = pl.BlockSpec((seqs, 1, SWA_KV_HEADS, HEAD_DIM, past), lambda b: (b, 0, 0, 0, 0))
            in_specs += [cache_spec] * 2
            args += [cache_k_t, cache_v_t]
        out_specs.append(pl.BlockSpec((rows_per_step, SWA_WIDTH), lambda b: (b, 0)))
        out_shape.append(jax.ShapeDtypeStruct((n_seq * seq_len, SWA_WIDTH), BF16))
    cast_specs, cast_shapes = _cast_specs(cast_weights, steps)
    in_specs += cast_specs
    args += list(cast_weights)
    out_specs += cast_specs
    out_shape += cast_shapes
    return pl.pallas_call(
        functools.partial(_gla_kernel, seq_len=seq_len, seqs=seqs, has_state=has_state, emit_state=emit_state,
                          with_attention=with_attention, n_cast=len(cast_weights)),
        grid=(steps,), in_specs=in_specs, out_specs=out_specs, out_shape=out_shape,
        scratch_shapes=[pltpu.VMEM((rows_per_step, GLA_WIDTH), F32)] * 2,
        compiler_params=_cparams(1), name="gla_bidirectional",
    )(*args)


def _group_attention(q, heads, sink_ref, score_fn, value_fn):
    rows = q.shape[0]
    q_stack = jnp.concatenate([q[:, h * HEAD_DIM:(h + 1) * HEAD_DIM] for h in heads], axis=0)
    scores = score_fn(q_stack)
    yield
    slabs = [slice(j * rows, (j + 1) * rows) for j in range(len(heads))]
    ms = []
    for slab, h in zip(slabs, heads):
        m = sink_ref[h]
        for s in scores:
            m = jnp.maximum(m, jnp.max(s[slab, :], axis=-1, keepdims=True))
        ms.append(m)
    yield
    e_parts = [[] for _ in scores]
    denoms = []
    for slab, h, m in zip(slabs, heads, ms):
        denom = jnp.exp(sink_ref[h] - m)
        for part, s in zip(e_parts, scores):
            e = jnp.exp(s[slab, :] - m)
            denom = denom + jnp.sum(e, axis=-1, keepdims=True)
            part.append(e.astype(BF16))
        denoms.append(denom)
    yield
    acc = value_fn([jnp.concatenate(part, axis=0) for part in e_parts])
    yield
    return [acc[slab, :] / denom for slab, denom in zip(slabs, denoms)]


def _run_staged(units):
    results = [None] * len(units)
    live = list(range(len(units)))
    while live:
        still = []
        for u in live:
            try:
                next(units[u])
                still.append(u)
            except StopIteration as stop:
                results[u] = stop.value
        live = still
    return results


def _context_attention_units(sink_ref, q_ref, k_ref, v_ref, rows):
    q = q_ref[rows, :]
    k = k_ref[rows, :]
    v = v_ref[rows, :]
    units = []
    for g in range(SWA_KV_HEADS):
        k_g = k[:, g * HEAD_DIM:(g + 1) * HEAD_DIM]
        v_g = v[:, g * HEAD_DIM:(g + 1) * HEAD_DIM]
        units.append(_group_attention(
            q, list(range(SWA_GROUP * g, SWA_GROUP * (g + 1))), sink_ref,
            lambda qs, k_g=k_g: [_dot_nt(qs, k_g)],
            lambda es, v_g=v_g: _dot(es[0], v_g)))
    return units


def _cast_specs(cast_weights, steps, step_of=lambda *idx: idx[0]):
    specs = [pl.BlockSpec((w.shape[0] // steps, w.shape[1]), lambda *idx: (step_of(*idx), 0))
             for w in cast_weights]
    shapes = [jax.ShapeDtypeStruct(w.shape, BF16) for w in cast_weights]
    return specs, shapes


def _latent_attention_units(sink_ref, q_ref, k_ref, v_ref, ctx_kv, i, seq_len):
    span = QBLOCK + 2 * WINDOW
    start = pl.multiple_of(jnp.clip(i * QBLOCK - WINDOW, 0, seq_len - span), QBLOCK)
    q = q_ref[pl.ds(pl.multiple_of(i * QBLOCK, QBLOCK), QBLOCK), :]
    k = k_ref[pl.ds(start, span), :]
    v = v_ref[pl.ds(start, span), :]
    qpos = i * QBLOCK + lax.broadcasted_iota(jnp.int32, (QBLOCK, span), 0)
    kpos = start + lax.broadcasted_iota(jnp.int32, (QBLOCK, span), 1)
    valid = jnp.abs(qpos - kpos) <= WINDOW
    valid = jnp.concatenate([valid] * SWA_GROUP, axis=0)
    units = []
    for g in range(SWA_KV_HEADS):
        k_g = k[:, g * HEAD_DIM:(g + 1) * HEAD_DIM]
        v_g = v[:, g * HEAD_DIM:(g + 1) * HEAD_DIM]
        kc_t, vc_t = ctx_kv[g]
        units.append(_group_attention(
            q, list(range(SWA_GROUP * g, SWA_GROUP * (g + 1))), sink_ref,
            lambda qs, k_g=k_g, kc_t=kc_t: [jnp.where(valid, _dot_nt(qs, k_g), NEG_INF), _dot(qs, kc_t)],
            lambda es, v_g=v_g, vc_t=vc_t: _dot(es[0], v_g) + _dot_nt(es[1], vc_t)))
    return units


def _out_kernel(xp_ref, xs_ref, ogp_ref, ogs_ref, oswp_ref, osws_ref, mod_ref, gain_ref,
                wo_hbm, w1_hbm, w2_hbm, yp_ref, ys_ref, wo_ref, w1_ref, w2_ref, sems,
                *, n_prompt_tiles, tiles_per_seq):
    i = pl.program_id(0)
    row = _mod_row(i, n_prompt_tiles, tiles_per_seq)
    n_ff = D_FF // FF_TILE
    ff_cols = [pl.ds(j * FF_TILE, FF_TILE) for j in range(n_ff)]
    wo_copy = pltpu.make_async_copy(wo_hbm, wo_ref, sems.at[0])
    w1_copies = [pltpu.make_async_copy(w1_hbm.at[:, c], w1_ref.at[:, c], sems.at[1 + j])
                 for j, c in enumerate(ff_cols)]
    w2_copies = [pltpu.make_async_copy(w2_hbm.at[c, :], w2_ref.at[c, :], sems.at[1 + n_ff + j])
                 for j, c in enumerate(ff_cols)]

    def body(x_ref, og_ref, osw_ref, y_ref, first):
        if first:
            wo_copy.start()
            for c1, c2 in zip(w1_copies, w2_copies):
                c1.start()
                c2.start()
            wo_copy.wait()
        mix = _dot(og_ref[...], wo_ref[:GLA_WIDTH, :]) + _dot(osw_ref[...], wo_ref[GLA_WIDTH:, :])
        x1 = x_ref[...] + _mod_chunk(mod_ref, row, 2) * mix
        h = _adaln(x1, gain_ref[...], _mod_chunk(mod_ref, row, 3), _mod_chunk(mod_ref, row, 4)).astype(BF16)
        acc = None
        for j in range(n_ff):
            cols = slice(j * FF_TILE, (j + 1) * FF_TILE)
            if first:
                w1_copies[j].wait()
            a = jnp.maximum(_dot(h, w1_ref[:, cols]), 0.0)
            if first:
                w2_copies[j].wait()
            part = _dot((a * a).astype(BF16), w2_ref[cols, :])
            acc = part if acc is None else acc + part
        y_ref[...] = x1 + _mod_chunk(mod_ref, row, 5) * acc

    @pl.when(i == 0)
    def _():
        body(xp_ref, ogp_ref, oswp_ref, yp_ref, first=True)

    @pl.when((i > 0) & (i < n_prompt_tiles))
    def _():
        body(xp_ref, ogp_ref, oswp_ref, yp_ref, first=False)

    @pl.when(i >= n_prompt_tiles)
    def _():
        body(xs_ref, ogs_ref, osws_ref, ys_ref, first=False)


def _mix_out(xp, xs, og_p, og_s, osw_p, osw_s, mod, gain, wo, w1, w2, seq_s):
    n_p, n_s = xp.shape[0], xs.shape[0]
    tp, ts = n_p // TM, n_s // TM
    tiles_per_seq = seq_s // TM
    const = lambda shape: pl.BlockSpec(shape, lambda i: (0,) * len(shape), pipeline_mode=pl.Buffered(1))
    row_p = lambda w: pl.BlockSpec((TM, w), lambda i: (jnp.minimum(i, tp - 1), 0))
    row_s = lambda w: pl.BlockSpec((TM, w), lambda i: (jnp.maximum(i - tp, 0), 0))
    return pl.pallas_call(
        functools.partial(_out_kernel, n_prompt_tiles=tp, tiles_per_seq=tiles_per_seq),
        grid=(tp + ts,),
        in_specs=[row_p(D_MODEL), row_s(D_MODEL), row_p(GLA_WIDTH), row_s(GLA_WIDTH),
                  row_p(SWA_WIDTH), row_s(SWA_WIDTH),
                  const(mod.shape),
                  const((1, D_MODEL))] + [pl.BlockSpec(memory_space=pl.ANY)] * 3,
        out_specs=[row_p(D_MODEL), row_s(D_MODEL)],
        out_shape=[jax.ShapeDtypeStruct((n_p, D_MODEL), F32), jax.ShapeDtypeStruct((n_s, D_MODEL), F32)],
        scratch_shapes=[pltpu.VMEM(wo.shape, BF16), pltpu.VMEM(w1.shape, BF16), pltpu.VMEM(w2.shape, BF16),
                        pltpu.SemaphoreType.DMA((1 + 2 * (D_FF // FF_TILE),))],
        compiler_params=_cparams(1), name="out_proj_mlp",
    )(xp, xs, og_p, og_s, osw_p, osw_s, mod, gain, wo, w1, w2)


def _rope_tables(seq_len):
    axis_dim = HEAD_DIM // 2
    half = axis_dim // 2
    t = np.arange(seq_len)
    rowp = (t // GRID_W).astype(np.float64)
    colp = (t % GRID_W).astype(np.float64)
    freqs = ROPE_THETA ** (-np.arange(half, dtype=np.float64) / half)
    lane = np.arange(LANES)
    d = lane % HEAD_DIM
    posv = np.where((d // axis_dim)[None, :] == 0, rowp[:, None], colp[:, None])
    ang = posv * freqs[d % half][None, :]
    second = ((d % axis_dim) >= half)[None, :]
    cos, sin = np.cos(ang), np.sin(ang)
    zero = np.zeros_like(sin)
    tabs = (cos, np.where(second, sin, zero), np.where(second, zero, -sin))
    return tuple(jnp.asarray(tab.astype(np.float32)) for tab in tabs)


def kernel(x_prompt, x_sample, c, cache_k, cache_v, state_gla_fwd, state_gla_bwd, c_ctx, w_ada, b_ada, norm_mix, norm_ff, w_in, w_gk_fwd, b_gk_fwd, w_gk_bwd, b_gk_bwd, gla_norm, q_norm, k_norm, sink, w_o, w_ff1, w_ff2):
    depth = w_in.shape[0]
    assert depth == 1, "single trunk layer"
    bp, tp, _ = x_prompt.shape
    bs, ts, _ = x_sample.shape
    n_p = bp * tp
    l = 0

    cond = jnp.concatenate([c_ctx[None, :], c, jnp.zeros((8 - 1 - bs, D_MODEL), F32)], axis=0)
    mod = _modulation(cond, w_ada[l], b_ada[l])

    zero = jnp.zeros((GATE_RANK, GLA_WIDTH), F32)
    wg = jnp.concatenate([jnp.concatenate([w_gk_fwd[l], zero], axis=1),
                          jnp.concatenate([zero, w_gk_bwd[l]], axis=1),
                          jnp.zeros((LANES - 2 * GATE_RANK, 2 * GLA_WIDTH), F32)], axis=0).astype(BF16)
    bg = jnp.concatenate([b_gk_fwd[l], b_gk_bwd[l]])[None, :]
    qkg = jnp.concatenate([jnp.tile(q_norm[l], SWA_HEADS), jnp.tile(k_norm[l], SWA_KV_HEADS)])[None, :]
    norm2 = jnp.tile(gla_norm[l], GLA_HEADS)[None, :]
    gain_mix = norm_mix[l][None, :]
    gain_ff = norm_ff[l][None, :]
    sink_l = sink[l]

    w_in_t = jnp.transpose(w_in[l])
    cache_k_t = jnp.swapaxes(cache_k, 3, 4)
    cache_v_t = jnp.swapaxes(cache_v, 3, 4)

    xp = x_prompt.reshape(n_p, D_MODEL)
    xs = x_sample.reshape(bs * ts, D_MODEL)
    (gq, gk, gv, sgg, gf, gb, sq, sk, sv, new_k_t, new_v_t) = _project(
        xp, xs, mod, gain_mix, w_in_t, wg, bg, qkg, _rope_tables(ts), tp, ts)
    new_k = jnp.swapaxes(new_k_t, 3, 4)
    new_v = jnp.swapaxes(new_v_t, 3, 4)

    og_p, s_f, s_b, osw_p, wo, w1, w2 = _gla(
        gq, gk, gv, gf, gb, sgg, norm2, None, None, bp, tp, 0, True, seqs=2,
        attention=(sink_l, sq, sk, sv), cast_weights=(w_o[l], w_ff1[l], w_ff2[l]))
    og_s, osw_s = _gla(gq, gk, gv, gf, gb, sgg, norm2, state_gla_fwd, state_gla_bwd, bs, ts, n_p // ts, False,
                       attention=(sink_l, sq, sk, sv, cache_k_t, cache_v_t))

    yp, ys = _mix_out(xp, xs, og_p, og_s, osw_p, osw_s, mod, gain_ff, wo, w1, w2, ts)
    return (yp.reshape(bp, tp, D_MODEL), ys.reshape(bs, ts, D_MODEL), new_k, new_v, s_f, s_b)
```

```python
import functools

import numpy as np
import jax
import jax.numpy as jnp
from jax import lax
from jax.experimental import pallas as pl
from jax.experimental.pallas import tpu as pltpu

F32 = jnp.float32
BF16 = jnp.bfloat16

D_MODEL = 1024
GRID_W = 64
HEAD_DIM = 64
GLA_HEADS = 8
GLA_WIDTH = GLA_HEADS * HEAD_DIM
GATE_RANK = 16
GATE_TEMP = 16.0
CHUNK = 64
GROUP = 4 * CHUNK
SWA_HEADS = 8
SWA_KV_HEADS = 2
SWA_GROUP = SWA_HEADS // SWA_KV_HEADS
SWA_WIDTH = SWA_HEADS * HEAD_DIM
KV_WIDTH = SWA_KV_HEADS * HEAD_DIM
WINDOW = 128
QBLOCK = 128
D_FF = 4 * D_MODEL
ROPE_THETA = 10000.0
EPS = 1e-6
NEG_INF = -1e30
LANES = 128
TM = 512
FF_TILE = 1024

D_IN = 2848
R_GQ, R_GK, R_GV, R_GG, R_LR = 0, 512, 1024, 1536, 2048
R_SQ = R_LR + 2 * GATE_RANK
R_SV = R_SQ + SWA_WIDTH + KV_WIDTH

VMEM_LIMIT = 56 * 1024 * 1024


def _cparams(n_axes, flags=None):
    return pltpu.CompilerParams(dimension_semantics=("arbitrary",) * n_axes,
                                vmem_limit_bytes=VMEM_LIMIT, flags=flags)


def _mod_kernel(cond_ref, w_ref, b_ref, o_ref):
    k = pl.program_id(0)
    cnd = cond_ref[...]
    s = (cnd * jax.nn.sigmoid(cnd)).astype(BF16)
    part = _dot(s, w_ref[...].astype(BF16))

    @pl.when(k == 0)
    def _():
        o_ref[...] = part + b_ref[...]

    @pl.when(k > 0)
    def _():
        o_ref[...] += part


def _modulation(cond, w_ada, b_ada):
    tk = 256
    n_out = w_ada.shape[1]
    return pl.pallas_call(
        _mod_kernel,
        grid=(D_MODEL // tk,),
        in_specs=[pl.BlockSpec((8, tk), lambda k: (0, k)),
                  pl.BlockSpec((tk, n_out), lambda k: (k, 0)),
                  pl.BlockSpec((1, n_out), lambda k: (0, 0))],
        out_specs=pl.BlockSpec((8, n_out), lambda k: (0, 0)),
        out_shape=jax.ShapeDtypeStruct((8, n_out), F32),
        compiler_params=_cparams(1),
        name="modulation",
    )(cond, w_ada, b_ada.reshape(1, n_out))


def _mod_row(tile, n_prompt_tiles, tiles_per_seq):
    return jnp.where(tile < n_prompt_tiles, 0, 1 + (tile - n_prompt_tiles) // tiles_per_seq)


def _mod_chunk(mod_ref, row, j):
    return mod_ref[pl.ds(row, 1), j * D_MODEL:(j + 1) * D_MODEL]


def _adaln(x, gain, shift, scale):
    ms = jnp.mean(x * x, axis=-1, keepdims=True)
    xn = x * lax.rsqrt(ms + EPS)
    return xn * (gain * (1.0 + scale)) + shift


def _head_rms_norm(y, gain):
    cols = y.shape[1]
    lane = lax.broadcasted_iota(jnp.int32, (1, LANES), 1)
    first = lane < HEAD_DIM
    outs = []
    for p in range(cols // LANES):
        s = y[:, p * LANES:(p + 1) * LANES]
        s = s * s
        tot = jnp.sum(s, axis=-1, keepdims=True)
        lo = jnp.sum(jnp.where(first, s, 0.0), axis=-1, keepdims=True)
        outs.append(jnp.where(first, lo, tot - lo))
    sums = outs[0] if len(outs) == 1 else jnp.concatenate(outs, axis=-1)
    return y * lax.rsqrt(sums + HEAD_DIM * EPS) * (gain * (HEAD_DIM ** 0.5))


def _log_sigmoid(x):
    return jnp.minimum(x, 0.0) - jnp.log(1.0 + jnp.exp(-jnp.abs(x)))


def _dot(a, b):
    return jnp.dot(a, b, preferred_element_type=F32)


def _dot_nt(a, b):
    return lax.dot_general(a, b, (((1,), (1,)), ((), ())), preferred_element_type=F32)


def _proj_kernel(xp_ref, xs_ref, mod_ref, gain_ref, w_ref, wg_ref, bg_ref, qkg_ref,
                 cos_ref, sin_a_ref, sin_b_ref,
                 gq_ref, gk_ref, gv_ref, sgg_ref, gf_ref, gb_ref, sq_ref, sk_ref, sv_ref, nk_ref, nv_ref,
                 wb_ref, h0_ref, h1_ref, *, n_prompt_tiles, n_tiles, tiles_per_seq, seqs_per_tile, seq_len):
    h_refs = (h0_ref, h1_ref)
    s = pl.program_id(0)

    @pl.when(s == 0)
    def _():
        blk = 32

        def conv(r, carry):
            rows = pl.ds(pl.multiple_of(r * blk, blk), blk)
            wb_ref[rows, :] = w_ref[rows, :].astype(BF16)
            return carry

        lax.fori_loop(0, D_IN // blk, conv, 0)

    def normalise_next(parity):
        x = jnp.where(s < n_prompt_tiles, xp_ref[...], xs_ref[...])
        row = _mod_row(jnp.minimum(s, n_tiles - 1), n_prompt_tiles, tiles_per_seq)
        shift, scale = _mod_chunk(mod_ref, row, 0), _mod_chunk(mod_ref, row, 1)
        h_refs[parity][...] = _adaln(x, gain_ref[...], shift, scale).astype(BF16)

    def body(rope, emit_cache, parity):
        h_ref = h_refs[1 - parity]

        def proj(r0, r1):
            return _dot_nt(h_ref[...], wb_ref[r0:r1, :])

        tail = proj(R_SQ, D_IN)
        lr = proj(R_LR, R_LR + LANES).astype(BF16)
        gate_pre = _dot(lr, wg_ref[...])
        gg = proj(R_GG, R_LR)
        sgg_ref[...] = (gg * jax.nn.sigmoid(gg)).astype(BF16)
        attention_inputs(tail, rope, emit_cache)
        gq_ref[...] = (proj(R_GQ, R_GK) * (HEAD_DIM ** -0.5)).astype(BF16)
        gf_ref[...] = _log_sigmoid(gate_pre[:, :GLA_WIDTH] + bg_ref[:, :GLA_WIDTH]) * (1.0 / GATE_TEMP)
        gk_ref[...] = proj(R_GK, R_GV).astype(BF16)
        gb_ref[...] = _log_sigmoid(gate_pre[:, GLA_WIDTH:] + bg_ref[:, GLA_WIDTH:]) * (1.0 / GATE_TEMP)
        gv_ref[...] = proj(R_GV, R_GG).astype(BF16)
        normalise_next(parity)

    def attention_inputs(tail, rope, emit_cache):
        qk = tail[:, :R_SV - R_SQ]
        qk = _head_rms_norm(qk, qkg_ref[...])
        if rope:
            reps = (SWA_WIDTH + KV_WIDTH) // LANES
            cos = jnp.concatenate([cos_ref[...]] * reps, axis=-1)
            sin_a = jnp.concatenate([sin_a_ref[...]] * reps, axis=-1)
            sin_b = jnp.concatenate([sin_b_ref[...]] * reps, axis=-1)
            width = qk.shape[1]
            quarter = HEAD_DIM // 4
            qk = (qk * cos + pltpu.roll(qk, quarter, 1) * sin_a
                  + pltpu.roll(qk, width - quarter, 1) * sin_b)
        sq_ref[...] = (qk[:, :SWA_WIDTH] * (HEAD_DIM ** -0.5)).astype(BF16)
        sk = qk[:, SWA_WIDTH:]
        sk_ref[...] = sk.astype(BF16)
        sv = tail[:, R_SV - R_SQ:]
        sv_ref[...] = sv.astype(BF16)
        if emit_cache:
            sk_t, sv_t = sk.T, sv.T
            for q in range(seqs_per_tile):
                cols = slice(q * seq_len, (q + 1) * seq_len)
                for g in range(SWA_KV_HEADS):
                    feat = slice(g * HEAD_DIM, (g + 1) * HEAD_DIM)
                    nk_ref[q, 0, g, :, :] = sk_t[feat, cols]
                    nv_ref[q, 0, g, :, :] = sv_t[feat, cols]

    @pl.when(s == 0)
    def _():
        normalise_next(0)

    for parity in range(2):
        on_parity = (s % 2) == parity

        @pl.when((s >= 1) & (s <= n_prompt_tiles) & on_parity)
        def _():
            body(rope=False, emit_cache=True, parity=parity)

        @pl.when((s > n_prompt_tiles) & on_parity)
        def _():
            body(rope=True, emit_cache=False, parity=parity)


def _project(xp, xs, mod, gain, w_in, wg, bg, qkg, rope_tabs, seq_p, seq_s):
    n_p, n_s = xp.shape[0], xs.shape[0]
    tp, ts = n_p // TM, n_s // TM
    n = n_p + n_s
    seqs_per_tile = TM // seq_p
    tiles_per_seq = seq_s // TM
    n_seq_p = n_p // seq_p
    const = lambda shape: pl.BlockSpec(shape, lambda s: (0,) * len(shape))
    n_tiles = tp + ts
    out_tile = lambda s: jnp.maximum(s - 1, 0)
    rope_spec = pl.BlockSpec((TM, LANES), lambda s: (jnp.maximum(out_tile(s) - tp, 0) % tiles_per_seq, 0))
    in_specs = [pl.BlockSpec((TM, D_MODEL), lambda s: (jnp.minimum(s, tp - 1), 0)),
                pl.BlockSpec((TM, D_MODEL), lambda s: (jnp.clip(s - tp, 0, ts - 1), 0)),
                const(mod.shape),
                const((1, D_MODEL)),
                pl.BlockSpec((D_IN, D_MODEL), lambda s: (0, 0), pipeline_mode=pl.Buffered(1)),
                const((LANES, 2 * GLA_WIDTH)), const((1, 2 * GLA_WIDTH)), const((1, SWA_WIDTH + KV_WIDTH)),
                rope_spec, rope_spec, rope_spec]
    row = lambda w: pl.BlockSpec((TM, w), lambda s: (out_tile(s), 0))
    cache_spec = pl.BlockSpec((seqs_per_tile, 1, SWA_KV_HEADS, HEAD_DIM, seq_p),
                              lambda s: (jnp.minimum(out_tile(s), tp - 1), 0, 0, 0, 0))
    out_specs = [row(GLA_WIDTH)] * 6 + [row(SWA_WIDTH), row(KV_WIDTH), row(KV_WIDTH)] + [cache_spec] * 2
    cache_shape = (n_seq_p, 1, SWA_KV_HEADS, HEAD_DIM, seq_p)
    out_shape = ([jax.ShapeDtypeStruct((n, GLA_WIDTH), BF16)] * 4
                 + [jax.ShapeDtypeStruct((n, GLA_WIDTH), F32)] * 2
                 + [jax.ShapeDtypeStruct((n, SWA_WIDTH), BF16),
                    jax.ShapeDtypeStruct((n, KV_WIDTH), BF16), jax.ShapeDtypeStruct((n, KV_WIDTH), BF16)]
                 + [jax.ShapeDtypeStruct(cache_shape, F32)] * 2)
    return pl.pallas_call(
        functools.partial(_proj_kernel, n_prompt_tiles=tp, n_tiles=n_tiles, tiles_per_seq=tiles_per_seq,
                          seqs_per_tile=seqs_per_tile, seq_len=seq_p),
        grid=(n_tiles + 1,), in_specs=in_specs, out_specs=out_specs, out_shape=out_shape,
        scratch_shapes=[pltpu.VMEM((D_IN, D_MODEL), BF16)] + [pltpu.VMEM((TM, D_MODEL), BF16)] * 2,
        compiler_params=_cparams(1), name="adaln_in_proj",
    )(xp, xs, mod, gain, w_in, wg, bg, qkg, *rope_tabs)


def _gla_kernel(*refs, seq_len, seqs, has_state, emit_state, with_attention, n_cast):
    (q_ref, k_ref, v_ref, gf_ref, gb_ref, sgg_ref, norm_ref,
     trif_ref, trib_ref, keepf_ref, keepb_ref) = refs[:11]
    pos = 11
    if has_state:
        s0f_ref, s0b_ref = refs[pos:pos + 2]
        pos += 2
    if with_attention:
        sink_ref, aq_ref, ak_ref, av_ref = refs[pos:pos + 4]
        pos += 4
    if with_attention == "latent":
        kc_ref, vc_ref = refs[pos:pos + 2]
        pos += 2
    cast_in = refs[pos:pos + n_cast]
    pos += n_cast
    og_ref = refs[pos]
    pos += 1
    if emit_state:
        sf_ref, sb_ref = refs[pos:pos + 2]
        pos += 2
    if with_attention:
        osw_ref = refs[pos]
        pos += 1
    cast_out = refs[pos:pos + n_cast]
    pos += n_cast
    accf_ref, accb_ref = refs[pos:pos + 2]

    for src, dst in zip(cast_in, cast_out):
        dst[...] = src[...].astype(BF16)

    n_groups = seq_len // GROUP
    per_group = GROUP // CHUNK
    n_pairs = q_ref.shape[1] // LANES
    lane = lax.broadcasted_iota(jnp.int32, (1, LANES), 1)
    head0 = lane < HEAD_DIM
    m0 = jnp.where(head0, 1.0, 0.0).astype(BF16)
    m1 = jnp.where(head0, 0.0, 1.0).astype(BF16)
    row_top = lax.broadcasted_iota(jnp.int32, (LANES, 1), 0) < HEAD_DIM

    def stack_heads(x):
        return jnp.concatenate([x * m0, x * m1], axis=0)

    def direction(row0, lanes, g_ref, tri_ref, keep_ref, state, acc_ref, backward):
        rows = pl.ds(row0, GROUP)
        g = g_ref[rows, lanes]
        g_hi = g.astype(BF16)
        g_lo = (g - g_hi.astype(F32)).astype(BF16)
        r = _dot(tri_ref[...], jnp.concatenate([g_hi, g_lo], axis=-1))
        yield
        b = (r[:, :LANES] + r[:, LANES:]).reshape(per_group, CHUNK, LANES)
        edge = 0 if backward else CHUNK - 1
        mid = CHUNK // 2
        b_edge = b[:, edge:edge + 1, :]
        b_mid = b[:, mid:mid + 1, :]
        q = q_ref[rows, lanes].astype(F32).reshape(per_group, CHUNK, LANES)
        k = k_ref[rows, lanes].astype(F32).reshape(per_group, CHUNK, LANES)
        v = v_ref[rows, lanes]
        q_mid = q * jnp.exp(b - b_mid)
        k_mid = k * jnp.exp(b_mid - b)
        q_in = q_mid.astype(BF16)
        k_in = k_mid.astype(BF16).reshape(GROUP, LANES)
        q_st = (q_mid * jnp.exp(b_mid)).astype(BF16)
        k_st = (k_mid * jnp.exp(b_edge - b_mid)).astype(BF16).reshape(GROUP, LANES)
        decay = jnp.exp(b_edge)
        keep = keep_ref[...]
        pair_rows = 2 * CHUNK
        n_cp = per_group // 2
        yield
        scores, kvs = [], []
        for p in range(n_cp):
            c0 = 2 * p
            prow = slice(p * pair_rows, (p + 1) * pair_rows)
            q_pair = jnp.concatenate([stack_heads(q_in[c0]), stack_heads(q_in[c0 + 1])], axis=0)
            scores.append(_dot_nt(q_pair, k_in[prow, :]))
            v_t = v[prow, :].astype(F32).T.astype(BF16)
            k_pair = k_st[prow, :]
            zero = jnp.zeros_like(k_pair)
            k_blk = jnp.concatenate([jnp.where(row_top, k_pair, zero), jnp.where(row_top, zero, k_pair)], axis=-1)
            kvs.append(_dot(v_t, k_blk))
        yield
        intras = [_dot(scores[p].astype(BF16) * keep, v[p * pair_rows:(p + 1) * pair_rows, :]) for p in range(n_cp)]
        yield
        for p in (range(n_cp - 1, -1, -1) if backward else range(n_cp)):
            for j in ((1, 0) if backward else (0, 1)):
                c = 2 * p + j
                inter = _dot_nt(stack_heads(q_st[c]), state.astype(BF16))
                tot = intras[p][j * pair_rows:(j + 1) * pair_rows, :] + inter
                acc_ref[pl.ds(row0 + c * CHUNK, CHUNK), lanes] = jnp.where(head0, tot[:CHUNK, :], tot[CHUNK:, :])
                state = state * decay[c] + kvs[p][:, j * LANES:(j + 1) * LANES]
            yield
        return state

    def load_state(s_ref, q, p):
        z = jnp.zeros((HEAD_DIM, HEAD_DIM), F32)
        top = jnp.concatenate([s_ref[q, 0, 2 * p, :, :].T, z], axis=-1)
        bot = jnp.concatenate([z, s_ref[q, 0, 2 * p + 1, :, :].T], axis=-1)
        return jnp.concatenate([top, bot], axis=0)

    per_dir = seqs * n_pairs
    if has_state:
        init = tuple(load_state(s_ref, q, p)
                     for s_ref in (s0f_ref, s0b_ref) for q in range(seqs) for p in range(n_pairs))
    else:
        init = (jnp.zeros((LANES, LANES), F32),) * (2 * per_dir)

    def body(t, carry):
        row_f = t * GROUP
        row_b = (n_groups - 1 - t) * GROUP
        if n_groups > 1:
            row_f = pl.multiple_of(row_f, GROUP)
            row_b = pl.multiple_of(row_b, GROUP)
        units = []
        for backward, row, g_ref, tri_ref, keep_ref, acc_ref in (
                (False, row_f, gf_ref, trif_ref, keepf_ref, accf_ref),
                (True, row_b, gb_ref, trib_ref, keepb_ref, accb_ref)):
            for q in range(seqs):
                for p in range(n_pairs):
                    lanes = slice(p * LANES, (p + 1) * LANES)
                    state = carry[backward * per_dir + q * n_pairs + p]
                    units.append(direction(q * seq_len + row, lanes, g_ref, tri_ref, keep_ref, state, acc_ref,
                                           backward))
        return units

    if n_groups == 1:
        units = body(0, init)
        if with_attention:
            for q in range(seqs):
                units += _context_attention_units(sink_ref, aq_ref, ak_ref, av_ref,
                                                  slice(q * seq_len, (q + 1) * seq_len))
        results = _run_staged(units)
        final = results[:2 * per_dir]
        if with_attention:
            for q in range(seqs):
                heads = [o for outs in results[2 * per_dir + q * SWA_KV_HEADS:
                                               2 * per_dir + (q + 1) * SWA_KV_HEADS] for o in outs]
                osw_ref[q * seq_len:(q + 1) * seq_len, :] = jnp.concatenate(heads, axis=-1).astype(BF16)
    else:
        assert with_attention != "context" and (seqs == 1 or not with_attention)
        if with_attention:
            ctx_kv = [(kc_ref[0, 0, g, :, :].astype(BF16), vc_ref[0, 0, g, :, :].astype(BF16))
                      for g in range(SWA_KV_HEADS)]
            q_blocks = GROUP // QBLOCK

        def loop_body(t, carry):
            units = body(t, carry)
            n_gla = len(units)
            if with_attention:
                for j in range(q_blocks):
                    units += _latent_attention_units(sink_ref, aq_ref, ak_ref, av_ref, ctx_kv,
                                                     t * q_blocks + j, seq_len)
            results = _run_staged(units)
            if with_attention:
                for j in range(q_blocks):
                    heads = [o for outs in results[n_gla + j * SWA_KV_HEADS:n_gla + (j + 1) * SWA_KV_HEADS]
                             for o in outs]
                    rows = pl.ds(pl.multiple_of((t * q_blocks + j) * QBLOCK, QBLOCK), QBLOCK)
                    osw_ref[rows, :] = jnp.concatenate(heads, axis=-1).astype(BF16)
            return tuple(results[:n_gla])

        final = lax.fori_loop(0, n_groups, loop_body, init)
    if emit_state:
        for d, s_ref in enumerate((sf_ref, sb_ref)):
            for q in range(seqs):
                for p in range(n_pairs):
                    fin = final[d * per_dir + q * n_pairs + p]
                    s_ref[q, 0, 2 * p, :, :] = fin[:HEAD_DIM, :HEAD_DIM].T
                    s_ref[q, 0, 2 * p + 1, :, :] = fin[HEAD_DIM:, HEAD_DIM:].T

    o = accf_ref[...] + accb_ref[...]
    o = _head_rms_norm(o, norm_ref[...])
    og_ref[...] = (o * sgg_ref[...].astype(F32)).astype(BF16)


def _gla_constants():
    i = np.arange(GROUP)[:, None]
    j = np.arange(GROUP)[None, :]
    same = (i // CHUNK) == (j // CHUNK)
    tri_f = same & (j <= i)
    tri_b = same & (j >= i)
    r = np.arange(4 * CHUNK)[:, None]
    l = np.arange(LANES)[None, :]
    same = (r // (2 * CHUNK)) == (l // CHUNK)
    keep_f = same & ((l % CHUNK) <= (r % CHUNK))
    keep_b = same & ((l % CHUNK) >= (r % CHUNK))
    return tuple(jnp.asarray(m.astype(np.float32), dtype=BF16) for m in (tri_f, tri_b, keep_f, keep_b))


def _gla(gq, gk, gv, gf, gb, sgg, norm_row, s0f, s0b, n_seq, seq_len, row_block0, emit_state, seqs=1,
         attention=None, cast_weights=()):
    has_state = s0f is not None
    steps = n_seq // seqs
    rows_per_step = seqs * seq_len
    blk = pl.BlockSpec((rows_per_step, GLA_WIDTH), lambda b: (row_block0 // seqs + b, 0))
    out_blk = pl.BlockSpec((rows_per_step, GLA_WIDTH), lambda b: (b, 0))
    st_spec = pl.BlockSpec((seqs, 1, GLA_HEADS, HEAD_DIM, HEAD_DIM), lambda b: (b, 0, 0, 0, 0))
    const = lambda shape: pl.BlockSpec(shape, lambda b: (0, 0))
    in_specs = ([blk] * 6 + [const((1, GLA_WIDTH))] + [const((GROUP, GROUP))] * 2
                + [const((4 * CHUNK, LANES))] * 2)
    args = [gq, gk, gv, gf, gb, sgg, norm_row, *_gla_constants()]
    if has_state:
        in_specs += [st_spec] * 2
        args += [s0f, s0b]
    out_specs = [out_blk]
    out_shape = [jax.ShapeDtypeStruct((n_seq * seq_len, GLA_WIDTH), BF16)]
    if emit_state:
        out_specs += [st_spec] * 2
        out_shape += [jax.ShapeDtypeStruct((n_seq, 1, GLA_HEADS, HEAD_DIM, HEAD_DIM), F32)] * 2
    with_attention = None
    if attention is not None:
        sink, sq, sk, sv = attention[:4]
        with_attention = "context"
        in_specs += [pl.BlockSpec(memory_space=pltpu.SMEM),
                     pl.BlockSpec((rows_per_step, SWA_WIDTH), lambda b: (row_block0 // seqs + b, 0)),
                     pl.BlockSpec((rows_per_step, KV_WIDTH), lambda b: (row_block0 // seqs + b, 0)),
                     pl.BlockSpec((rows_per_step, KV_WIDTH), lambda b: (row_block0 // seqs + b, 0))]
        args += [sink, sq, sk, sv]
        if len(attention) > 4:
            with_attention = "latent"
            cache_k_t, cache_v_t = attention[4:]
            past = cache_k_t.shape[4]
            cache_spec = pl.BlockSpec((seqs, 1, SWA_KV_HEADS, HEAD_DIM, past), lambda b: (b, 0, 0, 0, 0))
            in_specs += [cache_spec] * 2
            args += [cache_k_t, cache_v_t]
        out_specs.append(pl.BlockSpec((rows_per_step, SWA_WIDTH), lambda b: (b, 0)))
        out_shape.append(jax.ShapeDtypeStruct((n_seq * seq_len, SWA_WIDTH), BF16))
    cast_specs, cast_shapes = _cast_specs(cast_weights, steps)
    in_specs += cast_specs
    args += list(cast_weights)
    out_specs += cast_specs
    out_shape += cast_shapes
    return pl.pallas_call(
        functools.partial(_gla_kernel, seq_len=seq_len, seqs=seqs, has_state=has_state, emit_state=emit_state,
                          with_attention=with_attention, n_cast=len(cast_weights)),
        grid=(steps,), in_specs=in_specs, out_specs=out_specs, out_shape=out_shape,
        scratch_shapes=[pltpu.VMEM((rows_per_step, GLA_WIDTH), F32)] * 2,
        compiler_params=_cparams(1), name="gla_bidirectional",
    )(*args)


def _group_attention(q, heads, sink_ref, score_fn, value_fn):
    rows = q.shape[0]
    q_stack = jnp.concatenate([q[:, h * HEAD_DIM:(h + 1) * HEAD_DIM] for h in heads], axis=0)
    scores = score_fn(q_stack)
    yield
    slabs = [slice(j * rows, (j + 1) * rows) for j in range(len(heads))]
    ms = []
    for slab, h in zip(slabs, heads):
        m = sink_ref[h]
        for s in scores:
            m = jnp.maximum(m, jnp.max(s[slab, :], axis=-1, keepdims=True))
        ms.append(m)
    yield
    e_parts = [[] for _ in scores]
    denoms = []
    for slab, h, m in zip(slabs, heads, ms):
        denom = jnp.exp(sink_ref[h] - m)
        for part, s in zip(e_parts, scores):
            e = jnp.exp(s[slab, :] - m)
            denom = denom + jnp.sum(e, axis=-1, keepdims=True)
            part.append(e.astype(BF16))
        denoms.append(denom)
    yield
    acc = value_fn([jnp.concatenate(part, axis=0) for part in e_parts])
    yield
    return [acc[slab, :] / denom for slab, denom in zip(slabs, denoms)]


def _run_staged(units):
    results = [None] * len(units)
    live = list(range(len(units)))
    while live:
        still = []
        for u in live:
            try:
                next(units[u])
                still.append(u)
            except StopIteration as stop:
                results[u] = stop.value
        live = still
    return results


def _context_attention_units(sink_ref, q_ref, k_ref, v_ref, rows):
    q = q_ref[rows, :]
    k = k_ref[rows, :]
    v = v_ref[rows, :]
    units = []
    for g in range(SWA_KV_HEADS):
        k_g = k[:, g * HEAD_DIM:(g + 1) * HEAD_DIM]
        v_g = v[:, g * HEAD_DIM:(g + 1) * HEAD_DIM]
        units.append(_group_attention(
            q, list(range(SWA_GROUP * g, SWA_GROUP * (g + 1))), sink_ref,
            lambda qs, k_g=k_g: [_dot_nt(qs, k_g)],
            lambda es, v_g=v_g: _dot(es[0], v_g)))
    return units


def _cast_specs(cast_weights, steps, step_of=lambda *idx: idx[0]):
    specs = [pl.BlockSpec((w.shape[0] // steps, w.shape[1]), lambda *idx: (step_of(*idx), 0))
             for w in cast_weights]
    shapes = [jax.ShapeDtypeStruct(w.shape, BF16) for w in cast_weights]
    return specs, shapes


def _latent_attention_units(sink_ref, q_ref, k_ref, v_ref, ctx_kv, i, seq_len):
    span = QBLOCK + 2 * WINDOW
    start = pl.multiple_of(jnp.clip(i * QBLOCK - WINDOW, 0, seq_len - span), QBLOCK)
    q = q_ref[pl.ds(pl.multiple_of(i * QBLOCK, QBLOCK), QBLOCK), :]
    k = k_ref[pl.ds(start, span), :]
    v = v_ref[pl.ds(start, span), :]
    qpos = i * QBLOCK + lax.broadcasted_iota(jnp.int32, (QBLOCK, span), 0)
    kpos = start + lax.broadcasted_iota(jnp.int32, (QBLOCK, span), 1)
    valid = jnp.abs(qpos - kpos) <= WINDOW
    valid = jnp.concatenate([valid] * SWA_GROUP, axis=0)
    units = []
    for g in range(SWA_KV_HEADS):
        k_g = k[:, g * HEAD_DIM:(g + 1) * HEAD_DIM]
        v_g = v[:, g * HEAD_DIM:(g + 1) * HEAD_DIM]
        kc_t, vc_t = ctx_kv[g]
        units.append(_group_attention(
            q, list(range(SWA_GROUP * g, SWA_GROUP * (g + 1))), sink_ref,
            lambda qs, k_g=k_g, kc_t=kc_t: [jnp.where(valid, _dot_nt(qs, k_g), NEG_INF), _dot(qs, kc_t)],
            lambda es, v_g=v_g, vc_t=vc_t: _dot(es[0], v_g) + _dot_nt(es[1], vc_t)))
    return units


def _out_kernel(xp_ref, xs_ref, ogp_ref, ogs_ref, oswp_ref, osws_ref, mod_ref, gain_ref,
                wo_ref, w1_ref, w2_ref, yp_ref, ys_ref, *, n_prompt_tiles, tiles_per_seq):
    i = pl.program_id(0)
    row = _mod_row(i, n_prompt_tiles, tiles_per_seq)

    def body(x_ref, og_ref, osw_ref, y_ref):
        mix = _dot(og_ref[...], wo_ref[:GLA_WIDTH, :]) + _dot(osw_ref[...], wo_ref[GLA_WIDTH:, :])
        x1 = x_ref[...] + _mod_chunk(mod_ref, row, 2) * mix
        h = _adaln(x1, gain_ref[...], _mod_chunk(mod_ref, row, 3), _mod_chunk(mod_ref, row, 4)).astype(BF16)
        acc = None
        for j in range(D_FF // FF_TILE):
            cols = slice(j * FF_TILE, (j + 1) * FF_TILE)
            a = jnp.maximum(_dot(h, w1_ref[:, cols]), 0.0)
            part = _dot((a * a).astype(BF16), w2_ref[cols, :])
            acc = part if acc is None else acc + part
        y_ref[...] = x1 + _mod_chunk(mod_ref, row, 5) * acc

    @pl.when(i < n_prompt_tiles)
    def _():
        body(xp_ref, ogp_ref, oswp_ref, yp_ref)

    @pl.when(i >= n_prompt_tiles)
    def _():
        body(xs_ref, ogs_ref, osws_ref, ys_ref)


def _mix_out(xp, xs, og_p, og_s, osw_p, osw_s, mod, gain, wo, w1, w2, seq_s):
    n_p, n_s = xp.shape[0], xs.shape[0]
    tp, ts = n_p // TM, n_s // TM
    tiles_per_seq = seq_s // TM
    const = lambda shape: pl.BlockSpec(shape, lambda i: (0,) * len(shape), pipeline_mode=pl.Buffered(1))
    row_p = lambda w: pl.BlockSpec((TM, w), lambda i: (jnp.minimum(i, tp - 1), 0))
    row_s = lambda w: pl.BlockSpec((TM, w), lambda i: (jnp.maximum(i - tp, 0), 0))
    return pl.pallas_call(
        functools.partial(_out_kernel, n_prompt_tiles=tp, tiles_per_seq=tiles_per_seq),
        grid=(tp + ts,),
        in_specs=[row_p(D_MODEL), row_s(D_MODEL), row_p(GLA_WIDTH), row_s(GLA_WIDTH),
                  row_p(SWA_WIDTH), row_s(SWA_WIDTH),
                  const(mod.shape),
                  const((1, D_MODEL)), const((GLA_WIDTH + SWA_WIDTH, D_MODEL)),
                  const((D_MODEL, D_FF)), const((D_FF, D_MODEL))],
        out_specs=[row_p(D_MODEL), row_s(D_MODEL)],
        out_shape=[jax.ShapeDtypeStruct((n_p, D_MODEL), F32), jax.ShapeDtypeStruct((n_s, D_MODEL), F32)],
        compiler_params=_cparams(1), name="out_proj_mlp",
    )(xp, xs, og_p, og_s, osw_p, osw_s, mod, gain, wo, w1, w2)


def _rope_tables(seq_len):
    axis_dim = HEAD_DIM // 2
    half = axis_dim // 2
    t = np.arange(seq_len)
    rowp = (t // GRID_W).astype(np.float64)
    colp = (t % GRID_W).astype(np.float64)
    freqs = ROPE_THETA ** (-np.arange(half, dtype=np.float64) / half)
    lane = np.arange(LANES)
    d = lane % HEAD_DIM
    posv = np.where((d // axis_dim)[None, :] == 0, rowp[:, None], colp[:, None])
    ang = posv * freqs[d % half][None, :]
    second = ((d % axis_dim) >= half)[None, :]
    cos, sin = np.cos(ang), np.sin(ang)
    zero = np.zeros_like(sin)
    tabs = (cos, np.where(second, sin, zero), np.where(second, zero, -sin))
    return tuple(jnp.asarray(tab.astype(np.float32)) for tab in tabs)


def kernel(x_prompt, x_sample, c, cache_k, cache_v, state_gla_fwd, state_gla_bwd, c_ctx, w_ada, b_ada, norm_mix, norm_ff, w_in, w_gk_fwd, b_gk_fwd, w_gk_bwd, b_gk_bwd, gla_norm, q_norm, k_norm, sink, w_o, w_ff1, w_ff2):
    depth = w_in.shape[0]
    assert depth == 1, "single trunk layer"
    bp, tp, _ = x_prompt.shape
    bs, ts, _ = x_sample.shape
    n_p = bp * tp
    l = 0

    cond = jnp.concatenate([c_ctx[None, :], c, jnp.zeros((8 - 1 - bs, D_MODEL), F32)], axis=0)
    mod = _modulation(cond, w_ada[l], b_ada[l])

    zero = jnp.zeros((GATE_RANK, GLA_WIDTH), F32)
    wg = jnp.concatenate([jnp.concatenate([w_gk_fwd[l], zero], axis=1),
                          jnp.concatenate([zero, w_gk_bwd[l]], axis=1),
                          jnp.zeros((LANES - 2 * GATE_RANK, 2 * GLA_WIDTH), F32)], axis=0).astype(BF16)
    bg = jnp.concatenate([b_gk_fwd[l], b_gk_bwd[l]])[None, :]
    qkg = jnp.concatenate([jnp.tile(q_norm[l], SWA_HEADS), jnp.tile(k_norm[l], SWA_KV_HEADS)])[None, :]
    norm2 = jnp.tile(gla_norm[l], GLA_HEADS)[None, :]
    gain_mix = norm_mix[l][None, :]
    gain_ff = norm_ff[l][None, :]
    sink_l = sink[l]

    w_in_t = jnp.transpose(w_in[l])
    cache_k_t = jnp.swapaxes(cache_k, 3, 4)
    cache_v_t = jnp.swapaxes(cache_v, 3, 4)

    xp = x_prompt.reshape(n_p, D_MODEL)
    xs = x_sample.reshape(bs * ts, D_MODEL)
    (gq, gk, gv, sgg, gf, gb, sq, sk, sv, new_k_t, new_v_t) = _project(
        xp, xs, mod, gain_mix, w_in_t, wg, bg, qkg, _rope_tables(ts), tp, ts)
    new_k = jnp.swapaxes(new_k_t, 3, 4)
    new_v = jnp.swapaxes(new_v_t, 3, 4)

    og_p, s_f, s_b, osw_p, wo, w1, w2 = _gla(
        gq, gk, gv, gf, gb, sgg, norm2, None, None, bp, tp, 0, True, seqs=2,
        attention=(sink_l, sq, sk, sv), cast_weights=(w_o[l], w_ff1[l], w_ff2[l]))
    og_s, osw_s = _gla(gq, gk, gv, gf, gb, sgg, norm2, state_gla_fwd, state_gla_bwd, bs, ts, n_p // ts, False,
                       attention=(sink_l, sq, sk, sv, cache_k_t, cache_v_t))

    yp, ys = _mix_out(xp, xs, og_p, og_s, osw_p, osw_s, mod, gain_ff, wo, w1, w2, ts)
    return (yp.reshape(bp, tp, D_MODEL), ys.reshape(bs, ts, D_MODEL), new_k, new_v, s_f, s_b)
```

```python
import functools

import numpy as np
import jax
import jax.numpy as jnp
from jax import lax
from jax.experimental import pallas as pl
from jax.experimental.pallas import tpu as pltpu

F32 = jnp.float32
BF16 = jnp.bfloat16

D_MODEL = 1024
GRID_W = 64
HEAD_DIM = 64
GLA_HEADS = 8
GLA_WIDTH = GLA_HEADS * HEAD_DIM
GATE_RANK = 16
GATE_TEMP = 16.0
CHUNK = 64
GROUP = 4 * CHUNK
SWA_HEADS = 8
SWA_KV_HEADS = 2
SWA_GROUP = SWA_HEADS // SWA_KV_HEADS
SWA_WIDTH = SWA_HEADS * HEAD_DIM
KV_WIDTH = SWA_KV_HEADS * HEAD_DIM
WINDOW = 128
QBLOCK = 128
D_FF = 4 * D_MODEL
ROPE_THETA = 10000.0
EPS = 1e-6
NEG_INF = -1e30
LANES = 128
TM = 512
FF_TILE = 1024
UNITS_PER_WAVE = 10

D_IN = 2848
R_GQ, R_GK, R_GV, R_GG, R_LR = 0, 512, 1024, 1536, 2048
R_SQ = R_LR + 2 * GATE_RANK
R_SV = R_SQ + SWA_WIDTH + KV_WIDTH
C_TAIL = R_LR
C_LR = C_TAIL + D_IN - R_SQ
W_COLS = C_LR + LANES

VMEM_LIMIT = 56 * 1024 * 1024


def _cparams(n_axes, flags=None):
    return pltpu.CompilerParams(dimension_semantics=("arbitrary",) * n_axes,
                                vmem_limit_bytes=VMEM_LIMIT, flags=flags)


def _mod_kernel(cctx_ref, c_ref, w_ref, b_ref, o_ref):
    k = pl.program_id(0)
    pad = jnp.zeros((8 - 1 - c_ref.shape[0], c_ref.shape[1]), F32)
    cnd = jnp.concatenate([cctx_ref[...], c_ref[...], pad], axis=0)
    s = (cnd * jax.nn.sigmoid(cnd)).astype(BF16)
    part = _dot(s, w_ref[...].astype(BF16))

    @pl.when(k == 0)
    def _():
        o_ref[...] = part + b_ref[...]

    @pl.when(k > 0)
    def _():
        o_ref[...] += part


def _modulation(c_ctx, c, w_ada, b_ada):
    tk = 256
    n_out = w_ada.shape[1]
    return pl.pallas_call(
        _mod_kernel,
        grid=(D_MODEL // tk,),
        in_specs=[pl.BlockSpec((1, tk), lambda k: (0, k)),
                  pl.BlockSpec((c.shape[0], tk), lambda k: (0, k)),
                  pl.BlockSpec((tk, n_out), lambda k: (k, 0)),
                  pl.BlockSpec((1, n_out), lambda k: (0, 0))],
        out_specs=pl.BlockSpec((8, n_out), lambda k: (0, 0)),
        out_shape=jax.ShapeDtypeStruct((8, n_out), F32),
        compiler_params=_cparams(1),
        name="modulation",
    )(c_ctx, c, w_ada, b_ada.reshape(1, n_out))


def _mod_row(tile, n_prompt_tiles, tiles_per_seq):
    return jnp.where(tile < n_prompt_tiles, 0, 1 + (tile - n_prompt_tiles) // tiles_per_seq)


def _mod_chunk(mod_ref, row, j):
    return mod_ref[pl.ds(row, 1), j * D_MODEL:(j + 1) * D_MODEL]


def _adaln(x, gain, shift, scale):
    ms = jnp.mean(x * x, axis=-1, keepdims=True)
    xn = x * lax.rsqrt(ms + EPS)
    return xn * (gain * (1.0 + scale)) + shift


def _head_rms_norm(y, gain):
    cols = y.shape[1]
    lane = lax.broadcasted_iota(jnp.int32, (1, LANES), 1)
    first = lane < HEAD_DIM
    outs = []
    for p in range(cols // LANES):
        s = y[:, p * LANES:(p + 1) * LANES]
        s = s * s
        tot = jnp.sum(s, axis=-1, keepdims=True)
        lo = jnp.sum(jnp.where(first, s, 0.0), axis=-1, keepdims=True)
        outs.append(jnp.where(first, lo, tot - lo))
    sums = outs[0] if len(outs) == 1 else jnp.concatenate(outs, axis=-1)
    return y * lax.rsqrt(sums + HEAD_DIM * EPS) * (gain * (HEAD_DIM ** 0.5))


def _log_sigmoid(x):
    return jnp.minimum(x, 0.0) - jnp.log(1.0 + jnp.exp(-jnp.abs(x)))


def _dot(a, b):
    return jnp.dot(a, b, preferred_element_type=F32)


def _dot_nt(a, b):
    return lax.dot_general(a, b, (((1,), (1,)), ((), ())), preferred_element_type=F32)


def _proj_kernel(xp_ref, xs_ref, mod_ref, gain_ref, w_ref, wgf_ref, wgb_ref, bgf_ref, bgb_ref, qn_ref, kn_ref,
                 cos_ref, sin_a_ref, sin_b_ref,
                 gla_ref, gate_ref, swa_ref, nk_ref, nv_ref,
                 wb_ref, wg_ref, h0_ref, h1_ref,
                 *, n_prompt_tiles, n_tiles, tiles_per_seq, seqs_per_tile, seq_len):
    h_refs = (h0_ref, h1_ref)
    s = pl.program_id(0)

    @pl.when(s == 0)
    def _():
        moves = ([(c * LANES, c * LANES) for c in range(R_LR // LANES)]
                 + [(R_SQ + c * LANES, C_TAIL + c * LANES) for c in range((D_IN - R_SQ) // LANES)]
                 + [(R_LR, C_LR)])
        for src, dst in moves:
            wb_ref[:, dst:dst + LANES] = w_ref[src:src + LANES, :].T.astype(BF16)
        wg_ref[...] = jnp.zeros(wg_ref.shape, BF16)
        wg_ref[0:GATE_RANK, 0:GLA_WIDTH] = wgf_ref[...].astype(BF16)
        wg_ref[GATE_RANK:2 * GATE_RANK, GLA_WIDTH:] = wgb_ref[...].astype(BF16)

    def normalise_next(parity):
        x = jnp.where(s < n_prompt_tiles, xp_ref[...], xs_ref[...])
        row = _mod_row(jnp.minimum(s, n_tiles - 1), n_prompt_tiles, tiles_per_seq)
        shift, scale = _mod_chunk(mod_ref, row, 0), _mod_chunk(mod_ref, row, 1)
        h_refs[parity][...] = _adaln(x, gain_ref[...], shift, scale).astype(BF16)

    def body(rope, emit_cache, parity):
        h_ref = h_refs[1 - parity]

        def proj(c0, c1):
            return _dot(h_ref[...], wb_ref[:, c0:c1])

        tail = proj(C_TAIL, C_LR)
        lr = proj(C_LR, W_COLS).astype(BF16)
        gate_pre = _dot(lr, wg_ref[...])
        gg = proj(R_GG, R_LR)
        gla_ref[:, 3 * GLA_WIDTH:] = (gg * jax.nn.sigmoid(gg)).astype(BF16)
        attention_inputs(tail, rope, emit_cache)
        gla_ref[:, :GLA_WIDTH] = (proj(R_GQ, R_GK) * (HEAD_DIM ** -0.5)).astype(BF16)
        gate_ref[:, :GLA_WIDTH] = _log_sigmoid(gate_pre[:, :GLA_WIDTH] + bgf_ref[...]) * (1.0 / GATE_TEMP)
        gla_ref[:, GLA_WIDTH:2 * GLA_WIDTH] = proj(R_GK, R_GV).astype(BF16)
        gate_ref[:, GLA_WIDTH:] = _log_sigmoid(gate_pre[:, GLA_WIDTH:] + bgb_ref[...]) * (1.0 / GATE_TEMP)
        gla_ref[:, 2 * GLA_WIDTH:3 * GLA_WIDTH] = proj(R_GV, R_GG).astype(BF16)
        normalise_next(parity)

    def attention_inputs(tail, rope, emit_cache):
        qk = tail[:, :R_SV - R_SQ]
        qk_gain = jnp.concatenate([qn_ref[...]] * SWA_HEADS + [kn_ref[...]] * SWA_KV_HEADS, axis=-1)
        qk = _head_rms_norm(qk, qk_gain)
        if rope:
            reps = (SWA_WIDTH + KV_WIDTH) // LANES
            cos = jnp.concatenate([cos_ref[...]] * reps, axis=-1)
            sin_a = jnp.concatenate([sin_a_ref[...]] * reps, axis=-1)
            sin_b = jnp.concatenate([sin_b_ref[...]] * reps, axis=-1)
            width = qk.shape[1]
            quarter = HEAD_DIM // 4
            qk = (qk * cos + pltpu.roll(qk, quarter, 1) * sin_a
                  + pltpu.roll(qk, width - quarter, 1) * sin_b)
        swa_ref[:, :SWA_WIDTH] = (qk[:, :SWA_WIDTH] * (HEAD_DIM ** -0.5)).astype(BF16)
        sk = qk[:, SWA_WIDTH:]
        swa_ref[:, SWA_WIDTH:SWA_WIDTH + KV_WIDTH] = sk.astype(BF16)
        sv = tail[:, R_SV - R_SQ:]
        swa_ref[:, SWA_WIDTH + KV_WIDTH:] = sv.astype(BF16)
        if emit_cache:
            sk_t, sv_t = sk.T, sv.T
            for q in range(seqs_per_tile):
                cols = slice(q * seq_len, (q + 1) * seq_len)
                for g in range(SWA_KV_HEADS):
                    feat = slice(g * HEAD_DIM, (g + 1) * HEAD_DIM)
                    nk_ref[q, 0, g, :, :] = sk_t[feat, cols]
                    nv_ref[q, 0, g, :, :] = sv_t[feat, cols]

    @pl.when(s == 0)
    def _():
        normalise_next(0)

    for parity in range(2):
        on_parity = (s % 2) == parity

        @pl.when((s >= 1) & (s <= n_prompt_tiles) & on_parity)
        def _():
            body(rope=False, emit_cache=True, parity=parity)

        @pl.when((s > n_prompt_tiles) & on_parity)
        def _():
            body(rope=True, emit_cache=False, parity=parity)


def _project(xp, xs, mod, gain, w_in, w_gk_f, w_gk_b, b_gk_f, b_gk_b, q_norm, k_norm, rope_tabs, seq_p, seq_s):
    n_p, n_s = xp.shape[0], xs.shape[0]
    tp, ts = n_p // TM, n_s // TM
    n = n_p + n_s
    seqs_per_tile = TM // seq_p
    tiles_per_seq = seq_s // TM
    n_seq_p = n_p // seq_p
    const = lambda shape: pl.BlockSpec(shape, lambda s: (0,) * len(shape))
    n_tiles = tp + ts
    out_tile = lambda s: jnp.maximum(s - 1, 0)
    rope_spec = pl.BlockSpec((TM, LANES), lambda s: (jnp.maximum(out_tile(s) - tp, 0) % tiles_per_seq, 0))
    in_specs = [pl.BlockSpec((TM, D_MODEL), lambda s: (jnp.minimum(s, tp - 1), 0)),
                pl.BlockSpec((TM, D_MODEL), lambda s: (jnp.clip(s - tp, 0, ts - 1), 0)),
                const(mod.shape),
                const((1, D_MODEL)),
                pl.BlockSpec((D_IN, D_MODEL), lambda s: (0, 0), pipeline_mode=pl.Buffered(1)),
                const(w_gk_f.shape), const(w_gk_b.shape), const(b_gk_f.shape), const(b_gk_b.shape),
                const(q_norm.shape), const(k_norm.shape),
                rope_spec, rope_spec, rope_spec]
    row = lambda w: pl.BlockSpec((TM, w), lambda s: (out_tile(s), 0))
    cache_spec = pl.BlockSpec((seqs_per_tile, 1, SWA_KV_HEADS, HEAD_DIM, seq_p),
                              lambda s: (jnp.minimum(out_tile(s), tp - 1), 0, 0, 0, 0))
    widths = (4 * GLA_WIDTH, 2 * GLA_WIDTH, SWA_WIDTH + 2 * KV_WIDTH)
    out_specs = [row(w) for w in widths] + [cache_spec] * 2
    cache_shape = (n_seq_p, 1, SWA_KV_HEADS, HEAD_DIM, seq_p)
    out_shape = ([jax.ShapeDtypeStruct((n, w), dt) for w, dt in zip(widths, (BF16, F32, BF16))]
                 + [jax.ShapeDtypeStruct(cache_shape, F32)] * 2)
    return pl.pallas_call(
        functools.partial(_proj_kernel, n_prompt_tiles=tp, n_tiles=n_tiles, tiles_per_seq=tiles_per_seq,
                          seqs_per_tile=seqs_per_tile, seq_len=seq_p),
        grid=(n_tiles + 1,), in_specs=in_specs, out_specs=out_specs, out_shape=out_shape,
        scratch_shapes=([pltpu.VMEM((D_MODEL, W_COLS), BF16), pltpu.VMEM((LANES, 2 * GLA_WIDTH), BF16)]
                        + [pltpu.VMEM((TM, D_MODEL), BF16)] * 2),
        compiler_params=_cparams(1), name="adaln_in_proj",
    )(xp, xs, mod, gain, w_in, w_gk_f, w_gk_b, b_gk_f, b_gk_b, q_norm, k_norm, *rope_tabs)


def _gla_kernel(*refs, seq_len, seqs, has_state, emit_state, with_attention, n_cast):
    (q_ref, k_ref, v_ref, gf_ref, gb_ref, sgg_ref, norm_ref,
     trif_ref, trib_ref, keepf_ref, keepb_ref) = refs[:11]
    pos = 11
    if has_state:
        s0f_ref, s0b_ref = refs[pos:pos + 2]
        pos += 2
    if with_attention:
        sink_ref, aq_ref, ak_ref, av_ref = refs[pos:pos + 4]
        pos += 4
    if with_attention == "latent":
        kc_ref, vc_ref = refs[pos:pos + 2]
        pos += 2
    cast_in = refs[pos:pos + n_cast]
    pos += n_cast
    og_ref = refs[pos]
    pos += 1
    if emit_state:
        sf_ref, sb_ref = refs[pos:pos + 2]
        pos += 2
    if with_attention:
        osw_ref = refs[pos]
        pos += 1
    cast_out = refs[pos:pos + n_cast]
    pos += n_cast
    accf_ref, accb_ref = refs[pos:pos + 2]

    for src, dst in zip(cast_in, cast_out):
        dst[...] = src[...].astype(BF16)

    n_groups = seq_len // GROUP
    per_group = GROUP // CHUNK
    n_pairs = q_ref.shape[1] // LANES
    lane = lax.broadcasted_iota(jnp.int32, (1, LANES), 1)
    head0 = lane < HEAD_DIM
    m0 = jnp.where(head0, 1.0, 0.0).astype(BF16)
    m1 = jnp.where(head0, 0.0, 1.0).astype(BF16)
    row_top = lax.broadcasted_iota(jnp.int32, (LANES, 1), 0) < HEAD_DIM

    def stack_heads(x):
        return jnp.concatenate([x * m0, x * m1], axis=0)

    def direction(row0, lanes, g_ref, tri_ref, keep_ref, state, acc_ref, backward):
        rows = pl.ds(row0, GROUP)
        g = g_ref[rows, lanes]
        g_hi = g.astype(BF16)
        g_lo = (g - g_hi.astype(F32)).astype(BF16)
        r = _dot(tri_ref[...], jnp.concatenate([g_hi, g_lo], axis=-1))
        yield
        b = (r[:, :LANES] + r[:, LANES:]).reshape(per_group, CHUNK, LANES)
        edge = 0 if backward else CHUNK - 1
        mid = CHUNK // 2
        b_edge = b[:, edge:edge + 1, :]
        b_mid = b[:, mid:mid + 1, :]
        q = q_ref[rows, lanes].astype(F32).reshape(per_group, CHUNK, LANES)
        k = k_ref[rows, lanes].astype(F32).reshape(per_group, CHUNK, LANES)
        v = v_ref[rows, lanes]
        q_mid = q * jnp.exp(b - b_mid)
        k_mid = k * jnp.exp(b_mid - b)
        q_in = q_mid.astype(BF16)
        k_in = k_mid.astype(BF16).reshape(GROUP, LANES)
        q_st = (q_mid * jnp.exp(b_mid)).astype(BF16)
        k_st = (k_mid * jnp.exp(b_edge - b_mid)).astype(BF16).reshape(GROUP, LANES)
        decay = jnp.exp(b_edge)
        keep = keep_ref[...]
        pair_rows = 2 * CHUNK
        n_cp = per_group // 2
        yield
        scores, kvs = [], []
        for p in range(n_cp):
            c0 = 2 * p
            prow = slice(p * pair_rows, (p + 1) * pair_rows)
            q_pair = jnp.concatenate([stack_heads(q_in[c0]), stack_heads(q_in[c0 + 1])], axis=0)
            scores.append(_dot_nt(q_pair, k_in[prow, :]).astype(BF16) * keep)
            v_t = v[prow, :].astype(F32).T.astype(BF16)
            k_pair = k_st[prow, :]
            zero = jnp.zeros_like(k_pair)
            k_blk = jnp.concatenate([jnp.where(row_top, k_pair, zero), jnp.where(row_top, zero, k_pair)], axis=-1)
            kvs.append(_dot(v_t, k_blk))
        yield
        intras = [_dot(scores[p], v[p * pair_rows:(p + 1) * pair_rows, :]) for p in range(n_cp)]
        yield
        for p in (range(n_cp - 1, -1, -1) if backward else range(n_cp)):
            for j in ((1, 0) if backward else (0, 1)):
                c = 2 * p + j
                inter = _dot_nt(stack_heads(q_st[c]), state.astype(BF16))
                tot = intras[p][j * pair_rows:(j + 1) * pair_rows, :] + inter
                acc_ref[pl.ds(row0 + c * CHUNK, CHUNK), lanes] = jnp.where(head0, tot[:CHUNK, :], tot[CHUNK:, :])
                state = state * decay[c] + kvs[p][:, j * LANES:(j + 1) * LANES]
            yield
        return state

    def load_state(s_ref, q, p):
        z = jnp.zeros((HEAD_DIM, HEAD_DIM), F32)
        top = jnp.concatenate([s_ref[q, 0, 2 * p, :, :].T, z], axis=-1)
        bot = jnp.concatenate([z, s_ref[q, 0, 2 * p + 1, :, :].T], axis=-1)
        return jnp.concatenate([top, bot], axis=0)

    per_dir = seqs * n_pairs
    if has_state:
        init = tuple(load_state(s_ref, q, p)
                     for s_ref in (s0f_ref, s0b_ref) for q in range(seqs) for p in range(n_pairs))
    else:
        init = (jnp.zeros((LANES, LANES), F32),) * (2 * per_dir)

    def body(t, carry):
        row_f = t * GROUP
        row_b = (n_groups - 1 - t) * GROUP
        if n_groups > 1:
            row_f = pl.multiple_of(row_f, GROUP)
            row_b = pl.multiple_of(row_b, GROUP)
        units = []
        for backward, row, g_ref, tri_ref, keep_ref, acc_ref in (
                (False, row_f, gf_ref, trif_ref, keepf_ref, accf_ref),
                (True, row_b, gb_ref, trib_ref, keepb_ref, accb_ref)):
            for q in range(seqs):
                for p in range(n_pairs):
                    lanes = slice(p * LANES, (p + 1) * LANES)
                    state = carry[backward * per_dir + q * n_pairs + p]
                    units.append(direction(q * seq_len + row, lanes, g_ref, tri_ref, keep_ref, state, acc_ref,
                                           backward))
        return units

    if n_groups == 1:
        units = body(0, init)
        if with_attention:
            for q in range(seqs):
                units += _context_attention_units(sink_ref, aq_ref, ak_ref, av_ref,
                                                  slice(q * seq_len, (q + 1) * seq_len))
        n_att = SWA_KV_HEADS if with_attention else 0
        order = [d * per_dir + q * n_pairs + p for q in range(seqs) for d in range(2) for p in range(n_pairs)]
        order = [u for q in range(seqs)
                 for u in order[q * 2 * n_pairs:(q + 1) * 2 * n_pairs]
                 + [2 * per_dir + q * n_att + a for a in range(n_att)]]
        staged = _run_staged([units[u] for u in order], wave=UNITS_PER_WAVE)
        results = [None] * len(units)
        for u, r in zip(order, staged):
            results[u] = r
        final = results[:2 * per_dir]
        if with_attention:
            for q in range(seqs):
                heads = [o for outs in results[2 * per_dir + q * SWA_KV_HEADS:
                                               2 * per_dir + (q + 1) * SWA_KV_HEADS] for o in outs]
                osw_ref[q * seq_len:(q + 1) * seq_len, :] = jnp.concatenate(heads, axis=-1).astype(BF16)
    else:
        assert with_attention != "context" and (seqs == 1 or not with_attention)
        if with_attention:
            ctx_kv = [(kc_ref[0, 0, g, :, :].astype(BF16), vc_ref[0, 0, g, :, :].astype(BF16))
                      for g in range(SWA_KV_HEADS)]
            q_blocks = GROUP // QBLOCK

        def loop_body(t, carry):
            units = body(t, carry)
            n_gla = len(units)
            if with_attention:
                for j in range(q_blocks):
                    units += _latent_attention_units(sink_ref, aq_ref, ak_ref, av_ref, ctx_kv,
                                                     t * q_blocks + j, seq_len)
            results = _run_staged(units)
            if with_attention:
                for j in range(q_blocks):
                    heads = [o for outs in results[n_gla + j * SWA_KV_HEADS:n_gla + (j + 1) * SWA_KV_HEADS]
                             for o in outs]
                    rows = pl.ds(pl.multiple_of((t * q_blocks + j) * QBLOCK, QBLOCK), QBLOCK)
                    osw_ref[rows, :] = jnp.concatenate(heads, axis=-1).astype(BF16)
            return tuple(results[:n_gla])

        final = lax.fori_loop(0, n_groups, loop_body, init)
    if emit_state:
        for d, s_ref in enumerate((sf_ref, sb_ref)):
            for q in range(seqs):
                for p in range(n_pairs):
                    fin = final[d * per_dir + q * n_pairs + p]
                    s_ref[q, 0, 2 * p, :, :] = fin[:HEAD_DIM, :HEAD_DIM].T
                    s_ref[q, 0, 2 * p + 1, :, :] = fin[HEAD_DIM:, HEAD_DIM:].T

    o = accf_ref[...] + accb_ref[...]
    o = _head_rms_norm(o, jnp.concatenate([norm_ref[...]] * (2 * n_pairs), axis=-1))
    og_ref[...] = (o * sgg_ref[...].astype(F32)).astype(BF16)


def _gla_constants():
    i = np.arange(GROUP)[:, None]
    j = np.arange(GROUP)[None, :]
    same = (i // CHUNK) == (j // CHUNK)
    tri_f = same & (j <= i)
    tri_b = same & (j >= i)
    r = np.arange(4 * CHUNK)[:, None]
    l = np.arange(LANES)[None, :]
    same = (r // (2 * CHUNK)) == (l // CHUNK)
    keep_f = same & ((l % CHUNK) <= (r % CHUNK))
    keep_b = same & ((l % CHUNK) >= (r % CHUNK))
    return tuple(jnp.asarray(m.astype(np.float32), dtype=BF16) for m in (tri_f, tri_b, keep_f, keep_b))


def _gla(gla_in, gates, norm_row, s0f, s0b, n_seq, seq_len, row_block0, emit_state, seqs=1,
         attention=None, cast_weights=()):
    has_state = s0f is not None
    steps = n_seq // seqs
    rows_per_step = seqs * seq_len
    row_blk = row_block0 // seqs
    col = lambda width, j: pl.BlockSpec((rows_per_step, width), lambda b: (row_blk + b, j))
    out_blk = pl.BlockSpec((rows_per_step, GLA_WIDTH), lambda b: (b, 0))
    st_spec = pl.BlockSpec((seqs, 1, GLA_HEADS, HEAD_DIM, HEAD_DIM), lambda b: (b, 0, 0, 0, 0))
    const = lambda shape: pl.BlockSpec(shape, lambda b: (0, 0))
    in_specs = ([col(GLA_WIDTH, 0), col(GLA_WIDTH, 1), col(GLA_WIDTH, 2), col(GLA_WIDTH, 0), col(GLA_WIDTH, 1),
                 col(GLA_WIDTH, 3)]
                + [const(norm_row.shape)] + [const((GROUP, GROUP))] * 2 + [const((4 * CHUNK, LANES))] * 2)
    args = [gla_in, gla_in, gla_in, gates, gates, gla_in, norm_row, *_gla_constants()]
    if has_state:
        in_specs += [st_spec] * 2
        args += [s0f, s0b]
    out_specs = [out_blk]
    out_shape = [jax.ShapeDtypeStruct((n_seq * seq_len, GLA_WIDTH), BF16)]
    if emit_state:
        out_specs += [st_spec] * 2
        out_shape += [jax.ShapeDtypeStruct((n_seq, 1, GLA_HEADS, HEAD_DIM, HEAD_DIM), F32)] * 2
    with_attention = None
    if attention is not None:
        sink, swa_in = attention[:2]
        with_attention = "context"
        in_specs += [pl.BlockSpec(memory_space=pltpu.SMEM), col(SWA_WIDTH, 0),
                     col(KV_WIDTH, SWA_WIDTH // KV_WIDTH), col(KV_WIDTH, SWA_WIDTH // KV_WIDTH + 1)]
        args += [sink, swa_in, swa_in, swa_in]
        if len(attention) > 2:
            with_attention = "latent"
            cache_k_t, cache_v_t = attention[2:]
            past = cache_k_t.shape[4]
            cache_spec = pl.BlockSpec((seqs, 1, SWA_KV_HEADS, HEAD_DIM, past), lambda b: (b, 0, 0, 0, 0))
            in_specs += [cache_spec] * 2
            args += [cache_k_t, cache_v_t]
        out_specs.append(pl.BlockSpec((rows_per_step, SWA_WIDTH), lambda b: (b, 0)))
        out_shape.append(jax.ShapeDtypeStruct((n_seq * seq_len, SWA_WIDTH), BF16))
    cast_specs, cast_shapes = _cast_specs(cast_weights, steps)
    in_specs += cast_specs
    args += list(cast_weights)
    out_specs += cast_specs
    out_shape += cast_shapes
    return pl.pallas_call(
        functools.partial(_gla_kernel, seq_len=seq_len, seqs=seqs, has_state=has_state, emit_state=emit_state,
                          with_attention=with_attention, n_cast=len(cast_weights)),
        grid=(steps,), in_specs=in_specs, out_specs=out_specs, out_shape=out_shape,
        scratch_shapes=[pltpu.VMEM((rows_per_step, GLA_WIDTH), F32)] * 2,
        compiler_params=_cparams(1), name="gla_bidirectional",
    )(*args)


def _group_attention(q, heads, sink_ref, score_fn, value_fn):
    rows = q.shape[0]
    q_stack = jnp.concatenate([q[:, h * HEAD_DIM:(h + 1) * HEAD_DIM] for h in heads], axis=0)
    scores = score_fn(q_stack)
    yield
    slabs = [slice(j * rows, (j + 1) * rows) for j in range(len(heads))]
    ms = []
    for slab, h in zip(slabs, heads):
        m = sink_ref[h]
        for s in scores:
            m = jnp.maximum(m, jnp.max(s[slab, :], axis=-1, keepdims=True))
        ms.append(m)
    yield
    e_parts = [[] for _ in scores]
    denoms = []
    for slab, h, m in zip(slabs, heads, ms):
        denom = jnp.exp(sink_ref[h] - m)
        for part, s in zip(e_parts, scores):
            e = jnp.exp(s[slab, :] - m)
            denom = denom + jnp.sum(e, axis=-1, keepdims=True)
            part.append(e.astype(BF16))
        denoms.append(denom)
    yield
    acc = value_fn([jnp.concatenate(part, axis=0) for part in e_parts])
    yield
    return [acc[slab, :] / denom for slab, denom in zip(slabs, denoms)]


def _run_staged(units, wave=None):
    results = [None] * len(units)
    wave = wave or len(units)
    for w0 in range(0, len(units), wave):
        live = list(range(w0, min(w0 + wave, len(units))))
        while live:
            still = []
            for u in live:
                try:
                    next(units[u])
                    still.append(u)
                except StopIteration as stop:
                    results[u] = stop.value
            live = still
    return results


def _context_attention_units(sink_ref, q_ref, k_ref, v_ref, rows):
    q = q_ref[rows, :]
    k = k_ref[rows, :]
    v = v_ref[rows, :]
    units = []
    for g in range(SWA_KV_HEADS):
        k_g = k[:, g * HEAD_DIM:(g + 1) * HEAD_DIM]
        v_g = v[:, g * HEAD_DIM:(g + 1) * HEAD_DIM]
        units.append(_group_attention(
            q, list(range(SWA_GROUP * g, SWA_GROUP * (g + 1))), sink_ref,
            lambda qs, k_g=k_g: [_dot_nt(qs, k_g)],
            lambda es, v_g=v_g: _dot(es[0], v_g)))
    return units


def _cast_specs(cast_weights, steps, step_of=lambda *idx: idx[0]):
    specs = [pl.BlockSpec((w.shape[0] // steps, w.shape[1]), lambda *idx: (step_of(*idx), 0))
             for w in cast_weights]
    shapes = [jax.ShapeDtypeStruct(w.shape, BF16) for w in cast_weights]
    return specs, shapes


def _latent_attention_units(sink_ref, q_ref, k_ref, v_ref, ctx_kv, i, seq_len):
    span = QBLOCK + 2 * WINDOW
    start = pl.multiple_of(jnp.clip(i * QBLOCK - WINDOW, 0, seq_len - span), QBLOCK)
    q = q_ref[pl.ds(pl.multiple_of(i * QBLOCK, QBLOCK), QBLOCK), :]
    k = k_ref[pl.ds(start, span), :]
    v = v_ref[pl.ds(start, span), :]
    qpos = i * QBLOCK + lax.broadcasted_iota(jnp.int32, (QBLOCK, span), 0)
    kpos = start + lax.broadcasted_iota(jnp.int32, (QBLOCK, span), 1)
    valid = jnp.abs(qpos - kpos) <= WINDOW
    valid = jnp.concatenate([valid] * SWA_GROUP, axis=0)
    units = []
    for g in range(SWA_KV_HEADS):
        k_g = k[:, g * HEAD_DIM:(g + 1) * HEAD_DIM]
        v_g = v[:, g * HEAD_DIM:(g + 1) * HEAD_DIM]
        kc_t, vc_t = ctx_kv[g]
        units.append(_group_attention(
            q, list(range(SWA_GROUP * g, SWA_GROUP * (g + 1))), sink_ref,
            lambda qs, k_g=k_g, kc_t=kc_t: [jnp.where(valid, _dot_nt(qs, k_g), NEG_INF), _dot(qs, kc_t)],
            lambda es, v_g=v_g, vc_t=vc_t: _dot(es[0], v_g) + _dot_nt(es[1], vc_t)))
    return units


def _out_kernel(xp_ref, xs_ref, ogp_ref, ogs_ref, oswp_ref, osws_ref, mod_ref, gain_ref,
                wo_ref, w1_ref, w2_ref, yp_ref, ys_ref, *, n_prompt_tiles, tiles_per_seq):
    i = pl.program_id(0)
    row = _mod_row(i, n_prompt_tiles, tiles_per_seq)

    def body(x_ref, og_ref, osw_ref, y_ref):
        mix = _dot(og_ref[...], wo_ref[:GLA_WIDTH, :]) + _dot(osw_ref[...], wo_ref[GLA_WIDTH:, :])
        x1 = x_ref[...] + _mod_chunk(mod_ref, row, 2) * mix
        h = _adaln(x1, gain_ref[...], _mod_chunk(mod_ref, row, 3), _mod_chunk(mod_ref, row, 4)).astype(BF16)
        acc = None
        for j in range(D_FF // FF_TILE):
            cols = slice(j * FF_TILE, (j + 1) * FF_TILE)
            a = jnp.maximum(_dot(h, w1_ref[:, cols]), 0.0)
            part = _dot((a * a).astype(BF16), w2_ref[cols, :])
            acc = part if acc is None else acc + part
        y_ref[...] = x1 + _mod_chunk(mod_ref, row, 5) * acc

    @pl.when(i < n_prompt_tiles)
    def _():
        body(xp_ref, ogp_ref, oswp_ref, yp_ref)

    @pl.when(i >= n_prompt_tiles)
    def _():
        body(xs_ref, ogs_ref, osws_ref, ys_ref)


def _mix_out(xp, xs, og_p, og_s, osw_p, osw_s, mod, gain, wo, w1, w2, seq_s):
    n_p, n_s = xp.shape[0], xs.shape[0]
    tp, ts = n_p // TM, n_s // TM
    tiles_per_seq = seq_s // TM
    const = lambda shape: pl.BlockSpec(shape, lambda i: (0,) * len(shape), pipeline_mode=pl.Buffered(1))
    row_p = lambda w: pl.BlockSpec((TM, w), lambda i: (jnp.minimum(i, tp - 1), 0))
    row_s = lambda w: pl.BlockSpec((TM, w), lambda i: (jnp.maximum(i - tp, 0), 0))
    return pl.pallas_call(
        functools.partial(_out_kernel, n_prompt_tiles=tp, tiles_per_seq=tiles_per_seq),
        grid=(tp + ts,),
        in_specs=[row_p(D_MODEL), row_s(D_MODEL), row_p(GLA_WIDTH), row_s(GLA_WIDTH),
                  row_p(SWA_WIDTH), row_s(SWA_WIDTH),
                  const(mod.shape),
                  const((1, D_MODEL)), const((GLA_WIDTH + SWA_WIDTH, D_MODEL)),
                  const((D_MODEL, D_FF)), const((D_FF, D_MODEL))],
        out_specs=[row_p(D_MODEL), row_s(D_MODEL)],
        out_shape=[jax.ShapeDtypeStruct((n_p, D_MODEL), F32), jax.ShapeDtypeStruct((n_s, D_MODEL), F32)],
        compiler_params=_cparams(1), name="out_proj_mlp",
    )(xp, xs, og_p, og_s, osw_p, osw_s, mod, gain, wo, w1, w2)


def _rope_tables(seq_len):
    axis_dim = HEAD_DIM // 2
    half = axis_dim // 2
    t = np.arange(seq_len)
    rowp = (t // GRID_W).astype(np.float64)
    colp = (t % GRID_W).astype(np.float64)
    freqs = ROPE_THETA ** (-np.arange(half, dtype=np.float64) / half)
    lane = np.arange(LANES)
    d = lane % HEAD_DIM
    posv = np.where((d // axis_dim)[None, :] == 0, rowp[:, None], colp[:, None])
    ang = posv * freqs[d % half][None, :]
    second = ((d % axis_dim) >= half)[None, :]
    cos, sin = np.cos(ang), np.sin(ang)
    zero = np.zeros_like(sin)
    tabs = (cos, np.where(second, sin, zero), np.where(second, zero, -sin))
    return tuple(jnp.asarray(tab.astype(np.float32)) for tab in tabs)


def kernel(x_prompt, x_sample, c, cache_k, cache_v, state_gla_fwd, state_gla_bwd, c_ctx, w_ada, b_ada, norm_mix, norm_ff, w_in, w_gk_fwd, b_gk_fwd, w_gk_bwd, b_gk_bwd, gla_norm, q_norm, k_norm, sink, w_o, w_ff1, w_ff2):
    depth = w_in.shape[0]
    assert depth == 1, "single trunk layer"
    bp, tp, _ = x_prompt.shape
    bs, ts, _ = x_sample.shape
    n_p = bp * tp
    l = 0

    mod = _modulation(c_ctx[None, :], c, w_ada[l], b_ada[l])

    norm2 = gla_norm[l][None, :]
    gain_mix = norm_mix[l][None, :]
    gain_ff = norm_ff[l][None, :]
    sink_l = sink[l]

    w_in_t = jnp.transpose(w_in[l])
    cache_k_t = jnp.swapaxes(cache_k, 3, 4)
    cache_v_t = jnp.swapaxes(cache_v, 3, 4)

    xp = x_prompt.reshape(n_p, D_MODEL)
    xs = x_sample.reshape(bs * ts, D_MODEL)
    gla_in, gates, swa_in, new_k_t, new_v_t = _project(
        xp, xs, mod, gain_mix, w_in_t, w_gk_fwd[l], w_gk_bwd[l], b_gk_fwd[l][None, :], b_gk_bwd[l][None, :],
        q_norm[l][None, :], k_norm[l][None, :], _rope_tables(ts), tp, ts)
    new_k = jnp.swapaxes(new_k_t, 3, 4)
    new_v = jnp.swapaxes(new_v_t, 3, 4)

    og_p, s_f, s_b, osw_p, wo, w1, w2 = _gla(
        gla_in, gates, norm2, None, None, bp, tp, 0, True, seqs=2,
        attention=(sink_l, swa_in), cast_weights=(w_o[l], w_ff1[l], w_ff2[l]))
    og_s, osw_s = _gla(gla_in, gates, norm2, state_gla_fwd, state_gla_bwd, bs, ts, n_p // ts, False,
                       attention=(sink_l, swa_in, cache_k_t, cache_v_t))

    yp, ys = _mix_out(xp, xs, og_p, og_s, osw_p, osw_s, mod, gain_ff, wo, w1, w2, ts)
    return (yp.reshape(bp, tp, D_MODEL), ys.reshape(bs, ts, D_MODEL), new_k, new_v, s_f, s_b)
```

```python
import functools

import numpy as np
import jax
import jax.numpy as jnp
from jax import lax
from jax.experimental import pallas as pl
from jax.experimental.pallas import tpu as pltpu

F32 = jnp.float32
BF16 = jnp.bfloat16

D_MODEL = 1024
GRID_W = 64
HEAD_DIM = 64
GLA_HEADS = 8
GLA_WIDTH = GLA_HEADS * HEAD_DIM
GATE_RANK = 16
GATE_TEMP = 16.0
CHUNK = 64
GROUP = 4 * CHUNK
SWA_HEADS = 8
SWA_KV_HEADS = 2
SWA_GROUP = SWA_HEADS // SWA_KV_HEADS
SWA_WIDTH = SWA_HEADS * HEAD_DIM
KV_WIDTH = SWA_KV_HEADS * HEAD_DIM
WINDOW = 128
QBLOCK = 128
D_FF = 4 * D_MODEL
ROPE_THETA = 10000.0
EPS = 1e-6
NEG_INF = -1e30
LANES = 128
TM = 512
FF_TILE = 1024
UNITS_PER_WAVE = 10

D_IN = 2848
R_GQ, R_GK, R_GV, R_GG, R_LR = 0, 512, 1024, 1536, 2048
R_SQ = R_LR + 2 * GATE_RANK
R_SV = R_SQ + SWA_WIDTH + KV_WIDTH
C_TAIL = R_LR
C_LR = C_TAIL + D_IN - R_SQ
W_COLS = C_LR + LANES

VMEM_LIMIT = 56 * 1024 * 1024


def _cparams(n_axes, flags=None):
    return pltpu.CompilerParams(dimension_semantics=("arbitrary",) * n_axes,
                                vmem_limit_bytes=VMEM_LIMIT, flags=flags)


def _mod_kernel(cctx_ref, c_ref, w_ref, b_ref, o_ref):
    k = pl.program_id(0)
    pad = jnp.zeros((8 - 1 - c_ref.shape[0], c_ref.shape[1]), F32)
    cnd = jnp.concatenate([cctx_ref[...], c_ref[...], pad], axis=0)
    s = (cnd * jax.nn.sigmoid(cnd)).astype(BF16)
    part = _dot(s, w_ref[...].astype(BF16))

    @pl.when(k == 0)
    def _():
        o_ref[...] = part + b_ref[...]

    @pl.when(k > 0)
    def _():
        o_ref[...] += part


def _modulation(c_ctx, c, w_ada, b_ada):
    tk = 256
    n_out = w_ada.shape[1]
    return pl.pallas_call(
        _mod_kernel,
        grid=(D_MODEL // tk,),
        in_specs=[pl.BlockSpec((1, tk), lambda k: (0, k)),
                  pl.BlockSpec((c.shape[0], tk), lambda k: (0, k)),
                  pl.BlockSpec((tk, n_out), lambda k: (k, 0)),
                  pl.BlockSpec((1, n_out), lambda k: (0, 0))],
        out_specs=pl.BlockSpec((8, n_out), lambda k: (0, 0)),
        out_shape=jax.ShapeDtypeStruct((8, n_out), F32),
        compiler_params=_cparams(1),
        name="modulation",
    )(c_ctx, c, w_ada, b_ada.reshape(1, n_out))


def _mod_row(tile, n_prompt_tiles, tiles_per_seq):
    return jnp.where(tile < n_prompt_tiles, 0, 1 + (tile - n_prompt_tiles) // tiles_per_seq)


def _mod_chunk(mod_ref, row, j):
    return mod_ref[pl.ds(row, 1), j * D_MODEL:(j + 1) * D_MODEL]


def _adaln(x, gain, shift, scale):
    ms = jnp.mean(x * x, axis=-1, keepdims=True)
    xn = x * lax.rsqrt(ms + EPS)
    return xn * (gain * (1.0 + scale)) + shift


def _head_rms_norm(y, gain):
    cols = y.shape[1]
    lane = lax.broadcasted_iota(jnp.int32, (1, LANES), 1)
    first = lane < HEAD_DIM
    outs = []
    for p in range(cols // LANES):
        s = y[:, p * LANES:(p + 1) * LANES]
        s = s * s
        tot = jnp.sum(s, axis=-1, keepdims=True)
        lo = jnp.sum(jnp.where(first, s, 0.0), axis=-1, keepdims=True)
        outs.append(jnp.where(first, lo, tot - lo))
    sums = outs[0] if len(outs) == 1 else jnp.concatenate(outs, axis=-1)
    return y * lax.rsqrt(sums + HEAD_DIM * EPS) * (gain * (HEAD_DIM ** 0.5))


def _log_sigmoid(x):
    return jnp.minimum(x, 0.0) - jnp.log(1.0 + jnp.exp(-jnp.abs(x)))


def _dot(a, b):
    return jnp.dot(a, b, preferred_element_type=F32)


def _dot_nt(a, b):
    return lax.dot_general(a, b, (((1,), (1,)), ((), ())), preferred_element_type=F32)


def _proj_kernel(xp_ref, xs_ref, mod_ref, gain_ref, w_ref, wgf_ref, wgb_ref, bgf_ref, bgb_ref, qn_ref, kn_ref,
                 cos_ref, sin_a_ref, sin_b_ref,
                 gla_ref, gate_ref, swa_ref, nk_ref, nv_ref,
                 wb_ref, wg_ref, h0_ref, h1_ref, lr0_ref, lr1_ref,
                 *, n_prompt_tiles, n_tiles, tiles_per_seq, seqs_per_tile, seq_len):
    h_refs = (h0_ref, h1_ref)
    lr_refs = (lr0_ref, lr1_ref)
    s = pl.program_id(0)

    @pl.when(s == 0)
    def _():
        moves = ([(c * LANES, c * LANES) for c in range(R_LR // LANES)]
                 + [(R_SQ + c * LANES, C_TAIL + c * LANES) for c in range((D_IN - R_SQ) // LANES)]
                 + [(R_LR, C_LR)])
        for src, dst in moves:
            wb_ref[:, dst:dst + LANES] = w_ref[src:src + LANES, :].T.astype(BF16)
        wg_ref[...] = jnp.zeros(wg_ref.shape, BF16)
        wg_ref[0:GATE_RANK, 0:GLA_WIDTH] = wgf_ref[...].astype(BF16)
        wg_ref[GATE_RANK:2 * GATE_RANK, GLA_WIDTH:] = wgb_ref[...].astype(BF16)

    def normalise_next(parity):
        x = jnp.where(s < n_prompt_tiles, xp_ref[...], xs_ref[...])
        row = _mod_row(jnp.minimum(s, n_tiles - 1), n_prompt_tiles, tiles_per_seq)
        shift, scale = _mod_chunk(mod_ref, row, 0), _mod_chunk(mod_ref, row, 1)
        h_refs[parity][...] = _adaln(x, gain_ref[...], shift, scale).astype(BF16)

    def gates(parity):
        gate_pre = _dot(lr_refs[parity][...], wg_ref[...])
        gate_ref[:, :GLA_WIDTH] = _log_sigmoid(gate_pre[:, :GLA_WIDTH] + bgf_ref[...]) * (1.0 / GATE_TEMP)
        gate_ref[:, GLA_WIDTH:] = _log_sigmoid(gate_pre[:, GLA_WIDTH:] + bgb_ref[...]) * (1.0 / GATE_TEMP)

    def body(rope, emit_cache, parity, with_gates):
        h_ref = h_refs[1 - parity]
        if with_gates:
            gates(parity)

        def proj(c0, c1):
            return _dot(h_ref[...], wb_ref[:, c0:c1])

        tail = proj(C_TAIL, C_LR)
        lr_refs[1 - parity][...] = proj(C_LR, W_COLS).astype(BF16)
        gg = proj(R_GG, R_LR)
        gla_ref[:, 3 * GLA_WIDTH:] = (gg * jax.nn.sigmoid(gg)).astype(BF16)
        attention_inputs(tail, rope, emit_cache)
        gla_ref[:, :GLA_WIDTH] = (proj(R_GQ, R_GK) * (HEAD_DIM ** -0.5)).astype(BF16)
        gla_ref[:, GLA_WIDTH:2 * GLA_WIDTH] = proj(R_GK, R_GV).astype(BF16)
        gla_ref[:, 2 * GLA_WIDTH:3 * GLA_WIDTH] = proj(R_GV, R_GG).astype(BF16)
        normalise_next(parity)

    def attention_inputs(tail, rope, emit_cache):
        qk = tail[:, :R_SV - R_SQ]
        qk_gain = jnp.concatenate([qn_ref[...]] * SWA_HEADS + [kn_ref[...]] * SWA_KV_HEADS, axis=-1)
        qk = _head_rms_norm(qk, qk_gain)
        if rope:
            reps = (SWA_WIDTH + KV_WIDTH) // LANES
            cos = jnp.concatenate([cos_ref[...]] * reps, axis=-1)
            sin_a = jnp.concatenate([sin_a_ref[...]] * reps, axis=-1)
            sin_b = jnp.concatenate([sin_b_ref[...]] * reps, axis=-1)
            width = qk.shape[1]
            quarter = HEAD_DIM // 4
            qk = (qk * cos + pltpu.roll(qk, quarter, 1) * sin_a
                  + pltpu.roll(qk, width - quarter, 1) * sin_b)
        swa_ref[:, :SWA_WIDTH] = (qk[:, :SWA_WIDTH] * (HEAD_DIM ** -0.5)).astype(BF16)
        sk = qk[:, SWA_WIDTH:]
        swa_ref[:, SWA_WIDTH:SWA_WIDTH + KV_WIDTH] = sk.astype(BF16)
        sv = tail[:, R_SV - R_SQ:]
        swa_ref[:, SWA_WIDTH + KV_WIDTH:] = sv.astype(BF16)
        if emit_cache:
            sk_t, sv_t = sk.T, sv.T
            for q in range(seqs_per_tile):
                cols = slice(q * seq_len, (q + 1) * seq_len)
                for g in range(SWA_KV_HEADS):
                    feat = slice(g * HEAD_DIM, (g + 1) * HEAD_DIM)
                    nk_ref[q, 0, g, :, :] = sk_t[feat, cols]
                    nv_ref[q, 0, g, :, :] = sv_t[feat, cols]

    @pl.when(s == 0)
    def _():
        normalise_next(0)

    @pl.when(s == 1)
    def _():
        body(rope=False, emit_cache=True, parity=1, with_gates=False)

    for parity in range(2):
        on_parity = (s % 2) == parity

        @pl.when((s >= 2) & (s <= n_prompt_tiles) & on_parity)
        def _():
            body(rope=False, emit_cache=True, parity=parity, with_gates=True)

        @pl.when((s > n_prompt_tiles) & (s <= n_tiles) & on_parity)
        def _():
            body(rope=True, emit_cache=False, parity=parity, with_gates=True)

    @pl.when(s == n_tiles + 1)
    def _():
        gates((n_tiles + 1) % 2)


def _project(xp, xs, mod, gain, w_in, w_gk_f, w_gk_b, b_gk_f, b_gk_b, q_norm, k_norm, rope_tabs, seq_p, seq_s):
    n_p, n_s = xp.shape[0], xs.shape[0]
    tp, ts = n_p // TM, n_s // TM
    n = n_p + n_s
    seqs_per_tile = TM // seq_p
    tiles_per_seq = seq_s // TM
    n_seq_p = n_p // seq_p
    const = lambda shape: pl.BlockSpec(shape, lambda s: (0,) * len(shape))
    n_tiles = tp + ts
    out_tile = lambda s: jnp.clip(s - 1, 0, n_tiles - 1)
    gate_tile = lambda s: jnp.maximum(s - 2, 0)
    rope_spec = pl.BlockSpec((TM, LANES), lambda s: (jnp.maximum(out_tile(s) - tp, 0) % tiles_per_seq, 0))
    in_specs = [pl.BlockSpec((TM, D_MODEL), lambda s: (jnp.minimum(s, tp - 1), 0)),
                pl.BlockSpec((TM, D_MODEL), lambda s: (jnp.clip(s - tp, 0, ts - 1), 0)),
                const(mod.shape),
                const((1, D_MODEL)),
                pl.BlockSpec((D_IN, D_MODEL), lambda s: (0, 0), pipeline_mode=pl.Buffered(1)),
                const(w_gk_f.shape), const(w_gk_b.shape), const(b_gk_f.shape), const(b_gk_b.shape),
                const(q_norm.shape), const(k_norm.shape),
                rope_spec, rope_spec, rope_spec]
    row = lambda w: pl.BlockSpec((TM, w), lambda s: (out_tile(s), 0))
    cache_spec = pl.BlockSpec((seqs_per_tile, 1, SWA_KV_HEADS, HEAD_DIM, seq_p),
                              lambda s: (jnp.minimum(out_tile(s), tp - 1), 0, 0, 0, 0))
    widths = (4 * GLA_WIDTH, 2 * GLA_WIDTH, SWA_WIDTH + 2 * KV_WIDTH)
    out_specs = [row(widths[0]), pl.BlockSpec((TM, widths[1]), lambda s: (gate_tile(s), 0)), row(widths[2]),
                 cache_spec, cache_spec]
    cache_shape = (n_seq_p, 1, SWA_KV_HEADS, HEAD_DIM, seq_p)
    out_shape = ([jax.ShapeDtypeStruct((n, w), dt) for w, dt in zip(widths, (BF16, F32, BF16))]
                 + [jax.ShapeDtypeStruct(cache_shape, F32)] * 2)
    return pl.pallas_call(
        functools.partial(_proj_kernel, n_prompt_tiles=tp, n_tiles=n_tiles, tiles_per_seq=tiles_per_seq,
                          seqs_per_tile=seqs_per_tile, seq_len=seq_p),
        grid=(n_tiles + 2,), in_specs=in_specs, out_specs=out_specs, out_shape=out_shape,
        scratch_shapes=([pltpu.VMEM((D_MODEL, W_COLS), BF16), pltpu.VMEM((LANES, 2 * GLA_WIDTH), BF16)]
                        + [pltpu.VMEM((TM, D_MODEL), BF16)] * 2 + [pltpu.VMEM((TM, LANES), BF16)] * 2),
        compiler_params=_cparams(1), name="adaln_in_proj",
    )(xp, xs, mod, gain, w_in, w_gk_f, w_gk_b, b_gk_f, b_gk_b, q_norm, k_norm, *rope_tabs)


def _gla_kernel(*refs, seq_len, seqs, has_state, emit_state, with_attention, n_cast):
    (q_ref, k_ref, v_ref, gf_ref, gb_ref, sgg_ref, norm_ref,
     trif_ref, trib_ref, keepf_ref, keepb_ref) = refs[:11]
    pos = 11
    if has_state:
        s0f_ref, s0b_ref = refs[pos:pos + 2]
        pos += 2
    if with_attention:
        sink_ref, aq_ref, ak_ref, av_ref = refs[pos:pos + 4]
        pos += 4
    if with_attention == "latent":
        kc_ref, vc_ref = refs[pos:pos + 2]
        pos += 2
    cast_in = refs[pos:pos + n_cast]
    pos += n_cast
    og_ref = refs[pos]
    pos += 1
    if emit_state:
        sf_ref, sb_ref = refs[pos:pos + 2]
        pos += 2
    if with_attention:
        osw_ref = refs[pos]
        pos += 1
    cast_out = refs[pos:pos + n_cast]
    pos += n_cast
    accf_ref, accb_ref = refs[pos:pos + 2]

    for src, dst in zip(cast_in, cast_out):
        dst[...] = src[...].astype(BF16)

    n_groups = seq_len // GROUP
    per_group = GROUP // CHUNK
    n_pairs = q_ref.shape[1] // LANES
    lane = lax.broadcasted_iota(jnp.int32, (1, LANES), 1)
    head0 = lane < HEAD_DIM
    m0 = jnp.where(head0, 1.0, 0.0).astype(BF16)
    m1 = jnp.where(head0, 0.0, 1.0).astype(BF16)
    row_top = lax.broadcasted_iota(jnp.int32, (LANES, 1), 0) < HEAD_DIM

    def stack_heads(x):
        return jnp.concatenate([x * m0, x * m1], axis=0)

    def direction(row0, lanes, g_ref, tri_ref, keep_ref, state, acc_ref, backward):
        rows = pl.ds(row0, GROUP)
        g = g_ref[rows, lanes]
        g_hi = g.astype(BF16)
        g_lo = (g - g_hi.astype(F32)).astype(BF16)
        r = _dot(tri_ref[...], jnp.concatenate([g_hi, g_lo], axis=-1))
        yield
        b = (r[:, :LANES] + r[:, LANES:]).reshape(per_group, CHUNK, LANES)
        edge = 0 if backward else CHUNK - 1
        mid = CHUNK // 2
        b_edge = b[:, edge:edge + 1, :]
        b_mid = b[:, mid:mid + 1, :]
        q = q_ref[rows, lanes].astype(F32).reshape(per_group, CHUNK, LANES)
        k = k_ref[rows, lanes].astype(F32).reshape(per_group, CHUNK, LANES)
        v = v_ref[rows, lanes]
        q_mid = q * jnp.exp(b - b_mid)
        k_mid = k * jnp.exp(b_mid - b)
        q_in = q_mid.astype(BF16)
        k_in = k_mid.astype(BF16).reshape(GROUP, LANES)
        q_st = (q_mid * jnp.exp(b_mid)).astype(BF16)
        k_st = (k_mid * jnp.exp(b_edge - b_mid)).astype(BF16).reshape(GROUP, LANES)
        decay = jnp.exp(b_edge)
        keep = keep_ref[...]
        pair_rows = 2 * CHUNK
        n_cp = per_group // 2
        yield
        scores, kvs = [], []
        for p in range(n_cp):
            c0 = 2 * p
            prow = slice(p * pair_rows, (p + 1) * pair_rows)
            q_pair = jnp.concatenate([stack_heads(q_in[c0]), stack_heads(q_in[c0 + 1])], axis=0)
            scores.append(_dot_nt(q_pair, k_in[prow, :]).astype(BF16) * keep)
            v_t = v[prow, :].astype(F32).T.astype(BF16)
            k_pair = k_st[prow, :]
            zero = jnp.zeros_like(k_pair)
            k_blk = jnp.concatenate([jnp.where(row_top, k_pair, zero), jnp.where(row_top, zero, k_pair)], axis=-1)
            kvs.append(_dot(v_t, k_blk))
        yield
        intras = [_dot(scores[p], v[p * pair_rows:(p + 1) * pair_rows, :]) for p in range(n_cp)]
        yield
        for p in (range(n_cp - 1, -1, -1) if backward else range(n_cp)):
            for j in ((1, 0) if backward else (0, 1)):
                c = 2 * p + j
                inter = _dot_nt(stack_heads(q_st[c]), state.astype(BF16))
                tot = intras[p][j * pair_rows:(j + 1) * pair_rows, :] + inter
                acc_ref[pl.ds(row0 + c * CHUNK, CHUNK), lanes] = jnp.where(head0, tot[:CHUNK, :], tot[CHUNK:, :])
                state = state * decay[c] + kvs[p][:, j * LANES:(j + 1) * LANES]
            yield
        return state

    def load_state(s_ref, q, p):
        z = jnp.zeros((HEAD_DIM, HEAD_DIM), F32)
        top = jnp.concatenate([s_ref[q, 0, 2 * p, :, :].T, z], axis=-1)
        bot = jnp.concatenate([z, s_ref[q, 0, 2 * p + 1, :, :].T], axis=-1)
        return jnp.concatenate([top, bot], axis=0)

    per_dir = seqs * n_pairs
    if has_state:
        init = tuple(load_state(s_ref, q, p)
                     for s_ref in (s0f_ref, s0b_ref) for q in range(seqs) for p in range(n_pairs))
    else:
        init = (jnp.zeros((LANES, LANES), F32),) * (2 * per_dir)

    def body(t, carry):
        row_f = t * GROUP
        row_b = (n_groups - 1 - t) * GROUP
        if n_groups > 1:
            row_f = pl.multiple_of(row_f, GROUP)
            row_b = pl.multiple_of(row_b, GROUP)
        units = []
        for backward, row, g_ref, tri_ref, keep_ref, acc_ref in (
                (False, row_f, gf_ref, trif_ref, keepf_ref, accf_ref),
                (True, row_b, gb_ref, trib_ref, keepb_ref, accb_ref)):
            for q in range(seqs):
                for p in range(n_pairs):
                    lanes = slice(p * LANES, (p + 1) * LANES)
                    state = carry[backward * per_dir + q * n_pairs + p]
                    units.append(direction(q * seq_len + row, lanes, g_ref, tri_ref, keep_ref, state, acc_ref,
                                           backward))
        return units

    if n_groups == 1:
        units = body(0, init)
        if with_attention:
            for q in range(seqs):
                units += _context_attention_units(sink_ref, aq_ref, ak_ref, av_ref,
                                                  slice(q * seq_len, (q + 1) * seq_len))
        n_att = SWA_KV_HEADS if with_attention else 0
        order = [d * per_dir + q * n_pairs + p for q in range(seqs) for d in range(2) for p in range(n_pairs)]
        order = [u for q in range(seqs)
                 for u in order[q * 2 * n_pairs:(q + 1) * 2 * n_pairs]
                 + [2 * per_dir + q * n_att + a for a in range(n_att)]]
        staged = _run_staged([units[u] for u in order], wave=UNITS_PER_WAVE)
        results = [None] * len(units)
        for u, r in zip(order, staged):
            results[u] = r
        final = results[:2 * per_dir]
        if with_attention:
            for q in range(seqs):
                heads = [o for outs in results[2 * per_dir + q * SWA_KV_HEADS:
                                               2 * per_dir + (q + 1) * SWA_KV_HEADS] for o in outs]
                osw_ref[q * seq_len:(q + 1) * seq_len, :] = jnp.concatenate(heads, axis=-1).astype(BF16)
    else:
        assert with_attention != "context" and (seqs == 1 or not with_attention)
        if with_attention:
            ctx_kv = [(kc_ref[0, 0, g, :, :].astype(BF16), vc_ref[0, 0, g, :, :].astype(BF16))
                      for g in range(SWA_KV_HEADS)]
            q_blocks = GROUP // QBLOCK

        def loop_body(t, carry):
            units = body(t, carry)
            n_gla = len(units)
            if with_attention:
                for j in range(q_blocks):
                    units += _latent_attention_units(sink_ref, aq_ref, ak_ref, av_ref, ctx_kv,
                                                     t * q_blocks + j, seq_len)
            results = _run_staged(units)
            if with_attention:
                for j in range(q_blocks):
                    heads = [o for outs in results[n_gla + j * SWA_KV_HEADS:n_gla + (j + 1) * SWA_KV_HEADS]
                             for o in outs]
                    rows = pl.ds(pl.multiple_of((t * q_blocks + j) * QBLOCK, QBLOCK), QBLOCK)
                    osw_ref[rows, :] = jnp.concatenate(heads, axis=-1).astype(BF16)
            return tuple(results[:n_gla])

        final = lax.fori_loop(0, n_groups, loop_body, init)
    if emit_state:
        for d, s_ref in enumerate((sf_ref, sb_ref)):
            for q in range(seqs):
                for p in range(n_pairs):
                    fin = final[d * per_dir + q * n_pairs + p]
                    s_ref[q, 0, 2 * p, :, :] = fin[:HEAD_DIM, :HEAD_DIM].T
                    s_ref[q, 0, 2 * p + 1, :, :] = fin[HEAD_DIM:, HEAD_DIM:].T

    o = accf_ref[...] + accb_ref[...]
    o = _head_rms_norm(o, jnp.concatenate([norm_ref[...]] * (2 * n_pairs), axis=-1))
    og_ref[...] = (o * sgg_ref[...].astype(F32)).astype(BF16)


def _gla_constants():
    i = np.arange(GROUP)[:, None]
    j = np.arange(GROUP)[None, :]
    same = (i // CHUNK) == (j // CHUNK)
    tri_f = same & (j <= i)
    tri_b = same & (j >= i)
    r = np.arange(4 * CHUNK)[:, None]
    l = np.arange(LANES)[None, :]
    same = (r // (2 * CHUNK)) == (l // CHUNK)
    keep_f = same & ((l % CHUNK) <= (r % CHUNK))
    keep_b = same & ((l % CHUNK) >= (r % CHUNK))
    return tuple(jnp.asarray(m.astype(np.float32), dtype=BF16) for m in (tri_f, tri_b, keep_f, keep_b))


def _gla(gla_in, gates, norm_row, s0f, s0b, n_seq, seq_len, row_block0, emit_state, seqs=1,
         attention=None, cast_weights=()):
    has_state = s0f is not None
    steps = n_seq // seqs
    rows_per_step = seqs * seq_len
    row_blk = row_block0 // seqs
    col = lambda width, j: pl.BlockSpec((rows_per_step, width), lambda b: (row_blk + b, j))
    out_blk = pl.BlockSpec((rows_per_step, GLA_WIDTH), lambda b: (b, 0))
    st_spec = pl.BlockSpec((seqs, 1, GLA_HEADS, HEAD_DIM, HEAD_DIM), lambda b: (b, 0, 0, 0, 0))
    const = lambda shape: pl.BlockSpec(shape, lambda b: (0, 0))
    in_specs = ([col(GLA_WIDTH, 0), col(GLA_WIDTH, 1), col(GLA_WIDTH, 2), col(GLA_WIDTH, 0), col(GLA_WIDTH, 1),
                 col(GLA_WIDTH, 3)]
                + [const(norm_row.shape)] + [const((GROUP, GROUP))] * 2 + [const((4 * CHUNK, LANES))] * 2)
    args = [gla_in, gla_in, gla_in, gates, gates, gla_in, norm_row, *_gla_constants()]
    if has_state:
        in_specs += [st_spec] * 2
        args += [s0f, s0b]
    out_specs = [out_blk]
    out_shape = [jax.ShapeDtypeStruct((n_seq * seq_len, GLA_WIDTH), BF16)]
    if emit_state:
        out_specs += [st_spec] * 2
        out_shape += [jax.ShapeDtypeStruct((n_seq, 1, GLA_HEADS, HEAD_DIM, HEAD_DIM), F32)] * 2
    with_attention = None
    if attention is not None:
        sink, swa_in = attention[:2]
        with_attention = "context"
        in_specs += [pl.BlockSpec(memory_space=pltpu.SMEM), col(SWA_WIDTH, 0),
                     col(KV_WIDTH, SWA_WIDTH // KV_WIDTH), col(KV_WIDTH, SWA_WIDTH // KV_WIDTH + 1)]
        args += [sink, swa_in, swa_in, swa_in]
        if len(attention) > 2:
            with_attention = "latent"
            cache_k_t, cache_v_t = attention[2:]
            past = cache_k_t.shape[4]
            cache_spec = pl.BlockSpec((seqs, 1, SWA_KV_HEADS, HEAD_DIM, past), lambda b: (b, 0, 0, 0, 0))
            in_specs += [cache_spec] * 2
            args += [cache_k_t, cache_v_t]
        out_specs.append(pl.BlockSpec((rows_per_step, SWA_WIDTH), lambda b: (b, 0)))
        out_shape.append(jax.ShapeDtypeStruct((n_seq * seq_len, SWA_WIDTH), BF16))
    cast_specs, cast_shapes = _cast_specs(cast_weights, steps)
    in_specs += cast_specs
    args += list(cast_weights)
    out_specs += cast_specs
    out_shape += cast_shapes
    return pl.pallas_call(
        functools.partial(_gla_kernel, seq_len=seq_len, seqs=seqs, has_state=has_state, emit_state=emit_state,
                          with_attention=with_attention, n_cast=len(cast_weights)),
        grid=(steps,), in_specs=in_specs, out_specs=out_specs, out_shape=out_shape,
        scratch_shapes=[pltpu.VMEM((rows_per_step, GLA_WIDTH), F32)] * 2,
        compiler_params=_cparams(1), name="gla_bidirectional",
    )(*args)


def _group_attention(q, heads, sink_ref, score_fn, value_fn):
    rows = q.shape[0]
    q_stack = jnp.concatenate([q[:, h * HEAD_DIM:(h + 1) * HEAD_DIM] for h in heads], axis=0)
    scores = score_fn(q_stack)
    yield
    slabs = [slice(j * rows, (j + 1) * rows) for j in range(len(heads))]
    ms = []
    for slab, h in zip(slabs, heads):
        m = sink_ref[h]
        for s in scores:
            m = jnp.maximum(m, jnp.max(s[slab, :], axis=-1, keepdims=True))
        ms.append(m)
    yield
    e_parts = [[] for _ in scores]
    denoms = []
    for slab, h, m in zip(slabs, heads, ms):
        denom = jnp.exp(sink_ref[h] - m)
        for part, s in zip(e_parts, scores):
            e = jnp.exp(s[slab, :] - m)
            denom = denom + jnp.sum(e, axis=-1, keepdims=True)
            part.append(e.astype(BF16))
        denoms.append(denom)
    yield
    acc = value_fn([jnp.concatenate(part, axis=0) for part in e_parts])
    yield
    return [acc[slab, :] / denom for slab, denom in zip(slabs, denoms)]


def _run_staged(units, wave=None):
    results = [None] * len(units)
    wave = wave or len(units)
    for w0 in range(0, len(units), wave):
        live = list(range(w0, min(w0 + wave, len(units))))
        while live:
            still = []
            for u in live:
                try:
                    next(units[u])
                    still.append(u)
                except StopIteration as stop:
                    results[u] = stop.value
            live = still
    return results


def _context_attention_units(sink_ref, q_ref, k_ref, v_ref, rows):
    q = q_ref[rows, :]
    k = k_ref[rows, :]
    v = v_ref[rows, :]
    units = []
    for g in range(SWA_KV_HEADS):
        k_g = k[:, g * HEAD_DIM:(g + 1) * HEAD_DIM]
        v_g = v[:, g * HEAD_DIM:(g + 1) * HEAD_DIM]
        units.append(_group_attention(
            q, list(range(SWA_GROUP * g, SWA_GROUP * (g + 1))), sink_ref,
            lambda qs, k_g=k_g: [_dot_nt(qs, k_g)],
            lambda es, v_g=v_g: _dot(es[0], v_g)))
    return units


def _cast_specs(cast_weights, steps, step_of=lambda *idx: idx[0]):
    specs = [pl.BlockSpec((w.shape[0] // steps, w.shape[1]), lambda *idx: (step_of(*idx), 0))
             for w in cast_weights]
    shapes = [jax.ShapeDtypeStruct(w.shape, BF16) for w in cast_weights]
    return specs, shapes


def _latent_attention_units(sink_ref, q_ref, k_ref, v_ref, ctx_kv, i, seq_len):
    span = QBLOCK + 2 * WINDOW
    start = pl.multiple_of(jnp.clip(i * QBLOCK - WINDOW, 0, seq_len - span), QBLOCK)
    q = q_ref[pl.ds(pl.multiple_of(i * QBLOCK, QBLOCK), QBLOCK), :]
    k = k_ref[pl.ds(start, span), :]
    v = v_ref[pl.ds(start, span), :]
    qpos = i * QBLOCK + lax.broadcasted_iota(jnp.int32, (QBLOCK, span), 0)
    kpos = start + lax.broadcasted_iota(jnp.int32, (QBLOCK, span), 1)
    valid = jnp.abs(qpos - kpos) <= WINDOW
    valid = jnp.concatenate([valid] * SWA_GROUP, axis=0)
    units = []
    for g in range(SWA_KV_HEADS):
        k_g = k[:, g * HEAD_DIM:(g + 1) * HEAD_DIM]
        v_g = v[:, g * HEAD_DIM:(g + 1) * HEAD_DIM]
        kc_t, vc_t = ctx_kv[g]
        units.append(_group_attention(
            q, list(range(SWA_GROUP * g, SWA_GROUP * (g + 1))), sink_ref,
            lambda qs, k_g=k_g, kc_t=kc_t: [jnp.where(valid, _dot_nt(qs, k_g), NEG_INF), _dot(qs, kc_t)],
            lambda es, v_g=v_g, vc_t=vc_t: _dot(es[0], v_g) + _dot_nt(es[1], vc_t)))
    return units


def _out_kernel(xp_ref, xs_ref, ogp_ref, ogs_ref, oswp_ref, osws_ref, mod_ref, gain_ref,
                wo_ref, w1_ref, w2_ref, yp_ref, ys_ref, *, n_prompt_tiles, tiles_per_seq):
    i = pl.program_id(0)
    row = _mod_row(i, n_prompt_tiles, tiles_per_seq)

    def body(x_ref, og_ref, osw_ref, y_ref):
        mix = _dot(og_ref[...], wo_ref[:GLA_WIDTH, :]) + _dot(osw_ref[...], wo_ref[GLA_WIDTH:, :])
        x1 = x_ref[...] + _mod_chunk(mod_ref, row, 2) * mix
        h = _adaln(x1, gain_ref[...], _mod_chunk(mod_ref, row, 3), _mod_chunk(mod_ref, row, 4)).astype(BF16)
        acc = None
        for j in range(D_FF // FF_TILE):
            cols = slice(j * FF_TILE, (j + 1) * FF_TILE)
            a = jnp.maximum(_dot(h, w1_ref[:, cols]), 0.0)
            part = _dot((a * a).astype(BF16), w2_ref[cols, :])
            acc = part if acc is None else acc + part
        y_ref[...] = x1 + _mod_chunk(mod_ref, row, 5) * acc

    @pl.when(i < n_prompt_tiles)
    def _():
        body(xp_ref, ogp_ref, oswp_ref, yp_ref)

    @pl.when(i >= n_prompt_tiles)
    def _():
        body(xs_ref, ogs_ref, osws_ref, ys_ref)


def _mix_out(xp, xs, og_p, og_s, osw_p, osw_s, mod, gain, wo, w1, w2, seq_s):
    n_p, n_s = xp.shape[0], xs.shape[0]
    tp, ts = n_p // TM, n_s // TM
    tiles_per_seq = seq_s // TM
    const = lambda shape: pl.BlockSpec(shape, lambda i: (0,) * len(shape), pipeline_mode=pl.Buffered(1))
    row_p = lambda w: pl.BlockSpec((TM, w), lambda i: (jnp.minimum(i, tp - 1), 0))
    row_s = lambda w: pl.BlockSpec((TM, w), lambda i: (jnp.maximum(i - tp, 0), 0))
    return pl.pallas_call(
        functools.partial(_out_kernel, n_prompt_tiles=tp, tiles_per_seq=tiles_per_seq),
        grid=(tp + ts,),
        in_specs=[row_p(D_MODEL), row_s(D_MODEL), row_p(GLA_WIDTH), row_s(GLA_WIDTH),
                  row_p(SWA_WIDTH), row_s(SWA_WIDTH),
                  const(mod.shape),
                  const((1, D_MODEL)), const((GLA_WIDTH + SWA_WIDTH, D_MODEL)),
                  const((D_MODEL, D_FF)), const((D_FF, D_MODEL))],
        out_specs=[row_p(D_MODEL), row_s(D_MODEL)],
        out_shape=[jax.ShapeDtypeStruct((n_p, D_MODEL), F32), jax.ShapeDtypeStruct((n_s, D_MODEL), F32)],
        compiler_params=_cparams(1), name="out_proj_mlp",
    )(xp, xs, og_p, og_s, osw_p, osw_s, mod, gain, wo, w1, w2)


def _rope_tables(seq_len):
    axis_dim = HEAD_DIM // 2
    half = axis_dim // 2
    t = np.arange(seq_len)
    rowp = (t // GRID_W).astype(np.float64)
    colp = (t % GRID_W).astype(np.float64)
    freqs = ROPE_THETA ** (-np.arange(half, dtype=np.float64) / half)
    lane = np.arange(LANES)
    d = lane % HEAD_DIM
    posv = np.where((d // axis_dim)[None, :] == 0, rowp[:, None], colp[:, None])
    ang = posv * freqs[d % half][None, :]
    second = ((d % axis_dim) >= half)[None, :]
    cos, sin = np.cos(ang), np.sin(ang)
    zero = np.zeros_like(sin)
    tabs = (cos, np.where(second, sin, zero), np.where(second, zero, -sin))
    return tuple(jnp.asarray(tab.astype(np.float32)) for tab in tabs)


def kernel(x_prompt, x_sample, c, cache_k, cache_v, state_gla_fwd, state_gla_bwd, c_ctx, w_ada, b_ada, norm_mix, norm_ff, w_in, w_gk_fwd, b_gk_fwd, w_gk_bwd, b_gk_bwd, gla_norm, q_norm, k_norm, sink, w_o, w_ff1, w_ff2):
    depth = w_in.shape[0]
    assert depth == 1, "single trunk layer"
    bp, tp, _ = x_prompt.shape
    bs, ts, _ = x_sample.shape
    n_p = bp * tp
    l = 0

    mod = _modulation(c_ctx[None, :], c, w_ada[l], b_ada[l])

    norm2 = gla_norm[l][None, :]
    gain_mix = norm_mix[l][None, :]
    gain_ff = norm_ff[l][None, :]
    sink_l = sink[l]

    w_in_t = jnp.transpose(w_in[l])
    cache_k_t = jnp.swapaxes(cache_k, 3, 4)
    cache_v_t = jnp.swapaxes(cache_v, 3, 4)

    xp = x_prompt.reshape(n_p, D_MODEL)
    xs = x_sample.reshape(bs * ts, D_MODEL)
    gla_in, gates, swa_in, new_k_t, new_v_t = _project(
        xp, xs, mod, gain_mix, w_in_t, w_gk_fwd[l], w_gk_bwd[l], b_gk_fwd[l][None, :], b_gk_bwd[l][None, :],
        q_norm[l][None, :], k_norm[l][None, :], _rope_tables(ts), tp, ts)
    new_k = jnp.swapaxes(new_k_t, 3, 4)
    new_v = jnp.swapaxes(new_v_t, 3, 4)

    og_p, s_f, s_b, osw_p, wo, w1, w2 = _gla(
        gla_in, gates, norm2, None, None, bp, tp, 0, True, seqs=2,
        attention=(sink_l, swa_in), cast_weights=(w_o[l], w_ff1[l], w_ff2[l]))
    og_s, osw_s = _gla(gla_in, gates, norm2, state_gla_fwd, state_gla_bwd, bs, ts, n_p // ts, False,
                       attention=(sink_l, swa_in, cache_k_t, cache_v_t))

    yp, ys = _mix_out(xp, xs, og_p, og_s, osw_p, osw_s, mod, gain_ff, wo, w1, w2, ts)
    return (yp.reshape(bp, tp, D_MODEL), ys.reshape(bs, ts, D_MODEL), new_k, new_v, s_f, s_b)
```

```python
import functools

import numpy as np
import jax
import jax.numpy as jnp
from jax import lax
from jax.experimental import pallas as pl
from jax.experimental.pallas import tpu as pltpu

F32 = jnp.float32
BF16 = jnp.bfloat16

D_MODEL = 1024
GRID_W = 64
HEAD_DIM = 64
GLA_HEADS = 8
GLA_WIDTH = GLA_HEADS * HEAD_DIM
GATE_RANK = 16
GATE_TEMP = 16.0
CHUNK = 64
GROUP = 4 * CHUNK
SWA_HEADS = 8
SWA_KV_HEADS = 2
SWA_GROUP = SWA_HEADS // SWA_KV_HEADS
SWA_WIDTH = SWA_HEADS * HEAD_DIM
KV_WIDTH = SWA_KV_HEADS * HEAD_DIM
WINDOW = 128
QBLOCK = 128
D_FF = 4 * D_MODEL
ROPE_THETA = 10000.0
EPS = 1e-6
NEG_INF = -1e30
LANES = 128
TM = 512
FF_TILE = 1024

D_IN = 2848
R_GQ, R_GK, R_GV, R_GG, R_LR = 0, 512, 1024, 1536, 2048
R_SQ = R_LR + 2 * GATE_RANK
R_SV = R_SQ + SWA_WIDTH + KV_WIDTH
C_TAIL = R_LR
C_LR = C_TAIL + D_IN - R_SQ
W_COLS = C_LR + LANES

V7X_VMEM_BYTES = 64 * 1024 * 1024
VMEM_LIMIT = V7X_VMEM_BYTES - 8 * 1024 * 1024


def _cparams(n_axes):
    return pltpu.CompilerParams(dimension_semantics=("arbitrary",) * n_axes, vmem_limit_bytes=VMEM_LIMIT)


def _mod_kernel(cctx_ref, c_ref, w_ref, b_ref, o_ref):
    k = pl.program_id(0)
    pad = jnp.zeros((8 - 1 - c_ref.shape[0], c_ref.shape[1]), F32)
    cnd = jnp.concatenate([cctx_ref[...], c_ref[...], pad], axis=0)
    s = (cnd * jax.nn.sigmoid(cnd)).astype(BF16)
    part = _dot(s, w_ref[...].astype(BF16))

    @pl.when(k == 0)
    def _():
        o_ref[...] = part + b_ref[...]

    @pl.when(k > 0)
    def _():
        o_ref[...] += part


def _modulation(c_ctx, c, w_ada, b_ada):
    tk = 512
    n_out = w_ada.shape[1]
    return pl.pallas_call(
        _mod_kernel,
        grid=(D_MODEL // tk,),
        in_specs=[pl.BlockSpec((1, tk), lambda k: (0, k)),
                  pl.BlockSpec((c.shape[0], tk), lambda k: (0, k)),
                  pl.BlockSpec((tk, n_out), lambda k: (k, 0)),
                  pl.BlockSpec((1, n_out), lambda k: (0, 0))],
        out_specs=pl.BlockSpec((8, n_out), lambda k: (0, 0)),
        out_shape=jax.ShapeDtypeStruct((8, n_out), F32),
        compiler_params=_cparams(1),
        name="modulation",
    )(c_ctx, c, w_ada, b_ada.reshape(1, n_out))


def _mod_row(tile, n_prompt_tiles, tiles_per_seq):
    return jnp.where(tile < n_prompt_tiles, 0, 1 + (tile - n_prompt_tiles) // tiles_per_seq)


def _mod_chunk(mod_ref, row, j):
    return mod_ref[pl.ds(row, 1), j * D_MODEL:(j + 1) * D_MODEL]


def _adaln(x, gain, shift, scale):
    ms = jnp.mean(x * x, axis=-1, keepdims=True)
    xn = x * lax.rsqrt(ms + EPS)
    return xn * (gain * (1.0 + scale)) + shift


def _head_rms_norm(y, gain):
    cols = y.shape[1]
    lane = lax.broadcasted_iota(jnp.int32, (1, LANES), 1)
    first = lane < HEAD_DIM
    outs = []
    for p in range(cols // LANES):
        s = y[:, p * LANES:(p + 1) * LANES]
        s = s * s
        tot = jnp.sum(s, axis=-1, keepdims=True)
        lo = jnp.sum(jnp.where(first, s, 0.0), axis=-1, keepdims=True)
        outs.append(jnp.where(first, lo, tot - lo))
    sums = outs[0] if len(outs) == 1 else jnp.concatenate(outs, axis=-1)
    return y * lax.rsqrt(sums + HEAD_DIM * EPS) * (gain * (HEAD_DIM ** 0.5))


def _log_sigmoid(x):
    return jnp.minimum(x, 0.0) - jnp.log(1.0 + jnp.exp(-jnp.abs(x)))


def _dot(a, b):
    return jnp.dot(a, b, preferred_element_type=F32)


def _dot_nt(a, b):
    return lax.dot_general(a, b, (((1,), (1,)), ((), ())), preferred_element_type=F32)


def _proj_kernel(xp_ref, xs_ref, mod_ref, gain_ref, w_ref, wgf_ref, wgb_ref, bgf_ref, bgb_ref, qn_ref, kn_ref,
                 cos_ref, sin_a_ref, sin_b_ref,
                 gla_ref, gate_ref, swa_ref, nk_ref, nv_ref,
                 wb_ref, wg_ref, h0_ref, h1_ref, lr0_ref, lr1_ref,
                 *, n_prompt_tiles, n_tiles, tiles_per_seq, seqs_per_tile, seq_len):
    h_refs = (h0_ref, h1_ref)
    lr_refs = (lr0_ref, lr1_ref)
    s = pl.program_id(0)

    @pl.when(s == 0)
    def _():
        moves = ([(c * LANES, c * LANES) for c in range(R_LR // LANES)]
                 + [(R_SQ + c * LANES, C_TAIL + c * LANES) for c in range((D_IN - R_SQ) // LANES)]
                 + [(R_LR, C_LR)])
        for src, dst in moves:
            wb_ref[:, dst:dst + LANES] = w_ref[src:src + LANES, :].T.astype(BF16)
        wg_ref[...] = jnp.zeros(wg_ref.shape, BF16)
        wg_ref[0:GATE_RANK, 0:GLA_WIDTH] = wgf_ref[...].astype(BF16)
        wg_ref[GATE_RANK:2 * GATE_RANK, GLA_WIDTH:] = wgb_ref[...].astype(BF16)

    def normalise_next(parity):
        x = jnp.where(s < n_prompt_tiles, xp_ref[...], xs_ref[...])
        row = _mod_row(jnp.minimum(s, n_tiles - 1), n_prompt_tiles, tiles_per_seq)
        shift, scale = _mod_chunk(mod_ref, row, 0), _mod_chunk(mod_ref, row, 1)
        h_refs[parity][...] = _adaln(x, gain_ref[...], shift, scale).astype(BF16)

    def gates(parity):
        gate_pre = _dot(lr_refs[parity][...], wg_ref[...])
        gate_ref[:, :GLA_WIDTH] = _log_sigmoid(gate_pre[:, :GLA_WIDTH] + bgf_ref[...]) * (1.0 / GATE_TEMP)
        gate_ref[:, GLA_WIDTH:] = _log_sigmoid(gate_pre[:, GLA_WIDTH:] + bgb_ref[...]) * (1.0 / GATE_TEMP)

    def body(rope, emit_cache, parity, with_gates):
        h_ref = h_refs[1 - parity]
        if with_gates:
            gates(parity)

        def proj(c0, c1):
            return _dot(h_ref[...], wb_ref[:, c0:c1])

        tail = proj(C_TAIL, C_LR)
        lr_refs[1 - parity][...] = proj(C_LR, W_COLS).astype(BF16)
        gg = proj(R_GG, R_LR)
        gla_ref[:, 3 * GLA_WIDTH:] = (gg * jax.nn.sigmoid(gg)).astype(BF16)
        attention_inputs(tail, rope, emit_cache)
        gla_ref[:, :GLA_WIDTH] = (proj(R_GQ, R_GK) * (HEAD_DIM ** -0.5)).astype(BF16)
        gla_ref[:, GLA_WIDTH:2 * GLA_WIDTH] = proj(R_GK, R_GV).astype(BF16)
        gla_ref[:, 2 * GLA_WIDTH:3 * GLA_WIDTH] = proj(R_GV, R_GG).astype(BF16)
        normalise_next(parity)

    def attention_inputs(tail, rope, emit_cache):
        qk = tail[:, :R_SV - R_SQ]
        qk_gain = jnp.concatenate([qn_ref[...]] * SWA_HEADS + [kn_ref[...]] * SWA_KV_HEADS, axis=-1)
        qk = _head_rms_norm(qk, qk_gain)
        if rope:
            reps = (SWA_WIDTH + KV_WIDTH) // LANES
            cos = jnp.concatenate([cos_ref[...]] * reps, axis=-1)
            sin_a = jnp.concatenate([sin_a_ref[...]] * reps, axis=-1)
            sin_b = jnp.concatenate([sin_b_ref[...]] * reps, axis=-1)
            width = qk.shape[1]
            quarter = HEAD_DIM // 4
            qk = (qk * cos + pltpu.roll(qk, quarter, 1) * sin_a
                  + pltpu.roll(qk, width - quarter, 1) * sin_b)
        swa_ref[:, :SWA_WIDTH] = (qk[:, :SWA_WIDTH] * (HEAD_DIM ** -0.5)).astype(BF16)
        sk = qk[:, SWA_WIDTH:]
        swa_ref[:, SWA_WIDTH:SWA_WIDTH + KV_WIDTH] = sk.astype(BF16)
        sv = tail[:, R_SV - R_SQ:]
        swa_ref[:, SWA_WIDTH + KV_WIDTH:] = sv.astype(BF16)
        if emit_cache:
            sk_t, sv_t = sk.T, sv.T
            for q in range(seqs_per_tile):
                cols = slice(q * seq_len, (q + 1) * seq_len)
                for g in range(SWA_KV_HEADS):
                    feat = slice(g * HEAD_DIM, (g + 1) * HEAD_DIM)
                    nk_ref[q, 0, g, :, :] = sk_t[feat, cols]
                    nv_ref[q, 0, g, :, :] = sv_t[feat, cols]

    @pl.when(s == 0)
    def _():
        normalise_next(0)

    @pl.when(s == 1)
    def _():
        body(rope=False, emit_cache=True, parity=1, with_gates=False)

    for parity in range(2):
        on_parity = (s % 2) == parity

        @pl.when((s >= 2) & (s <= n_prompt_tiles) & on_parity)
        def _():
            body(rope=False, emit_cache=True, parity=parity, with_gates=True)

        @pl.when((s > n_prompt_tiles) & (s <= n_tiles) & on_parity)
        def _():
            body(rope=True, emit_cache=False, parity=parity, with_gates=True)

    @pl.when(s == n_tiles + 1)
    def _():
        gates((n_tiles + 1) % 2)


def _project(xp, xs, mod, gain, w_in, w_gk_f, w_gk_b, b_gk_f, b_gk_b, q_norm, k_norm, rope_tabs, seq_p, seq_s):
    n_p, n_s = xp.shape[0], xs.shape[0]
    tp, ts = n_p // TM, n_s // TM
    n = n_p + n_s
    seqs_per_tile = TM // seq_p
    tiles_per_seq = seq_s // TM
    n_seq_p = n_p // seq_p
    const = lambda shape: pl.BlockSpec(shape, lambda s: (0,) * len(shape))
    n_tiles = tp + ts
    out_tile = lambda s: jnp.clip(s - 1, 0, n_tiles - 1)
    gate_tile = lambda s: jnp.maximum(s - 2, 0)
    rope_spec = pl.BlockSpec((TM, LANES), lambda s: (jnp.maximum(out_tile(s) - tp, 0) % tiles_per_seq, 0))
    in_specs = [pl.BlockSpec((TM, D_MODEL), lambda s: (jnp.minimum(s, tp - 1), 0)),
                pl.BlockSpec((TM, D_MODEL), lambda s: (jnp.clip(s - tp, 0, ts - 1), 0)),
                const(mod.shape),
                const((1, D_MODEL)),
                pl.BlockSpec((D_IN, D_MODEL), lambda s: (0, 0), pipeline_mode=pl.Buffered(1)),
                const(w_gk_f.shape), const(w_gk_b.shape), const(b_gk_f.shape), const(b_gk_b.shape),
                const(q_norm.shape), const(k_norm.shape),
                rope_spec, rope_spec, rope_spec]
    row = lambda w: pl.BlockSpec((TM, w), lambda s: (out_tile(s), 0))
    cache_spec = pl.BlockSpec((seqs_per_tile, 1, SWA_KV_HEADS, HEAD_DIM, seq_p),
                              lambda s: (jnp.minimum(out_tile(s), tp - 1), 0, 0, 0, 0))
    widths = (4 * GLA_WIDTH, 2 * GLA_WIDTH, SWA_WIDTH + 2 * KV_WIDTH)
    out_specs = [row(widths[0]), pl.BlockSpec((TM, widths[1]), lambda s: (gate_tile(s), 0)), row(widths[2]),
                 cache_spec, cache_spec]
    cache_shape = (n_seq_p, 1, SWA_KV_HEADS, HEAD_DIM, seq_p)
    out_shape = ([jax.ShapeDtypeStruct((n, w), dt) for w, dt in zip(widths, (BF16, F32, BF16))]
                 + [jax.ShapeDtypeStruct(cache_shape, F32)] * 2)
    return pl.pallas_call(
        functools.partial(_proj_kernel, n_prompt_tiles=tp, n_tiles=n_tiles, tiles_per_seq=tiles_per_seq,
                          seqs_per_tile=seqs_per_tile, seq_len=seq_p),
        grid=(n_tiles + 2,), in_specs=in_specs, out_specs=out_specs, out_shape=out_shape,
        scratch_shapes=([pltpu.VMEM((D_MODEL, W_COLS), BF16), pltpu.VMEM((LANES, 2 * GLA_WIDTH), BF16)]
                        + [pltpu.VMEM((TM, D_MODEL), BF16)] * 2 + [pltpu.VMEM((TM, LANES), BF16)] * 2),
        compiler_params=_cparams(1), name="adaln_in_proj",
    )(xp, xs, mod, gain, w_in, w_gk_f, w_gk_b, b_gk_f, b_gk_b, q_norm, k_norm, *rope_tabs)


def _gla_kernel(*refs, seq_len, seqs, has_state, emit_state, with_attention, n_cast):
    (q_ref, k_ref, v_ref, gf_ref, gb_ref, sgg_ref, norm_ref,
     trif_ref, trib_ref, keepf_ref, keepb_ref) = refs[:11]
    pos = 11
    if has_state:
        s0f_ref, s0b_ref = refs[pos:pos + 2]
        pos += 2
    if with_attention:
        sink_ref, aq_ref, ak_ref, av_ref = refs[pos:pos + 4]
        pos += 4
    if with_attention == "latent":
        kc_ref, vc_ref = refs[pos:pos + 2]
        pos += 2
    cast_in = refs[pos:pos + n_cast]
    pos += n_cast
    og_ref = refs[pos]
    pos += 1
    if emit_state:
        sf_ref, sb_ref = refs[pos:pos + 2]
        pos += 2
    if with_attention:
        osw_ref = refs[pos]
        pos += 1
    cast_out = refs[pos:pos + n_cast]
    pos += n_cast
    accf_ref, accb_ref = refs[pos:pos + 2]

    for src, dst in zip(cast_in, cast_out):
        dst[...] = src[...].astype(BF16)

    n_groups = seq_len // GROUP
    per_group = GROUP // CHUNK
    n_pairs = q_ref.shape[1] // LANES
    lane = lax.broadcasted_iota(jnp.int32, (1, LANES), 1)
    head0 = lane < HEAD_DIM
    m0 = jnp.where(head0, 1.0, 0.0).astype(BF16)
    m1 = jnp.where(head0, 0.0, 1.0).astype(BF16)
    row_top = lax.broadcasted_iota(jnp.int32, (LANES, 1), 0) < HEAD_DIM

    def stack_heads(x):
        return jnp.concatenate([x * m0, x * m1], axis=0)

    def direction(row0, lanes, g_ref, tri_ref, keep_ref, state, acc_ref, backward):
        rows = pl.ds(row0, GROUP)
        g = g_ref[rows, lanes]
        g_hi = g.astype(BF16)
        g_lo = (g - g_hi.astype(F32)).astype(BF16)
        r = _dot(tri_ref[...], jnp.concatenate([g_hi, g_lo], axis=-1))
        yield
        b = (r[:, :LANES] + r[:, LANES:]).reshape(per_group, CHUNK, LANES)
        edge = 0 if backward else CHUNK - 1
        mid = CHUNK // 2
        b_edge = b[:, edge:edge + 1, :]
        b_mid = b[:, mid:mid + 1, :]
        q = q_ref[rows, lanes].astype(F32).reshape(per_group, CHUNK, LANES)
        k = k_ref[rows, lanes].astype(F32).reshape(per_group, CHUNK, LANES)
        v = v_ref[rows, lanes]
        q_mid = q * jnp.exp(b - b_mid)
        k_mid = k * jnp.exp(b_mid - b)
        q_in = q_mid.astype(BF16)
        k_in = k_mid.astype(BF16).reshape(GROUP, LANES)
        q_st = (q_mid * jnp.exp(b_mid)).astype(BF16)
        k_st = (k_mid * jnp.exp(b_edge - b_mid)).astype(BF16).reshape(GROUP, LANES)
        decay = jnp.exp(b_edge)
        keep = keep_ref[...]
        pair_rows = 2 * CHUNK
        n_cp = per_group // 2
        yield
        scores, kvs = [], []
        for p in range(n_cp):
            c0 = 2 * p
            prow = slice(p * pair_rows, (p + 1) * pair_rows)
            q_pair = jnp.concatenate([stack_heads(q_in[c0]), stack_heads(q_in[c0 + 1])], axis=0)
            scores.append(_dot_nt(q_pair, k_in[prow, :]).astype(BF16) * keep)
            v_t = v[prow, :].astype(F32).T.astype(BF16)
            k_pair = k_st[prow, :]
            zero = jnp.zeros_like(k_pair)
            k_blk = jnp.concatenate([jnp.where(row_top, k_pair, zero), jnp.where(row_top, zero, k_pair)], axis=-1)
            kvs.append(_dot(v_t, k_blk))
        yield
        intras = [_dot(scores[p], v[p * pair_rows:(p + 1) * pair_rows, :]) for p in range(n_cp)]
        yield
        for p in (range(n_cp - 1, -1, -1) if backward else range(n_cp)):
            for j in ((1, 0) if backward else (0, 1)):
                c = 2 * p + j
                inter = _dot_nt(stack_heads(q_st[c]), state.astype(BF16))
                tot = intras[p][j * pair_rows:(j + 1) * pair_rows, :] + inter
                acc_ref[pl.ds(row0 + c * CHUNK, CHUNK), lanes] = jnp.where(head0, tot[:CHUNK, :], tot[CHUNK:, :])
                state = state * decay[c] + kvs[p][:, j * LANES:(j + 1) * LANES]
            yield
        return state

    def load_state(s_ref, q, p):
        z = jnp.zeros((HEAD_DIM, HEAD_DIM), F32)
        top = jnp.concatenate([s_ref[q, 0, 2 * p, :, :].T, z], axis=-1)
        bot = jnp.concatenate([z, s_ref[q, 0, 2 * p + 1, :, :].T], axis=-1)
        return jnp.concatenate([top, bot], axis=0)

    per_dir = seqs * n_pairs
    if has_state:
        init = tuple(load_state(s_ref, q, p)
                     for s_ref in (s0f_ref, s0b_ref) for q in range(seqs) for p in range(n_pairs))
    else:
        init = (jnp.zeros((LANES, LANES), F32),) * (2 * per_dir)

    def body(t, carry):
        row_f = t * GROUP
        row_b = (n_groups - 1 - t) * GROUP
        units = []
        for backward, row, g_ref, tri_ref, keep_ref, acc_ref in (
                (False, row_f, gf_ref, trif_ref, keepf_ref, accf_ref),
                (True, row_b, gb_ref, trib_ref, keepb_ref, accb_ref)):
            for q in range(seqs):
                for p in range(n_pairs):
                    lanes = slice(p * LANES, (p + 1) * LANES)
                    state = carry[backward * per_dir + q * n_pairs + p]
                    units.append(direction(q * seq_len + row, lanes, g_ref, tri_ref, keep_ref, state, acc_ref,
                                           backward))
        return units

    norm_pair = jnp.concatenate([norm_ref[...]] * 2, axis=-1)

    def finish_outputs(rows):
        for p in range(n_pairs):
            lanes = slice(p * LANES, (p + 1) * LANES)
            o = _head_rms_norm(accf_ref[rows, lanes] + accb_ref[rows, lanes], norm_pair)
            og_ref[rows, lanes] = (o * sgg_ref[rows, lanes].astype(F32)).astype(BF16)
            yield

    if n_groups == 1:
        gla_units = body(0, init)
        final = [None] * len(gla_units)
        for q in range(seqs + 1):
            rows = slice(q * seq_len, (q + 1) * seq_len)
            wave, ids = [], []
            if q < seqs:
                ids = [d * per_dir + q * n_pairs + p for d in range(2) for p in range(n_pairs)]
                wave = [gla_units[u] for u in ids]
                if with_attention:
                    wave += _context_attention_units(sink_ref, aq_ref, ak_ref, av_ref, rows)
            if q > 0:
                wave.append(finish_outputs(slice((q - 1) * seq_len, q * seq_len)))
            results = _run_staged(wave)
            for u, r in zip(ids, results):
                final[u] = r
            if q < seqs and with_attention:
                heads = [o for outs in results[len(ids):len(ids) + SWA_KV_HEADS] for o in outs]
                osw_ref[rows, :] = jnp.concatenate(heads, axis=-1).astype(BF16)
    else:
        assert with_attention != "context" and seqs == 1
        if with_attention:
            ctx_kv = [(kc_ref[0, 0, g, :, :].astype(BF16), vc_ref[0, 0, g, :, :].astype(BF16))
                      for g in range(SWA_KV_HEADS)]
            q_blocks = GROUP // QBLOCK

        def done_after(t):
            return [g for g in range(n_groups) if max(g, n_groups - 1 - g) == t]

        def loop_body(t, carry):
            units = body(t, carry)
            n_gla = len(units)
            if with_attention:
                for j in range(q_blocks):
                    units += _latent_attention_units(sink_ref, aq_ref, ak_ref, av_ref, ctx_kv,
                                                     t * q_blocks + j, seq_len)
            units += [finish_outputs(slice(g * GROUP, (g + 1) * GROUP)) for g in done_after(t - 1)]
            results = _run_staged(units)
            if with_attention:
                for j in range(q_blocks):
                    heads = [o for outs in results[n_gla + j * SWA_KV_HEADS:n_gla + (j + 1) * SWA_KV_HEADS]
                             for o in outs]
                    rows = pl.ds((t * q_blocks + j) * QBLOCK, QBLOCK)
                    osw_ref[rows, :] = jnp.concatenate(heads, axis=-1).astype(BF16)
            return tuple(results[:n_gla])

        final = init
        for t in range(n_groups):
            final = loop_body(t, final)
        _run_staged([finish_outputs(slice(g * GROUP, (g + 1) * GROUP)) for g in done_after(n_groups - 1)])
    if emit_state:
        for d, s_ref in enumerate((sf_ref, sb_ref)):
            for q in range(seqs):
                for p in range(n_pairs):
                    fin = final[d * per_dir + q * n_pairs + p]
                    s_ref[q, 0, 2 * p, :, :] = fin[:HEAD_DIM, :HEAD_DIM].T
                    s_ref[q, 0, 2 * p + 1, :, :] = fin[HEAD_DIM:, HEAD_DIM:].T


def _gla_constants():
    i = np.arange(GROUP)[:, None]
    j = np.arange(GROUP)[None, :]
    same = (i // CHUNK) == (j // CHUNK)
    tri_f = same & (j <= i)
    tri_b = same & (j >= i)
    r = np.arange(4 * CHUNK)[:, None]
    l = np.arange(LANES)[None, :]
    same = (r // (2 * CHUNK)) == (l // CHUNK)
    keep_f = same & ((l % CHUNK) <= (r % CHUNK))
    keep_b = same & ((l % CHUNK) >= (r % CHUNK))
    return tuple(jnp.asarray(m.astype(np.float32), dtype=BF16) for m in (tri_f, tri_b, keep_f, keep_b))


def _gla(gla_in, gates, norm_row, s0f, s0b, n_seq, seq_len, row_block0, emit_state, seqs=1,
         attention=None, cast_weights=()):
    has_state = s0f is not None
    steps = n_seq // seqs
    rows_per_step = seqs * seq_len
    row_blk = row_block0 // seqs
    col = lambda width, j: pl.BlockSpec((rows_per_step, width), lambda b: (row_blk + b, j))
    out_blk = pl.BlockSpec((rows_per_step, GLA_WIDTH), lambda b: (b, 0))
    st_spec = pl.BlockSpec((seqs, 1, GLA_HEADS, HEAD_DIM, HEAD_DIM), lambda b: (b, 0, 0, 0, 0))
    const = lambda shape: pl.BlockSpec(shape, lambda b: (0, 0))
    in_specs = ([col(GLA_WIDTH, 0), col(GLA_WIDTH, 1), col(GLA_WIDTH, 2), col(GLA_WIDTH, 0), col(GLA_WIDTH, 1),
                 col(GLA_WIDTH, 3)]
                + [const(norm_row.shape)] + [const((GROUP, GROUP))] * 2 + [const((4 * CHUNK, LANES))] * 2)
    args = [gla_in, gla_in, gla_in, gates, gates, gla_in, norm_row, *_gla_constants()]
    if has_state:
        in_specs += [st_spec] * 2
        args += [s0f, s0b]
    out_specs = [out_blk]
    out_shape = [jax.ShapeDtypeStruct((n_seq * seq_len, GLA_WIDTH), BF16)]
    if emit_state:
        out_specs += [st_spec] * 2
        out_shape += [jax.ShapeDtypeStruct((n_seq, 1, GLA_HEADS, HEAD_DIM, HEAD_DIM), F32)] * 2
    with_attention = None
    if attention is not None:
        sink, swa_in = attention[:2]
        with_attention = "context"
        in_specs += [pl.BlockSpec(memory_space=pltpu.SMEM), col(SWA_WIDTH, 0),
                     col(KV_WIDTH, SWA_WIDTH // KV_WIDTH), col(KV_WIDTH, SWA_WIDTH // KV_WIDTH + 1)]
        args += [sink, swa_in, swa_in, swa_in]
        if len(attention) > 2:
            with_attention = "latent"
            cache_k_t, cache_v_t = attention[2:]
            past = cache_k_t.shape[4]
            cache_spec = pl.BlockSpec((seqs, 1, SWA_KV_HEADS, HEAD_DIM, past), lambda b: (b, 0, 0, 0, 0))
            in_specs += [cache_spec] * 2
            args += [cache_k_t, cache_v_t]
        out_specs.append(pl.BlockSpec((rows_per_step, SWA_WIDTH), lambda b: (b, 0)))
        out_shape.append(jax.ShapeDtypeStruct((n_seq * seq_len, SWA_WIDTH), BF16))
    cast_specs, cast_shapes = _cast_specs(cast_weights, steps)
    in_specs += cast_specs
    args += list(cast_weights)
    out_specs += cast_specs
    out_shape += cast_shapes
    return pl.pallas_call(
        functools.partial(_gla_kernel, seq_len=seq_len, seqs=seqs, has_state=has_state, emit_state=emit_state,
                          with_attention=with_attention, n_cast=len(cast_weights)),
        grid=(steps,), in_specs=in_specs, out_specs=out_specs, out_shape=out_shape,
        scratch_shapes=[pltpu.VMEM((rows_per_step, GLA_WIDTH), F32)] * 2,
        compiler_params=_cparams(1), name="gla_bidirectional",
    )(*args)


def _group_attention(q, heads, sink_ref, score_fn, value_fn):
    rows = q.shape[0]
    q_stack = jnp.concatenate([q[:, h * HEAD_DIM:(h + 1) * HEAD_DIM] for h in heads], axis=0)
    scores = score_fn(q_stack)
    yield
    slabs = [slice(j * rows, (j + 1) * rows) for j in range(len(heads))]
    ms = []
    for slab, h in zip(slabs, heads):
        m = sink_ref[h]
        for s in scores:
            m = jnp.maximum(m, jnp.max(s[slab, :], axis=-1, keepdims=True))
        ms.append(m)
    yield
    e_parts = [[] for _ in scores]
    denoms = []
    for slab, h, m in zip(slabs, heads, ms):
        denom = jnp.exp(sink_ref[h] - m)
        for part, s in zip(e_parts, scores):
            e = jnp.exp(s[slab, :] - m)
            denom = denom + jnp.sum(e, axis=-1, keepdims=True)
            part.append(e.astype(BF16))
        denoms.append(denom)
    yield
    acc = value_fn([jnp.concatenate(part, axis=0) for part in e_parts])
    yield
    return [acc[slab, :] / denom for slab, denom in zip(slabs, denoms)]


def _run_staged(units):
    results = [None] * len(units)
    live = list(range(len(units)))
    while live:
        still = []
        for u in live:
            try:
                next(units[u])
                still.append(u)
            except StopIteration as stop:
                results[u] = stop.value
        live = still
    return results


def _context_attention_units(sink_ref, q_ref, k_ref, v_ref, rows):
    q = q_ref[rows, :]
    k = k_ref[rows, :]
    v = v_ref[rows, :]
    units = []
    for g in range(SWA_KV_HEADS):
        k_g = k[:, g * HEAD_DIM:(g + 1) * HEAD_DIM]
        v_g = v[:, g * HEAD_DIM:(g + 1) * HEAD_DIM]
        units.append(_group_attention(
            q, list(range(SWA_GROUP * g, SWA_GROUP * (g + 1))), sink_ref,
            lambda qs, k_g=k_g: [_dot_nt(qs, k_g)],
            lambda es, v_g=v_g: _dot(es[0], v_g)))
    return units


def _cast_specs(cast_weights, steps, step_of=lambda *idx: idx[0]):
    specs = [pl.BlockSpec((w.shape[0] // steps, w.shape[1]), lambda *idx: (step_of(*idx), 0))
             for w in cast_weights]
    shapes = [jax.ShapeDtypeStruct(w.shape, BF16) for w in cast_weights]
    return specs, shapes


def _latent_attention_units(sink_ref, q_ref, k_ref, v_ref, ctx_kv, i, seq_len):
    span = QBLOCK + 2 * WINDOW
    start = min(max(i * QBLOCK - WINDOW, 0), seq_len - span)
    q = q_ref[pl.ds(i * QBLOCK, QBLOCK), :]
    k = k_ref[pl.ds(start, span), :]
    v = v_ref[pl.ds(start, span), :]
    qpos = i * QBLOCK + lax.broadcasted_iota(jnp.int32, (QBLOCK, span), 0)
    kpos = start + lax.broadcasted_iota(jnp.int32, (QBLOCK, span), 1)
    valid = jnp.abs(qpos - kpos) <= WINDOW
    valid = jnp.concatenate([valid] * SWA_GROUP, axis=0)
    units = []
    for g in range(SWA_KV_HEADS):
        k_g = k[:, g * HEAD_DIM:(g + 1) * HEAD_DIM]
        v_g = v[:, g * HEAD_DIM:(g + 1) * HEAD_DIM]
        kc_t, vc_t = ctx_kv[g]
        units.append(_group_attention(
            q, list(range(SWA_GROUP * g, SWA_GROUP * (g + 1))), sink_ref,
            lambda qs, k_g=k_g, kc_t=kc_t: [jnp.where(valid, _dot_nt(qs, k_g), NEG_INF), _dot(qs, kc_t)],
            lambda es, v_g=v_g, vc_t=vc_t: _dot(es[0], v_g) + _dot_nt(es[1], vc_t)))
    return units


def _out_kernel(xp_ref, xs_ref, ogp_ref, ogs_ref, oswp_ref, osws_ref, mod_ref, gain_ref,
                wo_ref, w1_ref, w2_ref, yp_ref, ys_ref, *, n_prompt_tiles, tiles_per_seq):
    i = pl.program_id(0)
    row = _mod_row(i, n_prompt_tiles, tiles_per_seq)

    def body(x_ref, og_ref, osw_ref, y_ref):
        mix = _dot(og_ref[...], wo_ref[:GLA_WIDTH, :]) + _dot(osw_ref[...], wo_ref[GLA_WIDTH:, :])
        x1 = x_ref[...] + _mod_chunk(mod_ref, row, 2) * mix
        h = _adaln(x1, gain_ref[...], _mod_chunk(mod_ref, row, 3), _mod_chunk(mod_ref, row, 4)).astype(BF16)
        acc = None
        for j in range(D_FF // FF_TILE):
            cols = slice(j * FF_TILE, (j + 1) * FF_TILE)
            a = jnp.maximum(_dot(h, w1_ref[:, cols]), 0.0)
            part = _dot((a * a).astype(BF16), w2_ref[cols, :])
            acc = part if acc is None else acc + part
        y_ref[...] = x1 + _mod_chunk(mod_ref, row, 5) * acc

    @pl.when(i < n_prompt_tiles)
    def _():
        body(xp_ref, ogp_ref, oswp_ref, yp_ref)

    @pl.when(i >= n_prompt_tiles)
    def _():
        body(xs_ref, ogs_ref, osws_ref, ys_ref)


def _mix_out(xp, xs, og_p, og_s, osw_p, osw_s, mod, gain, wo, w1, w2, seq_s):
    n_p, n_s = xp.shape[0], xs.shape[0]
    tp, ts = n_p // TM, n_s // TM
    tiles_per_seq = seq_s // TM
    const = lambda shape: pl.BlockSpec(shape, lambda i: (0,) * len(shape), pipeline_mode=pl.Buffered(1))
    row_p = lambda w: pl.BlockSpec((TM, w), lambda i: (jnp.minimum(i, tp - 1), 0))
    row_s = lambda w: pl.BlockSpec((TM, w), lambda i: (jnp.maximum(i - tp, 0), 0))
    return pl.pallas_call(
        functools.partial(_out_kernel, n_prompt_tiles=tp, tiles_per_seq=tiles_per_seq),
        grid=(tp + ts,),
        in_specs=[row_p(D_MODEL), row_s(D_MODEL), row_p(GLA_WIDTH), row_s(GLA_WIDTH),
                  row_p(SWA_WIDTH), row_s(SWA_WIDTH),
                  const(mod.shape),
                  const((1, D_MODEL)), const((GLA_WIDTH + SWA_WIDTH, D_MODEL)),
                  const((D_MODEL, D_FF)), const((D_FF, D_MODEL))],
        out_specs=[row_p(D_MODEL), row_s(D_MODEL)],
        out_shape=[jax.ShapeDtypeStruct((n_p, D_MODEL), F32), jax.ShapeDtypeStruct((n_s, D_MODEL), F32)],
        compiler_params=_cparams(1), name="out_proj_mlp",
    )(xp, xs, og_p, og_s, osw_p, osw_s, mod, gain, wo, w1, w2)


def _rope_tables(seq_len):
    axis_dim = HEAD_DIM // 2
    half = axis_dim // 2
    t = np.arange(seq_len)
    rowp = (t // GRID_W).astype(np.float64)
    colp = (t % GRID_W).astype(np.float64)
    freqs = ROPE_THETA ** (-np.arange(half, dtype=np.float64) / half)
    lane = np.arange(LANES)
    d = lane % HEAD_DIM
    posv = np.where((d // axis_dim)[None, :] == 0, rowp[:, None], colp[:, None])
    ang = posv * freqs[d % half][None, :]
    second = ((d % axis_dim) >= half)[None, :]
    cos, sin = np.cos(ang), np.sin(ang)
    zero = np.zeros_like(sin)
    tabs = (cos, np.where(second, sin, zero), np.where(second, zero, -sin))
    return tuple(jnp.asarray(tab.astype(np.float32)) for tab in tabs)


def kernel(x_prompt, x_sample, c, cache_k, cache_v, state_gla_fwd, state_gla_bwd, c_ctx, w_ada, b_ada, norm_mix, norm_ff, w_in, w_gk_fwd, b_gk_fwd, w_gk_bwd, b_gk_bwd, gla_norm, q_norm, k_norm, sink, w_o, w_ff1, w_ff2):
    depth = w_in.shape[0]
    assert depth == 1, "single trunk layer"
    bp, tp, _ = x_prompt.shape
    bs, ts, _ = x_sample.shape
    n_p = bp * tp
    l = 0

    mod = _modulation(c_ctx[None, :], c, w_ada[l], b_ada[l])

    norm2 = gla_norm[l][None, :]
    gain_mix = norm_mix[l][None, :]
    gain_ff = norm_ff[l][None, :]
    sink_l = sink[l]

    w_in_t = jnp.transpose(w_in[l])
    cache_k_t = jnp.swapaxes(cache_k, 3, 4)
    cache_v_t = jnp.swapaxes(cache_v, 3, 4)

    xp = x_prompt.reshape(n_p, D_MODEL)
    xs = x_sample.reshape(bs * ts, D_MODEL)
    gla_in, gates, swa_in, new_k_t, new_v_t = _project(
        xp, xs, mod, gain_mix, w_in_t, w_gk_fwd[l], w_gk_bwd[l], b_gk_fwd[l][None, :], b_gk_bwd[l][None, :],
        q_norm[l][None, :], k_norm[l][None, :], _rope_tables(ts), tp, ts)
    new_k = jnp.swapaxes(new_k_t, 3, 4)
    new_v = jnp.swapaxes(new_v_t, 3, 4)

    og_p, s_f, s_b, osw_p, wo, w1, w2 = _gla(
        gla_in, gates, norm2, None, None, bp, tp, 0, True, seqs=2,
        attention=(sink_l, swa_in), cast_weights=(w_o[l], w_ff1[l], w_ff2[l]))
    og_s, osw_s = _gla(gla_in, gates, norm2, state_gla_fwd, state_gla_bwd, bs, ts, n_p // ts, False,
                       attention=(sink_l, swa_in, cache_k_t, cache_v_t))

    yp, ys = _mix_out(xp, xs, og_p, og_s, osw_p, osw_s, mod, gain_ff, wo, w1, w2, ts)
    return (yp.reshape(bp, tp, D_MODEL), ys.reshape(bs, ts, D_MODEL), new_k, new_v, s_f, s_b)
```

```python
import functools

import numpy as np
import jax
import jax.numpy as jnp
from jax import lax
from jax.experimental import pallas as pl
from jax.experimental.pallas import tpu as pltpu

F32 = jnp.float32
BF16 = jnp.bfloat16

D_MODEL = 1024
GRID_W = 64
HEAD_DIM = 64
GLA_HEADS = 8
GLA_WIDTH = GLA_HEADS * HEAD_DIM
GATE_RANK = 16
GATE_TEMP = 16.0
CHUNK = 64
GROUP = 4 * CHUNK
SWA_HEADS = 8
SWA_KV_HEADS = 2
SWA_GROUP = SWA_HEADS // SWA_KV_HEADS
SWA_WIDTH = SWA_HEADS * HEAD_DIM
KV_WIDTH = SWA_KV_HEADS * HEAD_DIM
WINDOW = 128
QBLOCK = 128
D_FF = 4 * D_MODEL
ROPE_THETA = 10000.0
EPS = 1e-6
NEG_INF = -1e30
LANES = 128
TM = 512
FF_TILE = 1024

D_IN = 2848
R_GQ, R_GK, R_GV, R_GG, R_LR = 0, 512, 1024, 1536, 2048
R_SQ = R_LR + 2 * GATE_RANK
R_SV = R_SQ + SWA_WIDTH + KV_WIDTH
C_TAIL = R_LR
C_LR = C_TAIL + D_IN - R_SQ
W_COLS = C_LR + LANES

V7X_VMEM_BYTES = 64 * 1024 * 1024
VMEM_LIMIT = V7X_VMEM_BYTES - 8 * 1024 * 1024


def _cparams(n_axes):
    return pltpu.CompilerParams(dimension_semantics=("arbitrary",) * n_axes, vmem_limit_bytes=VMEM_LIMIT)


def _mod_kernel(cctx_ref, c_ref, w_ref, b_ref, o_ref):
    k = pl.program_id(0)
    pad = jnp.zeros((8 - 1 - c_ref.shape[0], c_ref.shape[1]), F32)
    cnd = jnp.concatenate([cctx_ref[...], c_ref[...], pad], axis=0)
    s = (cnd * jax.nn.sigmoid(cnd)).astype(BF16)
    part = _dot(s, w_ref[...].astype(BF16))

    @pl.when(k == 0)
    def _():
        o_ref[...] = part + b_ref[...]

    @pl.when(k > 0)
    def _():
        o_ref[...] += part


def _modulation(c_ctx, c, w_ada, b_ada):
    tk = 256
    n_out = w_ada.shape[1]
    return pl.pallas_call(
        _mod_kernel,
        grid=(D_MODEL // tk,),
        in_specs=[pl.BlockSpec((1, tk), lambda k: (0, k)),
                  pl.BlockSpec((c.shape[0], tk), lambda k: (0, k)),
                  pl.BlockSpec((tk, n_out), lambda k: (k, 0)),
                  pl.BlockSpec((1, n_out), lambda k: (0, 0))],
        out_specs=pl.BlockSpec((8, n_out), lambda k: (0, 0)),
        out_shape=jax.ShapeDtypeStruct((8, n_out), F32),
        compiler_params=_cparams(1),
        name="modulation",
    )(c_ctx, c, w_ada, b_ada.reshape(1, n_out))


def _mod_row(tile, n_prompt_tiles, tiles_per_seq):
    return jnp.where(tile < n_prompt_tiles, 0, 1 + (tile - n_prompt_tiles) // tiles_per_seq)


def _mod_chunk(mod_ref, row, j):
    return mod_ref[pl.ds(row, 1), j * D_MODEL:(j + 1) * D_MODEL]


def _adaln(x, gain, shift, scale):
    ms = jnp.mean(x * x, axis=-1, keepdims=True)
    xn = x * lax.rsqrt(ms + EPS)
    return xn * (gain * (1.0 + scale)) + shift


def _head_rms_norm(y, gain):
    cols = y.shape[1]
    lane = lax.broadcasted_iota(jnp.int32, (1, LANES), 1)
    first = lane < HEAD_DIM
    outs = []
    for p in range(cols // LANES):
        s = y[:, p * LANES:(p + 1) * LANES]
        s = s * s
        tot = jnp.sum(s, axis=-1, keepdims=True)
        lo = jnp.sum(jnp.where(first, s, 0.0), axis=-1, keepdims=True)
        outs.append(jnp.where(first, lo, tot - lo))
    sums = outs[0] if len(outs) == 1 else jnp.concatenate(outs, axis=-1)
    return y * lax.rsqrt(sums + HEAD_DIM * EPS) * (gain * (HEAD_DIM ** 0.5))


def _log_sigmoid(x):
    return jnp.minimum(x, 0.0) - jnp.log(1.0 + jnp.exp(-jnp.abs(x)))


def _dot(a, b):
    return jnp.dot(a, b, preferred_element_type=F32)


def _dot_nt(a, b):
    return lax.dot_general(a, b, (((1,), (1,)), ((), ())), preferred_element_type=F32)


def _proj_kernel(xp_ref, xs_ref, mod_ref, gain_ref, w_ref, wgf_ref, wgb_ref, bgf_ref, bgb_ref, qn_ref, kn_ref,
                 cos_ref, sin_a_ref, sin_b_ref,
                 gla_ref, gate_ref, swa_ref, nk_ref, nv_ref,
                 wb_ref, wg_ref, h0_ref, h1_ref, lr0_ref, lr1_ref,
                 *, n_prompt_tiles, n_tiles, tiles_per_seq, seqs_per_tile, seq_len):
    h_refs = (h0_ref, h1_ref)
    lr_refs = (lr0_ref, lr1_ref)
    s = pl.program_id(0)

    @pl.when(s == 0)
    def _():
        moves = ([(c * LANES, c * LANES) for c in range(R_LR // LANES)]
                 + [(R_SQ + c * LANES, C_TAIL + c * LANES) for c in range((D_IN - R_SQ) // LANES)]
                 + [(R_LR, C_LR)])
        for src, dst in moves:
            wb_ref[:, dst:dst + LANES] = w_ref[src:src + LANES, :].T.astype(BF16)
        wg_ref[...] = jnp.zeros(wg_ref.shape, BF16)
        wg_ref[0:GATE_RANK, 0:GLA_WIDTH] = wgf_ref[...].astype(BF16)
        wg_ref[GATE_RANK:2 * GATE_RANK, GLA_WIDTH:] = wgb_ref[...].astype(BF16)

    def normalise_next(parity):
        x = jnp.where(s < n_prompt_tiles, xp_ref[...], xs_ref[...])
        row = _mod_row(jnp.minimum(s, n_tiles - 1), n_prompt_tiles, tiles_per_seq)
        shift, scale = _mod_chunk(mod_ref, row, 0), _mod_chunk(mod_ref, row, 1)
        h_refs[parity][...] = _adaln(x, gain_ref[...], shift, scale).astype(BF16)

    def gates(parity):
        gate_pre = _dot(lr_refs[parity][...], wg_ref[...])
        gate_ref[:, :GLA_WIDTH] = _log_sigmoid(gate_pre[:, :GLA_WIDTH] + bgf_ref[...]) * (1.0 / GATE_TEMP)
        gate_ref[:, GLA_WIDTH:] = _log_sigmoid(gate_pre[:, GLA_WIDTH:] + bgb_ref[...]) * (1.0 / GATE_TEMP)

    def body(rope, emit_cache, parity):
        h_ref = h_refs[1 - parity]
        gates(parity)

        def proj(c0, c1):
            return _dot(h_ref[...], wb_ref[:, c0:c1])

        tail = proj(C_TAIL, C_LR)
        lr_refs[1 - parity][...] = proj(C_LR, W_COLS).astype(BF16)
        gg = proj(R_GG, R_LR)
        gla_ref[:, 3 * GLA_WIDTH:] = (gg * jax.nn.sigmoid(gg)).astype(BF16)
        attention_inputs(tail, rope, emit_cache)
        gla_ref[:, :GLA_WIDTH] = (proj(R_GQ, R_GK) * (HEAD_DIM ** -0.5)).astype(BF16)
        gla_ref[:, GLA_WIDTH:2 * GLA_WIDTH] = proj(R_GK, R_GV).astype(BF16)
        gla_ref[:, 2 * GLA_WIDTH:3 * GLA_WIDTH] = proj(R_GV, R_GG).astype(BF16)
        normalise_next(parity)

    def attention_inputs(tail, rope, emit_cache):
        qk = tail[:, :R_SV - R_SQ]
        qk_gain = jnp.concatenate([qn_ref[...]] * SWA_HEADS + [kn_ref[...]] * SWA_KV_HEADS, axis=-1)
        qk = _head_rms_norm(qk, qk_gain)
        if rope:
            reps = (SWA_WIDTH + KV_WIDTH) // LANES
            cos = jnp.concatenate([cos_ref[...]] * reps, axis=-1)
            sin_a = jnp.concatenate([sin_a_ref[...]] * reps, axis=-1)
            sin_b = jnp.concatenate([sin_b_ref[...]] * reps, axis=-1)
            width = qk.shape[1]
            quarter = HEAD_DIM // 4
            qk = (qk * cos + pltpu.roll(qk, quarter, 1) * sin_a
                  + pltpu.roll(qk, width - quarter, 1) * sin_b)
        swa_ref[:, :SWA_WIDTH] = (qk[:, :SWA_WIDTH] * (HEAD_DIM ** -0.5)).astype(BF16)
        sk = qk[:, SWA_WIDTH:]
        swa_ref[:, SWA_WIDTH:SWA_WIDTH + KV_WIDTH] = sk.astype(BF16)
        sv = tail[:, R_SV - R_SQ:]
        swa_ref[:, SWA_WIDTH + KV_WIDTH:] = sv.astype(BF16)
        if emit_cache:
            sk_t, sv_t = sk.T, sv.T
            for q in range(seqs_per_tile):
                cols = slice(q * seq_len, (q + 1) * seq_len)
                for g in range(SWA_KV_HEADS):
                    feat = slice(g * HEAD_DIM, (g + 1) * HEAD_DIM)
                    nk_ref[q, 0, g, :, :] = sk_t[feat, cols]
                    nv_ref[q, 0, g, :, :] = sv_t[feat, cols]

    @pl.when(s == 0)
    def _():
        normalise_next(0)
        lr_refs[1][...] = jnp.zeros(lr_refs[1].shape, BF16)

    for parity in range(2):
        on_parity = (s % 2) == parity

        @pl.when((s >= 1) & (s <= n_prompt_tiles) & on_parity)
        def _():
            body(rope=False, emit_cache=True, parity=parity)

        @pl.when((s > n_prompt_tiles) & (s <= n_tiles) & on_parity)
        def _():
            body(rope=True, emit_cache=False, parity=parity)

    @pl.when(s == n_tiles + 1)
    def _():
        gates((n_tiles + 1) % 2)


def _project(xp, xs, mod, gain, w_in, w_gk_f, w_gk_b, b_gk_f, b_gk_b, q_norm, k_norm, rope_tabs, seq_p, seq_s):
    n_p, n_s = xp.shape[0], xs.shape[0]
    tp, ts = n_p // TM, n_s // TM
    n = n_p + n_s
    seqs_per_tile = TM // seq_p
    tiles_per_seq = seq_s // TM
    n_seq_p = n_p // seq_p
    const = lambda shape: pl.BlockSpec(shape, lambda s: (0,) * len(shape))
    n_tiles = tp + ts
    out_tile = lambda s: jnp.clip(s - 1, 0, n_tiles - 1)
    gate_tile = lambda s: jnp.maximum(s - 2, 0)
    rope_spec = pl.BlockSpec((TM, LANES), lambda s: (jnp.maximum(out_tile(s) - tp, 0) % tiles_per_seq, 0))
    in_specs = [pl.BlockSpec((TM, D_MODEL), lambda s: (jnp.minimum(s, tp - 1), 0)),
                pl.BlockSpec((TM, D_MODEL), lambda s: (jnp.clip(s - tp, 0, ts - 1), 0)),
                const(mod.shape),
                const((1, D_MODEL)),
                pl.BlockSpec((D_IN, D_MODEL), lambda s: (0, 0), pipeline_mode=pl.Buffered(1)),
                const(w_gk_f.shape), const(w_gk_b.shape), const(b_gk_f.shape), const(b_gk_b.shape),
                const(q_norm.shape), const(k_norm.shape),
                rope_spec, rope_spec, rope_spec]
    row = lambda w: pl.BlockSpec((TM, w), lambda s: (out_tile(s), 0))
    cache_spec = pl.BlockSpec((seqs_per_tile, 1, SWA_KV_HEADS, HEAD_DIM, seq_p),
                              lambda s: (jnp.minimum(out_tile(s), tp - 1), 0, 0, 0, 0))
    widths = (4 * GLA_WIDTH, 2 * GLA_WIDTH, SWA_WIDTH + 2 * KV_WIDTH)
    out_specs = [row(widths[0]), pl.BlockSpec((TM, widths[1]), lambda s: (gate_tile(s), 0)), row(widths[2]),
                 cache_spec, cache_spec]
    cache_shape = (n_seq_p, 1, SWA_KV_HEADS, HEAD_DIM, seq_p)
    out_shape = ([jax.ShapeDtypeStruct((n, w), dt) for w, dt in zip(widths, (BF16, F32, BF16))]
                 + [jax.ShapeDtypeStruct(cache_shape, F32)] * 2)
    return pl.pallas_call(
        functools.partial(_proj_kernel, n_prompt_tiles=tp, n_tiles=n_tiles, tiles_per_seq=tiles_per_seq,
                          seqs_per_tile=seqs_per_tile, seq_len=seq_p),
        grid=(n_tiles + 2,), in_specs=in_specs, out_specs=out_specs, out_shape=out_shape,
        scratch_shapes=([pltpu.VMEM((D_MODEL, W_COLS), BF16), pltpu.VMEM((LANES, 2 * GLA_WIDTH), BF16)]
                        + [pltpu.VMEM((TM, D_MODEL), BF16)] * 2 + [pltpu.VMEM((TM, LANES), BF16)] * 2),
        compiler_params=_cparams(1), name="adaln_in_proj",
    )(xp, xs, mod, gain, w_in, w_gk_f, w_gk_b, b_gk_f, b_gk_b, q_norm, k_norm, *rope_tabs)


def _gla_kernel(*refs, seq_len, seqs, has_state, emit_state, with_attention, n_cast):
    (q_ref, k_ref, v_ref, gf_ref, gb_ref, sgg_ref, norm_ref,
     trif_ref, trib_ref, keepf_ref, keepb_ref) = refs[:11]
    pos = 11
    if has_state:
        s0f_ref, s0b_ref = refs[pos:pos + 2]
        pos += 2
    if with_attention:
        sink_ref, aq_ref, ak_ref, av_ref = refs[pos:pos + 4]
        pos += 4
    if with_attention == "latent":
        kc_ref, vc_ref = refs[pos:pos + 2]
        pos += 2
    cast_in = refs[pos:pos + n_cast]
    pos += n_cast
    og_ref = refs[pos]
    pos += 1
    if emit_state:
        sf_ref, sb_ref = refs[pos:pos + 2]
        pos += 2
    if with_attention:
        osw_ref = refs[pos]
        pos += 1
    cast_out = refs[pos:pos + n_cast]
    pos += n_cast
    accf_ref, accb_ref = refs[pos:pos + 2]

    for src, dst in zip(cast_in, cast_out):
        dst[...] = src[...].astype(BF16)

    n_groups = seq_len // GROUP
    per_group = GROUP // CHUNK
    n_pairs = q_ref.shape[1] // LANES
    lane = lax.broadcasted_iota(jnp.int32, (1, LANES), 1)
    head0 = lane < HEAD_DIM
    m0 = jnp.where(head0, 1.0, 0.0).astype(BF16)
    m1 = jnp.where(head0, 0.0, 1.0).astype(BF16)
    row_top = lax.broadcasted_iota(jnp.int32, (LANES, 1), 0) < HEAD_DIM

    def stack_heads(x):
        return jnp.concatenate([x * m0, x * m1], axis=0)

    def direction(row0, lanes, g_ref, tri_ref, keep_ref, state, acc_ref, backward):
        rows = pl.ds(row0, GROUP)
        g = g_ref[rows, lanes]
        g_hi = g.astype(BF16)
        g_lo = (g - g_hi.astype(F32)).astype(BF16)
        r = _dot(tri_ref[...], jnp.concatenate([g_hi, g_lo], axis=-1))
        yield
        b = (r[:, :LANES] + r[:, LANES:]).reshape(per_group, CHUNK, LANES)
        edge = 0 if backward else CHUNK - 1
        mid = CHUNK // 2
        b_edge = b[:, edge:edge + 1, :]
        b_mid = b[:, mid:mid + 1, :]
        q = q_ref[rows, lanes].astype(F32).reshape(per_group, CHUNK, LANES)
        k = k_ref[rows, lanes].astype(F32).reshape(per_group, CHUNK, LANES)
        v = v_ref[rows, lanes]
        q_mid = q * jnp.exp(b - b_mid)
        k_mid = k * jnp.exp(b_mid - b)
        q_in = q_mid.astype(BF16)
        k_in = k_mid.astype(BF16).reshape(GROUP, LANES)
        q_st = (q_mid * jnp.exp(b_mid)).astype(BF16)
        k_st = (k_mid * jnp.exp(b_edge - b_mid)).astype(BF16).reshape(GROUP, LANES)
        decay = jnp.exp(b_edge)
        keep = keep_ref[...]
        pair_rows = 2 * CHUNK
        n_cp = per_group // 2
        yield
        scores, kvs = [], []
        for p in range(n_cp):
            c0 = 2 * p
            prow = slice(p * pair_rows, (p + 1) * pair_rows)
            q_pair = jnp.concatenate([stack_heads(q_in[c0]), stack_heads(q_in[c0 + 1])], axis=0)
            scores.append(_dot_nt(q_pair, k_in[prow, :]).astype(BF16) * keep)
            v_t = v[prow, :].astype(F32).T.astype(BF16)
            k_pair = k_st[prow, :]
            zero = jnp.zeros_like(k_pair)
            k_blk = jnp.concatenate([jnp.where(row_top, k_pair, zero), jnp.where(row_top, zero, k_pair)], axis=-1)
            kvs.append(_dot(v_t, k_blk))
        yield
        intras = [_dot(scores[p], v[p * pair_rows:(p + 1) * pair_rows, :]) for p in range(n_cp)]
        yield
        for p in (range(n_cp - 1, -1, -1) if backward else range(n_cp)):
            for j in ((1, 0) if backward else (0, 1)):
                c = 2 * p + j
                inter = _dot_nt(stack_heads(q_st[c]), state.astype(BF16))
                tot = intras[p][j * pair_rows:(j + 1) * pair_rows, :] + inter
                acc_ref[pl.ds(row0 + c * CHUNK, CHUNK), lanes] = jnp.where(head0, tot[:CHUNK, :], tot[CHUNK:, :])
                state = state * decay[c] + kvs[p][:, j * LANES:(j + 1) * LANES]
            yield
        return state

    def load_state(s_ref, q, p):
        z = jnp.zeros((HEAD_DIM, HEAD_DIM), F32)
        top = jnp.concatenate([s_ref[q, 0, 2 * p, :, :].T, z], axis=-1)
        bot = jnp.concatenate([z, s_ref[q, 0, 2 * p + 1, :, :].T], axis=-1)
        return jnp.concatenate([top, bot], axis=0)

    per_dir = seqs * n_pairs
    if has_state:
        init = tuple(load_state(s_ref, q, p)
                     for s_ref in (s0f_ref, s0b_ref) for q in range(seqs) for p in range(n_pairs))
    else:
        init = (jnp.zeros((LANES, LANES), F32),) * (2 * per_dir)

    def body(t, carry):
        row_f = t * GROUP
        row_b = (n_groups - 1 - t) * GROUP
        units = []
        for backward, row, g_ref, tri_ref, keep_ref, acc_ref in (
                (False, row_f, gf_ref, trif_ref, keepf_ref, accf_ref),
                (True, row_b, gb_ref, trib_ref, keepb_ref, accb_ref)):
            for q in range(seqs):
                for p in range(n_pairs):
                    lanes = slice(p * LANES, (p + 1) * LANES)
                    state = carry[backward * per_dir + q * n_pairs + p]
                    units.append(direction(q * seq_len + row, lanes, g_ref, tri_ref, keep_ref, state, acc_ref,
                                           backward))
        return units

    norm_pair = jnp.concatenate([norm_ref[...]] * 2, axis=-1)

    def finish_outputs(rows):
        for p in range(n_pairs):
            lanes = slice(p * LANES, (p + 1) * LANES)
            o = _head_rms_norm(accf_ref[rows, lanes] + accb_ref[rows, lanes], norm_pair)
            og_ref[rows, lanes] = (o * sgg_ref[rows, lanes].astype(F32)).astype(BF16)
            yield

    if n_groups == 1:
        gla_units = body(0, init)
        final = [None] * len(gla_units)
        for q in range(seqs + 1):
            rows = slice(q * seq_len, (q + 1) * seq_len)
            wave, ids = [], []
            if q < seqs:
                ids = [d * per_dir + q * n_pairs + p for d in range(2) for p in range(n_pairs)]
                wave = [gla_units[u] for u in ids]
                if with_attention:
                    wave += _context_attention_units(sink_ref, aq_ref, ak_ref, av_ref, rows)
            if q > 0:
                wave.append(finish_outputs(slice((q - 1) * seq_len, q * seq_len)))
            results = _run_staged(wave)
            for u, r in zip(ids, results):
                final[u] = r
            if q < seqs and with_attention:
                heads = [o for outs in results[len(ids):len(ids) + SWA_KV_HEADS] for o in outs]
                osw_ref[rows, :] = jnp.concatenate(heads, axis=-1).astype(BF16)
    else:
        assert with_attention != "context" and seqs == 1
        if with_attention:
            ctx_kv = [(kc_ref[0, 0, g, :, :].astype(BF16), vc_ref[0, 0, g, :, :].astype(BF16))
                      for g in range(SWA_KV_HEADS)]
            q_blocks = GROUP // QBLOCK

        def done_after(t):
            return [g for g in range(n_groups) if max(g, n_groups - 1 - g) == t]

        def loop_body(t, carry):
            units = body(t, carry)
            n_gla = len(units)
            if with_attention:
                for j in range(q_blocks):
                    units += _latent_attention_units(sink_ref, aq_ref, ak_ref, av_ref, ctx_kv,
                                                     t * q_blocks + j, seq_len)
            units += [finish_outputs(slice(g * GROUP, (g + 1) * GROUP)) for g in done_after(t - 1)]
            results = _run_staged(units)
            if with_attention:
                for j in range(q_blocks):
                    heads = [o for outs in results[n_gla + j * SWA_KV_HEADS:n_gla + (j + 1) * SWA_KV_HEADS]
                             for o in outs]
                    rows = pl.ds((t * q_blocks + j) * QBLOCK, QBLOCK)
                    osw_ref[rows, :] = jnp.concatenate(heads, axis=-1).astype(BF16)
            return tuple(results[:n_gla])

        final = init
        for t in range(n_groups):
            final = loop_body(t, final)
        _run_staged([finish_outputs(slice(g * GROUP, (g + 1) * GROUP)) for g in done_after(n_groups - 1)])
    if emit_state:
        for d, s_ref in enumerate((sf_ref, sb_ref)):
            for q in range(seqs):
                for p in range(n_pairs):
                    fin = final[d * per_dir + q * n_pairs + p]
                    s_ref[q, 0, 2 * p, :, :] = fin[:HEAD_DIM, :HEAD_DIM].T
                    s_ref[q, 0, 2 * p + 1, :, :] = fin[HEAD_DIM:, HEAD_DIM:].T


def _gla_constants():
    i = np.arange(GROUP)[:, None]
    j = np.arange(GROUP)[None, :]
    same = (i // CHUNK) == (j // CHUNK)
    tri_f = same & (j <= i)
    tri_b = same & (j >= i)
    r = np.arange(4 * CHUNK)[:, None]
    l = np.arange(LANES)[None, :]
    same = (r // (2 * CHUNK)) == (l // CHUNK)
    keep_f = same & ((l % CHUNK) <= (r % CHUNK))
    keep_b = same & ((l % CHUNK) >= (r % CHUNK))
    return tuple(jnp.asarray(m.astype(np.float32), dtype=BF16) for m in (tri_f, tri_b, keep_f, keep_b))


def _gla(gla_in, gates, norm_row, s0f, s0b, n_seq, seq_len, row_block0, emit_state, seqs=1,
         attention=None, cast_weights=()):
    has_state = s0f is not None
    steps = n_seq // seqs
    rows_per_step = seqs * seq_len
    row_blk = row_block0 // seqs
    col = lambda width, j: pl.BlockSpec((rows_per_step, width), lambda b: (row_blk + b, j))
    out_blk = pl.BlockSpec((rows_per_step, GLA_WIDTH), lambda b: (b, 0))
    st_spec = pl.BlockSpec((seqs, 1, GLA_HEADS, HEAD_DIM, HEAD_DIM), lambda b: (b, 0, 0, 0, 0))
    const = lambda shape: pl.BlockSpec(shape, lambda b: (0, 0))
    in_specs = ([col(GLA_WIDTH, 0), col(GLA_WIDTH, 1), col(GLA_WIDTH, 2), col(GLA_WIDTH, 0), col(GLA_WIDTH, 1),
                 col(GLA_WIDTH, 3)]
                + [const(norm_row.shape)] + [const((GROUP, GROUP))] * 2 + [const((4 * CHUNK, LANES))] * 2)
    args = [gla_in, gla_in, gla_in, gates, gates, gla_in, norm_row, *_gla_constants()]
    if has_state:
        in_specs += [st_spec] * 2
        args += [s0f, s0b]
    out_specs = [out_blk]
    out_shape = [jax.ShapeDtypeStruct((n_seq * seq_len, GLA_WIDTH), BF16)]
    if emit_state:
        out_specs += [st_spec] * 2
        out_shape += [jax.ShapeDtypeStruct((n_seq, 1, GLA_HEADS, HEAD_DIM, HEAD_DIM), F32)] * 2
    with_attention = None
    if attention is not None:
        sink, swa_in = attention[:2]
        with_attention = "context"
        in_specs += [pl.BlockSpec(memory_space=pltpu.SMEM), col(SWA_WIDTH, 0),
                     col(KV_WIDTH, SWA_WIDTH // KV_WIDTH), col(KV_WIDTH, SWA_WIDTH // KV_WIDTH + 1)]
        args += [sink, swa_in, swa_in, swa_in]
        if len(attention) > 2:
            with_attention = "latent"
            cache_k_t, cache_v_t = attention[2:]
            past = cache_k_t.shape[4]
            cache_spec = pl.BlockSpec((seqs, 1, SWA_KV_HEADS, HEAD_DIM, past), lambda b: (b, 0, 0, 0, 0))
            in_specs += [cache_spec] * 2
            args += [cache_k_t, cache_v_t]
        out_specs.append(pl.BlockSpec((rows_per_step, SWA_WIDTH), lambda b: (b, 0)))
        out_shape.append(jax.ShapeDtypeStruct((n_seq * seq_len, SWA_WIDTH), BF16))
    cast_specs, cast_shapes = _cast_specs(cast_weights, steps)
    in_specs += cast_specs
    args += list(cast_weights)
    out_specs += cast_specs
    out_shape += cast_shapes
    return pl.pallas_call(
        functools.partial(_gla_kernel, seq_len=seq_len, seqs=seqs, has_state=has_state, emit_state=emit_state,
                          with_attention=with_attention, n_cast=len(cast_weights)),
        grid=(steps,), in_specs=in_specs, out_specs=out_specs, out_shape=out_shape,
        scratch_shapes=[pltpu.VMEM((rows_per_step, GLA_WIDTH), F32)] * 2,
        compiler_params=_cparams(1), name="gla_bidirectional",
    )(*args)


def _group_attention(q, heads, sink_ref, score_fn, value_fn):
    rows = q.shape[0]
    q_stack = jnp.concatenate([q[:, h * HEAD_DIM:(h + 1) * HEAD_DIM] for h in heads], axis=0)
    scores = score_fn(q_stack)
    yield
    slabs = [slice(j * rows, (j + 1) * rows) for j in range(len(heads))]
    ms = []
    for slab, h in zip(slabs, heads):
        m = sink_ref[h]
        for s in scores:
            m = jnp.maximum(m, jnp.max(s[slab, :], axis=-1, keepdims=True))
        ms.append(m)
    yield
    e_parts = [[] for _ in scores]
    denoms = []
    for slab, h, m in zip(slabs, heads, ms):
        denom = jnp.exp(sink_ref[h] - m)
        for part, s in zip(e_parts, scores):
            e = jnp.exp(s[slab, :] - m)
            denom = denom + jnp.sum(e, axis=-1, keepdims=True)
            part.append(e.astype(BF16))
        denoms.append(denom)
    yield
    acc = value_fn([jnp.concatenate(part, axis=0) for part in e_parts])
    yield
    return [acc[slab, :] / denom for slab, denom in zip(slabs, denoms)]


def _run_staged(units):
    results = [None] * len(units)
    live = list(range(len(units)))
    while live:
        still = []
        for u in live:
            try:
                next(units[u])
                still.append(u)
            except StopIteration as stop:
                results[u] = stop.value
        live = still
    return results


def _context_attention_units(sink_ref, q_ref, k_ref, v_ref, rows):
    q = q_ref[rows, :]
    k = k_ref[rows, :]
    v = v_ref[rows, :]
    units = []
    for g in range(SWA_KV_HEADS):
        k_g = k[:, g * HEAD_DIM:(g + 1) * HEAD_DIM]
        v_g = v[:, g * HEAD_DIM:(g + 1) * HEAD_DIM]
        units.append(_group_attention(
            q, list(range(SWA_GROUP * g, SWA_GROUP * (g + 1))), sink_ref,
            lambda qs, k_g=k_g: [_dot_nt(qs, k_g)],
            lambda es, v_g=v_g: _dot(es[0], v_g)))
    return units


def _cast_specs(cast_weights, steps, step_of=lambda *idx: idx[0]):
    specs = [pl.BlockSpec((w.shape[0] // steps, w.shape[1]), lambda *idx: (step_of(*idx), 0))
             for w in cast_weights]
    shapes = [jax.ShapeDtypeStruct(w.shape, BF16) for w in cast_weights]
    return specs, shapes


def _latent_attention_units(sink_ref, q_ref, k_ref, v_ref, ctx_kv, i, seq_len):
    span = QBLOCK + 2 * WINDOW
    start = min(max(i * QBLOCK - WINDOW, 0), seq_len - span)
    q = q_ref[pl.ds(i * QBLOCK, QBLOCK), :]
    k = k_ref[pl.ds(start, span), :]
    v = v_ref[pl.ds(start, span), :]
    qpos = i * QBLOCK + lax.broadcasted_iota(jnp.int32, (QBLOCK, span), 0)
    kpos = start + lax.broadcasted_iota(jnp.int32, (QBLOCK, span), 1)
    valid = jnp.abs(qpos - kpos) <= WINDOW
    valid = jnp.concatenate([valid] * SWA_GROUP, axis=0)
    units = []
    for g in range(SWA_KV_HEADS):
        k_g = k[:, g * HEAD_DIM:(g + 1) * HEAD_DIM]
        v_g = v[:, g * HEAD_DIM:(g + 1) * HEAD_DIM]
        kc_t, vc_t = ctx_kv[g]
        units.append(_group_attention(
            q, list(range(SWA_GROUP * g, SWA_GROUP * (g + 1))), sink_ref,
            lambda qs, k_g=k_g, kc_t=kc_t: [jnp.where(valid, _dot_nt(qs, k_g), NEG_INF), _dot(qs, kc_t)],
            lambda es, v_g=v_g, vc_t=vc_t: _dot(es[0], v_g) + _dot_nt(es[1], vc_t)))
    return units


def _out_kernel(xp_ref, xs_ref, ogp_ref, ogs_ref, oswp_ref, osws_ref, mod_ref, gain_ref,
                wo_ref, w1_ref, w2_ref, yp_ref, ys_ref, *, n_prompt_tiles, tiles_per_seq):
    i = pl.program_id(0)
    row = _mod_row(i, n_prompt_tiles, tiles_per_seq)

    def body(x_ref, og_ref, osw_ref, y_ref):
        mix = _dot(og_ref[...], wo_ref[:GLA_WIDTH, :]) + _dot(osw_ref[...], wo_ref[GLA_WIDTH:, :])
        x1 = x_ref[...] + _mod_chunk(mod_ref, row, 2) * mix
        h = _adaln(x1, gain_ref[...], _mod_chunk(mod_ref, row, 3), _mod_chunk(mod_ref, row, 4)).astype(BF16)
        acc = None
        for j in range(D_FF // FF_TILE):
            cols = slice(j * FF_TILE, (j + 1) * FF_TILE)
            a = jnp.maximum(_dot(h, w1_ref[:, cols]), 0.0)
            part = _dot((a * a).astype(BF16), w2_ref[cols, :])
            acc = part if acc is None else acc + part
        y_ref[...] = x1 + _mod_chunk(mod_ref, row, 5) * acc

    @pl.when(i < n_prompt_tiles)
    def _():
        body(xp_ref, ogp_ref, oswp_ref, yp_ref)

    @pl.when(i >= n_prompt_tiles)
    def _():
        body(xs_ref, ogs_ref, osws_ref, ys_ref)


def _mix_out(xp, xs, og_p, og_s, osw_p, osw_s, mod, gain, wo, w1, w2, seq_s):
    n_p, n_s = xp.shape[0], xs.shape[0]
    tp, ts = n_p // TM, n_s // TM
    tiles_per_seq = seq_s // TM
    const = lambda shape: pl.BlockSpec(shape, lambda i: (0,) * len(shape), pipeline_mode=pl.Buffered(1))
    row_p = lambda w: pl.BlockSpec((TM, w), lambda i: (jnp.minimum(i, tp - 1), 0))
    row_s = lambda w: pl.BlockSpec((TM, w), lambda i: (jnp.maximum(i - tp, 0), 0))
    return pl.pallas_call(
        functools.partial(_out_kernel, n_prompt_tiles=tp, tiles_per_seq=tiles_per_seq),
        grid=(tp + ts,),
        in_specs=[row_p(D_MODEL), row_s(D_MODEL), row_p(GLA_WIDTH), row_s(GLA_WIDTH),
                  row_p(SWA_WIDTH), row_s(SWA_WIDTH),
                  const(mod.shape),
                  const((1, D_MODEL)), const((GLA_WIDTH + SWA_WIDTH, D_MODEL)),
                  const((D_MODEL, D_FF)), const((D_FF, D_MODEL))],
        out_specs=[row_p(D_MODEL), row_s(D_MODEL)],
        out_shape=[jax.ShapeDtypeStruct((n_p, D_MODEL), F32), jax.ShapeDtypeStruct((n_s, D_MODEL), F32)],
        compiler_params=_cparams(1), name="out_proj_mlp",
    )(xp, xs, og_p, og_s, osw_p, osw_s, mod, gain, wo, w1, w2)


def _rope_tables(seq_len):
    axis_dim = HEAD_DIM // 2
    half = axis_dim // 2
    t = np.arange(seq_len)
    rowp = (t // GRID_W).astype(np.float64)
    colp = (t % GRID_W).astype(np.float64)
    freqs = ROPE_THETA ** (-np.arange(half, dtype=np.float64) / half)
    lane = np.arange(LANES)
    d = lane % HEAD_DIM
    posv = np.where((d // axis_dim)[None, :] == 0, rowp[:, None], colp[:, None])
    ang = posv * freqs[d % half][None, :]
    second = ((d % axis_dim) >= half)[None, :]
    cos, sin = np.cos(ang), np.sin(ang)
    zero = np.zeros_like(sin)
    tabs = (cos, np.where(second, sin, zero), np.where(second, zero, -sin))
    return tuple(jnp.asarray(tab.astype(np.float32)) for tab in tabs)


def kernel(x_prompt, x_sample, c, cache_k, cache_v, state_gla_fwd, state_gla_bwd, c_ctx, w_ada, b_ada, norm_mix, norm_ff, w_in, w_gk_fwd, b_gk_fwd, w_gk_bwd, b_gk_bwd, gla_norm, q_norm, k_norm, sink, w_o, w_ff1, w_ff2):
    depth = w_in.shape[0]
    assert depth == 1, "single trunk layer"
    bp, tp, _ = x_prompt.shape
    bs, ts, _ = x_sample.shape
    n_p = bp * tp
    l = 0

    mod = _modulation(c_ctx[None, :], c, w_ada[l], b_ada[l])

    norm2 = gla_norm[l][None, :]
    gain_mix = norm_mix[l][None, :]
    gain_ff = norm_ff[l][None, :]
    sink_l = sink[l]

    w_in_t = jnp.transpose(w_in[l])
    cache_k_t = jnp.swapaxes(cache_k, 3, 4)
    cache_v_t = jnp.swapaxes(cache_v, 3, 4)

    xp = x_prompt.reshape(n_p, D_MODEL)
    xs = x_sample.reshape(bs * ts, D_MODEL)
    gla_in, gates, swa_in, new_k_t, new_v_t = _project(
        xp, xs, mod, gain_mix, w_in_t, w_gk_fwd[l], w_gk_bwd[l], b_gk_fwd[l][None, :], b_gk_bwd[l][None, :],
        q_norm[l][None, :], k_norm[l][None, :], _rope_tables(ts), tp, ts)
    new_k = jnp.swapaxes(new_k_t, 3, 4)
    new_v = jnp.swapaxes(new_v_t, 3, 4)

    og_p, s_f, s_b, osw_p, wo, w1, w2 = _gla(
        gla_in, gates, norm2, None, None, bp, tp, 0, True, seqs=2,
        attention=(sink_l, swa_in), cast_weights=(w_o[l], w_ff1[l], w_ff2[l]))
    og_s, osw_s = _gla(gla_in, gates, norm2, state_gla_fwd, state_gla_bwd, bs, ts, n_p // ts, False,
                       attention=(sink_l, swa_in, cache_k_t, cache_v_t))

    yp, ys = _mix_out(xp, xs, og_p, og_s, osw_p, osw_s, mod, gain_ff, wo, w1, w2, ts)
    return (yp.reshape(bp, tp, D_MODEL), ys.reshape(bs, ts, D_MODEL), new_k, new_v, s_f, s_b)
```

```python
import functools

import numpy as np
import jax
import jax.numpy as jnp
from jax import lax
from jax.experimental import pallas as pl
from jax.experimental.pallas import tpu as pltpu

F32 = jnp.float32
BF16 = jnp.bfloat16

D_MODEL = 1024
GRID_W = 64
HEAD_DIM = 64
GLA_HEADS = 8
GLA_WIDTH = GLA_HEADS * HEAD_DIM
GATE_RANK = 16
GATE_TEMP = 16.0
CHUNK = 64
GROUP = 4 * CHUNK
SWA_HEADS = 8
SWA_KV_HEADS = 2
SWA_GROUP = SWA_HEADS // SWA_KV_HEADS
SWA_WIDTH = SWA_HEADS * HEAD_DIM
KV_WIDTH = SWA_KV_HEADS * HEAD_DIM
WINDOW = 128
QBLOCK = 128
D_FF = 4 * D_MODEL
ROPE_THETA = 10000.0
EPS = 1e-6
NEG_INF = -1e30
LANES = 128
TM = 512
FF_TILE = 1024

D_IN = 2848
R_GQ, R_GK, R_GV, R_GG, R_LR = 0, 512, 1024, 1536, 2048
R_SQ = R_LR + 2 * GATE_RANK
R_SV = R_SQ + SWA_WIDTH + KV_WIDTH
C_TAIL = R_LR
C_LR = C_TAIL + D_IN - R_SQ
W_COLS = C_LR + LANES

V7X_VMEM_BYTES = 64 * 1024 * 1024
VMEM_LIMIT = V7X_VMEM_BYTES - 8 * 1024 * 1024


def _cparams(n_axes):
    return pltpu.CompilerParams(dimension_semantics=("arbitrary",) * n_axes, vmem_limit_bytes=VMEM_LIMIT)


def _mod_kernel(cctx_ref, c_ref, w_ref, b_ref, o_ref):
    k = pl.program_id(0)
    pad = jnp.zeros((8 - 1 - c_ref.shape[0], c_ref.shape[1]), F32)
    cnd = jnp.concatenate([cctx_ref[...], c_ref[...], pad], axis=0)
    s = (cnd * jax.nn.sigmoid(cnd)).astype(BF16)
    part = _dot(s, w_ref[...].astype(BF16))

    @pl.when(k == 0)
    def _():
        o_ref[...] = part + b_ref[...]

    @pl.when(k > 0)
    def _():
        o_ref[...] += part


def _modulation(c_ctx, c, w_ada, b_ada):
    tk = 256
    n_out = w_ada.shape[1]
    return pl.pallas_call(
        _mod_kernel,
        grid=(D_MODEL // tk,),
        in_specs=[pl.BlockSpec((1, tk), lambda k: (0, k)),
                  pl.BlockSpec((c.shape[0], tk), lambda k: (0, k)),
                  pl.BlockSpec((tk, n_out), lambda k: (k, 0)),
                  pl.BlockSpec((1, n_out), lambda k: (0, 0))],
        out_specs=pl.BlockSpec((8, n_out), lambda k: (0, 0)),
        out_shape=jax.ShapeDtypeStruct((8, n_out), F32),
        compiler_params=_cparams(1),
        name="modulation",
    )(c_ctx, c, w_ada, b_ada.reshape(1, n_out))


def _mod_row(tile, n_prompt_tiles, tiles_per_seq):
    return jnp.where(tile < n_prompt_tiles, 0, 1 + (tile - n_prompt_tiles) // tiles_per_seq)


def _mod_chunk(mod_ref, row, j):
    return mod_ref[pl.ds(row, 1), j * D_MODEL:(j + 1) * D_MODEL]


def _adaln(x, gain, shift, scale):
    ms = jnp.mean(x * x, axis=-1, keepdims=True)
    xn = x * lax.rsqrt(ms + EPS)
    return xn * (gain * (1.0 + scale)) + shift


def _head_rms_norm(y, gain):
    cols = y.shape[1]
    lane = lax.broadcasted_iota(jnp.int32, (1, LANES), 1)
    first = lane < HEAD_DIM
    outs = []
    for p in range(cols // LANES):
        s = y[:, p * LANES:(p + 1) * LANES]
        s = s * s
        tot = jnp.sum(s, axis=-1, keepdims=True)
        lo = jnp.sum(jnp.where(first, s, 0.0), axis=-1, keepdims=True)
        outs.append(jnp.where(first, lo, tot - lo))
    sums = outs[0] if len(outs) == 1 else jnp.concatenate(outs, axis=-1)
    return y * lax.rsqrt(sums + HEAD_DIM * EPS) * (gain * (HEAD_DIM ** 0.5))


def _log_sigmoid(x):
    return jnp.minimum(x, 0.0) - jnp.log(1.0 + jnp.exp(-jnp.abs(x)))


def _dot(a, b):
    return jnp.dot(a, b, preferred_element_type=F32)


def _dot_nt(a, b):
    return lax.dot_general(a, b, (((1,), (1,)), ((), ())), preferred_element_type=F32)


def _proj_kernel(xp_ref, xs_ref, mod_ref, gain_ref, w_ref, wgf_ref, wgb_ref, bgf_ref, bgb_ref, qn_ref, kn_ref,
                 cos_ref, sin_a_ref, sin_b_ref,
                 gla_ref, gate_ref, swa_ref, nk_ref, nv_ref,
                 wb_ref, wg_ref, h0_ref, h1_ref, lr0_ref, lr1_ref,
                 *, n_prompt_tiles, n_tiles, tiles_per_seq, seqs_per_tile, seq_len):
    h_refs = (h0_ref, h1_ref)
    lr_refs = (lr0_ref, lr1_ref)
    s = pl.program_id(0)

    @pl.when(s == 0)
    def _():
        moves = ([(c * LANES, c * LANES) for c in range(R_LR // LANES)]
                 + [(R_SQ + c * LANES, C_TAIL + c * LANES) for c in range((D_IN - R_SQ) // LANES)]
                 + [(R_LR, C_LR)])
        for src, dst in moves:
            wb_ref[:, dst:dst + LANES] = w_ref[src:src + LANES, :].T.astype(BF16)
        wg_ref[...] = jnp.zeros(wg_ref.shape, BF16)
        wg_ref[0:GATE_RANK, 0:GLA_WIDTH] = wgf_ref[...].astype(BF16)
        wg_ref[GATE_RANK:2 * GATE_RANK, GLA_WIDTH:] = wgb_ref[...].astype(BF16)

    def normalise_next(parity):
        x = jnp.where(s < n_prompt_tiles, xp_ref[...], xs_ref[...])
        row = _mod_row(jnp.minimum(s, n_tiles - 1), n_prompt_tiles, tiles_per_seq)
        shift, scale = _mod_chunk(mod_ref, row, 0), _mod_chunk(mod_ref, row, 1)
        h_refs[parity][...] = _adaln(x, gain_ref[...], shift, scale).astype(BF16)

    def gates(parity):
        gate_pre = _dot(lr_refs[parity][...], wg_ref[...])
        gate_ref[:, :GLA_WIDTH] = _log_sigmoid(gate_pre[:, :GLA_WIDTH] + bgf_ref[...]) * (1.0 / GATE_TEMP)
        gate_ref[:, GLA_WIDTH:] = _log_sigmoid(gate_pre[:, GLA_WIDTH:] + bgb_ref[...]) * (1.0 / GATE_TEMP)

    def body(rope, emit_cache, parity):
        h_ref = h_refs[1 - parity]
        gates(parity)

        def proj(c0, c1):
            return _dot(h_ref[...], wb_ref[:, c0:c1])

        tail = proj(C_TAIL, C_LR)
        lr_refs[1 - parity][...] = proj(C_LR, W_COLS).astype(BF16)
        gg = proj(R_GG, R_LR)
        gla_ref[:, 3 * GLA_WIDTH:] = (gg * jax.nn.sigmoid(gg)).astype(BF16)
        attention_inputs(tail, rope, emit_cache)
        gla_ref[:, :GLA_WIDTH] = (proj(R_GQ, R_GK) * (HEAD_DIM ** -0.5)).astype(BF16)
        gla_ref[:, GLA_WIDTH:2 * GLA_WIDTH] = proj(R_GK, R_GV).astype(BF16)
        gla_ref[:, 2 * GLA_WIDTH:3 * GLA_WIDTH] = proj(R_GV, R_GG).astype(BF16)
        normalise_next(parity)

    def attention_inputs(tail, rope, emit_cache):
        qk = tail[:, :R_SV - R_SQ]
        qk_gain = jnp.concatenate([qn_ref[...]] * SWA_HEADS + [kn_ref[...]] * SWA_KV_HEADS, axis=-1)
        qk = _head_rms_norm(qk, qk_gain)
        if rope:
            reps = (SWA_WIDTH + KV_WIDTH) // LANES
            cos = jnp.concatenate([cos_ref[...]] * reps, axis=-1)
            sin_a = jnp.concatenate([sin_a_ref[...]] * reps, axis=-1)
            sin_b = jnp.concatenate([sin_b_ref[...]] * reps, axis=-1)
            width = qk.shape[1]
            quarter = HEAD_DIM // 4
            qk = (qk * cos + pltpu.roll(qk, quarter, 1) * sin_a
                  + pltpu.roll(qk, width - quarter, 1) * sin_b)
        swa_ref[:, :SWA_WIDTH] = (qk[:, :SWA_WIDTH] * (HEAD_DIM ** -0.5)).astype(BF16)
        sk = qk[:, SWA_WIDTH:]
        swa_ref[:, SWA_WIDTH:SWA_WIDTH + KV_WIDTH] = sk.astype(BF16)
        sv = tail[:, R_SV - R_SQ:]
        swa_ref[:, SWA_WIDTH + KV_WIDTH:] = sv.astype(BF16)
        if emit_cache:
            sk_t, sv_t = sk.T, sv.T
            for q in range(seqs_per_tile):
                cols = slice(q * seq_len, (q + 1) * seq_len)
                for g in range(SWA_KV_HEADS):
                    feat = slice(g * HEAD_DIM, (g + 1) * HEAD_DIM)
                    nk_ref[q, 0, g, :, :] = sk_t[feat, cols]
                    nv_ref[q, 0, g, :, :] = sv_t[feat, cols]

    @pl.when(s == 0)
    def _():
        normalise_next(0)
        lr_refs[1][...] = jnp.zeros(lr_refs[1].shape, BF16)

    for parity in range(2):
        on_parity = (s % 2) == parity

        @pl.when((s >= 1) & (s <= n_prompt_tiles) & on_parity)
        def _():
            body(rope=False, emit_cache=True, parity=parity)

        @pl.when((s > n_prompt_tiles) & (s <= n_tiles) & on_parity)
        def _():
            body(rope=True, emit_cache=False, parity=parity)

    @pl.when(s == n_tiles + 1)
    def _():
        gates((n_tiles + 1) % 2)


def _project(xp, xs, mod, gain, w_in, w_gk_f, w_gk_b, b_gk_f, b_gk_b, q_norm, k_norm, rope_tabs, seq_p, seq_s):
    n_p, n_s = xp.shape[0], xs.shape[0]
    tp, ts = n_p // TM, n_s // TM
    n = n_p + n_s
    seqs_per_tile = TM // seq_p
    tiles_per_seq = seq_s // TM
    n_seq_p = n_p // seq_p
    const = lambda shape: pl.BlockSpec(shape, lambda s: (0,) * len(shape))
    n_tiles = tp + ts
    out_tile = lambda s: jnp.clip(s - 1, 0, n_tiles - 1)
    gate_tile = lambda s: jnp.maximum(s - 2, 0)
    rope_spec = pl.BlockSpec((TM, LANES), lambda s: (jnp.maximum(out_tile(s) - tp, 0) % tiles_per_seq, 0))
    in_specs = [pl.BlockSpec((TM, D_MODEL), lambda s: (jnp.minimum(s, tp - 1), 0)),
                pl.BlockSpec((TM, D_MODEL), lambda s: (jnp.clip(s - tp, 0, ts - 1), 0)),
                const(mod.shape),
                const((1, D_MODEL)),
                pl.BlockSpec((D_IN, D_MODEL), lambda s: (0, 0), pipeline_mode=pl.Buffered(1)),
                const(w_gk_f.shape), const(w_gk_b.shape), const(b_gk_f.shape), const(b_gk_b.shape),
                const(q_norm.shape), const(k_norm.shape),
                rope_spec, rope_spec, rope_spec]
    row = lambda w: pl.BlockSpec((TM, w), lambda s: (out_tile(s), 0))
    cache_spec = pl.BlockSpec((seqs_per_tile, 1, SWA_KV_HEADS, HEAD_DIM, seq_p),
                              lambda s: (jnp.minimum(out_tile(s), tp - 1), 0, 0, 0, 0))
    widths = (4 * GLA_WIDTH, 2 * GLA_WIDTH, SWA_WIDTH + 2 * KV_WIDTH)
    out_specs = [row(widths[0]), pl.BlockSpec((TM, widths[1]), lambda s: (gate_tile(s), 0)), row(widths[2]),
                 cache_spec, cache_spec]
    cache_shape = (n_seq_p, 1, SWA_KV_HEADS, HEAD_DIM, seq_p)
    out_shape = ([jax.ShapeDtypeStruct((n, w), dt) for w, dt in zip(widths, (BF16, F32, BF16))]
                 + [jax.ShapeDtypeStruct(cache_shape, F32)] * 2)
    return pl.pallas_call(
        functools.partial(_proj_kernel, n_prompt_tiles=tp, n_tiles=n_tiles, tiles_per_seq=tiles_per_seq,
                          seqs_per_tile=seqs_per_tile, seq_len=seq_p),
        grid=(n_tiles + 2,), in_specs=in_specs, out_specs=out_specs, out_shape=out_shape,
        scratch_shapes=([pltpu.VMEM((D_MODEL, W_COLS), BF16), pltpu.VMEM((LANES, 2 * GLA_WIDTH), BF16)]
                        + [pltpu.VMEM((TM, D_MODEL), BF16)] * 2 + [pltpu.VMEM((TM, LANES), BF16)] * 2),
        compiler_params=_cparams(1), name="adaln_in_proj",
    )(xp, xs, mod, gain, w_in, w_gk_f, w_gk_b, b_gk_f, b_gk_b, q_norm, k_norm, *rope_tabs)


def _gla_kernel(*refs, seq_len, seqs, has_state, emit_state, with_attention, n_cast):
    (q_ref, k_ref, v_ref, gf_ref, gb_ref, sgg_ref, norm_ref,
     trif_ref, trib_ref, keepf_ref, keepb_ref) = refs[:11]
    pos = 11
    if has_state:
        s0f_ref, s0b_ref = refs[pos:pos + 2]
        pos += 2
    if with_attention:
        sink_ref, aq_ref, ak_ref, av_ref = refs[pos:pos + 4]
        pos += 4
    if with_attention == "latent":
        kc_ref, vc_ref = refs[pos:pos + 2]
        pos += 2
    cast_in = refs[pos:pos + n_cast]
    pos += n_cast
    og_ref = refs[pos]
    pos += 1
    if emit_state:
        sf_ref, sb_ref = refs[pos:pos + 2]
        pos += 2
    if with_attention:
        osw_ref = refs[pos]
        pos += 1
    cast_out = refs[pos:pos + n_cast]
    pos += n_cast
    accf_ref, accb_ref = refs[pos:pos + 2]

    for src, dst in zip(cast_in, cast_out):
        dst[...] = src[...].astype(BF16)

    n_groups = seq_len // GROUP
    per_group = GROUP // CHUNK
    n_pairs = q_ref.shape[1] // LANES
    lane = lax.broadcasted_iota(jnp.int32, (1, LANES), 1)
    head0 = lane < HEAD_DIM
    m0 = jnp.where(head0, 1.0, 0.0).astype(BF16)
    m1 = jnp.where(head0, 0.0, 1.0).astype(BF16)
    row_top = lax.broadcasted_iota(jnp.int32, (LANES, 1), 0) < HEAD_DIM

    def stack_heads(x):
        return jnp.concatenate([x * m0, x * m1], axis=0)

    def direction(row0, lanes, g_ref, tri_ref, keep_ref, state, acc_ref, backward):
        rows = pl.ds(row0, GROUP)
        g = g_ref[rows, lanes]
        g_hi = g.astype(BF16)
        g_lo = (g - g_hi.astype(F32)).astype(BF16)
        r = _dot(tri_ref[...], jnp.concatenate([g_hi, g_lo], axis=-1))
        yield
        b = (r[:, :LANES] + r[:, LANES:]).reshape(per_group, CHUNK, LANES)
        edge = 0 if backward else CHUNK - 1
        mid = CHUNK // 2
        b_edge = b[:, edge:edge + 1, :]
        b_mid = b[:, mid:mid + 1, :]
        q = q_ref[rows, lanes].astype(F32).reshape(per_group, CHUNK, LANES)
        k = k_ref[rows, lanes].astype(F32).reshape(per_group, CHUNK, LANES)
        v = v_ref[rows, lanes]
        q_mid = q * jnp.exp(b - b_mid)
        k_mid = k * jnp.exp(b_mid - b)
        q_in = q_mid.astype(BF16)
        k_in = k_mid.astype(BF16).reshape(GROUP, LANES)
        q_st = (q_mid * jnp.exp(b_mid)).astype(BF16)
        k_st = (k_mid * jnp.exp(b_edge - b_mid)).astype(BF16).reshape(GROUP, LANES)
        decay = jnp.exp(b_edge)
        keep = keep_ref[...]
        pair_rows = 2 * CHUNK
        n_cp = per_group // 2
        yield
        scores, kvs = [], []
        for p in range(n_cp):
            c0 = 2 * p
            prow = slice(p * pair_rows, (p + 1) * pair_rows)
            q_pair = jnp.concatenate([stack_heads(q_in[c0]), stack_heads(q_in[c0 + 1])], axis=0)
            scores.append(_dot_nt(q_pair, k_in[prow, :]).astype(BF16) * keep)
            v_t = v[prow, :].astype(F32).T.astype(BF16)
            k_pair = k_st[prow, :]
            zero = jnp.zeros_like(k_pair)
            k_blk = jnp.concatenate([jnp.where(row_top, k_pair, zero), jnp.where(row_top, zero, k_pair)], axis=-1)
            kvs.append(_dot(v_t, k_blk))
        yield
        intras = [_dot(scores[p], v[p * pair_rows:(p + 1) * pair_rows, :]) for p in range(n_cp)]
        yield
        for p in (range(n_cp - 1, -1, -1) if backward else range(n_cp)):
            for j in ((1, 0) if backward else (0, 1)):
                c = 2 * p + j
                inter = _dot_nt(stack_heads(q_st[c]), state.astype(BF16))
                tot = intras[p][j * pair_rows:(j + 1) * pair_rows, :] + inter
                acc_ref[pl.ds(row0 + c * CHUNK, CHUNK), lanes] = jnp.where(head0, tot[:CHUNK, :], tot[CHUNK:, :])
                state = state * decay[c] + kvs[p][:, j * LANES:(j + 1) * LANES]
            yield
        return state

    def load_state(s_ref, q, p):
        z = jnp.zeros((HEAD_DIM, HEAD_DIM), F32)
        top = jnp.concatenate([s_ref[q, 0, 2 * p, :, :].T, z], axis=-1)
        bot = jnp.concatenate([z, s_ref[q, 0, 2 * p + 1, :, :].T], axis=-1)
        return jnp.concatenate([top, bot], axis=0)

    per_dir = seqs * n_pairs
    if has_state:
        init = tuple(load_state(s_ref, q, p)
                     for s_ref in (s0f_ref, s0b_ref) for q in range(seqs) for p in range(n_pairs))
    else:
        init = (jnp.zeros((LANES, LANES), F32),) * (2 * per_dir)

    def body(t, carry):
        row_f = t * GROUP
        row_b = (n_groups - 1 - t) * GROUP
        if n_groups > 1:
            row_f = pl.multiple_of(row_f, GROUP)
            row_b = pl.multiple_of(row_b, GROUP)
        units = []
        for backward, row, g_ref, tri_ref, keep_ref, acc_ref in (
                (False, row_f, gf_ref, trif_ref, keepf_ref, accf_ref),
                (True, row_b, gb_ref, trib_ref, keepb_ref, accb_ref)):
            for q in range(seqs):
                for p in range(n_pairs):
                    lanes = slice(p * LANES, (p + 1) * LANES)
                    state = carry[backward * per_dir + q * n_pairs + p]
                    units.append(direction(q * seq_len + row, lanes, g_ref, tri_ref, keep_ref, state, acc_ref,
                                           backward))
        return units

    norm_pair = jnp.concatenate([norm_ref[...]] * 2, axis=-1)

    def finish_outputs(rows):
        for p in range(n_pairs):
            lanes = slice(p * LANES, (p + 1) * LANES)
            o = _head_rms_norm(accf_ref[rows, lanes] + accb_ref[rows, lanes], norm_pair)
            og_ref[rows, lanes] = (o * sgg_ref[rows, lanes].astype(F32)).astype(BF16)
            yield

    if n_groups == 1:
        gla_units = body(0, init)
        final = [None] * len(gla_units)
        for q in range(seqs + 1):
            rows = slice(q * seq_len, (q + 1) * seq_len)
            wave, ids = [], []
            if q < seqs:
                ids = [d * per_dir + q * n_pairs + p for d in range(2) for p in range(n_pairs)]
                wave = [gla_units[u] for u in ids]
                if with_attention:
                    wave += _context_attention_units(sink_ref, aq_ref, ak_ref, av_ref, rows)
            if q > 0:
                wave.append(finish_outputs(slice((q - 1) * seq_len, q * seq_len)))
            results = _run_staged(wave)
            for u, r in zip(ids, results):
                final[u] = r
            if q < seqs and with_attention:
                heads = [o for outs in results[len(ids):len(ids) + SWA_KV_HEADS] for o in outs]
                osw_ref[rows, :] = jnp.concatenate(heads, axis=-1).astype(BF16)
    else:
        assert with_attention != "context" and seqs == 1
        if with_attention:
            ctx_kv = [(kc_ref[0, 0, g, :, :].astype(BF16), vc_ref[0, 0, g, :, :].astype(BF16))
                      for g in range(SWA_KV_HEADS)]
            q_blocks = GROUP // QBLOCK

        def loop_body(t, carry):
            units = body(t, carry)
            n_gla = len(units)
            if with_attention:
                for j in range(q_blocks):
                    units += _latent_attention_units(sink_ref, aq_ref, ak_ref, av_ref, ctx_kv,
                                                     t * q_blocks + j, seq_len)
            results = _run_staged(units)
            if with_attention:
                for j in range(q_blocks):
                    heads = [o for outs in results[n_gla + j * SWA_KV_HEADS:n_gla + (j + 1) * SWA_KV_HEADS]
                             for o in outs]
                    rows = pl.ds(pl.multiple_of((t * q_blocks + j) * QBLOCK, QBLOCK), QBLOCK)
                    osw_ref[rows, :] = jnp.concatenate(heads, axis=-1).astype(BF16)
            return tuple(results[:n_gla])

        final = lax.fori_loop(0, n_groups, loop_body, init)
        _run_staged([finish_outputs(slice(0, seq_len))])
    if emit_state:
        for d, s_ref in enumerate((sf_ref, sb_ref)):
            for q in range(seqs):
                for p in range(n_pairs):
                    fin = final[d * per_dir + q * n_pairs + p]
                    s_ref[q, 0, 2 * p, :, :] = fin[:HEAD_DIM, :HEAD_DIM].T
                    s_ref[q, 0, 2 * p + 1, :, :] = fin[HEAD_DIM:, HEAD_DIM:].T


def _gla_constants():
    i = np.arange(GROUP)[:, None]
    j = np.arange(GROUP)[None, :]
    same = (i // CHUNK) == (j // CHUNK)
    tri_f = same & (j <= i)
    tri_b = same & (j >= i)
    r = np.arange(4 * CHUNK)[:, None]
    l = np.arange(LANES)[None, :]
    same = (r // (2 * CHUNK)) == (l // CHUNK)
    keep_f = same & ((l % CHUNK) <= (r % CHUNK))
    keep_b = same & ((l % CHUNK) >= (r % CHUNK))
    return tuple(jnp.asarray(m.astype(np.float32), dtype=BF16) for m in (tri_f, tri_b, keep_f, keep_b))


def _gla(gla_in, gates, norm_row, s0f, s0b, n_seq, seq_len, row_block0, emit_state, seqs=1,
         attention=None, cast_weights=()):
    has_state = s0f is not None
    steps = n_seq // seqs
    rows_per_step = seqs * seq_len
    row_blk = row_block0 // seqs
    col = lambda width, j: pl.BlockSpec((rows_per_step, width), lambda b: (row_blk + b, j))
    out_blk = pl.BlockSpec((rows_per_step, GLA_WIDTH), lambda b: (b, 0))
    st_spec = pl.BlockSpec((seqs, 1, GLA_HEADS, HEAD_DIM, HEAD_DIM), lambda b: (b, 0, 0, 0, 0))
    const = lambda shape: pl.BlockSpec(shape, lambda b: (0, 0))
    in_specs = ([col(GLA_WIDTH, 0), col(GLA_WIDTH, 1), col(GLA_WIDTH, 2), col(GLA_WIDTH, 0), col(GLA_WIDTH, 1),
                 col(GLA_WIDTH, 3)]
                + [const(norm_row.shape)] + [const((GROUP, GROUP))] * 2 + [const((4 * CHUNK, LANES))] * 2)
    args = [gla_in, gla_in, gla_in, gates, gates, gla_in, norm_row, *_gla_constants()]
    if has_state:
        in_specs += [st_spec] * 2
        args += [s0f, s0b]
    out_specs = [out_blk]
    out_shape = [jax.ShapeDtypeStruct((n_seq * seq_len, GLA_WIDTH), BF16)]
    if emit_state:
        out_specs += [st_spec] * 2
        out_shape += [jax.ShapeDtypeStruct((n_seq, 1, GLA_HEADS, HEAD_DIM, HEAD_DIM), F32)] * 2
    with_attention = None
    if attention is not None:
        sink, swa_in = attention[:2]
        with_attention = "context"
        in_specs += [pl.BlockSpec(memory_space=pltpu.SMEM), col(SWA_WIDTH, 0),
                     col(KV_WIDTH, SWA_WIDTH // KV_WIDTH), col(KV_WIDTH, SWA_WIDTH // KV_WIDTH + 1)]
        args += [sink, swa_in, swa_in, swa_in]
        if len(attention) > 2:
            with_attention = "latent"
            cache_k_t, cache_v_t = attention[2:]
            past = cache_k_t.shape[4]
            cache_spec = pl.BlockSpec((seqs, 1, SWA_KV_HEADS, HEAD_DIM, past), lambda b: (b, 0, 0, 0, 0))
            in_specs += [cache_spec] * 2
            args += [cache_k_t, cache_v_t]
        out_specs.append(pl.BlockSpec((rows_per_step, SWA_WIDTH), lambda b: (b, 0)))
        out_shape.append(jax.ShapeDtypeStruct((n_seq * seq_len, SWA_WIDTH), BF16))
    cast_specs, cast_shapes = _cast_specs(cast_weights, steps)
    in_specs += cast_specs
    args += list(cast_weights)
    out_specs += cast_specs
    out_shape += cast_shapes
    return pl.pallas_call(
        functools.partial(_gla_kernel, seq_len=seq_len, seqs=seqs, has_state=has_state, emit_state=emit_state,
                          with_attention=with_attention, n_cast=len(cast_weights)),
        grid=(steps,), in_specs=in_specs, out_specs=out_specs, out_shape=out_shape,
        scratch_shapes=[pltpu.VMEM((rows_per_step, GLA_WIDTH), F32)] * 2,
        compiler_params=_cparams(1), name="gla_bidirectional",
    )(*args)


def _group_attention(q, heads, sink_ref, score_fn, value_fn):
    rows = q.shape[0]
    q_stack = jnp.concatenate([q[:, h * HEAD_DIM:(h + 1) * HEAD_DIM] for h in heads], axis=0)
    scores = score_fn(q_stack)
    yield
    slabs = [slice(j * rows, (j + 1) * rows) for j in range(len(heads))]
    ms = []
    for slab, h in zip(slabs, heads):
        m = sink_ref[h]
        for s in scores:
            m = jnp.maximum(m, jnp.max(s[slab, :], axis=-1, keepdims=True))
        ms.append(m)
    yield
    e_parts = [[] for _ in scores]
    denoms = []
    for slab, h, m in zip(slabs, heads, ms):
        denom = jnp.exp(sink_ref[h] - m)
        for part, s in zip(e_parts, scores):
            e = jnp.exp(s[slab, :] - m)
            denom = denom + jnp.sum(e, axis=-1, keepdims=True)
            part.append(e.astype(BF16))
        denoms.append(denom)
    yield
    acc = value_fn([jnp.concatenate(part, axis=0) for part in e_parts])
    yield
    return [acc[slab, :] / denom for slab, denom in zip(slabs, denoms)]


def _run_staged(units):
    results = [None] * len(units)
    live = list(range(len(units)))
    while live:
        still = []
        for u in live:
            try:
                next(units[u])
                still.append(u)
            except StopIteration as stop:
                results[u] = stop.value
        live = still
    return results


def _context_attention_units(sink_ref, q_ref, k_ref, v_ref, rows):
    q = q_ref[rows, :]
    k = k_ref[rows, :]
    v = v_ref[rows, :]
    units = []
    for g in range(SWA_KV_HEADS):
        k_g = k[:, g * HEAD_DIM:(g + 1) * HEAD_DIM]
        v_g = v[:, g * HEAD_DIM:(g + 1) * HEAD_DIM]
        units.append(_group_attention(
            q, list(range(SWA_GROUP * g, SWA_GROUP * (g + 1))), sink_ref,
            lambda qs, k_g=k_g: [_dot_nt(qs, k_g)],
            lambda es, v_g=v_g: _dot(es[0], v_g)))
    return units


def _cast_specs(cast_weights, steps, step_of=lambda *idx: idx[0]):
    specs = [pl.BlockSpec((w.shape[0] // steps, w.shape[1]), lambda *idx: (step_of(*idx), 0))
             for w in cast_weights]
    shapes = [jax.ShapeDtypeStruct(w.shape, BF16) for w in cast_weights]
    return specs, shapes


def _latent_attention_units(sink_ref, q_ref, k_ref, v_ref, ctx_kv, i, seq_len):
    span = QBLOCK + 2 * WINDOW
    start = pl.multiple_of(jnp.clip(i * QBLOCK - WINDOW, 0, seq_len - span), QBLOCK)
    q = q_ref[pl.ds(pl.multiple_of(i * QBLOCK, QBLOCK), QBLOCK), :]
    k = k_ref[pl.ds(start, span), :]
    v = v_ref[pl.ds(start, span), :]
    qpos = i * QBLOCK + lax.broadcasted_iota(jnp.int32, (QBLOCK, span), 0)
    kpos = start + lax.broadcasted_iota(jnp.int32, (QBLOCK, span), 1)
    valid = jnp.abs(qpos - kpos) <= WINDOW
    valid = jnp.concatenate([valid] * SWA_GROUP, axis=0)
    units = []
    for g in range(SWA_KV_HEADS):
        k_g = k[:, g * HEAD_DIM:(g + 1) * HEAD_DIM]
        v_g = v[:, g * HEAD_DIM:(g + 1) * HEAD_DIM]
        kc_t, vc_t = ctx_kv[g]
        units.append(_group_attention(
            q, list(range(SWA_GROUP * g, SWA_GROUP * (g + 1))), sink_ref,
            lambda qs, k_g=k_g, kc_t=kc_t: [jnp.where(valid, _dot_nt(qs, k_g), NEG_INF), _dot(qs, kc_t)],
            lambda es, v_g=v_g, vc_t=vc_t: _dot(es[0], v_g) + _dot_nt(es[1], vc_t)))
    return units


def _out_kernel(xp_ref, xs_ref, ogp_ref, ogs_ref, oswp_ref, osws_ref, mod_ref, gain_ref,
                wo_ref, w1_ref, w2_ref, yp_ref, ys_ref, *, n_prompt_tiles, tiles_per_seq):
    i = pl.program_id(0)
    row = _mod_row(i, n_prompt_tiles, tiles_per_seq)

    def body(x_ref, og_ref, osw_ref, y_ref):
        mix = _dot(og_ref[...], wo_ref[:GLA_WIDTH, :]) + _dot(osw_ref[...], wo_ref[GLA_WIDTH:, :])
        x1 = x_ref[...] + _mod_chunk(mod_ref, row, 2) * mix
        h = _adaln(x1, gain_ref[...], _mod_chunk(mod_ref, row, 3), _mod_chunk(mod_ref, row, 4)).astype(BF16)
        acc = None
        for j in range(D_FF // FF_TILE):
            cols = slice(j * FF_TILE, (j + 1) * FF_TILE)
            a = jnp.maximum(_dot(h, w1_ref[:, cols]), 0.0)
            part = _dot((a * a).astype(BF16), w2_ref[cols, :])
            acc = part if acc is None else acc + part
        y_ref[...] = x1 + _mod_chunk(mod_ref, row, 5) * acc

    @pl.when(i < n_prompt_tiles)
    def _():
        body(xp_ref, ogp_ref, oswp_ref, yp_ref)

    @pl.when(i >= n_prompt_tiles)
    def _():
        body(xs_ref, ogs_ref, osws_ref, ys_ref)


def _mix_out(xp, xs, og_p, og_s, osw_p, osw_s, mod, gain, wo, w1, w2, seq_s):
    n_p, n_s = xp.shape[0], xs.shape[0]
    tp, ts = n_p // TM, n_s // TM
    tiles_per_seq = seq_s // TM
    const = lambda shape: pl.BlockSpec(shape, lambda i: (0,) * len(shape), pipeline_mode=pl.Buffered(1))
    row_p = lambda w: pl.BlockSpec((TM, w), lambda i: (jnp.minimum(i, tp - 1), 0))
    row_s = lambda w: pl.BlockSpec((TM, w), lambda i: (jnp.maximum(i - tp, 0), 0))
    return pl.pallas_call(
        functools.partial(_out_kernel, n_prompt_tiles=tp, tiles_per_seq=tiles_per_seq),
        grid=(tp + ts,),
        in_specs=[row_p(D_MODEL), row_s(D_MODEL), row_p(GLA_WIDTH), row_s(GLA_WIDTH),
                  row_p(SWA_WIDTH), row_s(SWA_WIDTH),
                  const(mod.shape),
                  const((1, D_MODEL)), const((GLA_WIDTH + SWA_WIDTH, D_MODEL)),
                  const((D_MODEL, D_FF)), const((D_FF, D_MODEL))],
        out_specs=[row_p(D_MODEL), row_s(D_MODEL)],
        out_shape=[jax.ShapeDtypeStruct((n_p, D_MODEL), F32), jax.ShapeDtypeStruct((n_s, D_MODEL), F32)],
        compiler_params=_cparams(1), name="out_proj_mlp",
    )(xp, xs, og_p, og_s, osw_p, osw_s, mod, gain, wo, w1, w2)


def _rope_tables(seq_len):
    axis_dim = HEAD_DIM // 2
    half = axis_dim // 2
    t = np.arange(seq_len)
    rowp = (t // GRID_W).astype(np.float64)
    colp = (t % GRID_W).astype(np.float64)
    freqs = ROPE_THETA ** (-np.arange(half, dtype=np.float64) / half)
    lane = np.arange(LANES)
    d = lane % HEAD_DIM
    posv = np.where((d // axis_dim)[None, :] == 0, rowp[:, None], colp[:, None])
    ang = posv * freqs[d % half][None, :]
    second = ((d % axis_dim) >= half)[None, :]
    cos, sin = np.cos(ang), np.sin(ang)
    zero = np.zeros_like(sin)
    tabs = (cos, np.where(second, sin, zero), np.where(second, zero, -sin))
    return tuple(jnp.asarray(tab.astype(np.float32)) for tab in tabs)


def kernel(x_prompt, x_sample, c, cache_k, cache_v, state_gla_fwd, state_gla_bwd, c_ctx, w_ada, b_ada, norm_mix, norm_ff, w_in, w_gk_fwd, b_gk_fwd, w_gk_bwd, b_gk_bwd, gla_norm, q_norm, k_norm, sink, w_o, w_ff1, w_ff2):
    depth = w_in.shape[0]
    assert depth == 1, "single trunk layer"
    bp, tp, _ = x_prompt.shape
    bs, ts, _ = x_sample.shape
    n_p = bp * tp
    l = 0

    mod = _modulation(c_ctx[None, :], c, w_ada[l], b_ada[l])

    norm2 = gla_norm[l][None, :]
    gain_mix = norm_mix[l][None, :]
    gain_ff = norm_ff[l][None, :]
    sink_l = sink[l]

    w_in_t = jnp.transpose(w_in[l])
    cache_k_t = jnp.swapaxes(cache_k, 3, 4)
    cache_v_t = jnp.swapaxes(cache_v, 3, 4)

    xp = x_prompt.reshape(n_p, D_MODEL)
    xs = x_sample.reshape(bs * ts, D_MODEL)
    gla_in, gates, swa_in, new_k_t, new_v_t = _project(
        xp, xs, mod, gain_mix, w_in_t, w_gk_fwd[l], w_gk_bwd[l], b_gk_fwd[l][None, :], b_gk_bwd[l][None, :],
        q_norm[l][None, :], k_norm[l][None, :], _rope_tables(ts), tp, ts)
    new_k = jnp.swapaxes(new_k_t, 3, 4)
    new_v = jnp.swapaxes(new_v_t, 3, 4)

    og_p, s_f, s_b, osw_p, wo, w1, w2 = _gla(
        gla_in, gates, norm2, None, None, bp, tp, 0, True, seqs=2,
        attention=(sink_l, swa_in), cast_weights=(w_o[l], w_ff1[l], w_ff2[l]))
    og_s, osw_s = _gla(gla_in, gates, norm2, state_gla_fwd, state_gla_bwd, bs, ts, n_p // ts, False,
                       attention=(sink_l, swa_in, cache_k_t, cache_v_t))

    yp, ys = _mix_out(xp, xs, og_p, og_s, osw_p, osw_s, mod, gain_ff, wo, w1, w2, ts)
    return (yp.reshape(bp, tp, D_MODEL), ys.reshape(bs, ts, D_MODEL), new_k, new_v, s_f, s_b)
```

```python
import functools

import numpy as np
import jax
import jax.numpy as jnp
from jax import lax
from jax.experimental import pallas as pl
from jax.experimental.pallas import tpu as pltpu

F32 = jnp.float32
BF16 = jnp.bfloat16

D_MODEL = 1024
GRID_W = 64
HEAD_DIM = 64
GLA_HEADS = 8
GLA_WIDTH = GLA_HEADS * HEAD_DIM
GATE_RANK = 16
GATE_TEMP = 16.0
CHUNK = 64
GROUP = 4 * CHUNK
SWA_HEADS = 8
SWA_KV_HEADS = 2
SWA_GROUP = SWA_HEADS // SWA_KV_HEADS
SWA_WIDTH = SWA_HEADS * HEAD_DIM
KV_WIDTH = SWA_KV_HEADS * HEAD_DIM
WINDOW = 128
QBLOCK = 128
D_FF = 4 * D_MODEL
ROPE_THETA = 10000.0
EPS = 1e-6
NEG_INF = -1e30
LOG2E = 1.4426950408889634
LANES = 128
TM = 512
FF_TILE = 1024

D_IN = 2848
R_GQ, R_GK, R_GV, R_GG, R_LR = 0, 512, 1024, 1536, 2048
R_SQ = R_LR + 2 * GATE_RANK
R_SV = R_SQ + SWA_WIDTH + KV_WIDTH
C_TAIL = R_LR
C_LR = C_TAIL + D_IN - R_SQ
W_COLS = C_LR + LANES

V7X_VMEM_BYTES = 64 * 1024 * 1024
VMEM_LIMIT = V7X_VMEM_BYTES - 8 * 1024 * 1024


def _cparams(n_axes):
    return pltpu.CompilerParams(dimension_semantics=("arbitrary",) * n_axes, vmem_limit_bytes=VMEM_LIMIT)


def _mod_kernel(cctx_ref, c_ref, w_ref, b_ref, o_ref):
    k = pl.program_id(0)
    pad = jnp.zeros((8 - 1 - c_ref.shape[0], c_ref.shape[1]), F32)
    cnd = jnp.concatenate([cctx_ref[...], c_ref[...], pad], axis=0)
    s = (cnd * jax.nn.sigmoid(cnd)).astype(BF16)
    part = _dot(s, w_ref[...].astype(BF16))

    @pl.when(k == 0)
    def _():
        o_ref[...] = part + b_ref[...]

    @pl.when(k > 0)
    def _():
        o_ref[...] += part


def _modulation(c_ctx, c, w_ada, b_ada):
    tk = 256
    n_out = w_ada.shape[1]
    return pl.pallas_call(
        _mod_kernel,
        grid=(D_MODEL // tk,),
        in_specs=[pl.BlockSpec((1, tk), lambda k: (0, k)),
                  pl.BlockSpec((c.shape[0], tk), lambda k: (0, k)),
                  pl.BlockSpec((tk, n_out), lambda k: (k, 0)),
                  pl.BlockSpec((1, n_out), lambda k: (0, 0))],
        out_specs=pl.BlockSpec((8, n_out), lambda k: (0, 0)),
        out_shape=jax.ShapeDtypeStruct((8, n_out), F32),
        compiler_params=_cparams(1),
        name="modulation",
    )(c_ctx, c, w_ada, b_ada.reshape(1, n_out))


def _mod_row(tile, n_prompt_tiles, tiles_per_seq):
    return jnp.where(tile < n_prompt_tiles, 0, 1 + (tile - n_prompt_tiles) // tiles_per_seq)


def _mod_chunk(mod_ref, row, j):
    return mod_ref[pl.ds(row, 1), j * D_MODEL:(j + 1) * D_MODEL]


def _adaln(x, gain, shift, scale):
    ms = jnp.mean(x * x, axis=-1, keepdims=True)
    xn = x * lax.rsqrt(ms + EPS)
    return xn * (gain * (1.0 + scale)) + shift


def _head_rms_norm(y, gain):
    cols = y.shape[1]
    lane = lax.broadcasted_iota(jnp.int32, (1, LANES), 1)
    first = lane < HEAD_DIM
    outs = []
    for p in range(cols // LANES):
        s = y[:, p * LANES:(p + 1) * LANES]
        s = s * s
        tot = jnp.sum(s, axis=-1, keepdims=True)
        lo = jnp.sum(jnp.where(first, s, 0.0), axis=-1, keepdims=True)
        outs.append(jnp.where(first, lo, tot - lo))
    sums = outs[0] if len(outs) == 1 else jnp.concatenate(outs, axis=-1)
    return y * lax.rsqrt(sums + HEAD_DIM * EPS) * (gain * (HEAD_DIM ** 0.5))


def _log_sigmoid(x):
    return jnp.minimum(x, 0.0) - jnp.log(1.0 + jnp.exp(-jnp.abs(x)))


def _dot(a, b):
    return jnp.dot(a, b, preferred_element_type=F32)


def _dot_nt(a, b):
    return lax.dot_general(a, b, (((1,), (1,)), ((), ())), preferred_element_type=F32)


def _proj_kernel(xp_ref, xs_ref, mod_ref, gain_ref, w_ref, wgf_ref, wgb_ref, bgf_ref, bgb_ref, qn_ref, kn_ref,
                 cos_ref, sin_a_ref, sin_b_ref,
                 gla_ref, gate_ref, swa_ref, nk_ref, nv_ref,
                 wb_ref, wg_ref, h0_ref, h1_ref, lr0_ref, lr1_ref,
                 *, n_prompt_tiles, n_tiles, tiles_per_seq, seqs_per_tile, seq_len):
    h_refs = (h0_ref, h1_ref)
    lr_refs = (lr0_ref, lr1_ref)
    s = pl.program_id(0)

    @pl.when(s == 0)
    def _():
        moves = ([(c * LANES, c * LANES) for c in range(R_LR // LANES)]
                 + [(R_SQ + c * LANES, C_TAIL + c * LANES) for c in range((D_IN - R_SQ) // LANES)]
                 + [(R_LR, C_LR)])
        for src, dst in moves:
            wb_ref[:, dst:dst + LANES] = w_ref[src:src + LANES, :].T.astype(BF16)
        wg_ref[...] = jnp.zeros(wg_ref.shape, BF16)
        wg_ref[0:GATE_RANK, 0:GLA_WIDTH] = wgf_ref[...].astype(BF16)
        wg_ref[GATE_RANK:2 * GATE_RANK, GLA_WIDTH:] = wgb_ref[...].astype(BF16)

    def normalise_next(parity):
        x = jnp.where(s < n_prompt_tiles, xp_ref[...], xs_ref[...])
        row = _mod_row(jnp.minimum(s, n_tiles - 1), n_prompt_tiles, tiles_per_seq)
        shift, scale = _mod_chunk(mod_ref, row, 0), _mod_chunk(mod_ref, row, 1)
        h_refs[parity][...] = _adaln(x, gain_ref[...], shift, scale).astype(BF16)

    def gates(parity):
        gate_pre = _dot(lr_refs[parity][...], wg_ref[...])
        gate_ref[:, :GLA_WIDTH] = _log_sigmoid(gate_pre[:, :GLA_WIDTH] + bgf_ref[...]) * (LOG2E / GATE_TEMP)
        gate_ref[:, GLA_WIDTH:] = _log_sigmoid(gate_pre[:, GLA_WIDTH:] + bgb_ref[...]) * (LOG2E / GATE_TEMP)

    def body(rope, emit_cache, parity):
        h_ref = h_refs[1 - parity]
        gates(parity)

        def proj(c0, c1):
            return _dot(h_ref[...], wb_ref[:, c0:c1])

        tail = proj(C_TAIL, C_LR)
        lr_refs[1 - parity][...] = proj(C_LR, W_COLS).astype(BF16)
        gg = proj(R_GG, R_LR)
        gla_ref[:, 3 * GLA_WIDTH:] = (gg * jax.nn.sigmoid(gg)).astype(BF16)
        attention_inputs(tail, rope, emit_cache)
        gla_ref[:, :GLA_WIDTH] = (proj(R_GQ, R_GK) * (HEAD_DIM ** -0.5)).astype(BF16)
        gla_ref[:, GLA_WIDTH:2 * GLA_WIDTH] = proj(R_GK, R_GV).astype(BF16)
        gla_ref[:, 2 * GLA_WIDTH:3 * GLA_WIDTH] = proj(R_GV, R_GG).astype(BF16)
        normalise_next(parity)

    def attention_inputs(tail, rope, emit_cache):
        qk = tail[:, :R_SV - R_SQ]
        qk_gain = jnp.concatenate([qn_ref[...]] * SWA_HEADS + [kn_ref[...]] * SWA_KV_HEADS, axis=-1)
        qk = _head_rms_norm(qk, qk_gain)
        if rope:
            reps = (SWA_WIDTH + KV_WIDTH) // LANES
            cos = jnp.concatenate([cos_ref[...]] * reps, axis=-1)
            sin_a = jnp.concatenate([sin_a_ref[...]] * reps, axis=-1)
            sin_b = jnp.concatenate([sin_b_ref[...]] * reps, axis=-1)
            width = qk.shape[1]
            quarter = HEAD_DIM // 4
            qk = (qk * cos + pltpu.roll(qk, quarter, 1) * sin_a
                  + pltpu.roll(qk, width - quarter, 1) * sin_b)
        swa_ref[:, :SWA_WIDTH] = (qk[:, :SWA_WIDTH] * (HEAD_DIM ** -0.5 * LOG2E)).astype(BF16)
        sk = qk[:, SWA_WIDTH:]
        swa_ref[:, SWA_WIDTH:SWA_WIDTH + KV_WIDTH] = sk.astype(BF16)
        sv = tail[:, R_SV - R_SQ:]
        swa_ref[:, SWA_WIDTH + KV_WIDTH:] = sv.astype(BF16)
        if emit_cache:
            sk_t, sv_t = sk.T, sv.T
            for q in range(seqs_per_tile):
                cols = slice(q * seq_len, (q + 1) * seq_len)
                for g in range(SWA_KV_HEADS):
                    feat = slice(g * HEAD_DIM, (g + 1) * HEAD_DIM)
                    nk_ref[q, 0, g, :, :] = sk_t[feat, cols]
                    nv_ref[q, 0, g, :, :] = sv_t[feat, cols]

    @pl.when(s == 0)
    def _():
        normalise_next(0)
        lr_refs[1][...] = jnp.zeros(lr_refs[1].shape, BF16)

    for parity in range(2):
        on_parity = (s % 2) == parity

        @pl.when((s >= 1) & (s <= n_prompt_tiles) & on_parity)
        def _():
            body(rope=False, emit_cache=True, parity=parity)

        @pl.when((s > n_prompt_tiles) & (s <= n_tiles) & on_parity)
        def _():
            body(rope=True, emit_cache=False, parity=parity)

    @pl.when(s == n_tiles + 1)
    def _():
        gates((n_tiles + 1) % 2)


def _project(xp, xs, mod, gain, w_in, w_gk_f, w_gk_b, b_gk_f, b_gk_b, q_norm, k_norm, rope_tabs, seq_p, seq_s):
    n_p, n_s = xp.shape[0], xs.shape[0]
    tp, ts = n_p // TM, n_s // TM
    n = n_p + n_s
    seqs_per_tile = TM // seq_p
    tiles_per_seq = seq_s // TM
    n_seq_p = n_p // seq_p
    const = lambda shape: pl.BlockSpec(shape, lambda s: (0,) * len(shape))
    n_tiles = tp + ts
    out_tile = lambda s: jnp.clip(s - 1, 0, n_tiles - 1)
    gate_tile = lambda s: jnp.maximum(s - 2, 0)
    rope_spec = pl.BlockSpec((TM, LANES), lambda s: (jnp.maximum(out_tile(s) - tp, 0) % tiles_per_seq, 0))
    in_specs = [pl.BlockSpec((TM, D_MODEL), lambda s: (jnp.minimum(s, tp - 1), 0)),
                pl.BlockSpec((TM, D_MODEL), lambda s: (jnp.clip(s - tp, 0, ts - 1), 0)),
                const(mod.shape),
                const((1, D_MODEL)),
                pl.BlockSpec((D_IN, D_MODEL), lambda s: (0, 0), pipeline_mode=pl.Buffered(1)),
                const(w_gk_f.shape), const(w_gk_b.shape), const(b_gk_f.shape), const(b_gk_b.shape),
                const(q_norm.shape), const(k_norm.shape),
                rope_spec, rope_spec, rope_spec]
    row = lambda w: pl.BlockSpec((TM, w), lambda s: (out_tile(s), 0))
    cache_spec = pl.BlockSpec((seqs_per_tile, 1, SWA_KV_HEADS, HEAD_DIM, seq_p),
                              lambda s: (jnp.minimum(out_tile(s), tp - 1), 0, 0, 0, 0))
    widths = (4 * GLA_WIDTH, 2 * GLA_WIDTH, SWA_WIDTH + 2 * KV_WIDTH)
    out_specs = [row(widths[0]), pl.BlockSpec((TM, widths[1]), lambda s: (gate_tile(s), 0)), row(widths[2]),
                 cache_spec, cache_spec]
    cache_shape = (n_seq_p, 1, SWA_KV_HEADS, HEAD_DIM, seq_p)
    out_shape = ([jax.ShapeDtypeStruct((n, w), dt) for w, dt in zip(widths, (BF16, F32, BF16))]
                 + [jax.ShapeDtypeStruct(cache_shape, F32)] * 2)
    return pl.pallas_call(
        functools.partial(_proj_kernel, n_prompt_tiles=tp, n_tiles=n_tiles, tiles_per_seq=tiles_per_seq,
                          seqs_per_tile=seqs_per_tile, seq_len=seq_p),
        grid=(n_tiles + 2,), in_specs=in_specs, out_specs=out_specs, out_shape=out_shape,
        scratch_shapes=([pltpu.VMEM((D_MODEL, W_COLS), BF16), pltpu.VMEM((LANES, 2 * GLA_WIDTH), BF16)]
                        + [pltpu.VMEM((TM, D_MODEL), BF16)] * 2 + [pltpu.VMEM((TM, LANES), BF16)] * 2),
        compiler_params=_cparams(1), name="adaln_in_proj",
    )(xp, xs, mod, gain, w_in, w_gk_f, w_gk_b, b_gk_f, b_gk_b, q_norm, k_norm, *rope_tabs)


def _gla_kernel(*refs, seq_len, seqs, has_state, emit_state, with_attention, n_cast):
    (q_ref, k_ref, v_ref, gf_ref, gb_ref, sgg_ref, norm_ref,
     trif_ref, trib_ref, keepf_ref, keepb_ref) = refs[:11]
    pos = 11
    if has_state:
        s0f_ref, s0b_ref = refs[pos:pos + 2]
        pos += 2
    if with_attention:
        sink_ref, aq_ref, ak_ref, av_ref = refs[pos:pos + 4]
        pos += 4
    if with_attention == "latent":
        kc_ref, vc_ref = refs[pos:pos + 2]
        pos += 2
    cast_in = refs[pos:pos + n_cast]
    pos += n_cast
    og_ref = refs[pos]
    pos += 1
    if emit_state:
        sf_ref, sb_ref = refs[pos:pos + 2]
        pos += 2
    if with_attention:
        osw_ref = refs[pos]
        pos += 1
    cast_out = refs[pos:pos + n_cast]
    pos += n_cast
    accf_ref, accb_ref = refs[pos:pos + 2]

    for src, dst in zip(cast_in, cast_out):
        dst[...] = src[...].astype(BF16)

    n_groups = seq_len // GROUP
    per_group = GROUP // CHUNK
    n_pairs = q_ref.shape[1] // LANES
    lane = lax.broadcasted_iota(jnp.int32, (1, LANES), 1)
    head0 = lane < HEAD_DIM
    m0 = jnp.where(head0, 1.0, 0.0).astype(BF16)
    m1 = jnp.where(head0, 0.0, 1.0).astype(BF16)
    row_top = lax.broadcasted_iota(jnp.int32, (LANES, 1), 0) < HEAD_DIM

    def stack_heads(x):
        return jnp.concatenate([x * m0, x * m1], axis=0)

    def direction(row0, lanes, g_ref, tri_ref, keep_ref, state, acc_ref, backward):
        rows = pl.ds(row0, GROUP)
        g = g_ref[rows, lanes]
        g_hi = g.astype(BF16)
        g_lo = (g - g_hi.astype(F32)).astype(BF16)
        r = _dot(tri_ref[...], jnp.concatenate([g_hi, g_lo], axis=-1))
        yield
        b = (r[:, :LANES] + r[:, LANES:]).reshape(per_group, CHUNK, LANES)
        edge = 0 if backward else CHUNK - 1
        mid = CHUNK // 2
        b_edge = b[:, edge:edge + 1, :]
        b_mid = b[:, mid:mid + 1, :]
        q = q_ref[rows, lanes].astype(F32).reshape(per_group, CHUNK, LANES)
        k = k_ref[rows, lanes].astype(F32).reshape(per_group, CHUNK, LANES)
        v = v_ref[rows, lanes]
        q_mid = q * jnp.exp2(b - b_mid)
        k_mid = k * jnp.exp2(b_mid - b)
        q_in = q_mid.astype(BF16)
        k_in = k_mid.astype(BF16).reshape(GROUP, LANES)
        q_st = (q_mid * jnp.exp2(b_mid)).astype(BF16)
        k_st = (k_mid * jnp.exp2(b_edge - b_mid)).astype(BF16).reshape(GROUP, LANES)
        decay = jnp.exp2(b_edge)
        keep = keep_ref[...]
        pair_rows = 2 * CHUNK
        n_cp = per_group // 2
        yield
        scores, kvs = [], []
        for p in range(n_cp):
            c0 = 2 * p
            prow = slice(p * pair_rows, (p + 1) * pair_rows)
            q_pair = jnp.concatenate([stack_heads(q_in[c0]), stack_heads(q_in[c0 + 1])], axis=0)
            scores.append(_dot_nt(q_pair, k_in[prow, :]).astype(BF16) * keep)
            v_t = v[prow, :].astype(F32).T.astype(BF16)
            k_pair = k_st[prow, :]
            zero = jnp.zeros_like(k_pair)
            k_blk = jnp.concatenate([jnp.where(row_top, k_pair, zero), jnp.where(row_top, zero, k_pair)], axis=-1)
            kvs.append(_dot(v_t, k_blk))
        yield
        intras = [_dot(scores[p], v[p * pair_rows:(p + 1) * pair_rows, :]) for p in range(n_cp)]
        yield
        for p in (range(n_cp - 1, -1, -1) if backward else range(n_cp)):
            for j in ((1, 0) if backward else (0, 1)):
                c = 2 * p + j
                inter = _dot_nt(stack_heads(q_st[c]), state.astype(BF16))
                tot = intras[p][j * pair_rows:(j + 1) * pair_rows, :] + inter
                acc_ref[pl.ds(row0 + c * CHUNK, CHUNK), lanes] = jnp.where(head0, tot[:CHUNK, :], tot[CHUNK:, :])
                state = state * decay[c] + kvs[p][:, j * LANES:(j + 1) * LANES]
            yield
        return state

    def load_state(s_ref, q, p):
        z = jnp.zeros((HEAD_DIM, HEAD_DIM), F32)
        top = jnp.concatenate([s_ref[q, 0, 2 * p, :, :].T, z], axis=-1)
        bot = jnp.concatenate([z, s_ref[q, 0, 2 * p + 1, :, :].T], axis=-1)
        return jnp.concatenate([top, bot], axis=0)

    per_dir = seqs * n_pairs
    if has_state:
        init = tuple(load_state(s_ref, q, p)
                     for s_ref in (s0f_ref, s0b_ref) for q in range(seqs) for p in range(n_pairs))
    else:
        init = (jnp.zeros((LANES, LANES), F32),) * (2 * per_dir)

    def body(t, carry):
        row_f = t * GROUP
        row_b = (n_groups - 1 - t) * GROUP
        if n_groups > 1:
            row_f = pl.multiple_of(row_f, GROUP)
            row_b = pl.multiple_of(row_b, GROUP)
        units = []
        for backward, row, g_ref, tri_ref, keep_ref, acc_ref in (
                (False, row_f, gf_ref, trif_ref, keepf_ref, accf_ref),
                (True, row_b, gb_ref, trib_ref, keepb_ref, accb_ref)):
            for q in range(seqs):
                for p in range(n_pairs):
                    lanes = slice(p * LANES, (p + 1) * LANES)
                    state = carry[backward * per_dir + q * n_pairs + p]
                    units.append(direction(q * seq_len + row, lanes, g_ref, tri_ref, keep_ref, state, acc_ref,
                                           backward))
        return units

    norm_pair = jnp.concatenate([norm_ref[...]] * 2, axis=-1)

    def finish_outputs(rows):
        for p in range(n_pairs):
            lanes = slice(p * LANES, (p + 1) * LANES)
            o = _head_rms_norm(accf_ref[rows, lanes] + accb_ref[rows, lanes], norm_pair)
            og_ref[rows, lanes] = (o * sgg_ref[rows, lanes].astype(F32)).astype(BF16)
            yield

    if n_groups == 1:
        gla_units = body(0, init)
        final = [None] * len(gla_units)
        for q in range(seqs + 1):
            rows = slice(q * seq_len, (q + 1) * seq_len)
            wave, ids = [], []
            if q < seqs:
                ids = [d * per_dir + q * n_pairs + p for d in range(2) for p in range(n_pairs)]
                wave = [gla_units[u] for u in ids]
                if with_attention:
                    wave += _context_attention_units(sink_ref, aq_ref, ak_ref, av_ref, rows)
            if q > 0:
                wave.append(finish_outputs(slice((q - 1) * seq_len, q * seq_len)))
            results = _run_staged(wave)
            for u, r in zip(ids, results):
                final[u] = r
            if q < seqs and with_attention:
                heads = [o for outs in results[len(ids):len(ids) + SWA_KV_HEADS] for o in outs]
                osw_ref[rows, :] = jnp.concatenate(heads, axis=-1).astype(BF16)
    else:
        assert with_attention != "context" and seqs == 1
        if with_attention:
            ctx_kv = [(kc_ref[0, 0, g, :, :].astype(BF16), vc_ref[0, 0, g, :, :].astype(BF16))
                      for g in range(SWA_KV_HEADS)]
            q_blocks = GROUP // QBLOCK

        def loop_body(t, carry):
            units = body(t, carry)
            n_gla = len(units)
            if with_attention:
                for j in range(q_blocks):
                    units += _latent_attention_units(sink_ref, aq_ref, ak_ref, av_ref, ctx_kv,
                                                     t * q_blocks + j, seq_len)
            results = _run_staged(units)
            if with_attention:
                for j in range(q_blocks):
                    heads = [o for outs in results[n_gla + j * SWA_KV_HEADS:n_gla + (j + 1) * SWA_KV_HEADS]
                             for o in outs]
                    rows = pl.ds(pl.multiple_of((t * q_blocks + j) * QBLOCK, QBLOCK), QBLOCK)
                    osw_ref[rows, :] = jnp.concatenate(heads, axis=-1).astype(BF16)
            return tuple(results[:n_gla])

        final = lax.fori_loop(0, n_groups, loop_body, init)
        _run_staged([finish_outputs(slice(0, seq_len))])
    if emit_state:
        for d, s_ref in enumerate((sf_ref, sb_ref)):
            for q in range(seqs):
                for p in range(n_pairs):
                    fin = final[d * per_dir + q * n_pairs + p]
                    s_ref[q, 0, 2 * p, :, :] = fin[:HEAD_DIM, :HEAD_DIM].T
                    s_ref[q, 0, 2 * p + 1, :, :] = fin[HEAD_DIM:, HEAD_DIM:].T


def _gla_constants():
    i = np.arange(GROUP)[:, None]
    j = np.arange(GROUP)[None, :]
    same = (i // CHUNK) == (j // CHUNK)
    tri_f = same & (j <= i)
    tri_b = same & (j >= i)
    r = np.arange(4 * CHUNK)[:, None]
    l = np.arange(LANES)[None, :]
    same = (r // (2 * CHUNK)) == (l // CHUNK)
    keep_f = same & ((l % CHUNK) <= (r % CHUNK))
    keep_b = same & ((l % CHUNK) >= (r % CHUNK))
    return tuple(jnp.asarray(m.astype(np.float32), dtype=BF16) for m in (tri_f, tri_b, keep_f, keep_b))


def _gla(gla_in, gates, norm_row, s0f, s0b, n_seq, seq_len, row_block0, emit_state, seqs=1,
         attention=None, cast_weights=()):
    has_state = s0f is not None
    steps = n_seq // seqs
    rows_per_step = seqs * seq_len
    row_blk = row_block0 // seqs
    col = lambda width, j: pl.BlockSpec((rows_per_step, width), lambda b: (row_blk + b, j))
    out_blk = pl.BlockSpec((rows_per_step, GLA_WIDTH), lambda b: (b, 0))
    st_spec = pl.BlockSpec((seqs, 1, GLA_HEADS, HEAD_DIM, HEAD_DIM), lambda b: (b, 0, 0, 0, 0))
    const = lambda shape: pl.BlockSpec(shape, lambda b: (0, 0))
    in_specs = ([col(GLA_WIDTH, 0), col(GLA_WIDTH, 1), col(GLA_WIDTH, 2), col(GLA_WIDTH, 0), col(GLA_WIDTH, 1),
                 col(GLA_WIDTH, 3)]
                + [const(norm_row.shape)] + [const((GROUP, GROUP))] * 2 + [const((4 * CHUNK, LANES))] * 2)
    args = [gla_in, gla_in, gla_in, gates, gates, gla_in, norm_row, *_gla_constants()]
    if has_state:
        in_specs += [st_spec] * 2
        args += [s0f, s0b]
    out_specs = [out_blk]
    out_shape = [jax.ShapeDtypeStruct((n_seq * seq_len, GLA_WIDTH), BF16)]
    if emit_state:
        out_specs += [st_spec] * 2
        out_shape += [jax.ShapeDtypeStruct((n_seq, 1, GLA_HEADS, HEAD_DIM, HEAD_DIM), F32)] * 2
    with_attention = None
    if attention is not None:
        sink, swa_in = attention[:2]
        with_attention = "context"
        in_specs += [pl.BlockSpec(memory_space=pltpu.SMEM), col(SWA_WIDTH, 0),
                     col(KV_WIDTH, SWA_WIDTH // KV_WIDTH), col(KV_WIDTH, SWA_WIDTH // KV_WIDTH + 1)]
        args += [sink, swa_in, swa_in, swa_in]
        if len(attention) > 2:
            with_attention = "latent"
            cache_k_t, cache_v_t = attention[2:]
            past = cache_k_t.shape[4]
            cache_spec = pl.BlockSpec((seqs, 1, SWA_KV_HEADS, HEAD_DIM, past), lambda b: (b, 0, 0, 0, 0))
            in_specs += [cache_spec] * 2
            args += [cache_k_t, cache_v_t]
        out_specs.append(pl.BlockSpec((rows_per_step, SWA_WIDTH), lambda b: (b, 0)))
        out_shape.append(jax.ShapeDtypeStruct((n_seq * seq_len, SWA_WIDTH), BF16))
    cast_specs, cast_shapes = _cast_specs(cast_weights, steps)
    in_specs += cast_specs
    args += list(cast_weights)
    out_specs += cast_specs
    out_shape += cast_shapes
    return pl.pallas_call(
        functools.partial(_gla_kernel, seq_len=seq_len, seqs=seqs, has_state=has_state, emit_state=emit_state,
                          with_attention=with_attention, n_cast=len(cast_weights)),
        grid=(steps,), in_specs=in_specs, out_specs=out_specs, out_shape=out_shape,
        scratch_shapes=[pltpu.VMEM((rows_per_step, GLA_WIDTH), F32)] * 2,
        compiler_params=_cparams(1), name="gla_bidirectional",
    )(*args)


def _group_attention(q, heads, sink_ref, score_fn, value_fn):
    rows = q.shape[0]
    q_stack = jnp.concatenate([q[:, h * HEAD_DIM:(h + 1) * HEAD_DIM] for h in heads], axis=0)
    scores = score_fn(q_stack)
    yield
    slabs = [slice(j * rows, (j + 1) * rows) for j in range(len(heads))]
    ms = []
    sinks = [sink_ref[h] * LOG2E for h in heads]
    for slab, sink in zip(slabs, sinks):
        m = sink
        for s in scores:
            m = jnp.maximum(m, jnp.max(s[slab, :], axis=-1, keepdims=True))
        ms.append(m)
    yield
    e_parts = [[] for _ in scores]
    denoms = []
    for slab, sink, m in zip(slabs, sinks, ms):
        denom = jnp.exp2(sink - m)
        for part, s in zip(e_parts, scores):
            e = jnp.exp2(s[slab, :] - m)
            denom = denom + jnp.sum(e, axis=-1, keepdims=True)
            part.append(e.astype(BF16))
        denoms.append(denom)
    yield
    acc = value_fn([jnp.concatenate(part, axis=0) for part in e_parts])
    yield
    return [acc[slab, :] / denom for slab, denom in zip(slabs, denoms)]


def _run_staged(units):
    results = [None] * len(units)
    live = list(range(len(units)))
    while live:
        still = []
        for u in live:
            try:
                next(units[u])
                still.append(u)
            except StopIteration as stop:
                results[u] = stop.value
        live = still
    return results


def _context_attention_units(sink_ref, q_ref, k_ref, v_ref, rows):
    q = q_ref[rows, :]
    k = k_ref[rows, :]
    v = v_ref[rows, :]
    units = []
    for g in range(SWA_KV_HEADS):
        k_g = k[:, g * HEAD_DIM:(g + 1) * HEAD_DIM]
        v_g = v[:, g * HEAD_DIM:(g + 1) * HEAD_DIM]
        units.append(_group_attention(
            q, list(range(SWA_GROUP * g, SWA_GROUP * (g + 1))), sink_ref,
            lambda qs, k_g=k_g: [_dot_nt(qs, k_g)],
            lambda es, v_g=v_g: _dot(es[0], v_g)))
    return units


def _cast_specs(cast_weights, steps, step_of=lambda *idx: idx[0]):
    specs = [pl.BlockSpec((w.shape[0] // steps, w.shape[1]), lambda *idx: (step_of(*idx), 0))
             for w in cast_weights]
    shapes = [jax.ShapeDtypeStruct(w.shape, BF16) for w in cast_weights]
    return specs, shapes


def _latent_attention_units(sink_ref, q_ref, k_ref, v_ref, ctx_kv, i, seq_len):
    span = QBLOCK + 2 * WINDOW
    start = pl.multiple_of(jnp.clip(i * QBLOCK - WINDOW, 0, seq_len - span), QBLOCK)
    q = q_ref[pl.ds(pl.multiple_of(i * QBLOCK, QBLOCK), QBLOCK), :]
    k = k_ref[pl.ds(start, span), :]
    v = v_ref[pl.ds(start, span), :]
    qpos = i * QBLOCK + lax.broadcasted_iota(jnp.int32, (QBLOCK, span), 0)
    kpos = start + lax.broadcasted_iota(jnp.int32, (QBLOCK, span), 1)
    valid = jnp.abs(qpos - kpos) <= WINDOW
    valid = jnp.concatenate([valid] * SWA_GROUP, axis=0)
    units = []
    for g in range(SWA_KV_HEADS):
        k_g = k[:, g * HEAD_DIM:(g + 1) * HEAD_DIM]
        v_g = v[:, g * HEAD_DIM:(g + 1) * HEAD_DIM]
        kc_t, vc_t = ctx_kv[g]
        units.append(_group_attention(
            q, list(range(SWA_GROUP * g, SWA_GROUP * (g + 1))), sink_ref,
            lambda qs, k_g=k_g, kc_t=kc_t: [jnp.where(valid, _dot_nt(qs, k_g), NEG_INF), _dot(qs, kc_t)],
            lambda es, v_g=v_g, vc_t=vc_t: _dot(es[0], v_g) + _dot_nt(es[1], vc_t)))
    return units


def _out_kernel(xp_ref, xs_ref, ogp_ref, ogs_ref, oswp_ref, osws_ref, mod_ref, gain_ref,
                wo_ref, w1_ref, w2_ref, yp_ref, ys_ref, *, n_prompt_tiles, tiles_per_seq):
    i = pl.program_id(0)
    row = _mod_row(i, n_prompt_tiles, tiles_per_seq)

    def body(x_ref, og_ref, osw_ref, y_ref):
        mix = _dot(og_ref[...], wo_ref[:GLA_WIDTH, :]) + _dot(osw_ref[...], wo_ref[GLA_WIDTH:, :])
        x1 = x_ref[...] + _mod_chunk(mod_ref, row, 2) * mix
        h = _adaln(x1, gain_ref[...], _mod_chunk(mod_ref, row, 3), _mod_chunk(mod_ref, row, 4)).astype(BF16)
        acc = None
        for j in range(D_FF // FF_TILE):
            cols = slice(j * FF_TILE, (j + 1) * FF_TILE)
            a = jnp.maximum(_dot(h, w1_ref[:, cols]), 0.0)
            part = _dot((a * a).astype(BF16), w2_ref[cols, :])
            acc = part if acc is None else acc + part
        y_ref[...] = x1 + _mod_chunk(mod_ref, row, 5) * acc

    @pl.when(i < n_prompt_tiles)
    def _():
        body(xp_ref, ogp_ref, oswp_ref, yp_ref)

    @pl.when(i >= n_prompt_tiles)
    def _():
        body(xs_ref, ogs_ref, osws_ref, ys_ref)


def _mix_out(xp, xs, og_p, og_s, osw_p, osw_s, mod, gain, wo, w1, w2, seq_s):
    n_p, n_s = xp.shape[0], xs.shape[0]
    tp, ts = n_p // TM, n_s // TM
    tiles_per_seq = seq_s // TM
    const = lambda shape: pl.BlockSpec(shape, lambda i: (0,) * len(shape), pipeline_mode=pl.Buffered(1))
    row_p = lambda w: pl.BlockSpec((TM, w), lambda i: (jnp.minimum(i, tp - 1), 0))
    row_s = lambda w: pl.BlockSpec((TM, w), lambda i: (jnp.maximum(i - tp, 0), 0))
    return pl.pallas_call(
        functools.partial(_out_kernel, n_prompt_tiles=tp, tiles_per_seq=tiles_per_seq),
        grid=(tp + ts,),
        in_specs=[row_p(D_MODEL), row_s(D_MODEL), row_p(GLA_WIDTH), row_s(GLA_WIDTH),
                  row_p(SWA_WIDTH), row_s(SWA_WIDTH),
                  const(mod.shape),
                  const((1, D_MODEL)), const((GLA_WIDTH + SWA_WIDTH, D_MODEL)),
                  const((D_MODEL, D_FF)), const((D_FF, D_MODEL))],
        out_specs=[row_p(D_MODEL), row_s(D_MODEL)],
        out_shape=[jax.ShapeDtypeStruct((n_p, D_MODEL), F32), jax.ShapeDtypeStruct((n_s, D_MODEL), F32)],
        compiler_params=_cparams(1), name="out_proj_mlp",
    )(xp, xs, og_p, og_s, osw_p, osw_s, mod, gain, wo, w1, w2)


def _rope_tables(seq_len):
    axis_dim = HEAD_DIM // 2
    half = axis_dim // 2
    t = np.arange(seq_len)
    rowp = (t // GRID_W).astype(np.float64)
    colp = (t % GRID_W).astype(np.float64)
    freqs = ROPE_THETA ** (-np.arange(half, dtype=np.float64) / half)
    lane = np.arange(LANES)
    d = lane % HEAD_DIM
    posv = np.where((d // axis_dim)[None, :] == 0, rowp[:, None], colp[:, None])
    ang = posv * freqs[d % half][None, :]
    second = ((d % axis_dim) >= half)[None, :]
    cos, sin = np.cos(ang), np.sin(ang)
    zero = np.zeros_like(sin)
    tabs = (cos, np.where(second, sin, zero), np.where(second, zero, -sin))
    return tuple(jnp.asarray(tab.astype(np.float32)) for tab in tabs)


def kernel(x_prompt, x_sample, c, cache_k, cache_v, state_gla_fwd, state_gla_bwd, c_ctx, w_ada, b_ada, norm_mix, norm_ff, w_in, w_gk_fwd, b_gk_fwd, w_gk_bwd, b_gk_bwd, gla_norm, q_norm, k_norm, sink, w_o, w_ff1, w_ff2):
    depth = w_in.shape[0]
    assert depth == 1, "single trunk layer"
    bp, tp, _ = x_prompt.shape
    bs, ts, _ = x_sample.shape
    n_p = bp * tp
    l = 0

    mod = _modulation(c_ctx[None, :], c, w_ada[l], b_ada[l])

    norm2 = gla_norm[l][None, :]
    gain_mix = norm_mix[l][None, :]
    gain_ff = norm_ff[l][None, :]
    sink_l = sink[l]

    w_in_t = jnp.transpose(w_in[l])
    cache_k_t = jnp.swapaxes(cache_k, 3, 4)
    cache_v_t = jnp.swapaxes(cache_v, 3, 4)

    xp = x_prompt.reshape(n_p, D_MODEL)
    xs = x_sample.reshape(bs * ts, D_MODEL)
    gla_in, gates, swa_in, new_k_t, new_v_t = _project(
        xp, xs, mod, gain_mix, w_in_t, w_gk_fwd[l], w_gk_bwd[l], b_gk_fwd[l][None, :], b_gk_bwd[l][None, :],
        q_norm[l][None, :], k_norm[l][None, :], _rope_tables(ts), tp, ts)
    new_k = jnp.swapaxes(new_k_t, 3, 4)
    new_v = jnp.swapaxes(new_v_t, 3, 4)

    og_p, s_f, s_b, osw_p, wo, w1, w2 = _gla(
        gla_in, gates, norm2, None, None, bp, tp, 0, True, seqs=2,
        attention=(sink_l, swa_in), cast_weights=(w_o[l], w_ff1[l], w_ff2[l]))
    og_s, osw_s = _gla(gla_in, gates, norm2, state_gla_fwd, state_gla_bwd, bs, ts, n_p // ts, False,
                       attention=(sink_l, swa_in, cache_k_t, cache_v_t))

    yp, ys = _mix_out(xp, xs, og_p, og_s, osw_p, osw_s, mod, gain_ff, wo, w1, w2, ts)
    return (yp.reshape(bp, tp, D_MODEL), ys.reshape(bs, ts, D_MODEL), new_k, new_v, s_f, s_b)
```

```python
import functools

import numpy as np
import jax
import jax.numpy as jnp
from jax import lax
from jax.experimental import pallas as pl
from jax.experimental.pallas import tpu as pltpu

F32 = jnp.float32
BF16 = jnp.bfloat16

D_MODEL = 1024
GRID_W = 64
HEAD_DIM = 64
GLA_HEADS = 8
GLA_WIDTH = GLA_HEADS * HEAD_DIM
GATE_RANK = 16
GATE_TEMP = 16.0
CHUNK = 64
GROUP = 4 * CHUNK
SWA_HEADS = 8
SWA_KV_HEADS = 2
SWA_GROUP = SWA_HEADS // SWA_KV_HEADS
SWA_WIDTH = SWA_HEADS * HEAD_DIM
KV_WIDTH = SWA_KV_HEADS * HEAD_DIM
WINDOW = 128
QBLOCK = 128
D_FF = 4 * D_MODEL
ROPE_THETA = 10000.0
EPS = 1e-6
NEG_INF = -1e30
LOG2E = 1.4426950408889634
LANES = 128
TM = 512
FF_TILE = 1024

D_IN = 2848
R_GQ, R_GK, R_GV, R_GG, R_LR = 0, 512, 1024, 1536, 2048
R_SQ = R_LR + 2 * GATE_RANK
R_SV = R_SQ + SWA_WIDTH + KV_WIDTH
C_TAIL = R_LR
C_LR = C_TAIL + D_IN - R_SQ
W_COLS = C_LR + LANES

V7X_VMEM_BYTES = 64 * 1024 * 1024
VMEM_LIMIT = V7X_VMEM_BYTES - 8 * 1024 * 1024


def _cparams(n_axes):
    return pltpu.CompilerParams(dimension_semantics=("arbitrary",) * n_axes, vmem_limit_bytes=VMEM_LIMIT)


def _mod_kernel(cctx_ref, c_ref, w_ref, b_ref, o_ref):
    k = pl.program_id(0)
    pad = jnp.zeros((8 - 1 - c_ref.shape[0], c_ref.shape[1]), F32)
    cnd = jnp.concatenate([cctx_ref[...], c_ref[...], pad], axis=0)
    s = (cnd * jax.nn.sigmoid(cnd)).astype(BF16)
    part = _dot(s, w_ref[...].astype(BF16))

    @pl.when(k == 0)
    def _():
        o_ref[...] = part + b_ref[...]

    @pl.when(k > 0)
    def _():
        o_ref[...] += part


def _modulation(c_ctx, c, w_ada, b_ada):
    tk = 256
    n_out = w_ada.shape[1]
    return pl.pallas_call(
        _mod_kernel,
        grid=(D_MODEL // tk,),
        in_specs=[pl.BlockSpec((1, tk), lambda k: (0, k)),
                  pl.BlockSpec((c.shape[0], tk), lambda k: (0, k)),
                  pl.BlockSpec((tk, n_out), lambda k: (k, 0)),
                  pl.BlockSpec((1, n_out), lambda k: (0, 0))],
        out_specs=pl.BlockSpec((8, n_out), lambda k: (0, 0)),
        out_shape=jax.ShapeDtypeStruct((8, n_out), F32),
        compiler_params=_cparams(1),
        name="modulation",
    )(c_ctx, c, w_ada, b_ada.reshape(1, n_out))


def _mod_row(tile, n_prompt_tiles, tiles_per_seq):
    return jnp.where(tile < n_prompt_tiles, 0, 1 + (tile - n_prompt_tiles) // tiles_per_seq)


def _mod_chunk(mod_ref, row, j):
    return mod_ref[pl.ds(row, 1), j * D_MODEL:(j + 1) * D_MODEL]


def _adaln(x, gain, shift, scale):
    ms = jnp.mean(x * x, axis=-1, keepdims=True)
    xn = x * lax.rsqrt(ms + EPS)
    return xn * (gain * (1.0 + scale)) + shift


def _head_rms_norm(y, gain):
    cols = y.shape[1]
    lane = lax.broadcasted_iota(jnp.int32, (1, LANES), 1)
    first = lane < HEAD_DIM
    outs = []
    for p in range(cols // LANES):
        s = y[:, p * LANES:(p + 1) * LANES]
        s = s * s
        tot = jnp.sum(s, axis=-1, keepdims=True)
        lo = jnp.sum(jnp.where(first, s, 0.0), axis=-1, keepdims=True)
        outs.append(jnp.where(first, lo, tot - lo))
    sums = outs[0] if len(outs) == 1 else jnp.concatenate(outs, axis=-1)
    return y * lax.rsqrt(sums + HEAD_DIM * EPS) * (gain * (HEAD_DIM ** 0.5))


def _log_sigmoid(x):
    return jnp.minimum(x, 0.0) - jnp.log(1.0 + jnp.exp(-jnp.abs(x)))


def _dot(a, b):
    return jnp.dot(a, b, preferred_element_type=F32)


def _dot_nt(a, b):
    return lax.dot_general(a, b, (((1,), (1,)), ((), ())), preferred_element_type=F32)


def _proj_kernel(xp_ref, xs_ref, mod_ref, gain_ref, w_ref, wgf_ref, wgb_ref, bgf_ref, bgb_ref, qn_ref, kn_ref,
                 cos_ref, sin_a_ref, sin_b_ref,
                 gla_ref, gate_ref, swa_ref, nk_ref, nv_ref,
                 wb_ref, wg_ref, h0_ref, h1_ref, lr0_ref, lr1_ref,
                 *, n_prompt_tiles, n_tiles, tiles_per_seq, seqs_per_tile, seq_len):
    h_refs = (h0_ref, h1_ref)
    lr_refs = (lr0_ref, lr1_ref)
    s = pl.program_id(0)

    @pl.when(s == 0)
    def _():
        moves = ([(c * LANES, c * LANES) for c in range(R_LR // LANES)]
                 + [(R_SQ + c * LANES, C_TAIL + c * LANES) for c in range((D_IN - R_SQ) // LANES)]
                 + [(R_LR, C_LR)])
        for src, dst in moves:
            wb_ref[:, dst:dst + LANES] = w_ref[src:src + LANES, :].T.astype(BF16)
        wg_ref[...] = jnp.zeros(wg_ref.shape, BF16)
        wg_ref[0:GATE_RANK, 0:GLA_WIDTH] = wgf_ref[...].astype(BF16)
        wg_ref[GATE_RANK:2 * GATE_RANK, GLA_WIDTH:] = wgb_ref[...].astype(BF16)

    def normalise_next(parity):
        x = jnp.where(s < n_prompt_tiles, xp_ref[...], xs_ref[...])
        row = _mod_row(jnp.minimum(s, n_tiles - 1), n_prompt_tiles, tiles_per_seq)
        shift, scale = _mod_chunk(mod_ref, row, 0), _mod_chunk(mod_ref, row, 1)
        h_refs[parity][...] = _adaln(x, gain_ref[...], shift, scale).astype(BF16)

    def gates(parity):
        gate_pre = _dot(lr_refs[parity][...], wg_ref[...])
        scale = LOG2E / GATE_TEMP
        gate_ref[:, :GLA_WIDTH] = (_log_sigmoid(gate_pre[:, :GLA_WIDTH] + bgf_ref[...]) * scale).astype(BF16)
        gate_ref[:, GLA_WIDTH:] = (_log_sigmoid(gate_pre[:, GLA_WIDTH:] + bgb_ref[...]) * scale).astype(BF16)

    def body(rope, emit_cache, parity):
        h_ref = h_refs[1 - parity]
        gates(parity)

        def proj(c0, c1):
            return _dot(h_ref[...], wb_ref[:, c0:c1])

        tail = proj(C_TAIL, C_LR)
        lr_refs[1 - parity][...] = proj(C_LR, W_COLS).astype(BF16)
        gg = proj(R_GG, R_LR)
        gla_ref[:, 3 * GLA_WIDTH:] = (gg * jax.nn.sigmoid(gg)).astype(BF16)
        attention_inputs(tail, rope, emit_cache)
        gla_ref[:, :GLA_WIDTH] = (proj(R_GQ, R_GK) * (HEAD_DIM ** -0.5)).astype(BF16)
        gla_ref[:, GLA_WIDTH:2 * GLA_WIDTH] = proj(R_GK, R_GV).astype(BF16)
        gla_ref[:, 2 * GLA_WIDTH:3 * GLA_WIDTH] = proj(R_GV, R_GG).astype(BF16)
        normalise_next(parity)

    def attention_inputs(tail, rope, emit_cache):
        qk = tail[:, :R_SV - R_SQ]
        qk_gain = jnp.concatenate([qn_ref[...]] * SWA_HEADS + [kn_ref[...]] * SWA_KV_HEADS, axis=-1)
        qk = _head_rms_norm(qk, qk_gain)
        if rope:
            reps = (SWA_WIDTH + KV_WIDTH) // LANES
            cos = jnp.concatenate([cos_ref[...]] * reps, axis=-1)
            sin_a = jnp.concatenate([sin_a_ref[...]] * reps, axis=-1)
            sin_b = jnp.concatenate([sin_b_ref[...]] * reps, axis=-1)
            width = qk.shape[1]
            quarter = HEAD_DIM // 4
            qk = (qk * cos + pltpu.roll(qk, quarter, 1) * sin_a
                  + pltpu.roll(qk, width - quarter, 1) * sin_b)
        swa_ref[:, :SWA_WIDTH] = (qk[:, :SWA_WIDTH] * (HEAD_DIM ** -0.5 * LOG2E)).astype(BF16)
        sk = qk[:, SWA_WIDTH:]
        swa_ref[:, SWA_WIDTH:SWA_WIDTH + KV_WIDTH] = sk.astype(BF16)
        sv = tail[:, R_SV - R_SQ:]
        swa_ref[:, SWA_WIDTH + KV_WIDTH:] = sv.astype(BF16)
        if emit_cache:
            sk_t, sv_t = sk.T, sv.T
            for q in range(seqs_per_tile):
                cols = slice(q * seq_len, (q + 1) * seq_len)
                for g in range(SWA_KV_HEADS):
                    feat = slice(g * HEAD_DIM, (g + 1) * HEAD_DIM)
                    nk_ref[q, 0, g, :, :] = sk_t[feat, cols]
                    nv_ref[q, 0, g, :, :] = sv_t[feat, cols]

    @pl.when(s == 0)
    def _():
        normalise_next(0)
        lr_refs[1][...] = jnp.zeros(lr_refs[1].shape, BF16)

    for parity in range(2):
        on_parity = (s % 2) == parity

        @pl.when((s >= 1) & (s <= n_prompt_tiles) & on_parity)
        def _():
            body(rope=False, emit_cache=True, parity=parity)

        @pl.when((s > n_prompt_tiles) & (s <= n_tiles) & on_parity)
        def _():
            body(rope=True, emit_cache=False, parity=parity)

    @pl.when(s == n_tiles + 1)
    def _():
        gates((n_tiles + 1) % 2)


def _project(xp, xs, mod, gain, w_in, w_gk_f, w_gk_b, b_gk_f, b_gk_b, q_norm, k_norm, rope_tabs, seq_p, seq_s):
    n_p, n_s = xp.shape[0], xs.shape[0]
    tp, ts = n_p // TM, n_s // TM
    n = n_p + n_s
    seqs_per_tile = TM // seq_p
    tiles_per_seq = seq_s // TM
    n_seq_p = n_p // seq_p
    const = lambda shape: pl.BlockSpec(shape, lambda s: (0,) * len(shape))
    n_tiles = tp + ts
    out_tile = lambda s: jnp.clip(s - 1, 0, n_tiles - 1)
    gate_tile = lambda s: jnp.maximum(s - 2, 0)
    rope_spec = pl.BlockSpec((TM, LANES), lambda s: (jnp.maximum(out_tile(s) - tp, 0) % tiles_per_seq, 0))
    in_specs = [pl.BlockSpec((TM, D_MODEL), lambda s: (jnp.minimum(s, tp - 1), 0)),
                pl.BlockSpec((TM, D_MODEL), lambda s: (jnp.clip(s - tp, 0, ts - 1), 0)),
                const(mod.shape),
                const((1, D_MODEL)),
                pl.BlockSpec((D_IN, D_MODEL), lambda s: (0, 0), pipeline_mode=pl.Buffered(1)),
                const(w_gk_f.shape), const(w_gk_b.shape), const(b_gk_f.shape), const(b_gk_b.shape),
                const(q_norm.shape), const(k_norm.shape),
                rope_spec, rope_spec, rope_spec]
    row = lambda w: pl.BlockSpec((TM, w), lambda s: (out_tile(s), 0))
    cache_spec = pl.BlockSpec((seqs_per_tile, 1, SWA_KV_HEADS, HEAD_DIM, seq_p),
                              lambda s: (jnp.minimum(out_tile(s), tp - 1), 0, 0, 0, 0))
    widths = (4 * GLA_WIDTH, 2 * GLA_WIDTH, SWA_WIDTH + 2 * KV_WIDTH)
    out_specs = [row(widths[0]), pl.BlockSpec((TM, widths[1]), lambda s: (gate_tile(s), 0)), row(widths[2]),
                 cache_spec, cache_spec]
    cache_shape = (n_seq_p, 1, SWA_KV_HEADS, HEAD_DIM, seq_p)
    out_shape = ([jax.ShapeDtypeStruct((n, w), BF16) for w in widths]
                 + [jax.ShapeDtypeStruct(cache_shape, F32)] * 2)
    return pl.pallas_call(
        functools.partial(_proj_kernel, n_prompt_tiles=tp, n_tiles=n_tiles, tiles_per_seq=tiles_per_seq,
                          seqs_per_tile=seqs_per_tile, seq_len=seq_p),
        grid=(n_tiles + 2,), in_specs=in_specs, out_specs=out_specs, out_shape=out_shape,
        scratch_shapes=([pltpu.VMEM((D_MODEL, W_COLS), BF16), pltpu.VMEM((LANES, 2 * GLA_WIDTH), BF16)]
                        + [pltpu.VMEM((TM, D_MODEL), BF16)] * 2 + [pltpu.VMEM((TM, LANES), BF16)] * 2),
        compiler_params=_cparams(1), name="adaln_in_proj",
    )(xp, xs, mod, gain, w_in, w_gk_f, w_gk_b, b_gk_f, b_gk_b, q_norm, k_norm, *rope_tabs)


def _gla_kernel(*refs, seq_len, seqs, has_state, emit_state, with_attention, n_cast):
    (q_ref, k_ref, v_ref, gf_ref, gb_ref, sgg_ref, norm_ref,
     trif_ref, trib_ref, keepf_ref, keepb_ref) = refs[:11]
    pos = 11
    if has_state:
        s0f_ref, s0b_ref = refs[pos:pos + 2]
        pos += 2
    if with_attention:
        sink_ref, aq_ref, ak_ref, av_ref = refs[pos:pos + 4]
        pos += 4
    if with_attention == "latent":
        kc_ref, vc_ref = refs[pos:pos + 2]
        pos += 2
    cast_in = refs[pos:pos + n_cast]
    pos += n_cast
    og_ref = refs[pos]
    pos += 1
    if emit_state:
        sf_ref, sb_ref = refs[pos:pos + 2]
        pos += 2
    if with_attention:
        osw_ref = refs[pos]
        pos += 1
    cast_out = refs[pos:pos + n_cast]
    pos += n_cast
    accf_ref, accb_ref = refs[pos:pos + 2]

    for src, dst in zip(cast_in, cast_out):
        dst[...] = src[...].astype(BF16)

    n_groups = seq_len // GROUP
    per_group = GROUP // CHUNK
    n_pairs = q_ref.shape[1] // LANES
    lane = lax.broadcasted_iota(jnp.int32, (1, LANES), 1)
    head0 = lane < HEAD_DIM
    m0 = jnp.where(head0, 1.0, 0.0).astype(BF16)
    m1 = jnp.where(head0, 0.0, 1.0).astype(BF16)
    row_top = lax.broadcasted_iota(jnp.int32, (LANES, 1), 0) < HEAD_DIM

    def stack_heads(x):
        return jnp.concatenate([x * m0, x * m1], axis=0)

    def direction(row0, lanes, g_ref, tri_ref, keep_ref, state, acc_ref, backward):
        rows = pl.ds(row0, GROUP)
        b = _dot(tri_ref[...], g_ref[rows, lanes])
        yield
        b = b.reshape(per_group, CHUNK, LANES)
        edge = 0 if backward else CHUNK - 1
        mid = CHUNK // 2
        b_edge = b[:, edge:edge + 1, :]
        b_mid = b[:, mid:mid + 1, :]
        q = q_ref[rows, lanes].astype(F32).reshape(per_group, CHUNK, LANES)
        k = k_ref[rows, lanes].astype(F32).reshape(per_group, CHUNK, LANES)
        v = v_ref[rows, lanes]
        q_mid = q * jnp.exp2(b - b_mid)
        k_mid = k * jnp.exp2(b_mid - b)
        q_in = q_mid.astype(BF16)
        k_in = k_mid.astype(BF16).reshape(GROUP, LANES)
        q_st = (q_mid * jnp.exp2(b_mid)).astype(BF16)
        k_st = (k_mid * jnp.exp2(b_edge - b_mid)).astype(BF16).reshape(GROUP, LANES)
        decay = jnp.exp2(b_edge)
        keep = keep_ref[...]
        pair_rows = 2 * CHUNK
        n_cp = per_group // 2
        yield
        scores, kvs = [], []
        for p in range(n_cp):
            c0 = 2 * p
            prow = slice(p * pair_rows, (p + 1) * pair_rows)
            q_pair = jnp.concatenate([stack_heads(q_in[c0]), stack_heads(q_in[c0 + 1])], axis=0)
            scores.append(_dot_nt(q_pair, k_in[prow, :]).astype(BF16) * keep)
            v_t = v[prow, :].astype(F32).T.astype(BF16)
            k_pair = k_st[prow, :]
            zero = jnp.zeros_like(k_pair)
            k_blk = jnp.concatenate([jnp.where(row_top, k_pair, zero), jnp.where(row_top, zero, k_pair)], axis=-1)
            kvs.append(_dot(v_t, k_blk))
        yield
        intras = [_dot(scores[p], v[p * pair_rows:(p + 1) * pair_rows, :]) for p in range(n_cp)]
        yield
        for p in (range(n_cp - 1, -1, -1) if backward else range(n_cp)):
            for j in ((1, 0) if backward else (0, 1)):
                c = 2 * p + j
                inter = _dot_nt(stack_heads(q_st[c]), state.astype(BF16))
                tot = intras[p][j * pair_rows:(j + 1) * pair_rows, :] + inter
                acc_ref[pl.ds(row0 + c * CHUNK, CHUNK), lanes] = jnp.where(head0, tot[:CHUNK, :], tot[CHUNK:, :])
                state = state * decay[c] + kvs[p][:, j * LANES:(j + 1) * LANES]
            yield
        return state

    def load_state(s_ref, q, p):
        z = jnp.zeros((HEAD_DIM, HEAD_DIM), F32)
        top = jnp.concatenate([s_ref[q, 0, 2 * p, :, :].T, z], axis=-1)
        bot = jnp.concatenate([z, s_ref[q, 0, 2 * p + 1, :, :].T], axis=-1)
        return jnp.concatenate([top, bot], axis=0)

    per_dir = seqs * n_pairs
    if has_state:
        init = tuple(load_state(s_ref, q, p)
                     for s_ref in (s0f_ref, s0b_ref) for q in range(seqs) for p in range(n_pairs))
    else:
        init = (jnp.zeros((LANES, LANES), F32),) * (2 * per_dir)

    def body(t, carry):
        row_f = t * GROUP
        row_b = (n_groups - 1 - t) * GROUP
        if n_groups > 1:
            row_f = pl.multiple_of(row_f, GROUP)
            row_b = pl.multiple_of(row_b, GROUP)
        units = []
        for backward, row, g_ref, tri_ref, keep_ref, acc_ref in (
                (False, row_f, gf_ref, trif_ref, keepf_ref, accf_ref),
                (True, row_b, gb_ref, trib_ref, keepb_ref, accb_ref)):
            for q in range(seqs):
                for p in range(n_pairs):
                    lanes = slice(p * LANES, (p + 1) * LANES)
                    state = carry[backward * per_dir + q * n_pairs + p]
                    units.append(direction(q * seq_len + row, lanes, g_ref, tri_ref, keep_ref, state, acc_ref,
                                           backward))
        return units

    norm_pair = jnp.concatenate([norm_ref[...]] * 2, axis=-1)

    def finish_outputs(rows):
        for p in range(n_pairs):
            lanes = slice(p * LANES, (p + 1) * LANES)
            o = _head_rms_norm(accf_ref[rows, lanes] + accb_ref[rows, lanes], norm_pair)
            og_ref[rows, lanes] = (o * sgg_ref[rows, lanes].astype(F32)).astype(BF16)
            yield

    if n_groups == 1:
        gla_units = body(0, init)
        final = [None] * len(gla_units)
        for q in range(seqs + 1):
            rows = slice(q * seq_len, (q + 1) * seq_len)
            wave, ids = [], []
            if q < seqs:
                ids = [d * per_dir + q * n_pairs + p for d in range(2) for p in range(n_pairs)]
                wave = [gla_units[u] for u in ids]
                if with_attention:
                    wave += _context_attention_units(sink_ref, aq_ref, ak_ref, av_ref, rows)
            if q > 0:
                wave.append(finish_outputs(slice((q - 1) * seq_len, q * seq_len)))
            results = _run_staged(wave)
            for u, r in zip(ids, results):
                final[u] = r
            if q < seqs and with_attention:
                heads = [o for outs in results[len(ids):len(ids) + SWA_KV_HEADS] for o in outs]
                osw_ref[rows, :] = jnp.concatenate(heads, axis=-1).astype(BF16)
    else:
        assert with_attention != "context" and seqs == 1
        if with_attention:
            ctx_kv = [(kc_ref[0, 0, g, :, :].astype(BF16), vc_ref[0, 0, g, :, :].astype(BF16))
                      for g in range(SWA_KV_HEADS)]
            q_blocks = GROUP // QBLOCK

        def loop_body(t, carry):
            units = body(t, carry)
            n_gla = len(units)
            if with_attention:
                for j in range(q_blocks):
                    units += _latent_attention_units(sink_ref, aq_ref, ak_ref, av_ref, ctx_kv,
                                                     t * q_blocks + j, seq_len)
            results = _run_staged(units)
            if with_attention:
                for j in range(q_blocks):
                    heads = [o for outs in results[n_gla + j * SWA_KV_HEADS:n_gla + (j + 1) * SWA_KV_HEADS]
                             for o in outs]
                    rows = pl.ds(pl.multiple_of((t * q_blocks + j) * QBLOCK, QBLOCK), QBLOCK)
                    osw_ref[rows, :] = jnp.concatenate(heads, axis=-1).astype(BF16)
            return tuple(results[:n_gla])

        final = lax.fori_loop(0, n_groups, loop_body, init)
        _run_staged([finish_outputs(slice(0, seq_len))])
    if emit_state:
        for d, s_ref in enumerate((sf_ref, sb_ref)):
            for q in range(seqs):
                for p in range(n_pairs):
                    fin = final[d * per_dir + q * n_pairs + p]
                    s_ref[q, 0, 2 * p, :, :] = fin[:HEAD_DIM, :HEAD_DIM].T
                    s_ref[q, 0, 2 * p + 1, :, :] = fin[HEAD_DIM:, HEAD_DIM:].T


def _gla_constants():
    i = np.arange(GROUP)[:, None]
    j = np.arange(GROUP)[None, :]
    same = (i // CHUNK) == (j // CHUNK)
    tri_f = same & (j <= i)
    tri_b = same & (j >= i)
    r = np.arange(4 * CHUNK)[:, None]
    l = np.arange(LANES)[None, :]
    same = (r // (2 * CHUNK)) == (l // CHUNK)
    keep_f = same & ((l % CHUNK) <= (r % CHUNK))
    keep_b = same & ((l % CHUNK) >= (r % CHUNK))
    return tuple(jnp.asarray(m.astype(np.float32), dtype=BF16) for m in (tri_f, tri_b, keep_f, keep_b))


def _gla(gla_in, gates, norm_row, s0f, s0b, n_seq, seq_len, row_block0, emit_state, seqs=1,
         attention=None, cast_weights=()):
    has_state = s0f is not None
    steps = n_seq // seqs
    rows_per_step = seqs * seq_len
    row_blk = row_block0 // seqs
    col = lambda width, j: pl.BlockSpec((rows_per_step, width), lambda b: (row_blk + b, j))
    out_blk = pl.BlockSpec((rows_per_step, GLA_WIDTH), lambda b: (b, 0))
    st_spec = pl.BlockSpec((seqs, 1, GLA_HEADS, HEAD_DIM, HEAD_DIM), lambda b: (b, 0, 0, 0, 0))
    const = lambda shape: pl.BlockSpec(shape, lambda b: (0, 0))
    in_specs = ([col(GLA_WIDTH, 0), col(GLA_WIDTH, 1), col(GLA_WIDTH, 2), col(GLA_WIDTH, 0), col(GLA_WIDTH, 1),
                 col(GLA_WIDTH, 3)]
                + [const(norm_row.shape)] + [const((GROUP, GROUP))] * 2 + [const((4 * CHUNK, LANES))] * 2)
    args = [gla_in, gla_in, gla_in, gates, gates, gla_in, norm_row, *_gla_constants()]
    if has_state:
        in_specs += [st_spec] * 2
        args += [s0f, s0b]
    out_specs = [out_blk]
    out_shape = [jax.ShapeDtypeStruct((n_seq * seq_len, GLA_WIDTH), BF16)]
    if emit_state:
        out_specs += [st_spec] * 2
        out_shape += [jax.ShapeDtypeStruct((n_seq, 1, GLA_HEADS, HEAD_DIM, HEAD_DIM), F32)] * 2
    with_attention = None
    if attention is not None:
        sink, swa_in = attention[:2]
        with_attention = "context"
        in_specs += [pl.BlockSpec(memory_space=pltpu.SMEM), col(SWA_WIDTH, 0),
                     col(KV_WIDTH, SWA_WIDTH // KV_WIDTH), col(KV_WIDTH, SWA_WIDTH // KV_WIDTH + 1)]
        args += [sink, swa_in, swa_in, swa_in]
        if len(attention) > 2:
            with_attention = "latent"
            cache_k_t, cache_v_t = attention[2:]
            past = cache_k_t.shape[4]
            cache_spec = pl.BlockSpec((seqs, 1, SWA_KV_HEADS, HEAD_DIM, past), lambda b: (b, 0, 0, 0, 0))
            in_specs += [cache_spec] * 2
            args += [cache_k_t, cache_v_t]
        out_specs.append(pl.BlockSpec((rows_per_step, SWA_WIDTH), lambda b: (b, 0)))
        out_shape.append(jax.ShapeDtypeStruct((n_seq * seq_len, SWA_WIDTH), BF16))
    cast_specs, cast_shapes = _cast_specs(cast_weights, steps)
    in_specs += cast_specs
    args += list(cast_weights)
    out_specs += cast_specs
    out_shape += cast_shapes
    return pl.pallas_call(
        functools.partial(_gla_kernel, seq_len=seq_len, seqs=seqs, has_state=has_state, emit_state=emit_state,
                          with_attention=with_attention, n_cast=len(cast_weights)),
        grid=(steps,), in_specs=in_specs, out_specs=out_specs, out_shape=out_shape,
        scratch_shapes=[pltpu.VMEM((rows_per_step, GLA_WIDTH), F32)] * 2,
        compiler_params=_cparams(1), name="gla_bidirectional",
    )(*args)


def _group_attention(q, heads, sink_ref, score_fn, value_fn):
    rows = q.shape[0]
    q_stack = jnp.concatenate([q[:, h * HEAD_DIM:(h + 1) * HEAD_DIM] for h in heads], axis=0)
    scores = score_fn(q_stack)
    yield
    slabs = [slice(j * rows, (j + 1) * rows) for j in range(len(heads))]
    ms = []
    sinks = [sink_ref[h] * LOG2E for h in heads]
    for slab, sink in zip(slabs, sinks):
        m = sink
        for s in scores:
            m = jnp.maximum(m, jnp.max(s[slab, :], axis=-1, keepdims=True))
        ms.append(m)
    yield
    e_parts = [[] for _ in scores]
    denoms = []
    for slab, sink, m in zip(slabs, sinks, ms):
        denom = jnp.exp2(sink - m)
        for part, s in zip(e_parts, scores):
            e = jnp.exp2(s[slab, :] - m)
            denom = denom + jnp.sum(e, axis=-1, keepdims=True)
            part.append(e.astype(BF16))
        denoms.append(denom)
    yield
    acc = value_fn([jnp.concatenate(part, axis=0) for part in e_parts])
    yield
    return [acc[slab, :] / denom for slab, denom in zip(slabs, denoms)]


def _run_staged(units):
    results = [None] * len(units)
    live = list(range(len(units)))
    while live:
        still = []
        for u in live:
            try:
                next(units[u])
                still.append(u)
            except StopIteration as stop:
                results[u] = stop.value
        live = still
    return results


def _context_attention_units(sink_ref, q_ref, k_ref, v_ref, rows):
    q = q_ref[rows, :]
    k = k_ref[rows, :]
    v = v_ref[rows, :]
    units = []
    for g in range(SWA_KV_HEADS):
        k_g = k[:, g * HEAD_DIM:(g + 1) * HEAD_DIM]
        v_g = v[:, g * HEAD_DIM:(g + 1) * HEAD_DIM]
        units.append(_group_attention(
            q, list(range(SWA_GROUP * g, SWA_GROUP * (g + 1))), sink_ref,
            lambda qs, k_g=k_g: [_dot_nt(qs, k_g)],
            lambda es, v_g=v_g: _dot(es[0], v_g)))
    return units


def _cast_specs(cast_weights, steps, step_of=lambda *idx: idx[0]):
    specs = [pl.BlockSpec((w.shape[0] // steps, w.shape[1]), lambda *idx: (step_of(*idx), 0))
             for w in cast_weights]
    shapes = [jax.ShapeDtypeStruct(w.shape, BF16) for w in cast_weights]
    return specs, shapes


def _latent_attention_units(sink_ref, q_ref, k_ref, v_ref, ctx_kv, i, seq_len):
    span = QBLOCK + 2 * WINDOW
    start = pl.multiple_of(jnp.clip(i * QBLOCK - WINDOW, 0, seq_len - span), QBLOCK)
    q = q_ref[pl.ds(pl.multiple_of(i * QBLOCK, QBLOCK), QBLOCK), :]
    k = k_ref[pl.ds(start, span), :]
    v = v_ref[pl.ds(start, span), :]
    qpos = i * QBLOCK + lax.broadcasted_iota(jnp.int32, (QBLOCK, span), 0)
    kpos = start + lax.broadcasted_iota(jnp.int32, (QBLOCK, span), 1)
    valid = jnp.abs(qpos - kpos) <= WINDOW
    valid = jnp.concatenate([valid] * SWA_GROUP, axis=0)
    units = []
    for g in range(SWA_KV_HEADS):
        k_g = k[:, g * HEAD_DIM:(g + 1) * HEAD_DIM]
        v_g = v[:, g * HEAD_DIM:(g + 1) * HEAD_DIM]
        kc_t, vc_t = ctx_kv[g]
        units.append(_group_attention(
            q, list(range(SWA_GROUP * g, SWA_GROUP * (g + 1))), sink_ref,
            lambda qs, k_g=k_g, kc_t=kc_t: [jnp.where(valid, _dot_nt(qs, k_g), NEG_INF), _dot(qs, kc_t)],
            lambda es, v_g=v_g, vc_t=vc_t: _dot(es[0], v_g) + _dot_nt(es[1], vc_t)))
    return units


def _out_kernel(xp_ref, xs_ref, ogp_ref, ogs_ref, oswp_ref, osws_ref, mod_ref, gain_ref,
                wo_ref, w1_ref, w2_ref, yp_ref, ys_ref, *, n_prompt_tiles, tiles_per_seq):
    i = pl.program_id(0)
    row = _mod_row(i, n_prompt_tiles, tiles_per_seq)

    def body(x_ref, og_ref, osw_ref, y_ref):
        mix = _dot(og_ref[...], wo_ref[:GLA_WIDTH, :]) + _dot(osw_ref[...], wo_ref[GLA_WIDTH:, :])
        x1 = x_ref[...] + _mod_chunk(mod_ref, row, 2) * mix
        h = _adaln(x1, gain_ref[...], _mod_chunk(mod_ref, row, 3), _mod_chunk(mod_ref, row, 4)).astype(BF16)
        acc = None
        for j in range(D_FF // FF_TILE):
            cols = slice(j * FF_TILE, (j + 1) * FF_TILE)
            a = jnp.maximum(_dot(h, w1_ref[:, cols]), 0.0)
            part = _dot((a * a).astype(BF16), w2_ref[cols, :])
            acc = part if acc is None else acc + part
        y_ref[...] = x1 + _mod_chunk(mod_ref, row, 5) * acc

    @pl.when(i < n_prompt_tiles)
    def _():
        body(xp_ref, ogp_ref, oswp_ref, yp_ref)

    @pl.when(i >= n_prompt_tiles)
    def _():
        body(xs_ref, ogs_ref, osws_ref, ys_ref)


def _mix_out(xp, xs, og_p, og_s, osw_p, osw_s, mod, gain, wo, w1, w2, seq_s):
    n_p, n_s = xp.shape[0], xs.shape[0]
    tp, ts = n_p // TM, n_s // TM
    tiles_per_seq = seq_s // TM
    const = lambda shape: pl.BlockSpec(shape, lambda i: (0,) * len(shape), pipeline_mode=pl.Buffered(1))
    row_p = lambda w: pl.BlockSpec((TM, w), lambda i: (jnp.minimum(i, tp - 1), 0))
    row_s = lambda w: pl.BlockSpec((TM, w), lambda i: (jnp.maximum(i - tp, 0), 0))
    return pl.pallas_call(
        functools.partial(_out_kernel, n_prompt_tiles=tp, tiles_per_seq=tiles_per_seq),
        grid=(tp + ts,),
        in_specs=[row_p(D_MODEL), row_s(D_MODEL), row_p(GLA_WIDTH), row_s(GLA_WIDTH),
                  row_p(SWA_WIDTH), row_s(SWA_WIDTH),
                  const(mod.shape),
                  const((1, D_MODEL)), const((GLA_WIDTH + SWA_WIDTH, D_MODEL)),
                  const((D_MODEL, D_FF)), const((D_FF, D_MODEL))],
        out_specs=[row_p(D_MODEL), row_s(D_MODEL)],
        out_shape=[jax.ShapeDtypeStruct((n_p, D_MODEL), F32), jax.ShapeDtypeStruct((n_s, D_MODEL), F32)],
        compiler_params=_cparams(1), name="out_proj_mlp",
    )(xp, xs, og_p, og_s, osw_p, osw_s, mod, gain, wo, w1, w2)


def _rope_tables(seq_len):
    axis_dim = HEAD_DIM // 2
    half = axis_dim // 2
    t = np.arange(seq_len)
    rowp = (t // GRID_W).astype(np.float64)
    colp = (t % GRID_W).astype(np.float64)
    freqs = ROPE_THETA ** (-np.arange(half, dtype=np.float64) / half)
    lane = np.arange(LANES)
    d = lane % HEAD_DIM
    posv = np.where((d // axis_dim)[None, :] == 0, rowp[:, None], colp[:, None])
    ang = posv * freqs[d % half][None, :]
    second = ((d % axis_dim) >= half)[None, :]
    cos, sin = np.cos(ang), np.sin(ang)
    zero = np.zeros_like(sin)
    tabs = (cos, np.where(second, sin, zero), np.where(second, zero, -sin))
    return tuple(jnp.asarray(tab.astype(np.float32)) for tab in tabs)


def kernel(x_prompt, x_sample, c, cache_k, cache_v, state_gla_fwd, state_gla_bwd, c_ctx, w_ada, b_ada, norm_mix, norm_ff, w_in, w_gk_fwd, b_gk_fwd, w_gk_bwd, b_gk_bwd, gla_norm, q_norm, k_norm, sink, w_o, w_ff1, w_ff2):
    depth = w_in.shape[0]
    assert depth == 1, "single trunk layer"
    bp, tp, _ = x_prompt.shape
    bs, ts, _ = x_sample.shape
    n_p = bp * tp
    l = 0

    mod = _modulation(c_ctx[None, :], c, w_ada[l], b_ada[l])

    norm2 = gla_norm[l][None, :]
    gain_mix = norm_mix[l][None, :]
    gain_ff = norm_ff[l][None, :]
    sink_l = sink[l]

    w_in_t = jnp.transpose(w_in[l])
    cache_k_t = jnp.swapaxes(cache_k, 3, 4)
    cache_v_t = jnp.swapaxes(cache_v, 3, 4)

    xp = x_prompt.reshape(n_p, D_MODEL)
    xs = x_sample.reshape(bs * ts, D_MODEL)
    gla_in, gates, swa_in, new_k_t, new_v_t = _project(
        xp, xs, mod, gain_mix, w_in_t, w_gk_fwd[l], w_gk_bwd[l], b_gk_fwd[l][None, :], b_gk_bwd[l][None, :],
        q_norm[l][None, :], k_norm[l][None, :], _rope_tables(ts), tp, ts)
    new_k = jnp.swapaxes(new_k_t, 3, 4)
    new_v = jnp.swapaxes(new_v_t, 3, 4)

    og_p, s_f, s_b, osw_p, wo, w1, w2 = _gla(
        gla_in, gates, norm2, None, None, bp, tp, 0, True, seqs=2,
        attention=(sink_l, swa_in), cast_weights=(w_o[l], w_ff1[l], w_ff2[l]))
    og_s, osw_s = _gla(gla_in, gates, norm2, state_gla_fwd, state_gla_bwd, bs, ts, n_p // ts, False,
                       attention=(sink_l, swa_in, cache_k_t, cache_v_t))

    yp, ys = _mix_out(xp, xs, og_p, og_s, osw_p, osw_s, mod, gain_ff, wo, w1, w2, ts)
    return (yp.reshape(bp, tp, D_MODEL), ys.reshape(bs, ts, D_MODEL), new_k, new_v, s_f, s_b)
```

```python
import functools

import numpy as np
import jax
import jax.numpy as jnp
from jax import lax
from jax.experimental import pallas as pl
from jax.experimental.pallas import tpu as pltpu

F32 = jnp.float32
BF16 = jnp.bfloat16

D_MODEL = 1024
GRID_W = 64
HEAD_DIM = 64
GLA_HEADS = 8
GLA_WIDTH = GLA_HEADS * HEAD_DIM
GATE_RANK = 16
GATE_TEMP = 16.0
CHUNK = 64
GROUP = 4 * CHUNK
SWA_HEADS = 8
SWA_KV_HEADS = 2
SWA_GROUP = SWA_HEADS // SWA_KV_HEADS
SWA_WIDTH = SWA_HEADS * HEAD_DIM
KV_WIDTH = SWA_KV_HEADS * HEAD_DIM
WINDOW = 128
QBLOCK = 128
D_FF = 4 * D_MODEL
ROPE_THETA = 10000.0
EPS = 1e-6
NEG_INF = -1e30
LOG2E = 1.4426950408889634
LANES = 128
TM = 512
FF_TILE = 1024

D_IN = 2848
R_GQ, R_GK, R_GV, R_GG, R_LR = 0, 512, 1024, 1536, 2048
R_SQ = R_LR + 2 * GATE_RANK
R_SV = R_SQ + SWA_WIDTH + KV_WIDTH
C_TAIL = R_LR
C_LR = C_TAIL + D_IN - R_SQ
W_COLS = C_LR + LANES
W_MOVES = ([(c * LANES, c * LANES) for c in range(R_LR // LANES)]
           + [(R_SQ + c * LANES, C_TAIL + c * LANES) for c in range((D_IN - R_SQ) // LANES)]
           + [(R_LR, C_LR)])

V7X_VMEM_BYTES = 64 * 1024 * 1024
VMEM_LIMIT = V7X_VMEM_BYTES - 8 * 1024 * 1024


def _cparams(n_axes):
    return pltpu.CompilerParams(dimension_semantics=("arbitrary",) * n_axes, vmem_limit_bytes=VMEM_LIMIT)


def _mod_kernel(cctx_ref, c_ref, w_ref, b_ref, o_ref):
    k = pl.program_id(0)
    pad = jnp.zeros((8 - 1 - c_ref.shape[0], c_ref.shape[1]), F32)
    cnd = jnp.concatenate([cctx_ref[...], c_ref[...], pad], axis=0)
    s = (cnd * jax.nn.sigmoid(cnd)).astype(BF16)
    part = _dot(s, w_ref[...].astype(BF16))

    @pl.when(k == 0)
    def _():
        o_ref[...] = part + b_ref[...]

    @pl.when(k > 0)
    def _():
        o_ref[...] += part


def _modulation(c_ctx, c, w_ada, b_ada):
    tk = 256
    n_out = w_ada.shape[1]
    return pl.pallas_call(
        _mod_kernel,
        grid=(D_MODEL // tk,),
        in_specs=[pl.BlockSpec((1, tk), lambda k: (0, k)),
                  pl.BlockSpec((c.shape[0], tk), lambda k: (0, k)),
                  pl.BlockSpec((tk, n_out), lambda k: (k, 0)),
                  pl.BlockSpec((1, n_out), lambda k: (0, 0))],
        out_specs=pl.BlockSpec((8, n_out), lambda k: (0, 0)),
        out_shape=jax.ShapeDtypeStruct((8, n_out), F32),
        compiler_params=_cparams(1),
        name="modulation",
    )(c_ctx, c, w_ada, b_ada.reshape(1, n_out))


def _mod_row(tile, n_prompt_tiles, tiles_per_seq):
    return jnp.where(tile < n_prompt_tiles, 0, 1 + (tile - n_prompt_tiles) // tiles_per_seq)


def _mod_chunk(mod_ref, row, j):
    return mod_ref[pl.ds(row, 1), j * D_MODEL:(j + 1) * D_MODEL]


def _adaln(x, gain, shift, scale):
    ms = jnp.mean(x * x, axis=-1, keepdims=True)
    xn = x * lax.rsqrt(ms + EPS)
    return xn * (gain * (1.0 + scale)) + shift


def _head_rms_norm(y, gain):
    cols = y.shape[1]
    lane = lax.broadcasted_iota(jnp.int32, (1, LANES), 1)
    first = lane < HEAD_DIM
    outs = []
    for p in range(cols // LANES):
        s = y[:, p * LANES:(p + 1) * LANES]
        s = s * s
        tot = jnp.sum(s, axis=-1, keepdims=True)
        lo = jnp.sum(jnp.where(first, s, 0.0), axis=-1, keepdims=True)
        outs.append(jnp.where(first, lo, tot - lo))
    sums = outs[0] if len(outs) == 1 else jnp.concatenate(outs, axis=-1)
    return y * lax.rsqrt(sums + HEAD_DIM * EPS) * (gain * (HEAD_DIM ** 0.5))


def _log_sigmoid(x):
    return jnp.minimum(x, 0.0) - jnp.log(1.0 + jnp.exp(-jnp.abs(x)))


def _dot(a, b):
    return jnp.dot(a, b, preferred_element_type=F32)


def _dot_nt(a, b):
    return lax.dot_general(a, b, (((1,), (1,)), ((), ())), preferred_element_type=F32)


def _proj_kernel(xp_ref, xs_ref, mod_ref, gain_ref, w_hbm, wgf_ref, wgb_ref, bgf_ref, bgb_ref, qn_ref, kn_ref,
                 cos_ref, sin_a_ref, sin_b_ref,
                 gla_ref, gate_ref, swa_ref, nk_ref, nv_ref,
                 wb_ref, wg_ref, h0_ref, h1_ref, lr0_ref, lr1_ref, stage_ref, sems,
                 *, n_prompt_tiles, n_tiles, tiles_per_seq, seqs_per_tile, seq_len):
    h_refs = (h0_ref, h1_ref)
    lr_refs = (lr0_ref, lr1_ref)
    s = pl.program_id(0)

    def load_weights():
        copies = [pltpu.make_async_copy(w_hbm.at[pl.ds(src, LANES), :], stage_ref.at[k], sems.at[k])
                  for k, (src, _) in enumerate(W_MOVES)]
        for copy in copies:
            copy.start()
        normalise_next(0)
        lr_refs[1][...] = jnp.zeros(lr_refs[1].shape, BF16)
        wg_ref[...] = jnp.zeros(wg_ref.shape, BF16)
        wg_ref[0:GATE_RANK, 0:GLA_WIDTH] = wgf_ref[...].astype(BF16)
        wg_ref[GATE_RANK:2 * GATE_RANK, GLA_WIDTH:] = wgb_ref[...].astype(BF16)
        for k, (copy, (_, dst)) in enumerate(zip(copies, W_MOVES)):
            copy.wait()
            wb_ref[:, dst:dst + LANES] = stage_ref[k].T.astype(BF16)

    def normalise_next(parity):
        x = jnp.where(s < n_prompt_tiles, xp_ref[...], xs_ref[...])
        row = _mod_row(jnp.minimum(s, n_tiles - 1), n_prompt_tiles, tiles_per_seq)
        shift, scale = _mod_chunk(mod_ref, row, 0), _mod_chunk(mod_ref, row, 1)
        h_refs[parity][...] = _adaln(x, gain_ref[...], shift, scale).astype(BF16)

    def gates(parity):
        gate_pre = _dot(lr_refs[parity][...], wg_ref[...])
        scale = LOG2E / GATE_TEMP
        gate_ref[:, :GLA_WIDTH] = (_log_sigmoid(gate_pre[:, :GLA_WIDTH] + bgf_ref[...]) * scale).astype(BF16)
        gate_ref[:, GLA_WIDTH:] = (_log_sigmoid(gate_pre[:, GLA_WIDTH:] + bgb_ref[...]) * scale).astype(BF16)

    def body(rope, emit_cache, parity):
        h_ref = h_refs[1 - parity]
        gates(parity)

        def proj(c0, c1):
            return _dot(h_ref[...], wb_ref[:, c0:c1])

        tail = proj(C_TAIL, C_LR)
        lr_refs[1 - parity][...] = proj(C_LR, W_COLS).astype(BF16)
        gg = proj(R_GG, R_LR)
        gla_ref[:, 3 * GLA_WIDTH:] = (gg * jax.nn.sigmoid(gg)).astype(BF16)
        attention_inputs(tail, rope, emit_cache)
        gla_ref[:, :GLA_WIDTH] = (proj(R_GQ, R_GK) * (HEAD_DIM ** -0.5)).astype(BF16)
        gla_ref[:, GLA_WIDTH:2 * GLA_WIDTH] = proj(R_GK, R_GV).astype(BF16)
        gla_ref[:, 2 * GLA_WIDTH:3 * GLA_WIDTH] = proj(R_GV, R_GG).astype(BF16)
        normalise_next(parity)

    def attention_inputs(tail, rope, emit_cache):
        qk = tail[:, :R_SV - R_SQ]
        qk_gain = jnp.concatenate([qn_ref[...]] * SWA_HEADS + [kn_ref[...]] * SWA_KV_HEADS, axis=-1)
        qk = _head_rms_norm(qk, qk_gain)
        if rope:
            reps = (SWA_WIDTH + KV_WIDTH) // LANES
            cos = jnp.concatenate([cos_ref[...]] * reps, axis=-1)
            sin_a = jnp.concatenate([sin_a_ref[...]] * reps, axis=-1)
            sin_b = jnp.concatenate([sin_b_ref[...]] * reps, axis=-1)
            width = qk.shape[1]
            quarter = HEAD_DIM // 4
            qk = (qk * cos + pltpu.roll(qk, quarter, 1) * sin_a
                  + pltpu.roll(qk, width - quarter, 1) * sin_b)
        swa_ref[:, :SWA_WIDTH] = (qk[:, :SWA_WIDTH] * (HEAD_DIM ** -0.5 * LOG2E)).astype(BF16)
        sk = qk[:, SWA_WIDTH:]
        swa_ref[:, SWA_WIDTH:SWA_WIDTH + KV_WIDTH] = sk.astype(BF16)
        sv = tail[:, R_SV - R_SQ:]
        swa_ref[:, SWA_WIDTH + KV_WIDTH:] = sv.astype(BF16)
        if emit_cache:
            sk_t, sv_t = sk.T, sv.T
            for q in range(seqs_per_tile):
                cols = slice(q * seq_len, (q + 1) * seq_len)
                for g in range(SWA_KV_HEADS):
                    feat = slice(g * HEAD_DIM, (g + 1) * HEAD_DIM)
                    nk_ref[q, 0, g, :, :] = sk_t[feat, cols]
                    nv_ref[q, 0, g, :, :] = sv_t[feat, cols]

    @pl.when(s == 0)
    def _():
        load_weights()

    for parity in range(2):
        on_parity = (s % 2) == parity

        @pl.when((s >= 1) & (s <= n_prompt_tiles) & on_parity)
        def _():
            body(rope=False, emit_cache=True, parity=parity)

        @pl.when((s > n_prompt_tiles) & (s <= n_tiles) & on_parity)
        def _():
            body(rope=True, emit_cache=False, parity=parity)

    @pl.when(s == n_tiles + 1)
    def _():
        gates((n_tiles + 1) % 2)


def _project(xp, xs, mod, gain, w_in, w_gk_f, w_gk_b, b_gk_f, b_gk_b, q_norm, k_norm, rope_tabs, seq_p, seq_s):
    n_p, n_s = xp.shape[0], xs.shape[0]
    tp, ts = n_p // TM, n_s // TM
    n = n_p + n_s
    seqs_per_tile = TM // seq_p
    tiles_per_seq = seq_s // TM
    n_seq_p = n_p // seq_p
    const = lambda shape: pl.BlockSpec(shape, lambda s: (0,) * len(shape))
    n_tiles = tp + ts
    out_tile = lambda s: jnp.clip(s - 1, 0, n_tiles - 1)
    gate_tile = lambda s: jnp.maximum(s - 2, 0)
    rope_spec = pl.BlockSpec((TM, LANES), lambda s: (jnp.maximum(out_tile(s) - tp, 0) % tiles_per_seq, 0))
    in_specs = [pl.BlockSpec((TM, D_MODEL), lambda s: (jnp.minimum(s, tp - 1), 0)),
                pl.BlockSpec((TM, D_MODEL), lambda s: (jnp.clip(s - tp, 0, ts - 1), 0)),
                const(mod.shape),
                const((1, D_MODEL)),
                pl.BlockSpec(memory_space=pl.ANY),
                const(w_gk_f.shape), const(w_gk_b.shape), const(b_gk_f.shape), const(b_gk_b.shape),
                const(q_norm.shape), const(k_norm.shape),
                rope_spec, rope_spec, rope_spec]
    row = lambda w: pl.BlockSpec((TM, w), lambda s: (out_tile(s), 0))
    cache_spec = pl.BlockSpec((seqs_per_tile, 1, SWA_KV_HEADS, HEAD_DIM, seq_p),
                              lambda s: (jnp.minimum(out_tile(s), tp - 1), 0, 0, 0, 0))
    widths = (4 * GLA_WIDTH, 2 * GLA_WIDTH, SWA_WIDTH + 2 * KV_WIDTH)
    out_specs = [row(widths[0]), pl.BlockSpec((TM, widths[1]), lambda s: (gate_tile(s), 0)), row(widths[2]),
                 cache_spec, cache_spec]
    cache_shape = (n_seq_p, 1, SWA_KV_HEADS, HEAD_DIM, seq_p)
    out_shape = ([jax.ShapeDtypeStruct((n, w), BF16) for w in widths]
                 + [jax.ShapeDtypeStruct(cache_shape, F32)] * 2)
    return pl.pallas_call(
        functools.partial(_proj_kernel, n_prompt_tiles=tp, n_tiles=n_tiles, tiles_per_seq=tiles_per_seq,
                          seqs_per_tile=seqs_per_tile, seq_len=seq_p),
        grid=(n_tiles + 2,), in_specs=in_specs, out_specs=out_specs, out_shape=out_shape,
        scratch_shapes=([pltpu.VMEM((D_MODEL, W_COLS), BF16), pltpu.VMEM((LANES, 2 * GLA_WIDTH), BF16)]
                        + [pltpu.VMEM((TM, D_MODEL), BF16)] * 2 + [pltpu.VMEM((TM, LANES), BF16)] * 2
                        + [pltpu.VMEM((len(W_MOVES), LANES, D_MODEL), F32),
                           pltpu.SemaphoreType.DMA((len(W_MOVES),))]),
        compiler_params=_cparams(1), name="adaln_in_proj",
    )(xp, xs, mod, gain, w_in, w_gk_f, w_gk_b, b_gk_f, b_gk_b, q_norm, k_norm, *rope_tabs)


def _gla_kernel(*refs, seq_len, seqs, has_state, emit_state, with_attention, n_cast):
    (q_ref, k_ref, v_ref, gf_ref, gb_ref, sgg_ref, norm_ref,
     trif_ref, trib_ref, keepf_ref, keepb_ref) = refs[:11]
    pos = 11
    if has_state:
        s0f_ref, s0b_ref = refs[pos:pos + 2]
        pos += 2
    if with_attention:
        sink_ref, aq_ref, ak_ref, av_ref = refs[pos:pos + 4]
        pos += 4
    if with_attention == "latent":
        kc_ref, vc_ref = refs[pos:pos + 2]
        pos += 2
    cast_in = refs[pos:pos + n_cast]
    pos += n_cast
    og_ref = refs[pos]
    pos += 1
    if emit_state:
        sf_ref, sb_ref = refs[pos:pos + 2]
        pos += 2
    if with_attention:
        osw_ref = refs[pos]
        pos += 1
    cast_out = refs[pos:pos + n_cast]
    pos += n_cast
    accf_ref, accb_ref = refs[pos:pos + 2]

    for src, dst in zip(cast_in, cast_out):
        dst[...] = src[...].astype(BF16)

    n_groups = seq_len // GROUP
    per_group = GROUP // CHUNK
    n_pairs = q_ref.shape[1] // LANES
    lane = lax.broadcasted_iota(jnp.int32, (1, LANES), 1)
    head0 = lane < HEAD_DIM
    m0 = jnp.where(head0, 1.0, 0.0).astype(BF16)
    m1 = jnp.where(head0, 0.0, 1.0).astype(BF16)
    row_top = lax.broadcasted_iota(jnp.int32, (LANES, 1), 0) < HEAD_DIM

    def stack_heads(x):
        return jnp.concatenate([x * m0, x * m1], axis=0)

    def direction(row0, lanes, g_ref, tri_ref, keep_ref, state, acc_ref, backward):
        rows = pl.ds(row0, GROUP)
        b = _dot(tri_ref[...], g_ref[rows, lanes])
        yield
        b = b.reshape(per_group, CHUNK, LANES)
        edge = 0 if backward else CHUNK - 1
        mid = CHUNK // 2
        b_edge = b[:, edge:edge + 1, :]
        b_mid = b[:, mid:mid + 1, :]
        q = q_ref[rows, lanes].astype(F32).reshape(per_group, CHUNK, LANES)
        k = k_ref[rows, lanes].astype(F32).reshape(per_group, CHUNK, LANES)
        v = v_ref[rows, lanes]
        q_mid = q * jnp.exp2(b - b_mid)
        k_mid = k * jnp.exp2(b_mid - b)
        q_in = q_mid.astype(BF16)
        k_in = k_mid.astype(BF16).reshape(GROUP, LANES)
        q_st = (q_mid * jnp.exp2(b_mid)).astype(BF16)
        k_st = (k_mid * jnp.exp2(b_edge - b_mid)).astype(BF16).reshape(GROUP, LANES)
        decay = jnp.exp2(b_edge)
        keep = keep_ref[...]
        pair_rows = 2 * CHUNK
        n_cp = per_group // 2
        yield
        scores, kvs = [], []
        for p in range(n_cp):
            c0 = 2 * p
            prow = slice(p * pair_rows, (p + 1) * pair_rows)
            q_pair = jnp.concatenate([stack_heads(q_in[c0]), stack_heads(q_in[c0 + 1])], axis=0)
            scores.append(_dot_nt(q_pair, k_in[prow, :]).astype(BF16) * keep)
            v_t = v[prow, :].astype(F32).T.astype(BF16)
            k_pair = k_st[prow, :]
            zero = jnp.zeros_like(k_pair)
            k_blk = jnp.concatenate([jnp.where(row_top, k_pair, zero), jnp.where(row_top, zero, k_pair)], axis=-1)
            kvs.append(_dot(v_t, k_blk))
        yield
        intras = [_dot(scores[p], v[p * pair_rows:(p + 1) * pair_rows, :]) for p in range(n_cp)]
        yield
        for p in (range(n_cp - 1, -1, -1) if backward else range(n_cp)):
            for j in ((1, 0) if backward else (0, 1)):
                c = 2 * p + j
                inter = _dot_nt(stack_heads(q_st[c]), state.astype(BF16))
                tot = intras[p][j * pair_rows:(j + 1) * pair_rows, :] + inter
                acc_ref[pl.ds(row0 + c * CHUNK, CHUNK), lanes] = jnp.where(head0, tot[:CHUNK, :], tot[CHUNK:, :])
                state = state * decay[c] + kvs[p][:, j * LANES:(j + 1) * LANES]
            yield
        return state

    def load_state(s_ref, q, p):
        z = jnp.zeros((HEAD_DIM, HEAD_DIM), F32)
        top = jnp.concatenate([s_ref[q, 0, 2 * p, :, :].T, z], axis=-1)
        bot = jnp.concatenate([z, s_ref[q, 0, 2 * p + 1, :, :].T], axis=-1)
        return jnp.concatenate([top, bot], axis=0)

    per_dir = seqs * n_pairs
    if has_state:
        init = tuple(load_state(s_ref, q, p)
                     for s_ref in (s0f_ref, s0b_ref) for q in range(seqs) for p in range(n_pairs))
    else:
        init = (jnp.zeros((LANES, LANES), F32),) * (2 * per_dir)

    def body(t, carry):
        row_f = t * GROUP
        row_b = (n_groups - 1 - t) * GROUP
        if n_groups > 1:
            row_f = pl.multiple_of(row_f, GROUP)
            row_b = pl.multiple_of(row_b, GROUP)
        units = []
        for backward, row, g_ref, tri_ref, keep_ref, acc_ref in (
                (False, row_f, gf_ref, trif_ref, keepf_ref, accf_ref),
                (True, row_b, gb_ref, trib_ref, keepb_ref, accb_ref)):
            for q in range(seqs):
                for p in range(n_pairs):
                    lanes = slice(p * LANES, (p + 1) * LANES)
                    state = carry[backward * per_dir + q * n_pairs + p]
                    units.append(direction(q * seq_len + row, lanes, g_ref, tri_ref, keep_ref, state, acc_ref,
                                           backward))
        return units

    norm_pair = jnp.concatenate([norm_ref[...]] * 2, axis=-1)

    def finish_outputs(rows):
        for p in range(n_pairs):
            lanes = slice(p * LANES, (p + 1) * LANES)
            o = _head_rms_norm(accf_ref[rows, lanes] + accb_ref[rows, lanes], norm_pair)
            og_ref[rows, lanes] = (o * sgg_ref[rows, lanes].astype(F32)).astype(BF16)
            yield

    if n_groups == 1:
        gla_units = body(0, init)
        final = [None] * len(gla_units)
        for q in range(seqs + 1):
            rows = slice(q * seq_len, (q + 1) * seq_len)
            wave, ids = [], []
            if q < seqs:
                ids = [d * per_dir + q * n_pairs + p for d in range(2) for p in range(n_pairs)]
                wave = [gla_units[u] for u in ids]
                if with_attention:
                    wave += _context_attention_units(sink_ref, aq_ref, ak_ref, av_ref, rows)
            if q > 0:
                wave.append(finish_outputs(slice((q - 1) * seq_len, q * seq_len)))
            results = _run_staged(wave)
            for u, r in zip(ids, results):
                final[u] = r
            if q < seqs and with_attention:
                heads = [o for outs in results[len(ids):len(ids) + SWA_KV_HEADS] for o in outs]
                osw_ref[rows, :] = jnp.concatenate(heads, axis=-1).astype(BF16)
    else:
        assert with_attention != "context" and seqs == 1
        if with_attention:
            ctx_kv = [(kc_ref[0, 0, g, :, :].astype(BF16), vc_ref[0, 0, g, :, :].astype(BF16))
                      for g in range(SWA_KV_HEADS)]
            q_blocks = GROUP // QBLOCK

        def loop_body(t, carry):
            units = body(t, carry)
            n_gla = len(units)
            if with_attention:
                for j in range(q_blocks):
                    units += _latent_attention_units(sink_ref, aq_ref, ak_ref, av_ref, ctx_kv,
                                                     t * q_blocks + j, seq_len)
            results = _run_staged(units)
            if with_attention:
                for j in range(q_blocks):
                    heads = [o for outs in results[n_gla + j * SWA_KV_HEADS:n_gla + (j + 1) * SWA_KV_HEADS]
                             for o in outs]
                    rows = pl.ds(pl.multiple_of((t * q_blocks + j) * QBLOCK, QBLOCK), QBLOCK)
                    osw_ref[rows, :] = jnp.concatenate(heads, axis=-1).astype(BF16)
            return tuple(results[:n_gla])

        final = lax.fori_loop(0, n_groups, loop_body, init)
        _run_staged([finish_outputs(slice(0, seq_len))])
    if emit_state:
        for d, s_ref in enumerate((sf_ref, sb_ref)):
            for q in range(seqs):
                for p in range(n_pairs):
                    fin = final[d * per_dir + q * n_pairs + p]
                    s_ref[q, 0, 2 * p, :, :] = fin[:HEAD_DIM, :HEAD_DIM].T
                    s_ref[q, 0, 2 * p + 1, :, :] = fin[HEAD_DIM:, HEAD_DIM:].T


def _gla_constants():
    i = np.arange(GROUP)[:, None]
    j = np.arange(GROUP)[None, :]
    same = (i // CHUNK) == (j // CHUNK)
    tri_f = same & (j <= i)
    tri_b = same & (j >= i)
    r = np.arange(4 * CHUNK)[:, None]
    l = np.arange(LANES)[None, :]
    same = (r // (2 * CHUNK)) == (l // CHUNK)
    keep_f = same & ((l % CHUNK) <= (r % CHUNK))
    keep_b = same & ((l % CHUNK) >= (r % CHUNK))
    return tuple(jnp.asarray(m.astype(np.float32), dtype=BF16) for m in (tri_f, tri_b, keep_f, keep_b))


def _gla(gla_in, gates, norm_row, s0f, s0b, n_seq, seq_len, row_block0, emit_state, seqs=1,
         attention=None, cast_weights=()):
    has_state = s0f is not None
    steps = n_seq // seqs
    rows_per_step = seqs * seq_len
    row_blk = row_block0 // seqs
    col = lambda width, j: pl.BlockSpec((rows_per_step, width), lambda b: (row_blk + b, j))
    out_blk = pl.BlockSpec((rows_per_step, GLA_WIDTH), lambda b: (b, 0))
    st_spec = pl.BlockSpec((seqs, 1, GLA_HEADS, HEAD_DIM, HEAD_DIM), lambda b: (b, 0, 0, 0, 0))
    const = lambda shape: pl.BlockSpec(shape, lambda b: (0, 0))
    in_specs = ([col(GLA_WIDTH, 0), col(GLA_WIDTH, 1), col(GLA_WIDTH, 2), col(GLA_WIDTH, 0), col(GLA_WIDTH, 1),
                 col(GLA_WIDTH, 3)]
                + [const(norm_row.shape)] + [const((GROUP, GROUP))] * 2 + [const((4 * CHUNK, LANES))] * 2)
    args = [gla_in, gla_in, gla_in, gates, gates, gla_in, norm_row, *_gla_constants()]
    if has_state:
        in_specs += [st_spec] * 2
        args += [s0f, s0b]
    out_specs = [out_blk]
    out_shape = [jax.ShapeDtypeStruct((n_seq * seq_len, GLA_WIDTH), BF16)]
    if emit_state:
        out_specs += [st_spec] * 2
        out_shape += [jax.ShapeDtypeStruct((n_seq, 1, GLA_HEADS, HEAD_DIM, HEAD_DIM), F32)] * 2
    with_attention = None
    if attention is not None:
        sink, swa_in = attention[:2]
        with_attention = "context"
        in_specs += [pl.BlockSpec(memory_space=pltpu.SMEM), col(SWA_WIDTH, 0),
                     col(KV_WIDTH, SWA_WIDTH // KV_WIDTH), col(KV_WIDTH, SWA_WIDTH // KV_WIDTH + 1)]
        args += [sink, swa_in, swa_in, swa_in]
        if len(attention) > 2:
            with_attention = "latent"
            cache_k_t, cache_v_t = attention[2:]
            past = cache_k_t.shape[4]
            cache_spec = pl.BlockSpec((seqs, 1, SWA_KV_HEADS, HEAD_DIM, past), lambda b: (b, 0, 0, 0, 0))
            in_specs += [cache_spec] * 2
            args += [cache_k_t, cache_v_t]
        out_specs.append(pl.BlockSpec((rows_per_step, SWA_WIDTH), lambda b: (b, 0)))
        out_shape.append(jax.ShapeDtypeStruct((n_seq * seq_len, SWA_WIDTH), BF16))
    cast_specs, cast_shapes = _cast_specs(cast_weights, steps)
    in_specs += cast_specs
    args += list(cast_weights)
    out_specs += cast_specs
    out_shape += cast_shapes
    return pl.pallas_call(
        functools.partial(_gla_kernel, seq_len=seq_len, seqs=seqs, has_state=has_state, emit_state=emit_state,
                          with_attention=with_attention, n_cast=len(cast_weights)),
        grid=(steps,), in_specs=in_specs, out_specs=out_specs, out_shape=out_shape,
        scratch_shapes=[pltpu.VMEM((rows_per_step, GLA_WIDTH), F32)] * 2,
        compiler_params=_cparams(1), name="gla_bidirectional",
    )(*args)


def _group_attention(q, heads, sink_ref, score_fn, value_fn):
    rows = q.shape[0]
    q_stack = jnp.concatenate([q[:, h * HEAD_DIM:(h + 1) * HEAD_DIM] for h in heads], axis=0)
    scores = score_fn(q_stack)
    yield
    slabs = [slice(j * rows, (j + 1) * rows) for j in range(len(heads))]
    ms = []
    sinks = [sink_ref[h] * LOG2E for h in heads]
    for slab, sink in zip(slabs, sinks):
        m = sink
        for s in scores:
            m = jnp.maximum(m, jnp.max(s[slab, :], axis=-1, keepdims=True))
        ms.append(m)
    yield
    e_parts = [[] for _ in scores]
    denoms = []
    for slab, sink, m in zip(slabs, sinks, ms):
        denom = jnp.exp2(sink - m)
        for part, s in zip(e_parts, scores):
            e = jnp.exp2(s[slab, :] - m)
            denom = denom + jnp.sum(e, axis=-1, keepdims=True)
            part.append(e.astype(BF16))
        denoms.append(denom)
    yield
    acc = value_fn([jnp.concatenate(part, axis=0) for part in e_parts])
    yield
    return [acc[slab, :] / denom for slab, denom in zip(slabs, denoms)]


def _run_staged(units):
    results = [None] * len(units)
    live = list(range(len(units)))
    while live:
        still = []
        for u in live:
            try:
                next(units[u])
                still.append(u)
            except StopIteration as stop:
                results[u] = stop.value
        live = still
    return results


def _context_attention_units(sink_ref, q_ref, k_ref, v_ref, rows):
    q = q_ref[rows, :]
    k = k_ref[rows, :]
    v = v_ref[rows, :]
    units = []
    for g in range(SWA_KV_HEADS):
        k_g = k[:, g * HEAD_DIM:(g + 1) * HEAD_DIM]
        v_g = v[:, g * HEAD_DIM:(g + 1) * HEAD_DIM]
        units.append(_group_attention(
            q, list(range(SWA_GROUP * g, SWA_GROUP * (g + 1))), sink_ref,
            lambda qs, k_g=k_g: [_dot_nt(qs, k_g)],
            lambda es, v_g=v_g: _dot(es[0], v_g)))
    return units


def _cast_specs(cast_weights, steps, step_of=lambda *idx: idx[0]):
    specs = [pl.BlockSpec((w.shape[0] // steps, w.shape[1]), lambda *idx: (step_of(*idx), 0))
             for w in cast_weights]
    shapes = [jax.ShapeDtypeStruct(w.shape, BF16) for w in cast_weights]
    return specs, shapes


def _latent_attention_units(sink_ref, q_ref, k_ref, v_ref, ctx_kv, i, seq_len):
    span = QBLOCK + 2 * WINDOW
    start = pl.multiple_of(jnp.clip(i * QBLOCK - WINDOW, 0, seq_len - span), QBLOCK)
    q = q_ref[pl.ds(pl.multiple_of(i * QBLOCK, QBLOCK), QBLOCK), :]
    k = k_ref[pl.ds(start, span), :]
    v = v_ref[pl.ds(start, span), :]
    qpos = i * QBLOCK + lax.broadcasted_iota(jnp.int32, (QBLOCK, span), 0)
    kpos = start + lax.broadcasted_iota(jnp.int32, (QBLOCK, span), 1)
    valid = jnp.abs(qpos - kpos) <= WINDOW
    valid = jnp.concatenate([valid] * SWA_GROUP, axis=0)
    units = []
    for g in range(SWA_KV_HEADS):
        k_g = k[:, g * HEAD_DIM:(g + 1) * HEAD_DIM]
        v_g = v[:, g * HEAD_DIM:(g + 1) * HEAD_DIM]
        kc_t, vc_t = ctx_kv[g]
        units.append(_group_attention(
            q, list(range(SWA_GROUP * g, SWA_GROUP * (g + 1))), sink_ref,
            lambda qs, k_g=k_g, kc_t=kc_t: [jnp.where(valid, _dot_nt(qs, k_g), NEG_INF), _dot(qs, kc_t)],
            lambda es, v_g=v_g, vc_t=vc_t: _dot(es[0], v_g) + _dot_nt(es[1], vc_t)))
    return units


def _out_kernel(xp_ref, xs_ref, ogp_ref, ogs_ref, oswp_ref, osws_ref, mod_ref, gain_ref,
                wo_ref, w1_ref, w2_ref, yp_ref, ys_ref, *, n_prompt_tiles, tiles_per_seq):
    i = pl.program_id(0)
    row = _mod_row(i, n_prompt_tiles, tiles_per_seq)

    def body(x_ref, og_ref, osw_ref, y_ref):
        mix = _dot(og_ref[...], wo_ref[:GLA_WIDTH, :]) + _dot(osw_ref[...], wo_ref[GLA_WIDTH:, :])
        x1 = x_ref[...] + _mod_chunk(mod_ref, row, 2) * mix
        h = _adaln(x1, gain_ref[...], _mod_chunk(mod_ref, row, 3), _mod_chunk(mod_ref, row, 4)).astype(BF16)
        acc = None
        for j in range(D_FF // FF_TILE):
            cols = slice(j * FF_TILE, (j + 1) * FF_TILE)
            a = jnp.maximum(_dot(h, w1_ref[:, cols]), 0.0)
            part = _dot((a * a).astype(BF16), w2_ref[cols, :])
            acc = part if acc is None else acc + part
        y_ref[...] = x1 + _mod_chunk(mod_ref, row, 5) * acc

    @pl.when(i < n_prompt_tiles)
    def _():
        body(xp_ref, ogp_ref, oswp_ref, yp_ref)

    @pl.when(i >= n_prompt_tiles)
    def _():
        body(xs_ref, ogs_ref, osws_ref, ys_ref)


def _mix_out(xp, xs, og_p, og_s, osw_p, osw_s, mod, gain, wo, w1, w2, seq_s):
    n_p, n_s = xp.shape[0], xs.shape[0]
    tp, ts = n_p // TM, n_s // TM
    tiles_per_seq = seq_s // TM
    const = lambda shape: pl.BlockSpec(shape, lambda i: (0,) * len(shape), pipeline_mode=pl.Buffered(1))
    row_p = lambda w: pl.BlockSpec((TM, w), lambda i: (jnp.minimum(i, tp - 1), 0))
    row_s = lambda w: pl.BlockSpec((TM, w), lambda i: (jnp.maximum(i - tp, 0), 0))
    return pl.pallas_call(
        functools.partial(_out_kernel, n_prompt_tiles=tp, tiles_per_seq=tiles_per_seq),
        grid=(tp + ts,),
        in_specs=[row_p(D_MODEL), row_s(D_MODEL), row_p(GLA_WIDTH), row_s(GLA_WIDTH),
                  row_p(SWA_WIDTH), row_s(SWA_WIDTH),
                  const(mod.shape),
                  const((1, D_MODEL)), const((GLA_WIDTH + SWA_WIDTH, D_MODEL)),
                  const((D_MODEL, D_FF)), const((D_FF, D_MODEL))],
        out_specs=[row_p(D_MODEL), row_s(D_MODEL)],
        out_shape=[jax.ShapeDtypeStruct((n_p, D_MODEL), F32), jax.ShapeDtypeStruct((n_s, D_MODEL), F32)],
        compiler_params=_cparams(1), name="out_proj_mlp",
    )(xp, xs, og_p, og_s, osw_p, osw_s, mod, gain, wo, w1, w2)


def _rope_tables(seq_len):
    axis_dim = HEAD_DIM // 2
    half = axis_dim // 2
    t = np.arange(seq_len)
    rowp = (t // GRID_W).astype(np.float64)
    colp = (t % GRID_W).astype(np.float64)
    freqs = ROPE_THETA ** (-np.arange(half, dtype=np.float64) / half)
    lane = np.arange(LANES)
    d = lane % HEAD_DIM
    posv = np.where((d // axis_dim)[None, :] == 0, rowp[:, None], colp[:, None])
    ang = posv * freqs[d % half][None, :]
    second = ((d % axis_dim) >= half)[None, :]
    cos, sin = np.cos(ang), np.sin(ang)
    zero = np.zeros_like(sin)
    tabs = (cos, np.where(second, sin, zero), np.where(second, zero, -sin))
    return tuple(jnp.asarray(tab.astype(np.float32)) for tab in tabs)


def kernel(x_prompt, x_sample, c, cache_k, cache_v, state_gla_fwd, state_gla_bwd, c_ctx, w_ada, b_ada, norm_mix, norm_ff, w_in, w_gk_fwd, b_gk_fwd, w_gk_bwd, b_gk_bwd, gla_norm, q_norm, k_norm, sink, w_o, w_ff1, w_ff2):
    depth = w_in.shape[0]
    assert depth == 1, "single trunk layer"
    bp, tp, _ = x_prompt.shape
    bs, ts, _ = x_sample.shape
    n_p = bp * tp
    l = 0

    mod = _modulation(c_ctx[None, :], c, w_ada[l], b_ada[l])

    norm2 = gla_norm[l][None, :]
    gain_mix = norm_mix[l][None, :]
    gain_ff = norm_ff[l][None, :]
    sink_l = sink[l]

    w_in_t = jnp.transpose(w_in[l])
    cache_k_t = jnp.swapaxes(cache_k, 3, 4)
    cache_v_t = jnp.swapaxes(cache_v, 3, 4)

    xp = x_prompt.reshape(n_p, D_MODEL)
    xs = x_sample.reshape(bs * ts, D_MODEL)
    gla_in, gates, swa_in, new_k_t, new_v_t = _project(
        xp, xs, mod, gain_mix, w_in_t, w_gk_fwd[l], w_gk_bwd[l], b_gk_fwd[l][None, :], b_gk_bwd[l][None, :],
        q_norm[l][None, :], k_norm[l][None, :], _rope_tables(ts), tp, ts)
    new_k = jnp.swapaxes(new_k_t, 3, 4)
    new_v = jnp.swapaxes(new_v_t, 3, 4)

    og_p, s_f, s_b, osw_p, wo, w1, w2 = _gla(
        gla_in, gates, norm2, None, None, bp, tp, 0, True, seqs=2,
        attention=(sink_l, swa_in), cast_weights=(w_o[l], w_ff1[l], w_ff2[l]))
    og_s, osw_s = _gla(gla_in, gates, norm2, state_gla_fwd, state_gla_bwd, bs, ts, n_p // ts, False,
                       attention=(sink_l, swa_in, cache_k_t, cache_v_t))

    yp, ys = _mix_out(xp, xs, og_p, og_s, osw_p, osw_s, mod, gain_ff, wo, w1, w2, ts)
    return (yp.reshape(bp, tp, D_MODEL), ys.reshape(bs, ts, D_MODEL), new_k, new_v, s_f, s_b)
```

```python
import functools

import numpy as np
import jax
import jax.numpy as jnp
from jax import lax
from jax.experimental import pallas as pl
from jax.experimental.pallas import tpu as pltpu

F32 = jnp.float32
BF16 = jnp.bfloat16

D_MODEL = 1024
GRID_W = 64
HEAD_DIM = 64
GLA_HEADS = 8
GLA_WIDTH = GLA_HEADS * HEAD_DIM
GATE_RANK = 16
GATE_TEMP = 16.0
CHUNK = 64
GROUP = 4 * CHUNK
SWA_HEADS = 8
SWA_KV_HEADS = 2
SWA_GROUP = SWA_HEADS // SWA_KV_HEADS
SWA_WIDTH = SWA_HEADS * HEAD_DIM
KV_WIDTH = SWA_KV_HEADS * HEAD_DIM
WINDOW = 128
QBLOCK = 128
D_FF = 4 * D_MODEL
ROPE_THETA = 10000.0
EPS = 1e-6
NEG_INF = -1e30
LOG2E = 1.4426950408889634
LANES = 128
TM = 512
FF_TILE = 1024

D_IN = 2848
R_GQ, R_GK, R_GV, R_GG, R_LR = 0, 512, 1024, 1536, 2048
R_SQ = R_LR + 2 * GATE_RANK
R_SV = R_SQ + SWA_WIDTH + KV_WIDTH
C_TAIL = R_LR
C_LR = C_TAIL + D_IN - R_SQ
W_COLS = C_LR + LANES

V7X_VMEM_BYTES = 64 * 1024 * 1024
VMEM_LIMIT = V7X_VMEM_BYTES - 8 * 1024 * 1024


def _cparams(n_axes):
    return pltpu.CompilerParams(dimension_semantics=("arbitrary",) * n_axes, vmem_limit_bytes=VMEM_LIMIT)


def _mod_kernel(cctx_ref, c_ref, wa_ref, wb_ref, b_ref, o_ref):
    k = pl.program_id(0)
    pad = jnp.zeros((8 - 1 - c_ref.shape[0], c_ref.shape[1]), F32)
    cnd = jnp.concatenate([cctx_ref[...], c_ref[...], pad], axis=0)
    s = (cnd * jax.nn.sigmoid(cnd)).astype(BF16)
    part = jnp.concatenate([_dot(s, wa_ref[...].astype(BF16)), _dot(s, wb_ref[...].astype(BF16))], axis=-1)

    @pl.when(k == 0)
    def _():
        o_ref[...] = part + b_ref[...]

    @pl.when(k > 0)
    def _():
        o_ref[...] += part


def _modulation(c_ctx, c, w_ada, b_ada):
    tk = 256
    n_out = w_ada.shape[1]
    return pl.pallas_call(
        _mod_kernel,
        grid=(D_MODEL // tk,),
        in_specs=[pl.BlockSpec((1, tk), lambda k: (0, k)),
                  pl.BlockSpec((c.shape[0], tk), lambda k: (0, k)),
                  pl.BlockSpec((tk, n_out // 2), lambda k: (k, 0)),
                  pl.BlockSpec((tk, n_out // 2), lambda k: (k, 1)),
                  pl.BlockSpec((1, n_out), lambda k: (0, 0))],
        out_specs=pl.BlockSpec((8, n_out), lambda k: (0, 0)),
        out_shape=jax.ShapeDtypeStruct((8, n_out), F32),
        compiler_params=_cparams(1),
        name="modulation",
    )(c_ctx, c, w_ada, w_ada, b_ada.reshape(1, n_out))


def _mod_row(tile, n_prompt_tiles, tiles_per_seq):
    return jnp.where(tile < n_prompt_tiles, 0, 1 + (tile - n_prompt_tiles) // tiles_per_seq)


def _mod_chunk(mod_ref, row, j):
    return mod_ref[pl.ds(row, 1), j * D_MODEL:(j + 1) * D_MODEL]


def _adaln(x, gain, shift, scale):
    ms = jnp.mean(x * x, axis=-1, keepdims=True)
    xn = x * lax.rsqrt(ms + EPS)
    return xn * (gain * (1.0 + scale)) + shift


def _head_rms_norm(y, gain):
    cols = y.shape[1]
    lane = lax.broadcasted_iota(jnp.int32, (1, LANES), 1)
    first = lane < HEAD_DIM
    outs = []
    for p in range(cols // LANES):
        s = y[:, p * LANES:(p + 1) * LANES]
        s = s * s
        tot = jnp.sum(s, axis=-1, keepdims=True)
        lo = jnp.sum(jnp.where(first, s, 0.0), axis=-1, keepdims=True)
        outs.append(jnp.where(first, lo, tot - lo))
    sums = outs[0] if len(outs) == 1 else jnp.concatenate(outs, axis=-1)
    return y * lax.rsqrt(sums + HEAD_DIM * EPS) * (gain * (HEAD_DIM ** 0.5))


def _log_sigmoid(x):
    return jnp.minimum(x, 0.0) - jnp.log(1.0 + jnp.exp(-jnp.abs(x)))


def _dot(a, b):
    return jnp.dot(a, b, preferred_element_type=F32)


def _dot_nt(a, b):
    return lax.dot_general(a, b, (((1,), (1,)), ((), ())), preferred_element_type=F32)


def _proj_kernel(xp_ref, xs_ref, mod_ref, gain_ref, w_ref, wgf_ref, wgb_ref, bgf_ref, bgb_ref, qn_ref, kn_ref,
                 cos_ref, sin_a_ref, sin_b_ref,
                 gla_ref, gate_ref, swa_ref, nk_ref, nv_ref,
                 wb_ref, wg_ref, h0_ref, h1_ref, lr0_ref, lr1_ref,
                 *, n_prompt_tiles, n_tiles, tiles_per_seq, seqs_per_tile, seq_len):
    h_refs = (h0_ref, h1_ref)
    lr_refs = (lr0_ref, lr1_ref)
    s = pl.program_id(0)

    @pl.when(s == 0)
    def _():
        moves = ([(c * LANES, c * LANES) for c in range(R_LR // LANES)]
                 + [(R_SQ + c * LANES, C_TAIL + c * LANES) for c in range((D_IN - R_SQ) // LANES)]
                 + [(R_LR, C_LR)])
        for src, dst in moves:
            wb_ref[:, dst:dst + LANES] = w_ref[src:src + LANES, :].T.astype(BF16)
        wg_ref[...] = jnp.zeros(wg_ref.shape, BF16)
        wg_ref[0:GATE_RANK, 0:GLA_WIDTH] = wgf_ref[...].astype(BF16)
        wg_ref[GATE_RANK:2 * GATE_RANK, GLA_WIDTH:] = wgb_ref[...].astype(BF16)

    def normalise_next(parity):
        x = jnp.where(s < n_prompt_tiles, xp_ref[...], xs_ref[...])
        row = _mod_row(jnp.minimum(s, n_tiles - 1), n_prompt_tiles, tiles_per_seq)
        shift, scale = _mod_chunk(mod_ref, row, 0), _mod_chunk(mod_ref, row, 1)
        h_refs[parity][...] = _adaln(x, gain_ref[...], shift, scale).astype(BF16)

    def gates(parity):
        gate_pre = _dot(lr_refs[parity][...], wg_ref[...])
        scale = LOG2E / GATE_TEMP
        gate_ref[:, :GLA_WIDTH] = (_log_sigmoid(gate_pre[:, :GLA_WIDTH] + bgf_ref[...]) * scale).astype(BF16)
        gate_ref[:, GLA_WIDTH:] = (_log_sigmoid(gate_pre[:, GLA_WIDTH:] + bgb_ref[...]) * scale).astype(BF16)

    def body(rope, emit_cache, parity):
        h_ref = h_refs[1 - parity]
        gates(parity)

        def proj(c0, c1):
            return _dot(h_ref[...], wb_ref[:, c0:c1])

        tail = proj(C_TAIL, C_LR)
        lr_refs[1 - parity][...] = proj(C_LR, W_COLS).astype(BF16)
        gg = proj(R_GG, R_LR)
        gla_ref[:, 3 * GLA_WIDTH:] = (gg * jax.nn.sigmoid(gg)).astype(BF16)
        attention_inputs(tail, rope, emit_cache)
        gla_ref[:, :GLA_WIDTH] = (proj(R_GQ, R_GK) * (HEAD_DIM ** -0.5)).astype(BF16)
        gla_ref[:, GLA_WIDTH:2 * GLA_WIDTH] = proj(R_GK, R_GV).astype(BF16)
        gla_ref[:, 2 * GLA_WIDTH:3 * GLA_WIDTH] = proj(R_GV, R_GG).astype(BF16)
        normalise_next(parity)

    def attention_inputs(tail, rope, emit_cache):
        qk = tail[:, :R_SV - R_SQ]
        qk_gain = jnp.concatenate([qn_ref[...]] * SWA_HEADS + [kn_ref[...]] * SWA_KV_HEADS, axis=-1)
        qk = _head_rms_norm(qk, qk_gain)
        if rope:
            reps = (SWA_WIDTH + KV_WIDTH) // LANES
            cos = jnp.concatenate([cos_ref[...]] * reps, axis=-1)
            sin_a = jnp.concatenate([sin_a_ref[...]] * reps, axis=-1)
            sin_b = jnp.concatenate([sin_b_ref[...]] * reps, axis=-1)
            width = qk.shape[1]
            quarter = HEAD_DIM // 4
            qk = (qk * cos + pltpu.roll(qk, quarter, 1) * sin_a
                  + pltpu.roll(qk, width - quarter, 1) * sin_b)
        swa_ref[:, :SWA_WIDTH] = (qk[:, :SWA_WIDTH] * (HEAD_DIM ** -0.5 * LOG2E)).astype(BF16)
        sk = qk[:, SWA_WIDTH:]
        swa_ref[:, SWA_WIDTH:SWA_WIDTH + KV_WIDTH] = sk.astype(BF16)
        sv = tail[:, R_SV - R_SQ:]
        swa_ref[:, SWA_WIDTH + KV_WIDTH:] = sv.astype(BF16)
        if emit_cache:
            sk_t, sv_t = sk.T, sv.T
            for q in range(seqs_per_tile):
                cols = slice(q * seq_len, (q + 1) * seq_len)
                for g in range(SWA_KV_HEADS):
                    feat = slice(g * HEAD_DIM, (g + 1) * HEAD_DIM)
                    nk_ref[q, 0, g, :, :] = sk_t[feat, cols]
                    nv_ref[q, 0, g, :, :] = sv_t[feat, cols]

    @pl.when(s == 0)
    def _():
        normalise_next(0)
        lr_refs[1][...] = jnp.zeros(lr_refs[1].shape, BF16)

    for parity in range(2):
        on_parity = (s % 2) == parity

        @pl.when((s >= 1) & (s <= n_prompt_tiles) & on_parity)
        def _():
            body(rope=False, emit_cache=True, parity=parity)

        @pl.when((s > n_prompt_tiles) & (s <= n_tiles) & on_parity)
        def _():
            body(rope=True, emit_cache=False, parity=parity)

    @pl.when(s == n_tiles + 1)
    def _():
        gates((n_tiles + 1) % 2)


def _project(xp, xs, mod, gain, w_in, w_gk_f, w_gk_b, b_gk_f, b_gk_b, q_norm, k_norm, rope_tabs, seq_p, seq_s):
    n_p, n_s = xp.shape[0], xs.shape[0]
    tp, ts = n_p // TM, n_s // TM
    n = n_p + n_s
    seqs_per_tile = TM // seq_p
    tiles_per_seq = seq_s // TM
    n_seq_p = n_p // seq_p
    const = lambda shape: pl.BlockSpec(shape, lambda s: (0,) * len(shape))
    n_tiles = tp + ts
    out_tile = lambda s: jnp.clip(s - 1, 0, n_tiles - 1)
    gate_tile = lambda s: jnp.maximum(s - 2, 0)
    rope_spec = pl.BlockSpec((TM, LANES), lambda s: (jnp.maximum(out_tile(s) - tp, 0) % tiles_per_seq, 0))
    in_specs = [pl.BlockSpec((TM, D_MODEL), lambda s: (jnp.minimum(s, tp - 1), 0)),
                pl.BlockSpec((TM, D_MODEL), lambda s: (jnp.clip(s - tp, 0, ts - 1), 0)),
                const(mod.shape),
                const((1, D_MODEL)),
                pl.BlockSpec((D_IN, D_MODEL), lambda s: (0, 0), pipeline_mode=pl.Buffered(1)),
                const(w_gk_f.shape), const(w_gk_b.shape), const(b_gk_f.shape), const(b_gk_b.shape),
                const(q_norm.shape), const(k_norm.shape),
                rope_spec, rope_spec, rope_spec]
    row = lambda w: pl.BlockSpec((TM, w), lambda s: (out_tile(s), 0))
    cache_spec = pl.BlockSpec((seqs_per_tile, 1, SWA_KV_HEADS, HEAD_DIM, seq_p),
                              lambda s: (jnp.minimum(out_tile(s), tp - 1), 0, 0, 0, 0))
    widths = (4 * GLA_WIDTH, 2 * GLA_WIDTH, SWA_WIDTH + 2 * KV_WIDTH)
    out_specs = [row(widths[0]), pl.BlockSpec((TM, widths[1]), lambda s: (gate_tile(s), 0)), row(widths[2]),
                 cache_spec, cache_spec]
    cache_shape = (n_seq_p, 1, SWA_KV_HEADS, HEAD_DIM, seq_p)
    out_shape = ([jax.ShapeDtypeStruct((n, w), BF16) for w in widths]
                 + [jax.ShapeDtypeStruct(cache_shape, F32)] * 2)
    return pl.pallas_call(
        functools.partial(_proj_kernel, n_prompt_tiles=tp, n_tiles=n_tiles, tiles_per_seq=tiles_per_seq,
                          seqs_per_tile=seqs_per_tile, seq_len=seq_p),
        grid=(n_tiles + 2,), in_specs=in_specs, out_specs=out_specs, out_shape=out_shape,
        scratch_shapes=([pltpu.VMEM((D_MODEL, W_COLS), BF16), pltpu.VMEM((LANES, 2 * GLA_WIDTH), BF16)]
                        + [pltpu.VMEM((TM, D_MODEL), BF16)] * 2 + [pltpu.VMEM((TM, LANES), BF16)] * 2),
        compiler_params=_cparams(1), name="adaln_in_proj",
    )(xp, xs, mod, gain, w_in, w_gk_f, w_gk_b, b_gk_f, b_gk_b, q_norm, k_norm, *rope_tabs)


def _gla_kernel(*refs, seq_len, seqs, has_state, emit_state, with_attention, n_cast):
    (q_ref, k_ref, v_ref, gf_ref, gb_ref, sgg_ref, norm_ref,
     trif_ref, trib_ref, keepf_ref, keepb_ref) = refs[:11]
    pos = 11
    if has_state:
        s0f_ref, s0b_ref = refs[pos:pos + 2]
        pos += 2
    if with_attention:
        sink_ref, aq_ref, ak_ref, av_ref = refs[pos:pos + 4]
        pos += 4
    if with_attention == "latent":
        kc_ref, vc_ref = refs[pos:pos + 2]
        pos += 2
    cast_in = refs[pos:pos + n_cast]
    pos += n_cast
    og_ref = refs[pos]
    pos += 1
    if emit_state:
        sf_ref, sb_ref = refs[pos:pos + 2]
        pos += 2
    if with_attention:
        osw_ref = refs[pos]
        pos += 1
    cast_out = refs[pos:pos + n_cast]
    pos += n_cast
    accf_ref, accb_ref = refs[pos:pos + 2]

    for src, dst in zip(cast_in, cast_out):
        dst[...] = src[...].astype(BF16)

    n_groups = seq_len // GROUP
    per_group = GROUP // CHUNK
    n_pairs = q_ref.shape[1] // LANES
    lane = lax.broadcasted_iota(jnp.int32, (1, LANES), 1)
    head0 = lane < HEAD_DIM
    m0 = jnp.where(head0, 1.0, 0.0).astype(BF16)
    m1 = jnp.where(head0, 0.0, 1.0).astype(BF16)
    row_top = lax.broadcasted_iota(jnp.int32, (LANES, 1), 0) < HEAD_DIM

    def stack_heads(x):
        return jnp.concatenate([x * m0, x * m1], axis=0)

    def direction(row0, lanes, g_ref, tri_ref, keep_ref, state, acc_ref, backward):
        rows = pl.ds(row0, GROUP)
        b = _dot(tri_ref[...], g_ref[rows, lanes])
        yield
        b = b.reshape(per_group, CHUNK, LANES)
        edge = 0 if backward else CHUNK - 1
        mid = CHUNK // 2
        b_edge = b[:, edge:edge + 1, :]
        b_mid = b[:, mid:mid + 1, :]
        q = q_ref[rows, lanes].astype(F32).reshape(per_group, CHUNK, LANES)
        k = k_ref[rows, lanes].astype(F32).reshape(per_group, CHUNK, LANES)
        v = v_ref[rows, lanes]
        q_mid = q * jnp.exp2(b - b_mid)
        k_mid = k * jnp.exp2(b_mid - b)
        q_in = q_mid.astype(BF16)
        k_in = k_mid.astype(BF16).reshape(GROUP, LANES)
        q_st = (q_mid * jnp.exp2(b_mid)).astype(BF16)
        k_st = (k_mid * jnp.exp2(b_edge - b_mid)).astype(BF16).reshape(GROUP, LANES)
        decay = jnp.exp2(b_edge)
        keep = keep_ref[...]
        pair_rows = 2 * CHUNK
        n_cp = per_group // 2
        yield
        scores, kvs = [], []
        for p in range(n_cp):
            c0 = 2 * p
            prow = slice(p * pair_rows, (p + 1) * pair_rows)
            q_pair = jnp.concatenate([stack_heads(q_in[c0]), stack_heads(q_in[c0 + 1])], axis=0)
            scores.append(_dot_nt(q_pair, k_in[prow, :]).astype(BF16) * keep)
            v_t = v[prow, :].astype(F32).T.astype(BF16)
            k_pair = k_st[prow, :]
            zero = jnp.zeros_like(k_pair)
            k_blk = jnp.concatenate([jnp.where(row_top, k_pair, zero), jnp.where(row_top, zero, k_pair)], axis=-1)
            kvs.append(_dot(v_t, k_blk))
        yield
        intras = [_dot(scores[p], v[p * pair_rows:(p + 1) * pair_rows, :]) for p in range(n_cp)]
        yield
        for p in (range(n_cp - 1, -1, -1) if backward else range(n_cp)):
            for j in ((1, 0) if backward else (0, 1)):
                c = 2 * p + j
                inter = _dot_nt(stack_heads(q_st[c]), state.astype(BF16))
                tot = intras[p][j * pair_rows:(j + 1) * pair_rows, :] + inter
                acc_ref[pl.ds(row0 + c * CHUNK, CHUNK), lanes] = jnp.where(head0, tot[:CHUNK, :], tot[CHUNK:, :])
                state = state * decay[c] + kvs[p][:, j * LANES:(j + 1) * LANES]
            yield
        return state

    def load_state(s_ref, q, p):
        z = jnp.zeros((HEAD_DIM, HEAD_DIM), F32)
        top = jnp.concatenate([s_ref[q, 0, 2 * p, :, :].T, z], axis=-1)
        bot = jnp.concatenate([z, s_ref[q, 0, 2 * p + 1, :, :].T], axis=-1)
        return jnp.concatenate([top, bot], axis=0)

    per_dir = seqs * n_pairs
    if has_state:
        init = tuple(load_state(s_ref, q, p)
                     for s_ref in (s0f_ref, s0b_ref) for q in range(seqs) for p in range(n_pairs))
    else:
        init = (jnp.zeros((LANES, LANES), F32),) * (2 * per_dir)

    def body(t, carry):
        row_f = t * GROUP
        row_b = (n_groups - 1 - t) * GROUP
        if n_groups > 1:
            row_f = pl.multiple_of(row_f, GROUP)
            row_b = pl.multiple_of(row_b, GROUP)
        units = []
        for backward, row, g_ref, tri_ref, keep_ref, acc_ref in (
                (False, row_f, gf_ref, trif_ref, keepf_ref, accf_ref),
                (True, row_b, gb_ref, trib_ref, keepb_ref, accb_ref)):
            for q in range(seqs):
                for p in range(n_pairs):
                    lanes = slice(p * LANES, (p + 1) * LANES)
                    state = carry[backward * per_dir + q * n_pairs + p]
                    units.append(direction(q * seq_len + row, lanes, g_ref, tri_ref, keep_ref, state, acc_ref,
                                           backward))
        return units

    norm_pair = jnp.concatenate([norm_ref[...]] * 2, axis=-1)

    def finish_outputs(rows):
        for p in range(n_pairs):
            lanes = slice(p * LANES, (p + 1) * LANES)
            o = _head_rms_norm(accf_ref[rows, lanes] + accb_ref[rows, lanes], norm_pair)
            og_ref[rows, lanes] = (o * sgg_ref[rows, lanes].astype(F32)).astype(BF16)
            yield

    if n_groups == 1:
        gla_units = body(0, init)
        final = [None] * len(gla_units)
        for q in range(seqs + 1):
            rows = slice(q * seq_len, (q + 1) * seq_len)
            wave, ids = [], []
            if q < seqs:
                ids = [d * per_dir + q * n_pairs + p for d in range(2) for p in range(n_pairs)]
                wave = [gla_units[u] for u in ids]
                if with_attention:
                    wave += _context_attention_units(sink_ref, aq_ref, ak_ref, av_ref, rows)
            if q > 0:
                wave.append(finish_outputs(slice((q - 1) * seq_len, q * seq_len)))
            results = _run_staged(wave)
            for u, r in zip(ids, results):
                final[u] = r
            if q < seqs and with_attention:
                heads = [o for outs in results[len(ids):len(ids) + SWA_KV_HEADS] for o in outs]
                osw_ref[rows, :] = jnp.concatenate(heads, axis=-1).astype(BF16)
    else:
        assert with_attention != "context" and seqs == 1
        if with_attention:
            ctx_kv = [(kc_ref[0, 0, g, :, :].astype(BF16), vc_ref[0, 0, g, :, :].astype(BF16))
                      for g in range(SWA_KV_HEADS)]
            q_blocks = GROUP // QBLOCK

        def loop_body(t, carry):
            units = body(t, carry)
            n_gla = len(units)
            if with_attention:
                for j in range(q_blocks):
                    units += _latent_attention_units(sink_ref, aq_ref, ak_ref, av_ref, ctx_kv,
                                                     t * q_blocks + j, seq_len)
            results = _run_staged(units)
            if with_attention:
                for j in range(q_blocks):
                    heads = [o for outs in results[n_gla + j * SWA_KV_HEADS:n_gla + (j + 1) * SWA_KV_HEADS]
                             for o in outs]
                    rows = pl.ds(pl.multiple_of((t * q_blocks + j) * QBLOCK, QBLOCK), QBLOCK)
                    osw_ref[rows, :] = jnp.concatenate(heads, axis=-1).astype(BF16)
            return tuple(results[:n_gla])

        final = lax.fori_loop(0, n_groups, loop_body, init)
        _run_staged([finish_outputs(slice(0, seq_len))])
    if emit_state:
        for d, s_ref in enumerate((sf_ref, sb_ref)):
            for q in range(seqs):
                for p in range(n_pairs):
                    fin = final[d * per_dir + q * n_pairs + p]
                    s_ref[q, 0, 2 * p, :, :] = fin[:HEAD_DIM, :HEAD_DIM].T
                    s_ref[q, 0, 2 * p + 1, :, :] = fin[HEAD_DIM:, HEAD_DIM:].T


def _gla_constants():
    i = np.arange(GROUP)[:, None]
    j = np.arange(GROUP)[None, :]
    same = (i // CHUNK) == (j // CHUNK)
    tri_f = same & (j <= i)
    tri_b = same & (j >= i)
    r = np.arange(4 * CHUNK)[:, None]
    l = np.arange(LANES)[None, :]
    same = (r // (2 * CHUNK)) == (l // CHUNK)
    keep_f = same & ((l % CHUNK) <= (r % CHUNK))
    keep_b = same & ((l % CHUNK) >= (r % CHUNK))
    return tuple(jnp.asarray(m.astype(np.float32), dtype=BF16) for m in (tri_f, tri_b, keep_f, keep_b))


def _gla(gla_in, gates, norm_row, s0f, s0b, n_seq, seq_len, row_block0, emit_state, seqs=1,
         attention=None, cast_weights=()):
    has_state = s0f is not None
    steps = n_seq // seqs
    rows_per_step = seqs * seq_len
    row_blk = row_block0 // seqs
    col = lambda width, j: pl.BlockSpec((rows_per_step, width), lambda b: (row_blk + b, j))
    out_blk = pl.BlockSpec((rows_per_step, GLA_WIDTH), lambda b: (b, 0))
    st_spec = pl.BlockSpec((seqs, 1, GLA_HEADS, HEAD_DIM, HEAD_DIM), lambda b: (b, 0, 0, 0, 0))
    const = lambda shape: pl.BlockSpec(shape, lambda b: (0, 0))
    in_specs = ([col(GLA_WIDTH, 0), col(GLA_WIDTH, 1), col(GLA_WIDTH, 2), col(GLA_WIDTH, 0), col(GLA_WIDTH, 1),
                 col(GLA_WIDTH, 3)]
                + [const(norm_row.shape)] + [const((GROUP, GROUP))] * 2 + [const((4 * CHUNK, LANES))] * 2)
    args = [gla_in, gla_in, gla_in, gates, gates, gla_in, norm_row, *_gla_constants()]
    if has_state:
        in_specs += [st_spec] * 2
        args += [s0f, s0b]
    out_specs = [out_blk]
    out_shape = [jax.ShapeDtypeStruct((n_seq * seq_len, GLA_WIDTH), BF16)]
    if emit_state:
        out_specs += [st_spec] * 2
        out_shape += [jax.ShapeDtypeStruct((n_seq, 1, GLA_HEADS, HEAD_DIM, HEAD_DIM), F32)] * 2
    with_attention = None
    if attention is not None:
        sink, swa_in = attention[:2]
        with_attention = "context"
        in_specs += [pl.BlockSpec(memory_space=pltpu.SMEM), col(SWA_WIDTH, 0),
                     col(KV_WIDTH, SWA_WIDTH // KV_WIDTH), col(KV_WIDTH, SWA_WIDTH // KV_WIDTH + 1)]
        args += [sink, swa_in, swa_in, swa_in]
        if len(attention) > 2:
            with_attention = "latent"
            cache_k_t, cache_v_t = attention[2:]
            past = cache_k_t.shape[4]
            cache_spec = pl.BlockSpec((seqs, 1, SWA_KV_HEADS, HEAD_DIM, past), lambda b: (b, 0, 0, 0, 0))
            in_specs += [cache_spec] * 2
            args += [cache_k_t, cache_v_t]
        out_specs.append(pl.BlockSpec((rows_per_step, SWA_WIDTH), lambda b: (b, 0)))
        out_shape.append(jax.ShapeDtypeStruct((n_seq * seq_len, SWA_WIDTH), BF16))
    cast_specs, cast_shapes = _cast_specs(cast_weights, steps)
    in_specs += cast_specs
    args += list(cast_weights)
    out_specs += cast_specs
    out_shape += cast_shapes
    return pl.pallas_call(
        functools.partial(_gla_kernel, seq_len=seq_len, seqs=seqs, has_state=has_state, emit_state=emit_state,
                          with_attention=with_attention, n_cast=len(cast_weights)),
        grid=(steps,), in_specs=in_specs, out_specs=out_specs, out_shape=out_shape,
        scratch_shapes=[pltpu.VMEM((rows_per_step, GLA_WIDTH), F32)] * 2,
        compiler_params=_cparams(1), name="gla_bidirectional",
    )(*args)


def _group_attention(q, heads, sink_ref, score_fn, value_fn):
    rows = q.shape[0]
    q_stack = jnp.concatenate([q[:, h * HEAD_DIM:(h + 1) * HEAD_DIM] for h in heads], axis=0)
    scores = score_fn(q_stack)
    yield
    slabs = [slice(j * rows, (j + 1) * rows) for j in range(len(heads))]
    ms = []
    sinks = [sink_ref[h] * LOG2E for h in heads]
    for slab, sink in zip(slabs, sinks):
        m = sink
        for s in scores:
            m = jnp.maximum(m, jnp.max(s[slab, :], axis=-1, keepdims=True))
        ms.append(m)
    yield
    e_parts = [[] for _ in scores]
    denoms = []
    for slab, sink, m in zip(slabs, sinks, ms):
        denom = jnp.exp2(sink - m)
        for part, s in zip(e_parts, scores):
            e = jnp.exp2(s[slab, :] - m)
            denom = denom + jnp.sum(e, axis=-1, keepdims=True)
            part.append(e.astype(BF16))
        denoms.append(denom)
    yield
    acc = value_fn([jnp.concatenate(part, axis=0) for part in e_parts])
    yield
    return [acc[slab, :] / denom for slab, denom in zip(slabs, denoms)]


def _run_staged(units):
    results = [None] * len(units)
    live = list(range(len(units)))
    while live:
        still = []
        for u in live:
            try:
                next(units[u])
                still.append(u)
            except StopIteration as stop:
                results[u] = stop.value
        live = still
    return results


def _context_attention_units(sink_ref, q_ref, k_ref, v_ref, rows):
    q = q_ref[rows, :]
    k = k_ref[rows, :]
    v = v_ref[rows, :]
    units = []
    for g in range(SWA_KV_HEADS):
        k_g = k[:, g * HEAD_DIM:(g + 1) * HEAD_DIM]
        v_g = v[:, g * HEAD_DIM:(g + 1) * HEAD_DIM]
        units.append(_group_attention(
            q, list(range(SWA_GROUP * g, SWA_GROUP * (g + 1))), sink_ref,
            lambda qs, k_g=k_g: [_dot_nt(qs, k_g)],
            lambda es, v_g=v_g: _dot(es[0], v_g)))
    return units


def _cast_specs(cast_weights, steps, step_of=lambda *idx: idx[0]):
    specs = [pl.BlockSpec((w.shape[0] // steps, w.shape[1]), lambda *idx: (step_of(*idx), 0))
             for w in cast_weights]
    shapes = [jax.ShapeDtypeStruct(w.shape, BF16) for w in cast_weights]
    return specs, shapes


def _latent_attention_units(sink_ref, q_ref, k_ref, v_ref, ctx_kv, i, seq_len):
    span = QBLOCK + 2 * WINDOW
    start = pl.multiple_of(jnp.clip(i * QBLOCK - WINDOW, 0, seq_len - span), QBLOCK)
    q = q_ref[pl.ds(pl.multiple_of(i * QBLOCK, QBLOCK), QBLOCK), :]
    k = k_ref[pl.ds(start, span), :]
    v = v_ref[pl.ds(start, span), :]
    qpos = i * QBLOCK + lax.broadcasted_iota(jnp.int32, (QBLOCK, span), 0)
    kpos = start + lax.broadcasted_iota(jnp.int32, (QBLOCK, span), 1)
    valid = jnp.abs(qpos - kpos) <= WINDOW
    valid = jnp.concatenate([valid] * SWA_GROUP, axis=0)
    units = []
    for g in range(SWA_KV_HEADS):
        k_g = k[:, g * HEAD_DIM:(g + 1) * HEAD_DIM]
        v_g = v[:, g * HEAD_DIM:(g + 1) * HEAD_DIM]
        kc_t, vc_t = ctx_kv[g]
        units.append(_group_attention(
            q, list(range(SWA_GROUP * g, SWA_GROUP * (g + 1))), sink_ref,
            lambda qs, k_g=k_g, kc_t=kc_t: [jnp.where(valid, _dot_nt(qs, k_g), NEG_INF), _dot(qs, kc_t)],
            lambda es, v_g=v_g, vc_t=vc_t: _dot(es[0], v_g) + _dot_nt(es[1], vc_t)))
    return units


def _out_kernel(xp_ref, xs_ref, ogp_ref, ogs_ref, oswp_ref, osws_ref, mod_ref, gain_ref,
                wo_ref, w1_ref, w2_ref, yp_ref, ys_ref, *, n_prompt_tiles, tiles_per_seq):
    i = pl.program_id(0)
    row = _mod_row(i, n_prompt_tiles, tiles_per_seq)

    def body(x_ref, og_ref, osw_ref, y_ref):
        mix = _dot(og_ref[...], wo_ref[:GLA_WIDTH, :]) + _dot(osw_ref[...], wo_ref[GLA_WIDTH:, :])
        x1 = x_ref[...] + _mod_chunk(mod_ref, row, 2) * mix
        h = _adaln(x1, gain_ref[...], _mod_chunk(mod_ref, row, 3), _mod_chunk(mod_ref, row, 4)).astype(BF16)
        acc = None
        for j in range(D_FF // FF_TILE):
            cols = slice(j * FF_TILE, (j + 1) * FF_TILE)
            a = jnp.maximum(_dot(h, w1_ref[:, cols]), 0.0)
            part = _dot((a * a).astype(BF16), w2_ref[cols, :])
            acc = part if acc is None else acc + part
        y_ref[...] = x1 + _mod_chunk(mod_ref, row, 5) * acc

    @pl.when(i < n_prompt_tiles)
    def _():
        body(xp_ref, ogp_ref, oswp_ref, yp_ref)

    @pl.when(i >= n_prompt_tiles)
    def _():
        body(xs_ref, ogs_ref, osws_ref, ys_ref)


def _mix_out(xp, xs, og_p, og_s, osw_p, osw_s, mod, gain, wo, w1, w2, seq_s):
    n_p, n_s = xp.shape[0], xs.shape[0]
    tp, ts = n_p // TM, n_s // TM
    tiles_per_seq = seq_s // TM
    const = lambda shape: pl.BlockSpec(shape, lambda i: (0,) * len(shape), pipeline_mode=pl.Buffered(1))
    row_p = lambda w: pl.BlockSpec((TM, w), lambda i: (jnp.minimum(i, tp - 1), 0))
    row_s = lambda w: pl.BlockSpec((TM, w), lambda i: (jnp.maximum(i - tp, 0), 0))
    return pl.pallas_call(
        functools.partial(_out_kernel, n_prompt_tiles=tp, tiles_per_seq=tiles_per_seq),
        grid=(tp + ts,),
        in_specs=[row_p(D_MODEL), row_s(D_MODEL), row_p(GLA_WIDTH), row_s(GLA_WIDTH),
                  row_p(SWA_WIDTH), row_s(SWA_WIDTH),
                  const(mod.shape),
                  const((1, D_MODEL)), const((GLA_WIDTH + SWA_WIDTH, D_MODEL)),
                  const((D_MODEL, D_FF)), const((D_FF, D_MODEL))],
        out_specs=[row_p(D_MODEL), row_s(D_MODEL)],
        out_shape=[jax.ShapeDtypeStruct((n_p, D_MODEL), F32), jax.ShapeDtypeStruct((n_s, D_MODEL), F32)],
        compiler_params=_cparams(1), name="out_proj_mlp",
    )(xp, xs, og_p, og_s, osw_p, osw_s, mod, gain, wo, w1, w2)


def _rope_tables(seq_len):
    axis_dim = HEAD_DIM // 2
    half = axis_dim // 2
    t = np.arange(seq_len)
    rowp = (t // GRID_W).astype(np.float64)
    colp = (t % GRID_W).astype(np.float64)
    freqs = ROPE_THETA ** (-np.arange(half, dtype=np.float64) / half)
    lane = np.arange(LANES)
    d = lane % HEAD_DIM
    posv = np.where((d // axis_dim)[None, :] == 0, rowp[:, None], colp[:, None])
    ang = posv * freqs[d % half][None, :]
    second = ((d % axis_dim) >= half)[None, :]
    cos, sin = np.cos(ang), np.sin(ang)
    zero = np.zeros_like(sin)
    tabs = (cos, np.where(second, sin, zero), np.where(second, zero, -sin))
    return tuple(jnp.asarray(tab.astype(np.float32)) for tab in tabs)


def kernel(x_prompt, x_sample, c, cache_k, cache_v, state_gla_fwd, state_gla_bwd, c_ctx, w_ada, b_ada, norm_mix, norm_ff, w_in, w_gk_fwd, b_gk_fwd, w_gk_bwd, b_gk_bwd, gla_norm, q_norm, k_norm, sink, w_o, w_ff1, w_ff2):
    depth = w_in.shape[0]
    assert depth == 1, "single trunk layer"
    bp, tp, _ = x_prompt.shape
    bs, ts, _ = x_sample.shape
    n_p = bp * tp
    l = 0

    mod = _modulation(c_ctx[None, :], c, w_ada[l], b_ada[l])

    norm2 = gla_norm[l][None, :]
    gain_mix = norm_mix[l][None, :]
    gain_ff = norm_ff[l][None, :]
    sink_l = sink[l]

    w_in_t = jnp.transpose(w_in[l])
    cache_k_t = jnp.swapaxes(cache_k, 3, 4)
    cache_v_t = jnp.swapaxes(cache_v, 3, 4)

    xp = x_prompt.reshape(n_p, D_MODEL)
    xs = x_sample.reshape(bs * ts, D_MODEL)
    gla_in, gates, swa_in, new_k_t, new_v_t = _project(
        xp, xs, mod, gain_mix, w_in_t, w_gk_fwd[l], w_gk_bwd[l], b_gk_fwd[l][None, :], b_gk_bwd[l][None, :],
        q_norm[l][None, :], k_norm[l][None, :], _rope_tables(ts), tp, ts)
    new_k = jnp.swapaxes(new_k_t, 3, 4)
    new_v = jnp.swapaxes(new_v_t, 3, 4)

    og_p, s_f, s_b, osw_p, wo, w1, w2 = _gla(
        gla_in, gates, norm2, None, None, bp, tp, 0, True, seqs=2,
        attention=(sink_l, swa_in), cast_weights=(w_o[l], w_ff1[l], w_ff2[l]))
    og_s, osw_s = _gla(gla_in, gates, norm2, state_gla_fwd, state_gla_bwd, bs, ts, n_p // ts, False,
                       attention=(sink_l, swa_in, cache_k_t, cache_v_t))

    yp, ys = _mix_out(xp, xs, og_p, og_s, osw_p, osw_s, mod, gain_ff, wo, w1, w2, ts)
    return (yp.reshape(bp, tp, D_MODEL), ys.reshape(bs, ts, D_MODEL), new_k, new_v, s_f, s_b)
```

```python
import functools

import numpy as np
import jax
import jax.numpy as jnp
from jax import lax
from jax.experimental import pallas as pl
from jax.experimental.pallas import tpu as pltpu

F32 = jnp.float32
BF16 = jnp.bfloat16

D_MODEL = 1024
GRID_W = 64
HEAD_DIM = 64
GLA_HEADS = 8
GLA_WIDTH = GLA_HEADS * HEAD_DIM
GATE_RANK = 16
GATE_TEMP = 16.0
CHUNK = 64
GROUP = 4 * CHUNK
SWA_HEADS = 8
SWA_KV_HEADS = 2
SWA_GROUP = SWA_HEADS // SWA_KV_HEADS
SWA_WIDTH = SWA_HEADS * HEAD_DIM
KV_WIDTH = SWA_KV_HEADS * HEAD_DIM
WINDOW = 128
QBLOCK = 128
D_FF = 4 * D_MODEL
ROPE_THETA = 10000.0
EPS = 1e-6
NEG_INF = -1e30
LOG2E = 1.4426950408889634
LANES = 128
TM = 512
FF_TILE = 1024

D_IN = 2848
R_GQ, R_GK, R_GV, R_GG, R_LR = 0, 512, 1024, 1536, 2048
R_SQ = R_LR + 2 * GATE_RANK
R_SV = R_SQ + SWA_WIDTH + KV_WIDTH
C_TAIL = R_LR
C_LR = C_TAIL + D_IN - R_SQ
W_COLS = C_LR + LANES

V7X_VMEM_BYTES = 64 * 1024 * 1024
VMEM_LIMIT = V7X_VMEM_BYTES - 8 * 1024 * 1024


def _cparams(n_axes):
    return pltpu.CompilerParams(dimension_semantics=("arbitrary",) * n_axes, vmem_limit_bytes=VMEM_LIMIT)


def _mod_kernel(cctx_ref, c_ref, w_ref, b_ref, o_ref):
    k = pl.program_id(0)
    pad = jnp.zeros((8 - 1 - c_ref.shape[0], c_ref.shape[1]), F32)
    cnd = jnp.concatenate([cctx_ref[...], c_ref[...], pad], axis=0)
    s = (cnd * jax.nn.sigmoid(cnd)).astype(BF16)
    part = _dot(s, w_ref[...].astype(BF16))

    @pl.when(k == 0)
    def _():
        o_ref[...] = part + b_ref[...]

    @pl.when(k > 0)
    def _():
        o_ref[...] += part


def _modulation(c_ctx, c, w_ada, b_ada):
    tk = 256
    n_out = w_ada.shape[1]
    return pl.pallas_call(
        _mod_kernel,
        grid=(D_MODEL // tk,),
        in_specs=[pl.BlockSpec((1, tk), lambda k: (0, k)),
                  pl.BlockSpec((c.shape[0], tk), lambda k: (0, k)),
                  pl.BlockSpec((tk, n_out), lambda k: (k, 0)),
                  pl.BlockSpec((1, n_out), lambda k: (0, 0))],
        out_specs=pl.BlockSpec((8, n_out), lambda k: (0, 0)),
        out_shape=jax.ShapeDtypeStruct((8, n_out), F32),
        compiler_params=_cparams(1),
        name="modulation",
    )(c_ctx, c, w_ada, b_ada.reshape(1, n_out))


def _mod_row(tile, n_prompt_tiles, tiles_per_seq):
    return jnp.where(tile < n_prompt_tiles, 0, 1 + (tile - n_prompt_tiles) // tiles_per_seq)


def _mod_chunk(mod_ref, row, j):
    return mod_ref[pl.ds(row, 1), j * D_MODEL:(j + 1) * D_MODEL]


def _adaln(x, gain, shift, scale):
    ms = jnp.mean(x * x, axis=-1, keepdims=True)
    xn = x * lax.rsqrt(ms + EPS)
    return xn * (gain * (1.0 + scale)) + shift


def _head_rms_norm(y, gain):
    cols = y.shape[1]
    lane = lax.broadcasted_iota(jnp.int32, (1, LANES), 1)
    first = lane < HEAD_DIM
    outs = []
    for p in range(cols // LANES):
        s = y[:, p * LANES:(p + 1) * LANES]
        s = s * s
        tot = jnp.sum(s, axis=-1, keepdims=True)
        lo = jnp.sum(jnp.where(first, s, 0.0), axis=-1, keepdims=True)
        outs.append(jnp.where(first, lo, tot - lo))
    sums = outs[0] if len(outs) == 1 else jnp.concatenate(outs, axis=-1)
    return y * lax.rsqrt(sums + HEAD_DIM * EPS) * (gain * (HEAD_DIM ** 0.5))


def _log_sigmoid(x):
    return jnp.minimum(x, 0.0) - jnp.log(1.0 + jnp.exp(-jnp.abs(x)))


def _dot(a, b):
    return jnp.dot(a, b, preferred_element_type=F32)


def _dot_nt(a, b):
    return lax.dot_general(a, b, (((1,), (1,)), ((), ())), preferred_element_type=F32)


def _proj_kernel(xp_ref, xs_ref, mod_ref, gain_ref, w_ref, wgf_ref, wgb_ref, bgf_ref, bgb_ref, qn_ref, kn_ref,
                 cos_ref, sin_a_ref, sin_b_ref,
                 gla_ref, gate_ref, swa_ref, nk_ref, nv_ref,
                 wb_ref, wg_ref, h0_ref, h1_ref, lr0_ref, lr1_ref,
                 *, n_prompt_tiles, n_tiles, tiles_per_seq, seqs_per_tile, seq_len):
    h_refs = (h0_ref, h1_ref)
    lr_refs = (lr0_ref, lr1_ref)
    s = pl.program_id(0)

    @pl.when(s == 0)
    def _():
        moves = ([(c * LANES, c * LANES) for c in range(R_LR // LANES)]
                 + [(R_SQ + c * LANES, C_TAIL + c * LANES) for c in range((D_IN - R_SQ) // LANES)]
                 + [(R_LR, C_LR)])
        for src, dst in moves:
            wb_ref[:, dst:dst + LANES] = w_ref[src:src + LANES, :].T.astype(BF16)
        wg_ref[...] = jnp.zeros(wg_ref.shape, BF16)
        wg_ref[0:GATE_RANK, 0:GLA_WIDTH] = wgf_ref[...].astype(BF16)
        wg_ref[GATE_RANK:2 * GATE_RANK, GLA_WIDTH:] = wgb_ref[...].astype(BF16)

    def normalise_next(parity):
        x = jnp.where(s < n_prompt_tiles, xp_ref[...], xs_ref[...])
        row = _mod_row(jnp.minimum(s, n_tiles - 1), n_prompt_tiles, tiles_per_seq)
        shift, scale = _mod_chunk(mod_ref, row, 0), _mod_chunk(mod_ref, row, 1)
        h_refs[parity][...] = _adaln(x, gain_ref[...], shift, scale).astype(BF16)

    def gates(parity):
        gate_pre = _dot(lr_refs[parity][...], wg_ref[...])
        scale = LOG2E / GATE_TEMP
        gate_ref[:, :GLA_WIDTH] = (_log_sigmoid(gate_pre[:, :GLA_WIDTH] + bgf_ref[...]) * scale).astype(BF16)
        gate_ref[:, GLA_WIDTH:] = (_log_sigmoid(gate_pre[:, GLA_WIDTH:] + bgb_ref[...]) * scale).astype(BF16)

    def body(rope, emit_cache, parity):
        h_ref = h_refs[1 - parity]
        gates(parity)

        def proj(c0, c1):
            return _dot(h_ref[...], wb_ref[:, c0:c1])

        tail = proj(C_TAIL, C_LR)
        lr_refs[1 - parity][...] = proj(C_LR, W_COLS).astype(BF16)
        gg = proj(R_GG, R_LR)
        gla_ref[:, 3 * GLA_WIDTH:] = (gg * jax.nn.sigmoid(gg)).astype(BF16)
        attention_inputs(tail, rope, emit_cache)
        gla_ref[:, :GLA_WIDTH] = (proj(R_GQ, R_GK) * (HEAD_DIM ** -0.5)).astype(BF16)
        gla_ref[:, GLA_WIDTH:2 * GLA_WIDTH] = proj(R_GK, R_GV).astype(BF16)
        gla_ref[:, 2 * GLA_WIDTH:3 * GLA_WIDTH] = proj(R_GV, R_GG).astype(BF16)
        normalise_next(parity)

    def attention_inputs(tail, rope, emit_cache):
        qk = tail[:, :R_SV - R_SQ]
        qk_gain = jnp.concatenate([qn_ref[...]] * SWA_HEADS + [kn_ref[...]] * SWA_KV_HEADS, axis=-1)
        qk = _head_rms_norm(qk, qk_gain)
        if rope:
            reps = (SWA_WIDTH + KV_WIDTH) // LANES
            cos = jnp.concatenate([cos_ref[...]] * reps, axis=-1)
            sin_a = jnp.concatenate([sin_a_ref[...]] * reps, axis=-1)
            sin_b = jnp.concatenate([sin_b_ref[...]] * reps, axis=-1)
            width = qk.shape[1]
            quarter = HEAD_DIM // 4
            qk = (qk * cos + pltpu.roll(qk, quarter, 1) * sin_a
                  + pltpu.roll(qk, width - quarter, 1) * sin_b)
        swa_ref[:, :SWA_WIDTH] = (qk[:, :SWA_WIDTH] * (HEAD_DIM ** -0.5 * LOG2E)).astype(BF16)
        sk = qk[:, SWA_WIDTH:]
        swa_ref[:, SWA_WIDTH:SWA_WIDTH + KV_WIDTH] = sk.astype(BF16)
        sv = tail[:, R_SV - R_SQ:]
        swa_ref[:, SWA_WIDTH + KV_WIDTH:] = sv.astype(BF16)
        if emit_cache:
            sk_t, sv_t = sk.T, sv.T
            for q in range(seqs_per_tile):
                cols = slice(q * seq_len, (q + 1) * seq_len)
                for g in range(SWA_KV_HEADS):
                    feat = slice(g * HEAD_DIM, (g + 1) * HEAD_DIM)
                    nk_ref[q, 0, g, :, :] = sk_t[feat, cols]
                    nv_ref[q, 0, g, :, :] = sv_t[feat, cols]

    @pl.when(s == 0)
    def _():
        normalise_next(0)
        lr_refs[1][...] = jnp.zeros(lr_refs[1].shape, BF16)

    for parity in range(2):
        on_parity = (s % 2) == parity

        @pl.when((s >= 1) & (s <= n_prompt_tiles) & on_parity)
        def _():
            body(rope=False, emit_cache=True, parity=parity)

        @pl.when((s > n_prompt_tiles) & (s <= n_tiles) & on_parity)
        def _():
            body(rope=True, emit_cache=False, parity=parity)

    @pl.when(s == n_tiles + 1)
    def _():
        gates((n_tiles + 1) % 2)


def _project(xp, xs, mod, gain, w_in, w_gk_f, w_gk_b, b_gk_f, b_gk_b, q_norm, k_norm, rope_tabs, seq_p, seq_s):
    n_p, n_s = xp.shape[0], xs.shape[0]
    tp, ts = n_p // TM, n_s // TM
    n = n_p + n_s
    seqs_per_tile = TM // seq_p
    tiles_per_seq = seq_s // TM
    n_seq_p = n_p // seq_p
    const = lambda shape: pl.BlockSpec(shape, lambda s: (0,) * len(shape))
    n_tiles = tp + ts
    out_tile = lambda s: jnp.clip(s - 1, 0, n_tiles - 1)
    gate_tile = lambda s: jnp.maximum(s - 2, 0)
    rope_spec = pl.BlockSpec((TM, LANES), lambda s: (jnp.maximum(out_tile(s) - tp, 0) % tiles_per_seq, 0))
    in_specs = [pl.BlockSpec((TM, D_MODEL), lambda s: (jnp.minimum(s, tp - 1), 0)),
                pl.BlockSpec((TM, D_MODEL), lambda s: (jnp.clip(s - tp, 0, ts - 1), 0)),
                const(mod.shape),
                const((1, D_MODEL)),
                pl.BlockSpec((D_IN, D_MODEL), lambda s: (0, 0), pipeline_mode=pl.Buffered(1)),
                const(w_gk_f.shape), const(w_gk_b.shape), const(b_gk_f.shape), const(b_gk_b.shape),
                const(q_norm.shape), const(k_norm.shape),
                rope_spec, rope_spec, rope_spec]
    row = lambda w: pl.BlockSpec((TM, w), lambda s: (out_tile(s), 0))
    cache_spec = pl.BlockSpec((seqs_per_tile, 1, SWA_KV_HEADS, HEAD_DIM, seq_p),
                              lambda s: (jnp.minimum(out_tile(s), tp - 1), 0, 0, 0, 0))
    widths = (4 * GLA_WIDTH, 2 * GLA_WIDTH, SWA_WIDTH + 2 * KV_WIDTH)
    out_specs = [row(widths[0]), pl.BlockSpec((TM, widths[1]), lambda s: (gate_tile(s), 0)), row(widths[2]),
                 cache_spec, cache_spec]
    cache_shape = (n_seq_p, 1, SWA_KV_HEADS, HEAD_DIM, seq_p)
    out_shape = ([jax.ShapeDtypeStruct((n, w), BF16) for w in widths]
                 + [jax.ShapeDtypeStruct(cache_shape, F32)] * 2)
    return pl.pallas_call(
        functools.partial(_proj_kernel, n_prompt_tiles=tp, n_tiles=n_tiles, tiles_per_seq=tiles_per_seq,
                          seqs_per_tile=seqs_per_tile, seq_len=seq_p),
        grid=(n_tiles + 2,), in_specs=in_specs, out_specs=out_specs, out_shape=out_shape,
        scratch_shapes=([pltpu.VMEM((D_MODEL, W_COLS), BF16), pltpu.VMEM((LANES, 2 * GLA_WIDTH), BF16)]
                        + [pltpu.VMEM((TM, D_MODEL), BF16)] * 2 + [pltpu.VMEM((TM, LANES), BF16)] * 2),
        compiler_params=_cparams(1), name="adaln_in_proj",
    )(xp, xs, mod, gain, w_in, w_gk_f, w_gk_b, b_gk_f, b_gk_b, q_norm, k_norm, *rope_tabs)


def _gla_kernel(*refs, seq_len, seqs, has_state, emit_state, with_attention, n_cast):
    (q_ref, k_ref, v_ref, gf_ref, gb_ref, sgg_ref, norm_ref,
     trif_ref, trib_ref, keepf_ref, keepb_ref) = refs[:11]
    pos = 11
    if has_state:
        s0f_ref, s0b_ref = refs[pos:pos + 2]
        pos += 2
    if with_attention:
        sink_ref, aq_ref, ak_ref, av_ref = refs[pos:pos + 4]
        pos += 4
    if with_attention == "latent":
        kc_ref, vc_ref = refs[pos:pos + 2]
        pos += 2
    cast_in = refs[pos:pos + n_cast]
    pos += n_cast
    og_ref = refs[pos]
    pos += 1
    if emit_state:
        sf_ref, sb_ref = refs[pos:pos + 2]
        pos += 2
    if with_attention:
        osw_ref = refs[pos]
        pos += 1
    cast_out = refs[pos:pos + n_cast]
    pos += n_cast
    accf_ref, accb_ref = refs[pos:pos + 2]

    for src, dst in zip(cast_in, cast_out):
        dst[...] = src[...].astype(BF16)

    n_groups = seq_len // GROUP
    per_group = GROUP // CHUNK
    n_pairs = q_ref.shape[1] // LANES
    lane = lax.broadcasted_iota(jnp.int32, (1, LANES), 1)
    head0 = lane < HEAD_DIM
    m0 = jnp.where(head0, 1.0, 0.0).astype(BF16)
    m1 = jnp.where(head0, 0.0, 1.0).astype(BF16)
    row_top = lax.broadcasted_iota(jnp.int32, (LANES, 1), 0) < HEAD_DIM

    def stack_heads(x):
        return jnp.concatenate([x * m0, x * m1], axis=0)

    def direction(row0, lanes, g_ref, tri_ref, keep_ref, state, acc_ref, backward):
        rows = pl.ds(row0, GROUP)
        b = _dot(tri_ref[...], g_ref[rows, lanes])
        yield
        b = b.reshape(per_group, CHUNK, LANES)
        edge = 0 if backward else CHUNK - 1
        mid = CHUNK // 2
        b_edge = b[:, edge:edge + 1, :]
        b_mid = b[:, mid:mid + 1, :]
        q = q_ref[rows, lanes].astype(F32).reshape(per_group, CHUNK, LANES)
        k = k_ref[rows, lanes].astype(F32).reshape(per_group, CHUNK, LANES)
        v = v_ref[rows, lanes]
        q_mid = q * jnp.exp2(b - b_mid)
        k_mid = k * jnp.exp2(b_mid - b)
        q_in = q_mid.astype(BF16)
        k_in = k_mid.astype(BF16).reshape(GROUP, LANES)
        q_st = (q_mid * jnp.exp2(b_mid)).astype(BF16)
        k_st = (k_mid * jnp.exp2(b_edge - b_mid)).astype(BF16).reshape(GROUP, LANES)
        decay = jnp.exp2(b_edge)
        keep = keep_ref[...]
        pair_rows = 2 * CHUNK
        n_cp = per_group // 2
        yield
        def pair_scores(p):
            c0 = 2 * p
            q_pair = jnp.concatenate([stack_heads(q_in[c0]), stack_heads(q_in[c0 + 1])], axis=0)
            return _dot_nt(q_pair, k_in[p * pair_rows:(p + 1) * pair_rows, :]).astype(BF16) * keep

        def pair_kv(p):
            prow = slice(p * pair_rows, (p + 1) * pair_rows)
            v_t = v[prow, :].astype(F32).T.astype(BF16)
            k_pair = k_st[prow, :]
            zero = jnp.zeros_like(k_pair)
            k_blk = jnp.concatenate([jnp.where(row_top, k_pair, zero), jnp.where(row_top, zero, k_pair)], axis=-1)
            return _dot(v_t, k_blk)

        if n_groups == 1:
            scores, kvs = zip(*[(pair_scores(p), pair_kv(p)) for p in range(n_cp)])
        else:
            scores = [pair_scores(p) for p in range(n_cp)]
            yield
            kvs = [pair_kv(p) for p in range(n_cp)]
        yield
        intras = [_dot(scores[p], v[p * pair_rows:(p + 1) * pair_rows, :]) for p in range(n_cp)]
        yield
        for p in (range(n_cp - 1, -1, -1) if backward else range(n_cp)):
            for j in ((1, 0) if backward else (0, 1)):
                c = 2 * p + j
                inter = _dot_nt(stack_heads(q_st[c]), state.astype(BF16))
                tot = intras[p][j * pair_rows:(j + 1) * pair_rows, :] + inter
                acc_ref[pl.ds(row0 + c * CHUNK, CHUNK), lanes] = jnp.where(head0, tot[:CHUNK, :], tot[CHUNK:, :])
                state = state * decay[c] + kvs[p][:, j * LANES:(j + 1) * LANES]
            yield
        return state

    def load_state(s_ref, q, p):
        z = jnp.zeros((HEAD_DIM, HEAD_DIM), F32)
        top = jnp.concatenate([s_ref[q, 0, 2 * p, :, :].T, z], axis=-1)
        bot = jnp.concatenate([z, s_ref[q, 0, 2 * p + 1, :, :].T], axis=-1)
        return jnp.concatenate([top, bot], axis=0)

    per_dir = seqs * n_pairs
    if has_state:
        init = tuple(load_state(s_ref, q, p)
                     for s_ref in (s0f_ref, s0b_ref) for q in range(seqs) for p in range(n_pairs))
    else:
        init = (jnp.zeros((LANES, LANES), F32),) * (2 * per_dir)

    def body(t, carry):
        row_f = t * GROUP
        row_b = (n_groups - 1 - t) * GROUP
        if n_groups > 1:
            row_f = pl.multiple_of(row_f, GROUP)
            row_b = pl.multiple_of(row_b, GROUP)
        units = []
        for backward, row, g_ref, tri_ref, keep_ref, acc_ref in (
                (False, row_f, gf_ref, trif_ref, keepf_ref, accf_ref),
                (True, row_b, gb_ref, trib_ref, keepb_ref, accb_ref)):
            for q in range(seqs):
                for p in range(n_pairs):
                    lanes = slice(p * LANES, (p + 1) * LANES)
                    state = carry[backward * per_dir + q * n_pairs + p]
                    units.append(direction(q * seq_len + row, lanes, g_ref, tri_ref, keep_ref, state, acc_ref,
                                           backward))
        return units

    norm_pair = jnp.concatenate([norm_ref[...]] * 2, axis=-1)

    def finish_outputs(rows):
        for p in range(n_pairs):
            lanes = slice(p * LANES, (p + 1) * LANES)
            o = _head_rms_norm(accf_ref[rows, lanes] + accb_ref[rows, lanes], norm_pair)
            og_ref[rows, lanes] = (o * sgg_ref[rows, lanes].astype(F32)).astype(BF16)
            yield

    if n_groups == 1:
        gla_units = body(0, init)
        final = [None] * len(gla_units)
        for q in range(seqs + 1):
            rows = slice(q * seq_len, (q + 1) * seq_len)
            wave, ids = [], []
            if q < seqs:
                ids = [d * per_dir + q * n_pairs + p for d in range(2) for p in range(n_pairs)]
                wave = [gla_units[u] for u in ids]
                if with_attention:
                    wave += _context_attention_units(sink_ref, aq_ref, ak_ref, av_ref, rows)
            if q > 0:
                wave.append(finish_outputs(slice((q - 1) * seq_len, q * seq_len)))
            results = _run_staged(wave)
            for u, r in zip(ids, results):
                final[u] = r
            if q < seqs and with_attention:
                heads = [o for outs in results[len(ids):len(ids) + SWA_KV_HEADS] for o in outs]
                osw_ref[rows, :] = jnp.concatenate(heads, axis=-1).astype(BF16)
    else:
        assert with_attention != "context" and seqs == 1
        if with_attention:
            ctx_kv = [(kc_ref[0, 0, g, :, :].astype(BF16), vc_ref[0, 0, g, :, :].astype(BF16))
                      for g in range(SWA_KV_HEADS)]
            q_blocks = GROUP // QBLOCK

        def loop_body(t, carry):
            units = body(t, carry)
            n_gla = len(units)
            if with_attention:
                for j in range(q_blocks):
                    units += _latent_attention_units(sink_ref, aq_ref, ak_ref, av_ref, ctx_kv,
                                                     t * q_blocks + j, seq_len)
            results = _run_staged(units)
            if with_attention:
                for j in range(q_blocks):
                    heads = [o for outs in results[n_gla + j * SWA_KV_HEADS:n_gla + (j + 1) * SWA_KV_HEADS]
                             for o in outs]
                    rows = pl.ds(pl.multiple_of((t * q_blocks + j) * QBLOCK, QBLOCK), QBLOCK)
                    osw_ref[rows, :] = jnp.concatenate(heads, axis=-1).astype(BF16)
            return tuple(results[:n_gla])

        final = lax.fori_loop(0, n_groups, loop_body, init)
        _run_staged([finish_outputs(slice(0, seq_len))])
    if emit_state:
        for d, s_ref in enumerate((sf_ref, sb_ref)):
            for q in range(seqs):
                for p in range(n_pairs):
                    fin = final[d * per_dir + q * n_pairs + p]
                    s_ref[q, 0, 2 * p, :, :] = fin[:HEAD_DIM, :HEAD_DIM].T
                    s_ref[q, 0, 2 * p + 1, :, :] = fin[HEAD_DIM:, HEAD_DIM:].T


def _gla_constants():
    i = np.arange(GROUP)[:, None]
    j = np.arange(GROUP)[None, :]
    same = (i // CHUNK) == (j // CHUNK)
    tri_f = same & (j <= i)
    tri_b = same & (j >= i)
    r = np.arange(4 * CHUNK)[:, None]
    l = np.arange(LANES)[None, :]
    same = (r // (2 * CHUNK)) == (l // CHUNK)
    keep_f = same & ((l % CHUNK) <= (r % CHUNK))
    keep_b = same & ((l % CHUNK) >= (r % CHUNK))
    return tuple(jnp.asarray(m.astype(np.float32), dtype=BF16) for m in (tri_f, tri_b, keep_f, keep_b))


def _gla(gla_in, gates, norm_row, s0f, s0b, n_seq, seq_len, row_block0, emit_state, seqs=1,
         attention=None, cast_weights=()):
    has_state = s0f is not None
    steps = n_seq // seqs
    rows_per_step = seqs * seq_len
    row_blk = row_block0 // seqs
    col = lambda width, j: pl.BlockSpec((rows_per_step, width), lambda b: (row_blk + b, j))
    out_blk = pl.BlockSpec((rows_per_step, GLA_WIDTH), lambda b: (b, 0))
    st_spec = pl.BlockSpec((seqs, 1, GLA_HEADS, HEAD_DIM, HEAD_DIM), lambda b: (b, 0, 0, 0, 0))
    const = lambda shape: pl.BlockSpec(shape, lambda b: (0, 0))
    in_specs = ([col(GLA_WIDTH, 0), col(GLA_WIDTH, 1), col(GLA_WIDTH, 2), col(GLA_WIDTH, 0), col(GLA_WIDTH, 1),
                 col(GLA_WIDTH, 3)]
                + [const(norm_row.shape)] + [const((GROUP, GROUP))] * 2 + [const((4 * CHUNK, LANES))] * 2)
    args = [gla_in, gla_in, gla_in, gates, gates, gla_in, norm_row, *_gla_constants()]
    if has_state:
        in_specs += [st_spec] * 2
        args += [s0f, s0b]
    out_specs = [out_blk]
    out_shape = [jax.ShapeDtypeStruct((n_seq * seq_len, GLA_WIDTH), BF16)]
    if emit_state:
        out_specs += [st_spec] * 2
        out_shape += [jax.ShapeDtypeStruct((n_seq, 1, GLA_HEADS, HEAD_DIM, HEAD_DIM), F32)] * 2
    with_attention = None
    if attention is not None:
        sink, swa_in = attention[:2]
        with_attention = "context"
        in_specs += [pl.BlockSpec(memory_space=pltpu.SMEM), col(SWA_WIDTH, 0),
                     col(KV_WIDTH, SWA_WIDTH // KV_WIDTH), col(KV_WIDTH, SWA_WIDTH // KV_WIDTH + 1)]
        args += [sink, swa_in, swa_in, swa_in]
        if len(attention) > 2:
            with_attention = "latent"
            cache_k_t, cache_v_t = attention[2:]
            past = cache_k_t.shape[4]
            cache_spec = pl.BlockSpec((seqs, 1, SWA_KV_HEADS, HEAD_DIM, past), lambda b: (b, 0, 0, 0, 0))
            in_specs += [cache_spec] * 2
            args += [cache_k_t, cache_v_t]
        out_specs.append(pl.BlockSpec((rows_per_step, SWA_WIDTH), lambda b: (b, 0)))
        out_shape.append(jax.ShapeDtypeStruct((n_seq * seq_len, SWA_WIDTH), BF16))
    cast_specs, cast_shapes = _cast_specs(cast_weights, steps)
    in_specs += cast_specs
    args += list(cast_weights)
    out_specs += cast_specs
    out_shape += cast_shapes
    return pl.pallas_call(
        functools.partial(_gla_kernel, seq_len=seq_len, seqs=seqs, has_state=has_state, emit_state=emit_state,
                          with_attention=with_attention, n_cast=len(cast_weights)),
        grid=(steps,), in_specs=in_specs, out_specs=out_specs, out_shape=out_shape,
        scratch_shapes=[pltpu.VMEM((rows_per_step, GLA_WIDTH), F32)] * 2,
        compiler_params=_cparams(1), name="gla_bidirectional",
    )(*args)


def _group_attention(q, heads, sink_ref, score_fn, value_fn):
    rows = q.shape[0]
    q_stack = jnp.concatenate([q[:, h * HEAD_DIM:(h + 1) * HEAD_DIM] for h in heads], axis=0)
    scores = score_fn(q_stack)
    yield
    slabs = [slice(j * rows, (j + 1) * rows) for j in range(len(heads))]
    ms = []
    sinks = [sink_ref[h] * LOG2E for h in heads]
    for slab, sink in zip(slabs, sinks):
        m = sink
        for s in scores:
            m = jnp.maximum(m, jnp.max(s[slab, :], axis=-1, keepdims=True))
        ms.append(m)
    yield
    e_parts = [[] for _ in scores]
    denoms = []
    for slab, sink, m in zip(slabs, sinks, ms):
        denom = jnp.exp2(sink - m)
        for part, s in zip(e_parts, scores):
            e = jnp.exp2(s[slab, :] - m)
            denom = denom + jnp.sum(e, axis=-1, keepdims=True)
            part.append(e.astype(BF16))
        denoms.append(denom)
    yield
    acc = value_fn([jnp.concatenate(part, axis=0) for part in e_parts])
    yield
    return [acc[slab, :] / denom for slab, denom in zip(slabs, denoms)]


def _run_staged(units):
    results = [None] * len(units)
    live = list(range(len(units)))
    while live:
        still = []
        for u in live:
            try:
                next(units[u])
                still.append(u)
            except StopIteration as stop:
                results[u] = stop.value
        live = still
    return results


def _context_attention_units(sink_ref, q_ref, k_ref, v_ref, rows):
    q = q_ref[rows, :]
    k = k_ref[rows, :]
    v = v_ref[rows, :]
    units = []
    for g in range(SWA_KV_HEADS):
        k_g = k[:, g * HEAD_DIM:(g + 1) * HEAD_DIM]
        v_g = v[:, g * HEAD_DIM:(g + 1) * HEAD_DIM]
        units.append(_group_attention(
            q, list(range(SWA_GROUP * g, SWA_GROUP * (g + 1))), sink_ref,
            lambda qs, k_g=k_g: [_dot_nt(qs, k_g)],
            lambda es, v_g=v_g: _dot(es[0], v_g)))
    return units


def _cast_specs(cast_weights, steps, step_of=lambda *idx: idx[0]):
    specs = [pl.BlockSpec((w.shape[0] // steps, w.shape[1]), lambda *idx: (step_of(*idx), 0))
             for w in cast_weights]
    shapes = [jax.ShapeDtypeStruct(w.shape, BF16) for w in cast_weights]
    return specs, shapes


def _latent_attention_units(sink_ref, q_ref, k_ref, v_ref, ctx_kv, i, seq_len):
    span = QBLOCK + 2 * WINDOW
    start = pl.multiple_of(jnp.clip(i * QBLOCK - WINDOW, 0, seq_len - span), QBLOCK)
    q = q_ref[pl.ds(pl.multiple_of(i * QBLOCK, QBLOCK), QBLOCK), :]
    k = k_ref[pl.ds(start, span), :]
    v = v_ref[pl.ds(start, span), :]
    qpos = i * QBLOCK + lax.broadcasted_iota(jnp.int32, (QBLOCK, span), 0)
    kpos = start + lax.broadcasted_iota(jnp.int32, (QBLOCK, span), 1)
    valid = jnp.abs(qpos - kpos) <= WINDOW
    valid = jnp.concatenate([valid] * SWA_GROUP, axis=0)
    units = []
    for g in range(SWA_KV_HEADS):
        k_g = k[:, g * HEAD_DIM:(g + 1) * HEAD_DIM]
        v_g = v[:, g * HEAD_DIM:(g + 1) * HEAD_DIM]
        kc_t, vc_t = ctx_kv[g]
        units.append(_group_attention(
            q, list(range(SWA_GROUP * g, SWA_GROUP * (g + 1))), sink_ref,
            lambda qs, k_g=k_g, kc_t=kc_t: [jnp.where(valid, _dot_nt(qs, k_g), NEG_INF), _dot(qs, kc_t)],
            lambda es, v_g=v_g, vc_t=vc_t: _dot(es[0], v_g) + _dot_nt(es[1], vc_t)))
    return units


def _out_kernel(xp_ref, xs_ref, ogp_ref, ogs_ref, oswp_ref, osws_ref, mod_ref, gain_ref,
                wo_ref, w1_ref, w2_ref, yp_ref, ys_ref, *, n_prompt_tiles, tiles_per_seq):
    i = pl.program_id(0)
    row = _mod_row(i, n_prompt_tiles, tiles_per_seq)

    def body(x_ref, og_ref, osw_ref, y_ref):
        mix = _dot(og_ref[...], wo_ref[:GLA_WIDTH, :]) + _dot(osw_ref[...], wo_ref[GLA_WIDTH:, :])
        x1 = x_ref[...] + _mod_chunk(mod_ref, row, 2) * mix
        h = _adaln(x1, gain_ref[...], _mod_chunk(mod_ref, row, 3), _mod_chunk(mod_ref, row, 4)).astype(BF16)
        acc = None
        for j in range(D_FF // FF_TILE):
            cols = slice(j * FF_TILE, (j + 1) * FF_TILE)
            a = jnp.maximum(_dot(h, w1_ref[:, cols]), 0.0)
            part = _dot((a * a).astype(BF16), w2_ref[cols, :])
            acc = part if acc is None else acc + part
        y_ref[...] = x1 + _mod_chunk(mod_ref, row, 5) * acc

    @pl.when(i < n_prompt_tiles)
    def _():
        body(xp_ref, ogp_ref, oswp_ref, yp_ref)

    @pl.when(i >= n_prompt_tiles)
    def _():
        body(xs_ref, ogs_ref, osws_ref, ys_ref)


def _mix_out(xp, xs, og_p, og_s, osw_p, osw_s, mod, gain, wo, w1, w2, seq_s):
    n_p, n_s = xp.shape[0], xs.shape[0]
    tp, ts = n_p // TM, n_s // TM
    tiles_per_seq = seq_s // TM
    const = lambda shape: pl.BlockSpec(shape, lambda i: (0,) * len(shape), pipeline_mode=pl.Buffered(1))
    row_p = lambda w: pl.BlockSpec((TM, w), lambda i: (jnp.minimum(i, tp - 1), 0))
    row_s = lambda w: pl.BlockSpec((TM, w), lambda i: (jnp.maximum(i - tp, 0), 0))
    return pl.pallas_call(
        functools.partial(_out_kernel, n_prompt_tiles=tp, tiles_per_seq=tiles_per_seq),
        grid=(tp + ts,),
        in_specs=[row_p(D_MODEL), row_s(D_MODEL), row_p(GLA_WIDTH), row_s(GLA_WIDTH),
                  row_p(SWA_WIDTH), row_s(SWA_WIDTH),
                  const(mod.shape),
                  const((1, D_MODEL)), const((GLA_WIDTH + SWA_WIDTH, D_MODEL)),
                  const((D_MODEL, D_FF)), const((D_FF, D_MODEL))],
        out_specs=[row_p(D_MODEL), row_s(D_MODEL)],
        out_shape=[jax.ShapeDtypeStruct((n_p, D_MODEL), F32), jax.ShapeDtypeStruct((n_s, D_MODEL), F32)],
        compiler_params=_cparams(1), name="out_proj_mlp",
    )(xp, xs, og_p, og_s, osw_p, osw_s, mod, gain, wo, w1, w2)


def _rope_tables(seq_len):
    axis_dim = HEAD_DIM // 2
    half = axis_dim // 2
    t = np.arange(seq_len)
    rowp = (t // GRID_W).astype(np.float64)
    colp = (t % GRID_W).astype(np.float64)
    freqs = ROPE_THETA ** (-np.arange(half, dtype=np.float64) / half)
    lane = np.arange(LANES)
    d = lane % HEAD_DIM
    posv = np.where((d // axis_dim)[None, :] == 0, rowp[:, None], colp[:, None])
    ang = posv * freqs[d % half][None, :]
    second = ((d % axis_dim) >= half)[None, :]
    cos, sin = np.cos(ang), np.sin(ang)
    zero = np.zeros_like(sin)
    tabs = (cos, np.where(second, sin, zero), np.where(second, zero, -sin))
    return tuple(jnp.asarray(tab.astype(np.float32)) for tab in tabs)


def kernel(x_prompt, x_sample, c, cache_k, cache_v, state_gla_fwd, state_gla_bwd, c_ctx, w_ada, b_ada, norm_mix, norm_ff, w_in, w_gk_fwd, b_gk_fwd, w_gk_bwd, b_gk_bwd, gla_norm, q_norm, k_norm, sink, w_o, w_ff1, w_ff2):
    depth = w_in.shape[0]
    assert depth == 1, "single trunk layer"
    bp, tp, _ = x_prompt.shape
    bs, ts, _ = x_sample.shape
    n_p = bp * tp
    l = 0

    mod = _modulation(c_ctx[None, :], c, w_ada[l], b_ada[l])

    norm2 = gla_norm[l][None, :]
    gain_mix = norm_mix[l][None, :]
    gain_ff = norm_ff[l][None, :]
    sink_l = sink[l]

    w_in_t = jnp.transpose(w_in[l])
    cache_k_t = jnp.swapaxes(cache_k, 3, 4)
    cache_v_t = jnp.swapaxes(cache_v, 3, 4)

    xp = x_prompt.reshape(n_p, D_MODEL)
    xs = x_sample.reshape(bs * ts, D_MODEL)
    gla_in, gates, swa_in, new_k_t, new_v_t = _project(
        xp, xs, mod, gain_mix, w_in_t, w_gk_fwd[l], w_gk_bwd[l], b_gk_fwd[l][None, :], b_gk_bwd[l][None, :],
        q_norm[l][None, :], k_norm[l][None, :], _rope_tables(ts), tp, ts)
    new_k = jnp.swapaxes(new_k_t, 3, 4)
    new_v = jnp.swapaxes(new_v_t, 3, 4)

    og_p, s_f, s_b, osw_p, wo, w1, w2 = _gla(
        gla_in, gates, norm2, None, None, bp, tp, 0, True, seqs=2,
        attention=(sink_l, swa_in), cast_weights=(w_o[l], w_ff1[l], w_ff2[l]))
    og_s, osw_s = _gla(gla_in, gates, norm2, state_gla_fwd, state_gla_bwd, bs, ts, n_p // ts, False,
                       attention=(sink_l, swa_in, cache_k_t, cache_v_t))

    yp, ys = _mix_out(xp, xs, og_p, og_s, osw_p, osw_s, mod, gain_ff, wo, w1, w2, ts)
    return (yp.reshape(bp, tp, D_MODEL), ys.reshape(bs, ts, D_MODEL), new_k, new_v, s_f, s_b)
```

```python
import functools

import numpy as np
import jax
import jax.numpy as jnp
from jax import lax
from jax.experimental import pallas as pl
from jax.experimental.pallas import tpu as pltpu

F32 = jnp.float32
BF16 = jnp.bfloat16

D_MODEL = 1024
GRID_W = 64
HEAD_DIM = 64
GLA_HEADS = 8
GLA_WIDTH = GLA_HEADS * HEAD_DIM
GATE_RANK = 16
GATE_TEMP = 16.0
CHUNK = 64
GROUP = 4 * CHUNK
SWA_HEADS = 8
SWA_KV_HEADS = 2
SWA_GROUP = SWA_HEADS // SWA_KV_HEADS
SWA_WIDTH = SWA_HEADS * HEAD_DIM
KV_WIDTH = SWA_KV_HEADS * HEAD_DIM
WINDOW = 128
QBLOCK = 128
D_FF = 4 * D_MODEL
ROPE_THETA = 10000.0
EPS = 1e-6
NEG_INF = -1e30
LOG2E = 1.4426950408889634
LANES = 128
TM = 512
FF_TILE = 1024

D_IN = 2848
R_GQ, R_GK, R_GV, R_GG, R_LR = 0, 512, 1024, 1536, 2048
R_SQ = R_LR + 2 * GATE_RANK
R_SV = R_SQ + SWA_WIDTH + KV_WIDTH
C_TAIL = R_LR
C_LR = C_TAIL + D_IN - R_SQ
W_COLS = C_LR + LANES

V7X_VMEM_BYTES = 64 * 1024 * 1024
VMEM_LIMIT = V7X_VMEM_BYTES - 8 * 1024 * 1024


def _cparams(n_axes):
    return pltpu.CompilerParams(dimension_semantics=("arbitrary",) * n_axes, vmem_limit_bytes=VMEM_LIMIT)


def _silu_cond(cctx_ref, c_ref):
    pad = jnp.zeros((8 - 1 - c_ref.shape[0], c_ref.shape[1]), F32)
    cnd = jnp.concatenate([cctx_ref[...], c_ref[...], pad], axis=0)
    return (cnd * jax.nn.sigmoid(cnd)).astype(BF16)


def _mod_kernel(cctx_ref, c_ref, w_ref, b_ref, o_ref):
    k = pl.program_id(0)
    part = _dot(_silu_cond(cctx_ref, c_ref), w_ref[...].astype(BF16))

    @pl.when(k == 0)
    def _():
        o_ref[...] = part + b_ref[...]

    @pl.when(k > 0)
    def _():
        o_ref[...] += part


def _modulation(c_ctx, c, w_ada, b_ada, n_out):
    tk = 256
    return pl.pallas_call(
        _mod_kernel,
        grid=(D_MODEL // tk,),
        in_specs=[pl.BlockSpec((1, tk), lambda k: (0, k)),
                  pl.BlockSpec((c.shape[0], tk), lambda k: (0, k)),
                  pl.BlockSpec((tk, n_out), lambda k: (k, 0)),
                  pl.BlockSpec((1, n_out), lambda k: (0, 0))],
        out_specs=pl.BlockSpec((8, n_out), lambda k: (0, 0)),
        out_shape=jax.ShapeDtypeStruct((8, n_out), F32),
        compiler_params=_cparams(1),
        name="modulation",
    )(c_ctx, c, w_ada, b_ada.reshape(1, -1))


def _mod_row(tile, n_prompt_tiles, tiles_per_seq):
    return jnp.where(tile < n_prompt_tiles, 0, 1 + (tile - n_prompt_tiles) // tiles_per_seq)


def _mod_chunk(mod_ref, row, j):
    return mod_ref[pl.ds(row, 1), j * D_MODEL:(j + 1) * D_MODEL]


def _adaln(x, gain, shift, scale):
    ms = jnp.mean(x * x, axis=-1, keepdims=True)
    xn = x * lax.rsqrt(ms + EPS)
    return xn * (gain * (1.0 + scale)) + shift


def _head_rms_norm(y, gain):
    cols = y.shape[1]
    lane = lax.broadcasted_iota(jnp.int32, (1, LANES), 1)
    first = lane < HEAD_DIM
    outs = []
    for p in range(cols // LANES):
        s = y[:, p * LANES:(p + 1) * LANES]
        s = s * s
        tot = jnp.sum(s, axis=-1, keepdims=True)
        lo = jnp.sum(jnp.where(first, s, 0.0), axis=-1, keepdims=True)
        outs.append(jnp.where(first, lo, tot - lo))
    sums = outs[0] if len(outs) == 1 else jnp.concatenate(outs, axis=-1)
    return y * lax.rsqrt(sums + HEAD_DIM * EPS) * (gain * (HEAD_DIM ** 0.5))


def _log_sigmoid(x):
    return jnp.minimum(x, 0.0) - jnp.log(1.0 + jnp.exp(-jnp.abs(x)))


def _dot(a, b):
    return jnp.dot(a, b, preferred_element_type=F32)


def _dot_nt(a, b):
    return lax.dot_general(a, b, (((1,), (1,)), ((), ())), preferred_element_type=F32)


def _proj_kernel(xp_ref, xs_ref, mod_ref, gain_ref, w_ref, wgf_ref, wgb_ref, bgf_ref, bgb_ref, qn_ref, kn_ref,
                 cos_ref, sin_a_ref, sin_b_ref,
                 gla_ref, gate_ref, swa_ref, nk_ref, nv_ref,
                 wb_ref, wg_ref, h0_ref, h1_ref, lr0_ref, lr1_ref,
                 *, n_prompt_tiles, n_tiles, tiles_per_seq, seqs_per_tile, seq_len):
    h_refs = (h0_ref, h1_ref)
    lr_refs = (lr0_ref, lr1_ref)
    s = pl.program_id(0)

    @pl.when(s == 0)
    def _():
        moves = ([(c * LANES, c * LANES) for c in range(R_LR // LANES)]
                 + [(R_SQ + c * LANES, C_TAIL + c * LANES) for c in range((D_IN - R_SQ) // LANES)]
                 + [(R_LR, C_LR)])
        for src, dst in moves:
            wb_ref[:, dst:dst + LANES] = w_ref[src:src + LANES, :].T.astype(BF16)
        wg_ref[...] = jnp.zeros(wg_ref.shape, BF16)
        wg_ref[0:GATE_RANK, 0:GLA_WIDTH] = wgf_ref[...].astype(BF16)
        wg_ref[GATE_RANK:2 * GATE_RANK, GLA_WIDTH:] = wgb_ref[...].astype(BF16)

    def normalise_next(parity):
        x = jnp.where(s < n_prompt_tiles, xp_ref[...], xs_ref[...])
        row = _mod_row(jnp.minimum(s, n_tiles - 1), n_prompt_tiles, tiles_per_seq)
        shift, scale = _mod_chunk(mod_ref, row, 0), _mod_chunk(mod_ref, row, 1)
        h_refs[parity][...] = _adaln(x, gain_ref[...], shift, scale).astype(BF16)

    def gates(parity):
        gate_pre = _dot(lr_refs[parity][...], wg_ref[...])
        scale = LOG2E / GATE_TEMP
        gate_ref[:, :GLA_WIDTH] = (_log_sigmoid(gate_pre[:, :GLA_WIDTH] + bgf_ref[...]) * scale).astype(BF16)
        gate_ref[:, GLA_WIDTH:] = (_log_sigmoid(gate_pre[:, GLA_WIDTH:] + bgb_ref[...]) * scale).astype(BF16)

    def body(rope, emit_cache, parity):
        h_ref = h_refs[1 - parity]
        gates(parity)

        def proj(c0, c1):
            return _dot(h_ref[...], wb_ref[:, c0:c1])

        tail = proj(C_TAIL, C_LR)
        lr_refs[1 - parity][...] = proj(C_LR, W_COLS).astype(BF16)
        gg = proj(R_GG, R_LR)
        gla_ref[:, 3 * GLA_WIDTH:] = (gg * jax.nn.sigmoid(gg)).astype(BF16)
        attention_inputs(tail, rope, emit_cache)
        gla_ref[:, :GLA_WIDTH] = (proj(R_GQ, R_GK) * (HEAD_DIM ** -0.5)).astype(BF16)
        gla_ref[:, GLA_WIDTH:2 * GLA_WIDTH] = proj(R_GK, R_GV).astype(BF16)
        gla_ref[:, 2 * GLA_WIDTH:3 * GLA_WIDTH] = proj(R_GV, R_GG).astype(BF16)
        normalise_next(parity)

    def attention_inputs(tail, rope, emit_cache):
        qk = tail[:, :R_SV - R_SQ]
        qk_gain = jnp.concatenate([qn_ref[...]] * SWA_HEADS + [kn_ref[...]] * SWA_KV_HEADS, axis=-1)
        qk = _head_rms_norm(qk, qk_gain)
        if rope:
            reps = (SWA_WIDTH + KV_WIDTH) // LANES
            cos = jnp.concatenate([cos_ref[...]] * reps, axis=-1)
            sin_a = jnp.concatenate([sin_a_ref[...]] * reps, axis=-1)
            sin_b = jnp.concatenate([sin_b_ref[...]] * reps, axis=-1)
            width = qk.shape[1]
            quarter = HEAD_DIM // 4
            qk = (qk * cos + pltpu.roll(qk, quarter, 1) * sin_a
                  + pltpu.roll(qk, width - quarter, 1) * sin_b)
        swa_ref[:, :SWA_WIDTH] = (qk[:, :SWA_WIDTH] * (HEAD_DIM ** -0.5 * LOG2E)).astype(BF16)
        sk = qk[:, SWA_WIDTH:]
        swa_ref[:, SWA_WIDTH:SWA_WIDTH + KV_WIDTH] = sk.astype(BF16)
        sv = tail[:, R_SV - R_SQ:]
        swa_ref[:, SWA_WIDTH + KV_WIDTH:] = sv.astype(BF16)
        if emit_cache:
            sk_t, sv_t = sk.T, sv.T
            for q in range(seqs_per_tile):
                cols = slice(q * seq_len, (q + 1) * seq_len)
                for g in range(SWA_KV_HEADS):
                    feat = slice(g * HEAD_DIM, (g + 1) * HEAD_DIM)
                    nk_ref[q, 0, g, :, :] = sk_t[feat, cols]
                    nv_ref[q, 0, g, :, :] = sv_t[feat, cols]

    @pl.when(s == 0)
    def _():
        normalise_next(0)
        lr_refs[1][...] = jnp.zeros(lr_refs[1].shape, BF16)

    for parity in range(2):
        on_parity = (s % 2) == parity

        @pl.when((s >= 1) & (s <= n_prompt_tiles) & on_parity)
        def _():
            body(rope=False, emit_cache=True, parity=parity)

        @pl.when((s > n_prompt_tiles) & (s <= n_tiles) & on_parity)
        def _():
            body(rope=True, emit_cache=False, parity=parity)

    @pl.when(s == n_tiles + 1)
    def _():
        gates((n_tiles + 1) % 2)


def _project(xp, xs, mod, gain, w_in, w_gk_f, w_gk_b, b_gk_f, b_gk_b, q_norm, k_norm, rope_tabs, seq_p, seq_s):
    n_p, n_s = xp.shape[0], xs.shape[0]
    tp, ts = n_p // TM, n_s // TM
    n = n_p + n_s
    seqs_per_tile = TM // seq_p
    tiles_per_seq = seq_s // TM
    n_seq_p = n_p // seq_p
    const = lambda shape: pl.BlockSpec(shape, lambda s: (0,) * len(shape))
    n_tiles = tp + ts
    out_tile = lambda s: jnp.clip(s - 1, 0, n_tiles - 1)
    gate_tile = lambda s: jnp.maximum(s - 2, 0)
    rope_spec = pl.BlockSpec((TM, LANES), lambda s: (jnp.maximum(out_tile(s) - tp, 0) % tiles_per_seq, 0))
    in_specs = [pl.BlockSpec((TM, D_MODEL), lambda s: (jnp.minimum(s, tp - 1), 0)),
                pl.BlockSpec((TM, D_MODEL), lambda s: (jnp.clip(s - tp, 0, ts - 1), 0)),
                const(mod.shape),
                const((1, D_MODEL)),
                pl.BlockSpec((D_IN, D_MODEL), lambda s: (0, 0), pipeline_mode=pl.Buffered(1)),
                const(w_gk_f.shape), const(w_gk_b.shape), const(b_gk_f.shape), const(b_gk_b.shape),
                const(q_norm.shape), const(k_norm.shape),
                rope_spec, rope_spec, rope_spec]
    row = lambda w: pl.BlockSpec((TM, w), lambda s: (out_tile(s), 0))
    cache_spec = pl.BlockSpec((seqs_per_tile, 1, SWA_KV_HEADS, HEAD_DIM, seq_p),
                              lambda s: (jnp.minimum(out_tile(s), tp - 1), 0, 0, 0, 0))
    widths = (4 * GLA_WIDTH, 2 * GLA_WIDTH, SWA_WIDTH + 2 * KV_WIDTH)
    out_specs = [row(widths[0]), pl.BlockSpec((TM, widths[1]), lambda s: (gate_tile(s), 0)), row(widths[2]),
                 cache_spec, cache_spec]
    cache_shape = (n_seq_p, 1, SWA_KV_HEADS, HEAD_DIM, seq_p)
    out_shape = ([jax.ShapeDtypeStruct((n, w), BF16) for w in widths]
                 + [jax.ShapeDtypeStruct(cache_shape, F32)] * 2)
    return pl.pallas_call(
        functools.partial(_proj_kernel, n_prompt_tiles=tp, n_tiles=n_tiles, tiles_per_seq=tiles_per_seq,
                          seqs_per_tile=seqs_per_tile, seq_len=seq_p),
        grid=(n_tiles + 2,), in_specs=in_specs, out_specs=out_specs, out_shape=out_shape,
        scratch_shapes=([pltpu.VMEM((D_MODEL, W_COLS), BF16), pltpu.VMEM((LANES, 2 * GLA_WIDTH), BF16)]
                        + [pltpu.VMEM((TM, D_MODEL), BF16)] * 2 + [pltpu.VMEM((TM, LANES), BF16)] * 2),
        compiler_params=_cparams(1), name="adaln_in_proj",
    )(xp, xs, mod, gain, w_in, w_gk_f, w_gk_b, b_gk_f, b_gk_b, q_norm, k_norm, *rope_tabs)


def _gla_kernel(*refs, seq_len, seqs, has_state, emit_state, with_attention, n_cast, with_modulation):
    (q_ref, k_ref, v_ref, gf_ref, gb_ref, sgg_ref, norm_ref,
     trif_ref, trib_ref, keepf_ref, keepb_ref) = refs[:11]
    pos = 11
    if has_state:
        s0f_ref, s0b_ref = refs[pos:pos + 2]
        pos += 2
    if with_attention:
        sink_ref, aq_ref, ak_ref, av_ref = refs[pos:pos + 4]
        pos += 4
    if with_attention == "latent":
        kc_ref, vc_ref = refs[pos:pos + 2]
        pos += 2
    cast_in = refs[pos:pos + n_cast]
    pos += n_cast
    if with_modulation:
        cctx_ref, c_ref, wada_ref, bada_ref = refs[pos:pos + 4]
        pos += 4
    og_ref = refs[pos]
    pos += 1
    if emit_state:
        sf_ref, sb_ref = refs[pos:pos + 2]
        pos += 2
    if with_attention:
        osw_ref = refs[pos]
        pos += 1
    cast_out = refs[pos:pos + n_cast]
    pos += n_cast
    if with_modulation:
        mod_ref = refs[pos]
        pos += 1
    accf_ref, accb_ref = refs[pos:pos + 2]

    for src, dst in zip(cast_in, cast_out):
        dst[...] = src[...].astype(BF16)
    if with_modulation:
        mod_ref[...] = _dot(_silu_cond(cctx_ref, c_ref), wada_ref[...].astype(BF16)) + bada_ref[...]

    n_groups = seq_len // GROUP
    per_group = GROUP // CHUNK
    n_pairs = q_ref.shape[1] // LANES
    lane = lax.broadcasted_iota(jnp.int32, (1, LANES), 1)
    head0 = lane < HEAD_DIM
    m0 = jnp.where(head0, 1.0, 0.0).astype(BF16)
    m1 = jnp.where(head0, 0.0, 1.0).astype(BF16)
    row_top = lax.broadcasted_iota(jnp.int32, (LANES, 1), 0) < HEAD_DIM

    def stack_heads(x):
        return jnp.concatenate([x * m0, x * m1], axis=0)

    def direction(row0, lanes, g_ref, tri_ref, keep_ref, state, acc_ref, backward):
        rows = pl.ds(row0, GROUP)
        b = _dot(tri_ref[...], g_ref[rows, lanes])
        yield
        b = b.reshape(per_group, CHUNK, LANES)
        edge = 0 if backward else CHUNK - 1
        mid = CHUNK // 2
        b_edge = b[:, edge:edge + 1, :]
        b_mid = b[:, mid:mid + 1, :]
        q = q_ref[rows, lanes].astype(F32).reshape(per_group, CHUNK, LANES)
        k = k_ref[rows, lanes].astype(F32).reshape(per_group, CHUNK, LANES)
        v = v_ref[rows, lanes]
        q_mid = q * jnp.exp2(b - b_mid)
        k_mid = k * jnp.exp2(b_mid - b)
        q_in = q_mid.astype(BF16)
        k_in = k_mid.astype(BF16).reshape(GROUP, LANES)
        q_st = (q_mid * jnp.exp2(b_mid)).astype(BF16)
        k_st = (k_mid * jnp.exp2(b_edge - b_mid)).astype(BF16).reshape(GROUP, LANES)
        decay = jnp.exp2(b_edge)
        keep = keep_ref[...]
        pair_rows = 2 * CHUNK
        n_cp = per_group // 2
        yield
        def pair_scores(p):
            c0 = 2 * p
            q_pair = jnp.concatenate([stack_heads(q_in[c0]), stack_heads(q_in[c0 + 1])], axis=0)
            return _dot_nt(q_pair, k_in[p * pair_rows:(p + 1) * pair_rows, :]).astype(BF16) * keep

        def pair_kv(p):
            prow = slice(p * pair_rows, (p + 1) * pair_rows)
            v_t = v[prow, :].astype(F32).T.astype(BF16)
            k_pair = k_st[prow, :]
            zero = jnp.zeros_like(k_pair)
            k_blk = jnp.concatenate([jnp.where(row_top, k_pair, zero), jnp.where(row_top, zero, k_pair)], axis=-1)
            return _dot(v_t, k_blk)

        if n_groups == 1:
            scores, kvs = zip(*[(pair_scores(p), pair_kv(p)) for p in range(n_cp)])
        else:
            scores = [pair_scores(p) for p in range(n_cp)]
            yield
            kvs = [pair_kv(p) for p in range(n_cp)]
        yield
        intras = [_dot(scores[p], v[p * pair_rows:(p + 1) * pair_rows, :]) for p in range(n_cp)]
        yield
        for p in (range(n_cp - 1, -1, -1) if backward else range(n_cp)):
            for j in ((1, 0) if backward else (0, 1)):
                c = 2 * p + j
                inter = _dot_nt(stack_heads(q_st[c]), state.astype(BF16))
                tot = intras[p][j * pair_rows:(j + 1) * pair_rows, :] + inter
                acc_ref[pl.ds(row0 + c * CHUNK, CHUNK), lanes] = jnp.where(head0, tot[:CHUNK, :], tot[CHUNK:, :])
                state = state * decay[c] + kvs[p][:, j * LANES:(j + 1) * LANES]
            yield
        return state

    def load_state(s_ref, q, p):
        z = jnp.zeros((HEAD_DIM, HEAD_DIM), F32)
        top = jnp.concatenate([s_ref[q, 0, 2 * p, :, :].T, z], axis=-1)
        bot = jnp.concatenate([z, s_ref[q, 0, 2 * p + 1, :, :].T], axis=-1)
        return jnp.concatenate([top, bot], axis=0)

    per_dir = seqs * n_pairs
    if has_state:
        init = tuple(load_state(s_ref, q, p)
                     for s_ref in (s0f_ref, s0b_ref) for q in range(seqs) for p in range(n_pairs))
    else:
        init = (jnp.zeros((LANES, LANES), F32),) * (2 * per_dir)

    def body(t, carry):
        row_f = t * GROUP
        row_b = (n_groups - 1 - t) * GROUP
        if n_groups > 1:
            row_f = pl.multiple_of(row_f, GROUP)
            row_b = pl.multiple_of(row_b, GROUP)
        units = []
        for backward, row, g_ref, tri_ref, keep_ref, acc_ref in (
                (False, row_f, gf_ref, trif_ref, keepf_ref, accf_ref),
                (True, row_b, gb_ref, trib_ref, keepb_ref, accb_ref)):
            for q in range(seqs):
                for p in range(n_pairs):
                    lanes = slice(p * LANES, (p + 1) * LANES)
                    state = carry[backward * per_dir + q * n_pairs + p]
                    units.append(direction(q * seq_len + row, lanes, g_ref, tri_ref, keep_ref, state, acc_ref,
                                           backward))
        return units

    norm_pair = jnp.concatenate([norm_ref[...]] * 2, axis=-1)

    def finish_outputs(rows):
        for p in range(n_pairs):
            lanes = slice(p * LANES, (p + 1) * LANES)
            o = _head_rms_norm(accf_ref[rows, lanes] + accb_ref[rows, lanes], norm_pair)
            og_ref[rows, lanes] = (o * sgg_ref[rows, lanes].astype(F32)).astype(BF16)
            yield

    if n_groups == 1:
        gla_units = body(0, init)
        final = [None] * len(gla_units)
        for q in range(seqs + 1):
            rows = slice(q * seq_len, (q + 1) * seq_len)
            wave, ids = [], []
            if q < seqs:
                ids = [d * per_dir + q * n_pairs + p for d in range(2) for p in range(n_pairs)]
                wave = [gla_units[u] for u in ids]
                if with_attention:
                    wave += _context_attention_units(sink_ref, aq_ref, ak_ref, av_ref, rows)
            if q > 0:
                wave.append(finish_outputs(slice((q - 1) * seq_len, q * seq_len)))
            results = _run_staged(wave)
            for u, r in zip(ids, results):
                final[u] = r
            if q < seqs and with_attention:
                heads = [o for outs in results[len(ids):len(ids) + SWA_KV_HEADS] for o in outs]
                osw_ref[rows, :] = jnp.concatenate(heads, axis=-1).astype(BF16)
    else:
        assert with_attention != "context" and seqs == 1
        if with_attention:
            ctx_kv = [(kc_ref[0, 0, g, :, :].astype(BF16), vc_ref[0, 0, g, :, :].astype(BF16))
                      for g in range(SWA_KV_HEADS)]
            q_blocks = GROUP // QBLOCK

        def loop_body(t, carry):
            units = body(t, carry)
            n_gla = len(units)
            if with_attention:
                for j in range(q_blocks):
                    units += _latent_attention_units(sink_ref, aq_ref, ak_ref, av_ref, ctx_kv,
                                                     t * q_blocks + j, seq_len)
            results = _run_staged(units)
            if with_attention:
                for j in range(q_blocks):
                    heads = [o for outs in results[n_gla + j * SWA_KV_HEADS:n_gla + (j + 1) * SWA_KV_HEADS]
                             for o in outs]
                    rows = pl.ds(pl.multiple_of((t * q_blocks + j) * QBLOCK, QBLOCK), QBLOCK)
                    osw_ref[rows, :] = jnp.concatenate(heads, axis=-1).astype(BF16)
            return tuple(results[:n_gla])

        final = lax.fori_loop(0, n_groups, loop_body, init)
        _run_staged([finish_outputs(slice(0, seq_len))])
    if emit_state:
        for d, s_ref in enumerate((sf_ref, sb_ref)):
            for q in range(seqs):
                for p in range(n_pairs):
                    fin = final[d * per_dir + q * n_pairs + p]
                    s_ref[q, 0, 2 * p, :, :] = fin[:HEAD_DIM, :HEAD_DIM].T
                    s_ref[q, 0, 2 * p + 1, :, :] = fin[HEAD_DIM:, HEAD_DIM:].T


def _gla_constants():
    i = np.arange(GROUP)[:, None]
    j = np.arange(GROUP)[None, :]
    same = (i // CHUNK) == (j // CHUNK)
    tri_f = same & (j <= i)
    tri_b = same & (j >= i)
    r = np.arange(4 * CHUNK)[:, None]
    l = np.arange(LANES)[None, :]
    same = (r // (2 * CHUNK)) == (l // CHUNK)
    keep_f = same & ((l % CHUNK) <= (r % CHUNK))
    keep_b = same & ((l % CHUNK) >= (r % CHUNK))
    return tuple(jnp.asarray(m.astype(np.float32), dtype=BF16) for m in (tri_f, tri_b, keep_f, keep_b))


def _gla(gla_in, gates, norm_row, s0f, s0b, n_seq, seq_len, row_block0, emit_state, seqs=1,
         attention=None, cast_weights=(), late_modulation=None):
    has_state = s0f is not None
    steps = n_seq // seqs
    rows_per_step = seqs * seq_len
    row_blk = row_block0 // seqs
    col = lambda width, j: pl.BlockSpec((rows_per_step, width), lambda b: (row_blk + b, j))
    out_blk = pl.BlockSpec((rows_per_step, GLA_WIDTH), lambda b: (b, 0))
    st_spec = pl.BlockSpec((seqs, 1, GLA_HEADS, HEAD_DIM, HEAD_DIM), lambda b: (b, 0, 0, 0, 0))
    const = lambda shape: pl.BlockSpec(shape, lambda b: (0, 0))
    in_specs = ([col(GLA_WIDTH, 0), col(GLA_WIDTH, 1), col(GLA_WIDTH, 2), col(GLA_WIDTH, 0), col(GLA_WIDTH, 1),
                 col(GLA_WIDTH, 3)]
                + [const(norm_row.shape)] + [const((GROUP, GROUP))] * 2 + [const((4 * CHUNK, LANES))] * 2)
    args = [gla_in, gla_in, gla_in, gates, gates, gla_in, norm_row, *_gla_constants()]
    if has_state:
        in_specs += [st_spec] * 2
        args += [s0f, s0b]
    out_specs = [out_blk]
    out_shape = [jax.ShapeDtypeStruct((n_seq * seq_len, GLA_WIDTH), BF16)]
    if emit_state:
        out_specs += [st_spec] * 2
        out_shape += [jax.ShapeDtypeStruct((n_seq, 1, GLA_HEADS, HEAD_DIM, HEAD_DIM), F32)] * 2
    with_attention = None
    if attention is not None:
        sink, swa_in = attention[:2]
        with_attention = "context"
        in_specs += [pl.BlockSpec(memory_space=pltpu.SMEM), col(SWA_WIDTH, 0),
                     col(KV_WIDTH, SWA_WIDTH // KV_WIDTH), col(KV_WIDTH, SWA_WIDTH // KV_WIDTH + 1)]
        args += [sink, swa_in, swa_in, swa_in]
        if len(attention) > 2:
            with_attention = "latent"
            cache_k_t, cache_v_t = attention[2:]
            past = cache_k_t.shape[4]
            cache_spec = pl.BlockSpec((seqs, 1, SWA_KV_HEADS, HEAD_DIM, past), lambda b: (b, 0, 0, 0, 0))
            in_specs += [cache_spec] * 2
            args += [cache_k_t, cache_v_t]
        out_specs.append(pl.BlockSpec((rows_per_step, SWA_WIDTH), lambda b: (b, 0)))
        out_shape.append(jax.ShapeDtypeStruct((n_seq * seq_len, SWA_WIDTH), BF16))
    cast_specs, cast_shapes = _cast_specs(cast_weights, steps)
    in_specs += cast_specs
    args += list(cast_weights)
    out_specs += cast_specs
    out_shape += cast_shapes
    if late_modulation is not None:
        c_ctx, c, w_ada, b_ada, col0 = late_modulation
        width = (w_ada.shape[1] - col0) // steps
        blk0 = col0 // width
        in_specs += [const(c_ctx.shape), const(c.shape),
                     pl.BlockSpec((w_ada.shape[0], width), lambda b: (0, blk0 + b)),
                     pl.BlockSpec((1, width), lambda b: (0, blk0 + b))]
        args += [c_ctx, c, w_ada, b_ada]
        out_specs.append(pl.BlockSpec((8, width), lambda b: (0, b)))
        out_shape.append(jax.ShapeDtypeStruct((8, width * steps), F32))
    return pl.pallas_call(
        functools.partial(_gla_kernel, seq_len=seq_len, seqs=seqs, has_state=has_state, emit_state=emit_state,
                          with_attention=with_attention, n_cast=len(cast_weights),
                          with_modulation=late_modulation is not None),
        grid=(steps,), in_specs=in_specs, out_specs=out_specs, out_shape=out_shape,
        scratch_shapes=[pltpu.VMEM((rows_per_step, GLA_WIDTH), F32)] * 2,
        compiler_params=_cparams(1), name="gla_bidirectional",
    )(*args)


def _group_attention(q, heads, sink_ref, score_fn, value_fn):
    rows = q.shape[0]
    q_stack = jnp.concatenate([q[:, h * HEAD_DIM:(h + 1) * HEAD_DIM] for h in heads], axis=0)
    scores = score_fn(q_stack)
    yield
    slabs = [slice(j * rows, (j + 1) * rows) for j in range(len(heads))]
    ms = []
    sinks = [sink_ref[h] * LOG2E for h in heads]
    for slab, sink in zip(slabs, sinks):
        m = sink
        for s in scores:
            m = jnp.maximum(m, jnp.max(s[slab, :], axis=-1, keepdims=True))
        ms.append(m)
    yield
    e_parts = [[] for _ in scores]
    denoms = []
    for slab, sink, m in zip(slabs, sinks, ms):
        denom = jnp.exp2(sink - m)
        for part, s in zip(e_parts, scores):
            e = jnp.exp2(s[slab, :] - m)
            denom = denom + jnp.sum(e, axis=-1, keepdims=True)
            part.append(e.astype(BF16))
        denoms.append(denom)
    yield
    acc = value_fn([jnp.concatenate(part, axis=0) for part in e_parts])
    yield
    return [acc[slab, :] / denom for slab, denom in zip(slabs, denoms)]


def _run_staged(units):
    results = [None] * len(units)
    live = list(range(len(units)))
    while live:
        still = []
        for u in live:
            try:
                next(units[u])
                still.append(u)
            except StopIteration as stop:
                results[u] = stop.value
        live = still
    return results


def _context_attention_units(sink_ref, q_ref, k_ref, v_ref, rows):
    q = q_ref[rows, :]
    k = k_ref[rows, :]
    v = v_ref[rows, :]
    units = []
    for g in range(SWA_KV_HEADS):
        k_g = k[:, g * HEAD_DIM:(g + 1) * HEAD_DIM]
        v_g = v[:, g * HEAD_DIM:(g + 1) * HEAD_DIM]
        units.append(_group_attention(
            q, list(range(SWA_GROUP * g, SWA_GROUP * (g + 1))), sink_ref,
            lambda qs, k_g=k_g: [_dot_nt(qs, k_g)],
            lambda es, v_g=v_g: _dot(es[0], v_g)))
    return units


def _cast_specs(cast_weights, steps, step_of=lambda *idx: idx[0]):
    specs = [pl.BlockSpec((w.shape[0] // steps, w.shape[1]), lambda *idx: (step_of(*idx), 0))
             for w in cast_weights]
    shapes = [jax.ShapeDtypeStruct(w.shape, BF16) for w in cast_weights]
    return specs, shapes


def _latent_attention_units(sink_ref, q_ref, k_ref, v_ref, ctx_kv, i, seq_len):
    span = QBLOCK + 2 * WINDOW
    start = pl.multiple_of(jnp.clip(i * QBLOCK - WINDOW, 0, seq_len - span), QBLOCK)
    q = q_ref[pl.ds(pl.multiple_of(i * QBLOCK, QBLOCK), QBLOCK), :]
    k = k_ref[pl.ds(start, span), :]
    v = v_ref[pl.ds(start, span), :]
    qpos = i * QBLOCK + lax.broadcasted_iota(jnp.int32, (QBLOCK, span), 0)
    kpos = start + lax.broadcasted_iota(jnp.int32, (QBLOCK, span), 1)
    valid = jnp.abs(qpos - kpos) <= WINDOW
    valid = jnp.concatenate([valid] * SWA_GROUP, axis=0)
    units = []
    for g in range(SWA_KV_HEADS):
        k_g = k[:, g * HEAD_DIM:(g + 1) * HEAD_DIM]
        v_g = v[:, g * HEAD_DIM:(g + 1) * HEAD_DIM]
        kc_t, vc_t = ctx_kv[g]
        units.append(_group_attention(
            q, list(range(SWA_GROUP * g, SWA_GROUP * (g + 1))), sink_ref,
            lambda qs, k_g=k_g, kc_t=kc_t: [jnp.where(valid, _dot_nt(qs, k_g), NEG_INF), _dot(qs, kc_t)],
            lambda es, v_g=v_g, vc_t=vc_t: _dot(es[0], v_g) + _dot_nt(es[1], vc_t)))
    return units


def _out_kernel(xp_ref, xs_ref, ogp_ref, ogs_ref, oswp_ref, osws_ref, mod_ref, gain_ref,
                wo_ref, w1_ref, w2_ref, yp_ref, ys_ref, *, n_prompt_tiles, tiles_per_seq):
    i = pl.program_id(0)
    row = _mod_row(i, n_prompt_tiles, tiles_per_seq)

    def body(x_ref, og_ref, osw_ref, y_ref):
        mix = _dot(og_ref[...], wo_ref[:GLA_WIDTH, :]) + _dot(osw_ref[...], wo_ref[GLA_WIDTH:, :])
        x1 = x_ref[...] + _mod_chunk(mod_ref, row, 0) * mix
        h = _adaln(x1, gain_ref[...], _mod_chunk(mod_ref, row, 1), _mod_chunk(mod_ref, row, 2)).astype(BF16)
        acc = None
        for j in range(D_FF // FF_TILE):
            cols = slice(j * FF_TILE, (j + 1) * FF_TILE)
            a = jnp.maximum(_dot(h, w1_ref[:, cols]), 0.0)
            part = _dot((a * a).astype(BF16), w2_ref[cols, :])
            acc = part if acc is None else acc + part
        y_ref[...] = x1 + _mod_chunk(mod_ref, row, 3) * acc

    @pl.when(i < n_prompt_tiles)
    def _():
        body(xp_ref, ogp_ref, oswp_ref, yp_ref)

    @pl.when(i >= n_prompt_tiles)
    def _():
        body(xs_ref, ogs_ref, osws_ref, ys_ref)


def _mix_out(xp, xs, og_p, og_s, osw_p, osw_s, mod, gain, wo, w1, w2, seq_s):
    n_p, n_s = xp.shape[0], xs.shape[0]
    tp, ts = n_p // TM, n_s // TM
    tiles_per_seq = seq_s // TM
    const = lambda shape: pl.BlockSpec(shape, lambda i: (0,) * len(shape), pipeline_mode=pl.Buffered(1))
    row_p = lambda w: pl.BlockSpec((TM, w), lambda i: (jnp.minimum(i, tp - 1), 0))
    row_s = lambda w: pl.BlockSpec((TM, w), lambda i: (jnp.maximum(i - tp, 0), 0))
    return pl.pallas_call(
        functools.partial(_out_kernel, n_prompt_tiles=tp, tiles_per_seq=tiles_per_seq),
        grid=(tp + ts,),
        in_specs=[row_p(D_MODEL), row_s(D_MODEL), row_p(GLA_WIDTH), row_s(GLA_WIDTH),
                  row_p(SWA_WIDTH), row_s(SWA_WIDTH),
                  const(mod.shape),
                  const((1, D_MODEL)), const((GLA_WIDTH + SWA_WIDTH, D_MODEL)),
                  const((D_MODEL, D_FF)), const((D_FF, D_MODEL))],
        out_specs=[row_p(D_MODEL), row_s(D_MODEL)],
        out_shape=[jax.ShapeDtypeStruct((n_p, D_MODEL), F32), jax.ShapeDtypeStruct((n_s, D_MODEL), F32)],
        compiler_params=_cparams(1), name="out_proj_mlp",
    )(xp, xs, og_p, og_s, osw_p, osw_s, mod, gain, wo, w1, w2)


def _rope_tables(seq_len):
    axis_dim = HEAD_DIM // 2
    half = axis_dim // 2
    t = np.arange(seq_len)
    rowp = (t // GRID_W).astype(np.float64)
    colp = (t % GRID_W).astype(np.float64)
    freqs = ROPE_THETA ** (-np.arange(half, dtype=np.float64) / half)
    lane = np.arange(LANES)
    d = lane % HEAD_DIM
    posv = np.where((d // axis_dim)[None, :] == 0, rowp[:, None], colp[:, None])
    ang = posv * freqs[d % half][None, :]
    second = ((d % axis_dim) >= half)[None, :]
    cos, sin = np.cos(ang), np.sin(ang)
    zero = np.zeros_like(sin)
    tabs = (cos, np.where(second, sin, zero), np.where(second, zero, -sin))
    return tuple(jnp.asarray(tab.astype(np.float32)) for tab in tabs)


def kernel(x_prompt, x_sample, c, cache_k, cache_v, state_gla_fwd, state_gla_bwd, c_ctx, w_ada, b_ada, norm_mix, norm_ff, w_in, w_gk_fwd, b_gk_fwd, w_gk_bwd, b_gk_bwd, gla_norm, q_norm, k_norm, sink, w_o, w_ff1, w_ff2):
    depth = w_in.shape[0]
    assert depth == 1, "single trunk layer"
    bp, tp, _ = x_prompt.shape
    bs, ts, _ = x_sample.shape
    n_p = bp * tp
    l = 0

    n_mod_in = 2 * D_MODEL
    b_ada_row = b_ada[l][None, :]
    mod_in = _modulation(c_ctx[None, :], c, w_ada[l], b_ada_row, n_mod_in)

    norm2 = gla_norm[l][None, :]
    gain_mix = norm_mix[l][None, :]
    gain_ff = norm_ff[l][None, :]
    sink_l = sink[l]

    w_in_t = jnp.transpose(w_in[l])
    cache_k_t = jnp.swapaxes(cache_k, 3, 4)
    cache_v_t = jnp.swapaxes(cache_v, 3, 4)

    xp = x_prompt.reshape(n_p, D_MODEL)
    xs = x_sample.reshape(bs * ts, D_MODEL)
    gla_in, gates, swa_in, new_k_t, new_v_t = _project(
        xp, xs, mod_in, gain_mix, w_in_t, w_gk_fwd[l], w_gk_bwd[l], b_gk_fwd[l][None, :], b_gk_bwd[l][None, :],
        q_norm[l][None, :], k_norm[l][None, :], _rope_tables(ts), tp, ts)
    new_k = jnp.swapaxes(new_k_t, 3, 4)
    new_v = jnp.swapaxes(new_v_t, 3, 4)

    og_p, s_f, s_b, osw_p, wo, w1, w2 = _gla(
        gla_in, gates, norm2, None, None, bp, tp, 0, True, seqs=2,
        attention=(sink_l, swa_in), cast_weights=(w_o[l], w_ff1[l], w_ff2[l]))
    og_s, osw_s, mod_out = _gla(gla_in, gates, norm2, state_gla_fwd, state_gla_bwd, bs, ts, n_p // ts, False,
                                attention=(sink_l, swa_in, cache_k_t, cache_v_t),
                                late_modulation=(c_ctx[None, :], c, w_ada[l], b_ada_row, n_mod_in))

    yp, ys = _mix_out(xp, xs, og_p, og_s, osw_p, osw_s, mod_out, gain_ff, wo, w1, w2, ts)
    return (yp.reshape(bp, tp, D_MODEL), ys.reshape(bs, ts, D_MODEL), new_k, new_v, s_f, s_b)
```

```python
import functools

import numpy as np
import jax
import jax.numpy as jnp
from jax import lax
from jax.experimental import pallas as pl
from jax.experimental.pallas import tpu as pltpu

F32 = jnp.float32
BF16 = jnp.bfloat16

D_MODEL = 1024
GRID_W = 64
HEAD_DIM = 64
GLA_HEADS = 8
GLA_WIDTH = GLA_HEADS * HEAD_DIM
GATE_RANK = 16
GATE_TEMP = 16.0
CHUNK = 64
GROUP = 4 * CHUNK
SWA_HEADS = 8
SWA_KV_HEADS = 2
SWA_GROUP = SWA_HEADS // SWA_KV_HEADS
SWA_WIDTH = SWA_HEADS * HEAD_DIM
KV_WIDTH = SWA_KV_HEADS * HEAD_DIM
WINDOW = 128
QBLOCK = 128
D_FF = 4 * D_MODEL
ROPE_THETA = 10000.0
EPS = 1e-6
NEG_INF = -1e30
LOG2E = 1.4426950408889634
LANES = 128
TM = 512
FF_TILE = 1024

D_IN = 2848
R_GQ, R_GK, R_GV, R_GG, R_LR = 0, 512, 1024, 1536, 2048
R_SQ = R_LR + 2 * GATE_RANK
R_SV = R_SQ + SWA_WIDTH + KV_WIDTH
C_TAIL = R_LR
C_LR = C_TAIL + D_IN - R_SQ
W_COLS = C_LR + LANES

V7X_VMEM_BYTES = 64 * 1024 * 1024
VMEM_LIMIT = V7X_VMEM_BYTES - 8 * 1024 * 1024


def _cparams(n_axes):
    return pltpu.CompilerParams(dimension_semantics=("arbitrary",) * n_axes, vmem_limit_bytes=VMEM_LIMIT)


def _silu_cond(cctx_ref, c_ref):
    pad = jnp.zeros((8 - 1 - c_ref.shape[0], c_ref.shape[1]), F32)
    cnd = jnp.concatenate([cctx_ref[...], c_ref[...], pad], axis=0)
    return (cnd * jax.nn.sigmoid(cnd)).astype(BF16)


def _mod_kernel(cctx_ref, c_ref, w_ref, b_ref, o_ref):
    k = pl.program_id(0)
    part = _dot(_silu_cond(cctx_ref, c_ref), w_ref[...].astype(BF16))

    @pl.when(k == 0)
    def _():
        o_ref[...] = part + b_ref[...]

    @pl.when(k > 0)
    def _():
        o_ref[...] += part


def _modulation(c_ctx, c, w_ada, b_ada, n_out):
    tk = 256
    return pl.pallas_call(
        _mod_kernel,
        grid=(D_MODEL // tk,),
        in_specs=[pl.BlockSpec((1, tk), lambda k: (0, k)),
                  pl.BlockSpec((c.shape[0], tk), lambda k: (0, k)),
                  pl.BlockSpec((tk, n_out), lambda k: (k, 0)),
                  pl.BlockSpec((1, n_out), lambda k: (0, 0))],
        out_specs=pl.BlockSpec((8, n_out), lambda k: (0, 0)),
        out_shape=jax.ShapeDtypeStruct((8, n_out), F32),
        compiler_params=_cparams(1),
        name="modulation",
    )(c_ctx, c, w_ada, b_ada.reshape(1, -1))


def _mod_row(tile, n_prompt_tiles, tiles_per_seq):
    return jnp.where(tile < n_prompt_tiles, 0, 1 + (tile - n_prompt_tiles) // tiles_per_seq)


def _mod_chunk(mod_ref, row, j):
    return mod_ref[pl.ds(row, 1), j * D_MODEL:(j + 1) * D_MODEL]


def _adaln(x, gain, shift, scale):
    ms = jnp.mean(x * x, axis=-1, keepdims=True)
    xn = x * lax.rsqrt(ms + EPS)
    return xn * (gain * (1.0 + scale)) + shift


def _head_rms_norm(y, gain):
    cols = y.shape[1]
    lane = lax.broadcasted_iota(jnp.int32, (1, LANES), 1)
    first = lane < HEAD_DIM
    outs = []
    for p in range(cols // LANES):
        s = y[:, p * LANES:(p + 1) * LANES]
        s = s * s
        tot = jnp.sum(s, axis=-1, keepdims=True)
        lo = jnp.sum(jnp.where(first, s, 0.0), axis=-1, keepdims=True)
        outs.append(jnp.where(first, lo, tot - lo))
    sums = outs[0] if len(outs) == 1 else jnp.concatenate(outs, axis=-1)
    return y * lax.rsqrt(sums + HEAD_DIM * EPS) * (gain * (HEAD_DIM ** 0.5))


def _log_sigmoid(x):
    return jnp.minimum(x, 0.0) - jnp.log(1.0 + jnp.exp(-jnp.abs(x)))


def _dot(a, b):
    return jnp.dot(a, b, preferred_element_type=F32)


def _dot_nt(a, b):
    return lax.dot_general(a, b, (((1,), (1,)), ((), ())), preferred_element_type=F32)


def _proj_kernel(xp_ref, xs_ref, mod_ref, gain_ref, w_ref, wgf_ref, wgb_ref, bgf_ref, bgb_ref, qn_ref, kn_ref,
                 cos_ref, sin_a_ref, sin_b_ref,
                 gla_ref, gate_ref, swa_ref, nk_ref, nv_ref,
                 wb_ref, wg_ref, h0_ref, h1_ref, lr0_ref, lr1_ref,
                 *, n_prompt_tiles, n_tiles, tiles_per_seq, seqs_per_tile, seq_len):
    h_refs = (h0_ref, h1_ref)
    lr_refs = (lr0_ref, lr1_ref)
    s = pl.program_id(0)

    @pl.when(s == 0)
    def _():
        moves = ([(c * LANES, c * LANES) for c in range(R_LR // LANES)]
                 + [(R_SQ + c * LANES, C_TAIL + c * LANES) for c in range((D_IN - R_SQ) // LANES)]
                 + [(R_LR, C_LR)])
        for src, dst in moves:
            wb_ref[:, dst:dst + LANES] = w_ref[src:src + LANES, :].T.astype(BF16)
        wg_ref[...] = jnp.zeros(wg_ref.shape, BF16)
        wg_ref[0:GATE_RANK, 0:GLA_WIDTH] = wgf_ref[...].astype(BF16)
        wg_ref[GATE_RANK:2 * GATE_RANK, GLA_WIDTH:] = wgb_ref[...].astype(BF16)

    def normalise_next(parity):
        x = jnp.where(s < n_prompt_tiles, xp_ref[...], xs_ref[...])
        row = _mod_row(jnp.minimum(s, n_tiles - 1), n_prompt_tiles, tiles_per_seq)
        shift, scale = _mod_chunk(mod_ref, row, 0), _mod_chunk(mod_ref, row, 1)
        h_refs[parity][...] = _adaln(x, gain_ref[...], shift, scale).astype(BF16)

    def gates(parity):
        gate_pre = _dot(lr_refs[parity][...], wg_ref[...])
        scale = LOG2E / GATE_TEMP
        gate_ref[:, :GLA_WIDTH] = (_log_sigmoid(gate_pre[:, :GLA_WIDTH] + bgf_ref[...]) * scale).astype(BF16)
        gate_ref[:, GLA_WIDTH:] = (_log_sigmoid(gate_pre[:, GLA_WIDTH:] + bgb_ref[...]) * scale).astype(BF16)

    def body(rope, emit_cache, parity):
        h_ref = h_refs[1 - parity]
        gates(parity)

        def proj(c0, c1):
            return _dot(h_ref[...], wb_ref[:, c0:c1])

        tail = proj(C_TAIL, C_LR)
        lr_refs[1 - parity][...] = proj(C_LR, W_COLS).astype(BF16)
        gg = proj(R_GG, R_LR)
        gla_ref[:, 3 * GLA_WIDTH:] = (gg * jax.nn.sigmoid(gg)).astype(BF16)
        attention_inputs(tail, rope, emit_cache)
        gla_ref[:, :GLA_WIDTH] = (proj(R_GQ, R_GK) * (HEAD_DIM ** -0.5)).astype(BF16)
        gla_ref[:, GLA_WIDTH:2 * GLA_WIDTH] = proj(R_GK, R_GV).astype(BF16)
        gla_ref[:, 2 * GLA_WIDTH:3 * GLA_WIDTH] = proj(R_GV, R_GG).astype(BF16)
        normalise_next(parity)

    def attention_inputs(tail, rope, emit_cache):
        qk = tail[:, :R_SV - R_SQ]
        qk_gain = jnp.concatenate([qn_ref[...]] * SWA_HEADS + [kn_ref[...]] * SWA_KV_HEADS, axis=-1)
        qk = _head_rms_norm(qk, qk_gain)
        if rope:
            reps = (SWA_WIDTH + KV_WIDTH) // LANES
            cos = jnp.concatenate([cos_ref[...]] * reps, axis=-1)
            sin_a = jnp.concatenate([sin_a_ref[...]] * reps, axis=-1)
            sin_b = jnp.concatenate([sin_b_ref[...]] * reps, axis=-1)
            width = qk.shape[1]
            quarter = HEAD_DIM // 4
            qk = (qk * cos + pltpu.roll(qk, quarter, 1) * sin_a
                  + pltpu.roll(qk, width - quarter, 1) * sin_b)
        swa_ref[:, :SWA_WIDTH] = (qk[:, :SWA_WIDTH] * (HEAD_DIM ** -0.5 * LOG2E)).astype(BF16)
        sk = qk[:, SWA_WIDTH:]
        swa_ref[:, SWA_WIDTH:SWA_WIDTH + KV_WIDTH] = sk.astype(BF16)
        sv = tail[:, R_SV - R_SQ:]
        swa_ref[:, SWA_WIDTH + KV_WIDTH:] = sv.astype(BF16)
        if emit_cache:
            sk_t, sv_t = sk.T, sv.T
            for q in range(seqs_per_tile):
                cols = slice(q * seq_len, (q + 1) * seq_len)
                for g in range(SWA_KV_HEADS):
                    feat = slice(g * HEAD_DIM, (g + 1) * HEAD_DIM)
                    nk_ref[q, 0, g, :, :] = sk_t[feat, cols]
                    nv_ref[q, 0, g, :, :] = sv_t[feat, cols]

    @pl.when(s == 0)
    def _():
        normalise_next(0)
        lr_refs[1][...] = jnp.zeros(lr_refs[1].shape, BF16)

    for parity in range(2):
        on_parity = (s % 2) == parity

        @pl.when((s >= 1) & (s <= n_prompt_tiles) & on_parity)
        def _():
            body(rope=False, emit_cache=True, parity=parity)

        @pl.when((s > n_prompt_tiles) & (s <= n_tiles) & on_parity)
        def _():
            body(rope=True, emit_cache=False, parity=parity)

    @pl.when(s == n_tiles + 1)
    def _():
        gates((n_tiles + 1) % 2)


def _project(xp, xs, mod, gain, w_in, w_gk_f, w_gk_b, b_gk_f, b_gk_b, q_norm, k_norm, rope_tabs, seq_p, seq_s):
    n_p, n_s = xp.shape[0], xs.shape[0]
    tp, ts = n_p // TM, n_s // TM
    n = n_p + n_s
    seqs_per_tile = TM // seq_p
    tiles_per_seq = seq_s // TM
    n_seq_p = n_p // seq_p
    const = lambda shape: pl.BlockSpec(shape, lambda s: (0,) * len(shape))
    n_tiles = tp + ts
    out_tile = lambda s: jnp.clip(s - 1, 0, n_tiles - 1)
    gate_tile = lambda s: jnp.maximum(s - 2, 0)
    rope_spec = pl.BlockSpec((TM, LANES), lambda s: (jnp.maximum(out_tile(s) - tp, 0) % tiles_per_seq, 0))
    in_specs = [pl.BlockSpec((TM, D_MODEL), lambda s: (jnp.minimum(s, tp - 1), 0)),
                pl.BlockSpec((TM, D_MODEL), lambda s: (jnp.clip(s - tp, 0, ts - 1), 0)),
                const(mod.shape),
                const((1, D_MODEL)),
                pl.BlockSpec((D_IN, D_MODEL), lambda s: (0, 0), pipeline_mode=pl.Buffered(1)),
                const(w_gk_f.shape), const(w_gk_b.shape), const(b_gk_f.shape), const(b_gk_b.shape),
                const(q_norm.shape), const(k_norm.shape),
                rope_spec, rope_spec, rope_spec]
    row = lambda w: pl.BlockSpec((TM, w), lambda s: (out_tile(s), 0))
    cache_spec = pl.BlockSpec((seqs_per_tile, 1, SWA_KV_HEADS, HEAD_DIM, seq_p),
                              lambda s: (jnp.minimum(out_tile(s), tp - 1), 0, 0, 0, 0))
    widths = (4 * GLA_WIDTH, 2 * GLA_WIDTH, SWA_WIDTH + 2 * KV_WIDTH)
    out_specs = [row(widths[0]), pl.BlockSpec((TM, widths[1]), lambda s: (gate_tile(s), 0)), row(widths[2]),
                 cache_spec, cache_spec]
    cache_shape = (n_seq_p, 1, SWA_KV_HEADS, HEAD_DIM, seq_p)
    out_shape = ([jax.ShapeDtypeStruct((n, w), BF16) for w in widths]
                 + [jax.ShapeDtypeStruct(cache_shape, F32)] * 2)
    return pl.pallas_call(
        functools.partial(_proj_kernel, n_prompt_tiles=tp, n_tiles=n_tiles, tiles_per_seq=tiles_per_seq,
                          seqs_per_tile=seqs_per_tile, seq_len=seq_p),
        grid=(n_tiles + 2,), in_specs=in_specs, out_specs=out_specs, out_shape=out_shape,
        scratch_shapes=([pltpu.VMEM((D_MODEL, W_COLS), BF16), pltpu.VMEM((LANES, 2 * GLA_WIDTH), BF16)]
                        + [pltpu.VMEM((TM, D_MODEL), BF16)] * 2 + [pltpu.VMEM((TM, LANES), BF16)] * 2),
        compiler_params=_cparams(1), name="adaln_in_proj",
    )(xp, xs, mod, gain, w_in, w_gk_f, w_gk_b, b_gk_f, b_gk_b, q_norm, k_norm, *rope_tabs)


def _gla_kernel(*refs, seq_len, seqs, has_state, emit_state, with_attention, n_cast, with_modulation):
    (q_ref, k_ref, v_ref, gf_ref, gb_ref, sgg_ref, norm_ref,
     trif_ref, trib_ref, keepf_ref, keepb_ref) = refs[:11]
    pos = 11
    if has_state:
        s0f_ref, s0b_ref = refs[pos:pos + 2]
        pos += 2
    if with_attention:
        sink_ref, aq_ref, ak_ref, av_ref = refs[pos:pos + 4]
        pos += 4
    if with_attention == "latent":
        kc_ref, vc_ref = refs[pos:pos + 2]
        pos += 2
    cast_in = refs[pos:pos + n_cast]
    pos += n_cast
    if with_modulation:
        cctx_ref, c_ref, wada_ref, bada_ref = refs[pos:pos + 4]
        pos += 4
    og_ref = refs[pos]
    pos += 1
    if emit_state:
        sf_ref, sb_ref = refs[pos:pos + 2]
        pos += 2
    if with_attention:
        osw_ref = refs[pos]
        pos += 1
    cast_out = refs[pos:pos + n_cast]
    pos += n_cast
    if with_modulation:
        mod_ref = refs[pos]
        pos += 1
    accf_ref, accb_ref = refs[pos:pos + 2]

    for src, dst in zip(cast_in, cast_out):
        dst[...] = src[...].astype(BF16)
    if with_modulation:
        mod_ref[...] = _dot(_silu_cond(cctx_ref, c_ref), wada_ref[...].astype(BF16)) + bada_ref[...]

    n_groups = seq_len // GROUP
    per_group = GROUP // CHUNK
    n_pairs = q_ref.shape[1] // LANES
    lane = lax.broadcasted_iota(jnp.int32, (1, LANES), 1)
    head0 = lane < HEAD_DIM
    m0 = jnp.where(head0, 1.0, 0.0).astype(BF16)
    m1 = jnp.where(head0, 0.0, 1.0).astype(BF16)
    row_top = lax.broadcasted_iota(jnp.int32, (LANES, 1), 0) < HEAD_DIM

    def stack_heads(x):
        return jnp.concatenate([x * m0, x * m1], axis=0)

    def direction(row0, lanes, g_ref, tri_ref, keep_ref, state, acc_ref, backward):
        rows = pl.ds(row0, GROUP)
        b = _dot(tri_ref[...], g_ref[rows, lanes])
        yield
        b = b.reshape(per_group, CHUNK, LANES)
        edge = 0 if backward else CHUNK - 1
        mid = CHUNK // 2
        b_edge = b[:, edge:edge + 1, :]
        b_mid = b[:, mid:mid + 1, :]
        q = q_ref[rows, lanes].astype(F32).reshape(per_group, CHUNK, LANES)
        k = k_ref[rows, lanes].astype(F32).reshape(per_group, CHUNK, LANES)
        v = v_ref[rows, lanes]
        q_mid = q * jnp.exp2(b - b_mid)
        k_mid = k * jnp.exp2(b_mid - b)
        q_in = q_mid.astype(BF16)
        k_in = k_mid.astype(BF16).reshape(GROUP, LANES)
        q_st = (q_mid * jnp.exp2(b_mid)).astype(BF16)
        k_st = (k_mid * jnp.exp2(b_edge - b_mid)).astype(BF16).reshape(GROUP, LANES)
        decay = jnp.exp2(b_edge)
        keep = keep_ref[...]
        pair_rows = 2 * CHUNK
        n_cp = per_group // 2
        yield
        def pair_scores(p):
            c0 = 2 * p
            q_pair = jnp.concatenate([stack_heads(q_in[c0]), stack_heads(q_in[c0 + 1])], axis=0)
            return _dot_nt(q_pair, k_in[p * pair_rows:(p + 1) * pair_rows, :]).astype(BF16) * keep

        def pair_kv(p):
            prow = slice(p * pair_rows, (p + 1) * pair_rows)
            v_t = v[prow, :].astype(F32).T.astype(BF16)
            k_pair = k_st[prow, :]
            zero = jnp.zeros_like(k_pair)
            k_blk = jnp.concatenate([jnp.where(row_top, k_pair, zero), jnp.where(row_top, zero, k_pair)], axis=-1)
            return _dot(v_t, k_blk)

        if n_groups == 1:
            scores, kvs = zip(*[(pair_scores(p), pair_kv(p)) for p in range(n_cp)])
        else:
            scores = [pair_scores(p) for p in range(n_cp)]
            yield
            kvs = [pair_kv(p) for p in range(n_cp)]
        yield
        intras = [_dot(scores[p], v[p * pair_rows:(p + 1) * pair_rows, :]) for p in range(n_cp)]
        yield
        for p in (range(n_cp - 1, -1, -1) if backward else range(n_cp)):
            for j in ((1, 0) if backward else (0, 1)):
                c = 2 * p + j
                inter = _dot_nt(stack_heads(q_st[c]), state.astype(BF16))
                tot = intras[p][j * pair_rows:(j + 1) * pair_rows, :] + inter
                acc_ref[pl.ds(row0 + c * CHUNK, CHUNK), lanes] = jnp.where(head0, tot[:CHUNK, :], tot[CHUNK:, :])
                state = state * decay[c] + kvs[p][:, j * LANES:(j + 1) * LANES]
            yield
        return state

    def load_state(s_ref, q, p):
        z = jnp.zeros((HEAD_DIM, HEAD_DIM), F32)
        top = jnp.concatenate([s_ref[q, 0, 2 * p, :, :].T, z], axis=-1)
        bot = jnp.concatenate([z, s_ref[q, 0, 2 * p + 1, :, :].T], axis=-1)
        return jnp.concatenate([top, bot], axis=0)

    per_dir = seqs * n_pairs
    if has_state:
        init = tuple(load_state(s_ref, q, p)
                     for s_ref in (s0f_ref, s0b_ref) for q in range(seqs) for p in range(n_pairs))
    else:
        init = (jnp.zeros((LANES, LANES), F32),) * (2 * per_dir)

    def body(t, carry):
        row_f = t * GROUP
        row_b = (n_groups - 1 - t) * GROUP
        if n_groups > 1:
            row_f = pl.multiple_of(row_f, GROUP)
            row_b = pl.multiple_of(row_b, GROUP)
        units = []
        for backward, row, g_ref, tri_ref, keep_ref, acc_ref in (
                (False, row_f, gf_ref, trif_ref, keepf_ref, accf_ref),
                (True, row_b, gb_ref, trib_ref, keepb_ref, accb_ref)):
            for q in range(seqs):
                for p in range(n_pairs):
                    lanes = slice(p * LANES, (p + 1) * LANES)
                    state = carry[backward * per_dir + q * n_pairs + p]
                    units.append(direction(q * seq_len + row, lanes, g_ref, tri_ref, keep_ref, state, acc_ref,
                                           backward))
        return units

    norm_pair = jnp.concatenate([norm_ref[...]] * 2, axis=-1)

    def finish_outputs(rows):
        for p in range(n_pairs):
            lanes = slice(p * LANES, (p + 1) * LANES)
            o = _head_rms_norm(accf_ref[rows, lanes] + accb_ref[rows, lanes], norm_pair)
            og_ref[rows, lanes] = (o * sgg_ref[rows, lanes].astype(F32)).astype(BF16)
            yield

    if n_groups == 1:
        gla_units = body(0, init)
        final = [None] * len(gla_units)
        for q in range(seqs + 1):
            rows = slice(q * seq_len, (q + 1) * seq_len)
            wave, ids = [], []
            if q < seqs:
                ids = [d * per_dir + q * n_pairs + p for d in range(2) for p in range(n_pairs)]
                wave = [gla_units[u] for u in ids]
                if with_attention:
                    wave += _context_attention_units(sink_ref, aq_ref, ak_ref, av_ref, rows)
            if q > 0:
                wave.append(finish_outputs(slice((q - 1) * seq_len, q * seq_len)))
            results = _run_staged(wave)
            for u, r in zip(ids, results):
                final[u] = r
            if q < seqs and with_attention:
                heads = [o for outs in results[len(ids):len(ids) + SWA_KV_HEADS] for o in outs]
                osw_ref[rows, :] = jnp.concatenate(heads, axis=-1).astype(BF16)
    else:
        assert with_attention != "context" and seqs == 1
        if with_attention:
            ctx_kv = [(kc_ref[0, 0, g, :, :].astype(BF16), vc_ref[0, 0, g, :, :].astype(BF16))
                      for g in range(SWA_KV_HEADS)]
            q_blocks = GROUP // QBLOCK

        def loop_body(t, carry):
            units = body(t, carry)
            n_gla = len(units)
            if with_attention:
                for j in range(q_blocks):
                    units += _latent_attention_units(sink_ref, aq_ref, ak_ref, av_ref, ctx_kv,
                                                     t * q_blocks + j, seq_len)
            results = _run_staged(units)
            if with_attention:
                for j in range(q_blocks):
                    heads = [o for outs in results[n_gla + j * SWA_KV_HEADS:n_gla + (j + 1) * SWA_KV_HEADS]
                             for o in outs]
                    rows = pl.ds(pl.multiple_of((t * q_blocks + j) * QBLOCK, QBLOCK), QBLOCK)
                    osw_ref[rows, :] = jnp.concatenate(heads, axis=-1).astype(BF16)
            return tuple(results[:n_gla])

        final = lax.fori_loop(0, n_groups, loop_body, init)
        _run_staged([finish_outputs(slice(0, seq_len))])
    if emit_state:
        for d, s_ref in enumerate((sf_ref, sb_ref)):
            for q in range(seqs):
                for p in range(n_pairs):
                    fin = final[d * per_dir + q * n_pairs + p]
                    s_ref[q, 0, 2 * p, :, :] = fin[:HEAD_DIM, :HEAD_DIM].T
                    s_ref[q, 0, 2 * p + 1, :, :] = fin[HEAD_DIM:, HEAD_DIM:].T


def _gla_constants():
    i = np.arange(GROUP)[:, None]
    j = np.arange(GROUP)[None, :]
    same = (i // CHUNK) == (j // CHUNK)
    tri_f = same & (j <= i)
    tri_b = same & (j >= i)
    r = np.arange(4 * CHUNK)[:, None]
    l = np.arange(LANES)[None, :]
    same = (r // (2 * CHUNK)) == (l // CHUNK)
    keep_f = same & ((l % CHUNK) <= (r % CHUNK))
    keep_b = same & ((l % CHUNK) >= (r % CHUNK))
    return tuple(jnp.asarray(m.astype(np.float32), dtype=BF16) for m in (tri_f, tri_b, keep_f, keep_b))


def _gla(gla_in, gates, norm_row, s0f, s0b, n_seq, seq_len, row_block0, emit_state, seqs=1,
         attention=None, cast_weights=(), late_modulation=None):
    has_state = s0f is not None
    steps = n_seq // seqs
    rows_per_step = seqs * seq_len
    row_blk = row_block0 // seqs
    col = lambda width, j: pl.BlockSpec((rows_per_step, width), lambda b: (row_blk + b, j))
    out_blk = pl.BlockSpec((rows_per_step, GLA_WIDTH), lambda b: (b, 0))
    st_spec = pl.BlockSpec((seqs, 1, GLA_HEADS, HEAD_DIM, HEAD_DIM), lambda b: (b, 0, 0, 0, 0))
    const = lambda shape: pl.BlockSpec(shape, lambda b: (0, 0))
    in_specs = ([col(GLA_WIDTH, 0), col(GLA_WIDTH, 1), col(GLA_WIDTH, 2), col(GLA_WIDTH, 0), col(GLA_WIDTH, 1),
                 col(GLA_WIDTH, 3)]
                + [const(norm_row.shape)] + [const((GROUP, GROUP))] * 2 + [const((4 * CHUNK, LANES))] * 2)
    args = [gla_in, gla_in, gla_in, gates, gates, gla_in, norm_row, *_gla_constants()]
    if has_state:
        in_specs += [st_spec] * 2
        args += [s0f, s0b]
    out_specs = [out_blk]
    out_shape = [jax.ShapeDtypeStruct((n_seq * seq_len, GLA_WIDTH), BF16)]
    if emit_state:
        out_specs += [st_spec] * 2
        out_shape += [jax.ShapeDtypeStruct((n_seq, 1, GLA_HEADS, HEAD_DIM, HEAD_DIM), F32)] * 2
    with_attention = None
    if attention is not None:
        sink, swa_in = attention[:2]
        with_attention = "context"
        in_specs += [pl.BlockSpec(memory_space=pltpu.SMEM), col(SWA_WIDTH, 0),
                     col(KV_WIDTH, SWA_WIDTH // KV_WIDTH), col(KV_WIDTH, SWA_WIDTH // KV_WIDTH + 1)]
        args += [sink, swa_in, swa_in, swa_in]
        if len(attention) > 2:
            with_attention = "latent"
            cache_k_t, cache_v_t = attention[2:]
            past = cache_k_t.shape[4]
            cache_spec = pl.BlockSpec((seqs, 1, SWA_KV_HEADS, HEAD_DIM, past), lambda b: (b, 0, 0, 0, 0))
            in_specs += [cache_spec] * 2
            args += [cache_k_t, cache_v_t]
        out_specs.append(pl.BlockSpec((rows_per_step, SWA_WIDTH), lambda b: (b, 0)))
        out_shape.append(jax.ShapeDtypeStruct((n_seq * seq_len, SWA_WIDTH), BF16))
    cast_specs, cast_shapes = _cast_specs(cast_weights, steps)
    in_specs += cast_specs
    args += list(cast_weights)
    out_specs += cast_specs
    out_shape += cast_shapes
    if late_modulation is not None:
        c_ctx, c, w_ada, b_ada, col0 = late_modulation
        width = (w_ada.shape[1] - col0) // steps
        blk0 = col0 // width
        in_specs += [const(c_ctx.shape), const(c.shape),
                     pl.BlockSpec((w_ada.shape[0], width), lambda b: (0, blk0 + b)),
                     pl.BlockSpec((1, width), lambda b: (0, blk0 + b))]
        args += [c_ctx, c, w_ada, b_ada]
        out_specs.append(pl.BlockSpec((8, width), lambda b: (0, b)))
        out_shape.append(jax.ShapeDtypeStruct((8, width * steps), F32))
    return pl.pallas_call(
        functools.partial(_gla_kernel, seq_len=seq_len, seqs=seqs, has_state=has_state, emit_state=emit_state,
                          with_attention=with_attention, n_cast=len(cast_weights),
                          with_modulation=late_modulation is not None),
        grid=(steps,), in_specs=in_specs, out_specs=out_specs, out_shape=out_shape,
        scratch_shapes=[pltpu.VMEM((rows_per_step, GLA_WIDTH), F32)] * 2,
        compiler_params=_cparams(1), name="gla_bidirectional",
    )(*args)


def _group_attention(q, heads, sink_ref, score_fn, value_fn):
    rows = q.shape[0]
    q_stack = jnp.concatenate([q[:, h * HEAD_DIM:(h + 1) * HEAD_DIM] for h in heads], axis=0)
    scores = score_fn(q_stack)
    yield
    slabs = [slice(j * rows, (j + 1) * rows) for j in range(len(heads))]
    ms = []
    sinks = [sink_ref[h] * LOG2E for h in heads]
    for slab, sink in zip(slabs, sinks):
        m = sink
        for s in scores:
            m = jnp.maximum(m, jnp.max(s[slab, :], axis=-1, keepdims=True))
        ms.append(m)
    yield
    e_parts = [[] for _ in scores]
    denoms = []
    for slab, sink, m in zip(slabs, sinks, ms):
        denom = jnp.exp2(sink - m)
        for part, s in zip(e_parts, scores):
            e = jnp.exp2(s[slab, :] - m)
            denom = denom + jnp.sum(e, axis=-1, keepdims=True)
            part.append(e.astype(BF16))
        denoms.append(denom)
    yield
    acc = value_fn([jnp.concatenate(part, axis=0) for part in e_parts])
    yield
    return [acc[slab, :] / denom for slab, denom in zip(slabs, denoms)]


def _run_staged(units):
    results = [None] * len(units)
    live = list(range(len(units)))
    while live:
        still = []
        for u in live:
            try:
                next(units[u])
                still.append(u)
            except StopIteration as stop:
                results[u] = stop.value
        live = still
    return results


def _context_attention_units(sink_ref, q_ref, k_ref, v_ref, rows):
    q = q_ref[rows, :]
    k = k_ref[rows, :]
    v = v_ref[rows, :]
    units = []
    for g in range(SWA_KV_HEADS):
        k_g = k[:, g * HEAD_DIM:(g + 1) * HEAD_DIM]
        v_g = v[:, g * HEAD_DIM:(g + 1) * HEAD_DIM]
        units.append(_group_attention(
            q, list(range(SWA_GROUP * g, SWA_GROUP * (g + 1))), sink_ref,
            lambda qs, k_g=k_g: [_dot_nt(qs, k_g)],
            lambda es, v_g=v_g: _dot(es[0], v_g)))
    return units


def _cast_specs(cast_weights, steps, step_of=lambda *idx: idx[0]):
    specs = [pl.BlockSpec((w.shape[0] // steps, w.shape[1]), lambda *idx: (step_of(*idx), 0))
             for w in cast_weights]
    shapes = [jax.ShapeDtypeStruct(w.shape, BF16) for w in cast_weights]
    return specs, shapes


def _latent_attention_units(sink_ref, q_ref, k_ref, v_ref, ctx_kv, i, seq_len):
    span = QBLOCK + 2 * WINDOW
    start = pl.multiple_of(jnp.clip(i * QBLOCK - WINDOW, 0, seq_len - span), QBLOCK)
    q = q_ref[pl.ds(pl.multiple_of(i * QBLOCK, QBLOCK), QBLOCK), :]
    k = k_ref[pl.ds(start, span), :]
    v = v_ref[pl.ds(start, span), :]
    qpos = i * QBLOCK + lax.broadcasted_iota(jnp.int32, (QBLOCK, span), 0)
    kpos = start + lax.broadcasted_iota(jnp.int32, (QBLOCK, span), 1)
    valid = jnp.abs(qpos - kpos) <= WINDOW
    valid = jnp.concatenate([valid] * SWA_GROUP, axis=0)
    units = []
    for g in range(SWA_KV_HEADS):
        k_g = k[:, g * HEAD_DIM:(g + 1) * HEAD_DIM]
        v_g = v[:, g * HEAD_DIM:(g + 1) * HEAD_DIM]
        kc_t, vc_t = ctx_kv[g]
        units.append(_group_attention(
            q, list(range(SWA_GROUP * g, SWA_GROUP * (g + 1))), sink_ref,
            lambda qs, k_g=k_g, kc_t=kc_t: [jnp.where(valid, _dot_nt(qs, k_g), NEG_INF), _dot(qs, kc_t)],
            lambda es, v_g=v_g, vc_t=vc_t: _dot(es[0], v_g) + _dot_nt(es[1], vc_t)))
    return units


def _out_kernel(xp_ref, xs_ref, ogp_ref, ogs_ref, oswp_ref, osws_ref, mod_ref, gain_ref,
                wo_ref, w1_ref, w2_ref, yp_ref, ys_ref, *, n_prompt_tiles, tiles_per_seq):
    i = pl.program_id(0)
    row = _mod_row(i, n_prompt_tiles, tiles_per_seq)

    def body(x_ref, og_ref, osw_ref, y_ref):
        mix = _dot(og_ref[...], wo_ref[:GLA_WIDTH, :]) + _dot(osw_ref[...], wo_ref[GLA_WIDTH:, :])
        x1 = x_ref[...] + _mod_chunk(mod_ref, row, 0) * mix
        h = _adaln(x1, gain_ref[...], _mod_chunk(mod_ref, row, 1), _mod_chunk(mod_ref, row, 2)).astype(BF16)
        acc = None
        for j in range(D_FF // FF_TILE):
            cols = slice(j * FF_TILE, (j + 1) * FF_TILE)
            a = jnp.maximum(_dot(h, w1_ref[:, cols]), 0.0)
            part = _dot((a * a).astype(BF16), w2_ref[cols, :])
            acc = part if acc is None else acc + part
        y_ref[...] = x1 + _mod_chunk(mod_ref, row, 3) * acc

    @pl.when(i < n_prompt_tiles)
    def _():
        body(xp_ref, ogp_ref, oswp_ref, yp_ref)

    @pl.when(i >= n_prompt_tiles)
    def _():
        body(xs_ref, ogs_ref, osws_ref, ys_ref)


def _mix_out(xp, xs, og_p, og_s, osw_p, osw_s, mod, gain, wo, w1, w2, seq_s):
    n_p, n_s = xp.shape[0], xs.shape[0]
    tp, ts = n_p // TM, n_s // TM
    tiles_per_seq = seq_s // TM
    const = lambda shape: pl.BlockSpec(shape, lambda i: (0,) * len(shape), pipeline_mode=pl.Buffered(1))
    row_p = lambda w: pl.BlockSpec((TM, w), lambda i: (jnp.minimum(i, tp - 1), 0))
    row_s = lambda w: pl.BlockSpec((TM, w), lambda i: (jnp.maximum(i - tp, 0), 0))
    return pl.pallas_call(
        functools.partial(_out_kernel, n_prompt_tiles=tp, tiles_per_seq=tiles_per_seq),
        grid=(tp + ts,),
        in_specs=[row_p(D_MODEL), row_s(D_MODEL), row_p(GLA_WIDTH), row_s(GLA_WIDTH),
                  row_p(SWA_WIDTH), row_s(SWA_WIDTH),
                  const(mod.shape),
                  const((1, D_MODEL)), const((GLA_WIDTH + SWA_WIDTH, D_MODEL)),
                  const((D_MODEL, D_FF)), const((D_FF, D_MODEL))],
        out_specs=[row_p(D_MODEL), row_s(D_MODEL)],
        out_shape=[jax.ShapeDtypeStruct((n_p, D_MODEL), F32), jax.ShapeDtypeStruct((n_s, D_MODEL), F32)],
        compiler_params=_cparams(1), name="out_proj_mlp",
    )(xp, xs, og_p, og_s, osw_p, osw_s, mod, gain, wo, w1, w2)


def _rope_tables(seq_len):
    axis_dim = HEAD_DIM // 2
    half = axis_dim // 2
    t = np.arange(seq_len)
    rowp = (t // GRID_W).astype(np.float64)
    colp = (t % GRID_W).astype(np.float64)
    freqs = ROPE_THETA ** (-np.arange(half, dtype=np.float64) / half)
    lane = np.arange(LANES)
    d = lane % HEAD_DIM
    posv = np.where((d // axis_dim)[None, :] == 0, rowp[:, None], colp[:, None])
    ang = posv * freqs[d % half][None, :]
    second = ((d % axis_dim) >= half)[None, :]
    cos, sin = np.cos(ang), np.sin(ang)
    zero = np.zeros_like(sin)
    tabs = (cos, np.where(second, sin, zero), np.where(second, zero, -sin))
    return tuple(jnp.asarray(tab.astype(np.float32)) for tab in tabs)


def kernel(x_prompt, x_sample, c, cache_k, cache_v, state_gla_fwd, state_gla_bwd, c_ctx, w_ada, b_ada, norm_mix, norm_ff, w_in, w_gk_fwd, b_gk_fwd, w_gk_bwd, b_gk_bwd, gla_norm, q_norm, k_norm, sink, w_o, w_ff1, w_ff2):
    depth = w_in.shape[0]
    assert depth == 1, "single trunk layer"
    bp, tp, _ = x_prompt.shape
    bs, ts, _ = x_sample.shape
    n_p = bp * tp
    l = 0

    n_mod_in = 2 * D_MODEL
    b_ada_row = b_ada[l][None, :]
    mod_in = _modulation(c_ctx[None, :], c, w_ada[l], b_ada_row, n_mod_in)

    norm2 = gla_norm[l][None, :]
    gain_mix = norm_mix[l][None, :]
    gain_ff = norm_ff[l][None, :]
    sink_l = sink[l]

    w_in_t = jnp.transpose(w_in[l])
    cache_k_t = jnp.swapaxes(cache_k, 3, 4)
    cache_v_t = jnp.swapaxes(cache_v, 3, 4)

    xp = x_prompt.reshape(n_p, D_MODEL)
    xs = x_sample.reshape(bs * ts, D_MODEL)
    gla_in, gates, swa_in, new_k_t, new_v_t = _project(
        xp, xs, mod_in, gain_mix, w_in_t, w_gk_fwd[l], w_gk_bwd[l], b_gk_fwd[l][None, :], b_gk_bwd[l][None, :],
        q_norm[l][None, :], k_norm[l][None, :], _rope_tables(ts), tp, ts)
    new_k = jnp.swapaxes(new_k_t, 3, 4)
    new_v = jnp.swapaxes(new_v_t, 3, 4)

    og_p, s_f, s_b, osw_p, wo, w1, w2, mod_out = _gla(
        gla_in, gates, norm2, None, None, bp, tp, 0, True, seqs=2,
        attention=(sink_l, swa_in), cast_weights=(w_o[l], w_ff1[l], w_ff2[l]),
        late_modulation=(c_ctx[None, :], c, w_ada[l], b_ada_row, n_mod_in))
    og_s, osw_s = _gla(gla_in, gates, norm2, state_gla_fwd, state_gla_bwd, bs, ts, n_p // ts, False,
                       attention=(sink_l, swa_in, cache_k_t, cache_v_t))

    yp, ys = _mix_out(xp, xs, og_p, og_s, osw_p, osw_s, mod_out, gain_ff, wo, w1, w2, ts)
    return (yp.reshape(bp, tp, D_MODEL), ys.reshape(bs, ts, D_MODEL), new_k, new_v, s_f, s_b)
```
